```python
import jax, jax.numpy as jnp
from jax import lax
import numpy as np

D_MODEL = 1024
BATCH = 4
SEQ = 8192
DEPTH = 1
DEC_BATCH = 128
DEC_SEQ = 4
PAST_LEN = 16384
PAGE_SIZE = 128

HEAD_DIM = 64
A_HEADS = 8
A_WIDTH = A_HEADS * HEAD_DIM
B_HEADS = 8
B_KV_HEADS = 2
B_GROUP = B_HEADS // B_KV_HEADS
B_WIDTH = B_HEADS * HEAD_DIM
B_KV_WIDTH = B_KV_HEADS * HEAD_DIM
MIX_WIDTH = A_WIDTH + B_WIDTH
DECAY_LORA = 64
ICLR_LORA = 64
GATE_LORA = 128
SHIFT_W = 3 * A_WIDTH + DECAY_LORA + ICLR_LORA + GATE_LORA
IN_W = SHIFT_W + B_WIDTH + 2 * B_KV_WIDTH
WINDOW = 128
ATTN_BLOCK = 128
N_GROUPS = 4
EXPERTS_PER_GROUP = 4
N_EXPERTS = N_GROUPS * EXPERTS_PER_GROUP
TOP_K_INNER = 2
EXPERT_FF = 512
MOE_BLOCK = 128
RMS_EPS = 1e-6
LNX_EPS = 64e-5
DECAY_OFFSET = 0.5

kernel_name = "hymba_rwkv7_swa_sink_hiermoe_step"


def rmsnorm(x, g):
    x32 = x.astype(jnp.float32)
    y = x32 * lax.rsqrt(jnp.mean(x32 * x32, axis=-1, keepdims=True) + RMS_EPS)
    return (y * g.astype(jnp.float32)).astype(x.dtype)


def alibi_slopes():
    h = jnp.arange(1, B_HEADS + 1, dtype=jnp.float32)
    return jnp.exp2(-8.0 * h / B_HEADS).reshape(B_KV_HEADS, B_GROUP)


def sink_probs(scores, mask, sink):
    scores = jnp.where(mask, scores, -jnp.inf)
    sk = sink.astype(jnp.float32).reshape(B_KV_HEADS, B_GROUP, 1, 1)
    m = jnp.maximum(jnp.max(scores, axis=-1, keepdims=True), sk)
    p = jnp.exp(scores - m)
    denom = jnp.sum(p, axis=-1, keepdims=True) + jnp.exp(sk - m)
    return p / denom


def swa_prompt(q, k, v, sink):
    Bn, T = q.shape[0], q.shape[1]
    nb = T // ATTN_BLOCK
    scale = HEAD_DIM ** -0.5
    qb = q.reshape(Bn, nb, ATTN_BLOCK, B_KV_HEADS, B_GROUP, HEAD_DIM)
    pad = jnp.zeros((Bn, ATTN_BLOCK, B_KV_HEADS, HEAD_DIM), k.dtype)
    kb = jnp.concatenate([pad, k], axis=1).reshape(Bn, nb + 1, ATTN_BLOCK, B_KV_HEADS, HEAD_DIM)
    vb = jnp.concatenate([pad.astype(v.dtype), v], axis=1).reshape(Bn, nb + 1, ATTN_BLOCK, B_KV_HEADS, HEAD_DIM)
    keys = jnp.concatenate([kb[:, :-1], kb[:, 1:]], axis=2)
    vals = jnp.concatenate([vb[:, :-1], vb[:, 1:]], axis=2)
    s = jnp.einsum('bnqkgd,bnskd->bnkgqs', qb, keys, preferred_element_type=jnp.float32) * scale
    dist = jnp.arange(ATTN_BLOCK)[:, None] + ATTN_BLOCK - jnp.arange(2 * ATTN_BLOCK)[None, :]
    kpos = jnp.arange(nb)[:, None] * ATTN_BLOCK - ATTN_BLOCK + jnp.arange(2 * ATTN_BLOCK)[None, :]
    mask = ((dist >= 0) & (dist < WINDOW))[None, None, None, None] & (kpos >= 0)[None, :, None, None, None, :]
    s = s - alibi_slopes()[:, :, None, None] * dist.astype(jnp.float32)
    p = sink_probs(s, mask, sink)
    o = jnp.einsum('bnkgqs,bnskd->bnqkgd', p.astype(vals.dtype), vals).reshape(Bn, T, B_WIDTH)
    return o, k[:, -WINDOW:], v[:, -WINDOW:]


def swa_sample(q, k, v, sink, win_k, win_v):
    Bn, T = q.shape[0], q.shape[1]
    Wb = win_k.shape[1]
    scale = HEAD_DIM ** -0.5
    keys = jnp.concatenate([win_k.astype(k.dtype), k], axis=1)
    vals = jnp.concatenate([win_v.astype(v.dtype), v], axis=1)
    qg = q.reshape(Bn, T, B_KV_HEADS, B_GROUP, HEAD_DIM)
    s = jnp.einsum('btkgd,bskd->bkgts', qg, keys, preferred_element_type=jnp.float32) * scale
    dist = jnp.arange(T)[:, None] + Wb - jnp.arange(Wb + T)[None, :]
    mask = (dist >= 0) & (dist < WINDOW)
    s = s - alibi_slopes()[:, :, None, None] * dist.astype(jnp.float32)
    p = sink_probs(s, mask, sink)
    o = jnp.einsum('bkgts,bskd->btkgd', p.astype(vals.dtype), vals).reshape(Bn, T, B_WIDTH)
    return o, keys[:, -Wb:], vals[:, -Wb:]


def rwkv_scan(S0, r, w, k, v, kk, a):
    def step(S, inp):
        r_t, w_t, k_t, v_t, kk_t, a_t = inp
        sa = jnp.einsum('bhvk,bhk->bhv', S, kk_t)
        S = S * w_t[:, :, None, :] - sa[..., None] * (kk_t * a_t)[:, :, None, :] + v_t[..., None] * k_t[:, :, None, :]
        y = jnp.einsum('bhvk,bhk->bhv', S, r_t)
        return S, y
    xs = (jnp.swapaxes(r, 0, 1), jnp.swapaxes(w, 0, 1), jnp.swapaxes(k, 0, 1),
          jnp.swapaxes(v, 0, 1), jnp.swapaxes(kk, 0, 1), jnp.swapaxes(a, 0, 1))
    S, ys = lax.scan(step, S0, xs)
    return S, jnp.swapaxes(ys, 0, 1)


def rwkv_mixer(pa, shift_prev, S0, prm):
    Bn, T = pa.shape[0], pa.shape[1]
    f32 = jnp.float32
    pa_prev = jnp.concatenate([shift_prev[:, None, :].astype(pa.dtype), pa[:, :-1]], axis=1)
    xm = (pa + prm['mu_shift'] * (pa_prev - pa)).astype(f32)
    c = np.cumsum([A_WIDTH, A_WIDTH, A_WIDTH, DECAY_LORA, ICLR_LORA])
    r, k, v, wd, ad, gd = jnp.split(xm, [int(i) for i in c], axis=-1)
    w_log = -jax.nn.softplus(-(prm['w_decay0'] + jnp.tanh(wd) @ prm['w_decay_up'])) - DECAY_OFFSET
    decay = jnp.exp(-jnp.exp(w_log))
    a = jax.nn.sigmoid(prm['w_iclr0'] + ad @ prm['w_iclr_up'])
    g = jax.nn.sigmoid(gd) @ prm['w_gate_up']
    kk = (k * prm['k_k']).reshape(Bn, T, A_HEADS, HEAD_DIM)
    kk = kk * lax.rsqrt(jnp.maximum(jnp.sum(kk * kk, axis=-1, keepdims=True), 1e-24))
    k = k * (1.0 + (a - 1.0) * prm['k_a'])
    hs = (Bn, T, A_HEADS, HEAD_DIM)
    rh, kh, vh = r.reshape(hs), k.reshape(hs), v.reshape(hs)
    S, y = rwkv_scan(S0.astype(f32), rh, decay.reshape(hs), kh, vh, kk, a.reshape(hs))
    mu = jnp.mean(y, axis=-1, keepdims=True)
    var = jnp.mean(jnp.square(y - mu), axis=-1, keepdims=True)
    yn = ((y - mu) * lax.rsqrt(var + LNX_EPS)).reshape(Bn, T, A_WIDTH) * prm['lnx_g'] + prm['lnx_b']
    bonus = (jnp.sum(rh * kh * prm['r_k'], axis=-1, keepdims=True) * vh).reshape(Bn, T, A_WIDTH)
    out = (yn + bonus) * g
    return out, S, pa[:, -1]


def hier_moe(h, prm):
    T = h.shape[0]
    h32 = h.astype(jnp.float32)
    pg = jax.nn.softmax(h32 @ prm['w_route_group'] + prm['b_route_group'], axis=-1)
    g_idx = jnp.argmax(pg, axis=-1)
    g_w = jnp.take_along_axis(pg, g_idx[:, None], axis=-1)
    le = (h32 @ prm['w_route_expert'] + prm['b_route_expert']).reshape(T, N_GROUPS, EXPERTS_PER_GROUP)
    le_g = jnp.take_along_axis(le, g_idx[:, None, None], axis=1)[:, 0]
    top_p, top_i = lax.top_k(jax.nn.softmax(le_g, axis=-1), TOP_K_INNER)
    wts = g_w * top_p / jnp.sum(top_p, axis=-1, keepdims=True)
    e_idx = g_idx[:, None] * EXPERTS_PER_GROUP + top_i
    A = T * TOP_K_INNER
    flat_e = e_idx.reshape(A).astype(jnp.int32)
    flat_w = wts.reshape(A)
    flat_tok = jnp.repeat(jnp.arange(T, dtype=jnp.int32), TOP_K_INNER)
    order = jnp.argsort(flat_e)
    se, stok, sw = flat_e[order], flat_tok[order], flat_w[order]
    counts = jnp.bincount(flat_e, length=N_EXPERTS)
    starts = jnp.cumsum(counts) - counts
    padded = (counts + MOE_BLOCK - 1) // MOE_BLOCK * MOE_BLOCK
    pend = jnp.cumsum(padded)
    pstarts = pend - padded
    dest = pstarts[se] + (jnp.arange(A, dtype=jnp.int32) - starts[se])
    n_blocks = -(-A // MOE_BLOCK) + N_EXPERTS
    P = n_blocks * MOE_BLOCK
    xbuf = jnp.zeros((P, h.shape[1]), h.dtype).at[dest].set(h[stok])
    block_e = jnp.clip(jnp.searchsorted(pend, jnp.arange(n_blocks) * MOE_BLOCK, side='right'), 0, N_EXPERTS - 1)
    w_gate, w_up, w_down = prm['w_exp_gate'], prm['w_exp_up'], prm['w_exp_down']

    def expert_block(args):
        xb, e = args
        return (jax.nn.silu(xb @ w_gate[e]) * (xb @ w_up[e])) @ w_down[e]

    ybuf = lax.map(expert_block, (xbuf.reshape(n_blocks, MOE_BLOCK, h.shape[1]), block_e)).reshape(P, -1)
    y = jnp.zeros((T, ybuf.shape[1]), ybuf.dtype).at[stok].add(ybuf[dest] * sw[:, None].astype(ybuf.dtype))
    return y


def trunk_layer(x, shift_prev, S0, win_k, win_v, prm):
    Bn, T, D = x.shape
    h = rmsnorm(x, prm['norm_mix_g'])
    p = h @ prm['w_in']
    pa = p[..., :SHIFT_W]
    pb = p[..., SHIFT_W:]
    o_a, S_new, shift_new = rwkv_mixer(pa, shift_prev, S0, prm)
    q = pb[..., :B_WIDTH].reshape(Bn, T, B_HEADS, HEAD_DIM)
    k = pb[..., B_WIDTH:B_WIDTH + B_KV_WIDTH].reshape(Bn, T, B_KV_HEADS, HEAD_DIM)
    v = pb[..., B_WIDTH + B_KV_WIDTH:].reshape(Bn, T, B_KV_HEADS, HEAD_DIM)
    if win_k is None:
        o_b, kbuf, vbuf = swa_prompt(q, k, v, prm['attn_sink'])
    else:
        o_b, kbuf, vbuf = swa_sample(q, k, v, prm['attn_sink'], win_k, win_v)
    o_b = rmsnorm(o_b, prm['attn_norm_g'])
    mixed = jnp.concatenate([o_a.astype(x.dtype), o_b.astype(x.dtype)], axis=-1)
    x = x + (mixed @ prm['w_out']).astype(x.dtype)
    h2 = rmsnorm(x, prm['norm_ffn_g'])
    x = x + hier_moe(h2.reshape(Bn * T, D), prm).reshape(Bn, T, D).astype(x.dtype)
    return x, S_new, shift_new, kbuf, vbuf


def setup_inputs(seed: int = 0) -> dict:
    key = jax.random.key(seed)
    ks = jax.random.split(key, 32)
    f32 = jnp.float32
    L = DEPTH
    win_buf = min(WINDOW, PAST_LEN)

    def nrm(k, shape, scale):
        return jax.random.normal(k, shape, f32) * scale

    return {
        "x_prompt": nrm(ks[0], (BATCH, SEQ, D_MODEL), 1.0),
        "x_sample": nrm(ks[1], (DEC_BATCH, DEC_SEQ, D_MODEL), 1.0),
        "state_rwkv": nrm(ks[2], (L, DEC_BATCH, A_HEADS, HEAD_DIM, HEAD_DIM), 0.5),
        "state_shift": nrm(ks[3], (L, DEC_BATCH, SHIFT_W), 1.0),
        "cache_win_k": nrm(ks[4], (L, DEC_BATCH, win_buf, B_KV_HEADS, HEAD_DIM), 1.0),
        "cache_win_v": nrm(ks[5], (L, DEC_BATCH, win_buf, B_KV_HEADS, HEAD_DIM), 1.0),
        "norm_mix_g": 1.0 + nrm(ks[6], (L, D_MODEL), 0.02),
        "w_in": nrm(ks[7], (L, D_MODEL, IN_W), D_MODEL ** -0.5),
        "mu_shift": jax.random.uniform(ks[8], (L, SHIFT_W), f32),
        "w_decay0": jax.random.uniform(ks[9], (L, A_WIDTH), f32, -6.0, -1.0),
        "w_decay_up": nrm(ks[10], (L, DECAY_LORA, A_WIDTH), DECAY_LORA ** -0.5),
        "w_iclr0": nrm(ks[11], (L, A_WIDTH), 0.1),
        "w_iclr_up": nrm(ks[12], (L, ICLR_LORA, A_WIDTH), ICLR_LORA ** -0.5),
        "w_gate_up": nrm(ks[13], (L, GATE_LORA, A_WIDTH), GATE_LORA ** -0.5),
        "k_k": 0.85 + nrm(ks[14], (L, A_WIDTH), 0.02),
        "k_a": 1.0 + nrm(ks[15], (L, A_WIDTH), 0.02),
        "r_k": nrm(ks[16], (L, A_HEADS, HEAD_DIM), 0.1),
        "lnx_g": 1.0 + nrm(ks[17], (L, A_WIDTH), 0.02),
        "lnx_b": nrm(ks[18], (L, A_WIDTH), 0.02),
        "attn_sink": nrm(ks[19], (L, B_HEADS), 0.5),
        "attn_norm_g": 1.0 + nrm(ks[20], (L, B_WIDTH), 0.02),
        "w_out": nrm(ks[21], (L, MIX_WIDTH, D_MODEL), MIX_WIDTH ** -0.5),
        "norm_ffn_g": 1.0 + nrm(ks[22], (L, D_MODEL), 0.02),
        "w_route_group": nrm(ks[23], (L, D_MODEL, N_GROUPS), D_MODEL ** -0.5),
        "b_route_group": nrm(ks[24], (L, N_GROUPS), 0.01),
        "w_route_expert": nrm(ks[25], (L, D_MODEL, N_EXPERTS), D_MODEL ** -0.5),
        "b_route_expert": nrm(ks[26], (L, N_EXPERTS), 0.01),
        "w_exp_gate": nrm(ks[27], (L, N_EXPERTS, D_MODEL, EXPERT_FF), D_MODEL ** -0.5),
        "w_exp_up": nrm(ks[28], (L, N_EXPERTS, D_MODEL, EXPERT_FF), D_MODEL ** -0.5),
        "w_exp_down": nrm(ks[29], (L, N_EXPERTS, EXPERT_FF, D_MODEL), EXPERT_FF ** -0.5),
        "norm_final_g": 1.0 + nrm(ks[30], (D_MODEL,), 0.02),
    }


def reference(x_prompt, x_sample, state_rwkv, state_shift, cache_win_k, cache_win_v,
              norm_mix_g, w_in, mu_shift, w_decay0, w_decay_up, w_iclr0, w_iclr_up, w_gate_up,
              k_k, k_a, r_k, lnx_g, lnx_b, attn_sink, attn_norm_g, w_out, norm_ffn_g,
              w_route_group, b_route_group, w_route_expert, b_route_expert,
              w_exp_gate, w_exp_up, w_exp_down, norm_final_g):
    xp, xs = x_prompt, x_sample
    Bp = xp.shape[0]
    rwkv_p, shift_p, kp_list, vp_list = [], [], [], []
    rwkv_s, shift_s, ks_list, vs_list = [], [], [], []
    for l in range(DEPTH):
        prm = dict(norm_mix_g=norm_mix_g[l], w_in=w_in[l], mu_shift=mu_shift[l], w_decay0=w_decay0[l],
                   w_decay_up=w_decay_up[l], w_iclr0=w_iclr0[l], w_iclr_up=w_iclr_up[l], w_gate_up=w_gate_up[l],
                   k_k=k_k[l], k_a=k_a[l], r_k=r_k[l], lnx_g=lnx_g[l], lnx_b=lnx_b[l], attn_sink=attn_sink[l],
                   attn_norm_g=attn_norm_g[l], w_out=w_out[l], norm_ffn_g=norm_ffn_g[l],
                   w_route_group=w_route_group[l], b_route_group=b_route_group[l],
                   w_route_expert=w_route_expert[l], b_route_expert=b_route_expert[l],
                   w_exp_gate=w_exp_gate[l], w_exp_up=w_exp_up[l], w_exp_down=w_exp_down[l])
        shift0 = jnp.zeros((Bp, SHIFT_W), xp.dtype)
        S0 = jnp.zeros((Bp, A_HEADS, HEAD_DIM, HEAD_DIM), jnp.float32)
        xp, S_p, sh_p, k_p, v_p = trunk_layer(xp, shift0, S0, None, None, prm)
        xs, S_s, sh_s, k_s, v_s = trunk_layer(xs, state_shift[l], state_rwkv[l], cache_win_k[l], cache_win_v[l], prm)
        rwkv_p.append(S_p); shift_p.append(sh_p); kp_list.append(k_p); vp_list.append(v_p)
        rwkv_s.append(S_s); shift_s.append(sh_s); ks_list.append(k_s); vs_list.append(v_s)
    y_prompt = rmsnorm(xp, norm_final_g)
    y_sample = rmsnorm(xs, norm_final_g)
    state_rwkv_prompt = jnp.stack(rwkv_p, axis=0)
    state_shift_prompt = jnp.stack(shift_p, axis=0)
    cache_win_k_prompt = jnp.stack(kp_list, axis=0)
    cache_win_v_prompt = jnp.stack(vp_list, axis=0)
    state_rwkv_sample = jnp.stack(rwkv_s, axis=0)
    state_shift_sample = jnp.stack(shift_s, axis=0)
    cache_win_k_sample = jnp.stack(ks_list, axis=0)
    cache_win_v_sample = jnp.stack(vs_list, axis=0)
    return (y_prompt, y_sample, state_rwkv_prompt, state_shift_prompt, cache_win_k_prompt, cache_win_v_prompt,
            state_rwkv_sample, state_shift_sample, cache_win_k_sample, cache_win_v_sample)
```

```python
import functools

import jax
import jax.numpy as jnp
from jax import lax
from jax.experimental import pallas as pl
from jax.experimental.pallas import tpu as pltpu

F32 = jnp.float32
BF16 = jnp.bfloat16

HEAD_DIM = 64
A_HEADS = 8
A_WIDTH = A_HEADS * HEAD_DIM
B_HEADS = 8
B_KV_HEADS = 2
B_GROUP = B_HEADS // B_KV_HEADS
B_WIDTH = B_HEADS * HEAD_DIM
B_KV_WIDTH = B_KV_HEADS * HEAD_DIM
DECAY_LORA = 64
ICLR_LORA = 64
GATE_LORA = 128
LORA_W = DECAY_LORA + ICLR_LORA
SHIFT_W = 3 * A_WIDTH + LORA_W + GATE_LORA
IN_W = SHIFT_W + B_WIDTH + 2 * B_KV_WIDTH
WINDOW = 128
N_GROUPS = 4
EXPERTS_PER_GROUP = 4
N_EXPERTS = N_GROUPS * EXPERTS_PER_GROUP
TOP_K_INNER = 2
EXPERT_FF = 512
RMS_EPS = 1e-6
LNX_EPS = 64e-5
DECAY_OFFSET = 0.5

LANE = 128
PAIR = LANE // HEAD_DIM
N_PAIRS = A_HEADS // PAIR
ROUTE_W = LANE
MASK_NEG = -1e30
VMEM_LIMIT = 48 * 1024 * 1024

HEAD_SHIFT = HEAD_DIM.bit_length() - 1
HI = lax.Precision.HIGHEST


def _row_tile(n, cap):
    t = cap
    while n % t:
        t //= 2
    return t


def _dot(a, b, precision=None):
    return jnp.dot(a, b, preferred_element_type=F32, precision=precision)


def _dot_nt(a, b, precision=None):
    return lax.dot_general(a, b, (((1,), (1,)), ((), ())), preferred_element_type=F32, precision=precision)


def _dot_tn(a, b, precision=None):
    return lax.dot_general(a, b, (((0,), (0,)), ((), ())), preferred_element_type=F32, precision=precision)


def _sigmoid(x):
    return 1.0 / (1.0 + jnp.exp(-x))


def _inproj_body(x_ref, g_ref, w_ref, pa_ref, q_ref, k_ref, v_ref):
    x = x_ref[...]
    h = x * lax.rsqrt(jnp.mean(x * x, axis=-1, keepdims=True) + RMS_EPS) * g_ref[...]
    p = _dot(h.astype(BF16), w_ref[...])
    pa_ref[...] = p[:, :SHIFT_W]
    q_ref[...] = p[:, SHIFT_W:SHIFT_W + B_WIDTH]
    k_ref[...] = p[:, SHIFT_W + B_WIDTH:SHIFT_W + B_WIDTH + B_KV_WIDTH]
    v_ref[...] = p[:, SHIFT_W + B_WIDTH + B_KV_WIDTH:]


def _inproj_call(x, g, w_bf16):
    n, d = x.shape
    tm = _row_tile(n, 512)
    row = lambda i: (i, 0)
    fixed = lambda i: (0, 0)
    return pl.pallas_call(
        _inproj_body,
        grid=(n // tm,),
        in_specs=[pl.BlockSpec((tm, d), row), pl.BlockSpec((1, d), fixed), pl.BlockSpec((d, IN_W), fixed)],
        out_specs=[pl.BlockSpec((tm, SHIFT_W), row), pl.BlockSpec((tm, B_WIDTH), row),
                   pl.BlockSpec((tm, B_KV_WIDTH), row), pl.BlockSpec((tm, B_KV_WIDTH), row)],
        out_shape=[jax.ShapeDtypeStruct((n, SHIFT_W), F32), jax.ShapeDtypeStruct((n, B_WIDTH), F32),
                   jax.ShapeDtypeStruct((n, B_KV_WIDTH), F32), jax.ShapeDtypeStruct((n, B_KV_WIDTH), F32)],
        compiler_params=pltpu.CompilerParams(dimension_semantics=("arbitrary",), vmem_limit_bytes=VMEM_LIMIT),
        name="inproj",
    )(x, g, w_bf16)


def _rwkv_body(pa_ref, shift0_ref, s0_ref, mu_ref, w0_ref, a0_ref, wlora_ref, wgate_ref, kk_ref, ka_ref,
               rk_ref, lng_ref, lnb_ref, oa_ref, sout_ref, s_scr, prev_scr, *, chunk, t_real):
    C = chunk
    c = pl.program_id(1)

    @pl.when(c == 0)
    def _():
        s_scr[...] = s0_ref[0]
        prev_scr[...] = shift0_ref[0]

    pa = pa_ref[...]
    row = lax.broadcasted_iota(jnp.int32, (C, 1), 0)
    pa_prev = jnp.where(row == 0, prev_scr[...], pltpu.roll(pa, 1, axis=0))
    prev_scr[...] = pa[t_real - 1:t_real]
    xm = pa + mu_ref[...] * (pa_prev - pa)

    r = xm[:, :A_WIDTH]
    k = xm[:, A_WIDTH:2 * A_WIDTH]
    v = xm[:, 2 * A_WIDTH:3 * A_WIDTH]
    lora_in = xm[:, 3 * A_WIDTH:3 * A_WIDTH + LORA_W]
    gd = xm[:, 3 * A_WIDTH + LORA_W:]

    lane = lax.broadcasted_iota(jnp.int32, (1, LANE), 1)
    lo_half = lane < HEAD_DIM
    z = jnp.where(lo_half, jnp.tanh(lora_in), lora_in)
    lw = _dot(z.astype(BF16), wlora_ref[...])
    dec_pre = w0_ref[...] + lw[:, :A_WIDTH]
    a = _sigmoid(a0_ref[...] + lw[:, A_WIDTH:])
    neg = -dec_pre
    softplus = jnp.maximum(neg, 0.0) + jnp.log(1.0 + jnp.exp(-jnp.abs(neg)))
    logdec = -jnp.exp(-softplus - DECAY_OFFSET)
    g = _dot(_sigmoid(gd).astype(BF16), wgate_ref[...])

    ri = lax.broadcasted_iota(jnp.int32, (LANE, LANE), 0)
    ci = lax.broadcasted_iota(jnp.int32, (LANE, LANE), 1)
    seg = jnp.where((ri >> HEAD_SHIFT) == (ci >> HEAD_SHIFT), 1.0, 0.0).astype(F32)

    def headsum(x):
        return jnp.concatenate([_dot(x[:, j * LANE:(j + 1) * LANE], seg, HI) for j in range(N_PAIRS)], axis=1)

    kk = k * kk_ref[...]
    kk = kk * lax.rsqrt(jnp.maximum(headsum(kk * kk), 1e-24))
    k = k * (1.0 + (a - 1.0) * ka_ref[...])
    bonus = headsum(r * k * rk_ref[...]) * v

    if t_real < C:
        valid = row < t_real
        logdec = jnp.where(valid, logdec, 0.0)
        kk = jnp.where(valid, kk, 0.0)
        k = jnp.where(valid, k, 0.0)
        v = jnp.where(valid, v, 0.0)

    tr = lax.broadcasted_iota(jnp.int32, (C, C), 0)
    tc = lax.broadcasted_iota(jnp.int32, (C, C), 1)
    cum = _dot(jnp.where(tc <= tr, 1.0, 0.0).astype(F32), logdec, HI)
    cum_end = cum[C - 1:C]
    w_incl = jnp.exp(cum)
    w_prev = jnp.exp(cum - logdec)
    w_inv = jnp.exp(-cum)
    w_end = jnp.exp(cum_end - cum)
    w_chunk = jnp.exp(cum_end)
    kka = kk * a
    terms = dict(A=-kk * w_prev, R=r * w_incl, B=kka * w_inv, K=k * w_inv, V=v, Be=kka * w_end, Ke=k * w_end)

    C2 = PAIR * C
    sr = lax.broadcasted_iota(jnp.int32, (C2, C2), 0)
    sc = lax.broadcasted_iota(jnp.int32, (C2, C2), 1)
    c_shift = C.bit_length() - 1
    same_head = (sr >> c_shift) == (sc >> c_shift)
    strict = same_head & ((sc & (C - 1)) < (sr & (C - 1)))
    incl = same_head & ((sc & (C - 1)) <= (sr & (C - 1)))
    eye = ri == ci
    n_levels = max(1, (C - 1).bit_length())

    ys = []
    for j in range(N_PAIRS):
        sl = slice(j * LANE, (j + 1) * LANE)

        def st(name):
            x = terms[name][:, sl]
            return jnp.concatenate([jnp.where(lo_half, x, 0.0), jnp.where(lo_half, 0.0, x)], axis=0)

        a_st, r_st, b_st, k_st, v_st, be_st, ke_st = (st(nm) for nm in ("A", "R", "B", "K", "V", "Be", "Ke"))
        aab = jnp.where(strict, _dot_nt(a_st, b_st, HI), 0.0)
        aak = jnp.where(strict, _dot_nt(a_st, k_st, HI), 0.0)
        arb = jnp.where(incl, _dot_nt(r_st, b_st, HI), 0.0)
        ark = jnp.where(incl, _dot_nt(r_st, k_st, HI), 0.0)

        x = jnp.concatenate([a_st, _dot(aak, v_st, HI)], axis=1)
        nn = aab
        for lvl in range(n_levels):
            x = x + _dot(nn, x, HI)
            if lvl + 1 < n_levels:
                nn = _dot(nn, nn, HI)
        zed = _dot(arb, x, HI)
        rhat = r_st + zed[:, :LANE]
        y0 = _dot(ark, v_st, HI) + zed[:, LANE:]
        lhs = jnp.concatenate([be_st, ke_st], axis=0)
        rhs = jnp.concatenate([x, jnp.concatenate([jnp.zeros_like(v_st), v_st], axis=1)], axis=0)
        pq = _dot_tn(lhs, rhs, HI)
        p_mat = pq[:, :LANE] + jnp.where(eye, w_chunk[:, sl], 0.0)
        q_mat = pq[:, LANE:]

        s_prev = s_scr[j]
        y_st = _dot(rhat, s_prev, HI) + y0
        s_scr[j] = _dot(p_mat, s_prev, HI) + q_mat
        ys.append(y_st[:C] + y_st[C:])

    y = jnp.concatenate(ys, axis=1)
    inv_n = 1.0 / HEAD_DIM
    mean = headsum(y) * inv_n
    d = y - mean
    var = headsum(d * d) * inv_n
    yn = d * lax.rsqrt(var + LNX_EPS) * lng_ref[...] + lnb_ref[...]
    oa_ref[...] = (yn + bonus) * g
    sout_ref[0] = s_scr[...]


def _rwkv_call(pa2d, shift0, s0_bd, prm, *, n_seq, n_chunks, chunk, t_real):
    row = lambda b, c: (b * n_chunks + c, 0)
    fixed = lambda b, c: (0, 0)
    vec = lambda w: pl.BlockSpec((1, w), fixed)
    body = functools.partial(_rwkv_body, chunk=chunk, t_real=t_real)
    n_rows = n_seq * n_chunks * chunk
    return pl.pallas_call(
        body,
        grid=(n_seq, n_chunks),
        in_specs=[pl.BlockSpec((chunk, SHIFT_W), row),
                  pl.BlockSpec((1, 1, SHIFT_W), lambda b, c: (b, 0, 0)),
                  pl.BlockSpec((1, N_PAIRS, LANE, LANE), lambda b, c: (b, 0, 0, 0)),
                  vec(SHIFT_W), vec(A_WIDTH), vec(A_WIDTH),
                  pl.BlockSpec((LORA_W, 2 * A_WIDTH), fixed), pl.BlockSpec((GATE_LORA, A_WIDTH), fixed),
                  vec(A_WIDTH), vec(A_WIDTH), vec(A_WIDTH), vec(A_WIDTH), vec(A_WIDTH)],
        out_specs=[pl.BlockSpec((chunk, A_WIDTH), row),
                   pl.BlockSpec((1, N_PAIRS, LANE, LANE), lambda b, c: (b, 0, 0, 0))],
        out_shape=[jax.ShapeDtypeStruct((n_rows, A_WIDTH), F32),
                   jax.ShapeDtypeStruct((n_seq, N_PAIRS, LANE, LANE), F32)],
        scratch_shapes=[pltpu.VMEM((N_PAIRS, LANE, LANE), F32), pltpu.VMEM((1, SHIFT_W), F32)],
        compiler_params=pltpu.CompilerParams(dimension_semantics=("arbitrary", "arbitrary"),
                                             vmem_limit_bytes=VMEM_LIMIT),
        name="rwkv",
    )(pa2d, shift0, s0_bd, prm["mu"], prm["w0"], prm["a0"], prm["wlora"], prm["wgate"], prm["kk"], prm["ka"],
      prm["rk"], prm["lng"], prm["lnb"])


def _state_to_pairs(state):
    b = state.shape[0]
    st = jnp.swapaxes(state, -1, -2).reshape(b, N_PAIRS, PAIR, HEAD_DIM, HEAD_DIM)
    eye = jnp.eye(PAIR, dtype=state.dtype)
    bd = st[:, :, :, :, None, :] * eye[None, None, :, None, :, None]
    return bd.reshape(b, N_PAIRS, LANE, LANE)


def _pairs_to_state(bd):
    b = bd.shape[0]
    x = bd.reshape(b, N_PAIRS, PAIR, HEAD_DIM, PAIR, HEAD_DIM)
    st = jnp.stack([x[:, :, i, :, i, :] for i in range(PAIR)], axis=2)
    return jnp.swapaxes(st.reshape(b, A_HEADS, HEAD_DIM, HEAD_DIM), -1, -2)


def _swa_body(sink_ref, q_ref, kp_ref, vp_ref, kc_ref, vc_ref, gn_ref, o_ref, *, tq, first_has_prev):
    nblk = pl.program_id(1)
    q = q_ref[...]
    kp = kp_ref[0].astype(BF16)
    vp = vp_ref[0]
    kc = kc_ref[...].astype(BF16)
    vc = vc_ref[...]
    rows = B_GROUP * tq
    scale = HEAD_DIM ** -0.5
    lane = lax.broadcasted_iota(jnp.int32, (1, LANE), 1)
    grp = lax.broadcasted_iota(jnp.int32, (rows, 1), 0) >> (tq.bit_length() - 1)
    t = lax.broadcasted_iota(jnp.int32, (rows, 1), 0) & (tq - 1)
    dist_p = t + WINDOW - lax.broadcasted_iota(jnp.int32, (1, WINDOW), 1)
    dist_c = t - lax.broadcasted_iota(jnp.int32, (1, tq), 1)
    mask_p = (dist_p >= 0) & (dist_p < WINDOW)
    if not first_has_prev:
        mask_p = mask_p & (nblk > 0)
    mask_c = (dist_c >= 0) & (dist_c < WINDOW)
    dist_p = dist_p.astype(F32)
    dist_c = dist_c.astype(F32)

    out = None
    for kv in range(B_KV_HEADS):
        mk = (lane < HEAD_DIM) if kv == 0 else (lane >= HEAD_DIM)
        q_st = jnp.concatenate([jnp.where(mk, q[:, g * LANE:(g + 1) * LANE], 0.0) for g in range(B_GROUP)],
                               axis=0).astype(BF16)
        slope = jnp.zeros((rows, 1), F32)
        sink = jnp.zeros((rows, 1), F32)
        for g in range(B_GROUP):
            head = kv * B_GROUP + g
            slope = jnp.where(grp == g, 2.0 ** -(head + 1), slope)
            sink = jnp.where(grp == g, sink_ref[head], sink)
        s_p = jnp.where(mask_p, _dot_nt(q_st, kp) * scale - slope * dist_p, MASK_NEG)
        s_c = jnp.where(mask_c, _dot_nt(q_st, kc) * scale - slope * dist_c, MASK_NEG)
        m = jnp.maximum(jnp.maximum(jnp.max(s_p, axis=-1, keepdims=True), jnp.max(s_c, axis=-1, keepdims=True)),
                        sink)
        e_p = jnp.exp(s_p - m)
        e_c = jnp.exp(s_c - m)
        denom = jnp.sum(e_p, axis=-1, keepdims=True) + jnp.sum(e_c, axis=-1, keepdims=True) + jnp.exp(sink - m)
        inv = 1.0 / denom
        o_kv = (_dot((e_p * inv).astype(BF16), jnp.where(mk, vp, 0.0).astype(BF16))
                + _dot((e_c * inv).astype(BF16), jnp.where(mk, vc, 0.0).astype(BF16)))
        out = o_kv if out is None else out + o_kv

    tiles = [out[g * tq:(g + 1) * tq] for g in range(B_GROUP)]
    ssq = sum(jnp.sum(x * x, axis=-1, keepdims=True) for x in tiles)
    inv_rms = lax.rsqrt(ssq * (1.0 / B_WIDTH) + RMS_EPS)
    o_ref[...] = jnp.concatenate(tiles, axis=1) * inv_rms * gn_ref[...]


def _swa_call(sink, q2d, kprev, vprev, kcur2d, vcur2d, gn, *, n_seq, n_blk, tq, first_has_prev, prev_map):
    row = lambda b, n: (b * n_blk + n, 0)
    body = functools.partial(_swa_body, tq=tq, first_has_prev=first_has_prev)
    return pl.pallas_call(
        body,
        grid=(n_seq, n_blk),
        in_specs=[pl.BlockSpec(memory_space=pltpu.SMEM),
                  pl.BlockSpec((tq, B_WIDTH), row),
                  pl.BlockSpec((1, WINDOW, B_KV_WIDTH), prev_map),
                  pl.BlockSpec((1, WINDOW, B_KV_WIDTH), prev_map),
                  pl.BlockSpec((tq, B_KV_WIDTH), row),
                  pl.BlockSpec((tq, B_KV_WIDTH), row),
                  pl.BlockSpec((1, B_WIDTH), lambda b, n: (0, 0))],
        out_specs=pl.BlockSpec((tq, B_WIDTH), row),
        out_shape=jax.ShapeDtypeStruct((n_seq * n_blk * tq, B_WIDTH), F32),
        compiler_params=pltpu.CompilerParams(dimension_semantics=("arbitrary", "arbitrary"),
                                             vmem_limit_bytes=VMEM_LIMIT),
        name="swa",
    )(sink, q2d, kprev, vprev, kcur2d, vcur2d, gn)


def _outproj_body(x_ref, oa_ref, ob_ref, wa_ref, wb_ref, g_ref, wr_ref, x1_ref, h2_ref, lg_ref):
    x1 = x_ref[...] + _dot(oa_ref[...].astype(BF16), wa_ref[...]) + _dot(ob_ref[...].astype(BF16), wb_ref[...])
    x1_ref[...] = x1
    h2 = x1 * lax.rsqrt(jnp.mean(x1 * x1, axis=-1, keepdims=True) + RMS_EPS) * g_ref[...]
    h2_ref[...] = h2.astype(BF16)
    lg_ref[...] = _dot(h2, wr_ref[...], HI)


def _outproj_call(x, oa, ob, wa, wb, g, wr):
    n, d = x.shape
    tm = _row_tile(n, 512)
    row = lambda i: (i, 0)
    fixed = lambda i: (0, 0)
    return pl.pallas_call(
        _outproj_body,
        grid=(n // tm,),
        in_specs=[pl.BlockSpec((tm, d), row), pl.BlockSpec((tm, A_WIDTH), row), pl.BlockSpec((tm, B_WIDTH), row),
                  pl.BlockSpec((A_WIDTH, d), fixed), pl.BlockSpec((B_WIDTH, d), fixed), pl.BlockSpec((1, d), fixed),
                  pl.BlockSpec((d, ROUTE_W), fixed)],
        out_specs=[pl.BlockSpec((tm, d), row), pl.BlockSpec((tm, d), row), pl.BlockSpec((tm, ROUTE_W), row)],
        out_shape=[jax.ShapeDtypeStruct((n, d), F32), jax.ShapeDtypeStruct((n, d), BF16),
                   jax.ShapeDtypeStruct((n, ROUTE_W), F32)],
        compiler_params=pltpu.CompilerParams(dimension_semantics=("arbitrary",), vmem_limit_bytes=VMEM_LIMIT),
        name="outproj",
    )(x, oa, ob, wa, wb, g, wr)


def _expert_body(be_ref, nused_ref, x_ref, sw_ref, wg_ref, wu_ref, wd_ref, y_ref):
    i = pl.program_id(0)

    @pl.when(i < nused_ref[0])
    def _():
        x = x_ref[...]
        gate = _dot(x, wg_ref[0])
        up = _dot(x, wu_ref[0])
        mid = (gate * _sigmoid(gate) * up).astype(BF16)
        y_ref[...] = _dot(mid, wd_ref[0]) * sw_ref[...]

    @pl.when(i >= nused_ref[0])
    def _():
        y_ref[...] = jnp.zeros_like(y_ref)


def _expert_call(block_e, n_used, xbuf, sw, wg, wu, wd, *, tb):
    p, d = xbuf.shape
    ff = wg.shape[-1]
    grid_spec = pltpu.PrefetchScalarGridSpec(
        num_scalar_prefetch=2,
        grid=(p // tb,),
        in_specs=[pl.BlockSpec((tb, d), lambda i, be, nu: (i, 0)),
                  pl.BlockSpec((tb, 1), lambda i, be, nu: (i, 0)),
                  pl.BlockSpec((1, d, ff), lambda i, be, nu: (be[i], 0, 0)),
                  pl.BlockSpec((1, d, ff), lambda i, be, nu: (be[i], 0, 0)),
                  pl.BlockSpec((1, ff, d), lambda i, be, nu: (be[i], 0, 0))],
        out_specs=pl.BlockSpec((tb, d), lambda i, be, nu: (i, 0)),
    )
    return pl.pallas_call(
        _expert_body,
        grid_spec=grid_spec,
        out_shape=jax.ShapeDtypeStruct((p, d), F32),
        compiler_params=pltpu.CompilerParams(dimension_semantics=("arbitrary",), vmem_limit_bytes=VMEM_LIMIT),
        name="experts",
    )(block_e, n_used, xbuf, sw, wg, wu, wd)


def _final_body(x1_ref, y0_ref, y1_ref, g_ref, o_ref):
    x = x1_ref[...] + (y0_ref[...] + y1_ref[...])
    o_ref[...] = x * lax.rsqrt(jnp.mean(x * x, axis=-1, keepdims=True) + RMS_EPS) * g_ref[...]


def _final_call(x1, yg, g):
    n, d = x1.shape
    tm = _row_tile(n, 512)
    return pl.pallas_call(
        _final_body,
        grid=(n // tm,),
        in_specs=[pl.BlockSpec((tm, d), lambda i: (i, 0)), pl.BlockSpec((tm, d), lambda i: (i, 0)),
                  pl.BlockSpec((tm, d), lambda i: (i, 1)), pl.BlockSpec((1, d), lambda i: (0, 0))],
        out_specs=pl.BlockSpec((tm, d), lambda i: (i, 0)),
        out_shape=jax.ShapeDtypeStruct((n, d), F32),
        compiler_params=pltpu.CompilerParams(dimension_semantics=("arbitrary",), vmem_limit_bytes=VMEM_LIMIT),
        name="final",
    )(x1, yg, yg, g)


def _route(logits, b_group, b_expert, tb):
    n = logits.shape[0]
    pg = jax.nn.softmax(logits[:, :N_GROUPS] + b_group, axis=-1)
    g_idx = jnp.argmax(pg, axis=-1)
    g_w = jnp.take_along_axis(pg, g_idx[:, None], axis=-1)
    le = (logits[:, N_GROUPS:N_GROUPS + N_EXPERTS] + b_expert).reshape(n, N_GROUPS, EXPERTS_PER_GROUP)
    le_g = jnp.take_along_axis(le, g_idx[:, None, None], axis=1)[:, 0]
    top_p, top_i = lax.top_k(jax.nn.softmax(le_g, axis=-1), TOP_K_INNER)
    wts = g_w * top_p / jnp.sum(top_p, axis=-1, keepdims=True)
    e_idx = (g_idx[:, None] * EXPERTS_PER_GROUP + top_i).astype(jnp.int32)

    n_assign = n * TOP_K_INNER
    flat_e = e_idx.reshape(n_assign)
    onehot = (flat_e[:, None] == jnp.arange(N_EXPERTS, dtype=jnp.int32)[None, :]).astype(jnp.int32)
    csum = jnp.cumsum(onehot, axis=0)
    rank = jnp.take_along_axis(csum, flat_e[:, None], axis=1)[:, 0] - 1
    counts = csum[-1]
    padded = (counts + tb - 1) // tb * tb
    pend = jnp.cumsum(padded)
    pstarts = pend - padded
    dest = (pstarts[flat_e] + rank).astype(jnp.int32)
    n_blocks = -(-n_assign // tb) + N_EXPERTS
    block_e = jnp.clip(jnp.searchsorted(pend, jnp.arange(n_blocks, dtype=jnp.int32) * tb, side="right"),
                       0, N_EXPERTS - 1).astype(jnp.int32)
    n_used = (pend[-1:] // tb).astype(jnp.int32)
    tok = jnp.arange(n_assign, dtype=jnp.int32) // TOP_K_INNER
    src_tok = jnp.zeros((n_blocks * tb,), jnp.int32).at[dest].set(tok)
    sw = jnp.zeros((n_blocks * tb,), F32).at[dest].set(wts.reshape(n_assign))
    return dest, block_e, n_used, src_tok, sw


def _q_perm():
    cols = []
    for g in range(B_GROUP):
        for kv in range(B_KV_HEADS):
            h = kv * B_GROUP + g
            cols.extend(range(h * HEAD_DIM, (h + 1) * HEAD_DIM))
    return jnp.array(cols, dtype=jnp.int32)


def _layer(x_prompt, x_sample, state_rwkv, state_shift, cache_win_k, cache_win_v, prm, norm_final_g, *,
           chunk, moe_block):
    bp, tp, d = x_prompt.shape
    bs, ts, _ = x_sample.shape
    n_p, n_s = bp * tp, bs * ts
    n = n_p + n_s
    ts_pad = 8
    x = jnp.concatenate([x_prompt.reshape(n_p, d), x_sample.reshape(n_s, d)], axis=0)

    qp = _q_perm()
    w_in = prm["w_in"]
    w_in = jnp.concatenate([w_in[:, :SHIFT_W], w_in[:, SHIFT_W:SHIFT_W + B_WIDTH][:, qp],
                            w_in[:, SHIFT_W + B_WIDTH:]], axis=1).astype(BF16)
    pa, q, k, v = _inproj_call(x, prm["norm_mix_g"][None], w_in)

    zero_blk = jnp.zeros((LORA_W // 2, A_WIDTH), F32)
    wlora = jnp.concatenate([jnp.concatenate([prm["w_decay_up"], zero_blk], axis=1),
                             jnp.concatenate([zero_blk, prm["w_iclr_up"]], axis=1)], axis=0).astype(BF16)
    rp = dict(mu=prm["mu_shift"][None], w0=prm["w_decay0"][None], a0=prm["w_iclr0"][None], wlora=wlora,
              wgate=prm["w_gate_up"].astype(BF16), kk=prm["k_k"][None], ka=prm["k_a"][None],
              rk=prm["r_k"].reshape(1, A_WIDTH), lng=prm["lnx_g"][None], lnb=prm["lnx_b"][None])

    oa_p, sbd_p = _rwkv_call(pa, jnp.zeros((bp, 1, SHIFT_W), F32), jnp.zeros((bp, N_PAIRS, LANE, LANE), F32), rp,
                             n_seq=bp, n_chunks=tp // chunk, chunk=chunk, t_real=chunk)
    pad_rows = lambda a: jnp.pad(a[n_p:].reshape(bs, ts, -1), ((0, 0), (0, ts_pad - ts), (0, 0))).reshape(
        bs * ts_pad, -1)
    oa_s, sbd_s = _rwkv_call(pad_rows(pa), state_shift[:, None, :], _state_to_pairs(state_rwkv), rp,
                             n_seq=bs, n_chunks=1, chunk=ts_pad, t_real=ts)
    oa_s = oa_s.reshape(bs, ts_pad, A_WIDTH)[:, :ts].reshape(n_s, A_WIDTH)

    gn = prm["attn_norm_g"][qp][None]
    nb = tp // WINDOW
    k_p3 = k.reshape(-1, WINDOW, B_KV_WIDTH)
    v_p3 = v.reshape(-1, WINDOW, B_KV_WIDTH)
    ob_p = _swa_call(prm["attn_sink"], q, k_p3, v_p3, k, v, gn, n_seq=bp, n_blk=nb, tq=WINDOW,
                     first_has_prev=False, prev_map=lambda b, i: (b * nb + jnp.maximum(i - 1, 0), 0, 0))
    ob_s = _swa_call(prm["attn_sink"], pad_rows(q), cache_win_k.reshape(bs, WINDOW, B_KV_WIDTH),
                     cache_win_v.reshape(bs, WINDOW, B_KV_WIDTH), pad_rows(k), pad_rows(v), gn,
                     n_seq=bs, n_blk=1, tq=ts_pad, first_has_prev=True, prev_map=lambda b, i: (b, 0, 0))
    ob_s = ob_s.reshape(bs, ts_pad, B_WIDTH)[:, :ts].reshape(n_s, B_WIDTH)

    oa = jnp.concatenate([oa_p, oa_s], axis=0)
    ob = jnp.concatenate([ob_p[:n_p], ob_s], axis=0)
    w_out = prm["w_out"]
    wr = jnp.zeros((d, ROUTE_W), F32)
    wr = wr.at[:, :N_GROUPS].set(prm["w_route_group"]).at[:, N_GROUPS:N_GROUPS + N_EXPERTS].set(
        prm["w_route_expert"])
    x1, h2, logits = _outproj_call(x, oa, ob, w_out[:A_WIDTH].astype(BF16), w_out[A_WIDTH:][qp].astype(BF16),
                                   prm["norm_ffn_g"][None], wr)

    dest, block_e, n_used, src_tok, sw = _route(logits, prm["b_route_group"], prm["b_route_expert"], moe_block)
    xbuf = h2[src_tok]
    ybuf = _expert_call(block_e, n_used, xbuf, sw[:, None], prm["w_exp_gate"].astype(BF16),
                        prm["w_exp_up"].astype(BF16), prm["w_exp_down"].astype(BF16), tb=moe_block)
    yg = ybuf[dest].reshape(n, TOP_K_INNER * d)
    y = _final_call(x1, yg, norm_final_g[None])

    kv4 = lambda a: a.reshape(a.shape[0], a.shape[1], B_KV_HEADS, HEAD_DIM)
    k_p = k[:n_p].reshape(bp, tp, B_KV_WIDTH)[:, -WINDOW:]
    v_p = v[:n_p].reshape(bp, tp, B_KV_WIDTH)[:, -WINDOW:]
    k_s = jnp.concatenate([cache_win_k.reshape(bs, -1, B_KV_WIDTH), k[n_p:].reshape(bs, ts, B_KV_WIDTH)],
                          axis=1)[:, -cache_win_k.shape[1]:]
    v_s = jnp.concatenate([cache_win_v.reshape(bs, -1, B_KV_WIDTH), v[n_p:].reshape(bs, ts, B_KV_WIDTH)],
                          axis=1)[:, -cache_win_v.shape[1]:]
    return (y[:n_p].reshape(bp, tp, d), y[n_p:].reshape(bs, ts, d),
            _pairs_to_state(sbd_p), pa[:n_p].reshape(bp, tp, SHIFT_W)[:, -1], kv4(k_p), kv4(v_p),
            _pairs_to_state(sbd_s), pa[n_p:].reshape(bs, ts, SHIFT_W)[:, -1], kv4(k_s), kv4(v_s))


def kernel(x_prompt, x_sample, state_rwkv, state_shift, cache_win_k, cache_win_v, norm_mix_g, w_in, mu_shift, w_decay0, w_decay_up, w_iclr0, w_iclr_up, w_gate_up, k_k, k_a, r_k, lnx_g, lnx_b, attn_sink, attn_norm_g, w_out, norm_ffn_g, w_route_group, b_route_group, w_route_expert, b_route_expert, w_exp_gate, w_exp_up, w_exp_down, norm_final_g):
    assert norm_mix_g.shape[0] == 1, "single-layer trunk"
    prm = dict(norm_mix_g=norm_mix_g[0], w_in=w_in[0], mu_shift=mu_shift[0], w_decay0=w_decay0[0],
               w_decay_up=w_decay_up[0], w_iclr0=w_iclr0[0], w_iclr_up=w_iclr_up[0], w_gate_up=w_gate_up[0],
               k_k=k_k[0], k_a=k_a[0], r_k=r_k[0], lnx_g=lnx_g[0], lnx_b=lnx_b[0], attn_sink=attn_sink[0],
               attn_norm_g=attn_norm_g[0], w_out=w_out[0], norm_ffn_g=norm_ffn_g[0],
               w_route_group=w_route_group[0], b_route_group=b_route_group[0],
               w_route_expert=w_route_expert[0], b_route_expert=b_route_expert[0],
               w_exp_gate=w_exp_gate[0], w_exp_up=w_exp_up[0], w_exp_down=w_exp_down[0])
    outs = _layer(x_prompt, x_sample, state_rwkv[0], state_shift[0], cache_win_k[0], cache_win_v[0], prm,
                  norm_final_g, chunk=64, moe_block=256)
    y_p, y_s, s_p, sh_p, kp, vp, s_s, sh_s, ks, vs = outs
    return (y_p, y_s, s_p[None], sh_p[None], kp[None], vp[None], s_s[None], sh_s[None], ks[None], vs[None])
```

```python
import functools

import jax
import jax.numpy as jnp
from jax import lax
from jax.experimental import pallas as pl
from jax.experimental.pallas import tpu as pltpu

F32 = jnp.float32
BF16 = jnp.bfloat16

HEAD_DIM = 64
A_HEADS = 8
A_WIDTH = A_HEADS * HEAD_DIM
B_HEADS = 8
B_KV_HEADS = 2
B_GROUP = B_HEADS // B_KV_HEADS
B_WIDTH = B_HEADS * HEAD_DIM
B_KV_WIDTH = B_KV_HEADS * HEAD_DIM
DECAY_LORA = 64
ICLR_LORA = 64
GATE_LORA = 128
LORA_W = DECAY_LORA + ICLR_LORA
SHIFT_W = 3 * A_WIDTH + LORA_W + GATE_LORA
IN_W = SHIFT_W + B_WIDTH + 2 * B_KV_WIDTH
WINDOW = 128
N_GROUPS = 4
EXPERTS_PER_GROUP = 4
N_EXPERTS = N_GROUPS * EXPERTS_PER_GROUP
TOP_K_INNER = 2
EXPERT_FF = 512
RMS_EPS = 1e-6
LNX_EPS = 64e-5
DECAY_OFFSET = 0.5

LANE = 128
PAIR = LANE // HEAD_DIM
N_PAIRS = A_HEADS // PAIR
ROUTE_W = LANE
MASK_NEG = -1e30
VMEM_LIMIT = 48 * 1024 * 1024

HEAD_SHIFT = HEAD_DIM.bit_length() - 1
HI = lax.Precision.HIGHEST


def _row_tile(n, cap):
    t = cap
    while n % t:
        t //= 2
    return t


def _dot(a, b, precision=None):
    return jnp.dot(a, b, preferred_element_type=F32, precision=precision)


def _dot_nt(a, b, precision=None):
    return lax.dot_general(a, b, (((1,), (1,)), ((), ())), preferred_element_type=F32, precision=precision)


def _dot_tn(a, b, precision=None):
    return lax.dot_general(a, b, (((0,), (0,)), ((), ())), preferred_element_type=F32, precision=precision)


def _sigmoid(x):
    return 1.0 / (1.0 + jnp.exp(-x))


def _hi_lo(x, axis):
    hi = x.astype(BF16)
    lo = (x - hi.astype(F32)).astype(BF16)
    return jnp.concatenate([hi, lo], axis=axis)


def _inproj_body(x_ref, g_ref, w_ref, pa_ref, q_ref, k_ref, v_ref):
    x = x_ref[...]
    h = x * lax.rsqrt(jnp.mean(x * x, axis=-1, keepdims=True) + RMS_EPS) * g_ref[...]
    p = _dot(h.astype(BF16), w_ref[...])
    pa_ref[...] = p[:, :SHIFT_W]
    q_ref[...] = p[:, SHIFT_W:SHIFT_W + B_WIDTH]
    k_ref[...] = p[:, SHIFT_W + B_WIDTH:SHIFT_W + B_WIDTH + B_KV_WIDTH]
    v_ref[...] = p[:, SHIFT_W + B_WIDTH + B_KV_WIDTH:]


def _inproj_call(x, g, w_bf16):
    n, d = x.shape
    tm = _row_tile(n, 512)
    row = lambda i: (i, 0)
    fixed = lambda i: (0, 0)
    return pl.pallas_call(
        _inproj_body,
        grid=(n // tm,),
        in_specs=[pl.BlockSpec((tm, d), row), pl.BlockSpec((1, d), fixed), pl.BlockSpec((d, IN_W), fixed)],
        out_specs=[pl.BlockSpec((tm, SHIFT_W), row), pl.BlockSpec((tm, B_WIDTH), row),
                   pl.BlockSpec((tm, B_KV_WIDTH), row), pl.BlockSpec((tm, B_KV_WIDTH), row)],
        out_shape=[jax.ShapeDtypeStruct((n, SHIFT_W), F32), jax.ShapeDtypeStruct((n, B_WIDTH), F32),
                   jax.ShapeDtypeStruct((n, B_KV_WIDTH), F32), jax.ShapeDtypeStruct((n, B_KV_WIDTH), F32)],
        compiler_params=pltpu.CompilerParams(dimension_semantics=("arbitrary",), vmem_limit_bytes=VMEM_LIMIT),
        name="inproj",
    )(x, g, w_bf16)


def _rwkv_body(pa_ref, shift0_ref, s0_ref, mu_ref, w0_ref, a0_ref, wlora_ref, wgate_ref, kk_ref, ka_ref,
               rk_ref, lng_ref, lnb_ref, oa_ref, sout_ref, s_scr, prev_scr, *, chunk, t_real):
    C = chunk
    c = pl.program_id(1)

    @pl.when(c == 0)
    def _():
        s_scr[...] = s0_ref[0]
        prev_scr[...] = shift0_ref[0]

    pa = pa_ref[...]
    row = lax.broadcasted_iota(jnp.int32, (C, 1), 0)
    pa_prev = jnp.where(row == 0, prev_scr[...], pltpu.roll(pa, 1, axis=0))
    prev_scr[...] = pa[t_real - 1:t_real]
    xm = pa + mu_ref[...] * (pa_prev - pa)

    r = xm[:, :A_WIDTH]
    k = xm[:, A_WIDTH:2 * A_WIDTH]
    v = xm[:, 2 * A_WIDTH:3 * A_WIDTH]
    lora_in = xm[:, 3 * A_WIDTH:3 * A_WIDTH + LORA_W]
    gd = xm[:, 3 * A_WIDTH + LORA_W:]

    lane = lax.broadcasted_iota(jnp.int32, (1, LANE), 1)
    lo_half = lane < HEAD_DIM
    z = jnp.where(lo_half, jnp.tanh(lora_in), lora_in)
    lw = _dot(z.astype(BF16), wlora_ref[...])
    dec_pre = w0_ref[...] + lw[:, :A_WIDTH]
    a = _sigmoid(a0_ref[...] + lw[:, A_WIDTH:])
    neg = -dec_pre
    softplus = jnp.maximum(neg, 0.0) + jnp.log(1.0 + jnp.exp(-jnp.abs(neg)))
    logdec = -jnp.exp(-softplus - DECAY_OFFSET)
    g = _dot(_sigmoid(gd).astype(BF16), wgate_ref[...])

    ri = lax.broadcasted_iota(jnp.int32, (LANE, LANE), 0)
    ci = lax.broadcasted_iota(jnp.int32, (LANE, LANE), 1)
    seg = jnp.where((ri >> HEAD_SHIFT) == (ci >> HEAD_SHIFT), 1.0, 0.0).astype(BF16)
    seg2 = jnp.concatenate([seg, seg], axis=0)

    def headsum(x):
        return jnp.concatenate([_dot(_hi_lo(x[:, j * LANE:(j + 1) * LANE], 1), seg2) for j in range(N_PAIRS)],
                               axis=1)

    kk = k * kk_ref[...]
    kk = kk * lax.rsqrt(jnp.maximum(headsum(kk * kk), 1e-24))
    k = k * (1.0 + (a - 1.0) * ka_ref[...])
    bonus = headsum(r * k * rk_ref[...]) * v

    if t_real < C:
        valid = row < t_real
        logdec = jnp.where(valid, logdec, 0.0)
        kk = jnp.where(valid, kk, 0.0)
        k = jnp.where(valid, k, 0.0)
        v = jnp.where(valid, v, 0.0)

    cum = logdec
    shift = 1
    while shift < C:
        cum = cum + jnp.where(row >= shift, pltpu.roll(cum, shift, axis=0), 0.0)
        shift *= 2
    cum_end = cum[C - 1:C]
    w_incl = jnp.exp(cum)
    w_prev = jnp.exp(cum - logdec)
    w_inv = jnp.exp(-cum)
    w_end = jnp.exp(cum_end - cum)
    w_chunk = jnp.exp(cum_end)
    kka = kk * a
    terms = dict(A=-kk * w_prev, R=r * w_incl, B=kka * w_inv, K=k * w_inv, V=v, Be=kka * w_end, Ke=k * w_end)

    C2 = PAIR * C
    sr = lax.broadcasted_iota(jnp.int32, (C2, C2), 0)
    sc = lax.broadcasted_iota(jnp.int32, (C2, C2), 1)
    c_shift = C.bit_length() - 1
    same_head = (sr >> c_shift) == (sc >> c_shift)
    strict = same_head & ((sc & (C - 1)) < (sr & (C - 1)))
    incl = same_head & ((sc & (C - 1)) <= (sr & (C - 1)))
    n_levels = max(1, (C - 1).bit_length())
    bf = lambda x: x.astype(BF16)

    ys = []
    for j in range(N_PAIRS):
        sl = slice(j * LANE, (j + 1) * LANE)

        def st(name):
            x = terms[name][:, sl]
            return jnp.concatenate([jnp.where(lo_half, x, 0.0), jnp.where(lo_half, 0.0, x)], axis=0)

        r_st = st("R")
        a_st, rb_st, b_st, k_st, v_st, be_st, ke_st = (bf(st(nm)) for nm in ("A", "R", "B", "K", "V", "Be", "Ke"))
        if C2 % LANE == 0:
            m1 = _dot_nt(jnp.concatenate([a_st, rb_st], axis=0), jnp.concatenate([b_st, k_st], axis=0))
            m_ab, m_ak, m_rb, m_rk = m1[:C2, :C2], m1[:C2, C2:], m1[C2:, :C2], m1[C2:, C2:]
        else:
            m_ab, m_ak = _dot_nt(a_st, b_st), _dot_nt(a_st, k_st)
            m_rb, m_rk = _dot_nt(rb_st, b_st), _dot_nt(rb_st, k_st)
        aak = bf(jnp.where(strict, m_ak, 0.0))
        arb = bf(jnp.where(incl, m_rb, 0.0))
        ark = bf(jnp.where(incl, m_rk, 0.0))

        x = jnp.concatenate([a_st.astype(F32), _dot(aak, v_st)], axis=1)
        nn = jnp.where(strict, m_ab, 0.0)
        for lvl in range(n_levels):
            nb = bf(nn)
            x = x + _dot(nb, bf(x))
            if lvl + 1 < n_levels:
                nn = _dot(nb, nb)
        xb = bf(x)
        zed = _dot(arb, xb)
        rhat = r_st + zed[:, :LANE]
        y0 = _dot(ark, v_st) + zed[:, LANE:]
        p_mat = _dot_tn(be_st, xb[:, :LANE])
        q_t = _dot_tn(jnp.concatenate([xb[:, LANE:], v_st], axis=0), jnp.concatenate([be_st, ke_st], axis=0))

        s_prev = s_scr[j]
        s_cat = _hi_lo(s_prev, 0)
        yy = _dot_nt(bf(rhat), s_cat)
        y_st = yy[:, :LANE] + yy[:, LANE:] + y0
        sp = _dot_nt(s_cat, bf(p_mat))
        s_scr[j] = s_prev * w_chunk[:, sl] + sp[:LANE] + sp[LANE:] + q_t
        ys.append(y_st[:C] + y_st[C:])

    y = jnp.concatenate(ys, axis=1)
    inv_n = 1.0 / HEAD_DIM
    mean = headsum(y) * inv_n
    d = y - mean
    var = headsum(d * d) * inv_n
    yn = d * lax.rsqrt(var + LNX_EPS) * lng_ref[...] + lnb_ref[...]
    oa_ref[...] = (yn + bonus) * g
    sout_ref[0] = s_scr[...]


def _rwkv_call(pa2d, shift0, s0_bd, prm, *, n_seq, n_chunks, chunk, t_real):
    row = lambda b, c: (b * n_chunks + c, 0)
    fixed = lambda b, c: (0, 0)
    vec = lambda w: pl.BlockSpec((1, w), fixed)
    body = functools.partial(_rwkv_body, chunk=chunk, t_real=t_real)
    n_rows = n_seq * n_chunks * chunk
    return pl.pallas_call(
        body,
        grid=(n_seq, n_chunks),
        in_specs=[pl.BlockSpec((chunk, SHIFT_W), row),
                  pl.BlockSpec((1, 1, SHIFT_W), lambda b, c: (b, 0, 0)),
                  pl.BlockSpec((1, N_PAIRS, LANE, LANE), lambda b, c: (b, 0, 0, 0)),
                  vec(SHIFT_W), vec(A_WIDTH), vec(A_WIDTH),
                  pl.BlockSpec((LORA_W, 2 * A_WIDTH), fixed), pl.BlockSpec((GATE_LORA, A_WIDTH), fixed),
                  vec(A_WIDTH), vec(A_WIDTH), vec(A_WIDTH), vec(A_WIDTH), vec(A_WIDTH)],
        out_specs=[pl.BlockSpec((chunk, A_WIDTH), row),
                   pl.BlockSpec((1, N_PAIRS, LANE, LANE), lambda b, c: (b, 0, 0, 0))],
        out_shape=[jax.ShapeDtypeStruct((n_rows, A_WIDTH), F32),
                   jax.ShapeDtypeStruct((n_seq, N_PAIRS, LANE, LANE), F32)],
        scratch_shapes=[pltpu.VMEM((N_PAIRS, LANE, LANE), F32), pltpu.VMEM((1, SHIFT_W), F32)],
        compiler_params=pltpu.CompilerParams(dimension_semantics=("arbitrary", "arbitrary"),
                                             vmem_limit_bytes=VMEM_LIMIT),
        name="rwkv",
    )(pa2d, shift0, s0_bd, prm["mu"], prm["w0"], prm["a0"], prm["wlora"], prm["wgate"], prm["kk"], prm["ka"],
      prm["rk"], prm["lng"], prm["lnb"])


def _state_to_pairs(state):
    b = state.shape[0]
    st = state.reshape(b, N_PAIRS, PAIR, HEAD_DIM, HEAD_DIM)
    eye = jnp.eye(PAIR, dtype=state.dtype)
    bd = st[:, :, :, :, None, :] * eye[None, None, :, None, :, None]
    return bd.reshape(b, N_PAIRS, LANE, LANE)


def _pairs_to_state(bd):
    b = bd.shape[0]
    x = bd.reshape(b, N_PAIRS, PAIR, HEAD_DIM, PAIR, HEAD_DIM)
    st = jnp.stack([x[:, :, i, :, i, :] for i in range(PAIR)], axis=2)
    return st.reshape(b, A_HEADS, HEAD_DIM, HEAD_DIM)


def _swa_body(sink_ref, q_ref, kp_ref, vp_ref, kc_ref, vc_ref, gn_ref, o_ref, *, tq, first_has_prev):
    nblk = pl.program_id(1)
    q = q_ref[...]
    kp = kp_ref[0].astype(BF16)
    vp = vp_ref[0]
    kc = kc_ref[...].astype(BF16)
    vc = vc_ref[...]
    rows = B_GROUP * tq
    scale = HEAD_DIM ** -0.5
    lane = lax.broadcasted_iota(jnp.int32, (1, LANE), 1)
    grp = lax.broadcasted_iota(jnp.int32, (rows, 1), 0) >> (tq.bit_length() - 1)
    t = lax.broadcasted_iota(jnp.int32, (rows, 1), 0) & (tq - 1)
    dist_p = t + WINDOW - lax.broadcasted_iota(jnp.int32, (1, WINDOW), 1)
    dist_c = t - lax.broadcasted_iota(jnp.int32, (1, tq), 1)
    mask_p = (dist_p >= 0) & (dist_p < WINDOW)
    if not first_has_prev:
        mask_p = mask_p & (nblk > 0)
    mask_c = (dist_c >= 0) & (dist_c < WINDOW)
    dist_p = dist_p.astype(F32)
    dist_c = dist_c.astype(F32)

    out = None
    for kv in range(B_KV_HEADS):
        mk = (lane < HEAD_DIM) if kv == 0 else (lane >= HEAD_DIM)
        q_st = jnp.concatenate([jnp.where(mk, q[:, g * LANE:(g + 1) * LANE], 0.0) for g in range(B_GROUP)],
                               axis=0).astype(BF16)
        slope = jnp.zeros((rows, 1), F32)
        sink = jnp.zeros((rows, 1), F32)
        for g in range(B_GROUP):
            head = kv * B_GROUP + g
            slope = jnp.where(grp == g, 2.0 ** -(head + 1), slope)
            sink = jnp.where(grp == g, sink_ref[head], sink)
        s_p = jnp.where(mask_p, _dot_nt(q_st, kp) * scale - slope * dist_p, MASK_NEG)
        s_c = jnp.where(mask_c, _dot_nt(q_st, kc) * scale - slope * dist_c, MASK_NEG)
        m = jnp.maximum(jnp.maximum(jnp.max(s_p, axis=-1, keepdims=True), jnp.max(s_c, axis=-1, keepdims=True)),
                        sink)
        e_p = jnp.exp(s_p - m)
        e_c = jnp.exp(s_c - m)
        denom = jnp.sum(e_p, axis=-1, keepdims=True) + jnp.sum(e_c, axis=-1, keepdims=True) + jnp.exp(sink - m)
        inv = 1.0 / denom
        o_kv = (_dot((e_p * inv).astype(BF16), jnp.where(mk, vp, 0.0).astype(BF16))
                + _dot((e_c * inv).astype(BF16), jnp.where(mk, vc, 0.0).astype(BF16)))
        out = o_kv if out is None else out + o_kv

    tiles = [out[g * tq:(g + 1) * tq] for g in range(B_GROUP)]
    ssq = sum(jnp.sum(x * x, axis=-1, keepdims=True) for x in tiles)
    inv_rms = lax.rsqrt(ssq * (1.0 / B_WIDTH) + RMS_EPS)
    o_ref[...] = jnp.concatenate(tiles, axis=1) * inv_rms * gn_ref[...]


def _swa_call(sink, q2d, kprev, vprev, kcur2d, vcur2d, gn, *, n_seq, n_blk, tq, first_has_prev, prev_map):
    row = lambda b, n: (b * n_blk + n, 0)
    body = functools.partial(_swa_body, tq=tq, first_has_prev=first_has_prev)
    return pl.pallas_call(
        body,
        grid=(n_seq, n_blk),
        in_specs=[pl.BlockSpec(memory_space=pltpu.SMEM),
                  pl.BlockSpec((tq, B_WIDTH), row),
                  pl.BlockSpec((1, WINDOW, B_KV_WIDTH), prev_map),
                  pl.BlockSpec((1, WINDOW, B_KV_WIDTH), prev_map),
                  pl.BlockSpec((tq, B_KV_WIDTH), row),
                  pl.BlockSpec((tq, B_KV_WIDTH), row),
                  pl.BlockSpec((1, B_WIDTH), lambda b, n: (0, 0))],
        out_specs=pl.BlockSpec((tq, B_WIDTH), row),
        out_shape=jax.ShapeDtypeStruct((n_seq * n_blk * tq, B_WIDTH), F32),
        compiler_params=pltpu.CompilerParams(dimension_semantics=("arbitrary", "arbitrary"),
                                             vmem_limit_bytes=VMEM_LIMIT),
        name="swa",
    )(sink, q2d, kprev, vprev, kcur2d, vcur2d, gn)


def _outproj_body(x_ref, oa_ref, ob_ref, wa_ref, wb_ref, g_ref, wr_ref, x1_ref, h2_ref, lg_ref):
    x1 = x_ref[...] + _dot(oa_ref[...].astype(BF16), wa_ref[...]) + _dot(ob_ref[...].astype(BF16), wb_ref[...])
    x1_ref[...] = x1
    h2 = x1 * lax.rsqrt(jnp.mean(x1 * x1, axis=-1, keepdims=True) + RMS_EPS) * g_ref[...]
    h2_ref[...] = h2.astype(BF16)
    lg_ref[...] = _dot(h2, wr_ref[...], HI)


def _outproj_call(x, oa, ob, wa, wb, g, wr):
    n, d = x.shape
    tm = _row_tile(n, 512)
    row = lambda i: (i, 0)
    fixed = lambda i: (0, 0)
    return pl.pallas_call(
        _outproj_body,
        grid=(n // tm,),
        in_specs=[pl.BlockSpec((tm, d), row), pl.BlockSpec((tm, A_WIDTH), row), pl.BlockSpec((tm, B_WIDTH), row),
                  pl.BlockSpec((A_WIDTH, d), fixed), pl.BlockSpec((B_WIDTH, d), fixed), pl.BlockSpec((1, d), fixed),
                  pl.BlockSpec((d, ROUTE_W), fixed)],
        out_specs=[pl.BlockSpec((tm, d), row), pl.BlockSpec((tm, d), row), pl.BlockSpec((tm, ROUTE_W), row)],
        out_shape=[jax.ShapeDtypeStruct((n, d), F32), jax.ShapeDtypeStruct((n, d), BF16),
                   jax.ShapeDtypeStruct((n, ROUTE_W), F32)],
        compiler_params=pltpu.CompilerParams(dimension_semantics=("arbitrary",), vmem_limit_bytes=VMEM_LIMIT),
        name="outproj",
    )(x, oa, ob, wa, wb, g, wr)


def _expert_body(be_ref, nused_ref, x_ref, sw_ref, wg_ref, wu_ref, wd_ref, y_ref):
    i = pl.program_id(0)

    @pl.when(i < nused_ref[0])
    def _():
        x = x_ref[...]
        gate = _dot(x, wg_ref[0])
        up = _dot(x, wu_ref[0])
        mid = (gate * _sigmoid(gate) * up).astype(BF16)
        y_ref[...] = _dot(mid, wd_ref[0]) * sw_ref[...]

    @pl.when(i >= nused_ref[0])
    def _():
        y_ref[...] = jnp.zeros_like(y_ref)


def _expert_call(block_e, n_used, xbuf, sw, wg, wu, wd, *, tb):
    p, d = xbuf.shape
    ff = wg.shape[-1]
    grid_spec = pltpu.PrefetchScalarGridSpec(
        num_scalar_prefetch=2,
        grid=(p // tb,),
        in_specs=[pl.BlockSpec((tb, d), lambda i, be, nu: (i, 0)),
                  pl.BlockSpec((tb, 1), lambda i, be, nu: (i, 0)),
                  pl.BlockSpec((1, d, ff), lambda i, be, nu: (be[i], 0, 0)),
                  pl.BlockSpec((1, d, ff), lambda i, be, nu: (be[i], 0, 0)),
                  pl.BlockSpec((1, ff, d), lambda i, be, nu: (be[i], 0, 0))],
        out_specs=pl.BlockSpec((tb, d), lambda i, be, nu: (i, 0)),
    )
    return pl.pallas_call(
        _expert_body,
        grid_spec=grid_spec,
        out_shape=jax.ShapeDtypeStruct((p, d), F32),
        compiler_params=pltpu.CompilerParams(dimension_semantics=("arbitrary",), vmem_limit_bytes=VMEM_LIMIT),
        name="experts",
    )(block_e, n_used, xbuf, sw, wg, wu, wd)


def _final_body(x1_ref, y0_ref, y1_ref, g_ref, o_ref):
    x = x1_ref[...] + (y0_ref[...] + y1_ref[...])
    o_ref[...] = x * lax.rsqrt(jnp.mean(x * x, axis=-1, keepdims=True) + RMS_EPS) * g_ref[...]


def _final_call(x1, yg, g):
    n, d = x1.shape
    tm = _row_tile(n, 512)
    return pl.pallas_call(
        _final_body,
        grid=(n // tm,),
        in_specs=[pl.BlockSpec((tm, d), lambda i: (i, 0)), pl.BlockSpec((tm, d), lambda i: (i, 0)),
                  pl.BlockSpec((tm, d), lambda i: (i, 1)), pl.BlockSpec((1, d), lambda i: (0, 0))],
        out_specs=pl.BlockSpec((tm, d), lambda i: (i, 0)),
        out_shape=jax.ShapeDtypeStruct((n, d), F32),
        compiler_params=pltpu.CompilerParams(dimension_semantics=("arbitrary",), vmem_limit_bytes=VMEM_LIMIT),
        name="final",
    )(x1, yg, yg, g)


def _route(logits, b_group, b_expert, tb):
    n = logits.shape[0]
    pg = jax.nn.softmax(logits[:, :N_GROUPS] + b_group, axis=-1)
    g_idx = jnp.argmax(pg, axis=-1)
    g_w = jnp.take_along_axis(pg, g_idx[:, None], axis=-1)
    le = (logits[:, N_GROUPS:N_GROUPS + N_EXPERTS] + b_expert).reshape(n, N_GROUPS, EXPERTS_PER_GROUP)
    le_g = jnp.take_along_axis(le, g_idx[:, None, None], axis=1)[:, 0]
    top_p, top_i = lax.top_k(jax.nn.softmax(le_g, axis=-1), TOP_K_INNER)
    wts = g_w * top_p / jnp.sum(top_p, axis=-1, keepdims=True)
    e_idx = (g_idx[:, None] * EXPERTS_PER_GROUP + top_i).astype(jnp.int32)

    n_assign = n * TOP_K_INNER
    flat_e = e_idx.reshape(n_assign)
    onehot = (flat_e[:, None] == jnp.arange(N_EXPERTS, dtype=jnp.int32)[None, :]).astype(jnp.int32)
    csum = jnp.cumsum(onehot, axis=0)
    rank = jnp.take_along_axis(csum, flat_e[:, None], axis=1)[:, 0] - 1
    counts = csum[-1]
    padded = (counts + tb - 1) // tb * tb
    pend = jnp.cumsum(padded)
    pstarts = pend - padded
    dest = (pstarts[flat_e] + rank).astype(jnp.int32)
    n_blocks = -(-n_assign // tb) + N_EXPERTS
    block_e = jnp.clip(jnp.searchsorted(pend, jnp.arange(n_blocks, dtype=jnp.int32) * tb, side="right"),
                       0, N_EXPERTS - 1).astype(jnp.int32)
    n_used = (pend[-1:] // tb).astype(jnp.int32)
    tok = jnp.arange(n_assign, dtype=jnp.int32) // TOP_K_INNER
    src_tok = jnp.zeros((n_blocks * tb,), jnp.int32).at[dest].set(tok)
    sw = jnp.zeros((n_blocks * tb,), F32).at[dest].set(wts.reshape(n_assign))
    return dest, block_e, n_used, src_tok, sw


def _q_perm():
    cols = []
    for g in range(B_GROUP):
        for kv in range(B_KV_HEADS):
            h = kv * B_GROUP + g
            cols.extend(range(h * HEAD_DIM, (h + 1) * HEAD_DIM))
    return jnp.array(cols, dtype=jnp.int32)


def _layer(x_prompt, x_sample, state_rwkv, state_shift, cache_win_k, cache_win_v, prm, norm_final_g, *,
           chunk, moe_block):
    bp, tp, d = x_prompt.shape
    bs, ts, _ = x_sample.shape
    n_p, n_s = bp * tp, bs * ts
    n = n_p + n_s
    ts_pad = 8
    x = jnp.concatenate([x_prompt.reshape(n_p, d), x_sample.reshape(n_s, d)], axis=0)

    qp = _q_perm()
    w_in = prm["w_in"]
    w_in = jnp.concatenate([w_in[:, :SHIFT_W], w_in[:, SHIFT_W:SHIFT_W + B_WIDTH][:, qp],
                            w_in[:, SHIFT_W + B_WIDTH:]], axis=1).astype(BF16)
    pa, q, k, v = _inproj_call(x, prm["norm_mix_g"][None], w_in)

    zero_blk = jnp.zeros((LORA_W // 2, A_WIDTH), F32)
    wlora = jnp.concatenate([jnp.concatenate([prm["w_decay_up"], zero_blk], axis=1),
                             jnp.concatenate([zero_blk, prm["w_iclr_up"]], axis=1)], axis=0).astype(BF16)
    rp = dict(mu=prm["mu_shift"][None], w0=prm["w_decay0"][None], a0=prm["w_iclr0"][None], wlora=wlora,
              wgate=prm["w_gate_up"].astype(BF16), kk=prm["k_k"][None], ka=prm["k_a"][None],
              rk=prm["r_k"].reshape(1, A_WIDTH), lng=prm["lnx_g"][None], lnb=prm["lnx_b"][None])

    oa_p, sbd_p = _rwkv_call(pa, jnp.zeros((bp, 1, SHIFT_W), F32), jnp.zeros((bp, N_PAIRS, LANE, LANE), F32), rp,
                             n_seq=bp, n_chunks=tp // chunk, chunk=chunk, t_real=chunk)
    pad_rows = lambda a: jnp.pad(a[n_p:].reshape(bs, ts, -1), ((0, 0), (0, ts_pad - ts), (0, 0))).reshape(
        bs * ts_pad, -1)
    oa_s, sbd_s = _rwkv_call(pad_rows(pa), state_shift[:, None, :], _state_to_pairs(state_rwkv), rp,
                             n_seq=bs, n_chunks=1, chunk=ts_pad, t_real=ts)
    oa_s = oa_s.reshape(bs, ts_pad, A_WIDTH)[:, :ts].reshape(n_s, A_WIDTH)

    gn = prm["attn_norm_g"][qp][None]
    nb = tp // WINDOW
    k_p3 = k.reshape(-1, WINDOW, B_KV_WIDTH)
    v_p3 = v.reshape(-1, WINDOW, B_KV_WIDTH)
    ob_p = _swa_call(prm["attn_sink"], q, k_p3, v_p3, k, v, gn, n_seq=bp, n_blk=nb, tq=WINDOW,
                     first_has_prev=False, prev_map=lambda b, i: (b * nb + jnp.maximum(i - 1, 0), 0, 0))
    ob_s = _swa_call(prm["attn_sink"], pad_rows(q), cache_win_k.reshape(bs, WINDOW, B_KV_WIDTH),
                     cache_win_v.reshape(bs, WINDOW, B_KV_WIDTH), pad_rows(k), pad_rows(v), gn,
                     n_seq=bs, n_blk=1, tq=ts_pad, first_has_prev=True, prev_map=lambda b, i: (b, 0, 0))
    ob_s = ob_s.reshape(bs, ts_pad, B_WIDTH)[:, :ts].reshape(n_s, B_WIDTH)

    oa = jnp.concatenate([oa_p, oa_s], axis=0)
    ob = jnp.concatenate([ob_p[:n_p], ob_s], axis=0)
    w_out = prm["w_out"]
    wr = jnp.zeros((d, ROUTE_W), F32)
    wr = wr.at[:, :N_GROUPS].set(prm["w_route_group"]).at[:, N_GROUPS:N_GROUPS + N_EXPERTS].set(
        prm["w_route_expert"])
    x1, h2, logits = _outproj_call(x, oa, ob, w_out[:A_WIDTH].astype(BF16), w_out[A_WIDTH:][qp].astype(BF16),
                                   prm["norm_ffn_g"][None], wr)

    dest, block_e, n_used, src_tok, sw = _route(logits, prm["b_route_group"], prm["b_route_expert"], moe_block)
    xbuf = h2[src_tok]
    ybuf = _expert_call(block_e, n_used, xbuf, sw[:, None], prm["w_exp_gate"].astype(BF16),
                        prm["w_exp_up"].astype(BF16), prm["w_exp_down"].astype(BF16), tb=moe_block)
    yg = ybuf[dest].reshape(n, TOP_K_INNER * d)
    y = _final_call(x1, yg, norm_final_g[None])

    kv4 = lambda a: a.reshape(a.shape[0], a.shape[1], B_KV_HEADS, HEAD_DIM)
    k_p = k[:n_p].reshape(bp, tp, B_KV_WIDTH)[:, -WINDOW:]
    v_p = v[:n_p].reshape(bp, tp, B_KV_WIDTH)[:, -WINDOW:]
    k_s = jnp.concatenate([cache_win_k.reshape(bs, -1, B_KV_WIDTH), k[n_p:].reshape(bs, ts, B_KV_WIDTH)],
                          axis=1)[:, -cache_win_k.shape[1]:]
    v_s = jnp.concatenate([cache_win_v.reshape(bs, -1, B_KV_WIDTH), v[n_p:].reshape(bs, ts, B_KV_WIDTH)],
                          axis=1)[:, -cache_win_v.shape[1]:]
    return (y[:n_p].reshape(bp, tp, d), y[n_p:].reshape(bs, ts, d),
            _pairs_to_state(sbd_p), pa[:n_p].reshape(bp, tp, SHIFT_W)[:, -1], kv4(k_p), kv4(v_p),
            _pairs_to_state(sbd_s), pa[n_p:].reshape(bs, ts, SHIFT_W)[:, -1], kv4(k_s), kv4(v_s))


def kernel(x_prompt, x_sample, state_rwkv, state_shift, cache_win_k, cache_win_v, norm_mix_g, w_in, mu_shift, w_decay0, w_decay_up, w_iclr0, w_iclr_up, w_gate_up, k_k, k_a, r_k, lnx_g, lnx_b, attn_sink, attn_norm_g, w_out, norm_ffn_g, w_route_group, b_route_group, w_route_expert, b_route_expert, w_exp_gate, w_exp_up, w_exp_down, norm_final_g):
    assert norm_mix_g.shape[0] == 1, "single-layer trunk"
    prm = dict(norm_mix_g=norm_mix_g[0], w_in=w_in[0], mu_shift=mu_shift[0], w_decay0=w_decay0[0],
               w_decay_up=w_decay_up[0], w_iclr0=w_iclr0[0], w_iclr_up=w_iclr_up[0], w_gate_up=w_gate_up[0],
               k_k=k_k[0], k_a=k_a[0], r_k=r_k[0], lnx_g=lnx_g[0], lnx_b=lnx_b[0], attn_sink=attn_sink[0],
               attn_norm_g=attn_norm_g[0], w_out=w_out[0], norm_ffn_g=norm_ffn_g[0],
               w_route_group=w_route_group[0], b_route_group=b_route_group[0],
               w_route_expert=w_route_expert[0], b_route_expert=b_route_expert[0],
               w_exp_gate=w_exp_gate[0], w_exp_up=w_exp_up[0], w_exp_down=w_exp_down[0])
    outs = _layer(x_prompt, x_sample, state_rwkv[0], state_shift[0], cache_win_k[0], cache_win_v[0], prm,
                  norm_final_g, chunk=64, moe_block=256)
    y_p, y_s, s_p, sh_p, kp, vp, s_s, sh_s, ks, vs = outs
    return (y_p, y_s, s_p[None], sh_p[None], kp[None], vp[None], s_s[None], sh_s[None], ks[None], vs[None])
```

```python
import functools

import jax
import jax.numpy as jnp
from jax import lax
from jax.experimental import pallas as pl
from jax.experimental.pallas import tpu as pltpu

F32 = jnp.float32
BF16 = jnp.bfloat16

HEAD_DIM = 64
A_HEADS = 8
A_WIDTH = A_HEADS * HEAD_DIM
B_HEADS = 8
B_KV_HEADS = 2
B_GROUP = B_HEADS // B_KV_HEADS
B_WIDTH = B_HEADS * HEAD_DIM
B_KV_WIDTH = B_KV_HEADS * HEAD_DIM
DECAY_LORA = 64
ICLR_LORA = 64
GATE_LORA = 128
LORA_W = DECAY_LORA + ICLR_LORA
SHIFT_W = 3 * A_WIDTH + LORA_W + GATE_LORA
IN_W = SHIFT_W + B_WIDTH + 2 * B_KV_WIDTH
WINDOW = 128
N_GROUPS = 4
EXPERTS_PER_GROUP = 4
N_EXPERTS = N_GROUPS * EXPERTS_PER_GROUP
TOP_K_INNER = 2
EXPERT_FF = 512
RMS_EPS = 1e-6
LNX_EPS = 64e-5
DECAY_OFFSET = 0.5

LANE = 128
PAIR = LANE // HEAD_DIM
N_PAIRS = A_HEADS // PAIR
ROUTE_W = LANE
MASK_NEG = -1e30
VMEM_LIMIT = 48 * 1024 * 1024

HEAD_SHIFT = HEAD_DIM.bit_length() - 1
HI = lax.Precision.HIGHEST


def _row_tile(n, cap):
    t = cap
    while n % t:
        t //= 2
    return t


def _dot(a, b, precision=None):
    return jnp.dot(a, b, preferred_element_type=F32, precision=precision)


def _dot_nt(a, b, precision=None):
    return lax.dot_general(a, b, (((1,), (1,)), ((), ())), preferred_element_type=F32, precision=precision)


def _dot_tn(a, b, precision=None):
    return lax.dot_general(a, b, (((0,), (0,)), ((), ())), preferred_element_type=F32, precision=precision)


def _sigmoid(x):
    return 1.0 / (1.0 + jnp.exp(-x))


def _hi_lo(x, axis):
    hi = x.astype(BF16)
    lo = (x - hi.astype(F32)).astype(BF16)
    return jnp.concatenate([hi, lo], axis=axis)


def _inproj_body(x_ref, g_ref, w_ref, pa_ref, q_ref, k_ref, v_ref):
    x = x_ref[...]
    h = x * lax.rsqrt(jnp.mean(x * x, axis=-1, keepdims=True) + RMS_EPS) * g_ref[...]
    p = _dot(h.astype(BF16), w_ref[...])
    pa_ref[...] = p[:, :SHIFT_W]
    q_ref[...] = p[:, SHIFT_W:SHIFT_W + B_WIDTH]
    k_ref[...] = p[:, SHIFT_W + B_WIDTH:SHIFT_W + B_WIDTH + B_KV_WIDTH]
    v_ref[...] = p[:, SHIFT_W + B_WIDTH + B_KV_WIDTH:]


def _inproj_call(x, g, w_bf16):
    n, d = x.shape
    tm = _row_tile(n, 512)
    row = lambda i: (i, 0)
    fixed = lambda i: (0, 0)
    return pl.pallas_call(
        _inproj_body,
        grid=(n // tm,),
        in_specs=[pl.BlockSpec((tm, d), row), pl.BlockSpec((1, d), fixed), pl.BlockSpec((d, IN_W), fixed)],
        out_specs=[pl.BlockSpec((tm, SHIFT_W), row), pl.BlockSpec((tm, B_WIDTH), row),
                   pl.BlockSpec((tm, B_KV_WIDTH), row), pl.BlockSpec((tm, B_KV_WIDTH), row)],
        out_shape=[jax.ShapeDtypeStruct((n, SHIFT_W), F32), jax.ShapeDtypeStruct((n, B_WIDTH), F32),
                   jax.ShapeDtypeStruct((n, B_KV_WIDTH), F32), jax.ShapeDtypeStruct((n, B_KV_WIDTH), F32)],
        compiler_params=pltpu.CompilerParams(dimension_semantics=("arbitrary",), vmem_limit_bytes=VMEM_LIMIT),
        name="inproj",
    )(x, g, w_bf16)


def _rwkv_body(pa_ref, shift0_ref, s0_ref, mu_ref, w0_ref, a0_ref, wlora_ref, wgate_ref, kk_ref, ka_ref,
               rk_ref, lng_ref, lnb_ref, oa_ref, sout_ref, s_scr, prev_scr, *, chunk, n_sub, t_real):
    C = chunk
    rows = n_sub * C
    c = pl.program_id(1)

    @pl.when(c == 0)
    def _():
        s_scr[...] = s0_ref[0]
        prev_scr[...] = shift0_ref[0]

    pa = pa_ref[...]
    row = lax.broadcasted_iota(jnp.int32, (rows, 1), 0)
    row_in_chunk = row & (C - 1)
    pa_prev = jnp.where(row == 0, prev_scr[...], pltpu.roll(pa, 1, axis=0))
    prev_scr[...] = pa[rows - C + t_real - 1:rows - C + t_real]
    xm = pa + mu_ref[...] * (pa_prev - pa)

    r = xm[:, :A_WIDTH]
    k = xm[:, A_WIDTH:2 * A_WIDTH]
    v = xm[:, 2 * A_WIDTH:3 * A_WIDTH]
    lora_in = xm[:, 3 * A_WIDTH:3 * A_WIDTH + LORA_W]
    gd = xm[:, 3 * A_WIDTH + LORA_W:]

    lane = lax.broadcasted_iota(jnp.int32, (1, LANE), 1)
    lo_half = lane < HEAD_DIM
    z = jnp.where(lo_half, jnp.tanh(lora_in), lora_in)
    lw = _dot(z.astype(BF16), wlora_ref[...])
    dec_pre = w0_ref[...] + lw[:, :A_WIDTH]
    a = _sigmoid(a0_ref[...] + lw[:, A_WIDTH:])
    neg = -dec_pre
    softplus = jnp.maximum(neg, 0.0) + jnp.log(1.0 + jnp.exp(-jnp.abs(neg)))
    logdec = -jnp.exp(-softplus - DECAY_OFFSET)
    g = _dot(_sigmoid(gd).astype(BF16), wgate_ref[...])

    ri = lax.broadcasted_iota(jnp.int32, (LANE, LANE), 0)
    ci = lax.broadcasted_iota(jnp.int32, (LANE, LANE), 1)
    seg = jnp.where((ri >> HEAD_SHIFT) == (ci >> HEAD_SHIFT), 1.0, 0.0).astype(BF16)
    seg2 = jnp.concatenate([seg, seg], axis=0)

    def headsum(x):
        return jnp.concatenate([_dot(_hi_lo(x[:, j * LANE:(j + 1) * LANE], 1), seg2) for j in range(N_PAIRS)],
                               axis=1)

    kk = k * kk_ref[...]
    kk = kk * lax.rsqrt(jnp.maximum(headsum(kk * kk), 1e-24))
    k = k * (1.0 + (a - 1.0) * ka_ref[...])
    bonus = headsum(r * k * rk_ref[...]) * v

    if t_real < C:
        valid = row_in_chunk < t_real
        logdec = jnp.where(valid, logdec, 0.0)
        kk = jnp.where(valid, kk, 0.0)
        k = jnp.where(valid, k, 0.0)
        v = jnp.where(valid, v, 0.0)

    cum = logdec
    shift = 1
    while shift < C:
        cum = cum + jnp.where(row_in_chunk >= shift, pltpu.roll(cum, shift, axis=0), 0.0)
        shift *= 2
    ends = [cum[(s + 1) * C - 1:(s + 1) * C] for s in range(n_sub)]
    cum_end = jnp.concatenate([jnp.broadcast_to(e, (C, A_WIDTH)) for e in ends], axis=0) if n_sub > 1 else ends[0]
    w_incl = jnp.exp(cum)
    w_prev = jnp.exp(cum - logdec)
    w_inv = jnp.exp(-cum)
    w_end = jnp.exp(cum_end - cum)
    w_chunk = [jnp.exp(e) for e in ends]
    kka = kk * a
    terms = dict(A=-kk * w_prev, R=r * w_incl, B=kka * w_inv, K=k * w_inv, V=v, Be=kka * w_end, Ke=k * w_end)

    C2 = PAIR * C
    sr = lax.broadcasted_iota(jnp.int32, (C2, C2), 0)
    sc = lax.broadcasted_iota(jnp.int32, (C2, C2), 1)
    c_shift = C.bit_length() - 1
    same_head = (sr >> c_shift) == (sc >> c_shift)
    strict = same_head & ((sc & (C - 1)) < (sr & (C - 1)))
    incl = same_head & ((sc & (C - 1)) <= (sr & (C - 1)))
    ident = jnp.where(sr == sc, 1.0, 0.0).astype(F32)
    n_levels = max(1, (C - 1).bit_length())
    bf = lambda x: x.astype(BF16)

    chains = [(s, j) for s in range(n_sub) for j in range(N_PAIRS)]

    def st(name, s, j):
        x = terms[name][s * C:(s + 1) * C, j * LANE:(j + 1) * LANE]
        return jnp.concatenate([jnp.where(lo_half, x, 0.0), jnp.where(lo_half, 0.0, x)], axis=0)

    r_st = {ch: st("R", *ch) for ch in chains}
    stk = {ch: {nm: bf(st(nm, *ch)) for nm in ("A", "B", "K", "V", "Be", "Ke")} for ch in chains}
    m_ab, aak, arb, ark = {}, {}, {}, {}
    for ch in chains:
        t, rb = stk[ch], bf(r_st[ch])
        if C2 % LANE == 0:
            m1 = _dot_nt(jnp.concatenate([t["A"], rb], axis=0), jnp.concatenate([t["B"], t["K"]], axis=0))
            ab, ak, rbm, rk = m1[:C2, :C2], m1[:C2, C2:], m1[C2:, :C2], m1[C2:, C2:]
        else:
            ab, ak = _dot_nt(t["A"], t["B"]), _dot_nt(t["A"], t["K"])
            rbm, rk = _dot_nt(rb, t["B"]), _dot_nt(rb, t["K"])
        m_ab[ch] = jnp.where(strict, ab, 0.0)
        aak[ch] = bf(jnp.where(strict, ak, 0.0))
        arb[ch] = bf(jnp.where(incl, rbm, 0.0))
        ark[ch] = bf(jnp.where(incl, rk, 0.0))

    nn = dict(m_ab)
    tinv = {ch: ident + m_ab[ch] for ch in chains}
    for _ in range(1, n_levels):
        for ch in chains:
            nb = bf(nn[ch])
            nn[ch] = _dot(nb, nb)
        for ch in chains:
            tinv[ch] = tinv[ch] + _dot(bf(nn[ch]), bf(tinv[ch]))
    akv = {ch: _dot(aak[ch], stk[ch]["V"]) for ch in chains}
    xb = {ch: bf(_dot(bf(tinv[ch]), jnp.concatenate([stk[ch]["A"], bf(akv[ch])], axis=1))) for ch in chains}
    zed = {ch: _dot(arb[ch], xb[ch]) for ch in chains}
    rhat = {ch: bf(r_st[ch] + zed[ch][:, :LANE]) for ch in chains}
    y0 = {ch: _dot(ark[ch], stk[ch]["V"]) + zed[ch][:, LANE:] for ch in chains}
    p_mat = {ch: bf(_dot_tn(stk[ch]["Be"], xb[ch][:, :LANE])) for ch in chains}
    q_t = {ch: _dot_tn(jnp.concatenate([xb[ch][:, LANE:], stk[ch]["V"]], axis=0),
                       jnp.concatenate([stk[ch]["Be"], stk[ch]["Ke"]], axis=0)) for ch in chains}

    state = [s_scr[j] for j in range(N_PAIRS)]
    y_rows = []
    for s in range(n_sub):
        ys = []
        for j in range(N_PAIRS):
            ch = (s, j)
            s_cat = _hi_lo(state[j], 0)
            yy = _dot_nt(rhat[ch], s_cat)
            y_st = yy[:, :LANE] + yy[:, LANE:] + y0[ch]
            sp = _dot_nt(s_cat, p_mat[ch])
            state[j] = state[j] * w_chunk[s][:, j * LANE:(j + 1) * LANE] + sp[:LANE] + sp[LANE:] + q_t[ch]
            ys.append(y_st[:C] + y_st[C:])
        y_rows.append(jnp.concatenate(ys, axis=1))
    for j in range(N_PAIRS):
        s_scr[j] = state[j]

    y = jnp.concatenate(y_rows, axis=0) if n_sub > 1 else y_rows[0]
    inv_n = 1.0 / HEAD_DIM
    mean = headsum(y) * inv_n
    d = y - mean
    var = headsum(d * d) * inv_n
    yn = d * lax.rsqrt(var + LNX_EPS) * lng_ref[...] + lnb_ref[...]
    oa_ref[...] = (yn + bonus) * g
    sout_ref[0] = s_scr[...]


def _rwkv_call(pa2d, shift0, s0_bd, prm, *, n_seq, n_chunks, chunk, n_sub, t_real):
    n_steps = n_chunks // n_sub
    row = lambda b, c: (b * n_steps + c, 0)
    fixed = lambda b, c: (0, 0)
    vec = lambda w: pl.BlockSpec((1, w), fixed)
    body = functools.partial(_rwkv_body, chunk=chunk, n_sub=n_sub, t_real=t_real)
    n_rows = n_seq * n_chunks * chunk
    return pl.pallas_call(
        body,
        grid=(n_seq, n_steps),
        in_specs=[pl.BlockSpec((n_sub * chunk, SHIFT_W), row),
                  pl.BlockSpec((1, 1, SHIFT_W), lambda b, c: (b, 0, 0)),
                  pl.BlockSpec((1, N_PAIRS, LANE, LANE), lambda b, c: (b, 0, 0, 0)),
                  vec(SHIFT_W), vec(A_WIDTH), vec(A_WIDTH),
                  pl.BlockSpec((LORA_W, 2 * A_WIDTH), fixed), pl.BlockSpec((GATE_LORA, A_WIDTH), fixed),
                  vec(A_WIDTH), vec(A_WIDTH), vec(A_WIDTH), vec(A_WIDTH), vec(A_WIDTH)],
        out_specs=[pl.BlockSpec((n_sub * chunk, A_WIDTH), row),
                   pl.BlockSpec((1, N_PAIRS, LANE, LANE), lambda b, c: (b, 0, 0, 0))],
        out_shape=[jax.ShapeDtypeStruct((n_rows, A_WIDTH), F32),
                   jax.ShapeDtypeStruct((n_seq, N_PAIRS, LANE, LANE), F32)],
        scratch_shapes=[pltpu.VMEM((N_PAIRS, LANE, LANE), F32), pltpu.VMEM((1, SHIFT_W), F32)],
        compiler_params=pltpu.CompilerParams(dimension_semantics=("arbitrary", "arbitrary"),
                                             vmem_limit_bytes=VMEM_LIMIT),
        name="rwkv",
    )(pa2d, shift0, s0_bd, prm["mu"], prm["w0"], prm["a0"], prm["wlora"], prm["wgate"], prm["kk"], prm["ka"],
      prm["rk"], prm["lng"], prm["lnb"])


def _state_to_pairs(state):
    b = state.shape[0]
    st = state.reshape(b, N_PAIRS, PAIR, HEAD_DIM, HEAD_DIM)
    eye = jnp.eye(PAIR, dtype=state.dtype)
    bd = st[:, :, :, :, None, :] * eye[None, None, :, None, :, None]
    return bd.reshape(b, N_PAIRS, LANE, LANE)


def _pairs_to_state(bd):
    b = bd.shape[0]
    x = bd.reshape(b, N_PAIRS, PAIR, HEAD_DIM, PAIR, HEAD_DIM)
    st = jnp.stack([x[:, :, i, :, i, :] for i in range(PAIR)], axis=2)
    return st.reshape(b, A_HEADS, HEAD_DIM, HEAD_DIM)


def _swa_body(sink_ref, q_ref, kp_ref, vp_ref, kc_ref, vc_ref, gn_ref, o_ref, *, tq, first_has_prev):
    nblk = pl.program_id(1)
    q = q_ref[...]
    kp = kp_ref[0].astype(BF16)
    vp = vp_ref[0]
    kc = kc_ref[...].astype(BF16)
    vc = vc_ref[...]
    rows = B_GROUP * tq
    scale = HEAD_DIM ** -0.5
    lane = lax.broadcasted_iota(jnp.int32, (1, LANE), 1)
    grp = lax.broadcasted_iota(jnp.int32, (rows, 1), 0) >> (tq.bit_length() - 1)
    t = lax.broadcasted_iota(jnp.int32, (rows, 1), 0) & (tq - 1)
    dist_p = t + WINDOW - lax.broadcasted_iota(jnp.int32, (1, WINDOW), 1)
    dist_c = t - lax.broadcasted_iota(jnp.int32, (1, tq), 1)
    mask_p = (dist_p >= 0) & (dist_p < WINDOW)
    if not first_has_prev:
        mask_p = mask_p & (nblk > 0)
    mask_c = (dist_c >= 0) & (dist_c < WINDOW)
    dist_p = dist_p.astype(F32)
    dist_c = dist_c.astype(F32)

    out = None
    for kv in range(B_KV_HEADS):
        mk = (lane < HEAD_DIM) if kv == 0 else (lane >= HEAD_DIM)
        q_st = jnp.concatenate([jnp.where(mk, q[:, g * LANE:(g + 1) * LANE], 0.0) for g in range(B_GROUP)],
                               axis=0).astype(BF16)
        slope = jnp.zeros((rows, 1), F32)
        sink = jnp.zeros((rows, 1), F32)
        for g in range(B_GROUP):
            head = kv * B_GROUP + g
            slope = jnp.where(grp == g, 2.0 ** -(head + 1), slope)
            sink = jnp.where(grp == g, sink_ref[head], sink)
        s_p = jnp.where(mask_p, _dot_nt(q_st, kp) * scale - slope * dist_p, MASK_NEG)
        s_c = jnp.where(mask_c, _dot_nt(q_st, kc) * scale - slope * dist_c, MASK_NEG)
        m = jnp.maximum(jnp.maximum(jnp.max(s_p, axis=-1, keepdims=True), jnp.max(s_c, axis=-1, keepdims=True)),
                        sink)
        e_p = jnp.exp(s_p - m)
        e_c = jnp.exp(s_c - m)
        denom = jnp.sum(e_p, axis=-1, keepdims=True) + jnp.sum(e_c, axis=-1, keepdims=True) + jnp.exp(sink - m)
        inv = 1.0 / denom
        o_kv = (_dot((e_p * inv).astype(BF16), jnp.where(mk, vp, 0.0).astype(BF16))
                + _dot((e_c * inv).astype(BF16), jnp.where(mk, vc, 0.0).astype(BF16)))
        out = o_kv if out is None else out + o_kv

    tiles = [out[g * tq:(g + 1) * tq] for g in range(B_GROUP)]
    ssq = sum(jnp.sum(x * x, axis=-1, keepdims=True) for x in tiles)
    inv_rms = lax.rsqrt(ssq * (1.0 / B_WIDTH) + RMS_EPS)
    o_ref[...] = jnp.concatenate(tiles, axis=1) * inv_rms * gn_ref[...]


def _swa_call(sink, q2d, kprev, vprev, kcur2d, vcur2d, gn, *, n_seq, n_blk, tq, first_has_prev, prev_map):
    row = lambda b, n: (b * n_blk + n, 0)
    body = functools.partial(_swa_body, tq=tq, first_has_prev=first_has_prev)
    return pl.pallas_call(
        body,
        grid=(n_seq, n_blk),
        in_specs=[pl.BlockSpec(memory_space=pltpu.SMEM),
                  pl.BlockSpec((tq, B_WIDTH), row),
                  pl.BlockSpec((1, WINDOW, B_KV_WIDTH), prev_map),
                  pl.BlockSpec((1, WINDOW, B_KV_WIDTH), prev_map),
                  pl.BlockSpec((tq, B_KV_WIDTH), row),
                  pl.BlockSpec((tq, B_KV_WIDTH), row),
                  pl.BlockSpec((1, B_WIDTH), lambda b, n: (0, 0))],
        out_specs=pl.BlockSpec((tq, B_WIDTH), row),
        out_shape=jax.ShapeDtypeStruct((n_seq * n_blk * tq, B_WIDTH), F32),
        compiler_params=pltpu.CompilerParams(dimension_semantics=("arbitrary", "arbitrary"),
                                             vmem_limit_bytes=VMEM_LIMIT),
        name="swa",
    )(sink, q2d, kprev, vprev, kcur2d, vcur2d, gn)


def _outproj_body(x_ref, oa_ref, ob_ref, wa_ref, wb_ref, g_ref, wr_ref, x1_ref, h2_ref, lg_ref):
    x1 = x_ref[...] + _dot(oa_ref[...].astype(BF16), wa_ref[...]) + _dot(ob_ref[...].astype(BF16), wb_ref[...])
    x1_ref[...] = x1
    h2 = x1 * lax.rsqrt(jnp.mean(x1 * x1, axis=-1, keepdims=True) + RMS_EPS) * g_ref[...]
    h2_ref[...] = h2.astype(BF16)
    lg_ref[...] = _dot(h2, wr_ref[...], HI)


def _outproj_call(x, oa, ob, wa, wb, g, wr):
    n, d = x.shape
    tm = _row_tile(n, 512)
    row = lambda i: (i, 0)
    fixed = lambda i: (0, 0)
    return pl.pallas_call(
        _outproj_body,
        grid=(n // tm,),
        in_specs=[pl.BlockSpec((tm, d), row), pl.BlockSpec((tm, A_WIDTH), row), pl.BlockSpec((tm, B_WIDTH), row),
                  pl.BlockSpec((A_WIDTH, d), fixed), pl.BlockSpec((B_WIDTH, d), fixed), pl.BlockSpec((1, d), fixed),
                  pl.BlockSpec((d, ROUTE_W), fixed)],
        out_specs=[pl.BlockSpec((tm, d), row), pl.BlockSpec((tm, d), row), pl.BlockSpec((tm, ROUTE_W), row)],
        out_shape=[jax.ShapeDtypeStruct((n, d), F32), jax.ShapeDtypeStruct((n, d), BF16),
                   jax.ShapeDtypeStruct((n, ROUTE_W), F32)],
        compiler_params=pltpu.CompilerParams(dimension_semantics=("arbitrary",), vmem_limit_bytes=VMEM_LIMIT),
        name="outproj",
    )(x, oa, ob, wa, wb, g, wr)


def _expert_body(be_ref, nused_ref, x_ref, sw_ref, wg_ref, wu_ref, wd_ref, y_ref):
    i = pl.program_id(0)

    @pl.when(i < nused_ref[0])
    def _():
        x = x_ref[...]
        gate = _dot(x, wg_ref[0])
        up = _dot(x, wu_ref[0])
        mid = (gate * _sigmoid(gate) * up).astype(BF16)
        y_ref[...] = _dot(mid, wd_ref[0]) * sw_ref[...]

    @pl.when(i >= nused_ref[0])
    def _():
        y_ref[...] = jnp.zeros_like(y_ref)


def _expert_call(block_e, n_used, xbuf, sw, wg, wu, wd, *, tb):
    p, d = xbuf.shape
    ff = wg.shape[-1]
    grid_spec = pltpu.PrefetchScalarGridSpec(
        num_scalar_prefetch=2,
        grid=(p // tb,),
        in_specs=[pl.BlockSpec((tb, d), lambda i, be, nu: (i, 0)),
                  pl.BlockSpec((tb, 1), lambda i, be, nu: (i, 0)),
                  pl.BlockSpec((1, d, ff), lambda i, be, nu: (be[i], 0, 0)),
                  pl.BlockSpec((1, d, ff), lambda i, be, nu: (be[i], 0, 0)),
                  pl.BlockSpec((1, ff, d), lambda i, be, nu: (be[i], 0, 0))],
        out_specs=pl.BlockSpec((tb, d), lambda i, be, nu: (i, 0)),
    )
    return pl.pallas_call(
        _expert_body,
        grid_spec=grid_spec,
        out_shape=jax.ShapeDtypeStruct((p, d), F32),
        compiler_params=pltpu.CompilerParams(dimension_semantics=("arbitrary",), vmem_limit_bytes=VMEM_LIMIT),
        name="experts",
    )(block_e, n_used, xbuf, sw, wg, wu, wd)


def _final_body(x1_ref, y0_ref, y1_ref, g_ref, o_ref):
    x = x1_ref[...] + (y0_ref[...] + y1_ref[...])
    o_ref[...] = x * lax.rsqrt(jnp.mean(x * x, axis=-1, keepdims=True) + RMS_EPS) * g_ref[...]


def _final_call(x1, yg, g):
    n, d = x1.shape
    tm = _row_tile(n, 512)
    return pl.pallas_call(
        _final_body,
        grid=(n // tm,),
        in_specs=[pl.BlockSpec((tm, d), lambda i: (i, 0)), pl.BlockSpec((tm, d), lambda i: (i, 0)),
                  pl.BlockSpec((tm, d), lambda i: (i, 1)), pl.BlockSpec((1, d), lambda i: (0, 0))],
        out_specs=pl.BlockSpec((tm, d), lambda i: (i, 0)),
        out_shape=jax.ShapeDtypeStruct((n, d), F32),
        compiler_params=pltpu.CompilerParams(dimension_semantics=("arbitrary",), vmem_limit_bytes=VMEM_LIMIT),
        name="final",
    )(x1, yg, yg, g)


def _route(logits, b_group, b_expert, tb):
    n = logits.shape[0]
    pg = jax.nn.softmax(logits[:, :N_GROUPS] + b_group, axis=-1)
    g_idx = jnp.argmax(pg, axis=-1)
    g_w = jnp.take_along_axis(pg, g_idx[:, None], axis=-1)
    le = (logits[:, N_GROUPS:N_GROUPS + N_EXPERTS] + b_expert).reshape(n, N_GROUPS, EXPERTS_PER_GROUP)
    le_g = jnp.take_along_axis(le, g_idx[:, None, None], axis=1)[:, 0]
    top_p, top_i = lax.top_k(jax.nn.softmax(le_g, axis=-1), TOP_K_INNER)
    wts = g_w * top_p / jnp.sum(top_p, axis=-1, keepdims=True)
    e_idx = (g_idx[:, None] * EXPERTS_PER_GROUP + top_i).astype(jnp.int32)

    n_assign = n * TOP_K_INNER
    flat_e = e_idx.reshape(n_assign)
    onehot = (flat_e[:, None] == jnp.arange(N_EXPERTS, dtype=jnp.int32)[None, :]).astype(jnp.int32)
    csum = jnp.cumsum(onehot, axis=0)
    rank = jnp.take_along_axis(csum, flat_e[:, None], axis=1)[:, 0] - 1
    counts = csum[-1]
    padded = (counts + tb - 1) // tb * tb
    pend = jnp.cumsum(padded)
    pstarts = pend - padded
    dest = (pstarts[flat_e] + rank).astype(jnp.int32)
    n_blocks = -(-n_assign // tb) + N_EXPERTS
    block_e = jnp.clip(jnp.searchsorted(pend, jnp.arange(n_blocks, dtype=jnp.int32) * tb, side="right"),
                       0, N_EXPERTS - 1).astype(jnp.int32)
    n_used = (pend[-1:] // tb).astype(jnp.int32)
    tok = jnp.arange(n_assign, dtype=jnp.int32) // TOP_K_INNER
    src_tok = jnp.zeros((n_blocks * tb,), jnp.int32).at[dest].set(tok)
    sw = jnp.zeros((n_blocks * tb,), F32).at[dest].set(wts.reshape(n_assign))
    return dest, block_e, n_used, src_tok, sw


def _q_perm():
    cols = []
    for g in range(B_GROUP):
        for kv in range(B_KV_HEADS):
            h = kv * B_GROUP + g
            cols.extend(range(h * HEAD_DIM, (h + 1) * HEAD_DIM))
    return jnp.array(cols, dtype=jnp.int32)


def _layer(x_prompt, x_sample, state_rwkv, state_shift, cache_win_k, cache_win_v, prm, norm_final_g, *,
           chunk, n_sub, moe_block):
    bp, tp, d = x_prompt.shape
    bs, ts, _ = x_sample.shape
    n_p, n_s = bp * tp, bs * ts
    n = n_p + n_s
    ts_pad = 8
    x = jnp.concatenate([x_prompt.reshape(n_p, d), x_sample.reshape(n_s, d)], axis=0)

    qp = _q_perm()
    w_in = prm["w_in"]
    w_in = jnp.concatenate([w_in[:, :SHIFT_W], w_in[:, SHIFT_W:SHIFT_W + B_WIDTH][:, qp],
                            w_in[:, SHIFT_W + B_WIDTH:]], axis=1).astype(BF16)
    pa, q, k, v = _inproj_call(x, prm["norm_mix_g"][None], w_in)

    zero_blk = jnp.zeros((LORA_W // 2, A_WIDTH), F32)
    wlora = jnp.concatenate([jnp.concatenate([prm["w_decay_up"], zero_blk], axis=1),
                             jnp.concatenate([zero_blk, prm["w_iclr_up"]], axis=1)], axis=0).astype(BF16)
    rp = dict(mu=prm["mu_shift"][None], w0=prm["w_decay0"][None], a0=prm["w_iclr0"][None], wlora=wlora,
              wgate=prm["w_gate_up"].astype(BF16), kk=prm["k_k"][None], ka=prm["k_a"][None],
              rk=prm["r_k"].reshape(1, A_WIDTH), lng=prm["lnx_g"][None], lnb=prm["lnx_b"][None])

    oa_p, sbd_p = _rwkv_call(pa, jnp.zeros((bp, 1, SHIFT_W), F32), jnp.zeros((bp, N_PAIRS, LANE, LANE), F32), rp,
                             n_seq=bp, n_chunks=tp // chunk, chunk=chunk, n_sub=n_sub, t_real=chunk)
    pad_rows = lambda a: jnp.pad(a[n_p:].reshape(bs, ts, -1), ((0, 0), (0, ts_pad - ts), (0, 0))).reshape(
        bs * ts_pad, -1)
    oa_s, sbd_s = _rwkv_call(pad_rows(pa), state_shift[:, None, :], _state_to_pairs(state_rwkv), rp,
                             n_seq=bs, n_chunks=1, chunk=ts_pad, n_sub=1, t_real=ts)
    oa_s = oa_s.reshape(bs, ts_pad, A_WIDTH)[:, :ts].reshape(n_s, A_WIDTH)

    gn = prm["attn_norm_g"][qp][None]
    nb = tp // WINDOW
    k_p3 = k.reshape(-1, WINDOW, B_KV_WIDTH)
    v_p3 = v.reshape(-1, WINDOW, B_KV_WIDTH)
    ob_p = _swa_call(prm["attn_sink"], q, k_p3, v_p3, k, v, gn, n_seq=bp, n_blk=nb, tq=WINDOW,
                     first_has_prev=False, prev_map=lambda b, i: (b * nb + jnp.maximum(i - 1, 0), 0, 0))
    ob_s = _swa_call(prm["attn_sink"], pad_rows(q), cache_win_k.reshape(bs, WINDOW, B_KV_WIDTH),
                     cache_win_v.reshape(bs, WINDOW, B_KV_WIDTH), pad_rows(k), pad_rows(v), gn,
                     n_seq=bs, n_blk=1, tq=ts_pad, first_has_prev=True, prev_map=lambda b, i: (b, 0, 0))
    ob_s = ob_s.reshape(bs, ts_pad, B_WIDTH)[:, :ts].reshape(n_s, B_WIDTH)

    oa = jnp.concatenate([oa_p, oa_s], axis=0)
    ob = jnp.concatenate([ob_p[:n_p], ob_s], axis=0)
    w_out = prm["w_out"]
    wr = jnp.zeros((d, ROUTE_W), F32)
    wr = wr.at[:, :N_GROUPS].set(prm["w_route_group"]).at[:, N_GROUPS:N_GROUPS + N_EXPERTS].set(
        prm["w_route_expert"])
    x1, h2, logits = _outproj_call(x, oa, ob, w_out[:A_WIDTH].astype(BF16), w_out[A_WIDTH:][qp].astype(BF16),
                                   prm["norm_ffn_g"][None], wr)

    dest, block_e, n_used, src_tok, sw = _route(logits, prm["b_route_group"], prm["b_route_expert"], moe_block)
    xbuf = h2[src_tok]
    ybuf = _expert_call(block_e, n_used, xbuf, sw[:, None], prm["w_exp_gate"].astype(BF16),
                        prm["w_exp_up"].astype(BF16), prm["w_exp_down"].astype(BF16), tb=moe_block)
    yg = ybuf[dest].reshape(n, TOP_K_INNER * d)
    y = _final_call(x1, yg, norm_final_g[None])

    kv4 = lambda a: a.reshape(a.shape[0], a.shape[1], B_KV_HEADS, HEAD_DIM)
    k_p = k[:n_p].reshape(bp, tp, B_KV_WIDTH)[:, -WINDOW:]
    v_p = v[:n_p].reshape(bp, tp, B_KV_WIDTH)[:, -WINDOW:]
    k_s = jnp.concatenate([cache_win_k.reshape(bs, -1, B_KV_WIDTH), k[n_p:].reshape(bs, ts, B_KV_WIDTH)],
                          axis=1)[:, -cache_win_k.shape[1]:]
    v_s = jnp.concatenate([cache_win_v.reshape(bs, -1, B_KV_WIDTH), v[n_p:].reshape(bs, ts, B_KV_WIDTH)],
                          axis=1)[:, -cache_win_v.shape[1]:]
    return (y[:n_p].reshape(bp, tp, d), y[n_p:].reshape(bs, ts, d),
            _pairs_to_state(sbd_p), pa[:n_p].reshape(bp, tp, SHIFT_W)[:, -1], kv4(k_p), kv4(v_p),
            _pairs_to_state(sbd_s), pa[n_p:].reshape(bs, ts, SHIFT_W)[:, -1], kv4(k_s), kv4(v_s))


def kernel(x_prompt, x_sample, state_rwkv, state_shift, cache_win_k, cache_win_v, norm_mix_g, w_in, mu_shift, w_decay0, w_decay_up, w_iclr0, w_iclr_up, w_gate_up, k_k, k_a, r_k, lnx_g, lnx_b, attn_sink, attn_norm_g, w_out, norm_ffn_g, w_route_group, b_route_group, w_route_expert, b_route_expert, w_exp_gate, w_exp_up, w_exp_down, norm_final_g):
    assert norm_mix_g.shape[0] == 1, "single-layer trunk"
    prm = dict(norm_mix_g=norm_mix_g[0], w_in=w_in[0], mu_shift=mu_shift[0], w_decay0=w_decay0[0],
               w_decay_up=w_decay_up[0], w_iclr0=w_iclr0[0], w_iclr_up=w_iclr_up[0], w_gate_up=w_gate_up[0],
               k_k=k_k[0], k_a=k_a[0], r_k=r_k[0], lnx_g=lnx_g[0], lnx_b=lnx_b[0], attn_sink=attn_sink[0],
               attn_norm_g=attn_norm_g[0], w_out=w_out[0], norm_ffn_g=norm_ffn_g[0],
               w_route_group=w_route_group[0], b_route_group=b_route_group[0],
               w_route_expert=w_route_expert[0], b_route_expert=b_route_expert[0],
               w_exp_gate=w_exp_gate[0], w_exp_up=w_exp_up[0], w_exp_down=w_exp_down[0])
    outs = _layer(x_prompt, x_sample, state_rwkv[0], state_shift[0], cache_win_k[0], cache_win_v[0], prm,
                  norm_final_g, chunk=64, n_sub=2, moe_block=256)
    y_p, y_s, s_p, sh_p, kp, vp, s_s, sh_s, ks, vs = outs
    return (y_p, y_s, s_p[None], sh_p[None], kp[None], vp[None], s_s[None], sh_s[None], ks[None], vs[None])
```

```python
import functools

import jax
import jax.numpy as jnp
from jax import lax
from jax.experimental import pallas as pl
from jax.experimental.pallas import tpu as pltpu

F32 = jnp.float32
BF16 = jnp.bfloat16

HEAD_DIM = 64
A_HEADS = 8
A_WIDTH = A_HEADS * HEAD_DIM
B_HEADS = 8
B_KV_HEADS = 2
B_GROUP = B_HEADS // B_KV_HEADS
B_WIDTH = B_HEADS * HEAD_DIM
B_KV_WIDTH = B_KV_HEADS * HEAD_DIM
DECAY_LORA = 64
ICLR_LORA = 64
GATE_LORA = 128
LORA_W = DECAY_LORA + ICLR_LORA
SHIFT_W = 3 * A_WIDTH + LORA_W + GATE_LORA
IN_W = SHIFT_W + B_WIDTH + 2 * B_KV_WIDTH
WINDOW = 128
N_GROUPS = 4
EXPERTS_PER_GROUP = 4
N_EXPERTS = N_GROUPS * EXPERTS_PER_GROUP
TOP_K_INNER = 2
EXPERT_FF = 512
RMS_EPS = 1e-6
LNX_EPS = 64e-5
DECAY_OFFSET = 0.5

LANE = 128
PAIR = LANE // HEAD_DIM
N_PAIRS = A_HEADS // PAIR
ROUTE_W = LANE
MASK_NEG = -1e30
VMEM_LIMIT = 48 * 1024 * 1024

HEAD_SHIFT = HEAD_DIM.bit_length() - 1
HI = lax.Precision.HIGHEST


def _row_tile(n, cap):
    t = cap
    while n % t:
        t //= 2
    return t


def _dot(a, b, precision=None):
    return jnp.dot(a, b, preferred_element_type=F32, precision=precision)


def _dot_nt(a, b, precision=None):
    return lax.dot_general(a, b, (((1,), (1,)), ((), ())), preferred_element_type=F32, precision=precision)


def _dot_tn(a, b, precision=None):
    return lax.dot_general(a, b, (((0,), (0,)), ((), ())), preferred_element_type=F32, precision=precision)


def _sigmoid(x):
    return 1.0 / (1.0 + jnp.exp(-x))


def _hi_lo(x, axis):
    hi = x.astype(BF16)
    lo = (x - hi.astype(F32)).astype(BF16)
    return jnp.concatenate([hi, lo], axis=axis)


def _two_stream(i, n_p_tiles, run, prompt_refs, sample_refs):
    @pl.when(i < n_p_tiles)
    def _():
        run(*prompt_refs)

    @pl.when(i >= n_p_tiles)
    def _():
        run(*sample_refs)


def _stream_specs(tm, width, n_p_tiles):
    return [pl.BlockSpec((tm, width), lambda i: (jnp.minimum(i, n_p_tiles - 1), 0)),
            pl.BlockSpec((tm, width), lambda i: (0, 0))]


def _inproj_body(xp_ref, xs_ref, g_ref, w_ref, pa_ref, q_ref, k_ref, v_ref, *, n_p_tiles):
    def run(x_ref):
        x = x_ref[...]
        h = x * lax.rsqrt(jnp.mean(x * x, axis=-1, keepdims=True) + RMS_EPS) * g_ref[...]
        p = _dot(h.astype(BF16), w_ref[...])
        pa_ref[...] = p[:, :SHIFT_W]
        q_ref[...] = p[:, SHIFT_W:SHIFT_W + B_WIDTH]
        k_ref[...] = p[:, SHIFT_W + B_WIDTH:SHIFT_W + B_WIDTH + B_KV_WIDTH]
        v_ref[...] = p[:, SHIFT_W + B_WIDTH + B_KV_WIDTH:]

    _two_stream(pl.program_id(0), n_p_tiles, run, (xp_ref,), (xs_ref,))


def _inproj_call(xp, xs, g, w_bf16):
    n_p, d = xp.shape
    tm = xs.shape[0]
    assert n_p % tm == 0 and tm % 8 == 0, "sample rows must form one row tile that divides the prompt rows"
    n_p_tiles = n_p // tm
    n = n_p + tm
    row = lambda i: (i, 0)
    fixed = lambda i: (0, 0)
    return pl.pallas_call(
        functools.partial(_inproj_body, n_p_tiles=n_p_tiles),
        grid=(n_p_tiles + 1,),
        in_specs=_stream_specs(tm, d, n_p_tiles) + [pl.BlockSpec((1, d), fixed), pl.BlockSpec((d, IN_W), fixed)],
        out_specs=[pl.BlockSpec((tm, SHIFT_W), row), pl.BlockSpec((tm, B_WIDTH), row),
                   pl.BlockSpec((tm, B_KV_WIDTH), row), pl.BlockSpec((tm, B_KV_WIDTH), row)],
        out_shape=[jax.ShapeDtypeStruct((n, SHIFT_W), F32), jax.ShapeDtypeStruct((n, B_WIDTH), F32),
                   jax.ShapeDtypeStruct((n, B_KV_WIDTH), F32), jax.ShapeDtypeStruct((n, B_KV_WIDTH), F32)],
        compiler_params=pltpu.CompilerParams(dimension_semantics=("arbitrary",), vmem_limit_bytes=VMEM_LIMIT),
        name="inproj",
    )(xp, xs, g, w_bf16)


def _rwkv_body(pa_ref, shift0_ref, s0_ref, mu_ref, w0_ref, a0_ref, wlora_ref, wgate_ref, kk_ref, ka_ref,
               rk_ref, lng_ref, lnb_ref, oa_ref, sout_ref, s_scr, prev_scr, *, chunk, n_sub, t_real):
    C = chunk
    rows = n_sub * C
    c = pl.program_id(1)

    @pl.when(c == 0)
    def _():
        s_scr[...] = s0_ref[0]
        prev_scr[...] = shift0_ref[0]

    pa = pa_ref[...]
    row = lax.broadcasted_iota(jnp.int32, (rows, 1), 0)
    row_in_chunk = row & (C - 1)
    pa_prev = jnp.where(row == 0, prev_scr[...], pltpu.roll(pa, 1, axis=0))
    prev_scr[...] = pa[rows - C + t_real - 1:rows - C + t_real]
    xm = pa + mu_ref[...] * (pa_prev - pa)

    r = xm[:, :A_WIDTH]
    k = xm[:, A_WIDTH:2 * A_WIDTH]
    v = xm[:, 2 * A_WIDTH:3 * A_WIDTH]
    lora_in = xm[:, 3 * A_WIDTH:3 * A_WIDTH + LORA_W]
    gd = xm[:, 3 * A_WIDTH + LORA_W:]

    lane = lax.broadcasted_iota(jnp.int32, (1, LANE), 1)
    lo_half = lane < HEAD_DIM
    z = jnp.where(lo_half, jnp.tanh(lora_in), lora_in)
    lw = _dot(z.astype(BF16), wlora_ref[...])
    dec_pre = w0_ref[...] + lw[:, :A_WIDTH]
    a = _sigmoid(a0_ref[...] + lw[:, A_WIDTH:])
    neg = -dec_pre
    softplus = jnp.maximum(neg, 0.0) + jnp.log(1.0 + jnp.exp(-jnp.abs(neg)))
    logdec = -jnp.exp(-softplus - DECAY_OFFSET)
    g = _dot(_sigmoid(gd).astype(BF16), wgate_ref[...])

    ri = lax.broadcasted_iota(jnp.int32, (LANE, LANE), 0)
    ci = lax.broadcasted_iota(jnp.int32, (LANE, LANE), 1)
    seg = jnp.where((ri >> HEAD_SHIFT) == (ci >> HEAD_SHIFT), 1.0, 0.0).astype(BF16)
    seg2 = jnp.concatenate([seg, seg], axis=0)

    def headsum(x):
        return jnp.concatenate([_dot(_hi_lo(x[:, j * LANE:(j + 1) * LANE], 1), seg2) for j in range(N_PAIRS)],
                               axis=1)

    kk = k * kk_ref[...]
    kk = kk * lax.rsqrt(jnp.maximum(headsum(kk * kk), 1e-24))
    k = k * (1.0 + (a - 1.0) * ka_ref[...])
    bonus = headsum(r * k * rk_ref[...]) * v

    if t_real < C:
        valid = row_in_chunk < t_real
        logdec = jnp.where(valid, logdec, 0.0)
        kk = jnp.where(valid, kk, 0.0)
        k = jnp.where(valid, k, 0.0)
        v = jnp.where(valid, v, 0.0)

    cum = logdec
    shift = 1
    while shift < C:
        cum = cum + jnp.where(row_in_chunk >= shift, pltpu.roll(cum, shift, axis=0), 0.0)
        shift *= 2
    ends = [cum[(s + 1) * C - 1:(s + 1) * C] for s in range(n_sub)]
    cum_end = jnp.concatenate([jnp.broadcast_to(e, (C, A_WIDTH)) for e in ends], axis=0) if n_sub > 1 else ends[0]
    w_incl = jnp.exp(cum)
    w_prev = jnp.exp(cum - logdec)
    w_inv = jnp.exp(-cum)
    w_end = jnp.exp(cum_end - cum)
    w_chunk = [jnp.exp(e) for e in ends]
    kka = kk * a
    terms = dict(A=-kk * w_prev, R=r * w_incl, B=kka * w_inv, K=k * w_inv, V=v, Be=kka * w_end, Ke=k * w_end)

    C2 = PAIR * C
    sr = lax.broadcasted_iota(jnp.int32, (C2, C2), 0)
    sc = lax.broadcasted_iota(jnp.int32, (C2, C2), 1)
    c_shift = C.bit_length() - 1
    same_head = (sr >> c_shift) == (sc >> c_shift)
    strict = same_head & ((sc & (C - 1)) < (sr & (C - 1)))
    incl = same_head & ((sc & (C - 1)) <= (sr & (C - 1)))
    ident = jnp.where(sr == sc, 1.0, 0.0).astype(F32)
    n_levels = max(1, (C - 1).bit_length())
    bf = lambda x: x.astype(BF16)

    chains = [(s, j) for s in range(n_sub) for j in range(N_PAIRS)]

    def st(name, s, j):
        x = terms[name][s * C:(s + 1) * C, j * LANE:(j + 1) * LANE]
        return jnp.concatenate([jnp.where(lo_half, x, 0.0), jnp.where(lo_half, 0.0, x)], axis=0)

    r_st = {ch: st("R", *ch) for ch in chains}
    stk = {ch: {nm: bf(st(nm, *ch)) for nm in ("A", "B", "K", "V", "Be", "Ke")} for ch in chains}
    m_ab, aak, arb, ark = {}, {}, {}, {}
    for ch in chains:
        t, rb = stk[ch], bf(r_st[ch])
        if C2 % LANE == 0:
            m1 = _dot_nt(jnp.concatenate([t["A"], rb], axis=0), jnp.concatenate([t["B"], t["K"]], axis=0))
            ab, ak, rbm, rk = m1[:C2, :C2], m1[:C2, C2:], m1[C2:, :C2], m1[C2:, C2:]
        else:
            ab, ak = _dot_nt(t["A"], t["B"]), _dot_nt(t["A"], t["K"])
            rbm, rk = _dot_nt(rb, t["B"]), _dot_nt(rb, t["K"])
        m_ab[ch] = jnp.where(strict, ab, 0.0)
        aak[ch] = bf(jnp.where(strict, ak, 0.0))
        arb[ch] = bf(jnp.where(incl, rbm, 0.0))
        ark[ch] = bf(jnp.where(incl, rk, 0.0))

    nn = dict(m_ab)
    tinv = {ch: ident + m_ab[ch] for ch in chains}
    for _ in range(1, n_levels):
        for ch in chains:
            nb = bf(nn[ch])
            nn[ch] = _dot(nb, nb)
        for ch in chains:
            tinv[ch] = tinv[ch] + _dot(bf(nn[ch]), bf(tinv[ch]))
    akv = {ch: _dot(aak[ch], stk[ch]["V"]) for ch in chains}
    xb = {ch: bf(_dot(bf(tinv[ch]), jnp.concatenate([stk[ch]["A"], bf(akv[ch])], axis=1))) for ch in chains}
    zed = {ch: _dot(arb[ch], xb[ch]) for ch in chains}
    rhat = {ch: bf(r_st[ch] + zed[ch][:, :LANE]) for ch in chains}
    y0 = {ch: _dot(ark[ch], stk[ch]["V"]) + zed[ch][:, LANE:] for ch in chains}
    p_mat = {ch: bf(_dot_tn(stk[ch]["Be"], xb[ch][:, :LANE])) for ch in chains}
    q_t = {ch: _dot_tn(jnp.concatenate([xb[ch][:, LANE:], stk[ch]["V"]], axis=0),
                       jnp.concatenate([stk[ch]["Be"], stk[ch]["Ke"]], axis=0)) for ch in chains}

    state = [s_scr[j] for j in range(N_PAIRS)]
    y_rows = []
    for s in range(n_sub):
        ys = []
        for j in range(N_PAIRS):
            ch = (s, j)
            s_cat = _hi_lo(state[j], 0)
            yy = _dot_nt(rhat[ch], s_cat)
            y_st = yy[:, :LANE] + yy[:, LANE:] + y0[ch]
            sp = _dot_nt(s_cat, p_mat[ch])
            state[j] = state[j] * w_chunk[s][:, j * LANE:(j + 1) * LANE] + sp[:LANE] + sp[LANE:] + q_t[ch]
            ys.append(y_st[:C] + y_st[C:])
        y_rows.append(jnp.concatenate(ys, axis=1))
    for j in range(N_PAIRS):
        s_scr[j] = state[j]

    y = jnp.concatenate(y_rows, axis=0) if n_sub > 1 else y_rows[0]
    inv_n = 1.0 / HEAD_DIM
    mean = headsum(y) * inv_n
    d = y - mean
    var = headsum(d * d) * inv_n
    yn = d * lax.rsqrt(var + LNX_EPS) * lng_ref[...] + lnb_ref[...]
    oa_ref[...] = (yn + bonus) * g
    sout_ref[0] = s_scr[...]


def _rwkv_call(pa2d, shift0, s0_bd, prm, *, n_seq, n_chunks, chunk, n_sub, t_real):
    n_steps = n_chunks // n_sub
    row = lambda b, c: (b * n_steps + c, 0)
    fixed = lambda b, c: (0, 0)
    vec = lambda w: pl.BlockSpec((1, w), fixed)
    body = functools.partial(_rwkv_body, chunk=chunk, n_sub=n_sub, t_real=t_real)
    n_rows = n_seq * n_chunks * chunk
    return pl.pallas_call(
        body,
        grid=(n_seq, n_steps),
        in_specs=[pl.BlockSpec((n_sub * chunk, SHIFT_W), row),
                  pl.BlockSpec((1, 1, SHIFT_W), lambda b, c: (b, 0, 0)),
                  pl.BlockSpec((1, N_PAIRS, LANE, LANE), lambda b, c: (b, 0, 0, 0)),
                  vec(SHIFT_W), vec(A_WIDTH), vec(A_WIDTH),
                  pl.BlockSpec((LORA_W, 2 * A_WIDTH), fixed), pl.BlockSpec((GATE_LORA, A_WIDTH), fixed),
                  vec(A_WIDTH), vec(A_WIDTH), vec(A_WIDTH), vec(A_WIDTH), vec(A_WIDTH)],
        out_specs=[pl.BlockSpec((n_sub * chunk, A_WIDTH), row),
                   pl.BlockSpec((1, N_PAIRS, LANE, LANE), lambda b, c: (b, 0, 0, 0))],
        out_shape=[jax.ShapeDtypeStruct((n_rows, A_WIDTH), F32),
                   jax.ShapeDtypeStruct((n_seq, N_PAIRS, LANE, LANE), F32)],
        scratch_shapes=[pltpu.VMEM((N_PAIRS, LANE, LANE), F32), pltpu.VMEM((1, SHIFT_W), F32)],
        compiler_params=pltpu.CompilerParams(dimension_semantics=("arbitrary", "arbitrary"),
                                             vmem_limit_bytes=VMEM_LIMIT),
        name="rwkv",
    )(pa2d, shift0, s0_bd, prm["mu"], prm["w0"], prm["a0"], prm["wlora"], prm["wgate"], prm["kk"], prm["ka"],
      prm["rk"], prm["lng"], prm["lnb"])


def _state_to_pairs(state):
    b = state.shape[0]
    st = state.reshape(b, N_PAIRS, PAIR, HEAD_DIM, HEAD_DIM)
    eye = jnp.eye(PAIR, dtype=state.dtype)
    bd = st[:, :, :, :, None, :] * eye[None, None, :, None, :, None]
    return bd.reshape(b, N_PAIRS, LANE, LANE)


def _pairs_to_state(bd):
    b = bd.shape[0]
    x = bd.reshape(b, N_PAIRS, PAIR, HEAD_DIM, PAIR, HEAD_DIM)
    st = jnp.stack([x[:, :, i, :, i, :] for i in range(PAIR)], axis=2)
    return st.reshape(b, A_HEADS, HEAD_DIM, HEAD_DIM)


def _swa_body(sink_ref, q_ref, kp_ref, vp_ref, kc_ref, vc_ref, gn_ref, o_ref, *, tq, first_has_prev):
    nblk = pl.program_id(1)
    q = q_ref[...]
    kp = kp_ref[0].astype(BF16)
    vp = vp_ref[0]
    kc = kc_ref[...].astype(BF16)
    vc = vc_ref[...]
    rows = B_GROUP * tq
    scale = HEAD_DIM ** -0.5
    lane = lax.broadcasted_iota(jnp.int32, (1, LANE), 1)
    grp = lax.broadcasted_iota(jnp.int32, (rows, 1), 0) >> (tq.bit_length() - 1)
    t = lax.broadcasted_iota(jnp.int32, (rows, 1), 0) & (tq - 1)
    dist_p = t + WINDOW - lax.broadcasted_iota(jnp.int32, (1, WINDOW), 1)
    dist_c = t - lax.broadcasted_iota(jnp.int32, (1, tq), 1)
    mask_p = (dist_p >= 0) & (dist_p < WINDOW)
    if not first_has_prev:
        mask_p = mask_p & (nblk > 0)
    mask_c = (dist_c >= 0) & (dist_c < WINDOW)
    dist_p = dist_p.astype(F32)
    dist_c = dist_c.astype(F32)

    out = None
    for kv in range(B_KV_HEADS):
        mk = (lane < HEAD_DIM) if kv == 0 else (lane >= HEAD_DIM)
        q_st = jnp.concatenate([jnp.where(mk, q[:, g * LANE:(g + 1) * LANE], 0.0) for g in range(B_GROUP)],
                               axis=0).astype(BF16)
        slope = jnp.zeros((rows, 1), F32)
        sink = jnp.zeros((rows, 1), F32)
        for g in range(B_GROUP):
            head = kv * B_GROUP + g
            slope = jnp.where(grp == g, 2.0 ** -(head + 1), slope)
            sink = jnp.where(grp == g, sink_ref[head], sink)
        s_p = jnp.where(mask_p, _dot_nt(q_st, kp) * scale - slope * dist_p, MASK_NEG)
        s_c = jnp.where(mask_c, _dot_nt(q_st, kc) * scale - slope * dist_c, MASK_NEG)
        m = jnp.maximum(jnp.maximum(jnp.max(s_p, axis=-1, keepdims=True), jnp.max(s_c, axis=-1, keepdims=True)),
                        sink)
        e_p = jnp.exp(s_p - m)
        e_c = jnp.exp(s_c - m)
        denom = jnp.sum(e_p, axis=-1, keepdims=True) + jnp.sum(e_c, axis=-1, keepdims=True) + jnp.exp(sink - m)
        inv = 1.0 / denom
        o_kv = (_dot((e_p * inv).astype(BF16), jnp.where(mk, vp, 0.0).astype(BF16))
                + _dot((e_c * inv).astype(BF16), jnp.where(mk, vc, 0.0).astype(BF16)))
        out = o_kv if out is None else out + o_kv

    tiles = [out[g * tq:(g + 1) * tq] for g in range(B_GROUP)]
    ssq = sum(jnp.sum(x * x, axis=-1, keepdims=True) for x in tiles)
    inv_rms = lax.rsqrt(ssq * (1.0 / B_WIDTH) + RMS_EPS)
    o_ref[...] = jnp.concatenate(tiles, axis=1) * inv_rms * gn_ref[...]


def _swa_call(sink, q2d, kprev, vprev, kcur2d, vcur2d, gn, *, n_seq, n_blk, tq, first_has_prev, prev_map):
    row = lambda b, n: (b * n_blk + n, 0)
    body = functools.partial(_swa_body, tq=tq, first_has_prev=first_has_prev)
    return pl.pallas_call(
        body,
        grid=(n_seq, n_blk),
        in_specs=[pl.BlockSpec(memory_space=pltpu.SMEM),
                  pl.BlockSpec((tq, B_WIDTH), row),
                  pl.BlockSpec((1, WINDOW, B_KV_WIDTH), prev_map),
                  pl.BlockSpec((1, WINDOW, B_KV_WIDTH), prev_map),
                  pl.BlockSpec((tq, B_KV_WIDTH), row),
                  pl.BlockSpec((tq, B_KV_WIDTH), row),
                  pl.BlockSpec((1, B_WIDTH), lambda b, n: (0, 0))],
        out_specs=pl.BlockSpec((tq, B_WIDTH), row),
        out_shape=jax.ShapeDtypeStruct((n_seq * n_blk * tq, B_WIDTH), F32),
        compiler_params=pltpu.CompilerParams(dimension_semantics=("arbitrary", "arbitrary"),
                                             vmem_limit_bytes=VMEM_LIMIT),
        name="swa",
    )(sink, q2d, kprev, vprev, kcur2d, vcur2d, gn)


def _route_rows(lg):
    lane = lax.broadcasted_iota(jnp.int32, (1, ROUTE_W), 1)
    lane_f = lane.astype(F32)
    no_lane = float(ROUTE_W)
    is_group = lane < N_GROUPS
    m_g = jnp.max(jnp.where(is_group, lg, MASK_NEG), axis=-1, keepdims=True)
    g_idx = jnp.min(jnp.where(is_group & (lg == m_g), lane_f, no_lane), axis=-1, keepdims=True)
    p_group = 1.0 / jnp.sum(jnp.where(is_group, jnp.exp(lg - m_g), 0.0), axis=-1, keepdims=True)
    e_lane = lane - N_GROUPS
    in_group = (e_lane >= 0) & (e_lane < N_EXPERTS) & ((e_lane >> 2).astype(F32) == g_idx)
    m_1 = jnp.max(jnp.where(in_group, lg, MASK_NEG), axis=-1, keepdims=True)
    i_1 = jnp.min(jnp.where(in_group & (lg == m_1), lane_f, no_lane), axis=-1, keepdims=True)
    rest = in_group & (lane_f != i_1)
    m_2 = jnp.max(jnp.where(rest, lg, MASK_NEG), axis=-1, keepdims=True)
    i_2 = jnp.min(jnp.where(rest & (lg == m_2), lane_f, no_lane), axis=-1, keepdims=True)
    ratio = jnp.exp(m_2 - m_1)
    w_1 = p_group / (1.0 + ratio)
    return jnp.where(lane == 0, i_1 - N_GROUPS,
                     jnp.where(lane == 1, i_2 - N_GROUPS,
                               jnp.where(lane == 2, w_1, jnp.where(lane == 3, w_1 * ratio, 0.0))))


def _outproj_body(xp_ref, xs_ref, oap_ref, oas_ref, obp_ref, obs_ref, wa_ref, wb_ref, g_ref, wr_ref, br_ref,
                  x1_ref, h2_ref, rt_ref, *, n_p_tiles):
    def run(x_ref, oa_ref, ob_ref):
        x1 = x_ref[...] + _dot(oa_ref[...].astype(BF16), wa_ref[...]) + _dot(ob_ref[...].astype(BF16), wb_ref[...])
        x1_ref[...] = x1
        h2 = x1 * lax.rsqrt(jnp.mean(x1 * x1, axis=-1, keepdims=True) + RMS_EPS) * g_ref[...]
        h2_ref[...] = h2.astype(BF16)
        rt_ref[...] = _route_rows(_dot(h2, wr_ref[...], HI) + br_ref[...])

    _two_stream(pl.program_id(0), n_p_tiles, run, (xp_ref, oap_ref, obp_ref), (xs_ref, oas_ref, obs_ref))


def _outproj_call(xp, xs, oap, oas, obp, obs, wa, wb, g, wr, br):
    n_p, d = xp.shape
    tm = xs.shape[0]
    n_p_tiles = n_p // tm
    n = n_p + tm
    row = lambda i: (i, 0)
    fixed = lambda i: (0, 0)
    return pl.pallas_call(
        functools.partial(_outproj_body, n_p_tiles=n_p_tiles),
        grid=(n_p_tiles + 1,),
        in_specs=(_stream_specs(tm, d, n_p_tiles) + _stream_specs(tm, A_WIDTH, n_p_tiles)
                  + _stream_specs(tm, B_WIDTH, n_p_tiles)
                  + [pl.BlockSpec((A_WIDTH, d), fixed), pl.BlockSpec((B_WIDTH, d), fixed), pl.BlockSpec((1, d), fixed),
                     pl.BlockSpec((d, ROUTE_W), fixed), pl.BlockSpec((1, ROUTE_W), fixed)]),
        out_specs=[pl.BlockSpec((tm, d), row), pl.BlockSpec((tm, d), row), pl.BlockSpec((tm, ROUTE_W), row)],
        out_shape=[jax.ShapeDtypeStruct((n, d), F32), jax.ShapeDtypeStruct((n, d), BF16),
                   jax.ShapeDtypeStruct((n, ROUTE_W), F32)],
        compiler_params=pltpu.CompilerParams(dimension_semantics=("arbitrary",), vmem_limit_bytes=VMEM_LIMIT),
        name="outproj",
    )(xp, xs, oap, oas, obp, obs, wa, wb, g, wr, br)


def _expert_body(be_ref, nused_ref, x_ref, wg_ref, wu_ref, wd_ref, y_ref, wg_bf, wu_bf, wd_bf):
    i = pl.program_id(0)

    @pl.when((i == 0) | (be_ref[i] != be_ref[jnp.maximum(i - 1, 0)]))
    def _():
        wg_bf[...] = wg_ref[0].astype(BF16)
        wu_bf[...] = wu_ref[0].astype(BF16)
        wd_bf[...] = wd_ref[0].astype(BF16)

    @pl.when(i < nused_ref[0])
    def _():
        x = x_ref[...]
        gate = _dot(x, wg_bf[...])
        up = _dot(x, wu_bf[...])
        mid = (gate * _sigmoid(gate) * up).astype(BF16)
        y_ref[...] = _dot(mid, wd_bf[...])

    @pl.when(i >= nused_ref[0])
    def _():
        y_ref[...] = jnp.zeros_like(y_ref)


def _expert_call(block_e, n_used, xbuf, wg, wu, wd, *, tb):
    p, d = xbuf.shape
    ff = wg.shape[-1]
    grid_spec = pltpu.PrefetchScalarGridSpec(
        num_scalar_prefetch=2,
        grid=(p // tb,),
        in_specs=[pl.BlockSpec((tb, d), lambda i, be, nu: (i, 0)),
                  pl.BlockSpec((1, d, ff), lambda i, be, nu: (be[i], 0, 0)),
                  pl.BlockSpec((1, d, ff), lambda i, be, nu: (be[i], 0, 0)),
                  pl.BlockSpec((1, ff, d), lambda i, be, nu: (be[i], 0, 0))],
        out_specs=pl.BlockSpec((tb, d), lambda i, be, nu: (i, 0)),
        scratch_shapes=[pltpu.VMEM((d, ff), BF16), pltpu.VMEM((d, ff), BF16), pltpu.VMEM((ff, d), BF16)],
    )
    return pl.pallas_call(
        _expert_body,
        grid_spec=grid_spec,
        out_shape=jax.ShapeDtypeStruct((p, d), F32),
        compiler_params=pltpu.CompilerParams(dimension_semantics=("arbitrary",), vmem_limit_bytes=VMEM_LIMIT),
        name="experts",
    )(block_e, n_used, xbuf, wg, wu, wd)


def _final_body(x1_ref, y0_ref, y1_ref, rt_ref, g_ref, o_ref):
    rt = rt_ref[...]
    x = x1_ref[...] + (rt[:, 2:3] * y0_ref[...] + rt[:, 3:4] * y1_ref[...])
    o_ref[...] = x * lax.rsqrt(jnp.mean(x * x, axis=-1, keepdims=True) + RMS_EPS) * g_ref[...]


def _final_call(x1, y0, y1, route, g, *, first_tile, n_tiles, tm):
    d = x1.shape[1]
    src = lambda i: (first_tile + i, 0)
    return pl.pallas_call(
        _final_body,
        grid=(n_tiles,),
        in_specs=[pl.BlockSpec((tm, d), src), pl.BlockSpec((tm, d), src), pl.BlockSpec((tm, d), src),
                  pl.BlockSpec((tm, ROUTE_W), src), pl.BlockSpec((1, d), lambda i: (0, 0))],
        out_specs=pl.BlockSpec((tm, d), lambda i: (i, 0)),
        out_shape=jax.ShapeDtypeStruct((n_tiles * tm, d), F32),
        compiler_params=pltpu.CompilerParams(dimension_semantics=("arbitrary",), vmem_limit_bytes=VMEM_LIMIT),
        name="final",
    )(x1, y0, y1, route, g)


def _dispatch(e_idx, tb):
    n = e_idx.shape[0]
    n_assign = n * TOP_K_INNER
    flat_e = e_idx.reshape(n_assign)
    onehot = (flat_e[:, None] == jnp.arange(N_EXPERTS, dtype=jnp.int32)[None, :]).astype(jnp.int32)
    csum = jnp.cumsum(onehot, axis=0)
    rank = jnp.sum(csum * onehot, axis=1) - 1
    counts = csum[-1]
    padded = (counts + tb - 1) // tb * tb
    pend = jnp.cumsum(padded)
    pstarts = pend - padded
    dest = (jnp.sum(pstarts[None, :] * onehot, axis=1) + rank).astype(jnp.int32)
    n_blocks = -(-n_assign // tb) + N_EXPERTS
    block_start = jnp.arange(n_blocks, dtype=jnp.int32) * tb
    block_e = jnp.minimum(jnp.sum((pend[None, :] <= block_start[:, None]).astype(jnp.int32), axis=1), N_EXPERTS - 1)
    n_used = (pend[-1:] // tb).astype(jnp.int32)
    tok = jnp.arange(n_assign, dtype=jnp.int32) // TOP_K_INNER
    src_tok = jnp.zeros((n_blocks * tb,), jnp.int32).at[dest].set(tok)
    return dest.reshape(n, TOP_K_INNER), block_e, n_used, src_tok


def _q_perm():
    cols = []
    for g in range(B_GROUP):
        for kv in range(B_KV_HEADS):
            h = kv * B_GROUP + g
            cols.extend(range(h * HEAD_DIM, (h + 1) * HEAD_DIM))
    return jnp.array(cols, dtype=jnp.int32)


def _layer(x_prompt, x_sample, state_rwkv, state_shift, cache_win_k, cache_win_v, prm, norm_final_g, *,
           chunk, n_sub, moe_block):
    bp, tp, d = x_prompt.shape
    bs, ts, _ = x_sample.shape
    n_p, n_s = bp * tp, bs * ts
    ts_pad = 8
    xp = x_prompt.reshape(n_p, d)
    xs = x_sample.reshape(n_s, d)

    qp = _q_perm()
    w_in = prm["w_in"]
    w_in = jnp.concatenate([w_in[:, :SHIFT_W], w_in[:, SHIFT_W:SHIFT_W + B_WIDTH][:, qp],
                            w_in[:, SHIFT_W + B_WIDTH:]], axis=1).astype(BF16)
    pa, q, k, v = _inproj_call(xp, xs, prm["norm_mix_g"][None], w_in)

    zero_blk = jnp.zeros((LORA_W // 2, A_WIDTH), F32)
    wlora = jnp.concatenate([jnp.concatenate([prm["w_decay_up"], zero_blk], axis=1),
                             jnp.concatenate([zero_blk, prm["w_iclr_up"]], axis=1)], axis=0).astype(BF16)
    rp = dict(mu=prm["mu_shift"][None], w0=prm["w_decay0"][None], a0=prm["w_iclr0"][None], wlora=wlora,
              wgate=prm["w_gate_up"].astype(BF16), kk=prm["k_k"][None], ka=prm["k_a"][None],
              rk=prm["r_k"].reshape(1, A_WIDTH), lng=prm["lnx_g"][None], lnb=prm["lnx_b"][None])

    oa_p, sbd_p = _rwkv_call(pa, jnp.zeros((bp, 1, SHIFT_W), F32), jnp.zeros((bp, N_PAIRS, LANE, LANE), F32), rp,
                             n_seq=bp, n_chunks=tp // chunk, chunk=chunk, n_sub=n_sub, t_real=chunk)
    pad_rows = lambda a: jnp.pad(a[n_p:].reshape(bs, ts, -1), ((0, 0), (0, ts_pad - ts), (0, 0))).reshape(
        bs * ts_pad, -1)
    oa_s, sbd_s = _rwkv_call(pad_rows(pa), state_shift[:, None, :], _state_to_pairs(state_rwkv), rp,
                             n_seq=bs, n_chunks=1, chunk=ts_pad, n_sub=1, t_real=ts)
    oa_s = oa_s.reshape(bs, ts_pad, A_WIDTH)[:, :ts].reshape(n_s, A_WIDTH)

    gn = prm["attn_norm_g"][qp][None]
    nb = tp // WINDOW
    k_p3 = k.reshape(-1, WINDOW, B_KV_WIDTH)
    v_p3 = v.reshape(-1, WINDOW, B_KV_WIDTH)
    ob_p = _swa_call(prm["attn_sink"], q, k_p3, v_p3, k, v, gn, n_seq=bp, n_blk=nb, tq=WINDOW,
                     first_has_prev=False, prev_map=lambda b, i: (b * nb + jnp.maximum(i - 1, 0), 0, 0))
    ob_s = _swa_call(prm["attn_sink"], pad_rows(q), cache_win_k.reshape(bs, WINDOW, B_KV_WIDTH),
                     cache_win_v.reshape(bs, WINDOW, B_KV_WIDTH), pad_rows(k), pad_rows(v), gn,
                     n_seq=bs, n_blk=1, tq=ts_pad, first_has_prev=True, prev_map=lambda b, i: (b, 0, 0))
    ob_s = ob_s.reshape(bs, ts_pad, B_WIDTH)[:, :ts].reshape(n_s, B_WIDTH)

    w_out = prm["w_out"]
    pad_lanes = jnp.zeros((d, ROUTE_W - N_GROUPS - N_EXPERTS), F32)
    wr = jnp.concatenate([prm["w_route_group"], prm["w_route_expert"], pad_lanes], axis=1)
    br = jnp.concatenate([prm["b_route_group"], prm["b_route_expert"], pad_lanes[0]])[None]
    x1, h2, route = _outproj_call(xp, xs, oa_p, oa_s, ob_p, ob_s, w_out[:A_WIDTH].astype(BF16),
                                  w_out[A_WIDTH:][qp].astype(BF16), prm["norm_ffn_g"][None], wr, br)

    dest, block_e, n_used, src_tok = _dispatch(route[:, :TOP_K_INNER].astype(jnp.int32), moe_block)
    xbuf = h2[src_tok]
    ybuf = _expert_call(block_e, n_used, xbuf, prm["w_exp_gate"], prm["w_exp_up"], prm["w_exp_down"], tb=moe_block)
    y0, y1 = ybuf[dest[:, 0]], ybuf[dest[:, 1]]
    gf = norm_final_g[None]
    y_p = _final_call(x1, y0, y1, route, gf, first_tile=0, n_tiles=n_p // n_s, tm=n_s)
    y_s = _final_call(x1, y0, y1, route, gf, first_tile=n_p // n_s, n_tiles=1, tm=n_s)

    kv4 = lambda a: a.reshape(a.shape[0], a.shape[1], B_KV_HEADS, HEAD_DIM)
    last_rows = lambda a, m: jnp.stack([a[(b + 1) * tp - m:(b + 1) * tp] for b in range(bp)], axis=0)
    wb = cache_win_k.shape[1]
    k_s = jnp.concatenate([cache_win_k.reshape(bs, wb, B_KV_WIDTH), k[n_p:].reshape(bs, ts, B_KV_WIDTH)],
                          axis=1)[:, -wb:]
    v_s = jnp.concatenate([cache_win_v.reshape(bs, wb, B_KV_WIDTH), v[n_p:].reshape(bs, ts, B_KV_WIDTH)],
                          axis=1)[:, -wb:]
    return (y_p.reshape(bp, tp, d), y_s.reshape(bs, ts, d),
            _pairs_to_state(sbd_p), last_rows(pa, 1)[:, 0], kv4(last_rows(k, WINDOW)), kv4(last_rows(v, WINDOW)),
            _pairs_to_state(sbd_s), pa[n_p:].reshape(bs, ts, SHIFT_W)[:, -1], kv4(k_s), kv4(v_s))


def kernel(x_prompt, x_sample, state_rwkv, state_shift, cache_win_k, cache_win_v, norm_mix_g, w_in, mu_shift, w_decay0, w_decay_up, w_iclr0, w_iclr_up, w_gate_up, k_k, k_a, r_k, lnx_g, lnx_b, attn_sink, attn_norm_g, w_out, norm_ffn_g, w_route_group, b_route_group, w_route_expert, b_route_expert, w_exp_gate, w_exp_up, w_exp_down, norm_final_g):
    assert norm_mix_g.shape[0] == 1, "single-layer trunk"
    prm = dict(norm_mix_g=norm_mix_g[0], w_in=w_in[0], mu_shift=mu_shift[0], w_decay0=w_decay0[0],
               w_decay_up=w_decay_up[0], w_iclr0=w_iclr0[0], w_iclr_up=w_iclr_up[0], w_gate_up=w_gate_up[0],
               k_k=k_k[0], k_a=k_a[0], r_k=r_k[0], lnx_g=lnx_g[0], lnx_b=lnx_b[0], attn_sink=attn_sink[0],
               attn_norm_g=attn_norm_g[0], w_out=w_out[0], norm_ffn_g=norm_ffn_g[0],
               w_route_group=w_route_group[0], b_route_group=b_route_group[0],
               w_route_expert=w_route_expert[0], b_route_expert=b_route_expert[0],
               w_exp_gate=w_exp_gate[0], w_exp_up=w_exp_up[0], w_exp_down=w_exp_down[0])
    outs = _layer(x_prompt, x_sample, state_rwkv[0], state_shift[0], cache_win_k[0], cache_win_v[0], prm,
                  norm_final_g, chunk=64, n_sub=2, moe_block=256)
    y_p, y_s, s_p, sh_p, kp, vp, s_s, sh_s, ks, vs = outs
    return (y_p, y_s, s_p[None], sh_p[None], kp[None], vp[None], s_s[None], sh_s[None], ks[None], vs[None])
```

```python
import functools

import jax
import jax.numpy as jnp
from jax import lax
from jax.experimental import pallas as pl
from jax.experimental.pallas import tpu as pltpu

F32 = jnp.float32
BF16 = jnp.bfloat16

HEAD_DIM = 64
A_HEADS = 8
A_WIDTH = A_HEADS * HEAD_DIM
B_HEADS = 8
B_KV_HEADS = 2
B_GROUP = B_HEADS // B_KV_HEADS
B_WIDTH = B_HEADS * HEAD_DIM
B_KV_WIDTH = B_KV_HEADS * HEAD_DIM
DECAY_LORA = 64
ICLR_LORA = 64
GATE_LORA = 128
LORA_W = DECAY_LORA + ICLR_LORA
SHIFT_W = 3 * A_WIDTH + LORA_W + GATE_LORA
IN_W = SHIFT_W + B_WIDTH + 2 * B_KV_WIDTH
WINDOW = 128
N_GROUPS = 4
EXPERTS_PER_GROUP = 4
N_EXPERTS = N_GROUPS * EXPERTS_PER_GROUP
TOP_K_INNER = 2
EXPERT_FF = 512
RMS_EPS = 1e-6
LNX_EPS = 64e-5
DECAY_OFFSET = 0.5

LANE = 128
PAIR = LANE // HEAD_DIM
N_PAIRS = A_HEADS // PAIR
ROUTE_W = LANE
MASK_NEG = -1e30
VMEM_LIMIT = 48 * 1024 * 1024

HEAD_SHIFT = HEAD_DIM.bit_length() - 1
HI = lax.Precision.HIGHEST


def _row_tile(n, cap):
    t = cap
    while n % t:
        t //= 2
    return t


def _dot(a, b, precision=None):
    return jnp.dot(a, b, preferred_element_type=F32, precision=precision)


def _dot_nt(a, b, precision=None):
    return lax.dot_general(a, b, (((1,), (1,)), ((), ())), preferred_element_type=F32, precision=precision)


def _dot_tn(a, b, precision=None):
    return lax.dot_general(a, b, (((0,), (0,)), ((), ())), preferred_element_type=F32, precision=precision)


def _sigmoid(x):
    return 1.0 / (1.0 + jnp.exp(-x))


def _hi_lo(x, axis):
    hi = x.astype(BF16)
    lo = (x - hi.astype(F32)).astype(BF16)
    return jnp.concatenate([hi, lo], axis=axis)


def _two_stream(i, n_p_tiles, run, prompt_refs, sample_refs):
    @pl.when(i < n_p_tiles)
    def _():
        run(*prompt_refs)

    @pl.when(i >= n_p_tiles)
    def _():
        run(*sample_refs)


def _stream_specs(tm, width, n_p_tiles):
    return [pl.BlockSpec((tm, width), lambda i: (jnp.minimum(i, n_p_tiles - 1), 0)),
            pl.BlockSpec((tm, width), lambda i: (0, 0))]


def _inproj_body(xp_ref, xs_ref, g_ref, w_ref, pa_ref, q_ref, k_ref, v_ref, *, n_p_tiles):
    def run(x_ref):
        x = x_ref[...]
        h = x * lax.rsqrt(jnp.mean(x * x, axis=-1, keepdims=True) + RMS_EPS) * g_ref[...]
        p = _dot(h.astype(BF16), w_ref[...])
        pa_ref[...] = p[:, :SHIFT_W]
        q_ref[...] = p[:, SHIFT_W:SHIFT_W + B_WIDTH]
        k_ref[...] = p[:, SHIFT_W + B_WIDTH:SHIFT_W + B_WIDTH + B_KV_WIDTH]
        v_ref[...] = p[:, SHIFT_W + B_WIDTH + B_KV_WIDTH:]

    _two_stream(pl.program_id(0), n_p_tiles, run, (xp_ref,), (xs_ref,))


def _inproj_call(xp, xs, g, w_bf16):
    n_p, d = xp.shape
    tm = xs.shape[0]
    assert n_p % tm == 0 and tm % 8 == 0, "sample rows must form one row tile that divides the prompt rows"
    n_p_tiles = n_p // tm
    n = n_p + tm
    row = lambda i: (i, 0)
    fixed = lambda i: (0, 0)
    return pl.pallas_call(
        functools.partial(_inproj_body, n_p_tiles=n_p_tiles),
        grid=(n_p_tiles + 1,),
        in_specs=_stream_specs(tm, d, n_p_tiles) + [pl.BlockSpec((1, d), fixed), pl.BlockSpec((d, IN_W), fixed)],
        out_specs=[pl.BlockSpec((tm, SHIFT_W), row), pl.BlockSpec((tm, B_WIDTH), row),
                   pl.BlockSpec((tm, B_KV_WIDTH), row), pl.BlockSpec((tm, B_KV_WIDTH), row)],
        out_shape=[jax.ShapeDtypeStruct((n, SHIFT_W), F32), jax.ShapeDtypeStruct((n, B_WIDTH), F32),
                   jax.ShapeDtypeStruct((n, B_KV_WIDTH), F32), jax.ShapeDtypeStruct((n, B_KV_WIDTH), F32)],
        compiler_params=pltpu.CompilerParams(dimension_semantics=("arbitrary",), vmem_limit_bytes=VMEM_LIMIT),
        name="inproj",
    )(xp, xs, g, w_bf16)


def _rwkv_body(pa_ref, shift0_ref, s0_ref, mu_ref, w0_ref, a0_ref, wlora_ref, wgate_ref, kk_ref, ka_ref,
               rk_ref, lng_ref, lnb_ref, oa_ref, sout_ref, s_scr, prev_scr, *, chunk, n_sub, t_real):
    C = chunk
    rows = n_sub * C
    c = pl.program_id(1)

    @pl.when(c == 0)
    def _():
        s_scr[...] = s0_ref[0]
        prev_scr[...] = shift0_ref[0]

    pa = pa_ref[...]
    row = lax.broadcasted_iota(jnp.int32, (rows, 1), 0)
    row_in_chunk = row & (C - 1)
    pa_prev = jnp.where(row == 0, prev_scr[...], pltpu.roll(pa, 1, axis=0))
    prev_scr[...] = pa[rows - C + t_real - 1:rows - C + t_real]
    xm = pa + mu_ref[...] * (pa_prev - pa)

    r = xm[:, :A_WIDTH]
    k = xm[:, A_WIDTH:2 * A_WIDTH]
    v = xm[:, 2 * A_WIDTH:3 * A_WIDTH]
    lora_in = xm[:, 3 * A_WIDTH:3 * A_WIDTH + LORA_W]
    gd = xm[:, 3 * A_WIDTH + LORA_W:]

    lane = lax.broadcasted_iota(jnp.int32, (1, LANE), 1)
    lo_half = lane < HEAD_DIM
    z = jnp.where(lo_half, jnp.tanh(lora_in), lora_in)
    lw = _dot(z.astype(BF16), wlora_ref[...])
    dec_pre = w0_ref[...] + lw[:, :A_WIDTH]
    a = _sigmoid(a0_ref[...] + lw[:, A_WIDTH:])
    neg = -dec_pre
    softplus = jnp.maximum(neg, 0.0) + jnp.log(1.0 + jnp.exp(-jnp.abs(neg)))
    logdec = -jnp.exp(-softplus - DECAY_OFFSET)
    g = _dot(_sigmoid(gd).astype(BF16), wgate_ref[...])

    ri = lax.broadcasted_iota(jnp.int32, (LANE, LANE), 0)
    ci = lax.broadcasted_iota(jnp.int32, (LANE, LANE), 1)
    seg = jnp.where((ri >> HEAD_SHIFT) == (ci >> HEAD_SHIFT), 1.0, 0.0).astype(BF16)
    seg2 = jnp.concatenate([seg, seg], axis=0)

    def headsum(x):
        return jnp.concatenate([_dot(_hi_lo(x[:, j * LANE:(j + 1) * LANE], 1), seg2) for j in range(N_PAIRS)],
                               axis=1)

    kk = k * kk_ref[...]
    kk = kk * lax.rsqrt(jnp.maximum(headsum(kk * kk), 1e-24))
    k = k * (1.0 + (a - 1.0) * ka_ref[...])
    bonus = headsum(r * k * rk_ref[...]) * v

    if t_real < C:
        valid = row_in_chunk < t_real
        logdec = jnp.where(valid, logdec, 0.0)
        kk = jnp.where(valid, kk, 0.0)
        k = jnp.where(valid, k, 0.0)
        v = jnp.where(valid, v, 0.0)

    cum = logdec
    shift = 1
    while shift < C:
        cum = cum + jnp.where(row_in_chunk >= shift, pltpu.roll(cum, shift, axis=0), 0.0)
        shift *= 2
    ends = [cum[(s + 1) * C - 1:(s + 1) * C] for s in range(n_sub)]
    cum_end = jnp.concatenate([jnp.broadcast_to(e, (C, A_WIDTH)) for e in ends], axis=0) if n_sub > 1 else ends[0]
    w_incl = jnp.exp(cum)
    w_prev = jnp.exp(cum - logdec)
    w_inv = jnp.exp(-cum)
    w_end = jnp.exp(cum_end - cum)
    w_chunk = [jnp.exp(e) for e in ends]
    kka = kk * a
    terms = dict(A=-kk * w_prev, R=r * w_incl, B=kka * w_inv, K=k * w_inv, V=v, Be=kka * w_end, Ke=k * w_end)

    C2 = PAIR * C
    sr = lax.broadcasted_iota(jnp.int32, (C2, C2), 0)
    sc = lax.broadcasted_iota(jnp.int32, (C2, C2), 1)
    c_shift = C.bit_length() - 1
    same_head = (sr >> c_shift) == (sc >> c_shift)
    strict = same_head & ((sc & (C - 1)) < (sr & (C - 1)))
    incl = same_head & ((sc & (C - 1)) <= (sr & (C - 1)))
    ident = jnp.where(sr == sc, 1.0, 0.0).astype(F32)
    n_levels = max(1, (C - 1).bit_length())
    bf = lambda x: x.astype(BF16)

    chains = [(s, j) for s in range(n_sub) for j in range(N_PAIRS)]

    def st(name, s, j):
        x = terms[name][s * C:(s + 1) * C, j * LANE:(j + 1) * LANE]
        return jnp.concatenate([jnp.where(lo_half, x, 0.0), jnp.where(lo_half, 0.0, x)], axis=0)

    r_st = {ch: st("R", *ch) for ch in chains}
    stk = {ch: {nm: bf(st(nm, *ch)) for nm in ("A", "B", "K", "V", "Be", "Ke")} for ch in chains}
    m_ab, aak, arb, ark = {}, {}, {}, {}
    for ch in chains:
        t, rb = stk[ch], bf(r_st[ch])
        if C2 % LANE == 0:
            m1 = _dot_nt(jnp.concatenate([t["A"], rb], axis=0), jnp.concatenate([t["B"], t["K"]], axis=0))
            ab, ak, rbm, rk = m1[:C2, :C2], m1[:C2, C2:], m1[C2:, :C2], m1[C2:, C2:]
        else:
            ab, ak = _dot_nt(t["A"], t["B"]), _dot_nt(t["A"], t["K"])
            rbm, rk = _dot_nt(rb, t["B"]), _dot_nt(rb, t["K"])
        m_ab[ch] = jnp.where(strict, ab, 0.0)
        aak[ch] = bf(jnp.where(strict, ak, 0.0))
        arb[ch] = bf(jnp.where(incl, rbm, 0.0))
        ark[ch] = bf(jnp.where(incl, rk, 0.0))

    nn = dict(m_ab)
    tinv = {ch: ident + m_ab[ch] for ch in chains}
    for _ in range(1, n_levels):
        for ch in chains:
            nb = bf(nn[ch])
            nn[ch] = _dot(nb, nb)
        for ch in chains:
            tinv[ch] = tinv[ch] + _dot(bf(nn[ch]), bf(tinv[ch]))
    akv = {ch: _dot(aak[ch], stk[ch]["V"]) for ch in chains}
    xb = {ch: bf(_dot(bf(tinv[ch]), jnp.concatenate([stk[ch]["A"], bf(akv[ch])], axis=1))) for ch in chains}
    zed = {ch: _dot(arb[ch], xb[ch]) for ch in chains}
    rhat = {ch: bf(r_st[ch] + zed[ch][:, :LANE]) for ch in chains}
    y0 = {ch: _dot(ark[ch], stk[ch]["V"]) + zed[ch][:, LANE:] for ch in chains}
    p_mat = {ch: bf(_dot_tn(stk[ch]["Be"], xb[ch][:, :LANE])) for ch in chains}
    q_t = {ch: _dot_tn(jnp.concatenate([xb[ch][:, LANE:], stk[ch]["V"]], axis=0),
                       jnp.concatenate([stk[ch]["Be"], stk[ch]["Ke"]], axis=0)) for ch in chains}

    state = [s_scr[j] for j in range(N_PAIRS)]
    y_rows = []
    for s in range(n_sub):
        ys = []
        for j in range(N_PAIRS):
            ch = (s, j)
            s_cat = _hi_lo(state[j], 0)
            yy = _dot_nt(rhat[ch], s_cat)
            y_st = yy[:, :LANE] + yy[:, LANE:] + y0[ch]
            sp = _dot_nt(s_cat, p_mat[ch])
            state[j] = state[j] * w_chunk[s][:, j * LANE:(j + 1) * LANE] + sp[:LANE] + sp[LANE:] + q_t[ch]
            ys.append(y_st[:C] + y_st[C:])
        y_rows.append(jnp.concatenate(ys, axis=1))
    for j in range(N_PAIRS):
        s_scr[j] = state[j]

    y = jnp.concatenate(y_rows, axis=0) if n_sub > 1 else y_rows[0]
    inv_n = 1.0 / HEAD_DIM
    mean = headsum(y) * inv_n
    d = y - mean
    var = headsum(d * d) * inv_n
    yn = d * lax.rsqrt(var + LNX_EPS) * lng_ref[...] + lnb_ref[...]
    oa_ref[...] = (yn + bonus) * g
    sout_ref[0] = s_scr[...]


def _rwkv_call(pa2d, shift0, s0_bd, prm, *, n_seq, n_chunks, chunk, n_sub, t_real):
    n_steps = n_chunks // n_sub
    row = lambda b, c: (b * n_steps + c, 0)
    fixed = lambda b, c: (0, 0)
    vec = lambda w: pl.BlockSpec((1, w), fixed)
    body = functools.partial(_rwkv_body, chunk=chunk, n_sub=n_sub, t_real=t_real)
    n_rows = n_seq * n_chunks * chunk
    return pl.pallas_call(
        body,
        grid=(n_seq, n_steps),
        in_specs=[pl.BlockSpec((n_sub * chunk, SHIFT_W), row),
                  pl.BlockSpec((1, 1, SHIFT_W), lambda b, c: (b, 0, 0)),
                  pl.BlockSpec((1, N_PAIRS, LANE, LANE), lambda b, c: (b, 0, 0, 0)),
                  vec(SHIFT_W), vec(A_WIDTH), vec(A_WIDTH),
                  pl.BlockSpec((LORA_W, 2 * A_WIDTH), fixed), pl.BlockSpec((GATE_LORA, A_WIDTH), fixed),
                  vec(A_WIDTH), vec(A_WIDTH), vec(A_WIDTH), vec(A_WIDTH), vec(A_WIDTH)],
        out_specs=[pl.BlockSpec((n_sub * chunk, A_WIDTH), row),
                   pl.BlockSpec((1, N_PAIRS, LANE, LANE), lambda b, c: (b, 0, 0, 0))],
        out_shape=[jax.ShapeDtypeStruct((n_rows, A_WIDTH), F32),
                   jax.ShapeDtypeStruct((n_seq, N_PAIRS, LANE, LANE), F32)],
        scratch_shapes=[pltpu.VMEM((N_PAIRS, LANE, LANE), F32), pltpu.VMEM((1, SHIFT_W), F32)],
        compiler_params=pltpu.CompilerParams(dimension_semantics=("arbitrary", "arbitrary"),
                                             vmem_limit_bytes=VMEM_LIMIT),
        name="rwkv",
    )(pa2d, shift0, s0_bd, prm["mu"], prm["w0"], prm["a0"], prm["wlora"], prm["wgate"], prm["kk"], prm["ka"],
      prm["rk"], prm["lng"], prm["lnb"])


def _state_to_pairs(state):
    b = state.shape[0]
    st = state.reshape(b, N_PAIRS, PAIR, HEAD_DIM, HEAD_DIM)
    eye = jnp.eye(PAIR, dtype=state.dtype)
    bd = st[:, :, :, :, None, :] * eye[None, None, :, None, :, None]
    return bd.reshape(b, N_PAIRS, LANE, LANE)


def _pairs_to_state(bd):
    b = bd.shape[0]
    x = bd.reshape(b, N_PAIRS, PAIR, HEAD_DIM, PAIR, HEAD_DIM)
    st = jnp.stack([x[:, :, i, :, i, :] for i in range(PAIR)], axis=2)
    return st.reshape(b, A_HEADS, HEAD_DIM, HEAD_DIM)


def _swa_bias(tq):
    rows = B_GROUP * tq
    grp = jnp.arange(rows, dtype=jnp.int32)[:, None] // tq
    t = jnp.arange(rows, dtype=jnp.int32)[:, None] % tq
    dist_p = t + WINDOW - jnp.arange(WINDOW, dtype=jnp.int32)[None, :]
    dist_c = t - jnp.arange(tq, dtype=jnp.int32)[None, :]

    def bias(dist, kv):
        slope = sum(jnp.where(grp == g, 2.0 ** -(kv * B_GROUP + g + 1), 0.0) for g in range(B_GROUP))
        return jnp.where((dist >= 0) & (dist < WINDOW), -slope * dist.astype(F32), MASK_NEG)

    return (jnp.stack([bias(dist_p, kv) for kv in range(B_KV_HEADS)]),
            jnp.stack([bias(dist_c, kv) for kv in range(B_KV_HEADS)]))


def _swa_body(sink_ref, q_ref, kp_ref, vp_ref, kc_ref, vc_ref, bp_ref, bc_ref, gn_ref, o_ref, *, tq, n_bat,
              first_has_prev):
    for s in range(n_bat):
        rs = slice(s * tq, (s + 1) * tq)
        _swa_one(sink_ref, q_ref[rs], kp_ref[s], vp_ref[s], kc_ref[rs], vc_ref[rs], bp_ref, bc_ref, gn_ref,
                 o_ref.at[rs], tq=tq, first_has_prev=first_has_prev)


def _swa_one(sink_ref, q, kp, vp, kc, vc, bp_ref, bc_ref, gn_ref, o_ref, *, tq, first_has_prev):
    nblk = pl.program_id(1)
    q = q * (HEAD_DIM ** -0.5)
    kp = kp.astype(BF16)
    kc = kc.astype(BF16)
    rows = B_GROUP * tq
    lane = lax.broadcasted_iota(jnp.int32, (1, LANE), 1)
    grp = lax.broadcasted_iota(jnp.int32, (rows, 1), 0) >> (tq.bit_length() - 1)
    prev_pen = 0.0 if first_has_prev else jnp.where(nblk > 0, 0.0, MASK_NEG)

    out = None
    for kv in range(B_KV_HEADS):
        mk = (lane < HEAD_DIM) if kv == 0 else (lane >= HEAD_DIM)
        q_st = jnp.concatenate([jnp.where(mk, q[:, g * LANE:(g + 1) * LANE], 0.0) for g in range(B_GROUP)],
                               axis=0).astype(BF16)
        sink = jnp.zeros((rows, 1), F32)
        for g in range(B_GROUP):
            sink = jnp.where(grp == g, sink_ref[kv * B_GROUP + g], sink)
        s_p = _dot_nt(q_st, kp) + (bp_ref[kv] + prev_pen)
        s_c = _dot_nt(q_st, kc) + bc_ref[kv]
        if tq == WINDOW:
            m = jnp.maximum(jnp.max(jnp.maximum(s_p, s_c), axis=-1, keepdims=True), sink)
        else:
            m = jnp.maximum(jnp.maximum(jnp.max(s_p, axis=-1, keepdims=True),
                                        jnp.max(s_c, axis=-1, keepdims=True)), sink)
        e_p = jnp.exp(s_p - m)
        e_c = jnp.exp(s_c - m)
        if tq == WINDOW:
            e_sum = jnp.sum(e_p + e_c, axis=-1, keepdims=True)
        else:
            e_sum = jnp.sum(e_p, axis=-1, keepdims=True) + jnp.sum(e_c, axis=-1, keepdims=True)
        inv = 1.0 / (e_sum + jnp.exp(sink - m))
        o_kv = (_dot(e_p.astype(BF16), jnp.where(mk, vp, 0.0).astype(BF16))
                + _dot(e_c.astype(BF16), jnp.where(mk, vc, 0.0).astype(BF16))) * inv
        out = o_kv if out is None else out + o_kv

    tiles = [out[g * tq:(g + 1) * tq] for g in range(B_GROUP)]
    ssq = sum(jnp.sum(x * x, axis=-1, keepdims=True) for x in tiles)
    inv_rms = lax.rsqrt(ssq * (1.0 / B_WIDTH) + RMS_EPS)
    o_ref[...] = jnp.concatenate(tiles, axis=1) * inv_rms * gn_ref[...]


def _swa_call(sink, q2d, kprev, vprev, kcur2d, vcur2d, gn, *, n_seq, n_blk, tq, n_bat, first_has_prev, prev_map):
    assert n_seq % n_bat == 0 and (n_bat == 1 or n_blk == 1)
    row = lambda b, n: (b * n_blk + n, 0)
    body = functools.partial(_swa_body, tq=tq, n_bat=n_bat, first_has_prev=first_has_prev)
    bias_p, bias_c = _swa_bias(tq)
    whole = lambda b, n: (0, 0, 0)
    return pl.pallas_call(
        body,
        grid=(n_seq // n_bat, n_blk),
        in_specs=[pl.BlockSpec(memory_space=pltpu.SMEM),
                  pl.BlockSpec((n_bat * tq, B_WIDTH), row),
                  pl.BlockSpec((n_bat, WINDOW, B_KV_WIDTH), prev_map),
                  pl.BlockSpec((n_bat, WINDOW, B_KV_WIDTH), prev_map),
                  pl.BlockSpec((n_bat * tq, B_KV_WIDTH), row),
                  pl.BlockSpec((n_bat * tq, B_KV_WIDTH), row),
                  pl.BlockSpec(bias_p.shape, whole),
                  pl.BlockSpec(bias_c.shape, whole),
                  pl.BlockSpec((1, B_WIDTH), lambda b, n: (0, 0))],
        out_specs=pl.BlockSpec((n_bat * tq, B_WIDTH), row),
        out_shape=jax.ShapeDtypeStruct((n_seq * n_blk * tq, B_WIDTH), F32),
        compiler_params=pltpu.CompilerParams(dimension_semantics=("arbitrary", "arbitrary"),
                                             vmem_limit_bytes=VMEM_LIMIT),
        name="swa",
    )(sink, q2d, kprev, vprev, kcur2d, vcur2d, bias_p, bias_c, gn)


def _route_rows(lg):
    lane = lax.broadcasted_iota(jnp.int32, (1, ROUTE_W), 1)
    lane_f = lane.astype(F32)
    no_lane = float(ROUTE_W)
    is_group = lane < N_GROUPS
    m_g = jnp.max(jnp.where(is_group, lg, MASK_NEG), axis=-1, keepdims=True)
    g_idx = jnp.min(jnp.where(is_group & (lg == m_g), lane_f, no_lane), axis=-1, keepdims=True)
    p_group = 1.0 / jnp.sum(jnp.where(is_group, jnp.exp(lg - m_g), 0.0), axis=-1, keepdims=True)
    e_lane = lane - N_GROUPS
    in_group = (e_lane >= 0) & (e_lane < N_EXPERTS) & ((e_lane >> 2).astype(F32) == g_idx)
    m_1 = jnp.max(jnp.where(in_group, lg, MASK_NEG), axis=-1, keepdims=True)
    i_1 = jnp.min(jnp.where(in_group & (lg == m_1), lane_f, no_lane), axis=-1, keepdims=True)
    rest = in_group & (lane_f != i_1)
    m_2 = jnp.max(jnp.where(rest, lg, MASK_NEG), axis=-1, keepdims=True)
    i_2 = jnp.min(jnp.where(rest & (lg == m_2), lane_f, no_lane), axis=-1, keepdims=True)
    ratio = jnp.exp(m_2 - m_1)
    w_1 = p_group / (1.0 + ratio)
    return jnp.where(lane == 0, i_1 - N_GROUPS,
                     jnp.where(lane == 1, i_2 - N_GROUPS,
                               jnp.where(lane == 2, w_1, jnp.where(lane == 3, w_1 * ratio, 0.0))))


def _outproj_body(xp_ref, xs_ref, oap_ref, oas_ref, obp_ref, obs_ref, wa_ref, wb_ref, g_ref, wr_ref, br_ref,
                  x1_ref, h2_ref, rt_ref, *, n_p_tiles):
    def run(x_ref, oa_ref, ob_ref):
        x1 = x_ref[...] + _dot(oa_ref[...].astype(BF16), wa_ref[...]) + _dot(ob_ref[...].astype(BF16), wb_ref[...])
        x1_ref[...] = x1
        h2 = x1 * lax.rsqrt(jnp.mean(x1 * x1, axis=-1, keepdims=True) + RMS_EPS) * g_ref[...]
        h2b = h2.astype(BF16)
        h2_ref[...] = h2b
        rt_ref[...] = _route_rows(_dot(h2b, wr_ref[...]) + br_ref[...])

    _two_stream(pl.program_id(0), n_p_tiles, run, (xp_ref, oap_ref, obp_ref), (xs_ref, oas_ref, obs_ref))


def _outproj_call(xp, xs, oap, oas, obp, obs, wa, wb, g, wr, br):
    n_p, d = xp.shape
    tm = xs.shape[0]
    n_p_tiles = n_p // tm
    n = n_p + tm
    row = lambda i: (i, 0)
    fixed = lambda i: (0, 0)
    return pl.pallas_call(
        functools.partial(_outproj_body, n_p_tiles=n_p_tiles),
        grid=(n_p_tiles + 1,),
        in_specs=(_stream_specs(tm, d, n_p_tiles) + _stream_specs(tm, A_WIDTH, n_p_tiles)
                  + _stream_specs(tm, B_WIDTH, n_p_tiles)
                  + [pl.BlockSpec((A_WIDTH, d), fixed), pl.BlockSpec((B_WIDTH, d), fixed), pl.BlockSpec((1, d), fixed),
                     pl.BlockSpec((d, ROUTE_W), fixed), pl.BlockSpec((1, ROUTE_W), fixed)]),
        out_specs=[pl.BlockSpec((tm, d), row), pl.BlockSpec((tm, d), row), pl.BlockSpec((tm, ROUTE_W), row)],
        out_shape=[jax.ShapeDtypeStruct((n, d), F32), jax.ShapeDtypeStruct((n, d), BF16),
                   jax.ShapeDtypeStruct((n, ROUTE_W), F32)],
        compiler_params=pltpu.CompilerParams(dimension_semantics=("arbitrary",), vmem_limit_bytes=VMEM_LIMIT),
        name="outproj",
    )(xp, xs, oap, oas, obp, obs, wa, wb, g, wr, br)


def _expert_body(vb_ref, ve_ref, lo_ref, hi_ref, x_ref, wg_ref, wu_ref, wd_ref, y_ref, wg_bf, wu_bf, wd_bf):
    v = pl.program_id(0)
    pv = jnp.maximum(v - 1, 0)
    lo, hi = lo_ref[v], hi_ref[v]
    first_visit = (v == 0) | (vb_ref[v] != vb_ref[pv])

    @pl.when((v == 0) | (ve_ref[v] != ve_ref[pv]))
    def _():
        wg_bf[...] = wg_ref[0].astype(BF16)
        wu_bf[...] = wu_ref[0].astype(BF16)
        wd_bf[...] = wd_ref[0].astype(BF16)

    @pl.when(hi > lo)
    def _():
        x = x_ref[...]
        gate = _dot(x, wg_bf[...])
        up = _dot(x, wu_bf[...])
        mid = (gate * _sigmoid(gate) * up).astype(BF16)
        y = _dot(mid, wd_bf[...])
        row = lax.broadcasted_iota(jnp.int32, (y.shape[0], 1), 0)
        mine = (row >= lo) & (row < hi)

        @pl.when(first_visit)
        def _():
            y_ref[...] = jnp.where(mine, y, 0.0)

        @pl.when(jnp.logical_not(first_visit))
        def _():
            y_ref[...] = jnp.where(mine, y, y_ref[...])


def _expert_call(visits, xs, wg, wu, wd, *, tb):
    n_rows, d = xs.shape
    ff = wg.shape[-1]
    vb, ve, lo, hi = visits
    blk = lambda v, vb, ve, lo, hi: (vb[v], 0)
    wsel = lambda v, vb, ve, lo, hi: (ve[v], 0, 0)
    grid_spec = pltpu.PrefetchScalarGridSpec(
        num_scalar_prefetch=4,
        grid=(vb.shape[0],),
        in_specs=[pl.BlockSpec((tb, d), blk), pl.BlockSpec((1, d, ff), wsel), pl.BlockSpec((1, d, ff), wsel),
                  pl.BlockSpec((1, ff, d), wsel)],
        out_specs=pl.BlockSpec((tb, d), blk),
        scratch_shapes=[pltpu.VMEM((d, ff), BF16), pltpu.VMEM((d, ff), BF16), pltpu.VMEM((ff, d), BF16)],
    )
    return pl.pallas_call(
        _expert_body,
        grid_spec=grid_spec,
        out_shape=jax.ShapeDtypeStruct((n_rows, d), F32),
        compiler_params=pltpu.CompilerParams(dimension_semantics=("arbitrary",), vmem_limit_bytes=VMEM_LIMIT),
        name="experts",
    )(vb, ve, lo, hi, xs, wg, wu, wd)


def _final_body(x1_ref, y0_ref, y1_ref, rt_ref, g_ref, o_ref):
    rt = rt_ref[...]
    x = x1_ref[...] + (rt[:, 2:3] * y0_ref[...] + rt[:, 3:4] * y1_ref[...])
    o_ref[...] = x * lax.rsqrt(jnp.mean(x * x, axis=-1, keepdims=True) + RMS_EPS) * g_ref[...]


def _final_call(x1, y0, y1, route, g, *, first_tile, n_tiles, tm):
    d = x1.shape[1]
    src = lambda i: (first_tile + i, 0)
    return pl.pallas_call(
        _final_body,
        grid=(n_tiles,),
        in_specs=[pl.BlockSpec((tm, d), src), pl.BlockSpec((tm, d), src), pl.BlockSpec((tm, d), src),
                  pl.BlockSpec((tm, ROUTE_W), src), pl.BlockSpec((1, d), lambda i: (0, 0))],
        out_specs=pl.BlockSpec((tm, d), lambda i: (i, 0)),
        out_shape=jax.ShapeDtypeStruct((n_tiles * tm, d), F32),
        compiler_params=pltpu.CompilerParams(dimension_semantics=("arbitrary",), vmem_limit_bytes=VMEM_LIMIT),
        name="final",
    )(x1, y0, y1, route, g)


def _dispatch(e_idx, tb):
    n = e_idx.shape[0]
    n_assign = n * TOP_K_INNER
    idx_bits = (n_assign - 1).bit_length()
    assert N_EXPERTS << idx_bits < 2 ** 31 and n_assign % tb == 0
    flat_e = e_idx.reshape(n_assign)
    ids = jnp.arange(n_assign, dtype=jnp.int32)
    sorted_key = jnp.sort((flat_e << idx_bits) | ids)
    src_tok = (sorted_key & ((1 << idx_bits) - 1)) // TOP_K_INNER

    experts = jnp.arange(N_EXPERTS, dtype=jnp.int32)
    onehot = (flat_e[:, None] == experts[None, :]).astype(jnp.int32)
    csum = jnp.cumsum(onehot, axis=0)
    counts = csum[-1]
    ends = jnp.cumsum(counts)
    starts = ends - counts
    dest = (jnp.sum((csum - 1 + starts[None, :]) * onehot, axis=1)).astype(jnp.int32)

    n_blocks = n_assign // tb
    n_visits = n_blocks + N_EXPERTS - 1
    first_blk = starts // tb
    n_vis_e = jnp.where(counts > 0, (ends + tb - 1) // tb - first_blk, 0)
    v_end = jnp.cumsum(n_vis_e)
    v = jnp.arange(n_visits, dtype=jnp.int32)
    valid = v < v_end[-1]
    last_e = jnp.max(jnp.where(counts > 0, experts, 0))
    ve = jnp.where(valid, jnp.sum((v_end[None, :] <= v[:, None]).astype(jnp.int32), axis=1), last_e)
    pick = lambda a: jnp.sum(jnp.where(ve[:, None] == experts[None, :], a[None, :], 0), axis=1)
    vb = pick(first_blk) + v - pick(v_end - n_vis_e)
    lo = jnp.maximum(pick(starts), vb * tb) - vb * tb
    hi = jnp.minimum(pick(ends), (vb + 1) * tb) - vb * tb
    vb = jnp.where(valid, vb, n_blocks - 1)
    lo = jnp.where(valid, lo, 0)
    hi = jnp.where(valid, hi, 0)
    return dest.reshape(n, TOP_K_INNER), src_tok, (vb.astype(jnp.int32), ve.astype(jnp.int32), lo.astype(jnp.int32),
                                                   hi.astype(jnp.int32))


def _q_perm():
    cols = []
    for g in range(B_GROUP):
        for kv in range(B_KV_HEADS):
            h = kv * B_GROUP + g
            cols.extend(range(h * HEAD_DIM, (h + 1) * HEAD_DIM))
    return jnp.array(cols, dtype=jnp.int32)


def _layer(x_prompt, x_sample, state_rwkv, state_shift, cache_win_k, cache_win_v, prm, norm_final_g, *,
           chunk, n_sub, moe_block):
    bp, tp, d = x_prompt.shape
    bs, ts, _ = x_sample.shape
    n_p, n_s = bp * tp, bs * ts
    ts_pad = 8
    xp = x_prompt.reshape(n_p, d)
    xs = x_sample.reshape(n_s, d)

    qp = _q_perm()
    w_in = prm["w_in"]
    w_in = jnp.concatenate([w_in[:, :SHIFT_W], w_in[:, SHIFT_W:SHIFT_W + B_WIDTH][:, qp],
                            w_in[:, SHIFT_W + B_WIDTH:]], axis=1).astype(BF16)
    pa, q, k, v = _inproj_call(xp, xs, prm["norm_mix_g"][None], w_in)

    zero_blk = jnp.zeros((LORA_W // 2, A_WIDTH), F32)
    wlora = jnp.concatenate([jnp.concatenate([prm["w_decay_up"], zero_blk], axis=1),
                             jnp.concatenate([zero_blk, prm["w_iclr_up"]], axis=1)], axis=0).astype(BF16)
    rp = dict(mu=prm["mu_shift"][None], w0=prm["w_decay0"][None], a0=prm["w_iclr0"][None], wlora=wlora,
              wgate=prm["w_gate_up"].astype(BF16), kk=prm["k_k"][None], ka=prm["k_a"][None],
              rk=prm["r_k"].reshape(1, A_WIDTH), lng=prm["lnx_g"][None], lnb=prm["lnx_b"][None])

    oa_p, sbd_p = _rwkv_call(pa, jnp.zeros((bp, 1, SHIFT_W), F32), jnp.zeros((bp, N_PAIRS, LANE, LANE), F32), rp,
                             n_seq=bp, n_chunks=tp // chunk, chunk=chunk, n_sub=n_sub, t_real=chunk)
    pad_rows = lambda a: jnp.pad(a[n_p:].reshape(bs, ts, -1), ((0, 0), (0, ts_pad - ts), (0, 0))).reshape(
        bs * ts_pad, -1)
    oa_s, sbd_s = _rwkv_call(pad_rows(pa), state_shift[:, None, :], _state_to_pairs(state_rwkv), rp,
                             n_seq=bs, n_chunks=1, chunk=ts_pad, n_sub=1, t_real=ts)
    oa_s = oa_s.reshape(bs, ts_pad, A_WIDTH)[:, :ts].reshape(n_s, A_WIDTH)

    gn = prm["attn_norm_g"][qp][None]
    nb = tp // WINDOW
    k_p3 = k.reshape(-1, WINDOW, B_KV_WIDTH)
    v_p3 = v.reshape(-1, WINDOW, B_KV_WIDTH)
    ob_p = _swa_call(prm["attn_sink"], q, k_p3, v_p3, k, v, gn, n_seq=bp, n_blk=nb, tq=WINDOW, n_bat=1,
                     first_has_prev=False, prev_map=lambda b, i: (b * nb + jnp.maximum(i - 1, 0), 0, 0))
    ob_s = _swa_call(prm["attn_sink"], pad_rows(q), cache_win_k.reshape(bs, WINDOW, B_KV_WIDTH),
                     cache_win_v.reshape(bs, WINDOW, B_KV_WIDTH), pad_rows(k), pad_rows(v), gn,
                     n_seq=bs, n_blk=1, tq=ts_pad, n_bat=_row_tile(bs, 16), first_has_prev=True,
                     prev_map=lambda b, i: (b, 0, 0))
    ob_s = ob_s.reshape(bs, ts_pad, B_WIDTH)[:, :ts].reshape(n_s, B_WIDTH)

    w_out = prm["w_out"]
    pad_lanes = jnp.zeros((d, ROUTE_W - N_GROUPS - N_EXPERTS), F32)
    wr = jnp.concatenate([prm["w_route_group"], prm["w_route_expert"], pad_lanes], axis=1).astype(BF16)
    br = jnp.concatenate([prm["b_route_group"], prm["b_route_expert"], pad_lanes[0]])[None]
    x1, h2, route = _outproj_call(xp, xs, oa_p, oa_s, ob_p, ob_s, w_out[:A_WIDTH].astype(BF16),
                                  w_out[A_WIDTH:][qp].astype(BF16), prm["norm_ffn_g"][None], wr, br)

    dest, src_tok, visits = _dispatch(route[:, :TOP_K_INNER].astype(jnp.int32), moe_block)
    ybuf = _expert_call(visits, h2[src_tok], prm["w_exp_gate"], prm["w_exp_up"], prm["w_exp_down"], tb=moe_block)
    y0, y1 = ybuf[dest[:, 0]], ybuf[dest[:, 1]]
    gf = norm_final_g[None]
    y_p = _final_call(x1, y0, y1, route, gf, first_tile=0, n_tiles=n_p // n_s, tm=n_s)
    y_s = _final_call(x1, y0, y1, route, gf, first_tile=n_p // n_s, n_tiles=1, tm=n_s)

    kv4 = lambda a: a.reshape(a.shape[0], a.shape[1], B_KV_HEADS, HEAD_DIM)
    last_rows = lambda a, m: jnp.stack([a[(b + 1) * tp - m:(b + 1) * tp] for b in range(bp)], axis=0)
    wb = cache_win_k.shape[1]
    k_s = jnp.concatenate([cache_win_k.reshape(bs, wb, B_KV_WIDTH), k[n_p:].reshape(bs, ts, B_KV_WIDTH)],
                          axis=1)[:, -wb:]
    v_s = jnp.concatenate([cache_win_v.reshape(bs, wb, B_KV_WIDTH), v[n_p:].reshape(bs, ts, B_KV_WIDTH)],
                          axis=1)[:, -wb:]
    return (y_p.reshape(bp, tp, d), y_s.reshape(bs, ts, d),
            _pairs_to_state(sbd_p), last_rows(pa, 1)[:, 0], kv4(last_rows(k, WINDOW)), kv4(last_rows(v, WINDOW)),
            _pairs_to_state(sbd_s), pa[n_p:].reshape(bs, ts, SHIFT_W)[:, -1], kv4(k_s), kv4(v_s))


def kernel(x_prompt, x_sample, state_rwkv, state_shift, cache_win_k, cache_win_v, norm_mix_g, w_in, mu_shift, w_decay0, w_decay_up, w_iclr0, w_iclr_up, w_gate_up, k_k, k_a, r_k, lnx_g, lnx_b, attn_sink, attn_norm_g, w_out, norm_ffn_g, w_route_group, b_route_group, w_route_expert, b_route_expert, w_exp_gate, w_exp_up, w_exp_down, norm_final_g):
    assert norm_mix_g.shape[0] == 1, "single-layer trunk"
    prm = dict(norm_mix_g=norm_mix_g[0], w_in=w_in[0], mu_shift=mu_shift[0], w_decay0=w_decay0[0],
               w_decay_up=w_decay_up[0], w_iclr0=w_iclr0[0], w_iclr_up=w_iclr_up[0], w_gate_up=w_gate_up[0],
               k_k=k_k[0], k_a=k_a[0], r_k=r_k[0], lnx_g=lnx_g[0], lnx_b=lnx_b[0], attn_sink=attn_sink[0],
               attn_norm_g=attn_norm_g[0], w_out=w_out[0], norm_ffn_g=norm_ffn_g[0],
               w_route_group=w_route_group[0], b_route_group=b_route_group[0],
               w_route_expert=w_route_expert[0], b_route_expert=b_route_expert[0],
               w_exp_gate=w_exp_gate[0], w_exp_up=w_exp_up[0], w_exp_down=w_exp_down[0])
    outs = _layer(x_prompt, x_sample, state_rwkv[0], state_shift[0], cache_win_k[0], cache_win_v[0], prm,
                  norm_final_g, chunk=64, n_sub=2, moe_block=512)
    y_p, y_s, s_p, sh_p, kp, vp, s_s, sh_s, ks, vs = outs
    return (y_p, y_s, s_p[None], sh_p[None], kp[None], vp[None], s_s[None], sh_s[None], ks[None], vs[None])
```

```python
import functools

import jax
import jax.numpy as jnp
from jax import lax
from jax.experimental import pallas as pl
from jax.experimental.pallas import tpu as pltpu

F32 = jnp.float32
BF16 = jnp.bfloat16

HEAD_DIM = 64
A_HEADS = 8
A_WIDTH = A_HEADS * HEAD_DIM
B_HEADS = 8
B_KV_HEADS = 2
B_GROUP = B_HEADS // B_KV_HEADS
B_WIDTH = B_HEADS * HEAD_DIM
B_KV_WIDTH = B_KV_HEADS * HEAD_DIM
DECAY_LORA = 64
ICLR_LORA = 64
GATE_LORA = 128
LORA_W = DECAY_LORA + ICLR_LORA
SHIFT_W = 3 * A_WIDTH + LORA_W + GATE_LORA
IN_W = SHIFT_W + B_WIDTH + 2 * B_KV_WIDTH
WINDOW = 128
N_GROUPS = 4
EXPERTS_PER_GROUP = 4
N_EXPERTS = N_GROUPS * EXPERTS_PER_GROUP
TOP_K_INNER = 2
EXPERT_FF = 512
RMS_EPS = 1e-6
LNX_EPS = 64e-5
DECAY_OFFSET = 0.5

LANE = 128
PAIR = LANE // HEAD_DIM
N_PAIRS = A_HEADS // PAIR
ROUTE_W = LANE
MASK_NEG = -1e30
VMEM_LIMIT = 48 * 1024 * 1024

HEAD_SHIFT = HEAD_DIM.bit_length() - 1
HI = lax.Precision.HIGHEST


def _row_tile(n, cap):
    t = cap
    while n % t:
        t //= 2
    return t


def _dot(a, b, precision=None):
    return jnp.dot(a, b, preferred_element_type=F32, precision=precision)


def _dot_nt(a, b, precision=None):
    return lax.dot_general(a, b, (((1,), (1,)), ((), ())), preferred_element_type=F32, precision=precision)


def _dot_tn(a, b, precision=None):
    return lax.dot_general(a, b, (((0,), (0,)), ((), ())), preferred_element_type=F32, precision=precision)


def _sigmoid(x):
    return 1.0 / (1.0 + jnp.exp(-x))


def _hi_lo(x, axis):
    hi = x.astype(BF16)
    lo = (x - hi.astype(F32)).astype(BF16)
    return jnp.concatenate([hi, lo], axis=axis)


def _two_stream(i, n_p_tiles, run, prompt_refs, sample_refs):
    @pl.when(i < n_p_tiles)
    def _():
        run(*prompt_refs)

    @pl.when(i >= n_p_tiles)
    def _():
        run(*sample_refs)


def _stream_specs(tm, width, n_p_tiles):
    return [pl.BlockSpec((tm, width), lambda i: (jnp.minimum(i, n_p_tiles - 1), 0)),
            pl.BlockSpec((tm, width), lambda i: (0, 0))]


def _inproj_body(xp_ref, xs_ref, g_ref, w_ref, pa_ref, q_ref, k_ref, v_ref, *, n_p_tiles):
    def run(x_ref):
        x = x_ref[...]
        h = x * lax.rsqrt(jnp.mean(x * x, axis=-1, keepdims=True) + RMS_EPS) * g_ref[...]
        p = _dot(h.astype(BF16), w_ref[...])
        pa_ref[...] = p[:, :SHIFT_W]
        q_ref[...] = (p[:, SHIFT_W:SHIFT_W + B_WIDTH] * (HEAD_DIM ** -0.5)).astype(BF16)
        k_ref[...] = p[:, SHIFT_W + B_WIDTH:SHIFT_W + B_WIDTH + B_KV_WIDTH]
        v_ref[...] = p[:, SHIFT_W + B_WIDTH + B_KV_WIDTH:]

    _two_stream(pl.program_id(0), n_p_tiles, run, (xp_ref,), (xs_ref,))


def _inproj_call(xp, xs, g, w_bf16):
    n_p, d = xp.shape
    tm = xs.shape[0]
    assert n_p % tm == 0 and tm % 8 == 0, "sample rows must form one row tile that divides the prompt rows"
    n_p_tiles = n_p // tm
    n = n_p + tm
    row = lambda i: (i, 0)
    fixed = lambda i: (0, 0)
    return pl.pallas_call(
        functools.partial(_inproj_body, n_p_tiles=n_p_tiles),
        grid=(n_p_tiles + 1,),
        in_specs=_stream_specs(tm, d, n_p_tiles) + [pl.BlockSpec((1, d), fixed), pl.BlockSpec((d, IN_W), fixed)],
        out_specs=[pl.BlockSpec((tm, SHIFT_W), row), pl.BlockSpec((tm, B_WIDTH), row),
                   pl.BlockSpec((tm, B_KV_WIDTH), row), pl.BlockSpec((tm, B_KV_WIDTH), row)],
        out_shape=[jax.ShapeDtypeStruct((n, SHIFT_W), F32), jax.ShapeDtypeStruct((n, B_WIDTH), BF16),
                   jax.ShapeDtypeStruct((n, B_KV_WIDTH), F32), jax.ShapeDtypeStruct((n, B_KV_WIDTH), F32)],
        compiler_params=pltpu.CompilerParams(dimension_semantics=("arbitrary",), vmem_limit_bytes=VMEM_LIMIT),
        name="inproj",
    )(xp, xs, g, w_bf16)


def _rwkv_body(pa_ref, shift0_ref, s0_ref, mu_ref, w0_ref, a0_ref, wlora_ref, wgate_ref, kk_ref, ka_ref,
               rk_ref, lng_ref, lnb_ref, oa_ref, sout_ref, s_scr, prev_scr, *, chunk, n_sub, n_bat, t_real):
    C = chunk
    seq_rows = n_sub * C
    n_seg = n_bat * n_sub
    rows = n_seg * C
    c = pl.program_id(1)

    @pl.when(c == 0)
    def _():
        s_scr[...] = s0_ref[...]
        prev_scr[...] = shift0_ref[...]

    pa = pa_ref[...]
    row = lax.broadcasted_iota(jnp.int32, (rows, 1), 0)
    row_in_chunk = row & (C - 1)
    pa_prev = pltpu.roll(pa, 1, axis=0)
    for b in range(n_bat):
        pa_prev = jnp.where(row == b * seq_rows, prev_scr[b], pa_prev)
        last = (b + 1) * seq_rows - C + t_real - 1
        prev_scr[b] = pa[last:last + 1]
    xm = pa + mu_ref[...] * (pa_prev - pa)

    r = xm[:, :A_WIDTH]
    k = xm[:, A_WIDTH:2 * A_WIDTH]
    v = xm[:, 2 * A_WIDTH:3 * A_WIDTH]
    lora_in = xm[:, 3 * A_WIDTH:3 * A_WIDTH + LORA_W]
    gd = xm[:, 3 * A_WIDTH + LORA_W:]

    lane = lax.broadcasted_iota(jnp.int32, (1, LANE), 1)
    lo_half = lane < HEAD_DIM
    z = jnp.where(lo_half, jnp.tanh(lora_in), lora_in)
    lw = _dot(z.astype(BF16), wlora_ref[...])
    dec_pre = w0_ref[...] + lw[:, :A_WIDTH]
    a = _sigmoid(a0_ref[...] + lw[:, A_WIDTH:])
    neg = -dec_pre
    softplus = jnp.maximum(neg, 0.0) + jnp.log(1.0 + jnp.exp(-jnp.abs(neg)))
    logdec = -jnp.exp(-softplus - DECAY_OFFSET)
    g = _dot(_sigmoid(gd).astype(BF16), wgate_ref[...])

    ri = lax.broadcasted_iota(jnp.int32, (LANE, LANE), 0)
    ci = lax.broadcasted_iota(jnp.int32, (LANE, LANE), 1)
    seg = jnp.where((ri >> HEAD_SHIFT) == (ci >> HEAD_SHIFT), 1.0, 0.0).astype(BF16)
    seg2 = jnp.concatenate([seg, seg], axis=0)

    def headsum(x):
        return jnp.concatenate([_dot(_hi_lo(x[:, j * LANE:(j + 1) * LANE], 1), seg2) for j in range(N_PAIRS)],
                               axis=1)

    kk = k * kk_ref[...]
    kk = kk * lax.rsqrt(jnp.maximum(headsum(kk * kk), 1e-24))
    k = k * (1.0 + (a - 1.0) * ka_ref[...])
    bonus = headsum(r * k * rk_ref[...]) * v

    if t_real < C:
        valid = row_in_chunk < t_real
        logdec = jnp.where(valid, logdec, 0.0)
        kk = jnp.where(valid, kk, 0.0)
        k = jnp.where(valid, k, 0.0)
        v = jnp.where(valid, v, 0.0)

    cum = logdec
    shift = 1
    while shift < C:
        cum = cum + jnp.where(row_in_chunk >= shift, pltpu.roll(cum, shift, axis=0), 0.0)
        shift *= 2
    ends = [cum[(s + 1) * C - 1:(s + 1) * C] for s in range(n_seg)]
    cum_end = jnp.concatenate([jnp.broadcast_to(e, (C, A_WIDTH)) for e in ends], axis=0) if n_seg > 1 else ends[0]
    w_incl = jnp.exp(cum)
    w_prev = jnp.exp(cum - logdec)
    w_inv = jnp.exp(-cum)
    w_end = jnp.exp(cum_end - cum)
    w_chunk = [jnp.exp(e) for e in ends]
    kka = kk * a
    terms = dict(A=-kk * w_prev, R=r * w_incl, B=kka * w_inv, K=k * w_inv, V=v, Be=kka * w_end, Ke=k * w_end)

    C2 = PAIR * C
    t_idx = lax.broadcasted_iota(jnp.int32, (C, 1), 0)
    i_idx = lax.broadcasted_iota(jnp.int32, (1, C2), 1) & (C - 1)
    strict = i_idx < t_idx
    incl = i_idx <= t_idx
    ident = jnp.where(i_idx == t_idx, 1.0, 0.0).astype(F32)
    lo_time = lax.broadcasted_iota(jnp.int32, (1, C2), 1) < C
    n_levels = max(1, (C - 1).bit_length())
    bf = lambda x: x.astype(BF16)

    def bd(x, lo_mask=lo_half):
        return bf(jnp.concatenate([jnp.where(lo_mask, x, 0.0), jnp.where(lo_mask, 0.0, x)], axis=0))

    chains = [(s, j) for s in range(n_seg) for j in range(N_PAIRS)]
    tile = lambda name, s, j: terms[name][s * C:(s + 1) * C, j * LANE:(j + 1) * LANE]
    r_sbs = {ch: tile("R", *ch) for ch in chains}
    stk = {ch: {nm: bd(tile(nm, *ch)) for nm in ("A", "B", "K", "V", "Be", "Ke")} for ch in chains}
    m_ab, aak, arb, ark = {}, {}, {}, {}
    for ch in chains:
        t = stk[ch]
        a_sbs, rb = bf(tile("A", *ch)), bf(r_sbs[ch])
        if C2 % LANE == 0:
            m1 = _dot_nt(jnp.concatenate([a_sbs, rb], axis=0), jnp.concatenate([t["B"], t["K"]], axis=0))
            ab, ak, rbm, rk = m1[:C, :C2], m1[:C, C2:], m1[C:, :C2], m1[C:, C2:]
        else:
            ab, ak = _dot_nt(a_sbs, t["B"]), _dot_nt(a_sbs, t["K"])
            rbm, rk = _dot_nt(rb, t["B"]), _dot_nt(rb, t["K"])
        m_ab[ch] = jnp.where(strict, ab, 0.0)
        aak[ch] = bf(jnp.where(strict, ak, 0.0))
        arb[ch] = bf(jnp.where(incl, rbm, 0.0))
        ark[ch] = bf(jnp.where(incl, rk, 0.0))

    nn = dict(m_ab)
    tinv = {ch: ident + m_ab[ch] for ch in chains}
    for _ in range(1, n_levels):
        for ch in chains:
            nn[ch] = _dot(bf(nn[ch]), bd(nn[ch], lo_time))
        for ch in chains:
            tinv[ch] = tinv[ch] + _dot(bf(nn[ch]), bd(tinv[ch], lo_time))
    akv = {ch: _dot(aak[ch], stk[ch]["V"]) for ch in chains}
    x = {ch: _dot(bf(tinv[ch]), jnp.concatenate([stk[ch]["A"], bd(akv[ch])], axis=1)) for ch in chains}
    a_eff = {ch: bd(x[ch][:, :LANE]) for ch in chains}
    v_eff = {ch: bd(x[ch][:, LANE:]) for ch in chains}
    zed = {ch: _dot(arb[ch], jnp.concatenate([a_eff[ch], v_eff[ch]], axis=1)) for ch in chains}
    rhat = {ch: bf(r_sbs[ch] + zed[ch][:, :LANE]) for ch in chains}
    y0 = {ch: _dot(ark[ch], stk[ch]["V"]) + zed[ch][:, LANE:] for ch in chains}
    p_mat = {ch: bf(_dot_tn(stk[ch]["Be"], a_eff[ch])) for ch in chains}
    q_bd = {ch: _dot_tn(jnp.concatenate([v_eff[ch], stk[ch]["V"]], axis=0),
                        jnp.concatenate([stk[ch]["Be"], stk[ch]["Ke"]], axis=0)) for ch in chains}

    y_rows = []
    for b in range(n_bat):
        state = [s_scr[b * N_PAIRS + j] for j in range(N_PAIRS)]
        for sub in range(n_sub):
            s = b * n_sub + sub
            ys = []
            for j in range(N_PAIRS):
                ch = (s, j)
                hi = bf(state[j])
                lo = bf(state[j] - hi.astype(F32))
                yy = _dot_nt(rhat[ch], jnp.concatenate([bd(hi), bd(lo)], axis=0))
                ys.append(yy[:, :LANE] + yy[:, LANE:] + y0[ch])
                sp = _dot_nt(jnp.concatenate([hi, lo], axis=0), p_mat[ch])
                q_sbs = q_bd[ch][:HEAD_DIM] + q_bd[ch][HEAD_DIM:]
                state[j] = state[j] * w_chunk[s][:, j * LANE:(j + 1) * LANE] + sp[:HEAD_DIM] + sp[HEAD_DIM:] + q_sbs
            y_rows.append(jnp.concatenate(ys, axis=1))
        for j in range(N_PAIRS):
            s_scr[b * N_PAIRS + j] = state[j]

    y = jnp.concatenate(y_rows, axis=0) if n_seg > 1 else y_rows[0]
    inv_n = 1.0 / HEAD_DIM
    mean = headsum(y) * inv_n
    d = y - mean
    var = headsum(d * d) * inv_n
    yn = d * lax.rsqrt(var + LNX_EPS) * lng_ref[...] + lnb_ref[...]
    oa_ref[...] = ((yn + bonus) * g).astype(oa_ref.dtype)
    sout_ref[...] = s_scr[...]


def _rwkv_call(pa2d, shift0, s0_pairs, prm, *, n_seq, n_chunks, chunk, n_sub, n_bat, t_real):
    n_steps = n_chunks // n_sub
    assert n_seq % n_bat == 0 and (n_bat == 1 or n_steps == 1)
    row = lambda b, c: (b * n_steps + c, 0)
    seq3 = lambda b, c: (b, 0, 0)
    fixed = lambda b, c: (0, 0)
    vec = lambda w: pl.BlockSpec((1, w), fixed)
    body = functools.partial(_rwkv_body, chunk=chunk, n_sub=n_sub, n_bat=n_bat, t_real=t_real)
    n_rows = n_seq * n_chunks * chunk
    blk_rows = n_bat * n_sub * chunk
    state_blk = (n_bat * N_PAIRS, HEAD_DIM, LANE)
    return pl.pallas_call(
        body,
        grid=(n_seq // n_bat, n_steps),
        in_specs=[pl.BlockSpec((blk_rows, SHIFT_W), row),
                  pl.BlockSpec((n_bat, 1, SHIFT_W), seq3),
                  pl.BlockSpec(state_blk, seq3),
                  vec(SHIFT_W), vec(A_WIDTH), vec(A_WIDTH),
                  pl.BlockSpec((LORA_W, 2 * A_WIDTH), fixed), pl.BlockSpec((GATE_LORA, A_WIDTH), fixed),
                  vec(A_WIDTH), vec(A_WIDTH), vec(A_WIDTH), vec(A_WIDTH), vec(A_WIDTH)],
        out_specs=[pl.BlockSpec((blk_rows, A_WIDTH), row), pl.BlockSpec(state_blk, seq3)],
        out_shape=[jax.ShapeDtypeStruct((n_rows, A_WIDTH), BF16),
                   jax.ShapeDtypeStruct((n_seq * N_PAIRS, HEAD_DIM, LANE), F32)],
        scratch_shapes=[pltpu.VMEM(state_blk, F32), pltpu.VMEM((n_bat, 1, SHIFT_W), F32)],
        compiler_params=pltpu.CompilerParams(dimension_semantics=("arbitrary", "arbitrary"),
                                             vmem_limit_bytes=VMEM_LIMIT),
        name="rwkv",
    )(pa2d, shift0, s0_pairs, prm["mu"], prm["w0"], prm["a0"], prm["wlora"], prm["wgate"], prm["kk"], prm["ka"],
      prm["rk"], prm["lng"], prm["lnb"])


def _state_to_pairs(state):
    b = state.shape[0]
    st = state.reshape(b, N_PAIRS, PAIR, HEAD_DIM, HEAD_DIM)
    return jnp.swapaxes(st, 2, 3).reshape(b * N_PAIRS, HEAD_DIM, LANE)


def _pairs_to_state(sbs):
    b = sbs.shape[0] // N_PAIRS
    st = sbs.reshape(b, N_PAIRS, HEAD_DIM, PAIR, HEAD_DIM)
    return jnp.swapaxes(st, 2, 3).reshape(b, A_HEADS, HEAD_DIM, HEAD_DIM)


def _swa_bias(tq):
    rows = B_GROUP * tq
    grp = jnp.arange(rows, dtype=jnp.int32)[:, None] // tq
    t = jnp.arange(rows, dtype=jnp.int32)[:, None] % tq
    dist_p = t + WINDOW - jnp.arange(WINDOW, dtype=jnp.int32)[None, :]
    dist_c = t - jnp.arange(tq, dtype=jnp.int32)[None, :]

    def bias(dist, kv):
        slope = sum(jnp.where(grp == g, 2.0 ** -(kv * B_GROUP + g + 1), 0.0) for g in range(B_GROUP))
        return jnp.where((dist >= 0) & (dist < WINDOW), -slope * dist.astype(F32), MASK_NEG)

    return (jnp.stack([bias(dist_p, kv) for kv in range(B_KV_HEADS)]),
            jnp.stack([bias(dist_c, kv) for kv in range(B_KV_HEADS)]))


def _swa_body(sink_ref, q_ref, kp_ref, vp_ref, kc_ref, vc_ref, bp_ref, bc_ref, gn_ref, o_ref, *, tq, n_bat,
              first_has_prev):
    for s in range(n_bat):
        rs = slice(s * tq, (s + 1) * tq)
        _swa_one(sink_ref, q_ref[rs], kp_ref[s], vp_ref[s], kc_ref[rs], vc_ref[rs], bp_ref, bc_ref, gn_ref,
                 o_ref.at[rs], tq=tq, first_has_prev=first_has_prev)


def _swa_one(sink_ref, q, kp, vp, kc, vc, bp_ref, bc_ref, gn_ref, o_ref, *, tq, first_has_prev):
    nblk = pl.program_id(1)
    kp = kp.astype(BF16)
    kc = kc.astype(BF16)
    rows = B_GROUP * tq
    lane = lax.broadcasted_iota(jnp.int32, (1, LANE), 1)
    grp = lax.broadcasted_iota(jnp.int32, (rows, 1), 0) >> (tq.bit_length() - 1)
    prev_pen = 0.0 if first_has_prev else jnp.where(nblk > 0, 0.0, MASK_NEG)

    out = None
    for kv in range(B_KV_HEADS):
        mk = (lane < HEAD_DIM) if kv == 0 else (lane >= HEAD_DIM)
        q_st = jnp.concatenate([jnp.where(mk, q[:, g * LANE:(g + 1) * LANE], 0.0) for g in range(B_GROUP)],
                               axis=0).astype(BF16)
        sink = jnp.zeros((rows, 1), F32)
        for g in range(B_GROUP):
            sink = jnp.where(grp == g, sink_ref[kv * B_GROUP + g], sink)
        s_p = _dot_nt(q_st, kp) + (bp_ref[kv] + prev_pen)
        s_c = _dot_nt(q_st, kc) + bc_ref[kv]
        if tq == WINDOW:
            m = jnp.maximum(jnp.max(jnp.maximum(s_p, s_c), axis=-1, keepdims=True), sink)
        else:
            m = jnp.maximum(jnp.maximum(jnp.max(s_p, axis=-1, keepdims=True),
                                        jnp.max(s_c, axis=-1, keepdims=True)), sink)
        e_p = jnp.exp(s_p - m)
        e_c = jnp.exp(s_c - m)
        if tq == WINDOW:
            e_sum = jnp.sum(e_p + e_c, axis=-1, keepdims=True)
        else:
            e_sum = jnp.sum(e_p, axis=-1, keepdims=True) + jnp.sum(e_c, axis=-1, keepdims=True)
        inv = 1.0 / (e_sum + jnp.exp(sink - m))
        o_kv = (_dot(e_p.astype(BF16), jnp.where(mk, vp, 0.0).astype(BF16))
                + _dot(e_c.astype(BF16), jnp.where(mk, vc, 0.0).astype(BF16))) * inv
        out = o_kv if out is None else out + o_kv

    tiles = [out[g * tq:(g + 1) * tq] for g in range(B_GROUP)]
    ssq = sum(jnp.sum(x * x, axis=-1, keepdims=True) for x in tiles)
    inv_rms = lax.rsqrt(ssq * (1.0 / B_WIDTH) + RMS_EPS)
    o_ref[...] = (jnp.concatenate(tiles, axis=1) * inv_rms * gn_ref[...]).astype(o_ref.dtype)


def _swa_call(sink, q2d, kprev, vprev, kcur2d, vcur2d, gn, *, n_seq, n_blk, tq, n_bat, first_has_prev, prev_map,
              out_dtype):
    assert n_seq % n_bat == 0 and (n_bat == 1 or n_blk == 1)
    row = lambda b, n: (b * n_blk + n, 0)
    body = functools.partial(_swa_body, tq=tq, n_bat=n_bat, first_has_prev=first_has_prev)
    bias_p, bias_c = _swa_bias(tq)
    whole = lambda b, n: (0, 0, 0)
    return pl.pallas_call(
        body,
        grid=(n_seq // n_bat, n_blk),
        in_specs=[pl.BlockSpec(memory_space=pltpu.SMEM),
                  pl.BlockSpec((n_bat * tq, B_WIDTH), row),
                  pl.BlockSpec((n_bat, WINDOW, B_KV_WIDTH), prev_map),
                  pl.BlockSpec((n_bat, WINDOW, B_KV_WIDTH), prev_map),
                  pl.BlockSpec((n_bat * tq, B_KV_WIDTH), row),
                  pl.BlockSpec((n_bat * tq, B_KV_WIDTH), row),
                  pl.BlockSpec(bias_p.shape, whole),
                  pl.BlockSpec(bias_c.shape, whole),
                  pl.BlockSpec((1, B_WIDTH), lambda b, n: (0, 0))],
        out_specs=pl.BlockSpec((n_bat * tq, B_WIDTH), row),
        out_shape=jax.ShapeDtypeStruct((n_seq * n_blk * tq, B_WIDTH), out_dtype),
        compiler_params=pltpu.CompilerParams(dimension_semantics=("arbitrary", "arbitrary"),
                                             vmem_limit_bytes=VMEM_LIMIT),
        name="swa",
    )(sink, q2d, kprev, vprev, kcur2d, vcur2d, bias_p, bias_c, gn)


def _route_rows(lg):
    lane = lax.broadcasted_iota(jnp.int32, (1, ROUTE_W), 1)
    lane_f = lane.astype(F32)
    no_lane = float(ROUTE_W)
    is_group = lane < N_GROUPS
    m_g = jnp.max(jnp.where(is_group, lg, MASK_NEG), axis=-1, keepdims=True)
    g_idx = jnp.min(jnp.where(is_group & (lg == m_g), lane_f, no_lane), axis=-1, keepdims=True)
    p_group = 1.0 / jnp.sum(jnp.where(is_group, jnp.exp(lg - m_g), 0.0), axis=-1, keepdims=True)
    e_lane = lane - N_GROUPS
    in_group = (e_lane >= 0) & (e_lane < N_EXPERTS) & ((e_lane >> 2).astype(F32) == g_idx)
    m_1 = jnp.max(jnp.where(in_group, lg, MASK_NEG), axis=-1, keepdims=True)
    i_1 = jnp.min(jnp.where(in_group & (lg == m_1), lane_f, no_lane), axis=-1, keepdims=True)
    rest = in_group & (lane_f != i_1)
    m_2 = jnp.max(jnp.where(rest, lg, MASK_NEG), axis=-1, keepdims=True)
    i_2 = jnp.min(jnp.where(rest & (lg == m_2), lane_f, no_lane), axis=-1, keepdims=True)
    ratio = jnp.exp(m_2 - m_1)
    w_1 = p_group / (1.0 + ratio)
    return jnp.where(lane == 0, i_1 - N_GROUPS,
                     jnp.where(lane == 1, i_2 - N_GROUPS,
                               jnp.where(lane == 2, w_1, jnp.where(lane == 3, w_1 * ratio, 0.0))))


def _outproj_body(xp_ref, xs_ref, oap_ref, oas_ref, obp_ref, obs_ref, wa_ref, wb_ref, g_ref, wr_ref, br_ref,
                  x1_ref, h2_ref, rt_ref, *, n_p_tiles):
    def run(x_ref, oa_ref, ob_ref):
        x1 = x_ref[...] + _dot(oa_ref[...].astype(BF16), wa_ref[...]) + _dot(ob_ref[...].astype(BF16), wb_ref[...])
        x1_ref[...] = x1
        h2 = x1 * lax.rsqrt(jnp.mean(x1 * x1, axis=-1, keepdims=True) + RMS_EPS) * g_ref[...]
        h2b = h2.astype(BF16)
        h2_ref[...] = h2b
        rt_ref[...] = _route_rows(_dot(h2b, wr_ref[...]) + br_ref[...])

    _two_stream(pl.program_id(0), n_p_tiles, run, (xp_ref, oap_ref, obp_ref), (xs_ref, oas_ref, obs_ref))


def _outproj_call(xp, xs, oap, oas, obp, obs, wa, wb, g, wr, br):
    n_p, d = xp.shape
    tm = xs.shape[0]
    n_p_tiles = n_p // tm
    n = n_p + tm
    row = lambda i: (i, 0)
    fixed = lambda i: (0, 0)
    return pl.pallas_call(
        functools.partial(_outproj_body, n_p_tiles=n_p_tiles),
        grid=(n_p_tiles + 1,),
        in_specs=(_stream_specs(tm, d, n_p_tiles) + _stream_specs(tm, A_WIDTH, n_p_tiles)
                  + _stream_specs(tm, B_WIDTH, n_p_tiles)
                  + [pl.BlockSpec((A_WIDTH, d), fixed), pl.BlockSpec((B_WIDTH, d), fixed), pl.BlockSpec((1, d), fixed),
                     pl.BlockSpec((d, ROUTE_W), fixed), pl.BlockSpec((1, ROUTE_W), fixed)]),
        out_specs=[pl.BlockSpec((tm, d), row), pl.BlockSpec((tm, d), row), pl.BlockSpec((tm, ROUTE_W), row)],
        out_shape=[jax.ShapeDtypeStruct((n, d), F32), jax.ShapeDtypeStruct((n, d), BF16),
                   jax.ShapeDtypeStruct((n, ROUTE_W), F32)],
        compiler_params=pltpu.CompilerParams(dimension_semantics=("arbitrary",), vmem_limit_bytes=VMEM_LIMIT),
        name="outproj",
    )(xp, xs, oap, oas, obp, obs, wa, wb, g, wr, br)


def _expert_body(vb_ref, ve_ref, lo_ref, hi_ref, x_ref, wg_ref, wu_ref, wd_ref, y_ref, wg_bf, wu_bf, wd_bf):
    v = pl.program_id(0)
    pv = jnp.maximum(v - 1, 0)
    lo, hi = lo_ref[v], hi_ref[v]
    first_visit = (v == 0) | (vb_ref[v] != vb_ref[pv])

    @pl.when((v == 0) | (ve_ref[v] != ve_ref[pv]))
    def _():
        wg_bf[...] = wg_ref[0].astype(BF16)
        wu_bf[...] = wu_ref[0].astype(BF16)
        wd_bf[...] = wd_ref[0].astype(BF16)

    @pl.when(hi > lo)
    def _():
        x = x_ref[...]
        gate = _dot(x, wg_bf[...])
        up = _dot(x, wu_bf[...])
        mid = (gate * _sigmoid(gate) * up).astype(BF16)
        y = _dot(mid, wd_bf[...])
        row = lax.broadcasted_iota(jnp.int32, (y.shape[0], 1), 0)
        mine = (row >= lo) & (row < hi)

        @pl.when(first_visit)
        def _():
            y_ref[...] = jnp.where(mine, y, 0.0)

        @pl.when(jnp.logical_not(first_visit))
        def _():
            y_ref[...] = jnp.where(mine, y, y_ref[...])


def _expert_call(visits, xs, wg, wu, wd, *, tb):
    n_rows, d = xs.shape
    ff = wg.shape[-1]
    vb, ve, lo, hi = visits
    blk = lambda v, vb, ve, lo, hi: (vb[v], 0)
    wsel = lambda v, vb, ve, lo, hi: (ve[v], 0, 0)
    grid_spec = pltpu.PrefetchScalarGridSpec(
        num_scalar_prefetch=4,
        grid=(vb.shape[0],),
        in_specs=[pl.BlockSpec((tb, d), blk), pl.BlockSpec((1, d, ff), wsel), pl.BlockSpec((1, d, ff), wsel),
                  pl.BlockSpec((1, ff, d), wsel)],
        out_specs=pl.BlockSpec((tb, d), blk),
        scratch_shapes=[pltpu.VMEM((d, ff), BF16), pltpu.VMEM((d, ff), BF16), pltpu.VMEM((ff, d), BF16)],
    )
    return pl.pallas_call(
        _expert_body,
        grid_spec=grid_spec,
        out_shape=jax.ShapeDtypeStruct((n_rows, d), F32),
        compiler_params=pltpu.CompilerParams(dimension_semantics=("arbitrary",), vmem_limit_bytes=VMEM_LIMIT),
        name="experts",
    )(vb, ve, lo, hi, xs, wg, wu, wd)


def _final_body(x1_ref, y0_ref, y1_ref, rt_ref, g_ref, o_ref):
    rt = rt_ref[...]
    x = x1_ref[...] + (rt[:, 2:3] * y0_ref[...] + rt[:, 3:4] * y1_ref[...])
    o_ref[...] = x * lax.rsqrt(jnp.mean(x * x, axis=-1, keepdims=True) + RMS_EPS) * g_ref[...]


def _final_call(x1, y0, y1, route, g, *, first_tile, n_tiles, tm):
    d = x1.shape[1]
    src = lambda i: (first_tile + i, 0)
    return pl.pallas_call(
        _final_body,
        grid=(n_tiles,),
        in_specs=[pl.BlockSpec((tm, d), src), pl.BlockSpec((tm, d), src), pl.BlockSpec((tm, d), src),
                  pl.BlockSpec((tm, ROUTE_W), src), pl.BlockSpec((1, d), lambda i: (0, 0))],
        out_specs=pl.BlockSpec((tm, d), lambda i: (i, 0)),
        out_shape=jax.ShapeDtypeStruct((n_tiles * tm, d), F32),
        compiler_params=pltpu.CompilerParams(dimension_semantics=("arbitrary",), vmem_limit_bytes=VMEM_LIMIT),
        name="final",
    )(x1, y0, y1, route, g)


def _dispatch(e_idx, tb):
    n = e_idx.shape[0]
    n_assign = n * TOP_K_INNER
    idx_bits = (n_assign - 1).bit_length()
    assert N_EXPERTS << idx_bits < 2 ** 31 and n_assign % tb == 0
    flat_e = e_idx.reshape(n_assign)
    ids = jnp.arange(n_assign, dtype=jnp.int32)
    sorted_key = jnp.sort((flat_e << idx_bits) | ids)
    src_tok = (sorted_key & ((1 << idx_bits) - 1)) // TOP_K_INNER

    experts = jnp.arange(N_EXPERTS, dtype=jnp.int32)
    onehot = (flat_e[:, None] == experts[None, :]).astype(jnp.int32)
    csum = jnp.cumsum(onehot, axis=0)
    counts = csum[-1]
    ends = jnp.cumsum(counts)
    starts = ends - counts
    dest = (jnp.sum((csum - 1 + starts[None, :]) * onehot, axis=1)).astype(jnp.int32)

    n_blocks = n_assign // tb
    n_visits = n_blocks + N_EXPERTS - 1
    first_blk = starts // tb
    n_vis_e = jnp.where(counts > 0, (ends + tb - 1) // tb - first_blk, 0)
    v_end = jnp.cumsum(n_vis_e)
    v = jnp.arange(n_visits, dtype=jnp.int32)
    valid = v < v_end[-1]
    last_e = jnp.max(jnp.where(counts > 0, experts, 0))
    ve = jnp.where(valid, jnp.sum((v_end[None, :] <= v[:, None]).astype(jnp.int32), axis=1), last_e)
    pick = lambda a: jnp.sum(jnp.where(ve[:, None] == experts[None, :], a[None, :], 0), axis=1)
    vb = pick(first_blk) + v - pick(v_end - n_vis_e)
    lo = jnp.maximum(pick(starts), vb * tb) - vb * tb
    hi = jnp.minimum(pick(ends), (vb + 1) * tb) - vb * tb
    vb = jnp.where(valid, vb, n_blocks - 1)
    lo = jnp.where(valid, lo, 0)
    hi = jnp.where(valid, hi, 0)
    return dest.reshape(n, TOP_K_INNER), src_tok, (vb.astype(jnp.int32), ve.astype(jnp.int32), lo.astype(jnp.int32),
                                                   hi.astype(jnp.int32))


def _q_perm():
    cols = []
    for g in range(B_GROUP):
        for kv in range(B_KV_HEADS):
            h = kv * B_GROUP + g
            cols.extend(range(h * HEAD_DIM, (h + 1) * HEAD_DIM))
    return jnp.array(cols, dtype=jnp.int32)


def _layer(x_prompt, x_sample, state_rwkv, state_shift, cache_win_k, cache_win_v, prm, norm_final_g, *,
           chunk, n_sub, moe_block):
    bp, tp, d = x_prompt.shape
    bs, ts, _ = x_sample.shape
    n_p, n_s = bp * tp, bs * ts
    ts_pad = 8
    xp = x_prompt.reshape(n_p, d)
    xs = x_sample.reshape(n_s, d)

    qp = _q_perm()
    w_in = prm["w_in"]
    w_in = jnp.concatenate([w_in[:, :SHIFT_W], w_in[:, SHIFT_W:SHIFT_W + B_WIDTH][:, qp],
                            w_in[:, SHIFT_W + B_WIDTH:]], axis=1).astype(BF16)
    pa, q, k, v = _inproj_call(xp, xs, prm["norm_mix_g"][None], w_in)

    zero_blk = jnp.zeros((LORA_W // 2, A_WIDTH), F32)
    wlora = jnp.concatenate([jnp.concatenate([prm["w_decay_up"], zero_blk], axis=1),
                             jnp.concatenate([zero_blk, prm["w_iclr_up"]], axis=1)], axis=0).astype(BF16)
    rp = dict(mu=prm["mu_shift"][None], w0=prm["w_decay0"][None], a0=prm["w_iclr0"][None], wlora=wlora,
              wgate=prm["w_gate_up"].astype(BF16), kk=prm["k_k"][None], ka=prm["k_a"][None],
              rk=prm["r_k"].reshape(1, A_WIDTH), lng=prm["lnx_g"][None], lnb=prm["lnx_b"][None])

    oa_p, sbd_p = _rwkv_call(pa, jnp.zeros((bp, 1, SHIFT_W), F32), jnp.zeros((bp * N_PAIRS, HEAD_DIM, LANE), F32),
                             rp, n_seq=bp, n_chunks=tp // chunk, chunk=chunk, n_sub=n_sub, n_bat=1, t_real=chunk)
    pad_rows = lambda a: jnp.pad(a[n_p:].reshape(bs, ts, -1), ((0, 0), (0, ts_pad - ts), (0, 0))).reshape(
        bs * ts_pad, -1)
    oa_s, sbd_s = _rwkv_call(pad_rows(pa), state_shift[:, None, :], _state_to_pairs(state_rwkv), rp, n_seq=bs,
                             n_chunks=1, chunk=ts_pad, n_sub=1, n_bat=_row_tile(bs, 8), t_real=ts)
    oa_s = oa_s.reshape(bs, ts_pad, A_WIDTH)[:, :ts].reshape(n_s, A_WIDTH)

    gn = prm["attn_norm_g"][qp][None]
    nb = tp // WINDOW
    k_p3 = k.reshape(-1, WINDOW, B_KV_WIDTH)
    v_p3 = v.reshape(-1, WINDOW, B_KV_WIDTH)
    ob_p = _swa_call(prm["attn_sink"], q, k_p3, v_p3, k, v, gn, n_seq=bp, n_blk=nb, tq=WINDOW, n_bat=1,
                     first_has_prev=False, prev_map=lambda b, i: (b * nb + jnp.maximum(i - 1, 0), 0, 0),
                     out_dtype=BF16)
    ob_s = _swa_call(prm["attn_sink"], pad_rows(q).astype(F32), cache_win_k.reshape(bs, WINDOW, B_KV_WIDTH),
                     cache_win_v.reshape(bs, WINDOW, B_KV_WIDTH), pad_rows(k), pad_rows(v), gn,
                     n_seq=bs, n_blk=1, tq=ts_pad, n_bat=_row_tile(bs, 16), first_has_prev=True,
                     prev_map=lambda b, i: (b, 0, 0), out_dtype=F32)
    ob_s = ob_s.reshape(bs, ts_pad, B_WIDTH)[:, :ts].reshape(n_s, B_WIDTH)

    w_out = prm["w_out"]
    pad_lanes = jnp.zeros((d, ROUTE_W - N_GROUPS - N_EXPERTS), F32)
    wr = jnp.concatenate([prm["w_route_group"], prm["w_route_expert"], pad_lanes], axis=1).astype(BF16)
    br = jnp.concatenate([prm["b_route_group"], prm["b_route_expert"], pad_lanes[0]])[None]
    x1, h2, route = _outproj_call(xp, xs, oa_p, oa_s, ob_p, ob_s, w_out[:A_WIDTH].astype(BF16),
                                  w_out[A_WIDTH:][qp].astype(BF16), prm["norm_ffn_g"][None], wr, br)

    dest, src_tok, visits = _dispatch(route[:, :TOP_K_INNER].astype(jnp.int32), moe_block)
    ybuf = _expert_call(visits, h2[src_tok], prm["w_exp_gate"], prm["w_exp_up"], prm["w_exp_down"], tb=moe_block)
    y0, y1 = ybuf[dest[:, 0]], ybuf[dest[:, 1]]
    gf = norm_final_g[None]
    y_p = _final_call(x1, y0, y1, route, gf, first_tile=0, n_tiles=n_p // n_s, tm=n_s)
    y_s = _final_call(x1, y0, y1, route, gf, first_tile=n_p // n_s, n_tiles=1, tm=n_s)

    kv4 = lambda a: a.reshape(a.shape[0], a.shape[1], B_KV_HEADS, HEAD_DIM)
    last_rows = lambda a, m: jnp.stack([a[(b + 1) * tp - m:(b + 1) * tp] for b in range(bp)], axis=0)
    wb = cache_win_k.shape[1]
    k_s = jnp.concatenate([cache_win_k.reshape(bs, wb, B_KV_WIDTH), k[n_p:].reshape(bs, ts, B_KV_WIDTH)],
                          axis=1)[:, -wb:]
    v_s = jnp.concatenate([cache_win_v.reshape(bs, wb, B_KV_WIDTH), v[n_p:].reshape(bs, ts, B_KV_WIDTH)],
                          axis=1)[:, -wb:]
    return (y_p.reshape(bp, tp, d), y_s.reshape(bs, ts, d),
            _pairs_to_state(sbd_p), last_rows(pa, 1)[:, 0], kv4(last_rows(k, WINDOW)), kv4(last_rows(v, WINDOW)),
            _pairs_to_state(sbd_s), pa[n_p:].reshape(bs, ts, SHIFT_W)[:, -1], kv4(k_s), kv4(v_s))


def kernel(x_prompt, x_sample, state_rwkv, state_shift, cache_win_k, cache_win_v, norm_mix_g, w_in, mu_shift, w_decay0, w_decay_up, w_iclr0, w_iclr_up, w_gate_up, k_k, k_a, r_k, lnx_g, lnx_b, attn_sink, attn_norm_g, w_out, norm_ffn_g, w_route_group, b_route_group, w_route_expert, b_route_expert, w_exp_gate, w_exp_up, w_exp_down, norm_final_g):
    assert norm_mix_g.shape[0] == 1, "single-layer trunk"
    prm = dict(norm_mix_g=norm_mix_g[0], w_in=w_in[0], mu_shift=mu_shift[0], w_decay0=w_decay0[0],
               w_decay_up=w_decay_up[0], w_iclr0=w_iclr0[0], w_iclr_up=w_iclr_up[0], w_gate_up=w_gate_up[0],
               k_k=k_k[0], k_a=k_a[0], r_k=r_k[0], lnx_g=lnx_g[0], lnx_b=lnx_b[0], attn_sink=attn_sink[0],
               attn_norm_g=attn_norm_g[0], w_out=w_out[0], norm_ffn_g=norm_ffn_g[0],
               w_route_group=w_route_group[0], b_route_group=b_route_group[0],
               w_route_expert=w_route_expert[0], b_route_expert=b_route_expert[0],
               w_exp_gate=w_exp_gate[0], w_exp_up=w_exp_up[0], w_exp_down=w_exp_down[0])
    outs = _layer(x_prompt, x_sample, state_rwkv[0], state_shift[0], cache_win_k[0], cache_win_v[0], prm,
                  norm_final_g, chunk=64, n_sub=4, moe_block=512)
    y_p, y_s, s_p, sh_p, kp, vp, s_s, sh_s, ks, vs = outs
    return (y_p, y_s, s_p[None], sh_p[None], kp[None], vp[None], s_s[None], sh_s[None], ks[None], vs[None])
```

```python
import functools

import jax
import jax.numpy as jnp
from jax import lax
from jax.experimental import pallas as pl
from jax.experimental.pallas import tpu as pltpu

F32 = jnp.float32
BF16 = jnp.bfloat16

HEAD_DIM = 64
A_HEADS = 8
A_WIDTH = A_HEADS * HEAD_DIM
B_HEADS = 8
B_KV_HEADS = 2
B_GROUP = B_HEADS // B_KV_HEADS
B_WIDTH = B_HEADS * HEAD_DIM
B_KV_WIDTH = B_KV_HEADS * HEAD_DIM
DECAY_LORA = 64
ICLR_LORA = 64
GATE_LORA = 128
LORA_W = DECAY_LORA + ICLR_LORA
SHIFT_W = 3 * A_WIDTH + LORA_W + GATE_LORA
IN_W = SHIFT_W + B_WIDTH + 2 * B_KV_WIDTH
WINDOW = 128
N_GROUPS = 4
EXPERTS_PER_GROUP = 4
N_EXPERTS = N_GROUPS * EXPERTS_PER_GROUP
TOP_K_INNER = 2
EXPERT_FF = 512
RMS_EPS = 1e-6
LNX_EPS = 64e-5
DECAY_OFFSET = 0.5

LANE = 128
PAIR = LANE // HEAD_DIM
N_PAIRS = A_HEADS // PAIR
ROUTE_W = LANE
ROUTE_ROWS = 8
MASK_NEG = -1e30
VMEM_LIMIT = 48 * 1024 * 1024

HEAD_SHIFT = HEAD_DIM.bit_length() - 1
HI = lax.Precision.HIGHEST


def _row_tile(n, cap):
    t = cap
    while n % t:
        t //= 2
    return t


def _dot(a, b, precision=None):
    return jnp.dot(a, b, preferred_element_type=F32, precision=precision)


def _dot_nt(a, b, precision=None):
    return lax.dot_general(a, b, (((1,), (1,)), ((), ())), preferred_element_type=F32, precision=precision)


def _dot_tn(a, b, precision=None):
    return lax.dot_general(a, b, (((0,), (0,)), ((), ())), preferred_element_type=F32, precision=precision)


def _sigmoid(x):
    return 1.0 / (1.0 + jnp.exp(-x))


def _hi_lo(x, axis):
    hi = x.astype(BF16)
    lo = (x - hi.astype(F32)).astype(BF16)
    return jnp.concatenate([hi, lo], axis=axis)


def _two_stream(i, n_p_tiles, run, prompt_refs, sample_refs):
    @pl.when(i < n_p_tiles)
    def _():
        run(*prompt_refs)

    @pl.when(i >= n_p_tiles)
    def _():
        run(*sample_refs)


def _stream_specs(tm, width, n_p_tiles):
    return [pl.BlockSpec((tm, width), lambda i: (jnp.minimum(i, n_p_tiles - 1), 0)),
            pl.BlockSpec((tm, width), lambda i: (0, 0))]


def _inproj_body(xp_ref, xs_ref, g_ref, w_ref, pa_ref, q_ref, k_ref, v_ref, *, n_p_tiles):
    def run(x_ref):
        x = x_ref[...]
        h = x * lax.rsqrt(jnp.mean(x * x, axis=-1, keepdims=True) + RMS_EPS) * g_ref[...]
        p = _dot(h.astype(BF16), w_ref[...])
        pa_ref[...] = p[:, :SHIFT_W]
        q_ref[...] = (p[:, SHIFT_W:SHIFT_W + B_WIDTH] * (HEAD_DIM ** -0.5)).astype(BF16)
        k_ref[...] = p[:, SHIFT_W + B_WIDTH:SHIFT_W + B_WIDTH + B_KV_WIDTH]
        v_ref[...] = p[:, SHIFT_W + B_WIDTH + B_KV_WIDTH:]

    _two_stream(pl.program_id(0), n_p_tiles, run, (xp_ref,), (xs_ref,))


def _inproj_call(xp, xs, g, w_bf16):
    n_p, d = xp.shape
    tm = xs.shape[0]
    assert n_p % tm == 0 and tm % 8 == 0, "sample rows must form one row tile that divides the prompt rows"
    n_p_tiles = n_p // tm
    n = n_p + tm
    row = lambda i: (i, 0)
    fixed = lambda i: (0, 0)
    return pl.pallas_call(
        functools.partial(_inproj_body, n_p_tiles=n_p_tiles),
        grid=(n_p_tiles + 1,),
        in_specs=_stream_specs(tm, d, n_p_tiles) + [pl.BlockSpec((1, d), fixed), pl.BlockSpec((d, IN_W), fixed)],
        out_specs=[pl.BlockSpec((tm, SHIFT_W), row), pl.BlockSpec((tm, B_WIDTH), row),
                   pl.BlockSpec((tm, B_KV_WIDTH), row), pl.BlockSpec((tm, B_KV_WIDTH), row)],
        out_shape=[jax.ShapeDtypeStruct((n, SHIFT_W), F32), jax.ShapeDtypeStruct((n, B_WIDTH), BF16),
                   jax.ShapeDtypeStruct((n, B_KV_WIDTH), F32), jax.ShapeDtypeStruct((n, B_KV_WIDTH), F32)],
        compiler_params=pltpu.CompilerParams(dimension_semantics=("arbitrary",), vmem_limit_bytes=VMEM_LIMIT),
        name="inproj",
    )(xp, xs, g, w_bf16)


def _rwkv_body(pa_ref, shift0_ref, s0_ref, mu_ref, w0_ref, a0_ref, wlora_ref, wgate_ref, kk_ref, ka_ref,
               rk_ref, lng_ref, lnb_ref, oa_ref, sout_ref, s_scr, prev_scr, *, chunk, n_sub, n_bat, t_real):
    C = chunk
    seq_rows = n_sub * C
    n_seg = n_bat * n_sub
    rows = n_seg * C
    c = pl.program_id(1)

    @pl.when(c == 0)
    def _():
        prev_scr[...] = shift0_ref[...]
        for b in range(n_bat):
            for j in range(N_PAIRS):
                s_scr[b * N_PAIRS + j] = jnp.concatenate([s0_ref[b, PAIR * j + h] for h in range(PAIR)], axis=1)

    pa = pa_ref[...]
    row = lax.broadcasted_iota(jnp.int32, (rows, 1), 0)
    row_in_chunk = row & (C - 1)
    pa_prev = pltpu.roll(pa, 1, axis=0)
    for b in range(n_bat):
        pa_prev = jnp.where(row == b * seq_rows, prev_scr[b], pa_prev)
        last = (b + 1) * seq_rows - C + t_real - 1
        prev_scr[b] = pa[last:last + 1]
    xm = pa + mu_ref[...] * (pa_prev - pa)

    r = xm[:, :A_WIDTH]
    k = xm[:, A_WIDTH:2 * A_WIDTH]
    v = xm[:, 2 * A_WIDTH:3 * A_WIDTH]
    lora_in = xm[:, 3 * A_WIDTH:3 * A_WIDTH + LORA_W]
    gd = xm[:, 3 * A_WIDTH + LORA_W:]

    lane = lax.broadcasted_iota(jnp.int32, (1, LANE), 1)
    lo_half = lane < HEAD_DIM
    z = jnp.where(lo_half, jnp.tanh(lora_in), lora_in)
    lw = _dot(z.astype(BF16), wlora_ref[...])
    dec_pre = w0_ref[...] + lw[:, :A_WIDTH]
    a = _sigmoid(a0_ref[...] + lw[:, A_WIDTH:])
    neg = -dec_pre
    softplus = jnp.maximum(neg, 0.0) + jnp.log(1.0 + jnp.exp(-jnp.abs(neg)))
    logdec = -jnp.exp(-softplus - DECAY_OFFSET)
    g = _dot(_sigmoid(gd).astype(BF16), wgate_ref[...])

    ri = lax.broadcasted_iota(jnp.int32, (LANE, LANE), 0)
    ci = lax.broadcasted_iota(jnp.int32, (LANE, LANE), 1)
    seg = jnp.where((ri >> HEAD_SHIFT) == (ci >> HEAD_SHIFT), 1.0, 0.0).astype(BF16)
    seg2 = jnp.concatenate([seg, seg], axis=0)

    def headsum(x):
        return jnp.concatenate([_dot(_hi_lo(x[:, j * LANE:(j + 1) * LANE], 1), seg2) for j in range(N_PAIRS)],
                               axis=1)

    kk = k * kk_ref[...]
    kk = kk * lax.rsqrt(jnp.maximum(headsum(kk * kk), 1e-24))
    k = k * (1.0 + (a - 1.0) * ka_ref[...])
    bonus = headsum(r * k * rk_ref[...]) * v

    if t_real < C:
        valid = row_in_chunk < t_real
        logdec = jnp.where(valid, logdec, 0.0)
        kk = jnp.where(valid, kk, 0.0)
        k = jnp.where(valid, k, 0.0)
        v = jnp.where(valid, v, 0.0)

    cum = logdec
    shift = 1
    while shift < C:
        cum = cum + jnp.where(row_in_chunk >= shift, pltpu.roll(cum, shift, axis=0), 0.0)
        shift *= 2
    ends = [cum[(s + 1) * C - 1:(s + 1) * C] for s in range(n_seg)]
    cum_end = jnp.concatenate([jnp.broadcast_to(e, (C, A_WIDTH)) for e in ends], axis=0) if n_seg > 1 else ends[0]
    w_incl = jnp.exp(cum)
    w_prev = jnp.exp(cum - logdec)
    w_inv = jnp.exp(-cum)
    w_end = jnp.exp(cum_end - cum)
    w_chunk = [jnp.exp(e) for e in ends]
    kka = kk * a
    terms = dict(A=-kk * w_prev, R=r * w_incl, B=kka * w_inv, K=k * w_inv, V=v, Be=kka * w_end, Ke=k * w_end)

    C2 = PAIR * C
    t_idx = lax.broadcasted_iota(jnp.int32, (C, 1), 0)
    i_idx = lax.broadcasted_iota(jnp.int32, (1, C2), 1) & (C - 1)
    strict = i_idx < t_idx
    incl = i_idx <= t_idx
    ident = jnp.where(i_idx == t_idx, 1.0, 0.0).astype(F32)
    lo_time = lax.broadcasted_iota(jnp.int32, (1, C2), 1) < C
    n_levels = max(1, (C - 1).bit_length())
    bf = lambda x: x.astype(BF16)

    def bd(x, lo_mask=lo_half):
        return bf(jnp.concatenate([jnp.where(lo_mask, x, 0.0), jnp.where(lo_mask, 0.0, x)], axis=0))

    chains = [(s, j) for s in range(n_seg) for j in range(N_PAIRS)]
    tile = lambda name, s, j: terms[name][s * C:(s + 1) * C, j * LANE:(j + 1) * LANE]
    r_sbs = {ch: tile("R", *ch) for ch in chains}
    stk = {ch: {nm: bd(tile(nm, *ch)) for nm in ("A", "B", "K", "V", "Be", "Ke")} for ch in chains}
    m_ab, aak, arb, ark = {}, {}, {}, {}
    for ch in chains:
        t = stk[ch]
        a_sbs, rb = bf(tile("A", *ch)), bf(r_sbs[ch])
        if C2 % LANE == 0:
            m1 = _dot_nt(jnp.concatenate([a_sbs, rb], axis=0), jnp.concatenate([t["B"], t["K"]], axis=0))
            ab, ak, rbm, rk = m1[:C, :C2], m1[:C, C2:], m1[C:, :C2], m1[C:, C2:]
        else:
            ab, ak = _dot_nt(a_sbs, t["B"]), _dot_nt(a_sbs, t["K"])
            rbm, rk = _dot_nt(rb, t["B"]), _dot_nt(rb, t["K"])
        m_ab[ch] = jnp.where(strict, ab, 0.0)
        aak[ch] = bf(jnp.where(strict, ak, 0.0))
        arb[ch] = bf(jnp.where(incl, rbm, 0.0))
        ark[ch] = bf(jnp.where(incl, rk, 0.0))

    nn = dict(m_ab)
    tinv = {ch: ident + m_ab[ch] for ch in chains}
    for _ in range(1, n_levels):
        for ch in chains:
            nn[ch] = _dot(bf(nn[ch]), bd(nn[ch], lo_time))
        for ch in chains:
            tinv[ch] = tinv[ch] + _dot(bf(nn[ch]), bd(tinv[ch], lo_time))
    akv = {ch: _dot(aak[ch], stk[ch]["V"]) for ch in chains}
    x = {ch: _dot(bf(tinv[ch]), jnp.concatenate([stk[ch]["A"], bd(akv[ch])], axis=1)) for ch in chains}
    a_eff = {ch: bd(x[ch][:, :LANE]) for ch in chains}
    v_eff = {ch: bd(x[ch][:, LANE:]) for ch in chains}
    zed = {ch: _dot(arb[ch], jnp.concatenate([a_eff[ch], v_eff[ch]], axis=1)) for ch in chains}
    rhat = {ch: bf(r_sbs[ch] + zed[ch][:, :LANE]) for ch in chains}
    y0 = {ch: _dot(ark[ch], stk[ch]["V"]) + zed[ch][:, LANE:] for ch in chains}
    p_mat = {ch: bf(_dot_tn(stk[ch]["Be"], a_eff[ch])) for ch in chains}
    q_bd = {ch: _dot_tn(jnp.concatenate([v_eff[ch], stk[ch]["V"]], axis=0),
                        jnp.concatenate([stk[ch]["Be"], stk[ch]["Ke"]], axis=0)) for ch in chains}

    y_rows = []
    for b in range(n_bat):
        state = [s_scr[b * N_PAIRS + j] for j in range(N_PAIRS)]
        for sub in range(n_sub):
            s = b * n_sub + sub
            ys = []
            for j in range(N_PAIRS):
                ch = (s, j)
                hi = bf(state[j])
                lo = bf(state[j] - hi.astype(F32))
                yy = _dot_nt(rhat[ch], jnp.concatenate([bd(hi), bd(lo)], axis=0))
                ys.append(yy[:, :LANE] + yy[:, LANE:] + y0[ch])
                sp = _dot_nt(jnp.concatenate([hi, lo], axis=0), p_mat[ch])
                q_sbs = q_bd[ch][:HEAD_DIM] + q_bd[ch][HEAD_DIM:]
                state[j] = state[j] * w_chunk[s][:, j * LANE:(j + 1) * LANE] + sp[:HEAD_DIM] + sp[HEAD_DIM:] + q_sbs
            y_rows.append(jnp.concatenate(ys, axis=1))
        for j in range(N_PAIRS):
            s_scr[b * N_PAIRS + j] = state[j]

    y = jnp.concatenate(y_rows, axis=0) if n_seg > 1 else y_rows[0]
    inv_n = 1.0 / HEAD_DIM
    mean = headsum(y) * inv_n
    d = y - mean
    var = headsum(d * d) * inv_n
    yn = d * lax.rsqrt(var + LNX_EPS) * lng_ref[...] + lnb_ref[...]
    oa_ref[...] = ((yn + bonus) * g).astype(oa_ref.dtype)

    @pl.when(c == pl.num_programs(1) - 1)
    def _():
        for b in range(n_bat):
            for j in range(N_PAIRS):
                for h in range(PAIR):
                    sout_ref[b, PAIR * j + h] = s_scr[b * N_PAIRS + j][:, h * HEAD_DIM:(h + 1) * HEAD_DIM]


def _rwkv_call(pa2d, shift0, s0_pairs, prm, *, n_seq, n_chunks, chunk, n_sub, n_bat, t_real):
    n_steps = n_chunks // n_sub
    assert n_seq % n_bat == 0 and (n_bat == 1 or n_steps == 1)
    row = lambda b, c: (b * n_steps + c, 0)
    seq3 = lambda b, c: (b, 0, 0)
    fixed = lambda b, c: (0, 0)
    vec = lambda w: pl.BlockSpec((1, w), fixed)
    body = functools.partial(_rwkv_body, chunk=chunk, n_sub=n_sub, n_bat=n_bat, t_real=t_real)
    n_rows = n_seq * n_chunks * chunk
    blk_rows = n_bat * n_sub * chunk
    state_blk = (n_bat, A_HEADS, HEAD_DIM, HEAD_DIM)
    seq4 = lambda b, c: (b, 0, 0, 0)
    return pl.pallas_call(
        body,
        grid=(n_seq // n_bat, n_steps),
        in_specs=[pl.BlockSpec((blk_rows, SHIFT_W), row),
                  pl.BlockSpec((n_bat, 1, SHIFT_W), seq3),
                  pl.BlockSpec(state_blk, seq4),
                  vec(SHIFT_W), vec(A_WIDTH), vec(A_WIDTH),
                  pl.BlockSpec((LORA_W, 2 * A_WIDTH), fixed), pl.BlockSpec((GATE_LORA, A_WIDTH), fixed),
                  vec(A_WIDTH), vec(A_WIDTH), vec(A_WIDTH), vec(A_WIDTH), vec(A_WIDTH)],
        out_specs=[pl.BlockSpec((blk_rows, A_WIDTH), row), pl.BlockSpec(state_blk, seq4)],
        out_shape=[jax.ShapeDtypeStruct((n_rows, A_WIDTH), BF16),
                   jax.ShapeDtypeStruct((n_seq, A_HEADS, HEAD_DIM, HEAD_DIM), F32)],
        scratch_shapes=[pltpu.VMEM((n_bat * N_PAIRS, HEAD_DIM, LANE), F32), pltpu.VMEM((n_bat, 1, SHIFT_W), F32)],
        compiler_params=pltpu.CompilerParams(dimension_semantics=("arbitrary", "arbitrary"),
                                             vmem_limit_bytes=VMEM_LIMIT),
        name="rwkv",
    )(pa2d, shift0, s0_pairs, prm["mu"], prm["w0"], prm["a0"], prm["wlora"], prm["wgate"], prm["kk"], prm["ka"],
      prm["rk"], prm["lng"], prm["lnb"])


def _swa_bias(tq):
    rows = B_GROUP * tq
    grp = jnp.arange(rows, dtype=jnp.int32)[:, None] // tq
    t = jnp.arange(rows, dtype=jnp.int32)[:, None] % tq
    dist_p = t + WINDOW - jnp.arange(WINDOW, dtype=jnp.int32)[None, :]
    dist_c = t - jnp.arange(tq, dtype=jnp.int32)[None, :]

    def bias(dist, kv):
        slope = sum(jnp.where(grp == g, 2.0 ** -(kv * B_GROUP + g + 1), 0.0) for g in range(B_GROUP))
        return jnp.where((dist >= 0) & (dist < WINDOW), -slope * dist.astype(F32), MASK_NEG)

    return (jnp.stack([bias(dist_p, kv) for kv in range(B_KV_HEADS)]),
            jnp.stack([bias(dist_c, kv) for kv in range(B_KV_HEADS)]))


def _swa_body(sink_ref, q_ref, kp_ref, vp_ref, kc_ref, vc_ref, bp_ref, bc_ref, gn_ref, o_ref, *cache_refs, tq, n_bat,
              t_real, first_has_prev):
    for s in range(n_bat):
        rs = slice(s * tq, (s + 1) * tq)
        kp, vp, kc, vc = kp_ref[s], vp_ref[s], kc_ref[rs], vc_ref[rs]
        _swa_one(sink_ref, q_ref[rs], kp, vp, kc, vc, bp_ref, bc_ref, gn_ref, o_ref.at[rs], tq=tq,
                 first_has_prev=first_has_prev)
        if cache_refs:
            row = lax.broadcasted_iota(jnp.int32, (WINDOW, 1), 0)
            for out_ref, old, new in zip(cache_refs, (kp, vp), (kc, vc)):
                new_tail = jnp.concatenate([pltpu.roll(new, tq - t_real, axis=0)] * (WINDOW // tq), axis=0)
                out_ref[s] = jnp.where(row >= WINDOW - t_real, new_tail, pltpu.roll(old, WINDOW - t_real, axis=0))


def _swa_one(sink_ref, q, kp, vp, kc, vc, bp_ref, bc_ref, gn_ref, o_ref, *, tq, first_has_prev):
    nblk = pl.program_id(1)
    kp = kp.astype(BF16)
    kc = kc.astype(BF16)
    lane = lax.broadcasted_iota(jnp.int32, (1, LANE), 1)
    prev_pen = 0.0 if first_has_prev else jnp.where(nblk > 0, 0.0, MASK_NEG)

    tiles = [None] * B_GROUP
    for kv in range(B_KV_HEADS):
        mk = (lane < HEAD_DIM) if kv == 0 else (lane >= HEAD_DIM)
        q_st = jnp.concatenate([jnp.where(mk, q[:, g * LANE:(g + 1) * LANE], 0.0) for g in range(B_GROUP)],
                               axis=0).astype(BF16)
        s_p = _dot_nt(q_st, kp)
        s_c = _dot_nt(q_st, kc)
        vpm = jnp.where(mk, vp, 0.0).astype(BF16)
        vcm = jnp.where(mk, vc, 0.0).astype(BF16)
        for g in range(B_GROUP):
            rs = slice(g * tq, (g + 1) * tq)
            sink = sink_ref[kv * B_GROUP + g]
            sp = s_p[rs] + (bp_ref[kv, rs, :] + prev_pen)
            sc = s_c[rs] + bc_ref[kv, rs, :]
            if tq == WINDOW:
                m = jnp.maximum(jnp.max(jnp.maximum(sp, sc), axis=-1, keepdims=True), sink)
            else:
                m = jnp.maximum(jnp.maximum(jnp.max(sp, axis=-1, keepdims=True),
                                            jnp.max(sc, axis=-1, keepdims=True)), sink)
            e_p = jnp.exp(sp - m)
            e_c = jnp.exp(sc - m)
            if tq == WINDOW:
                e_sum = jnp.sum(e_p + e_c, axis=-1, keepdims=True)
            else:
                e_sum = jnp.sum(e_p, axis=-1, keepdims=True) + jnp.sum(e_c, axis=-1, keepdims=True)
            inv = 1.0 / (e_sum + jnp.exp(sink - m))
            o = (_dot(e_p.astype(BF16), vpm) + _dot(e_c.astype(BF16), vcm)) * inv
            tiles[g] = o if tiles[g] is None else tiles[g] + o

    ssq = sum(jnp.sum(x * x, axis=-1, keepdims=True) for x in tiles)
    inv_rms = lax.rsqrt(ssq * (1.0 / B_WIDTH) + RMS_EPS)
    o_ref[...] = (jnp.concatenate(tiles, axis=1) * inv_rms * gn_ref[...]).astype(o_ref.dtype)


def _swa_call(sink, q2d, kprev, vprev, kcur2d, vcur2d, gn, *, n_seq, n_blk, tq, n_bat, first_has_prev, prev_map,
              out_dtype, cache_rows=0):
    assert n_seq % n_bat == 0 and (n_bat == 1 or n_blk == 1) and (cache_rows == 0 or n_blk == 1)
    row = lambda b, n: (b * n_blk + n, 0)
    body = functools.partial(_swa_body, tq=tq, n_bat=n_bat, t_real=cache_rows, first_has_prev=first_has_prev)
    bias_p, bias_c = _swa_bias(tq)
    whole = lambda b, n: (0, 0, 0)
    out_specs = [pl.BlockSpec((n_bat * tq, B_WIDTH), row)]
    out_shape = [jax.ShapeDtypeStruct((n_seq * n_blk * tq, B_WIDTH), out_dtype)]
    if cache_rows:
        out_specs += [pl.BlockSpec((n_bat, WINDOW, B_KV_WIDTH), prev_map)] * 2
        out_shape += [jax.ShapeDtypeStruct((n_seq, WINDOW, B_KV_WIDTH), F32)] * 2
    return pl.pallas_call(
        body,
        grid=(n_seq // n_bat, n_blk),
        in_specs=[pl.BlockSpec(memory_space=pltpu.SMEM),
                  pl.BlockSpec((n_bat * tq, B_WIDTH), row),
                  pl.BlockSpec((n_bat, WINDOW, B_KV_WIDTH), prev_map),
                  pl.BlockSpec((n_bat, WINDOW, B_KV_WIDTH), prev_map),
                  pl.BlockSpec((n_bat * tq, B_KV_WIDTH), row),
                  pl.BlockSpec((n_bat * tq, B_KV_WIDTH), row),
                  pl.BlockSpec(bias_p.shape, whole),
                  pl.BlockSpec(bias_c.shape, whole),
                  pl.BlockSpec((1, B_WIDTH), lambda b, n: (0, 0))],
        out_specs=out_specs,
        out_shape=out_shape,
        compiler_params=pltpu.CompilerParams(dimension_semantics=("arbitrary", "arbitrary"),
                                             vmem_limit_bytes=VMEM_LIMIT),
        name="swa",
    )(sink, q2d, kprev, vprev, kcur2d, vcur2d, bias_p, bias_c, gn)


def _route_rows(lg):
    lane = lax.broadcasted_iota(jnp.int32, (1, ROUTE_W), 1)
    lane_f = lane.astype(F32)
    no_lane = float(ROUTE_W)
    is_group = lane < N_GROUPS
    m_g = jnp.max(jnp.where(is_group, lg, MASK_NEG), axis=-1, keepdims=True)
    g_idx = jnp.min(jnp.where(is_group & (lg == m_g), lane_f, no_lane), axis=-1, keepdims=True)
    p_group = 1.0 / jnp.sum(jnp.where(is_group, jnp.exp(lg - m_g), 0.0), axis=-1, keepdims=True)
    e_lane = lane - N_GROUPS
    in_group = (e_lane >= 0) & (e_lane < N_EXPERTS) & ((e_lane >> 2).astype(F32) == g_idx)
    m_1 = jnp.max(jnp.where(in_group, lg, MASK_NEG), axis=-1, keepdims=True)
    i_1 = jnp.min(jnp.where(in_group & (lg == m_1), lane_f, no_lane), axis=-1, keepdims=True)
    rest = in_group & (lane_f != i_1)
    m_2 = jnp.max(jnp.where(rest, lg, MASK_NEG), axis=-1, keepdims=True)
    i_2 = jnp.min(jnp.where(rest & (lg == m_2), lane_f, no_lane), axis=-1, keepdims=True)
    ratio = jnp.exp(m_2 - m_1)
    w_1 = p_group / (1.0 + ratio)
    return jnp.where(lane == 0, i_1 - N_GROUPS,
                     jnp.where(lane == 1, i_2 - N_GROUPS,
                               jnp.where(lane == 2, w_1, jnp.where(lane == 3, w_1 * ratio, 0.0))))


def _outproj_body(xp_ref, xs_ref, oap_ref, oas_ref, obp_ref, obs_ref, wa_ref, wb_ref, g_ref, wr_ref, br_ref,
                  x1_ref, h2_ref, rt_ref, rtt_ref, *, n_p_tiles):
    def run(x_ref, oa_ref, ob_ref):
        x1 = x_ref[...] + _dot(oa_ref[...].astype(BF16), wa_ref[...]) + _dot(ob_ref[...].astype(BF16), wb_ref[...])
        x1_ref[...] = x1
        h2 = x1 * lax.rsqrt(jnp.mean(x1 * x1, axis=-1, keepdims=True) + RMS_EPS) * g_ref[...]
        h2b = h2.astype(BF16)
        h2_ref[...] = h2b
        rt = _route_rows(_dot(h2b, wr_ref[...]) + br_ref[...])
        rt_ref[...] = rt
        rtt_ref[...] = rt.T[:ROUTE_ROWS]

    _two_stream(pl.program_id(0), n_p_tiles, run, (xp_ref, oap_ref, obp_ref), (xs_ref, oas_ref, obs_ref))


def _outproj_call(xp, xs, oap, oas, obp, obs, wa, wb, g, wr, br):
    n_p, d = xp.shape
    tm = xs.shape[0]
    n_p_tiles = n_p // tm
    n = n_p + tm
    row = lambda i: (i, 0)
    fixed = lambda i: (0, 0)
    return pl.pallas_call(
        functools.partial(_outproj_body, n_p_tiles=n_p_tiles),
        grid=(n_p_tiles + 1,),
        in_specs=(_stream_specs(tm, d, n_p_tiles) + _stream_specs(tm, A_WIDTH, n_p_tiles)
                  + _stream_specs(tm, B_WIDTH, n_p_tiles)
                  + [pl.BlockSpec((A_WIDTH, d), fixed), pl.BlockSpec((B_WIDTH, d), fixed), pl.BlockSpec((1, d), fixed),
                     pl.BlockSpec((d, ROUTE_W), fixed), pl.BlockSpec((1, ROUTE_W), fixed)]),
        out_specs=[pl.BlockSpec((tm, d), row), pl.BlockSpec((tm, d), row), pl.BlockSpec((tm, ROUTE_W), row),
                   pl.BlockSpec((ROUTE_ROWS, tm), row)],
        out_shape=[jax.ShapeDtypeStruct((n, d), F32), jax.ShapeDtypeStruct((n, d), BF16),
                   jax.ShapeDtypeStruct((n, ROUTE_W), F32),
                   jax.ShapeDtypeStruct(((n_p_tiles + 1) * ROUTE_ROWS, tm), F32)],
        compiler_params=pltpu.CompilerParams(dimension_semantics=("arbitrary",), vmem_limit_bytes=VMEM_LIMIT),
        name="outproj",
    )(xp, xs, oap, oas, obp, obs, wa, wb, g, wr, br)


def _expert_body(vb_ref, ve_ref, lo_ref, hi_ref, x_ref, wg_ref, wu_ref, wd_ref, y_ref, wg_bf, wu_bf, wd_bf):
    v = pl.program_id(0)
    pv = jnp.maximum(v - 1, 0)
    lo, hi = lo_ref[v], hi_ref[v]
    first_visit = (v == 0) | (vb_ref[v] != vb_ref[pv])

    @pl.when((v == 0) | (ve_ref[v] != ve_ref[pv]))
    def _():
        wg_bf[...] = wg_ref[0].astype(BF16)
        wu_bf[...] = wu_ref[0].astype(BF16)
        wd_bf[...] = wd_ref[0].astype(BF16)

    @pl.when(hi > lo)
    def _():
        x = x_ref[...]
        gate = _dot(x, wg_bf[...])
        up = _dot(x, wu_bf[...])
        mid = (gate * _sigmoid(gate) * up).astype(BF16)
        y = _dot(mid, wd_bf[...])
        row = lax.broadcasted_iota(jnp.int32, (y.shape[0], 1), 0)
        mine = (row >= lo) & (row < hi)

        @pl.when(first_visit)
        def _():
            y_ref[...] = jnp.where(mine, y, 0.0)

        @pl.when(jnp.logical_not(first_visit))
        def _():
            y_ref[...] = jnp.where(mine, y, y_ref[...])


def _expert_call(visits, xs, wg, wu, wd, *, tb):
    n_rows, d = xs.shape
    ff = wg.shape[-1]
    vb, ve, lo, hi = visits
    blk = lambda v, vb, ve, lo, hi: (vb[v], 0)
    wsel = lambda v, vb, ve, lo, hi: (ve[v], 0, 0)
    grid_spec = pltpu.PrefetchScalarGridSpec(
        num_scalar_prefetch=4,
        grid=(vb.shape[0],),
        in_specs=[pl.BlockSpec((tb, d), blk), pl.BlockSpec((1, d, ff), wsel), pl.BlockSpec((1, d, ff), wsel),
                  pl.BlockSpec((1, ff, d), wsel)],
        out_specs=pl.BlockSpec((tb, d), blk),
        scratch_shapes=[pltpu.VMEM((d, ff), BF16), pltpu.VMEM((d, ff), BF16), pltpu.VMEM((ff, d), BF16)],
    )
    return pl.pallas_call(
        _expert_body,
        grid_spec=grid_spec,
        out_shape=jax.ShapeDtypeStruct((n_rows, d), F32),
        compiler_params=pltpu.CompilerParams(dimension_semantics=("arbitrary",), vmem_limit_bytes=VMEM_LIMIT),
        name="experts",
    )(vb, ve, lo, hi, xs, wg, wu, wd)


def _final_body(x1_ref, y0_ref, y1_ref, rt_ref, g_ref, o_ref):
    rt = rt_ref[...]
    x = x1_ref[...] + (rt[:, 2:3] * y0_ref[...] + rt[:, 3:4] * y1_ref[...])
    o_ref[...] = x * lax.rsqrt(jnp.mean(x * x, axis=-1, keepdims=True) + RMS_EPS) * g_ref[...]


def _final_call(x1, y0, y1, route, g, *, first_tile, n_tiles, tm):
    d = x1.shape[1]
    src = lambda i: (first_tile + i, 0)
    return pl.pallas_call(
        _final_body,
        grid=(n_tiles,),
        in_specs=[pl.BlockSpec((tm, d), src), pl.BlockSpec((tm, d), src), pl.BlockSpec((tm, d), src),
                  pl.BlockSpec((tm, ROUTE_W), src), pl.BlockSpec((1, d), lambda i: (0, 0))],
        out_specs=pl.BlockSpec((tm, d), lambda i: (i, 0)),
        out_shape=jax.ShapeDtypeStruct((n_tiles * tm, d), F32),
        compiler_params=pltpu.CompilerParams(dimension_semantics=("arbitrary",), vmem_limit_bytes=VMEM_LIMIT),
        name="final",
    )(x1, y0, y1, route, g)


def _dispatch(flat_e, n, tb):
    n_assign = n * TOP_K_INNER
    idx_bits = (n_assign - 1).bit_length()
    assert N_EXPERTS << idx_bits < 2 ** 31 and n_assign % tb == 0
    ids = jnp.arange(n_assign, dtype=jnp.int32)
    sorted_key = jnp.sort((flat_e << idx_bits) | ids)
    sorted_id = sorted_key & ((1 << idx_bits) - 1)
    src_tok = sorted_id - n * (sorted_id // n)

    experts = jnp.arange(N_EXPERTS, dtype=jnp.int32)
    onehot = (flat_e[:, None] == experts[None, :]).astype(jnp.int32)
    csum = jnp.cumsum(onehot, axis=0)
    counts = csum[-1]
    ends = jnp.cumsum(counts)
    starts = ends - counts
    dest = (jnp.sum((csum - 1 + starts[None, :]) * onehot, axis=1)).astype(jnp.int32)

    n_blocks = n_assign // tb
    n_visits = n_blocks + N_EXPERTS - 1
    first_blk = starts // tb
    n_vis_e = jnp.where(counts > 0, (ends + tb - 1) // tb - first_blk, 0)
    v_end = jnp.cumsum(n_vis_e)
    v = jnp.arange(n_visits, dtype=jnp.int32)
    valid = v < v_end[-1]
    last_e = jnp.max(jnp.where(counts > 0, experts, 0))
    ve = jnp.where(valid, jnp.sum((v_end[None, :] <= v[:, None]).astype(jnp.int32), axis=1), last_e)
    pick = lambda a: jnp.sum(jnp.where(ve[:, None] == experts[None, :], a[None, :], 0), axis=1)
    vb = pick(first_blk) + v - pick(v_end - n_vis_e)
    lo = jnp.maximum(pick(starts), vb * tb) - vb * tb
    hi = jnp.minimum(pick(ends), (vb + 1) * tb) - vb * tb
    vb = jnp.where(valid, vb, n_blocks - 1)
    lo = jnp.where(valid, lo, 0)
    hi = jnp.where(valid, hi, 0)
    return dest, src_tok, (vb.astype(jnp.int32), ve.astype(jnp.int32), lo.astype(jnp.int32), hi.astype(jnp.int32))


def _q_perm():
    cols = []
    for g in range(B_GROUP):
        for kv in range(B_KV_HEADS):
            h = kv * B_GROUP + g
            cols.extend(range(h * HEAD_DIM, (h + 1) * HEAD_DIM))
    return jnp.array(cols, dtype=jnp.int32)


def _layer(x_prompt, x_sample, state_rwkv, state_shift, cache_win_k, cache_win_v, prm, norm_final_g, *,
           chunk, n_sub, moe_block):
    bp, tp, d = x_prompt.shape
    bs, ts, _ = x_sample.shape
    n_p, n_s = bp * tp, bs * ts
    ts_pad = 8
    xp = x_prompt.reshape(n_p, d)
    xs = x_sample.reshape(n_s, d)

    qp = _q_perm()
    w_in = prm["w_in"]
    w_in = jnp.concatenate([w_in[:, :SHIFT_W], w_in[:, SHIFT_W:SHIFT_W + B_WIDTH][:, qp],
                            w_in[:, SHIFT_W + B_WIDTH:]], axis=1).astype(BF16)
    pa, q, k, v = _inproj_call(xp, xs, prm["norm_mix_g"][None], w_in)

    zero_blk = jnp.zeros((LORA_W // 2, A_WIDTH), F32)
    wlora = jnp.concatenate([jnp.concatenate([prm["w_decay_up"], zero_blk], axis=1),
                             jnp.concatenate([zero_blk, prm["w_iclr_up"]], axis=1)], axis=0).astype(BF16)
    rp = dict(mu=prm["mu_shift"][None], w0=prm["w_decay0"][None], a0=prm["w_iclr0"][None], wlora=wlora,
              wgate=prm["w_gate_up"].astype(BF16), kk=prm["k_k"][None], ka=prm["k_a"][None],
              rk=prm["r_k"].reshape(1, A_WIDTH), lng=prm["lnx_g"][None], lnb=prm["lnx_b"][None])

    oa_p, s_p = _rwkv_call(pa, jnp.zeros((bp, 1, SHIFT_W), F32), jnp.zeros((bp,) + state_rwkv.shape[1:], F32),
                           rp, n_seq=bp, n_chunks=tp // chunk, chunk=chunk, n_sub=n_sub, n_bat=1, t_real=chunk)
    pad_rows = lambda a: jnp.pad(a[n_p:].reshape(bs, ts, -1), ((0, 0), (0, ts_pad - ts), (0, 0))).reshape(
        bs * ts_pad, -1)
    oa_s, s_s = _rwkv_call(pad_rows(pa), state_shift[:, None, :], state_rwkv, rp, n_seq=bs,
                           n_chunks=1, chunk=ts_pad, n_sub=1, n_bat=_row_tile(bs, 8), t_real=ts)
    oa_s = oa_s.reshape(bs, ts_pad, A_WIDTH)[:, :ts].reshape(n_s, A_WIDTH)

    assert cache_win_k.shape[1] == WINDOW, "the sample path assumes a full window buffer"
    gn = prm["attn_norm_g"][qp][None]
    nb = tp // WINDOW
    k_p3 = k.reshape(-1, WINDOW, B_KV_WIDTH)
    v_p3 = v.reshape(-1, WINDOW, B_KV_WIDTH)
    ob_p, = _swa_call(prm["attn_sink"], q, k_p3, v_p3, k, v, gn, n_seq=bp, n_blk=nb, tq=WINDOW, n_bat=1,
                      first_has_prev=False, prev_map=lambda b, i: (b * nb + jnp.maximum(i - 1, 0), 0, 0),
                      out_dtype=BF16)
    ob_s, k_s, v_s = _swa_call(prm["attn_sink"], pad_rows(q).astype(F32),
                               cache_win_k.reshape(bs, WINDOW, B_KV_WIDTH),
                               cache_win_v.reshape(bs, WINDOW, B_KV_WIDTH), pad_rows(k), pad_rows(v), gn,
                               n_seq=bs, n_blk=1, tq=ts_pad, n_bat=_row_tile(bs, 16), first_has_prev=True,
                               prev_map=lambda b, i: (b, 0, 0), out_dtype=F32, cache_rows=ts)
    ob_s = ob_s.reshape(bs, ts_pad, B_WIDTH)[:, :ts].reshape(n_s, B_WIDTH)

    w_out = prm["w_out"]
    pad_lanes = jnp.zeros((d, ROUTE_W - N_GROUPS - N_EXPERTS), F32)
    wr = jnp.concatenate([prm["w_route_group"], prm["w_route_expert"], pad_lanes], axis=1).astype(BF16)
    br = jnp.concatenate([prm["b_route_group"], prm["b_route_expert"], pad_lanes[0]])[None]
    x1, h2, route, route_t = _outproj_call(xp, xs, oa_p, oa_s, ob_p, ob_s, w_out[:A_WIDTH].astype(BF16),
                                           w_out[A_WIDTH:][qp].astype(BF16), prm["norm_ffn_g"][None], wr, br)

    n = n_p + n_s
    route_t = route_t.reshape(n // n_s, ROUTE_ROWS, n_s)
    flat_e = jnp.concatenate([route_t[:, j, :].reshape(n) for j in range(TOP_K_INNER)]).astype(jnp.int32)
    dest, src_tok, visits = _dispatch(flat_e, n, moe_block)
    ybuf = _expert_call(visits, h2[src_tok], prm["w_exp_gate"], prm["w_exp_up"], prm["w_exp_down"], tb=moe_block)
    y0, y1 = ybuf[dest[:n]], ybuf[dest[n:]]
    gf = norm_final_g[None]
    y_p = _final_call(x1, y0, y1, route, gf, first_tile=0, n_tiles=n_p // n_s, tm=n_s)
    y_s = _final_call(x1, y0, y1, route, gf, first_tile=n_p // n_s, n_tiles=1, tm=n_s)

    kv4 = lambda a: a.reshape(a.shape[0], a.shape[1], B_KV_HEADS, HEAD_DIM)
    last_rows = lambda a, m: jnp.stack([a[(b + 1) * tp - m:(b + 1) * tp] for b in range(bp)], axis=0)
    return (y_p.reshape(bp, tp, d), y_s.reshape(bs, ts, d),
            s_p, last_rows(pa, 1)[:, 0], kv4(last_rows(k, WINDOW)), kv4(last_rows(v, WINDOW)),
            s_s, pa[n_p:].reshape(bs, ts, SHIFT_W)[:, -1], kv4(k_s), kv4(v_s))


def kernel(x_prompt, x_sample, state_rwkv, state_shift, cache_win_k, cache_win_v, norm_mix_g, w_in, mu_shift, w_decay0, w_decay_up, w_iclr0, w_iclr_up, w_gate_up, k_k, k_a, r_k, lnx_g, lnx_b, attn_sink, attn_norm_g, w_out, norm_ffn_g, w_route_group, b_route_group, w_route_expert, b_route_expert, w_exp_gate, w_exp_up, w_exp_down, norm_final_g):
    assert norm_mix_g.shape[0] == 1, "single-layer trunk"
    prm = dict(norm_mix_g=norm_mix_g[0], w_in=w_in[0], mu_shift=mu_shift[0], w_decay0=w_decay0[0],
               w_decay_up=w_decay_up[0], w_iclr0=w_iclr0[0], w_iclr_up=w_iclr_up[0], w_gate_up=w_gate_up[0],
               k_k=k_k[0], k_a=k_a[0], r_k=r_k[0], lnx_g=lnx_g[0], lnx_b=lnx_b[0], attn_sink=attn_sink[0],
               attn_norm_g=attn_norm_g[0], w_out=w_out[0], norm_ffn_g=norm_ffn_g[0],
               w_route_group=w_route_group[0], b_route_group=b_route_group[0],
               w_route_expert=w_route_expert[0], b_route_expert=b_route_expert[0],
               w_exp_gate=w_exp_gate[0], w_exp_up=w_exp_up[0], w_exp_down=w_exp_down[0])
    outs = _layer(x_prompt, x_sample, state_rwkv[0], state_shift[0], cache_win_k[0], cache_win_v[0], prm,
                  norm_final_g, chunk=64, n_sub=4, moe_block=512)
    y_p, y_s, s_p, sh_p, kp, vp, s_s, sh_s, ks, vs = outs
    return (y_p, y_s, s_p[None], sh_p[None], kp[None], vp[None], s_s[None], sh_s[None], ks[None], vs[None])
```

```python
import functools

import jax
import jax.numpy as jnp
from jax import lax
from jax.experimental import pallas as pl
from jax.experimental.pallas import tpu as pltpu

F32 = jnp.float32
BF16 = jnp.bfloat16

HEAD_DIM = 64
A_HEADS = 8
A_WIDTH = A_HEADS * HEAD_DIM
B_HEADS = 8
B_KV_HEADS = 2
B_GROUP = B_HEADS // B_KV_HEADS
B_WIDTH = B_HEADS * HEAD_DIM
B_KV_WIDTH = B_KV_HEADS * HEAD_DIM
DECAY_LORA = 64
ICLR_LORA = 64
GATE_LORA = 128
LORA_W = DECAY_LORA + ICLR_LORA
SHIFT_W = 3 * A_WIDTH + LORA_W + GATE_LORA
IN_W = SHIFT_W + B_WIDTH + 2 * B_KV_WIDTH
WINDOW = 128
N_GROUPS = 4
EXPERTS_PER_GROUP = 4
N_EXPERTS = N_GROUPS * EXPERTS_PER_GROUP
TOP_K_INNER = 2
EXPERT_FF = 512
RMS_EPS = 1e-6
LNX_EPS = 64e-5
DECAY_OFFSET = 0.5

LANE = 128
PAIR = LANE // HEAD_DIM
N_PAIRS = A_HEADS // PAIR
ROUTE_W = LANE
ROUTE_ROWS = 8
MASK_NEG = -1e30
VMEM_LIMIT = 48 * 1024 * 1024

HEAD_SHIFT = HEAD_DIM.bit_length() - 1
HI = lax.Precision.HIGHEST


def _row_tile(n, cap):
    t = cap
    while n % t:
        t //= 2
    return t


def _dot(a, b, precision=None):
    return jnp.dot(a, b, preferred_element_type=F32, precision=precision)


def _dot_nt(a, b, precision=None):
    return lax.dot_general(a, b, (((1,), (1,)), ((), ())), preferred_element_type=F32, precision=precision)


def _dot_tn(a, b, precision=None):
    return lax.dot_general(a, b, (((0,), (0,)), ((), ())), preferred_element_type=F32, precision=precision)


def _sigmoid(x):
    return 1.0 / (1.0 + jnp.exp(-x))


def _hi_lo(x, axis):
    hi = x.astype(BF16)
    lo = (x - hi.astype(F32)).astype(BF16)
    return jnp.concatenate([hi, lo], axis=axis)


def _run_segment(i, seg_sizes, run, seg_refs):
    start = 0
    for size, refs in zip(seg_sizes, seg_refs):
        pl.when((i >= start) & (i < start + size))(functools.partial(run, *refs))
        start += size


def _seg_specs(tm, width, seg_sizes, firsts):
    specs, start = [], 0
    for size, first in zip(seg_sizes, firsts):
        specs.append(pl.BlockSpec((tm, width),
                                  lambda i, s=start, f=first, n=size: (f + jnp.clip(i - s, 0, n - 1), 0)))
        start += size
    return specs


def _inproj_body(*refs, seg_sizes):
    n_seg = len(seg_sizes)
    g_ref, w_ref, pa_ref, q_ref, k_ref, v_ref = refs[n_seg:]

    def run(x_ref):
        x = x_ref[...]
        h = x * lax.rsqrt(jnp.mean(x * x, axis=-1, keepdims=True) + RMS_EPS) * g_ref[...]
        p = _dot(h.astype(BF16), w_ref[...])
        pa_ref[...] = p[:, :SHIFT_W]
        q_ref[...] = (p[:, SHIFT_W:SHIFT_W + B_WIDTH] * (HEAD_DIM ** -0.5)).astype(BF16)
        k_ref[...] = p[:, SHIFT_W + B_WIDTH:SHIFT_W + B_WIDTH + B_KV_WIDTH]
        v_ref[...] = p[:, SHIFT_W + B_WIDTH + B_KV_WIDTH:]

    _run_segment(pl.program_id(0), seg_sizes, run, [(r,) for r in refs[:n_seg]])


def _inproj_call(xs_list, firsts, seg_sizes, g, w_bf16, *, tm):
    d = xs_list[0].shape[1]
    n_tiles = sum(seg_sizes)
    n = n_tiles * tm
    row = lambda i: (i, 0)
    fixed = lambda i: (0, 0)
    return pl.pallas_call(
        functools.partial(_inproj_body, seg_sizes=tuple(seg_sizes)),
        grid=(n_tiles,),
        in_specs=_seg_specs(tm, d, seg_sizes, firsts) + [pl.BlockSpec((1, d), fixed), pl.BlockSpec((d, IN_W), fixed)],
        out_specs=[pl.BlockSpec((tm, SHIFT_W), row), pl.BlockSpec((tm, B_WIDTH), row),
                   pl.BlockSpec((tm, B_KV_WIDTH), row), pl.BlockSpec((tm, B_KV_WIDTH), row)],
        out_shape=[jax.ShapeDtypeStruct((n, SHIFT_W), F32), jax.ShapeDtypeStruct((n, B_WIDTH), BF16),
                   jax.ShapeDtypeStruct((n, B_KV_WIDTH), F32), jax.ShapeDtypeStruct((n, B_KV_WIDTH), F32)],
        compiler_params=pltpu.CompilerParams(dimension_semantics=("arbitrary",), vmem_limit_bytes=VMEM_LIMIT),
        name="inproj",
    )(*xs_list, g, w_bf16)


def _rwkv_body(pa_ref, shift0_ref, s0_ref, mu_ref, w0_ref, a0_ref, wlora_ref, wgate_ref, kk_ref, ka_ref,
               rk_ref, lng_ref, lnb_ref, oa_ref, sout_ref, s_scr, prev_scr, *, chunk, n_sub, n_bat, t_real):
    C = chunk
    seq_rows = n_sub * C
    n_seg = n_bat * n_sub
    rows = n_seg * C
    c = pl.program_id(1)

    @pl.when(c == 0)
    def _():
        prev_scr[...] = shift0_ref[...]
        for b in range(n_bat):
            for j in range(N_PAIRS):
                s_scr[b * N_PAIRS + j] = jnp.concatenate([s0_ref[b, PAIR * j + h] for h in range(PAIR)], axis=1)

    pa = pa_ref[...]
    row = lax.broadcasted_iota(jnp.int32, (rows, 1), 0)
    row_in_chunk = row & (C - 1)
    pa_prev = pltpu.roll(pa, 1, axis=0)
    for b in range(n_bat):
        pa_prev = jnp.where(row == b * seq_rows, prev_scr[b], pa_prev)
        last = (b + 1) * seq_rows - C + t_real - 1
        prev_scr[b] = pa[last:last + 1]
    xm = pa + mu_ref[...] * (pa_prev - pa)

    r = xm[:, :A_WIDTH]
    k = xm[:, A_WIDTH:2 * A_WIDTH]
    v = xm[:, 2 * A_WIDTH:3 * A_WIDTH]
    lora_in = xm[:, 3 * A_WIDTH:3 * A_WIDTH + LORA_W]
    gd = xm[:, 3 * A_WIDTH + LORA_W:]

    lane = lax.broadcasted_iota(jnp.int32, (1, LANE), 1)
    lo_half = lane < HEAD_DIM
    z = jnp.where(lo_half, jnp.tanh(lora_in), lora_in)
    lw = _dot(z.astype(BF16), wlora_ref[...])
    dec_pre = w0_ref[...] + lw[:, :A_WIDTH]
    a = _sigmoid(a0_ref[...] + lw[:, A_WIDTH:])
    neg = -dec_pre
    softplus = jnp.maximum(neg, 0.0) + jnp.log(1.0 + jnp.exp(-jnp.abs(neg)))
    logdec = -jnp.exp(-softplus - DECAY_OFFSET)
    g = _dot(_sigmoid(gd).astype(BF16), wgate_ref[...])

    ri = lax.broadcasted_iota(jnp.int32, (LANE, LANE), 0)
    ci = lax.broadcasted_iota(jnp.int32, (LANE, LANE), 1)
    seg = jnp.where((ri >> HEAD_SHIFT) == (ci >> HEAD_SHIFT), 1.0, 0.0).astype(BF16)
    seg2 = jnp.concatenate([seg, seg], axis=0)

    def headsum(x):
        return jnp.concatenate([_dot(_hi_lo(x[:, j * LANE:(j + 1) * LANE], 1), seg2) for j in range(N_PAIRS)],
                               axis=1)

    kk = k * kk_ref[...]
    kk = kk * lax.rsqrt(jnp.maximum(headsum(kk * kk), 1e-24))
    k = k * (1.0 + (a - 1.0) * ka_ref[...])
    bonus = headsum(r * k * rk_ref[...]) * v

    if t_real < C:
        valid = row_in_chunk < t_real
        logdec = jnp.where(valid, logdec, 0.0)
        kk = jnp.where(valid, kk, 0.0)
        k = jnp.where(valid, k, 0.0)
        v = jnp.where(valid, v, 0.0)

    cum = logdec
    shift = 1
    while shift < C:
        cum = cum + jnp.where(row_in_chunk >= shift, pltpu.roll(cum, shift, axis=0), 0.0)
        shift *= 2
    ends = [cum[(s + 1) * C - 1:(s + 1) * C] for s in range(n_seg)]
    cum_end = jnp.concatenate([jnp.broadcast_to(e, (C, A_WIDTH)) for e in ends], axis=0) if n_seg > 1 else ends[0]
    w_incl = jnp.exp(cum)
    w_prev = jnp.exp(cum - logdec)
    w_inv = jnp.exp(-cum)
    w_end = jnp.exp(cum_end - cum)
    w_chunk = [jnp.exp(e) for e in ends]
    kka = kk * a
    terms = dict(A=-kk * w_prev, R=r * w_incl, B=kka * w_inv, K=k * w_inv, V=v, Be=kka * w_end, Ke=k * w_end)

    C2 = PAIR * C
    t_idx = lax.broadcasted_iota(jnp.int32, (C, 1), 0)
    i_idx = lax.broadcasted_iota(jnp.int32, (1, C2), 1) & (C - 1)
    strict = i_idx < t_idx
    incl = i_idx <= t_idx
    ident = jnp.where(i_idx == t_idx, 1.0, 0.0).astype(F32)
    lo_time = lax.broadcasted_iota(jnp.int32, (1, C2), 1) < C
    n_levels = max(1, (C - 1).bit_length())
    bf = lambda x: x.astype(BF16)

    def bd(x, lo_mask=lo_half):
        return bf(jnp.concatenate([jnp.where(lo_mask, x, 0.0), jnp.where(lo_mask, 0.0, x)], axis=0))

    chains = [(s, j) for s in range(n_seg) for j in range(N_PAIRS)]
    tile = lambda name, s, j: terms[name][s * C:(s + 1) * C, j * LANE:(j + 1) * LANE]
    r_sbs = {ch: tile("R", *ch) for ch in chains}
    stk = {ch: {nm: bd(tile(nm, *ch)) for nm in ("A", "B", "K", "V", "Be", "Ke")} for ch in chains}
    m_ab, aak, arb, ark = {}, {}, {}, {}
    for ch in chains:
        t = stk[ch]
        a_sbs, rb = bf(tile("A", *ch)), bf(r_sbs[ch])
        if C2 % LANE == 0:
            m1 = _dot_nt(jnp.concatenate([a_sbs, rb], axis=0), jnp.concatenate([t["B"], t["K"]], axis=0))
            ab, ak, rbm, rk = m1[:C, :C2], m1[:C, C2:], m1[C:, :C2], m1[C:, C2:]
        else:
            ab, ak = _dot_nt(a_sbs, t["B"]), _dot_nt(a_sbs, t["K"])
            rbm, rk = _dot_nt(rb, t["B"]), _dot_nt(rb, t["K"])
        m_ab[ch] = jnp.where(strict, ab, 0.0)
        aak[ch] = bf(jnp.where(strict, ak, 0.0))
        arb[ch] = bf(jnp.where(incl, rbm, 0.0))
        ark[ch] = bf(jnp.where(incl, rk, 0.0))

    nn = dict(m_ab)
    tinv = {ch: ident + m_ab[ch] for ch in chains}
    for _ in range(1, n_levels):
        for ch in chains:
            nn[ch] = _dot(bf(nn[ch]), bd(nn[ch], lo_time))
        for ch in chains:
            tinv[ch] = tinv[ch] + _dot(bf(nn[ch]), bd(tinv[ch], lo_time))
    akv = {ch: _dot(aak[ch], stk[ch]["V"]) for ch in chains}
    x = {ch: _dot(bf(tinv[ch]), jnp.concatenate([stk[ch]["A"], bd(akv[ch])], axis=1)) for ch in chains}
    a_eff = {ch: bd(x[ch][:, :LANE]) for ch in chains}
    v_eff = {ch: bd(x[ch][:, LANE:]) for ch in chains}
    zed = {ch: _dot(arb[ch], jnp.concatenate([a_eff[ch], v_eff[ch]], axis=1)) for ch in chains}
    rhat = {ch: bf(r_sbs[ch] + zed[ch][:, :LANE]) for ch in chains}
    y0 = {ch: _dot(ark[ch], stk[ch]["V"]) + zed[ch][:, LANE:] for ch in chains}
    p_mat = {ch: bf(_dot_tn(stk[ch]["Be"], a_eff[ch])) for ch in chains}
    q_bd = {ch: _dot_tn(jnp.concatenate([v_eff[ch], stk[ch]["V"]], axis=0),
                        jnp.concatenate([stk[ch]["Be"], stk[ch]["Ke"]], axis=0)) for ch in chains}

    y_rows = []
    for b in range(n_bat):
        state = [s_scr[b * N_PAIRS + j] for j in range(N_PAIRS)]
        for sub in range(n_sub):
            s = b * n_sub + sub
            ys = []
            for j in range(N_PAIRS):
                ch = (s, j)
                hi = bf(state[j])
                lo = bf(state[j] - hi.astype(F32))
                yy = _dot_nt(rhat[ch], jnp.concatenate([bd(hi), bd(lo)], axis=0))
                ys.append(yy[:, :LANE] + yy[:, LANE:] + y0[ch])
                sp = _dot_nt(jnp.concatenate([hi, lo], axis=0), p_mat[ch])
                q_sbs = q_bd[ch][:HEAD_DIM] + q_bd[ch][HEAD_DIM:]
                state[j] = state[j] * w_chunk[s][:, j * LANE:(j + 1) * LANE] + sp[:HEAD_DIM] + sp[HEAD_DIM:] + q_sbs
            y_rows.append(jnp.concatenate(ys, axis=1))
        for j in range(N_PAIRS):
            s_scr[b * N_PAIRS + j] = state[j]

    y = jnp.concatenate(y_rows, axis=0) if n_seg > 1 else y_rows[0]
    inv_n = 1.0 / HEAD_DIM
    mean = headsum(y) * inv_n
    d = y - mean
    var = headsum(d * d) * inv_n
    yn = d * lax.rsqrt(var + LNX_EPS) * lng_ref[...] + lnb_ref[...]
    oa_ref[...] = ((yn + bonus) * g).astype(oa_ref.dtype)

    @pl.when(c == pl.num_programs(1) - 1)
    def _():
        for b in range(n_bat):
            for j in range(N_PAIRS):
                for h in range(PAIR):
                    sout_ref[b, PAIR * j + h] = s_scr[b * N_PAIRS + j][:, h * HEAD_DIM:(h + 1) * HEAD_DIM]


def _rwkv_call(pa2d, shift0, s0_pairs, prm, *, n_seq, n_chunks, chunk, n_sub, n_bat, t_real):
    n_steps = n_chunks // n_sub
    assert n_seq % n_bat == 0 and (n_bat == 1 or n_steps == 1)
    row = lambda b, c: (b * n_steps + c, 0)
    seq3 = lambda b, c: (b, 0, 0)
    fixed = lambda b, c: (0, 0)
    vec = lambda w: pl.BlockSpec((1, w), fixed)
    body = functools.partial(_rwkv_body, chunk=chunk, n_sub=n_sub, n_bat=n_bat, t_real=t_real)
    n_rows = n_seq * n_chunks * chunk
    blk_rows = n_bat * n_sub * chunk
    state_blk = (n_bat, A_HEADS, HEAD_DIM, HEAD_DIM)
    seq4 = lambda b, c: (b, 0, 0, 0)
    return pl.pallas_call(
        body,
        grid=(n_seq // n_bat, n_steps),
        in_specs=[pl.BlockSpec((blk_rows, SHIFT_W), row),
                  pl.BlockSpec((n_bat, 1, SHIFT_W), seq3),
                  pl.BlockSpec(state_blk, seq4),
                  vec(SHIFT_W), vec(A_WIDTH), vec(A_WIDTH),
                  pl.BlockSpec((LORA_W, 2 * A_WIDTH), fixed), pl.BlockSpec((GATE_LORA, A_WIDTH), fixed),
                  vec(A_WIDTH), vec(A_WIDTH), vec(A_WIDTH), vec(A_WIDTH), vec(A_WIDTH)],
        out_specs=[pl.BlockSpec((blk_rows, A_WIDTH), row), pl.BlockSpec(state_blk, seq4)],
        out_shape=[jax.ShapeDtypeStruct((n_rows, A_WIDTH), BF16),
                   jax.ShapeDtypeStruct((n_seq, A_HEADS, HEAD_DIM, HEAD_DIM), F32)],
        scratch_shapes=[pltpu.VMEM((n_bat * N_PAIRS, HEAD_DIM, LANE), F32), pltpu.VMEM((n_bat, 1, SHIFT_W), F32)],
        compiler_params=pltpu.CompilerParams(dimension_semantics=("arbitrary", "arbitrary"),
                                             vmem_limit_bytes=VMEM_LIMIT),
        name="rwkv",
    )(pa2d, shift0, s0_pairs, prm["mu"], prm["w0"], prm["a0"], prm["wlora"], prm["wgate"], prm["kk"], prm["ka"],
      prm["rk"], prm["lng"], prm["lnb"])


def _swa_bias(tq):
    rows = B_GROUP * tq
    grp = jnp.arange(rows, dtype=jnp.int32)[:, None] // tq
    t = jnp.arange(rows, dtype=jnp.int32)[:, None] % tq
    dist_p = t + WINDOW - jnp.arange(WINDOW, dtype=jnp.int32)[None, :]
    dist_c = t - jnp.arange(tq, dtype=jnp.int32)[None, :]

    def bias(dist, kv):
        slope = sum(jnp.where(grp == g, 2.0 ** -(kv * B_GROUP + g + 1), 0.0) for g in range(B_GROUP))
        return jnp.where((dist >= 0) & (dist < WINDOW), -slope * dist.astype(F32), MASK_NEG)

    return (jnp.stack([bias(dist_p, kv) for kv in range(B_KV_HEADS)]),
            jnp.stack([bias(dist_c, kv) for kv in range(B_KV_HEADS)]))


def _swa_body(sink_ref, q_ref, kp_ref, vp_ref, kc_ref, vc_ref, bp_ref, bc_ref, gn_ref, o_ref, *cache_refs, tq, n_bat,
              n_qb, t_real, first_has_prev):
    for s in range(n_bat * n_qb):
        rs = slice(s * tq, (s + 1) * tq)
        kc, vc = kc_ref[rs], vc_ref[rs]
        if n_qb > 1 and s > 0:
            ps = slice((s - 1) * tq, s * tq)
            kp, vp, has_prev = kc_ref[ps], vc_ref[ps], True
        else:
            kp, vp, has_prev = kp_ref[s], vp_ref[s], first_has_prev
        _swa_one(sink_ref, q_ref[rs], kp, vp, kc, vc, bp_ref, bc_ref, gn_ref, o_ref.at[rs], tq=tq,
                 first_has_prev=has_prev)
        if cache_refs:
            row = lax.broadcasted_iota(jnp.int32, (WINDOW, 1), 0)
            for out_ref, old, new in zip(cache_refs, (kp, vp), (kc, vc)):
                new_tail = jnp.concatenate([pltpu.roll(new, tq - t_real, axis=0)] * (WINDOW // tq), axis=0)
                out_ref[s] = jnp.where(row >= WINDOW - t_real, new_tail, pltpu.roll(old, WINDOW - t_real, axis=0))


def _swa_one(sink_ref, q, kp, vp, kc, vc, bp_ref, bc_ref, gn_ref, o_ref, *, tq, first_has_prev):
    nblk = pl.program_id(1)
    kp = kp.astype(BF16)
    kc = kc.astype(BF16)
    lane = lax.broadcasted_iota(jnp.int32, (1, LANE), 1)
    prev_pen = 0.0 if first_has_prev else jnp.where(nblk > 0, 0.0, MASK_NEG)

    tiles = [None] * B_GROUP
    for kv in range(B_KV_HEADS):
        mk = (lane < HEAD_DIM) if kv == 0 else (lane >= HEAD_DIM)
        q_st = jnp.concatenate([jnp.where(mk, q[:, g * LANE:(g + 1) * LANE], 0.0) for g in range(B_GROUP)],
                               axis=0).astype(BF16)
        s_p = _dot_nt(q_st, kp)
        s_c = _dot_nt(q_st, kc)
        vpm = jnp.where(mk, vp, 0.0).astype(BF16)
        vcm = jnp.where(mk, vc, 0.0).astype(BF16)
        for g in range(B_GROUP):
            rs = slice(g * tq, (g + 1) * tq)
            sink = sink_ref[kv * B_GROUP + g]
            sp = s_p[rs] + (bp_ref[kv, rs, :] + prev_pen)
            sc = s_c[rs] + bc_ref[kv, rs, :]
            if tq == WINDOW:
                m = jnp.maximum(jnp.max(jnp.maximum(sp, sc), axis=-1, keepdims=True), sink)
            else:
                m = jnp.maximum(jnp.maximum(jnp.max(sp, axis=-1, keepdims=True),
                                            jnp.max(sc, axis=-1, keepdims=True)), sink)
            e_p = jnp.exp(sp - m)
            e_c = jnp.exp(sc - m)
            if tq == WINDOW:
                e_sum = jnp.sum(e_p + e_c, axis=-1, keepdims=True)
            else:
                e_sum = jnp.sum(e_p, axis=-1, keepdims=True) + jnp.sum(e_c, axis=-1, keepdims=True)
            inv = 1.0 / (e_sum + jnp.exp(sink - m))
            o = (_dot(e_p.astype(BF16), vpm) + _dot(e_c.astype(BF16), vcm)) * inv
            tiles[g] = o if tiles[g] is None else tiles[g] + o

    ssq = sum(jnp.sum(x * x, axis=-1, keepdims=True) for x in tiles)
    inv_rms = lax.rsqrt(ssq * (1.0 / B_WIDTH) + RMS_EPS)
    o_ref[...] = (jnp.concatenate(tiles, axis=1) * inv_rms * gn_ref[...]).astype(o_ref.dtype)


def _swa_call(sink, q2d, kprev, vprev, kcur2d, vcur2d, gn, *, n_seq, n_blk, tq, n_bat, first_has_prev, prev_map,
              out_dtype, n_qb=1, cache_rows=0):
    assert n_seq % n_bat == 0 and (n_bat == 1 or n_blk * n_qb == 1) and (cache_rows == 0 or n_blk * n_qb == 1)
    row = lambda b, n: (b * n_blk + n, 0)
    body = functools.partial(_swa_body, tq=tq, n_bat=n_bat, n_qb=n_qb, t_real=cache_rows,
                             first_has_prev=first_has_prev)
    bias_p, bias_c = _swa_bias(tq)
    whole = lambda b, n: (0, 0, 0)
    blk_rows = n_bat * n_qb * tq
    out_specs = [pl.BlockSpec((blk_rows, B_WIDTH), row)]
    out_shape = [jax.ShapeDtypeStruct((n_seq * n_blk * n_qb * tq, B_WIDTH), out_dtype)]
    if cache_rows:
        out_specs += [pl.BlockSpec((n_bat, WINDOW, B_KV_WIDTH), prev_map)] * 2
        out_shape += [jax.ShapeDtypeStruct((n_seq, WINDOW, B_KV_WIDTH), F32)] * 2
    return pl.pallas_call(
        body,
        grid=(n_seq // n_bat, n_blk),
        in_specs=[pl.BlockSpec(memory_space=pltpu.SMEM),
                  pl.BlockSpec((blk_rows, B_WIDTH), row),
                  pl.BlockSpec((n_bat, WINDOW, B_KV_WIDTH), prev_map),
                  pl.BlockSpec((n_bat, WINDOW, B_KV_WIDTH), prev_map),
                  pl.BlockSpec((blk_rows, B_KV_WIDTH), row),
                  pl.BlockSpec((blk_rows, B_KV_WIDTH), row),
                  pl.BlockSpec(bias_p.shape, whole),
                  pl.BlockSpec(bias_c.shape, whole),
                  pl.BlockSpec((1, B_WIDTH), lambda b, n: (0, 0))],
        out_specs=out_specs,
        out_shape=out_shape,
        compiler_params=pltpu.CompilerParams(dimension_semantics=("arbitrary", "arbitrary"),
                                             vmem_limit_bytes=VMEM_LIMIT),
        name="swa",
    )(sink, q2d, kprev, vprev, kcur2d, vcur2d, bias_p, bias_c, gn)


def _route_rows(lg):
    lane = lax.broadcasted_iota(jnp.int32, (1, ROUTE_W), 1)
    lane_f = lane.astype(F32)
    no_lane = float(ROUTE_W)
    is_group = lane < N_GROUPS
    m_g = jnp.max(jnp.where(is_group, lg, MASK_NEG), axis=-1, keepdims=True)
    g_idx = jnp.min(jnp.where(is_group & (lg == m_g), lane_f, no_lane), axis=-1, keepdims=True)
    p_group = 1.0 / jnp.sum(jnp.where(is_group, jnp.exp(lg - m_g), 0.0), axis=-1, keepdims=True)
    e_lane = lane - N_GROUPS
    in_group = (e_lane >= 0) & (e_lane < N_EXPERTS) & ((e_lane >> 2).astype(F32) == g_idx)
    m_1 = jnp.max(jnp.where(in_group, lg, MASK_NEG), axis=-1, keepdims=True)
    i_1 = jnp.min(jnp.where(in_group & (lg == m_1), lane_f, no_lane), axis=-1, keepdims=True)
    rest = in_group & (lane_f != i_1)
    m_2 = jnp.max(jnp.where(rest, lg, MASK_NEG), axis=-1, keepdims=True)
    i_2 = jnp.min(jnp.where(rest & (lg == m_2), lane_f, no_lane), axis=-1, keepdims=True)
    ratio = jnp.exp(m_2 - m_1)
    w_1 = p_group / (1.0 + ratio)
    return jnp.where(lane == 0, i_1 - N_GROUPS,
                     jnp.where(lane == 1, i_2 - N_GROUPS,
                               jnp.where(lane == 2, w_1, jnp.where(lane == 3, w_1 * ratio, 0.0))))


def _outproj_body(*refs, seg_sizes):
    n_seg = len(seg_sizes)
    x_refs, oa_refs, ob_refs = refs[:n_seg], refs[n_seg:2 * n_seg], refs[2 * n_seg:3 * n_seg]
    wa_ref, wb_ref, g_ref, wr_ref, br_ref, x1_ref, h2_ref, rt_ref, rtt_ref = refs[3 * n_seg:]

    def run(x_ref, oa_ref, ob_ref):
        x1 = x_ref[...] + _dot(oa_ref[...].astype(BF16), wa_ref[...]) + _dot(ob_ref[...].astype(BF16), wb_ref[...])
        x1_ref[...] = x1
        h2 = x1 * lax.rsqrt(jnp.mean(x1 * x1, axis=-1, keepdims=True) + RMS_EPS) * g_ref[...]
        h2b = h2.astype(BF16)
        h2_ref[...] = h2b
        rt = _route_rows(_dot(h2b, wr_ref[...]) + br_ref[...])
        rt_ref[...] = rt
        rtt_ref[...] = rt.T[:ROUTE_ROWS]

    _run_segment(pl.program_id(0), seg_sizes, run, list(zip(x_refs, oa_refs, ob_refs)))


def _outproj_call(x_list, x_firsts, oa_list, ob_list, seg_sizes, wa, wb, g, wr, br, *, tm):
    d = x_list[0].shape[1]
    n_tiles = sum(seg_sizes)
    n = n_tiles * tm
    row = lambda i: (i, 0)
    fixed = lambda i: (0, 0)
    zeros = [0] * len(seg_sizes)
    return pl.pallas_call(
        functools.partial(_outproj_body, seg_sizes=tuple(seg_sizes)),
        grid=(n_tiles,),
        in_specs=(_seg_specs(tm, d, seg_sizes, x_firsts) + _seg_specs(tm, A_WIDTH, seg_sizes, zeros)
                  + _seg_specs(tm, B_WIDTH, seg_sizes, zeros)
                  + [pl.BlockSpec((A_WIDTH, d), fixed), pl.BlockSpec((B_WIDTH, d), fixed), pl.BlockSpec((1, d), fixed),
                     pl.BlockSpec((d, ROUTE_W), fixed), pl.BlockSpec((1, ROUTE_W), fixed)]),
        out_specs=[pl.BlockSpec((tm, d), row), pl.BlockSpec((tm, d), row), pl.BlockSpec((tm, ROUTE_W), row),
                   pl.BlockSpec((ROUTE_ROWS, tm), row)],
        out_shape=[jax.ShapeDtypeStruct((n, d), F32), jax.ShapeDtypeStruct((n, d), BF16),
                   jax.ShapeDtypeStruct((n, ROUTE_W), F32),
                   jax.ShapeDtypeStruct((n_tiles * ROUTE_ROWS, tm), F32)],
        compiler_params=pltpu.CompilerParams(dimension_semantics=("arbitrary",), vmem_limit_bytes=VMEM_LIMIT),
        name="outproj",
    )(*x_list, *oa_list, *ob_list, wa, wb, g, wr, br)


def _expert_body(vb_ref, ve_ref, lo_ref, hi_ref, x_ref, wg_ref, wu_ref, wd_ref, y_ref, wg_bf, wu_bf, wd_bf):
    v = pl.program_id(0)
    pv = jnp.maximum(v - 1, 0)
    lo, hi = lo_ref[v], hi_ref[v]
    first_visit = (v == 0) | (vb_ref[v] != vb_ref[pv])

    @pl.when((v == 0) | (ve_ref[v] != ve_ref[pv]))
    def _():
        wg_bf[...] = wg_ref[0].astype(BF16)
        wu_bf[...] = wu_ref[0].astype(BF16)
        wd_bf[...] = wd_ref[0].astype(BF16)

    @pl.when(hi > lo)
    def _():
        x = x_ref[...]
        gate = _dot(x, wg_bf[...])
        up = _dot(x, wu_bf[...])
        mid = (gate * _sigmoid(gate) * up).astype(BF16)
        y = _dot(mid, wd_bf[...])
        row = lax.broadcasted_iota(jnp.int32, (y.shape[0], 1), 0)
        mine = (row >= lo) & (row < hi)

        @pl.when(first_visit)
        def _():
            y_ref[...] = jnp.where(mine, y, 0.0)

        @pl.when(jnp.logical_not(first_visit))
        def _():
            y_ref[...] = jnp.where(mine, y, y_ref[...])


def _expert_call(visits, xs, wg, wu, wd, *, tb):
    n_rows, d = xs.shape
    ff = wg.shape[-1]
    vb, ve, lo, hi = visits
    blk = lambda v, vb, ve, lo, hi: (vb[v], 0)
    wsel = lambda v, vb, ve, lo, hi: (ve[v], 0, 0)
    grid_spec = pltpu.PrefetchScalarGridSpec(
        num_scalar_prefetch=4,
        grid=(vb.shape[0],),
        in_specs=[pl.BlockSpec((tb, d), blk), pl.BlockSpec((1, d, ff), wsel), pl.BlockSpec((1, d, ff), wsel),
                  pl.BlockSpec((1, ff, d), wsel)],
        out_specs=pl.BlockSpec((tb, d), blk),
        scratch_shapes=[pltpu.VMEM((d, ff), BF16), pltpu.VMEM((d, ff), BF16), pltpu.VMEM((ff, d), BF16)],
    )
    return pl.pallas_call(
        _expert_body,
        grid_spec=grid_spec,
        out_shape=jax.ShapeDtypeStruct((n_rows, d), F32),
        compiler_params=pltpu.CompilerParams(dimension_semantics=("arbitrary",), vmem_limit_bytes=VMEM_LIMIT),
        name="experts",
    )(vb, ve, lo, hi, xs, wg, wu, wd)


def _final_body(*refs, seg_sizes):
    n_seg = len(seg_sizes)
    g_ref, o_ref = refs[4 * n_seg:]

    def run(x1_ref, y0_ref, y1_ref, rt_ref):
        rt = rt_ref[...]
        x = x1_ref[...] + (rt[:, 2:3] * y0_ref[...] + rt[:, 3:4] * y1_ref[...])
        o_ref[...] = x * lax.rsqrt(jnp.mean(x * x, axis=-1, keepdims=True) + RMS_EPS) * g_ref[...]

    _run_segment(pl.program_id(0), seg_sizes, run, [refs[4 * s:4 * s + 4] for s in range(n_seg)])


def _final_call(sources, firsts, seg_sizes, g, *, tm):
    d = sources[0][0].shape[1]
    n_tiles = sum(seg_sizes)
    in_specs, start = [], 0
    for size, first in zip(seg_sizes, firsts):
        src = lambda i, s=start, f=first, n=size: (f + jnp.clip(i - s, 0, n - 1), 0)
        in_specs += [pl.BlockSpec((tm, d), src)] * 3 + [pl.BlockSpec((tm, ROUTE_W), src)]
        start += size
    return pl.pallas_call(
        functools.partial(_final_body, seg_sizes=tuple(seg_sizes)),
        grid=(n_tiles,),
        in_specs=in_specs + [pl.BlockSpec((1, d), lambda i: (0, 0))],
        out_specs=pl.BlockSpec((tm, d), lambda i: (i, 0)),
        out_shape=jax.ShapeDtypeStruct((n_tiles * tm, d), F32),
        compiler_params=pltpu.CompilerParams(dimension_semantics=("arbitrary",), vmem_limit_bytes=VMEM_LIMIT),
        name="final",
    )(*[a for src in sources for a in src], g)


def _dispatch(flat_e, n, tb):
    n_assign = n * TOP_K_INNER
    idx_bits = (n_assign - 1).bit_length()
    assert N_EXPERTS << idx_bits < 2 ** 31 and n_assign % tb == 0
    ids = jnp.arange(n_assign, dtype=jnp.int32)
    sorted_key = jnp.sort((flat_e << idx_bits) | ids)
    sorted_id = sorted_key & ((1 << idx_bits) - 1)
    src_tok = sorted_id - n * (sorted_id // n)

    experts = jnp.arange(N_EXPERTS, dtype=jnp.int32)
    onehot = (flat_e[:, None] == experts[None, :]).astype(jnp.int32)
    csum = jnp.cumsum(onehot, axis=0)
    counts = csum[-1]
    ends = jnp.cumsum(counts)
    starts = ends - counts
    dest = (jnp.sum((csum - 1 + starts[None, :]) * onehot, axis=1)).astype(jnp.int32)

    n_blocks = n_assign // tb
    n_visits = n_blocks + N_EXPERTS - 1
    first_blk = starts // tb
    n_vis_e = jnp.where(counts > 0, (ends + tb - 1) // tb - first_blk, 0)
    v_end = jnp.cumsum(n_vis_e)
    v = jnp.arange(n_visits, dtype=jnp.int32)
    valid = v < v_end[-1]
    last_e = jnp.max(jnp.where(counts > 0, experts, 0))
    ve = jnp.where(valid, jnp.sum((v_end[None, :] <= v[:, None]).astype(jnp.int32), axis=1), last_e)
    pick = lambda a: jnp.sum(jnp.where(ve[:, None] == experts[None, :], a[None, :], 0), axis=1)
    vb = pick(first_blk) + v - pick(v_end - n_vis_e)
    lo = jnp.maximum(pick(starts), vb * tb) - vb * tb
    hi = jnp.minimum(pick(ends), (vb + 1) * tb) - vb * tb
    vb = jnp.where(valid, vb, n_blocks - 1)
    lo = jnp.where(valid, lo, 0)
    hi = jnp.where(valid, hi, 0)
    return dest, src_tok, (vb.astype(jnp.int32), ve.astype(jnp.int32), lo.astype(jnp.int32), hi.astype(jnp.int32))


def _q_perm():
    cols = []
    for g in range(B_GROUP):
        for kv in range(B_KV_HEADS):
            h = kv * B_GROUP + g
            cols.extend(range(h * HEAD_DIM, (h + 1) * HEAD_DIM))
    return jnp.array(cols, dtype=jnp.int32)


def _layer(x_prompt, x_sample, state_rwkv, state_shift, cache_win_k, cache_win_v, prm, norm_final_g, *,
           chunk, n_sub, moe_block):
    bp, tp, d = x_prompt.shape
    bs, ts, _ = x_sample.shape
    n_p, n_s = bp * tp, bs * ts
    ts_pad = 8
    tm = n_s
    assert tp % tm == 0 and tm % 8 == 0 and cache_win_k.shape[1] == WINDOW
    xp = x_prompt.reshape(n_p, d)
    xs = x_sample.reshape(n_s, d)

    qp = _q_perm()
    w_in = prm["w_in"]
    w_in = jnp.concatenate([w_in[:, :SHIFT_W], w_in[:, SHIFT_W:SHIFT_W + B_WIDTH][:, qp],
                            w_in[:, SHIFT_W + B_WIDTH:]], axis=1).astype(BF16)
    gn = prm["attn_norm_g"][qp][None]
    w_out = prm["w_out"]
    wa, wb = w_out[:A_WIDTH].astype(BF16), w_out[A_WIDTH:][qp].astype(BF16)
    pad_lanes = jnp.zeros((d, ROUTE_W - N_GROUPS - N_EXPERTS), F32)
    wr = jnp.concatenate([prm["w_route_group"], prm["w_route_expert"], pad_lanes], axis=1).astype(BF16)
    br = jnp.concatenate([prm["b_route_group"], prm["b_route_expert"], pad_lanes[0]])[None]

    zero_blk = jnp.zeros((LORA_W // 2, A_WIDTH), F32)
    wlora = jnp.concatenate([jnp.concatenate([prm["w_decay_up"], zero_blk], axis=1),
                             jnp.concatenate([zero_blk, prm["w_iclr_up"]], axis=1)], axis=0).astype(BF16)
    rp = dict(mu=prm["mu_shift"][None], w0=prm["w_decay0"][None], a0=prm["w_iclr0"][None], wlora=wlora,
              wgate=prm["w_gate_up"].astype(BF16), kk=prm["k_k"][None], ka=prm["k_a"][None],
              rk=prm["r_k"].reshape(1, A_WIDTH), lng=prm["lnx_g"][None], lnb=prm["lnx_b"][None])

    half = bp // 2 if bp % 2 == 0 else bp
    groups = [(s0, min(s0 + half, bp)) for s0 in range(0, bp, half)]
    tiles_per_seq = tp // tm
    nb = tp // WINDOW
    n_qb = _row_tile(nb, 4)
    kv4 = lambda a: a.reshape(a.shape[0], a.shape[1], B_KV_HEADS, HEAD_DIM)
    finals, s_p, sh_p, k_p, v_p = [], [], [], [], []
    for gi, (s0, s1) in enumerate(groups):
        has_sample = gi == len(groups) - 1
        n_seq = s1 - s0
        n_pt = n_seq * tiles_per_seq
        n_pg = n_pt * tm
        seg_sizes = [n_pt] + ([1] if has_sample else [])
        pa, q, k, v = _inproj_call([xp] + ([xs] if has_sample else []), [s0 * tiles_per_seq, 0], seg_sizes,
                                   prm["norm_mix_g"][None], w_in, tm=tm)

        oa_p, s_pg = _rwkv_call(pa, jnp.zeros((n_seq, 1, SHIFT_W), F32),
                                jnp.zeros((n_seq,) + state_rwkv.shape[1:], F32), rp, n_seq=n_seq,
                                n_chunks=tp // chunk, chunk=chunk, n_sub=n_sub, n_bat=1, t_real=chunk)
        ob_p, = _swa_call(prm["attn_sink"], q, k.reshape(-1, WINDOW, B_KV_WIDTH), v.reshape(-1, WINDOW, B_KV_WIDTH),
                          k, v, gn, n_seq=n_seq, n_blk=nb // n_qb, tq=WINDOW, n_bat=1, n_qb=n_qb,
                          first_has_prev=False, out_dtype=BF16,
                          prev_map=lambda b, i: (b * nb + jnp.maximum(i * n_qb - 1, 0), 0, 0))
        oa_list, ob_list = [oa_p], [ob_p]
        if has_sample:
            pad_rows = lambda a: jnp.pad(a[n_pg:].reshape(bs, ts, -1), ((0, 0), (0, ts_pad - ts), (0, 0))).reshape(
                bs * ts_pad, -1)
            oa_s, s_s = _rwkv_call(pad_rows(pa), state_shift[:, None, :], state_rwkv, rp, n_seq=bs,
                                   n_chunks=1, chunk=ts_pad, n_sub=1, n_bat=_row_tile(bs, 8), t_real=ts)
            ob_s, k_s, v_s = _swa_call(prm["attn_sink"], pad_rows(q).astype(F32),
                                       cache_win_k.reshape(bs, WINDOW, B_KV_WIDTH),
                                       cache_win_v.reshape(bs, WINDOW, B_KV_WIDTH), pad_rows(k), pad_rows(v), gn,
                                       n_seq=bs, n_blk=1, tq=ts_pad, n_bat=_row_tile(bs, 16), first_has_prev=True,
                                       prev_map=lambda b, i: (b, 0, 0), out_dtype=F32, cache_rows=ts)
            oa_list.append(oa_s.reshape(bs, ts_pad, A_WIDTH)[:, :ts].reshape(n_s, A_WIDTH))
            ob_list.append(ob_s.reshape(bs, ts_pad, B_WIDTH)[:, :ts].reshape(n_s, B_WIDTH))
            sh_s = pa[n_pg:].reshape(bs, ts, SHIFT_W)[:, -1]

        x1, h2, route, route_t = _outproj_call([xp] + ([xs] if has_sample else []), [s0 * tiles_per_seq, 0],
                                               oa_list, ob_list, seg_sizes, wa, wb, prm["norm_ffn_g"][None], wr, br,
                                               tm=tm)
        n = sum(seg_sizes) * tm
        route_t = route_t.reshape(n // tm, ROUTE_ROWS, tm)
        flat_e = jnp.concatenate([route_t[:, j, :].reshape(n) for j in range(TOP_K_INNER)]).astype(jnp.int32)
        dest, src_tok, visits = _dispatch(flat_e, n, moe_block)
        ybuf = _expert_call(visits, h2[src_tok], prm["w_exp_gate"], prm["w_exp_up"], prm["w_exp_down"], tb=moe_block)
        finals.append((x1, ybuf[dest[:n]], ybuf[dest[n:]], route))

        last_rows = lambda a, m: jnp.stack([a[(b + 1) * tp - m:(b + 1) * tp] for b in range(n_seq)], axis=0)
        s_p.append(s_pg)
        sh_p.append(last_rows(pa, 1)[:, 0])
        k_p.append(last_rows(k, WINDOW))
        v_p.append(last_rows(v, WINDOW))

    gf = norm_final_g[None]
    n_pts = [(s1 - s0) * tiles_per_seq for s0, s1 in groups]
    y_p = _final_call(finals, [0] * len(groups), n_pts, gf, tm=tm)
    y_s = _final_call(finals[-1:], n_pts[-1:], [1], gf, tm=tm)
    cat = lambda parts: jnp.concatenate(parts, axis=0)
    return (y_p.reshape(bp, tp, d), y_s.reshape(bs, ts, d), cat(s_p), cat(sh_p), kv4(cat(k_p)), kv4(cat(v_p)),
            s_s, sh_s, kv4(k_s), kv4(v_s))


def kernel(x_prompt, x_sample, state_rwkv, state_shift, cache_win_k, cache_win_v, norm_mix_g, w_in, mu_shift, w_decay0, w_decay_up, w_iclr0, w_iclr_up, w_gate_up, k_k, k_a, r_k, lnx_g, lnx_b, attn_sink, attn_norm_g, w_out, norm_ffn_g, w_route_group, b_route_group, w_route_expert, b_route_expert, w_exp_gate, w_exp_up, w_exp_down, norm_final_g):
    assert norm_mix_g.shape[0] == 1, "single-layer trunk"
    prm = dict(norm_mix_g=norm_mix_g[0], w_in=w_in[0], mu_shift=mu_shift[0], w_decay0=w_decay0[0],
               w_decay_up=w_decay_up[0], w_iclr0=w_iclr0[0], w_iclr_up=w_iclr_up[0], w_gate_up=w_gate_up[0],
               k_k=k_k[0], k_a=k_a[0], r_k=r_k[0], lnx_g=lnx_g[0], lnx_b=lnx_b[0], attn_sink=attn_sink[0],
               attn_norm_g=attn_norm_g[0], w_out=w_out[0], norm_ffn_g=norm_ffn_g[0],
               w_route_group=w_route_group[0], b_route_group=b_route_group[0],
               w_route_expert=w_route_expert[0], b_route_expert=b_route_expert[0],
               w_exp_gate=w_exp_gate[0], w_exp_up=w_exp_up[0], w_exp_down=w_exp_down[0])
    outs = _layer(x_prompt, x_sample, state_rwkv[0], state_shift[0], cache_win_k[0], cache_win_v[0], prm,
                  norm_final_g, chunk=64, n_sub=4, moe_block=512)
    y_p, y_s, s_p, sh_p, kp, vp, s_s, sh_s, ks, vs = outs
    return (y_p, y_s, s_p[None], sh_p[None], kp[None], vp[None], s_s[None], sh_s[None], ks[None], vs[None])
```

```python
import functools

import jax
import jax.numpy as jnp
from jax import lax
from jax.experimental import pallas as pl
from jax.experimental.pallas import tpu as pltpu

F32 = jnp.float32
BF16 = jnp.bfloat16

HEAD_DIM = 64
A_HEADS = 8
A_WIDTH = A_HEADS * HEAD_DIM
B_HEADS = 8
B_KV_HEADS = 2
B_GROUP = B_HEADS // B_KV_HEADS
B_WIDTH = B_HEADS * HEAD_DIM
B_KV_WIDTH = B_KV_HEADS * HEAD_DIM
DECAY_LORA = 64
ICLR_LORA = 64
GATE_LORA = 128
LORA_W = DECAY_LORA + ICLR_LORA
SHIFT_W = 3 * A_WIDTH + LORA_W + GATE_LORA
IN_W = SHIFT_W + B_WIDTH + 2 * B_KV_WIDTH
WINDOW = 128
N_GROUPS = 4
EXPERTS_PER_GROUP = 4
N_EXPERTS = N_GROUPS * EXPERTS_PER_GROUP
TOP_K_INNER = 2
EXPERT_FF = 512
RMS_EPS = 1e-6
LNX_EPS = 64e-5
DECAY_OFFSET = 0.5

LANE = 128
PAIR = LANE // HEAD_DIM
N_PAIRS = A_HEADS // PAIR
ROUTE_W = LANE
ROUTE_ROWS = 8
MASK_NEG = -1e30
VMEM_LIMIT = 48 * 1024 * 1024

HEAD_SHIFT = HEAD_DIM.bit_length() - 1
HI = lax.Precision.HIGHEST


def _row_tile(n, cap):
    t = cap
    while n % t:
        t //= 2
    return t


def _dot(a, b, precision=None):
    return jnp.dot(a, b, preferred_element_type=F32, precision=precision)


def _dot_nt(a, b, precision=None):
    return lax.dot_general(a, b, (((1,), (1,)), ((), ())), preferred_element_type=F32, precision=precision)


def _dot_tn(a, b, precision=None):
    return lax.dot_general(a, b, (((0,), (0,)), ((), ())), preferred_element_type=F32, precision=precision)


def _sigmoid(x):
    return 1.0 / (1.0 + jnp.exp(-x))


def _hi_lo(x, axis):
    hi = x.astype(BF16)
    lo = (x - hi.astype(F32)).astype(BF16)
    return jnp.concatenate([hi, lo], axis=axis)


def _run_segment(i, seg_sizes, run, seg_refs):
    start = 0
    for size, refs in zip(seg_sizes, seg_refs):
        pl.when((i >= start) & (i < start + size))(functools.partial(run, *refs))
        start += size


def _seg_specs(tm, width, seg_sizes, firsts):
    specs, start = [], 0
    for size, first in zip(seg_sizes, firsts):
        specs.append(pl.BlockSpec((tm, width),
                                  lambda i, s=start, f=first, n=size: (f + jnp.clip(i - s, 0, n - 1), 0)))
        start += size
    return specs


def _inproj_body(*refs, seg_sizes):
    n_seg = len(seg_sizes)
    g_ref, w_ref, pa_ref, q_ref, k_ref, v_ref = refs[n_seg:]

    def run(x_ref):
        x = x_ref[...]
        h = x * lax.rsqrt(jnp.mean(x * x, axis=-1, keepdims=True) + RMS_EPS) * g_ref[...]
        p = _dot(h.astype(BF16), w_ref[...])
        pa_ref[...] = p[:, :SHIFT_W]
        q_ref[...] = (p[:, SHIFT_W:SHIFT_W + B_WIDTH] * (HEAD_DIM ** -0.5)).astype(BF16)
        k_ref[...] = p[:, SHIFT_W + B_WIDTH:SHIFT_W + B_WIDTH + B_KV_WIDTH]
        v_ref[...] = p[:, SHIFT_W + B_WIDTH + B_KV_WIDTH:]

    _run_segment(pl.program_id(0), seg_sizes, run, [(r,) for r in refs[:n_seg]])


def _inproj_call(xs_list, firsts, seg_sizes, g, w_bf16, *, tm):
    d = xs_list[0].shape[1]
    n_tiles = sum(seg_sizes)
    n = n_tiles * tm
    row = lambda i: (i, 0)
    fixed = lambda i: (0, 0)
    return pl.pallas_call(
        functools.partial(_inproj_body, seg_sizes=tuple(seg_sizes)),
        grid=(n_tiles,),
        in_specs=_seg_specs(tm, d, seg_sizes, firsts) + [pl.BlockSpec((1, d), fixed), pl.BlockSpec((d, IN_W), fixed)],
        out_specs=[pl.BlockSpec((tm, SHIFT_W), row), pl.BlockSpec((tm, B_WIDTH), row),
                   pl.BlockSpec((tm, B_KV_WIDTH), row), pl.BlockSpec((tm, B_KV_WIDTH), row)],
        out_shape=[jax.ShapeDtypeStruct((n, SHIFT_W), F32), jax.ShapeDtypeStruct((n, B_WIDTH), BF16),
                   jax.ShapeDtypeStruct((n, B_KV_WIDTH), F32), jax.ShapeDtypeStruct((n, B_KV_WIDTH), F32)],
        compiler_params=pltpu.CompilerParams(dimension_semantics=("arbitrary",), vmem_limit_bytes=VMEM_LIMIT),
        name="inproj",
    )(*xs_list, g, w_bf16)


def _rwkv_body(pa_ref, shift0_ref, s0_ref, mu_ref, w0_ref, a0_ref, wlora_ref, wgate_ref, kk_ref, ka_ref,
               rk_ref, lng_ref, lnb_ref, oa_ref, sout_ref, s_scr, prev_scr, *, chunk, n_sub, n_bat, t_real):
    C = chunk
    seq_rows = n_sub * C
    n_seg = n_bat * n_sub
    rows = n_seg * C
    c = pl.program_id(1)

    @pl.when(c == 0)
    def _():
        prev_scr[...] = shift0_ref[...]
        for b in range(n_bat):
            for j in range(N_PAIRS):
                s_scr[b * N_PAIRS + j] = jnp.concatenate([s0_ref[b, PAIR * j + h] for h in range(PAIR)], axis=1)

    pa = pa_ref[...]
    row = lax.broadcasted_iota(jnp.int32, (rows, 1), 0)
    row_in_chunk = row & (C - 1)
    pa_prev = pltpu.roll(pa, 1, axis=0)
    for b in range(n_bat):
        pa_prev = jnp.where(row == b * seq_rows, prev_scr[b], pa_prev)
        last = (b + 1) * seq_rows - C + t_real - 1
        prev_scr[b] = pa[last:last + 1]
    xm = pa + mu_ref[...] * (pa_prev - pa)

    r = xm[:, :A_WIDTH]
    k = xm[:, A_WIDTH:2 * A_WIDTH]
    v = xm[:, 2 * A_WIDTH:3 * A_WIDTH]
    lora_in = xm[:, 3 * A_WIDTH:3 * A_WIDTH + LORA_W]
    gd = xm[:, 3 * A_WIDTH + LORA_W:]

    lane = lax.broadcasted_iota(jnp.int32, (1, LANE), 1)
    lo_half = lane < HEAD_DIM
    z = jnp.where(lo_half, jnp.tanh(lora_in), lora_in)
    lw = _dot(z.astype(BF16), wlora_ref[...])
    dec_pre = w0_ref[...] + lw[:, :A_WIDTH]
    a = _sigmoid(a0_ref[...] + lw[:, A_WIDTH:])
    neg = -dec_pre
    softplus = jnp.maximum(neg, 0.0) + jnp.log(1.0 + jnp.exp(-jnp.abs(neg)))
    logdec = -jnp.exp(-softplus - DECAY_OFFSET)
    g = _dot(_sigmoid(gd).astype(BF16), wgate_ref[...])

    ri = lax.broadcasted_iota(jnp.int32, (LANE, LANE), 0)
    ci = lax.broadcasted_iota(jnp.int32, (LANE, LANE), 1)
    seg = jnp.where((ri >> HEAD_SHIFT) == (ci >> HEAD_SHIFT), 1.0, 0.0).astype(BF16)
    seg2 = jnp.concatenate([seg, seg], axis=0)

    def headsum(x):
        return jnp.concatenate([_dot(_hi_lo(x[:, j * LANE:(j + 1) * LANE], 1), seg2) for j in range(N_PAIRS)],
                               axis=1)

    kk = k * kk_ref[...]
    kk = kk * lax.rsqrt(jnp.maximum(headsum(kk * kk), 1e-24))
    k = k * (1.0 + (a - 1.0) * ka_ref[...])
    bonus = headsum(r * k * rk_ref[...]) * v

    if t_real < C:
        valid = row_in_chunk < t_real
        logdec = jnp.where(valid, logdec, 0.0)
        kk = jnp.where(valid, kk, 0.0)
        k = jnp.where(valid, k, 0.0)
        v = jnp.where(valid, v, 0.0)

    cum = logdec
    shift = 1
    while shift < C:
        cum = cum + jnp.where(row_in_chunk >= shift, pltpu.roll(cum, shift, axis=0), 0.0)
        shift *= 2
    ends = [cum[(s + 1) * C - 1:(s + 1) * C] for s in range(n_seg)]
    cum_end = jnp.concatenate([jnp.broadcast_to(e, (C, A_WIDTH)) for e in ends], axis=0) if n_seg > 1 else ends[0]
    w_incl = jnp.exp(cum)
    w_prev = jnp.exp(cum - logdec)
    w_inv = jnp.exp(-cum)
    w_end = jnp.exp(cum_end - cum)
    w_chunk = [jnp.exp(e) for e in ends]
    kka = kk * a
    terms = dict(A=-kk * w_prev, R=r * w_incl, B=kka * w_inv, K=k * w_inv, V=v, Be=kka * w_end, Ke=k * w_end)

    C2 = PAIR * C
    t_idx = lax.broadcasted_iota(jnp.int32, (C, 1), 0)
    i_idx = lax.broadcasted_iota(jnp.int32, (1, C2), 1) & (C - 1)
    strict = i_idx < t_idx
    incl = i_idx <= t_idx
    ident = jnp.where(i_idx == t_idx, 1.0, 0.0).astype(F32)
    lo_time = lax.broadcasted_iota(jnp.int32, (1, C2), 1) < C
    n_levels = max(1, (C - 1).bit_length())
    bf = lambda x: x.astype(BF16)

    def bd(x, lo_mask=lo_half):
        return bf(jnp.concatenate([jnp.where(lo_mask, x, 0.0), jnp.where(lo_mask, 0.0, x)], axis=0))

    chains = [(s, j) for s in range(n_seg) for j in range(N_PAIRS)]
    tile = lambda name, s, j: terms[name][s * C:(s + 1) * C, j * LANE:(j + 1) * LANE]
    r_sbs = {ch: tile("R", *ch) for ch in chains}
    stk = {ch: {nm: bd(tile(nm, *ch)) for nm in ("A", "B", "K", "V", "Be", "Ke")} for ch in chains}
    m_ab, aak, arb, ark = {}, {}, {}, {}
    for ch in chains:
        t = stk[ch]
        a_sbs, rb = bf(tile("A", *ch)), bf(r_sbs[ch])
        if C2 % LANE == 0:
            m1 = _dot_nt(jnp.concatenate([a_sbs, rb], axis=0), jnp.concatenate([t["B"], t["K"]], axis=0))
            ab, ak, rbm, rk = m1[:C, :C2], m1[:C, C2:], m1[C:, :C2], m1[C:, C2:]
        else:
            ab, ak = _dot_nt(a_sbs, t["B"]), _dot_nt(a_sbs, t["K"])
            rbm, rk = _dot_nt(rb, t["B"]), _dot_nt(rb, t["K"])
        m_ab[ch] = jnp.where(strict, ab, 0.0)
        aak[ch] = jnp.where(strict, ak, 0.0)
        arb[ch] = bf(jnp.where(incl, rbm, 0.0))
        ark[ch] = jnp.where(incl, rk, 0.0)

    nn = dict(m_ab)
    tinv = {ch: ident + m_ab[ch] for ch in chains}
    for lvl in range(1, n_levels):
        if lvl == 1:
            for ch in chains:
                nn[ch] = _dot(bf(nn[ch]), bd(nn[ch], lo_time))
            continue
        for ch in chains:
            both = _dot(bf(jnp.concatenate([nn[ch], tinv[ch]], axis=0)), bd(nn[ch], lo_time))
            nn[ch] = both[:C]
            tinv[ch] = tinv[ch] + both[C:]
    if n_levels > 1:
        for ch in chains:
            tinv[ch] = tinv[ch] + _dot(bf(tinv[ch]), bd(nn[ch], lo_time))
    av = {ch: _dot(bf(jnp.concatenate([aak[ch], ark[ch]], axis=0)), stk[ch]["V"]) for ch in chains}
    x = {ch: _dot(bf(tinv[ch]), jnp.concatenate([stk[ch]["A"], bd(av[ch][:C])], axis=1)) for ch in chains}
    a_eff = {ch: bd(x[ch][:, :LANE]) for ch in chains}
    v_eff = {ch: bd(x[ch][:, LANE:]) for ch in chains}
    zed = {ch: _dot(arb[ch], jnp.concatenate([a_eff[ch], v_eff[ch]], axis=1)) for ch in chains}
    rhat = {ch: bf(r_sbs[ch] + zed[ch][:, :LANE]) for ch in chains}
    y0 = {ch: av[ch][C:] + zed[ch][:, LANE:] for ch in chains}
    p_mat = {ch: bf(_dot_tn(stk[ch]["Be"], a_eff[ch])) for ch in chains}
    q_bd = {ch: _dot_tn(jnp.concatenate([v_eff[ch], stk[ch]["V"]], axis=0),
                        jnp.concatenate([stk[ch]["Be"], stk[ch]["Ke"]], axis=0)) for ch in chains}

    y_rows = []
    for b in range(n_bat):
        state = [s_scr[b * N_PAIRS + j] for j in range(N_PAIRS)]
        for sub in range(n_sub):
            s = b * n_sub + sub
            ys = []
            for j in range(N_PAIRS):
                ch = (s, j)
                hi = bf(state[j])
                lo = bf(state[j] - hi.astype(F32))
                yy = _dot_nt(rhat[ch], jnp.concatenate([bd(hi), bd(lo)], axis=0))
                ys.append(yy[:, :LANE] + yy[:, LANE:] + y0[ch])
                sp = _dot_nt(jnp.concatenate([hi, lo], axis=0), p_mat[ch])
                q_sbs = q_bd[ch][:HEAD_DIM] + q_bd[ch][HEAD_DIM:]
                state[j] = state[j] * w_chunk[s][:, j * LANE:(j + 1) * LANE] + sp[:HEAD_DIM] + sp[HEAD_DIM:] + q_sbs
            y_rows.append(jnp.concatenate(ys, axis=1))
        for j in range(N_PAIRS):
            s_scr[b * N_PAIRS + j] = state[j]

    y = jnp.concatenate(y_rows, axis=0) if n_seg > 1 else y_rows[0]
    inv_n = 1.0 / HEAD_DIM
    mean = headsum(y) * inv_n
    d = y - mean
    var = headsum(d * d) * inv_n
    yn = d * lax.rsqrt(var + LNX_EPS) * lng_ref[...] + lnb_ref[...]
    oa_ref[...] = ((yn + bonus) * g).astype(oa_ref.dtype)

    @pl.when(c == pl.num_programs(1) - 1)
    def _():
        for b in range(n_bat):
            for j in range(N_PAIRS):
                for h in range(PAIR):
                    sout_ref[b, PAIR * j + h] = s_scr[b * N_PAIRS + j][:, h * HEAD_DIM:(h + 1) * HEAD_DIM]


def _rwkv_call(pa2d, shift0, s0_pairs, prm, *, n_seq, n_chunks, chunk, n_sub, n_bat, t_real):
    n_steps = n_chunks // n_sub
    assert n_seq % n_bat == 0 and (n_bat == 1 or n_steps == 1)
    row = lambda b, c: (b * n_steps + c, 0)
    seq3 = lambda b, c: (b, 0, 0)
    fixed = lambda b, c: (0, 0)
    vec = lambda w: pl.BlockSpec((1, w), fixed)
    body = functools.partial(_rwkv_body, chunk=chunk, n_sub=n_sub, n_bat=n_bat, t_real=t_real)
    n_rows = n_seq * n_chunks * chunk
    blk_rows = n_bat * n_sub * chunk
    state_blk = (n_bat, A_HEADS, HEAD_DIM, HEAD_DIM)
    seq4 = lambda b, c: (b, 0, 0, 0)
    return pl.pallas_call(
        body,
        grid=(n_seq // n_bat, n_steps),
        in_specs=[pl.BlockSpec((blk_rows, SHIFT_W), row),
                  pl.BlockSpec((n_bat, 1, SHIFT_W), seq3),
                  pl.BlockSpec(state_blk, seq4),
                  vec(SHIFT_W), vec(A_WIDTH), vec(A_WIDTH),
                  pl.BlockSpec((LORA_W, 2 * A_WIDTH), fixed), pl.BlockSpec((GATE_LORA, A_WIDTH), fixed),
                  vec(A_WIDTH), vec(A_WIDTH), vec(A_WIDTH), vec(A_WIDTH), vec(A_WIDTH)],
        out_specs=[pl.BlockSpec((blk_rows, A_WIDTH), row), pl.BlockSpec(state_blk, seq4)],
        out_shape=[jax.ShapeDtypeStruct((n_rows, A_WIDTH), BF16),
                   jax.ShapeDtypeStruct((n_seq, A_HEADS, HEAD_DIM, HEAD_DIM), F32)],
        scratch_shapes=[pltpu.VMEM((n_bat * N_PAIRS, HEAD_DIM, LANE), F32), pltpu.VMEM((n_bat, 1, SHIFT_W), F32)],
        compiler_params=pltpu.CompilerParams(dimension_semantics=("arbitrary", "arbitrary"),
                                             vmem_limit_bytes=VMEM_LIMIT),
        name="rwkv",
    )(pa2d, shift0, s0_pairs, prm["mu"], prm["w0"], prm["a0"], prm["wlora"], prm["wgate"], prm["kk"], prm["ka"],
      prm["rk"], prm["lng"], prm["lnb"])


def _swa_bias(tq):
    rows = B_GROUP * tq
    grp = jnp.arange(rows, dtype=jnp.int32)[:, None] // tq
    t = jnp.arange(rows, dtype=jnp.int32)[:, None] % tq
    dist_p = t + WINDOW - jnp.arange(WINDOW, dtype=jnp.int32)[None, :]
    dist_c = t - jnp.arange(tq, dtype=jnp.int32)[None, :]

    def bias(dist, kv):
        slope = sum(jnp.where(grp == g, 2.0 ** -(kv * B_GROUP + g + 1), 0.0) for g in range(B_GROUP))
        return jnp.where((dist >= 0) & (dist < WINDOW), -slope * dist.astype(F32), MASK_NEG)

    return (jnp.stack([bias(dist_p, kv) for kv in range(B_KV_HEADS)]),
            jnp.stack([bias(dist_c, kv) for kv in range(B_KV_HEADS)]))


def _swa_body(sink_ref, q_ref, kp_ref, vp_ref, kc_ref, vc_ref, bp_ref, bc_ref, gn_ref, o_ref, *cache_refs, tq, n_bat,
              n_qb, t_real, first_has_prev):
    for s in range(n_bat * n_qb):
        rs = slice(s * tq, (s + 1) * tq)
        kc, vc = kc_ref[rs], vc_ref[rs]
        if n_qb > 1 and s > 0:
            ps = slice((s - 1) * tq, s * tq)
            kp, vp, has_prev = kc_ref[ps], vc_ref[ps], True
        else:
            kp, vp, has_prev = kp_ref[s], vp_ref[s], first_has_prev
        _swa_one(sink_ref, q_ref[rs], kp, vp, kc, vc, bp_ref, bc_ref, gn_ref, o_ref.at[rs], tq=tq,
                 first_has_prev=has_prev)
        if cache_refs:
            row = lax.broadcasted_iota(jnp.int32, (WINDOW, 1), 0)
            for out_ref, old, new in zip(cache_refs, (kp, vp), (kc, vc)):
                new_tail = jnp.concatenate([pltpu.roll(new, tq - t_real, axis=0)] * (WINDOW // tq), axis=0)
                out_ref[s] = jnp.where(row >= WINDOW - t_real, new_tail, pltpu.roll(old, WINDOW - t_real, axis=0))


def _swa_one(sink_ref, q, kp, vp, kc, vc, bp_ref, bc_ref, gn_ref, o_ref, *, tq, first_has_prev):
    nblk = pl.program_id(1)
    kp = kp.astype(BF16)
    kc = kc.astype(BF16)
    lane = lax.broadcasted_iota(jnp.int32, (1, LANE), 1)
    prev_pen = 0.0 if first_has_prev else jnp.where(nblk > 0, 0.0, MASK_NEG)

    tiles = [None] * B_GROUP
    for kv in range(B_KV_HEADS):
        mk = (lane < HEAD_DIM) if kv == 0 else (lane >= HEAD_DIM)
        q_st = jnp.concatenate([jnp.where(mk, q[:, g * LANE:(g + 1) * LANE], 0.0) for g in range(B_GROUP)],
                               axis=0).astype(BF16)
        s_p = _dot_nt(q_st, kp)
        s_c = _dot_nt(q_st, kc)
        vpm = jnp.where(mk, vp, 0.0).astype(BF16)
        vcm = jnp.where(mk, vc, 0.0).astype(BF16)
        for g in range(B_GROUP):
            rs = slice(g * tq, (g + 1) * tq)
            sink = sink_ref[kv * B_GROUP + g]
            sp = s_p[rs] + (bp_ref[kv, rs, :] + prev_pen)
            sc = s_c[rs] + bc_ref[kv, rs, :]
            if tq == WINDOW:
                m = jnp.maximum(jnp.max(jnp.maximum(sp, sc), axis=-1, keepdims=True), sink)
            else:
                m = jnp.maximum(jnp.maximum(jnp.max(sp, axis=-1, keepdims=True),
                                            jnp.max(sc, axis=-1, keepdims=True)), sink)
            e_p = jnp.exp(sp - m)
            e_c = jnp.exp(sc - m)
            if tq == WINDOW:
                e_sum = jnp.sum(e_p + e_c, axis=-1, keepdims=True)
            else:
                e_sum = jnp.sum(e_p, axis=-1, keepdims=True) + jnp.sum(e_c, axis=-1, keepdims=True)
            inv = 1.0 / (e_sum + jnp.exp(sink - m))
            o = (_dot(e_p.astype(BF16), vpm) + _dot(e_c.astype(BF16), vcm)) * inv
            tiles[g] = o if tiles[g] is None else tiles[g] + o

    ssq = sum(jnp.sum(x * x, axis=-1, keepdims=True) for x in tiles)
    inv_rms = lax.rsqrt(ssq * (1.0 / B_WIDTH) + RMS_EPS)
    o_ref[...] = (jnp.concatenate(tiles, axis=1) * inv_rms * gn_ref[...]).astype(o_ref.dtype)


def _swa_call(sink, q2d, kprev, vprev, kcur2d, vcur2d, gn, *, n_seq, n_blk, tq, n_bat, first_has_prev, prev_map,
              out_dtype, n_qb=1, cache_rows=0):
    assert n_seq % n_bat == 0 and (n_bat == 1 or n_blk * n_qb == 1) and (cache_rows == 0 or n_blk * n_qb == 1)
    row = lambda b, n: (b * n_blk + n, 0)
    body = functools.partial(_swa_body, tq=tq, n_bat=n_bat, n_qb=n_qb, t_real=cache_rows,
                             first_has_prev=first_has_prev)
    bias_p, bias_c = _swa_bias(tq)
    whole = lambda b, n: (0, 0, 0)
    blk_rows = n_bat * n_qb * tq
    out_specs = [pl.BlockSpec((blk_rows, B_WIDTH), row)]
    out_shape = [jax.ShapeDtypeStruct((n_seq * n_blk * n_qb * tq, B_WIDTH), out_dtype)]
    if cache_rows:
        out_specs += [pl.BlockSpec((n_bat, WINDOW, B_KV_WIDTH), prev_map)] * 2
        out_shape += [jax.ShapeDtypeStruct((n_seq, WINDOW, B_KV_WIDTH), F32)] * 2
    return pl.pallas_call(
        body,
        grid=(n_seq // n_bat, n_blk),
        in_specs=[pl.BlockSpec(memory_space=pltpu.SMEM),
                  pl.BlockSpec((blk_rows, B_WIDTH), row),
                  pl.BlockSpec((n_bat, WINDOW, B_KV_WIDTH), prev_map),
                  pl.BlockSpec((n_bat, WINDOW, B_KV_WIDTH), prev_map),
                  pl.BlockSpec((blk_rows, B_KV_WIDTH), row),
                  pl.BlockSpec((blk_rows, B_KV_WIDTH), row),
                  pl.BlockSpec(bias_p.shape, whole),
                  pl.BlockSpec(bias_c.shape, whole),
                  pl.BlockSpec((1, B_WIDTH), lambda b, n: (0, 0))],
        out_specs=out_specs,
        out_shape=out_shape,
        compiler_params=pltpu.CompilerParams(dimension_semantics=("arbitrary", "arbitrary"),
                                             vmem_limit_bytes=VMEM_LIMIT),
        name="swa",
    )(sink, q2d, kprev, vprev, kcur2d, vcur2d, bias_p, bias_c, gn)


def _route_rows(lg):
    lane = lax.broadcasted_iota(jnp.int32, (1, ROUTE_W), 1)
    lane_f = lane.astype(F32)
    no_lane = float(ROUTE_W)
    is_group = lane < N_GROUPS
    m_g = jnp.max(jnp.where(is_group, lg, MASK_NEG), axis=-1, keepdims=True)
    g_idx = jnp.min(jnp.where(is_group & (lg == m_g), lane_f, no_lane), axis=-1, keepdims=True)
    p_group = 1.0 / jnp.sum(jnp.where(is_group, jnp.exp(lg - m_g), 0.0), axis=-1, keepdims=True)
    e_lane = lane - N_GROUPS
    in_group = (e_lane >= 0) & (e_lane < N_EXPERTS) & ((e_lane >> 2).astype(F32) == g_idx)
    m_1 = jnp.max(jnp.where(in_group, lg, MASK_NEG), axis=-1, keepdims=True)
    i_1 = jnp.min(jnp.where(in_group & (lg == m_1), lane_f, no_lane), axis=-1, keepdims=True)
    rest = in_group & (lane_f != i_1)
    m_2 = jnp.max(jnp.where(rest, lg, MASK_NEG), axis=-1, keepdims=True)
    i_2 = jnp.min(jnp.where(rest & (lg == m_2), lane_f, no_lane), axis=-1, keepdims=True)
    ratio = jnp.exp(m_2 - m_1)
    w_1 = p_group / (1.0 + ratio)
    return jnp.where(lane == 0, i_1 - N_GROUPS,
                     jnp.where(lane == 1, i_2 - N_GROUPS,
                               jnp.where(lane == 2, w_1, jnp.where(lane == 3, w_1 * ratio, 0.0))))


def _outproj_body(*refs, seg_sizes):
    n_seg = len(seg_sizes)
    x_refs, oa_refs, ob_refs = refs[:n_seg], refs[n_seg:2 * n_seg], refs[2 * n_seg:3 * n_seg]
    wa_ref, wb_ref, g_ref, wr_ref, br_ref, x1_ref, h2_ref, rt_ref, rtt_ref = refs[3 * n_seg:]

    def run(x_ref, oa_ref, ob_ref):
        x1 = x_ref[...] + _dot(oa_ref[...].astype(BF16), wa_ref[...]) + _dot(ob_ref[...].astype(BF16), wb_ref[...])
        x1_ref[...] = x1
        h2 = x1 * lax.rsqrt(jnp.mean(x1 * x1, axis=-1, keepdims=True) + RMS_EPS) * g_ref[...]
        h2b = h2.astype(BF16)
        h2_ref[...] = h2b
        rt = _route_rows(_dot(h2b, wr_ref[...]) + br_ref[...])
        rt_ref[...] = rt
        rtt_ref[...] = rt.T[:ROUTE_ROWS]

    _run_segment(pl.program_id(0), seg_sizes, run, list(zip(x_refs, oa_refs, ob_refs)))


def _outproj_call(x_list, x_firsts, oa_list, ob_list, seg_sizes, wa, wb, g, wr, br, *, tm):
    d = x_list[0].shape[1]
    n_tiles = sum(seg_sizes)
    n = n_tiles * tm
    row = lambda i: (i, 0)
    fixed = lambda i: (0, 0)
    zeros = [0] * len(seg_sizes)
    return pl.pallas_call(
        functools.partial(_outproj_body, seg_sizes=tuple(seg_sizes)),
        grid=(n_tiles,),
        in_specs=(_seg_specs(tm, d, seg_sizes, x_firsts) + _seg_specs(tm, A_WIDTH, seg_sizes, zeros)
                  + _seg_specs(tm, B_WIDTH, seg_sizes, zeros)
                  + [pl.BlockSpec((A_WIDTH, d), fixed), pl.BlockSpec((B_WIDTH, d), fixed), pl.BlockSpec((1, d), fixed),
                     pl.BlockSpec((d, ROUTE_W), fixed), pl.BlockSpec((1, ROUTE_W), fixed)]),
        out_specs=[pl.BlockSpec((tm, d), row), pl.BlockSpec((tm, d), row), pl.BlockSpec((tm, ROUTE_W), row),
                   pl.BlockSpec((ROUTE_ROWS, tm), row)],
        out_shape=[jax.ShapeDtypeStruct((n, d), F32), jax.ShapeDtypeStruct((n, d), BF16),
                   jax.ShapeDtypeStruct((n, ROUTE_W), F32),
                   jax.ShapeDtypeStruct((n_tiles * ROUTE_ROWS, tm), F32)],
        compiler_params=pltpu.CompilerParams(dimension_semantics=("arbitrary",), vmem_limit_bytes=VMEM_LIMIT),
        name="outproj",
    )(*x_list, *oa_list, *ob_list, wa, wb, g, wr, br)


def _expert_body(vb_ref, ve_ref, lo_ref, hi_ref, x_ref, wg_ref, wu_ref, wd_ref, y_ref, wg_bf, wu_bf, wd_bf):
    v = pl.program_id(0)
    pv = jnp.maximum(v - 1, 0)
    lo, hi = lo_ref[v], hi_ref[v]
    first_visit = (v == 0) | (vb_ref[v] != vb_ref[pv])

    @pl.when((v == 0) | (ve_ref[v] != ve_ref[pv]))
    def _():
        wg_bf[...] = wg_ref[0].astype(BF16)
        wu_bf[...] = wu_ref[0].astype(BF16)
        wd_bf[...] = wd_ref[0].astype(BF16)

    @pl.when(hi > lo)
    def _():
        x = x_ref[...]
        gate = _dot(x, wg_bf[...])
        up = _dot(x, wu_bf[...])
        mid = (gate * _sigmoid(gate) * up).astype(BF16)
        y = _dot(mid, wd_bf[...])
        row = lax.broadcasted_iota(jnp.int32, (y.shape[0], 1), 0)
        mine = (row >= lo) & (row < hi)

        @pl.when(first_visit)
        def _():
            y_ref[...] = jnp.where(mine, y, 0.0)

        @pl.when(jnp.logical_not(first_visit))
        def _():
            y_ref[...] = jnp.where(mine, y, y_ref[...])


def _expert_call(visits, xs, wg, wu, wd, *, tb):
    n_rows, d = xs.shape
    ff = wg.shape[-1]
    vb, ve, lo, hi = visits
    blk = lambda v, vb, ve, lo, hi: (vb[v], 0)
    wsel = lambda v, vb, ve, lo, hi: (ve[v], 0, 0)
    grid_spec = pltpu.PrefetchScalarGridSpec(
        num_scalar_prefetch=4,
        grid=(vb.shape[0],),
        in_specs=[pl.BlockSpec((tb, d), blk), pl.BlockSpec((1, d, ff), wsel), pl.BlockSpec((1, d, ff), wsel),
                  pl.BlockSpec((1, ff, d), wsel)],
        out_specs=pl.BlockSpec((tb, d), blk),
        scratch_shapes=[pltpu.VMEM((d, ff), BF16), pltpu.VMEM((d, ff), BF16), pltpu.VMEM((ff, d), BF16)],
    )
    return pl.pallas_call(
        _expert_body,
        grid_spec=grid_spec,
        out_shape=jax.ShapeDtypeStruct((n_rows, d), F32),
        compiler_params=pltpu.CompilerParams(dimension_semantics=("arbitrary",), vmem_limit_bytes=VMEM_LIMIT),
        name="experts",
    )(vb, ve, lo, hi, xs, wg, wu, wd)


def _final_body(*refs, seg_sizes):
    n_seg = len(seg_sizes)
    g_ref, o_ref = refs[4 * n_seg:]

    def run(x1_ref, y0_ref, y1_ref, rt_ref):
        rt = rt_ref[...]
        x = x1_ref[...] + (rt[:, 2:3] * y0_ref[...] + rt[:, 3:4] * y1_ref[...])
        o_ref[...] = x * lax.rsqrt(jnp.mean(x * x, axis=-1, keepdims=True) + RMS_EPS) * g_ref[...]

    _run_segment(pl.program_id(0), seg_sizes, run, [refs[4 * s:4 * s + 4] for s in range(n_seg)])


def _final_call(sources, firsts, seg_sizes, g, *, tm):
    d = sources[0][0].shape[1]
    n_tiles = sum(seg_sizes)
    in_specs, start = [], 0
    for size, first in zip(seg_sizes, firsts):
        src = lambda i, s=start, f=first, n=size: (f + jnp.clip(i - s, 0, n - 1), 0)
        in_specs += [pl.BlockSpec((tm, d), src)] * 3 + [pl.BlockSpec((tm, ROUTE_W), src)]
        start += size
    return pl.pallas_call(
        functools.partial(_final_body, seg_sizes=tuple(seg_sizes)),
        grid=(n_tiles,),
        in_specs=in_specs + [pl.BlockSpec((1, d), lambda i: (0, 0))],
        out_specs=pl.BlockSpec((tm, d), lambda i: (i, 0)),
        out_shape=jax.ShapeDtypeStruct((n_tiles * tm, d), F32),
        compiler_params=pltpu.CompilerParams(dimension_semantics=("arbitrary",), vmem_limit_bytes=VMEM_LIMIT),
        name="final",
    )(*[a for src in sources for a in src], g)


def _dispatch(flat_e, n, tb):
    n_assign = n * TOP_K_INNER
    idx_bits = (n_assign - 1).bit_length()
    assert N_EXPERTS << idx_bits < 2 ** 31 and n_assign % tb == 0
    ids = jnp.arange(n_assign, dtype=jnp.int32)
    sorted_key = jnp.sort((flat_e << idx_bits) | ids)
    sorted_id = sorted_key & ((1 << idx_bits) - 1)
    src_tok = sorted_id - n * (sorted_id // n)

    experts = jnp.arange(N_EXPERTS, dtype=jnp.int32)
    onehot = (flat_e[:, None] == experts[None, :]).astype(jnp.int32)
    csum = jnp.cumsum(onehot, axis=0)
    counts = csum[-1]
    ends = jnp.cumsum(counts)
    starts = ends - counts
    dest = (jnp.sum((csum - 1 + starts[None, :]) * onehot, axis=1)).astype(jnp.int32)

    n_blocks = n_assign // tb
    n_visits = n_blocks + N_EXPERTS - 1
    first_blk = starts // tb
    n_vis_e = jnp.where(counts > 0, (ends + tb - 1) // tb - first_blk, 0)
    v_end = jnp.cumsum(n_vis_e)
    v = jnp.arange(n_visits, dtype=jnp.int32)
    valid = v < v_end[-1]
    last_e = jnp.max(jnp.where(counts > 0, experts, 0))
    ve = jnp.where(valid, jnp.sum((v_end[None, :] <= v[:, None]).astype(jnp.int32), axis=1), last_e)
    pick = lambda a: jnp.sum(jnp.where(ve[:, None] == experts[None, :], a[None, :], 0), axis=1)
    vb = pick(first_blk) + v - pick(v_end - n_vis_e)
    lo = jnp.maximum(pick(starts), vb * tb) - vb * tb
    hi = jnp.minimum(pick(ends), (vb + 1) * tb) - vb * tb
    vb = jnp.where(valid, vb, n_blocks - 1)
    lo = jnp.where(valid, lo, 0)
    hi = jnp.where(valid, hi, 0)
    return dest, src_tok, (vb.astype(jnp.int32), ve.astype(jnp.int32), lo.astype(jnp.int32), hi.astype(jnp.int32))


def _q_perm():
    cols = []
    for g in range(B_GROUP):
        for kv in range(B_KV_HEADS):
            h = kv * B_GROUP + g
            cols.extend(range(h * HEAD_DIM, (h + 1) * HEAD_DIM))
    return jnp.array(cols, dtype=jnp.int32)


def _layer(x_prompt, x_sample, state_rwkv, state_shift, cache_win_k, cache_win_v, prm, norm_final_g, *,
           chunk, n_sub, moe_block):
    bp, tp, d = x_prompt.shape
    bs, ts, _ = x_sample.shape
    n_p, n_s = bp * tp, bs * ts
    ts_pad = 8
    tm = n_s
    assert tp % tm == 0 and tm % 8 == 0 and cache_win_k.shape[1] == WINDOW
    xp = x_prompt.reshape(n_p, d)
    xs = x_sample.reshape(n_s, d)

    qp = _q_perm()
    w_in = prm["w_in"]
    w_in = jnp.concatenate([w_in[:, :SHIFT_W], w_in[:, SHIFT_W:SHIFT_W + B_WIDTH][:, qp],
                            w_in[:, SHIFT_W + B_WIDTH:]], axis=1).astype(BF16)
    gn = prm["attn_norm_g"][qp][None]
    w_out = prm["w_out"]
    wa, wb = w_out[:A_WIDTH].astype(BF16), w_out[A_WIDTH:][qp].astype(BF16)
    pad_lanes = jnp.zeros((d, ROUTE_W - N_GROUPS - N_EXPERTS), F32)
    wr = jnp.concatenate([prm["w_route_group"], prm["w_route_expert"], pad_lanes], axis=1).astype(BF16)
    br = jnp.concatenate([prm["b_route_group"], prm["b_route_expert"], pad_lanes[0]])[None]

    zero_blk = jnp.zeros((LORA_W // 2, A_WIDTH), F32)
    wlora = jnp.concatenate([jnp.concatenate([prm["w_decay_up"], zero_blk], axis=1),
                             jnp.concatenate([zero_blk, prm["w_iclr_up"]], axis=1)], axis=0).astype(BF16)
    rp = dict(mu=prm["mu_shift"][None], w0=prm["w_decay0"][None], a0=prm["w_iclr0"][None], wlora=wlora,
              wgate=prm["w_gate_up"].astype(BF16), kk=prm["k_k"][None], ka=prm["k_a"][None],
              rk=prm["r_k"].reshape(1, A_WIDTH), lng=prm["lnx_g"][None], lnb=prm["lnx_b"][None])

    groups = [(0, bp)]
    tiles_per_seq = tp // tm
    nb = tp // WINDOW
    n_qb = _row_tile(nb, 4)
    kv4 = lambda a: a.reshape(a.shape[0], a.shape[1], B_KV_HEADS, HEAD_DIM)
    finals, s_p, sh_p, k_p, v_p = [], [], [], [], []
    for gi, (s0, s1) in enumerate(groups):
        has_sample = gi == len(groups) - 1
        n_seq = s1 - s0
        n_pt = n_seq * tiles_per_seq
        n_pg = n_pt * tm
        seg_sizes = [n_pt] + ([1] if has_sample else [])
        pa, q, k, v = _inproj_call([xp] + ([xs] if has_sample else []), [s0 * tiles_per_seq, 0], seg_sizes,
                                   prm["norm_mix_g"][None], w_in, tm=tm)

        oa_p, s_pg = _rwkv_call(pa, jnp.zeros((n_seq, 1, SHIFT_W), F32),
                                jnp.zeros((n_seq,) + state_rwkv.shape[1:], F32), rp, n_seq=n_seq,
                                n_chunks=tp // chunk, chunk=chunk, n_sub=n_sub, n_bat=1, t_real=chunk)
        ob_p, = _swa_call(prm["attn_sink"], q, k.reshape(-1, WINDOW, B_KV_WIDTH), v.reshape(-1, WINDOW, B_KV_WIDTH),
                          k, v, gn, n_seq=n_seq, n_blk=nb // n_qb, tq=WINDOW, n_bat=1, n_qb=n_qb,
                          first_has_prev=False, out_dtype=BF16,
                          prev_map=lambda b, i: (b * nb + jnp.maximum(i * n_qb - 1, 0), 0, 0))
        oa_list, ob_list = [oa_p], [ob_p]
        if has_sample:
            pad_rows = lambda a: jnp.pad(a[n_pg:].reshape(bs, ts, -1), ((0, 0), (0, ts_pad - ts), (0, 0))).reshape(
                bs * ts_pad, -1)
            oa_s, s_s = _rwkv_call(pad_rows(pa), state_shift[:, None, :], state_rwkv, rp, n_seq=bs,
                                   n_chunks=1, chunk=ts_pad, n_sub=1, n_bat=_row_tile(bs, 8), t_real=ts)
            ob_s, k_s, v_s = _swa_call(prm["attn_sink"], pad_rows(q).astype(F32),
                                       cache_win_k.reshape(bs, WINDOW, B_KV_WIDTH),
                                       cache_win_v.reshape(bs, WINDOW, B_KV_WIDTH), pad_rows(k), pad_rows(v), gn,
                                       n_seq=bs, n_blk=1, tq=ts_pad, n_bat=_row_tile(bs, 16), first_has_prev=True,
                                       prev_map=lambda b, i: (b, 0, 0), out_dtype=F32, cache_rows=ts)
            oa_list.append(oa_s.reshape(bs, ts_pad, A_WIDTH)[:, :ts].reshape(n_s, A_WIDTH))
            ob_list.append(ob_s.reshape(bs, ts_pad, B_WIDTH)[:, :ts].reshape(n_s, B_WIDTH))
            sh_s = pa[n_pg:].reshape(bs, ts, SHIFT_W)[:, -1]

        x1, h2, route, route_t = _outproj_call([xp] + ([xs] if has_sample else []), [s0 * tiles_per_seq, 0],
                                               oa_list, ob_list, seg_sizes, wa, wb, prm["norm_ffn_g"][None], wr, br,
                                               tm=tm)
        n = sum(seg_sizes) * tm
        route_t = route_t.reshape(n // tm, ROUTE_ROWS, tm)
        flat_e = jnp.concatenate([route_t[:, j, :].reshape(n) for j in range(TOP_K_INNER)]).astype(jnp.int32)
        dest, src_tok, visits = _dispatch(flat_e, n, moe_block)
        ybuf = _expert_call(visits, h2[src_tok], prm["w_exp_gate"], prm["w_exp_up"], prm["w_exp_down"], tb=moe_block)
        finals.append((x1, ybuf[dest[:n]], ybuf[dest[n:]], route))

        last_rows = lambda a, m: jnp.stack([a[(b + 1) * tp - m:(b + 1) * tp] for b in range(n_seq)], axis=0)
        s_p.append(s_pg)
        sh_p.append(last_rows(pa, 1)[:, 0])
        k_p.append(last_rows(k, WINDOW))
        v_p.append(last_rows(v, WINDOW))

    gf = norm_final_g[None]
    n_pts = [(s1 - s0) * tiles_per_seq for s0, s1 in groups]
    y_p = _final_call(finals, [0] * len(groups), n_pts, gf, tm=tm)
    y_s = _final_call(finals[-1:], n_pts[-1:], [1], gf, tm=tm)
    cat = lambda parts: jnp.concatenate(parts, axis=0)
    return (y_p.reshape(bp, tp, d), y_s.reshape(bs, ts, d), cat(s_p), cat(sh_p), kv4(cat(k_p)), kv4(cat(v_p)),
            s_s, sh_s, kv4(k_s), kv4(v_s))


def kernel(x_prompt, x_sample, state_rwkv, state_shift, cache_win_k, cache_win_v, norm_mix_g, w_in, mu_shift, w_decay0, w_decay_up, w_iclr0, w_iclr_up, w_gate_up, k_k, k_a, r_k, lnx_g, lnx_b, attn_sink, attn_norm_g, w_out, norm_ffn_g, w_route_group, b_route_group, w_route_expert, b_route_expert, w_exp_gate, w_exp_up, w_exp_down, norm_final_g):
    assert norm_mix_g.shape[0] == 1, "single-layer trunk"
    prm = dict(norm_mix_g=norm_mix_g[0], w_in=w_in[0], mu_shift=mu_shift[0], w_decay0=w_decay0[0],
               w_decay_up=w_decay_up[0], w_iclr0=w_iclr0[0], w_iclr_up=w_iclr_up[0], w_gate_up=w_gate_up[0],
               k_k=k_k[0], k_a=k_a[0], r_k=r_k[0], lnx_g=lnx_g[0], lnx_b=lnx_b[0], attn_sink=attn_sink[0],
               attn_norm_g=attn_norm_g[0], w_out=w_out[0], norm_ffn_g=norm_ffn_g[0],
               w_route_group=w_route_group[0], b_route_group=b_route_group[0],
               w_route_expert=w_route_expert[0], b_route_expert=b_route_expert[0],
               w_exp_gate=w_exp_gate[0], w_exp_up=w_exp_up[0], w_exp_down=w_exp_down[0])
    outs = _layer(x_prompt, x_sample, state_rwkv[0], state_shift[0], cache_win_k[0], cache_win_v[0], prm,
                  norm_final_g, chunk=64, n_sub=4, moe_block=512)
    y_p, y_s, s_p, sh_p, kp, vp, s_s, sh_s, ks, vs = outs
    return (y_p, y_s, s_p[None], sh_p[None], kp[None], vp[None], s_s[None], sh_s[None], ks[None], vs[None])
```

```python
import functools

import jax
import jax.numpy as jnp
from jax import lax
from jax.experimental import pallas as pl
from jax.experimental.pallas import tpu as pltpu

F32 = jnp.float32
BF16 = jnp.bfloat16

HEAD_DIM = 64
A_HEADS = 8
A_WIDTH = A_HEADS * HEAD_DIM
B_HEADS = 8
B_KV_HEADS = 2
B_GROUP = B_HEADS // B_KV_HEADS
B_WIDTH = B_HEADS * HEAD_DIM
B_KV_WIDTH = B_KV_HEADS * HEAD_DIM
DECAY_LORA = 64
ICLR_LORA = 64
GATE_LORA = 128
LORA_W = DECAY_LORA + ICLR_LORA
SHIFT_W = 3 * A_WIDTH + LORA_W + GATE_LORA
IN_W = SHIFT_W + B_WIDTH + 2 * B_KV_WIDTH
WINDOW = 128
N_GROUPS = 4
EXPERTS_PER_GROUP = 4
N_EXPERTS = N_GROUPS * EXPERTS_PER_GROUP
TOP_K_INNER = 2
EXPERT_FF = 512
RMS_EPS = 1e-6
LNX_EPS = 64e-5
DECAY_OFFSET = 0.5

LANE = 128
PAIR = LANE // HEAD_DIM
N_PAIRS = A_HEADS // PAIR
ROUTE_W = LANE
ROUTE_ROWS = 8
MASK_NEG = -1e30
VMEM_LIMIT = 48 * 1024 * 1024

LOG2_E = 1.4426950408889634
HEAD_SHIFT = HEAD_DIM.bit_length() - 1
HI = lax.Precision.HIGHEST


def _row_tile(n, cap):
    t = cap
    while n % t:
        t //= 2
    return t


def _dot(a, b, precision=None):
    return jnp.dot(a, b, preferred_element_type=F32, precision=precision)


def _dot_nt(a, b, precision=None):
    return lax.dot_general(a, b, (((1,), (1,)), ((), ())), preferred_element_type=F32, precision=precision)


def _dot_tn(a, b, precision=None):
    return lax.dot_general(a, b, (((0,), (0,)), ((), ())), preferred_element_type=F32, precision=precision)


def _sigmoid(x):
    return 1.0 / (1.0 + jnp.exp(-x))


def _hi_lo(x, axis):
    hi = x.astype(BF16)
    lo = (x - hi.astype(F32)).astype(BF16)
    return jnp.concatenate([hi, lo], axis=axis)


def _run_segment(i, seg_sizes, run, seg_refs):
    start = 0
    for size, refs in zip(seg_sizes, seg_refs):
        pl.when((i >= start) & (i < start + size))(functools.partial(run, *refs))
        start += size


def _seg_specs(tm, width, seg_sizes, firsts):
    specs, start = [], 0
    for size, first in zip(seg_sizes, firsts):
        specs.append(pl.BlockSpec((tm, width),
                                  lambda i, s=start, f=first, n=size: (f + jnp.clip(i - s, 0, n - 1), 0)))
        start += size
    return specs


def _inproj_body(*refs, seg_sizes):
    n_seg = len(seg_sizes)
    g_ref, w_ref, pa_ref, q_ref, k_ref, v_ref = refs[n_seg:]

    def run(x_ref):
        x = x_ref[...]
        h = x * lax.rsqrt(jnp.mean(x * x, axis=-1, keepdims=True) + RMS_EPS) * g_ref[...]
        p = _dot(h.astype(BF16), w_ref[...])
        pa_ref[...] = p[:, :SHIFT_W]
        q_ref[...] = (p[:, SHIFT_W:SHIFT_W + B_WIDTH] * (HEAD_DIM ** -0.5)).astype(BF16)
        k_ref[...] = p[:, SHIFT_W + B_WIDTH:SHIFT_W + B_WIDTH + B_KV_WIDTH]
        v_ref[...] = p[:, SHIFT_W + B_WIDTH + B_KV_WIDTH:]

    _run_segment(pl.program_id(0), seg_sizes, run, [(r,) for r in refs[:n_seg]])


def _inproj_call(xs_list, firsts, seg_sizes, g, w_bf16, *, tm):
    d = xs_list[0].shape[1]
    n_tiles = sum(seg_sizes)
    n = n_tiles * tm
    row = lambda i: (i, 0)
    fixed = lambda i: (0, 0)
    return pl.pallas_call(
        functools.partial(_inproj_body, seg_sizes=tuple(seg_sizes)),
        grid=(n_tiles,),
        in_specs=_seg_specs(tm, d, seg_sizes, firsts) + [pl.BlockSpec((1, d), fixed), pl.BlockSpec((d, IN_W), fixed)],
        out_specs=[pl.BlockSpec((tm, SHIFT_W), row), pl.BlockSpec((tm, B_WIDTH), row),
                   pl.BlockSpec((tm, B_KV_WIDTH), row), pl.BlockSpec((tm, B_KV_WIDTH), row)],
        out_shape=[jax.ShapeDtypeStruct((n, SHIFT_W), F32), jax.ShapeDtypeStruct((n, B_WIDTH), BF16),
                   jax.ShapeDtypeStruct((n, B_KV_WIDTH), F32), jax.ShapeDtypeStruct((n, B_KV_WIDTH), F32)],
        compiler_params=pltpu.CompilerParams(dimension_semantics=("arbitrary",), vmem_limit_bytes=VMEM_LIMIT),
        name="inproj",
    )(*xs_list, g, w_bf16)


def _rwkv_body(pa_ref, shift0_ref, s0_ref, mu_ref, w0_ref, a0_ref, wlora_ref, wgate_ref, kk_ref, ka_ref,
               rk_ref, lng_ref, lnb_ref, oa_ref, sout_ref, s_scr, prev_scr, *, chunk, n_sub, n_bat, t_real):
    C = chunk
    seq_rows = n_sub * C
    n_seg = n_bat * n_sub
    rows = n_seg * C
    c = pl.program_id(1)

    @pl.when(c == 0)
    def _():
        prev_scr[...] = shift0_ref[...]
        for b in range(n_bat):
            for j in range(N_PAIRS):
                s_scr[b * N_PAIRS + j] = jnp.concatenate([s0_ref[b, PAIR * j + h] for h in range(PAIR)], axis=1)

    pa = pa_ref[...]
    row = lax.broadcasted_iota(jnp.int32, (rows, 1), 0)
    row_in_chunk = row & (C - 1)
    pa_prev = pltpu.roll(pa, 1, axis=0)
    for b in range(n_bat):
        pa_prev = jnp.where(row == b * seq_rows, prev_scr[b], pa_prev)
        last = (b + 1) * seq_rows - C + t_real - 1
        prev_scr[b] = pa[last:last + 1]
    xm = pa + mu_ref[...] * (pa_prev - pa)

    r = xm[:, :A_WIDTH]
    k = xm[:, A_WIDTH:2 * A_WIDTH]
    v = xm[:, 2 * A_WIDTH:3 * A_WIDTH]
    lora_in = xm[:, 3 * A_WIDTH:3 * A_WIDTH + LORA_W]
    gd = xm[:, 3 * A_WIDTH + LORA_W:]

    lane = lax.broadcasted_iota(jnp.int32, (1, LANE), 1)
    lo_half = lane < HEAD_DIM
    z = jnp.where(lo_half, jnp.tanh(lora_in), lora_in)
    lw = _dot(z.astype(BF16), wlora_ref[...])
    dec_pre = w0_ref[...] + lw[:, :A_WIDTH]
    a = _sigmoid(a0_ref[...] + lw[:, A_WIDTH:])
    neg = -dec_pre
    softplus = jnp.maximum(neg, 0.0) + jnp.log(1.0 + jnp.exp(-jnp.abs(neg)))
    logdec = -LOG2_E * jnp.exp(-softplus - DECAY_OFFSET)
    g = _dot(_sigmoid(gd).astype(BF16), wgate_ref[...])

    ri = lax.broadcasted_iota(jnp.int32, (LANE, LANE), 0)
    ci = lax.broadcasted_iota(jnp.int32, (LANE, LANE), 1)
    seg = jnp.where((ri >> HEAD_SHIFT) == (ci >> HEAD_SHIFT), 1.0, 0.0).astype(BF16)
    seg2 = jnp.concatenate([seg, seg], axis=0)

    def headsum(x):
        return jnp.concatenate([_dot(_hi_lo(x[:, j * LANE:(j + 1) * LANE], 1), seg2) for j in range(N_PAIRS)],
                               axis=1)

    kk = k * kk_ref[...]
    kk = kk * lax.rsqrt(jnp.maximum(headsum(kk * kk), 1e-24))
    k = k * (1.0 + (a - 1.0) * ka_ref[...])
    bonus = headsum(r * k * rk_ref[...]) * v

    if t_real < C:
        valid = row_in_chunk < t_real
        logdec = jnp.where(valid, logdec, 0.0)
        kk = jnp.where(valid, kk, 0.0)
        k = jnp.where(valid, k, 0.0)
        v = jnp.where(valid, v, 0.0)

    cum = logdec
    shift = 1
    while shift < C:
        cum = cum + jnp.where(row_in_chunk >= shift, pltpu.roll(cum, shift, axis=0), 0.0)
        shift *= 2
    ends = [cum[(s + 1) * C - 1:(s + 1) * C] for s in range(n_seg)]
    cum_end = jnp.concatenate([jnp.broadcast_to(e, (C, A_WIDTH)) for e in ends], axis=0) if n_seg > 1 else ends[0]
    w_incl = jnp.exp2(cum)
    w_prev = jnp.exp2(cum - logdec)
    w_inv = jnp.exp2(-cum)
    w_end = jnp.exp2(cum_end - cum)
    w_chunk = [jnp.exp2(e) for e in ends]
    kka = kk * a
    terms = dict(A=-kk * w_prev, R=r * w_incl, B=kka * w_inv, K=k * w_inv, V=v, Be=kka * w_end, Ke=k * w_end)

    C2 = PAIR * C
    t_idx = lax.broadcasted_iota(jnp.int32, (C, 1), 0)
    i_idx = lax.broadcasted_iota(jnp.int32, (1, C2), 1) & (C - 1)
    strict = i_idx < t_idx
    incl = i_idx <= t_idx
    ident = jnp.where(i_idx == t_idx, 1.0, 0.0).astype(F32)
    lo_time = lax.broadcasted_iota(jnp.int32, (1, C2), 1) < C
    n_levels = max(1, (C - 1).bit_length())
    bf = lambda x: x.astype(BF16)

    def bd(x, lo_mask=lo_half):
        return bf(jnp.concatenate([jnp.where(lo_mask, x, 0.0), jnp.where(lo_mask, 0.0, x)], axis=0))

    chains = [(s, j) for s in range(n_seg) for j in range(N_PAIRS)]
    tile = lambda name, s, j: terms[name][s * C:(s + 1) * C, j * LANE:(j + 1) * LANE]
    r_sbs = {ch: tile("R", *ch) for ch in chains}
    stk = {ch: {nm: bd(tile(nm, *ch)) for nm in ("A", "B", "K", "V", "Be", "Ke")} for ch in chains}
    m_ab, aak, arb, ark = {}, {}, {}, {}
    for ch in chains:
        t = stk[ch]
        a_sbs, rb = bf(tile("A", *ch)), bf(r_sbs[ch])
        if C2 % LANE == 0:
            m1 = _dot_nt(jnp.concatenate([a_sbs, rb], axis=0), jnp.concatenate([t["B"], t["K"]], axis=0))
            ab, ak, rbm, rk = m1[:C, :C2], m1[:C, C2:], m1[C:, :C2], m1[C:, C2:]
        else:
            ab, ak = _dot_nt(a_sbs, t["B"]), _dot_nt(a_sbs, t["K"])
            rbm, rk = _dot_nt(rb, t["B"]), _dot_nt(rb, t["K"])
        m_ab[ch] = jnp.where(strict, ab, 0.0)
        aak[ch] = jnp.where(strict, ak, 0.0)
        arb[ch] = bf(jnp.where(incl, rbm, 0.0))
        ark[ch] = jnp.where(incl, rk, 0.0)

    nn = dict(m_ab)
    tinv = {ch: ident + m_ab[ch] for ch in chains}
    for lvl in range(1, n_levels):
        if lvl == 1:
            for ch in chains:
                nn[ch] = _dot(bf(nn[ch]), bd(nn[ch], lo_time))
            continue
        for ch in chains:
            both = _dot(bf(jnp.concatenate([nn[ch], tinv[ch]], axis=0)), bd(nn[ch], lo_time))
            nn[ch] = both[:C]
            tinv[ch] = tinv[ch] + both[C:]
    if n_levels > 1:
        for ch in chains:
            tinv[ch] = tinv[ch] + _dot(bf(tinv[ch]), bd(nn[ch], lo_time))
    av = {ch: _dot(bf(jnp.concatenate([aak[ch], ark[ch]], axis=0)), stk[ch]["V"]) for ch in chains}
    x = {ch: _dot(bf(tinv[ch]), jnp.concatenate([stk[ch]["A"], bd(av[ch][:C])], axis=1)) for ch in chains}
    a_eff = {ch: bd(x[ch][:, :LANE]) for ch in chains}
    v_eff = {ch: bd(x[ch][:, LANE:]) for ch in chains}
    zed = {ch: _dot(arb[ch], jnp.concatenate([a_eff[ch], v_eff[ch]], axis=1)) for ch in chains}
    rhat = {ch: bf(r_sbs[ch] + zed[ch][:, :LANE]) for ch in chains}
    y0 = {ch: av[ch][C:] + zed[ch][:, LANE:] for ch in chains}
    p_mat = {ch: bf(_dot_tn(stk[ch]["Be"], a_eff[ch])) for ch in chains}
    q_bd = {ch: _dot_tn(jnp.concatenate([v_eff[ch], stk[ch]["V"]], axis=0),
                        jnp.concatenate([stk[ch]["Be"], stk[ch]["Ke"]], axis=0)) for ch in chains}

    y_rows = []
    for b in range(n_bat):
        state = [s_scr[b * N_PAIRS + j] for j in range(N_PAIRS)]
        for sub in range(n_sub):
            s = b * n_sub + sub
            ys = []
            for j in range(N_PAIRS):
                ch = (s, j)
                hi = bf(state[j])
                lo = bf(state[j] - hi.astype(F32))
                yy = _dot_nt(rhat[ch], jnp.concatenate([bd(hi), bd(lo)], axis=0))
                ys.append(yy[:, :LANE] + yy[:, LANE:] + y0[ch])
                sp = _dot_nt(jnp.concatenate([hi, lo], axis=0), p_mat[ch])
                q_sbs = q_bd[ch][:HEAD_DIM] + q_bd[ch][HEAD_DIM:]
                state[j] = state[j] * w_chunk[s][:, j * LANE:(j + 1) * LANE] + sp[:HEAD_DIM] + sp[HEAD_DIM:] + q_sbs
            y_rows.append(jnp.concatenate(ys, axis=1))
        for j in range(N_PAIRS):
            s_scr[b * N_PAIRS + j] = state[j]

    y = jnp.concatenate(y_rows, axis=0) if n_seg > 1 else y_rows[0]
    inv_n = 1.0 / HEAD_DIM
    mean = headsum(y) * inv_n
    d = y - mean
    var = headsum(d * d) * inv_n
    yn = d * lax.rsqrt(var + LNX_EPS) * lng_ref[...] + lnb_ref[...]
    oa_ref[...] = ((yn + bonus) * g).astype(oa_ref.dtype)

    @pl.when(c == pl.num_programs(1) - 1)
    def _():
        for b in range(n_bat):
            for j in range(N_PAIRS):
                for h in range(PAIR):
                    sout_ref[b, PAIR * j + h] = s_scr[b * N_PAIRS + j][:, h * HEAD_DIM:(h + 1) * HEAD_DIM]


def _rwkv_call(pa2d, shift0, s0_pairs, prm, *, n_seq, n_chunks, chunk, n_sub, n_bat, t_real):
    n_steps = n_chunks // n_sub
    assert n_seq % n_bat == 0 and (n_bat == 1 or n_steps == 1)
    row = lambda b, c: (b * n_steps + c, 0)
    seq3 = lambda b, c: (b, 0, 0)
    fixed = lambda b, c: (0, 0)
    vec = lambda w: pl.BlockSpec((1, w), fixed)
    body = functools.partial(_rwkv_body, chunk=chunk, n_sub=n_sub, n_bat=n_bat, t_real=t_real)
    n_rows = n_seq * n_chunks * chunk
    blk_rows = n_bat * n_sub * chunk
    state_blk = (n_bat, A_HEADS, HEAD_DIM, HEAD_DIM)
    seq4 = lambda b, c: (b, 0, 0, 0)
    return pl.pallas_call(
        body,
        grid=(n_seq // n_bat, n_steps),
        in_specs=[pl.BlockSpec((blk_rows, SHIFT_W), row),
                  pl.BlockSpec((n_bat, 1, SHIFT_W), seq3),
                  pl.BlockSpec(state_blk, seq4),
                  vec(SHIFT_W), vec(A_WIDTH), vec(A_WIDTH),
                  pl.BlockSpec((LORA_W, 2 * A_WIDTH), fixed), pl.BlockSpec((GATE_LORA, A_WIDTH), fixed),
                  vec(A_WIDTH), vec(A_WIDTH), vec(A_WIDTH), vec(A_WIDTH), vec(A_WIDTH)],
        out_specs=[pl.BlockSpec((blk_rows, A_WIDTH), row), pl.BlockSpec(state_blk, seq4)],
        out_shape=[jax.ShapeDtypeStruct((n_rows, A_WIDTH), BF16),
                   jax.ShapeDtypeStruct((n_seq, A_HEADS, HEAD_DIM, HEAD_DIM), F32)],
        scratch_shapes=[pltpu.VMEM((n_bat * N_PAIRS, HEAD_DIM, LANE), F32), pltpu.VMEM((n_bat, 1, SHIFT_W), F32)],
        compiler_params=pltpu.CompilerParams(dimension_semantics=("arbitrary", "arbitrary"),
                                             vmem_limit_bytes=VMEM_LIMIT),
        name="rwkv",
    )(pa2d, shift0, s0_pairs, prm["mu"], prm["w0"], prm["a0"], prm["wlora"], prm["wgate"], prm["kk"], prm["ka"],
      prm["rk"], prm["lng"], prm["lnb"])


def _swa_bias(tq):
    rows = B_GROUP * tq
    grp = jnp.arange(rows, dtype=jnp.int32)[:, None] // tq
    t = jnp.arange(rows, dtype=jnp.int32)[:, None] % tq
    dist_p = t + WINDOW - jnp.arange(WINDOW, dtype=jnp.int32)[None, :]
    dist_c = t - jnp.arange(tq, dtype=jnp.int32)[None, :]

    def bias(dist, kv):
        slope = sum(jnp.where(grp == g, 2.0 ** -(kv * B_GROUP + g + 1), 0.0) for g in range(B_GROUP))
        return jnp.where((dist >= 0) & (dist < WINDOW), -slope * dist.astype(F32), MASK_NEG)

    return (jnp.stack([bias(dist_p, kv) for kv in range(B_KV_HEADS)]),
            jnp.stack([bias(dist_c, kv) for kv in range(B_KV_HEADS)]))


def _swa_body(sink_ref, q_ref, kp_ref, vp_ref, kc_ref, vc_ref, bp_ref, bc_ref, gn_ref, o_ref, *cache_refs, tq, n_bat,
              n_qb, t_real, first_has_prev):
    nblk = pl.program_id(1)
    lane = lax.broadcasted_iota(jnp.int32, (1, LANE), 1)
    kv_masks = [(lane < HEAD_DIM) if kv == 0 else (lane >= HEAD_DIM) for kv in range(B_KV_HEADS)]
    blocks = []
    for s in range(n_bat * n_qb):
        rs = slice(s * tq, (s + 1) * tq)
        kc, vc = kc_ref[rs], vc_ref[rs]
        if n_qb > 1 and s > 0:
            ps = slice((s - 1) * tq, s * tq)
            kp, vp, has_prev = kc_ref[ps], vc_ref[ps], True
        else:
            kp, vp, has_prev = kp_ref[s], vp_ref[s], first_has_prev
        if cache_refs:
            row = lax.broadcasted_iota(jnp.int32, (WINDOW, 1), 0)
            for out_ref, old, new in zip(cache_refs, (kp, vp), (kc, vc)):
                new_tail = jnp.concatenate([pltpu.roll(new, tq - t_real, axis=0)] * (WINDOW // tq), axis=0)
                out_ref[s] = jnp.where(row >= WINDOW - t_real, new_tail, pltpu.roll(old, WINDOW - t_real, axis=0))
        pen = 0.0 if has_prev else jnp.where(nblk > 0, 0.0, MASK_NEG)
        blocks.append(dict(rs=rs, q=q_ref[rs], kp=kp.astype(BF16), kc=kc.astype(BF16), vp=vp, vc=vc, pen=pen))

    keys = [(b, kv) for b in range(len(blocks)) for kv in range(B_KV_HEADS)]
    s_p, s_c, vpm, vcm = {}, {}, {}, {}
    for b, kv in keys:
        blk, mk = blocks[b], kv_masks[kv]
        q_st = jnp.concatenate([jnp.where(mk, blk["q"][:, g * LANE:(g + 1) * LANE], 0.0) for g in range(B_GROUP)],
                               axis=0).astype(BF16)
        s_p[b, kv] = _dot_nt(q_st, blk["kp"])
        s_c[b, kv] = _dot_nt(q_st, blk["kc"])
        vpm[b, kv] = jnp.where(mk, blk["vp"], 0.0).astype(BF16)
        vcm[b, kv] = jnp.where(mk, blk["vc"], 0.0).astype(BF16)

    slabs = [(b, kv, g) for b, kv in keys for g in range(B_GROUP)]
    e_p, e_c, inv = {}, {}, {}
    for b, kv, g in slabs:
        rs = slice(g * tq, (g + 1) * tq)
        sink = sink_ref[kv * B_GROUP + g]
        sp = s_p[b, kv][rs] + (bp_ref[kv, rs, :] + blocks[b]["pen"])
        sc = s_c[b, kv][rs] + bc_ref[kv, rs, :]
        if tq == WINDOW:
            m = jnp.maximum(jnp.max(jnp.maximum(sp, sc), axis=-1, keepdims=True), sink)
        else:
            m = jnp.maximum(jnp.maximum(jnp.max(sp, axis=-1, keepdims=True), jnp.max(sc, axis=-1, keepdims=True)),
                            sink)
        ep, ec = jnp.exp(sp - m), jnp.exp(sc - m)
        if tq == WINDOW:
            e_sum = jnp.sum(ep + ec, axis=-1, keepdims=True)
        else:
            e_sum = jnp.sum(ep, axis=-1, keepdims=True) + jnp.sum(ec, axis=-1, keepdims=True)
        e_p[b, kv, g], e_c[b, kv, g] = ep.astype(BF16), ec.astype(BF16)
        inv[b, kv, g] = 1.0 / (e_sum + jnp.exp(sink - m))

    tiles = {}
    for b, kv, g in slabs:
        o = (_dot(e_p[b, kv, g], vpm[b, kv]) + _dot(e_c[b, kv, g], vcm[b, kv])) * inv[b, kv, g]
        tiles[b, g] = o if kv == 0 else tiles[b, g] + o

    for b, blk in enumerate(blocks):
        ssq = sum(jnp.sum(tiles[b, g] * tiles[b, g], axis=-1, keepdims=True) for g in range(B_GROUP))
        inv_rms = lax.rsqrt(ssq * (1.0 / B_WIDTH) + RMS_EPS)
        out = jnp.concatenate([tiles[b, g] for g in range(B_GROUP)], axis=1) * inv_rms * gn_ref[...]
        o_ref[blk["rs"]] = out.astype(o_ref.dtype)


def _swa_call(sink, q2d, kprev, vprev, kcur2d, vcur2d, gn, *, n_seq, n_blk, tq, n_bat, first_has_prev, prev_map,
              out_dtype, n_qb=1, cache_rows=0):
    assert n_seq % n_bat == 0 and (n_bat == 1 or n_blk * n_qb == 1) and (cache_rows == 0 or n_blk * n_qb == 1)
    row = lambda b, n: (b * n_blk + n, 0)
    body = functools.partial(_swa_body, tq=tq, n_bat=n_bat, n_qb=n_qb, t_real=cache_rows,
                             first_has_prev=first_has_prev)
    bias_p, bias_c = _swa_bias(tq)
    whole = lambda b, n: (0, 0, 0)
    blk_rows = n_bat * n_qb * tq
    out_specs = [pl.BlockSpec((blk_rows, B_WIDTH), row)]
    out_shape = [jax.ShapeDtypeStruct((n_seq * n_blk * n_qb * tq, B_WIDTH), out_dtype)]
    if cache_rows:
        out_specs += [pl.BlockSpec((n_bat, WINDOW, B_KV_WIDTH), prev_map)] * 2
        out_shape += [jax.ShapeDtypeStruct((n_seq, WINDOW, B_KV_WIDTH), F32)] * 2
    return pl.pallas_call(
        body,
        grid=(n_seq // n_bat, n_blk),
        in_specs=[pl.BlockSpec(memory_space=pltpu.SMEM),
                  pl.BlockSpec((blk_rows, B_WIDTH), row),
                  pl.BlockSpec((n_bat, WINDOW, B_KV_WIDTH), prev_map),
                  pl.BlockSpec((n_bat, WINDOW, B_KV_WIDTH), prev_map),
                  pl.BlockSpec((blk_rows, B_KV_WIDTH), row),
                  pl.BlockSpec((blk_rows, B_KV_WIDTH), row),
                  pl.BlockSpec(bias_p.shape, whole),
                  pl.BlockSpec(bias_c.shape, whole),
                  pl.BlockSpec((1, B_WIDTH), lambda b, n: (0, 0))],
        out_specs=out_specs,
        out_shape=out_shape,
        compiler_params=pltpu.CompilerParams(dimension_semantics=("arbitrary", "arbitrary"),
                                             vmem_limit_bytes=VMEM_LIMIT),
        name="swa",
    )(sink, q2d, kprev, vprev, kcur2d, vcur2d, bias_p, bias_c, gn)


def _route_rows(lg):
    lane = lax.broadcasted_iota(jnp.int32, (1, ROUTE_W), 1)
    lane_f = lane.astype(F32)
    no_lane = float(ROUTE_W)
    is_group = lane < N_GROUPS
    m_g = jnp.max(jnp.where(is_group, lg, MASK_NEG), axis=-1, keepdims=True)
    g_idx = jnp.min(jnp.where(is_group & (lg == m_g), lane_f, no_lane), axis=-1, keepdims=True)
    p_group = 1.0 / jnp.sum(jnp.where(is_group, jnp.exp(lg - m_g), 0.0), axis=-1, keepdims=True)
    e_lane = lane - N_GROUPS
    in_group = (e_lane >= 0) & (e_lane < N_EXPERTS) & ((e_lane >> 2).astype(F32) == g_idx)
    m_1 = jnp.max(jnp.where(in_group, lg, MASK_NEG), axis=-1, keepdims=True)
    i_1 = jnp.min(jnp.where(in_group & (lg == m_1), lane_f, no_lane), axis=-1, keepdims=True)
    rest = in_group & (lane_f != i_1)
    m_2 = jnp.max(jnp.where(rest, lg, MASK_NEG), axis=-1, keepdims=True)
    i_2 = jnp.min(jnp.where(rest & (lg == m_2), lane_f, no_lane), axis=-1, keepdims=True)
    ratio = jnp.exp(m_2 - m_1)
    w_1 = p_group / (1.0 + ratio)
    return jnp.where(lane == 0, i_1 - N_GROUPS,
                     jnp.where(lane == 1, i_2 - N_GROUPS,
                               jnp.where(lane == 2, w_1, jnp.where(lane == 3, w_1 * ratio, 0.0))))


def _outproj_body(*refs, seg_sizes):
    n_seg = len(seg_sizes)
    x_refs, oa_refs, ob_refs = refs[:n_seg], refs[n_seg:2 * n_seg], refs[2 * n_seg:3 * n_seg]
    wa_ref, wb_ref, g_ref, wr_ref, br_ref, x1_ref, h2_ref, rt_ref, rtt_ref = refs[3 * n_seg:]

    def run(x_ref, oa_ref, ob_ref):
        x1 = x_ref[...] + _dot(oa_ref[...].astype(BF16), wa_ref[...]) + _dot(ob_ref[...].astype(BF16), wb_ref[...])
        x1_ref[...] = x1
        h2 = x1 * lax.rsqrt(jnp.mean(x1 * x1, axis=-1, keepdims=True) + RMS_EPS) * g_ref[...]
        h2b = h2.astype(BF16)
        h2_ref[...] = h2b
        lg = _dot(h2b, wr_ref[...]) + br_ref[...]
        slab = _row_tile(lg.shape[0], LANE)
        rt = jnp.concatenate([_route_rows(lg[r:r + slab]) for r in range(0, lg.shape[0], slab)], axis=0)
        rt_ref[...] = rt
        rtt_ref[...] = rt.T[:ROUTE_ROWS]

    _run_segment(pl.program_id(0), seg_sizes, run, list(zip(x_refs, oa_refs, ob_refs)))


def _outproj_call(x_list, x_firsts, oa_list, ob_list, seg_sizes, wa, wb, g, wr, br, *, tm):
    d = x_list[0].shape[1]
    n_tiles = sum(seg_sizes)
    n = n_tiles * tm
    row = lambda i: (i, 0)
    fixed = lambda i: (0, 0)
    zeros = [0] * len(seg_sizes)
    return pl.pallas_call(
        functools.partial(_outproj_body, seg_sizes=tuple(seg_sizes)),
        grid=(n_tiles,),
        in_specs=(_seg_specs(tm, d, seg_sizes, x_firsts) + _seg_specs(tm, A_WIDTH, seg_sizes, zeros)
                  + _seg_specs(tm, B_WIDTH, seg_sizes, zeros)
                  + [pl.BlockSpec((A_WIDTH, d), fixed), pl.BlockSpec((B_WIDTH, d), fixed), pl.BlockSpec((1, d), fixed),
                     pl.BlockSpec((d, ROUTE_W), fixed), pl.BlockSpec((1, ROUTE_W), fixed)]),
        out_specs=[pl.BlockSpec((tm, d), row), pl.BlockSpec((tm, d), row), pl.BlockSpec((tm, ROUTE_W), row),
                   pl.BlockSpec((ROUTE_ROWS, tm), row)],
        out_shape=[jax.ShapeDtypeStruct((n, d), F32), jax.ShapeDtypeStruct((n, d), BF16),
                   jax.ShapeDtypeStruct((n, ROUTE_W), F32),
                   jax.ShapeDtypeStruct((n_tiles * ROUTE_ROWS, tm), F32)],
        compiler_params=pltpu.CompilerParams(dimension_semantics=("arbitrary",), vmem_limit_bytes=VMEM_LIMIT),
        name="outproj",
    )(*x_list, *oa_list, *ob_list, wa, wb, g, wr, br)


def _expert_body(vb_ref, ve_ref, lo_ref, hi_ref, x_ref, wg_ref, wu_ref, wd_ref, y_ref, wg_bf, wu_bf, wd_bf):
    v = pl.program_id(0)
    pv = jnp.maximum(v - 1, 0)
    lo, hi = lo_ref[v], hi_ref[v]
    first_visit = (v == 0) | (vb_ref[v] != vb_ref[pv])

    @pl.when((v == 0) | (ve_ref[v] != ve_ref[pv]))
    def _():
        wg_bf[...] = wg_ref[0].astype(BF16)
        wu_bf[...] = wu_ref[0].astype(BF16)
        wd_bf[...] = wd_ref[0].astype(BF16)

    @pl.when(hi > lo)
    def _():
        x = x_ref[...]
        gate = _dot(x, wg_bf[...])
        up = _dot(x, wu_bf[...])
        mid = (gate * _sigmoid(gate) * up).astype(BF16)
        y = _dot(mid, wd_bf[...])
        row = lax.broadcasted_iota(jnp.int32, (y.shape[0], 1), 0)
        mine = (row >= lo) & (row < hi)

        @pl.when(first_visit)
        def _():
            y_ref[...] = jnp.where(mine, y, 0.0)

        @pl.when(jnp.logical_not(first_visit))
        def _():
            y_ref[...] = jnp.where(mine, y, y_ref[...])


def _expert_call(visits, xs, wg, wu, wd, *, tb):
    n_rows, d = xs.shape
    ff = wg.shape[-1]
    vb, ve, lo, hi = visits
    blk = lambda v, vb, ve, lo, hi: (vb[v], 0)
    wsel = lambda v, vb, ve, lo, hi: (ve[v], 0, 0)
    grid_spec = pltpu.PrefetchScalarGridSpec(
        num_scalar_prefetch=4,
        grid=(vb.shape[0],),
        in_specs=[pl.BlockSpec((tb, d), blk), pl.BlockSpec((1, d, ff), wsel), pl.BlockSpec((1, d, ff), wsel),
                  pl.BlockSpec((1, ff, d), wsel)],
        out_specs=pl.BlockSpec((tb, d), blk),
        scratch_shapes=[pltpu.VMEM((d, ff), BF16), pltpu.VMEM((d, ff), BF16), pltpu.VMEM((ff, d), BF16)],
    )
    return pl.pallas_call(
        _expert_body,
        grid_spec=grid_spec,
        out_shape=jax.ShapeDtypeStruct((n_rows, d), F32),
        compiler_params=pltpu.CompilerParams(dimension_semantics=("arbitrary",), vmem_limit_bytes=VMEM_LIMIT),
        name="experts",
    )(vb, ve, lo, hi, xs, wg, wu, wd)


def _final_body(*refs, seg_sizes):
    n_seg = len(seg_sizes)
    g_ref, o_ref = refs[4 * n_seg:]

    def run(x1_ref, y0_ref, y1_ref, rt_ref):
        rt = rt_ref[...]
        x = x1_ref[...] + (rt[:, 2:3] * y0_ref[...] + rt[:, 3:4] * y1_ref[...])
        o_ref[...] = x * lax.rsqrt(jnp.mean(x * x, axis=-1, keepdims=True) + RMS_EPS) * g_ref[...]

    _run_segment(pl.program_id(0), seg_sizes, run, [refs[4 * s:4 * s + 4] for s in range(n_seg)])


def _final_call(sources, firsts, seg_sizes, g, *, tm):
    d = sources[0][0].shape[1]
    n_tiles = sum(seg_sizes)
    in_specs, start = [], 0
    for size, first in zip(seg_sizes, firsts):
        src = lambda i, s=start, f=first, n=size: (f + jnp.clip(i - s, 0, n - 1), 0)
        in_specs += [pl.BlockSpec((tm, d), src)] * 3 + [pl.BlockSpec((tm, ROUTE_W), src)]
        start += size
    return pl.pallas_call(
        functools.partial(_final_body, seg_sizes=tuple(seg_sizes)),
        grid=(n_tiles,),
        in_specs=in_specs + [pl.BlockSpec((1, d), lambda i: (0, 0))],
        out_specs=pl.BlockSpec((tm, d), lambda i: (i, 0)),
        out_shape=jax.ShapeDtypeStruct((n_tiles * tm, d), F32),
        compiler_params=pltpu.CompilerParams(dimension_semantics=("arbitrary",), vmem_limit_bytes=VMEM_LIMIT),
        name="final",
    )(*[a for src in sources for a in src], g)


def _dispatch(flat_e, n, tb):
    n_assign = n * TOP_K_INNER
    idx_bits = (n_assign - 1).bit_length()
    assert N_EXPERTS << idx_bits < 2 ** 31 and n_assign % tb == 0
    ids = jnp.arange(n_assign, dtype=jnp.int32)
    sorted_key = jnp.sort((flat_e << idx_bits) | ids)
    sorted_id = sorted_key & ((1 << idx_bits) - 1)
    src_tok = sorted_id - n * (sorted_id // n)

    experts = jnp.arange(N_EXPERTS, dtype=jnp.int32)
    onehot = (flat_e[:, None] == experts[None, :]).astype(jnp.int32)
    csum = jnp.cumsum(onehot, axis=0)
    counts = csum[-1]
    ends = jnp.cumsum(counts)
    starts = ends - counts
    dest = (jnp.sum((csum - 1 + starts[None, :]) * onehot, axis=1)).astype(jnp.int32)

    n_blocks = n_assign // tb
    n_visits = n_blocks + N_EXPERTS - 1
    first_blk = starts // tb
    n_vis_e = jnp.where(counts > 0, (ends + tb - 1) // tb - first_blk, 0)
    v_end = jnp.cumsum(n_vis_e)
    v = jnp.arange(n_visits, dtype=jnp.int32)
    valid = v < v_end[-1]
    last_e = jnp.max(jnp.where(counts > 0, experts, 0))
    ve = jnp.where(valid, jnp.sum((v_end[None, :] <= v[:, None]).astype(jnp.int32), axis=1), last_e)
    pick = lambda a: jnp.sum(jnp.where(ve[:, None] == experts[None, :], a[None, :], 0), axis=1)
    vb = pick(first_blk) + v - pick(v_end - n_vis_e)
    lo = jnp.maximum(pick(starts), vb * tb) - vb * tb
    hi = jnp.minimum(pick(ends), (vb + 1) * tb) - vb * tb
    vb = jnp.where(valid, vb, n_blocks - 1)
    lo = jnp.where(valid, lo, 0)
    hi = jnp.where(valid, hi, 0)
    return dest, src_tok, (vb.astype(jnp.int32), ve.astype(jnp.int32), lo.astype(jnp.int32), hi.astype(jnp.int32))


def _q_perm():
    cols = []
    for g in range(B_GROUP):
        for kv in range(B_KV_HEADS):
            h = kv * B_GROUP + g
            cols.extend(range(h * HEAD_DIM, (h + 1) * HEAD_DIM))
    return jnp.array(cols, dtype=jnp.int32)


def _layer(x_prompt, x_sample, state_rwkv, state_shift, cache_win_k, cache_win_v, prm, norm_final_g, *,
           chunk, n_sub, moe_block):
    bp, tp, d = x_prompt.shape
    bs, ts, _ = x_sample.shape
    n_p, n_s = bp * tp, bs * ts
    ts_pad = 8
    tm = n_s
    assert tp % tm == 0 and tm % 8 == 0 and cache_win_k.shape[1] == WINDOW
    xp = x_prompt.reshape(n_p, d)
    xs = x_sample.reshape(n_s, d)

    qp = _q_perm()
    w_in = prm["w_in"]
    w_in = jnp.concatenate([w_in[:, :SHIFT_W], w_in[:, SHIFT_W:SHIFT_W + B_WIDTH][:, qp],
                            w_in[:, SHIFT_W + B_WIDTH:]], axis=1).astype(BF16)
    gn = prm["attn_norm_g"][qp][None]
    w_out = prm["w_out"]
    wa, wb = w_out[:A_WIDTH].astype(BF16), w_out[A_WIDTH:][qp].astype(BF16)
    pad_lanes = jnp.zeros((d, ROUTE_W - N_GROUPS - N_EXPERTS), F32)
    wr = jnp.concatenate([prm["w_route_group"], prm["w_route_expert"], pad_lanes], axis=1).astype(BF16)
    br = jnp.concatenate([prm["b_route_group"], prm["b_route_expert"], pad_lanes[0]])[None]

    zero_blk = jnp.zeros((LORA_W // 2, A_WIDTH), F32)
    wlora = jnp.concatenate([jnp.concatenate([prm["w_decay_up"], zero_blk], axis=1),
                             jnp.concatenate([zero_blk, prm["w_iclr_up"]], axis=1)], axis=0).astype(BF16)
    rp = dict(mu=prm["mu_shift"][None], w0=prm["w_decay0"][None], a0=prm["w_iclr0"][None], wlora=wlora,
              wgate=prm["w_gate_up"].astype(BF16), kk=prm["k_k"][None], ka=prm["k_a"][None],
              rk=prm["r_k"].reshape(1, A_WIDTH), lng=prm["lnx_g"][None], lnb=prm["lnx_b"][None])

    groups = [(0, bp)]
    tiles_per_seq = tp // tm
    nb = tp // WINDOW
    n_qb = _row_tile(nb, 4)
    kv4 = lambda a: a.reshape(a.shape[0], a.shape[1], B_KV_HEADS, HEAD_DIM)
    finals, s_p, sh_p, k_p, v_p = [], [], [], [], []
    for gi, (s0, s1) in enumerate(groups):
        has_sample = gi == len(groups) - 1
        n_seq = s1 - s0
        n_pt = n_seq * tiles_per_seq
        n_pg = n_pt * tm
        seg_sizes = [n_pt] + ([1] if has_sample else [])
        pa, q, k, v = _inproj_call([xp] + ([xs] if has_sample else []), [s0 * tiles_per_seq, 0], seg_sizes,
                                   prm["norm_mix_g"][None], w_in, tm=tm)

        oa_p, s_pg = _rwkv_call(pa, jnp.zeros((n_seq, 1, SHIFT_W), F32),
                                jnp.zeros((n_seq,) + state_rwkv.shape[1:], F32), rp, n_seq=n_seq,
                                n_chunks=tp // chunk, chunk=chunk, n_sub=n_sub, n_bat=1, t_real=chunk)
        ob_p, = _swa_call(prm["attn_sink"], q, k.reshape(-1, WINDOW, B_KV_WIDTH), v.reshape(-1, WINDOW, B_KV_WIDTH),
                          k, v, gn, n_seq=n_seq, n_blk=nb // n_qb, tq=WINDOW, n_bat=1, n_qb=n_qb,
                          first_has_prev=False, out_dtype=BF16,
                          prev_map=lambda b, i: (b * nb + jnp.maximum(i * n_qb - 1, 0), 0, 0))
        oa_list, ob_list = [oa_p], [ob_p]
        if has_sample:
            pad_rows = lambda a: jnp.pad(a[n_pg:].reshape(bs, ts, -1), ((0, 0), (0, ts_pad - ts), (0, 0))).reshape(
                bs * ts_pad, -1)
            oa_s, s_s = _rwkv_call(pad_rows(pa), state_shift[:, None, :], state_rwkv, rp, n_seq=bs,
                                   n_chunks=1, chunk=ts_pad, n_sub=1, n_bat=_row_tile(bs, 8), t_real=ts)
            ob_s, k_s, v_s = _swa_call(prm["attn_sink"], pad_rows(q).astype(F32),
                                       cache_win_k.reshape(bs, WINDOW, B_KV_WIDTH),
                                       cache_win_v.reshape(bs, WINDOW, B_KV_WIDTH), pad_rows(k), pad_rows(v), gn,
                                       n_seq=bs, n_blk=1, tq=ts_pad, n_bat=_row_tile(bs, 16), first_has_prev=True,
                                       prev_map=lambda b, i: (b, 0, 0), out_dtype=F32, cache_rows=ts)
            oa_list.append(oa_s.reshape(bs, ts_pad, A_WIDTH)[:, :ts].reshape(n_s, A_WIDTH))
            ob_list.append(ob_s.reshape(bs, ts_pad, B_WIDTH)[:, :ts].reshape(n_s, B_WIDTH))
            sh_s = pa[n_pg:].reshape(bs, ts, SHIFT_W)[:, -1]

        x1, h2, route, route_t = _outproj_call([xp] + ([xs] if has_sample else []), [s0 * tiles_per_seq, 0],
                                               oa_list, ob_list, seg_sizes, wa, wb, prm["norm_ffn_g"][None], wr, br,
                                               tm=tm)
        n = sum(seg_sizes) * tm
        route_t = route_t.reshape(n // tm, ROUTE_ROWS, tm)
        flat_e = jnp.concatenate([route_t[:, j, :].reshape(n) for j in range(TOP_K_INNER)]).astype(jnp.int32)
        dest, src_tok, visits = _dispatch(flat_e, n, moe_block)
        ybuf = _expert_call(visits, h2[src_tok], prm["w_exp_gate"], prm["w_exp_up"], prm["w_exp_down"], tb=moe_block)
        finals.append((x1, ybuf[dest[:n]], ybuf[dest[n:]], route))

        last_rows = lambda a, m: jnp.stack([a[(b + 1) * tp - m:(b + 1) * tp] for b in range(n_seq)], axis=0)
        s_p.append(s_pg)
        sh_p.append(last_rows(pa, 1)[:, 0])
        k_p.append(last_rows(k, WINDOW))
        v_p.append(last_rows(v, WINDOW))

    gf = norm_final_g[None]
    n_pts = [(s1 - s0) * tiles_per_seq for s0, s1 in groups]
    y_p = _final_call(finals, [0] * len(groups), n_pts, gf, tm=tm)
    y_s = _final_call(finals[-1:], n_pts[-1:], [1], gf, tm=tm)
    cat = lambda parts: jnp.concatenate(parts, axis=0)
    return (y_p.reshape(bp, tp, d), y_s.reshape(bs, ts, d), cat(s_p), cat(sh_p), kv4(cat(k_p)), kv4(cat(v_p)),
            s_s, sh_s, kv4(k_s), kv4(v_s))


def kernel(x_prompt, x_sample, state_rwkv, state_shift, cache_win_k, cache_win_v, norm_mix_g, w_in, mu_shift, w_decay0, w_decay_up, w_iclr0, w_iclr_up, w_gate_up, k_k, k_a, r_k, lnx_g, lnx_b, attn_sink, attn_norm_g, w_out, norm_ffn_g, w_route_group, b_route_group, w_route_expert, b_route_expert, w_exp_gate, w_exp_up, w_exp_down, norm_final_g):
    assert norm_mix_g.shape[0] == 1, "single-layer trunk"
    prm = dict(norm_mix_g=norm_mix_g[0], w_in=w_in[0], mu_shift=mu_shift[0], w_decay0=w_decay0[0],
               w_decay_up=w_decay_up[0], w_iclr0=w_iclr0[0], w_iclr_up=w_iclr_up[0], w_gate_up=w_gate_up[0],
               k_k=k_k[0], k_a=k_a[0], r_k=r_k[0], lnx_g=lnx_g[0], lnx_b=lnx_b[0], attn_sink=attn_sink[0],
               attn_norm_g=attn_norm_g[0], w_out=w_out[0], norm_ffn_g=norm_ffn_g[0],
               w_route_group=w_route_group[0], b_route_group=b_route_group[0],
               w_route_expert=w_route_expert[0], b_route_expert=b_route_expert[0],
               w_exp_gate=w_exp_gate[0], w_exp_up=w_exp_up[0], w_exp_down=w_exp_down[0])
    outs = _layer(x_prompt, x_sample, state_rwkv[0], state_shift[0], cache_win_k[0], cache_win_v[0], prm,
                  norm_final_g, chunk=64, n_sub=4, moe_block=512)
    y_p, y_s, s_p, sh_p, kp, vp, s_s, sh_s, ks, vs = outs
    return (y_p, y_s, s_p[None], sh_p[None], kp[None], vp[None], s_s[None], sh_s[None], ks[None], vs[None])
```

```python
import functools

import jax
import jax.numpy as jnp
from jax import lax
from jax.experimental import pallas as pl
from jax.experimental.pallas import tpu as pltpu

F32 = jnp.float32
BF16 = jnp.bfloat16

HEAD_DIM = 64
A_HEADS = 8
A_WIDTH = A_HEADS * HEAD_DIM
B_HEADS = 8
B_KV_HEADS = 2
B_GROUP = B_HEADS // B_KV_HEADS
B_WIDTH = B_HEADS * HEAD_DIM
B_KV_WIDTH = B_KV_HEADS * HEAD_DIM
DECAY_LORA = 64
ICLR_LORA = 64
GATE_LORA = 128
LORA_W = DECAY_LORA + ICLR_LORA
SHIFT_W = 3 * A_WIDTH + LORA_W + GATE_LORA
IN_W = SHIFT_W + B_WIDTH + 2 * B_KV_WIDTH
WINDOW = 128
N_GROUPS = 4
EXPERTS_PER_GROUP = 4
N_EXPERTS = N_GROUPS * EXPERTS_PER_GROUP
TOP_K_INNER = 2
EXPERT_FF = 512
RMS_EPS = 1e-6
LNX_EPS = 64e-5
DECAY_OFFSET = 0.5

LANE = 128
PAIR = LANE // HEAD_DIM
N_PAIRS = A_HEADS // PAIR
ROUTE_W = LANE
ROUTE_ROWS = 8
MASK_NEG = -1e30
VMEM_LIMIT = 48 * 1024 * 1024

LOG2_E = 1.4426950408889634
HEAD_SHIFT = HEAD_DIM.bit_length() - 1
HI = lax.Precision.HIGHEST


def _row_tile(n, cap):
    t = cap
    while n % t:
        t //= 2
    return t


def _dot(a, b, precision=None):
    return jnp.dot(a, b, preferred_element_type=F32, precision=precision)


def _dot_nt(a, b, precision=None):
    return lax.dot_general(a, b, (((1,), (1,)), ((), ())), preferred_element_type=F32, precision=precision)


def _dot_tn(a, b, precision=None):
    return lax.dot_general(a, b, (((0,), (0,)), ((), ())), preferred_element_type=F32, precision=precision)


def _sigmoid(x):
    return 1.0 / (1.0 + jnp.exp(-x))


def _hi_lo(x, axis):
    hi = x.astype(BF16)
    lo = (x - hi.astype(F32)).astype(BF16)
    return jnp.concatenate([hi, lo], axis=axis)


def _run_segment(i, seg_sizes, run, seg_refs):
    start = 0
    for size, refs in zip(seg_sizes, seg_refs):
        pl.when((i >= start) & (i < start + size))(functools.partial(run, *refs))
        start += size


def _seg_specs(tm, width, seg_sizes, firsts):
    specs, start = [], 0
    for size, first in zip(seg_sizes, firsts):
        specs.append(pl.BlockSpec((tm, width),
                                  lambda i, s=start, f=first, n=size: (f + jnp.clip(i - s, 0, n - 1), 0)))
        start += size
    return specs


def _inproj_body(*refs, seg_sizes):
    n_seg = len(seg_sizes)
    g_ref, w_ref, pa_ref, q_ref, k_ref, v_ref = refs[n_seg:]

    def run(x_ref):
        x = x_ref[...]
        h = x * lax.rsqrt(jnp.mean(x * x, axis=-1, keepdims=True) + RMS_EPS) * g_ref[...]
        p = _dot(h.astype(BF16), w_ref[...])
        pa_ref[...] = p[:, :SHIFT_W]
        q_ref[...] = (p[:, SHIFT_W:SHIFT_W + B_WIDTH] * (HEAD_DIM ** -0.5)).astype(BF16)
        k_ref[...] = p[:, SHIFT_W + B_WIDTH:SHIFT_W + B_WIDTH + B_KV_WIDTH]
        v_ref[...] = p[:, SHIFT_W + B_WIDTH + B_KV_WIDTH:]

    _run_segment(pl.program_id(0), seg_sizes, run, [(r,) for r in refs[:n_seg]])


def _inproj_call(xs_list, firsts, seg_sizes, g, w_bf16, *, tm):
    d = xs_list[0].shape[1]
    n_tiles = sum(seg_sizes)
    n = n_tiles * tm
    row = lambda i: (i, 0)
    fixed = lambda i: (0, 0)
    return pl.pallas_call(
        functools.partial(_inproj_body, seg_sizes=tuple(seg_sizes)),
        grid=(n_tiles,),
        in_specs=_seg_specs(tm, d, seg_sizes, firsts) + [pl.BlockSpec((1, d), fixed), pl.BlockSpec((d, IN_W), fixed)],
        out_specs=[pl.BlockSpec((tm, SHIFT_W), row), pl.BlockSpec((tm, B_WIDTH), row),
                   pl.BlockSpec((tm, B_KV_WIDTH), row), pl.BlockSpec((tm, B_KV_WIDTH), row)],
        out_shape=[jax.ShapeDtypeStruct((n, SHIFT_W), F32), jax.ShapeDtypeStruct((n, B_WIDTH), BF16),
                   jax.ShapeDtypeStruct((n, B_KV_WIDTH), F32), jax.ShapeDtypeStruct((n, B_KV_WIDTH), F32)],
        compiler_params=pltpu.CompilerParams(dimension_semantics=("arbitrary",), vmem_limit_bytes=VMEM_LIMIT),
        name="inproj",
    )(*xs_list, g, w_bf16)


def _rwkv_body(pa_ref, shift0_ref, s0_ref, mu_ref, w0_ref, a0_ref, wlora_ref, wgate_ref, kk_ref, ka_ref,
               rk_ref, lng_ref, lnb_ref, oa_ref, sout_ref, s_scr, prev_scr, *, chunk, n_sub, n_bat, t_real):
    C = chunk
    seq_rows = n_sub * C
    n_seg = n_bat * n_sub
    rows = n_seg * C
    c = pl.program_id(1)

    @pl.when(c == 0)
    def _():
        prev_scr[...] = shift0_ref[...]
        for b in range(n_bat):
            for j in range(N_PAIRS):
                s_scr[b * N_PAIRS + j] = jnp.concatenate([s0_ref[b, PAIR * j + h] for h in range(PAIR)], axis=1)

    pa = pa_ref[...]
    row = lax.broadcasted_iota(jnp.int32, (rows, 1), 0)
    row_in_chunk = row & (C - 1)
    pa_prev = pltpu.roll(pa, 1, axis=0)
    for b in range(n_bat):
        pa_prev = jnp.where(row == b * seq_rows, prev_scr[b], pa_prev)
        last = (b + 1) * seq_rows - C + t_real - 1
        prev_scr[b] = pa[last:last + 1]
    xm = pa + mu_ref[...] * (pa_prev - pa)

    r = xm[:, :A_WIDTH]
    k = xm[:, A_WIDTH:2 * A_WIDTH]
    v = xm[:, 2 * A_WIDTH:3 * A_WIDTH]
    lora_in = xm[:, 3 * A_WIDTH:3 * A_WIDTH + LORA_W]
    gd = xm[:, 3 * A_WIDTH + LORA_W:]

    lane = lax.broadcasted_iota(jnp.int32, (1, LANE), 1)
    lo_half = lane < HEAD_DIM
    z = jnp.where(lo_half, jnp.tanh(lora_in), lora_in)
    lw = _dot(z.astype(BF16), wlora_ref[...])
    dec_pre = w0_ref[...] + lw[:, :A_WIDTH]
    a = _sigmoid(a0_ref[...] + lw[:, A_WIDTH:])
    neg = -dec_pre
    softplus = jnp.maximum(neg, 0.0) + jnp.log(1.0 + jnp.exp(-jnp.abs(neg)))
    logdec = -LOG2_E * jnp.exp(-softplus - DECAY_OFFSET)
    g = _dot(_sigmoid(gd).astype(BF16), wgate_ref[...])

    ri = lax.broadcasted_iota(jnp.int32, (LANE, LANE), 0)
    ci = lax.broadcasted_iota(jnp.int32, (LANE, LANE), 1)
    seg = jnp.where((ri >> HEAD_SHIFT) == (ci >> HEAD_SHIFT), 1.0, 0.0).astype(BF16)
    seg2 = jnp.concatenate([seg, seg], axis=0)

    def headsum(x):
        return jnp.concatenate([_dot(_hi_lo(x[:, j * LANE:(j + 1) * LANE], 1), seg2) for j in range(N_PAIRS)],
                               axis=1)

    kk = k * kk_ref[...]
    kk = kk * lax.rsqrt(jnp.maximum(headsum(kk * kk), 1e-24))
    k = k * (1.0 + (a - 1.0) * ka_ref[...])
    bonus = headsum(r * k * rk_ref[...]) * v

    if t_real < C:
        valid = row_in_chunk < t_real
        logdec = jnp.where(valid, logdec, 0.0)
        kk = jnp.where(valid, kk, 0.0)
        k = jnp.where(valid, k, 0.0)
        v = jnp.where(valid, v, 0.0)

    cum = logdec
    shift = 1
    while shift < C:
        cum = cum + jnp.where(row_in_chunk >= shift, pltpu.roll(cum, shift, axis=0), 0.0)
        shift *= 2
    ends = [cum[(s + 1) * C - 1:(s + 1) * C] for s in range(n_seg)]
    cum_end = jnp.concatenate([jnp.broadcast_to(e, (C, A_WIDTH)) for e in ends], axis=0) if n_seg > 1 else ends[0]
    w_incl = jnp.exp2(cum)
    w_prev = jnp.exp2(cum - logdec)
    w_inv = jnp.exp2(-cum)
    w_end = jnp.exp2(cum_end - cum)
    w_chunk = [jnp.exp2(e) for e in ends]
    kka = kk * a
    terms = dict(A=-kk * w_prev, R=r * w_incl, B=kka * w_inv, K=k * w_inv, V=v, Be=kka * w_end, Ke=k * w_end)

    C2 = PAIR * C
    t_idx = lax.broadcasted_iota(jnp.int32, (C, 1), 0)
    i_idx = lax.broadcasted_iota(jnp.int32, (1, C2), 1) & (C - 1)
    strict = i_idx < t_idx
    incl = i_idx <= t_idx
    ident = jnp.where(i_idx == t_idx, 1.0, 0.0).astype(F32)
    lo_time = lax.broadcasted_iota(jnp.int32, (1, C2), 1) < C
    n_levels = max(1, (C - 1).bit_length())
    bf = lambda x: x.astype(BF16)

    def bd(x, lo_mask=lo_half):
        return bf(jnp.concatenate([jnp.where(lo_mask, x, 0.0), jnp.where(lo_mask, 0.0, x)], axis=0))

    chains = [(s, j) for s in range(n_seg) for j in range(N_PAIRS)]
    tile = lambda name, s, j: terms[name][s * C:(s + 1) * C, j * LANE:(j + 1) * LANE]
    r_sbs = {ch: tile("R", *ch) for ch in chains}
    stk = {ch: {nm: bd(tile(nm, *ch)) for nm in ("A", "B", "K", "V", "Be", "Ke")} for ch in chains}
    m_ab, aak, arb, ark = {}, {}, {}, {}
    for ch in chains:
        t = stk[ch]
        a_sbs, rb = bf(tile("A", *ch)), bf(r_sbs[ch])
        if C2 % LANE == 0:
            m1 = _dot_nt(jnp.concatenate([a_sbs, rb], axis=0), jnp.concatenate([t["B"], t["K"]], axis=0))
            ab, ak, rbm, rk = m1[:C, :C2], m1[:C, C2:], m1[C:, :C2], m1[C:, C2:]
        else:
            ab, ak = _dot_nt(a_sbs, t["B"]), _dot_nt(a_sbs, t["K"])
            rbm, rk = _dot_nt(rb, t["B"]), _dot_nt(rb, t["K"])
        m_ab[ch] = jnp.where(strict, ab, 0.0)
        aak[ch] = jnp.where(strict, ak, 0.0)
        arb[ch] = bf(jnp.where(incl, rbm, 0.0))
        ark[ch] = jnp.where(incl, rk, 0.0)

    nn = dict(m_ab)
    tinv = {ch: ident + m_ab[ch] for ch in chains}
    for lvl in range(1, n_levels):
        if lvl == 1:
            for ch in chains:
                nn[ch] = _dot(bf(nn[ch]), bd(nn[ch], lo_time))
            continue
        for ch in chains:
            both = _dot(bf(jnp.concatenate([nn[ch], tinv[ch]], axis=0)), bd(nn[ch], lo_time))
            nn[ch] = both[:C]
            tinv[ch] = tinv[ch] + both[C:]
    if n_levels > 1:
        for ch in chains:
            tinv[ch] = tinv[ch] + _dot(bf(tinv[ch]), bd(nn[ch], lo_time))
    av = {ch: _dot(bf(jnp.concatenate([aak[ch], ark[ch]], axis=0)), stk[ch]["V"]) for ch in chains}
    x = {ch: _dot(bf(tinv[ch]), jnp.concatenate([stk[ch]["A"], bd(av[ch][:C])], axis=1)) for ch in chains}
    a_eff = {ch: bd(x[ch][:, :LANE]) for ch in chains}
    v_eff = {ch: bd(x[ch][:, LANE:]) for ch in chains}
    zed = {ch: _dot(arb[ch], jnp.concatenate([a_eff[ch], v_eff[ch]], axis=1)) for ch in chains}
    rhat = {ch: bf(r_sbs[ch] + zed[ch][:, :LANE]) for ch in chains}
    y0 = {ch: av[ch][C:] + zed[ch][:, LANE:] for ch in chains}
    p_mat = {ch: bf(_dot_tn(stk[ch]["Be"], a_eff[ch])) for ch in chains}
    q_bd = {ch: _dot_tn(jnp.concatenate([v_eff[ch], stk[ch]["V"]], axis=0),
                        jnp.concatenate([stk[ch]["Be"], stk[ch]["Ke"]], axis=0)) for ch in chains}

    y_rows = []
    for b in range(n_bat):
        state = [s_scr[b * N_PAIRS + j] for j in range(N_PAIRS)]
        for sub in range(n_sub):
            s = b * n_sub + sub
            ys = []
            for j in range(N_PAIRS):
                ch = (s, j)
                hi = bf(state[j])
                lo = bf(state[j] - hi.astype(F32))
                yy = _dot_nt(rhat[ch], jnp.concatenate([bd(hi), bd(lo)], axis=0))
                ys.append(yy[:, :LANE] + yy[:, LANE:] + y0[ch])
                sp = _dot_nt(jnp.concatenate([hi, lo], axis=0), p_mat[ch])
                q_sbs = q_bd[ch][:HEAD_DIM] + q_bd[ch][HEAD_DIM:]
                state[j] = state[j] * w_chunk[s][:, j * LANE:(j + 1) * LANE] + sp[:HEAD_DIM] + sp[HEAD_DIM:] + q_sbs
            y_rows.append(jnp.concatenate(ys, axis=1))
        for j in range(N_PAIRS):
            s_scr[b * N_PAIRS + j] = state[j]

    y = jnp.concatenate(y_rows, axis=0) if n_seg > 1 else y_rows[0]
    inv_n = 1.0 / HEAD_DIM
    mean = headsum(y) * inv_n
    d = y - mean
    var = headsum(d * d) * inv_n
    yn = d * lax.rsqrt(var + LNX_EPS) * lng_ref[...] + lnb_ref[...]
    oa_ref[...] = ((yn + bonus) * g).astype(oa_ref.dtype)

    @pl.when(c == pl.num_programs(1) - 1)
    def _():
        for b in range(n_bat):
            for j in range(N_PAIRS):
                for h in range(PAIR):
                    sout_ref[b, PAIR * j + h] = s_scr[b * N_PAIRS + j][:, h * HEAD_DIM:(h + 1) * HEAD_DIM]


def _rwkv_call(pa2d, shift0, s0_pairs, prm, *, n_seq, n_chunks, chunk, n_sub, n_bat, t_real):
    n_steps = n_chunks // n_sub
    assert n_seq % n_bat == 0 and (n_bat == 1 or n_steps == 1)
    row = lambda b, c: (b * n_steps + c, 0)
    seq3 = lambda b, c: (b, 0, 0)
    fixed = lambda b, c: (0, 0)
    vec = lambda w: pl.BlockSpec((1, w), fixed)
    body = functools.partial(_rwkv_body, chunk=chunk, n_sub=n_sub, n_bat=n_bat, t_real=t_real)
    n_rows = n_seq * n_chunks * chunk
    blk_rows = n_bat * n_sub * chunk
    state_blk = (n_bat, A_HEADS, HEAD_DIM, HEAD_DIM)
    seq4 = lambda b, c: (b, 0, 0, 0)
    return pl.pallas_call(
        body,
        grid=(n_seq // n_bat, n_steps),
        in_specs=[pl.BlockSpec((blk_rows, SHIFT_W), row),
                  pl.BlockSpec((n_bat, 1, SHIFT_W), seq3),
                  pl.BlockSpec(state_blk, seq4),
                  vec(SHIFT_W), vec(A_WIDTH), vec(A_WIDTH),
                  pl.BlockSpec((LORA_W, 2 * A_WIDTH), fixed), pl.BlockSpec((GATE_LORA, A_WIDTH), fixed),
                  vec(A_WIDTH), vec(A_WIDTH), vec(A_WIDTH), vec(A_WIDTH), vec(A_WIDTH)],
        out_specs=[pl.BlockSpec((blk_rows, A_WIDTH), row), pl.BlockSpec(state_blk, seq4)],
        out_shape=[jax.ShapeDtypeStruct((n_rows, A_WIDTH), BF16),
                   jax.ShapeDtypeStruct((n_seq, A_HEADS, HEAD_DIM, HEAD_DIM), F32)],
        scratch_shapes=[pltpu.VMEM((n_bat * N_PAIRS, HEAD_DIM, LANE), F32), pltpu.VMEM((n_bat, 1, SHIFT_W), F32)],
        compiler_params=pltpu.CompilerParams(dimension_semantics=("arbitrary", "arbitrary"),
                                             vmem_limit_bytes=VMEM_LIMIT),
        name="rwkv",
    )(pa2d, shift0, s0_pairs, prm["mu"], prm["w0"], prm["a0"], prm["wlora"], prm["wgate"], prm["kk"], prm["ka"],
      prm["rk"], prm["lng"], prm["lnb"])


def _swa_bias(tq):
    rows = B_GROUP * tq
    grp = jnp.arange(rows, dtype=jnp.int32)[:, None] // tq
    t = jnp.arange(rows, dtype=jnp.int32)[:, None] % tq
    dist_p = t + WINDOW - jnp.arange(WINDOW, dtype=jnp.int32)[None, :]
    dist_c = t - jnp.arange(tq, dtype=jnp.int32)[None, :]

    def bias(dist, kv):
        slope = sum(jnp.where(grp == g, 2.0 ** -(kv * B_GROUP + g + 1), 0.0) for g in range(B_GROUP))
        return jnp.where((dist >= 0) & (dist < WINDOW), -slope * dist.astype(F32), MASK_NEG)

    return (jnp.stack([bias(dist_p, kv) for kv in range(B_KV_HEADS)]),
            jnp.stack([bias(dist_c, kv) for kv in range(B_KV_HEADS)]))


def _swa_body(sink_ref, q_ref, kp_ref, vp_ref, kc_ref, vc_ref, bp_ref, bc_ref, gn_ref, o_ref, *cache_refs, tq, n_bat,
              n_qb, t_real, first_has_prev):
    nblk = pl.program_id(1)
    lane = lax.broadcasted_iota(jnp.int32, (1, LANE), 1)
    kv_masks = [(lane < HEAD_DIM) if kv == 0 else (lane >= HEAD_DIM) for kv in range(B_KV_HEADS)]
    blocks = []
    for s in range(n_bat * n_qb):
        rs = slice(s * tq, (s + 1) * tq)
        kc, vc = kc_ref[rs], vc_ref[rs]
        if n_qb > 1 and s > 0:
            ps = slice((s - 1) * tq, s * tq)
            kp, vp, has_prev = kc_ref[ps], vc_ref[ps], True
        else:
            kp, vp, has_prev = kp_ref[s], vp_ref[s], first_has_prev
        if cache_refs:
            row = lax.broadcasted_iota(jnp.int32, (WINDOW, 1), 0)
            for out_ref, old, new in zip(cache_refs, (kp, vp), (kc, vc)):
                new_tail = jnp.concatenate([pltpu.roll(new, tq - t_real, axis=0)] * (WINDOW // tq), axis=0)
                out_ref[s] = jnp.where(row >= WINDOW - t_real, new_tail, pltpu.roll(old, WINDOW - t_real, axis=0))
        pen = 0.0 if has_prev else jnp.where(nblk > 0, 0.0, MASK_NEG)
        blocks.append(dict(rs=rs, q=q_ref[rs], kp=kp.astype(BF16), kc=kc.astype(BF16), vp=vp, vc=vc, pen=pen))

    keys = [(b, kv) for b in range(len(blocks)) for kv in range(B_KV_HEADS)]
    s_p, s_c, vpm, vcm = {}, {}, {}, {}
    for b, kv in keys:
        blk, mk = blocks[b], kv_masks[kv]
        q_st = jnp.concatenate([jnp.where(mk, blk["q"][:, g * LANE:(g + 1) * LANE], 0.0) for g in range(B_GROUP)],
                               axis=0).astype(BF16)
        s_p[b, kv] = _dot_nt(q_st, blk["kp"])
        s_c[b, kv] = _dot_nt(q_st, blk["kc"])
        vpm[b, kv] = jnp.where(mk, blk["vp"], 0.0).astype(BF16)
        vcm[b, kv] = jnp.where(mk, blk["vc"], 0.0).astype(BF16)

    slabs = [(b, kv, g) for b, kv in keys for g in range(B_GROUP)]
    e_p, e_c, inv = {}, {}, {}
    for b, kv, g in slabs:
        rs = slice(g * tq, (g + 1) * tq)
        sink = sink_ref[kv * B_GROUP + g]
        sp = s_p[b, kv][rs] + (bp_ref[kv, rs, :] + blocks[b]["pen"])
        sc = s_c[b, kv][rs] + bc_ref[kv, rs, :]
        if tq == WINDOW:
            m = jnp.maximum(jnp.max(jnp.maximum(sp, sc), axis=-1, keepdims=True), sink)
        else:
            m = jnp.maximum(jnp.maximum(jnp.max(sp, axis=-1, keepdims=True), jnp.max(sc, axis=-1, keepdims=True)),
                            sink)
        ep, ec = jnp.exp(sp - m), jnp.exp(sc - m)
        if tq == WINDOW:
            e_sum = jnp.sum(ep + ec, axis=-1, keepdims=True)
        else:
            e_sum = jnp.sum(ep, axis=-1, keepdims=True) + jnp.sum(ec, axis=-1, keepdims=True)
        e_p[b, kv, g], e_c[b, kv, g] = ep.astype(BF16), ec.astype(BF16)
        inv[b, kv, g] = 1.0 / (e_sum + jnp.exp(sink - m))

    tiles = {}
    for b, kv, g in slabs:
        o = (_dot(e_p[b, kv, g], vpm[b, kv]) + _dot(e_c[b, kv, g], vcm[b, kv])) * inv[b, kv, g]
        tiles[b, g] = o if kv == 0 else tiles[b, g] + o

    for b, blk in enumerate(blocks):
        ssq = sum(jnp.sum(tiles[b, g] * tiles[b, g], axis=-1, keepdims=True) for g in range(B_GROUP))
        inv_rms = lax.rsqrt(ssq * (1.0 / B_WIDTH) + RMS_EPS)
        out = jnp.concatenate([tiles[b, g] for g in range(B_GROUP)], axis=1) * inv_rms * gn_ref[...]
        o_ref[blk["rs"]] = out.astype(o_ref.dtype)


def _swa_call(sink, q2d, kprev, vprev, kcur2d, vcur2d, gn, *, n_seq, n_blk, tq, n_bat, first_has_prev, prev_map,
              out_dtype, n_qb=1, cache_rows=0):
    assert n_seq % n_bat == 0 and (n_bat == 1 or n_blk * n_qb == 1) and (cache_rows == 0 or n_blk * n_qb == 1)
    row = lambda b, n: (b * n_blk + n, 0)
    body = functools.partial(_swa_body, tq=tq, n_bat=n_bat, n_qb=n_qb, t_real=cache_rows,
                             first_has_prev=first_has_prev)
    bias_p, bias_c = _swa_bias(tq)
    whole = lambda b, n: (0, 0, 0)
    blk_rows = n_bat * n_qb * tq
    out_specs = [pl.BlockSpec((blk_rows, B_WIDTH), row)]
    out_shape = [jax.ShapeDtypeStruct((n_seq * n_blk * n_qb * tq, B_WIDTH), out_dtype)]
    if cache_rows:
        out_specs += [pl.BlockSpec((n_bat, WINDOW, B_KV_WIDTH), prev_map)] * 2
        out_shape += [jax.ShapeDtypeStruct((n_seq, WINDOW, B_KV_WIDTH), F32)] * 2
    return pl.pallas_call(
        body,
        grid=(n_seq // n_bat, n_blk),
        in_specs=[pl.BlockSpec(memory_space=pltpu.SMEM),
                  pl.BlockSpec((blk_rows, B_WIDTH), row),
                  pl.BlockSpec((n_bat, WINDOW, B_KV_WIDTH), prev_map),
                  pl.BlockSpec((n_bat, WINDOW, B_KV_WIDTH), prev_map),
                  pl.BlockSpec((blk_rows, B_KV_WIDTH), row),
                  pl.BlockSpec((blk_rows, B_KV_WIDTH), row),
                  pl.BlockSpec(bias_p.shape, whole),
                  pl.BlockSpec(bias_c.shape, whole),
                  pl.BlockSpec((1, B_WIDTH), lambda b, n: (0, 0))],
        out_specs=out_specs,
        out_shape=out_shape,
        compiler_params=pltpu.CompilerParams(dimension_semantics=("arbitrary", "arbitrary"),
                                             vmem_limit_bytes=VMEM_LIMIT),
        name="swa",
    )(sink, q2d, kprev, vprev, kcur2d, vcur2d, bias_p, bias_c, gn)


def _route_rows(lg):
    lane = lax.broadcasted_iota(jnp.int32, (1, ROUTE_W), 1)
    lane_f = lane.astype(F32)
    no_lane = float(ROUTE_W)
    is_group = lane < N_GROUPS
    m_g = jnp.max(jnp.where(is_group, lg, MASK_NEG), axis=-1, keepdims=True)
    g_idx = jnp.min(jnp.where(is_group & (lg == m_g), lane_f, no_lane), axis=-1, keepdims=True)
    p_group = 1.0 / jnp.sum(jnp.where(is_group, jnp.exp(lg - m_g), 0.0), axis=-1, keepdims=True)
    e_lane = lane - N_GROUPS
    in_group = (e_lane >= 0) & (e_lane < N_EXPERTS) & ((e_lane >> 2).astype(F32) == g_idx)
    m_1 = jnp.max(jnp.where(in_group, lg, MASK_NEG), axis=-1, keepdims=True)
    i_1 = jnp.min(jnp.where(in_group & (lg == m_1), lane_f, no_lane), axis=-1, keepdims=True)
    rest = in_group & (lane_f != i_1)
    m_2 = jnp.max(jnp.where(rest, lg, MASK_NEG), axis=-1, keepdims=True)
    i_2 = jnp.min(jnp.where(rest & (lg == m_2), lane_f, no_lane), axis=-1, keepdims=True)
    ratio = jnp.exp(m_2 - m_1)
    w_1 = p_group / (1.0 + ratio)
    return jnp.where(lane == 0, i_1 - N_GROUPS,
                     jnp.where(lane == 1, i_2 - N_GROUPS,
                               jnp.where(lane == 2, w_1, jnp.where(lane == 3, w_1 * ratio, 0.0))))


def _outproj_body(*refs, seg_sizes):
    n_seg = len(seg_sizes)
    x_refs, oa_refs, ob_refs = refs[:n_seg], refs[n_seg:2 * n_seg], refs[2 * n_seg:3 * n_seg]
    wa_ref, wb_ref, g_ref, wr_ref, br_ref, x1_ref, h2_ref, rt_ref, rtt_ref = refs[3 * n_seg:]

    def run(x_ref, oa_ref, ob_ref):
        tm = x_ref.shape[0]
        part = _row_tile(tm, 2 * LANE)
        for r in range(0, tm, part):
            rows = slice(r, r + part)
            x1 = (x_ref[rows] + _dot(oa_ref[rows].astype(BF16), wa_ref[...])
                  + _dot(ob_ref[rows].astype(BF16), wb_ref[...]))
            x1_ref[rows] = x1
            h2 = x1 * lax.rsqrt(jnp.mean(x1 * x1, axis=-1, keepdims=True) + RMS_EPS) * g_ref[...]
            h2b = h2.astype(BF16)
            h2_ref[rows] = h2b
            lg = _dot(h2b, wr_ref[...]) + br_ref[...]
            rt = jnp.concatenate([_route_rows(lg[s:s + LANE]) for s in range(0, part, LANE)], axis=0)
            rt_ref[rows] = rt
            rtt_ref[:, rows] = rt.T[:ROUTE_ROWS]

    _run_segment(pl.program_id(0), seg_sizes, run, list(zip(x_refs, oa_refs, ob_refs)))


def _outproj_call(x_list, x_firsts, oa_list, ob_list, seg_sizes, wa, wb, g, wr, br, *, tm):
    d = x_list[0].shape[1]
    n_tiles = sum(seg_sizes)
    n = n_tiles * tm
    row = lambda i: (i, 0)
    fixed = lambda i: (0, 0)
    zeros = [0] * len(seg_sizes)
    return pl.pallas_call(
        functools.partial(_outproj_body, seg_sizes=tuple(seg_sizes)),
        grid=(n_tiles,),
        in_specs=(_seg_specs(tm, d, seg_sizes, x_firsts) + _seg_specs(tm, A_WIDTH, seg_sizes, zeros)
                  + _seg_specs(tm, B_WIDTH, seg_sizes, zeros)
                  + [pl.BlockSpec((A_WIDTH, d), fixed), pl.BlockSpec((B_WIDTH, d), fixed), pl.BlockSpec((1, d), fixed),
                     pl.BlockSpec((d, ROUTE_W), fixed), pl.BlockSpec((1, ROUTE_W), fixed)]),
        out_specs=[pl.BlockSpec((tm, d), row), pl.BlockSpec((tm, d), row), pl.BlockSpec((tm, ROUTE_W), row),
                   pl.BlockSpec((ROUTE_ROWS, tm), row)],
        out_shape=[jax.ShapeDtypeStruct((n, d), F32), jax.ShapeDtypeStruct((n, d), BF16),
                   jax.ShapeDtypeStruct((n, ROUTE_W), F32),
                   jax.ShapeDtypeStruct((n_tiles * ROUTE_ROWS, tm), F32)],
        compiler_params=pltpu.CompilerParams(dimension_semantics=("arbitrary",), vmem_limit_bytes=VMEM_LIMIT),
        name="outproj",
    )(*x_list, *oa_list, *ob_list, wa, wb, g, wr, br)


def _expert_body(vb_ref, ve_ref, lo_ref, hi_ref, x_ref, wg_ref, wu_ref, wd_ref, y_ref, wg_bf, wu_bf, wd_bf):
    v = pl.program_id(0)
    pv = jnp.maximum(v - 1, 0)
    lo, hi = lo_ref[v], hi_ref[v]
    first_visit = (v == 0) | (vb_ref[v] != vb_ref[pv])

    @pl.when((v == 0) | (ve_ref[v] != ve_ref[pv]))
    def _():
        wg_bf[...] = wg_ref[0].astype(BF16)
        wu_bf[...] = wu_ref[0].astype(BF16)
        wd_bf[...] = wd_ref[0].astype(BF16)

    @pl.when(hi > lo)
    def _():
        x = x_ref[...]
        gate = _dot(x, wg_bf[...])
        up = _dot(x, wu_bf[...])
        mid = (gate * _sigmoid(gate) * up).astype(BF16)
        y = _dot(mid, wd_bf[...])
        row = lax.broadcasted_iota(jnp.int32, (y.shape[0], 1), 0)
        mine = (row >= lo) & (row < hi)

        @pl.when(first_visit)
        def _():
            y_ref[...] = jnp.where(mine, y, 0.0)

        @pl.when(jnp.logical_not(first_visit))
        def _():
            y_ref[...] = jnp.where(mine, y, y_ref[...])


def _expert_call(visits, xs, wg, wu, wd, *, tb):
    n_rows, d = xs.shape
    ff = wg.shape[-1]
    vb, ve, lo, hi = visits
    blk = lambda v, vb, ve, lo, hi: (vb[v], 0)
    wsel = lambda v, vb, ve, lo, hi: (ve[v], 0, 0)
    grid_spec = pltpu.PrefetchScalarGridSpec(
        num_scalar_prefetch=4,
        grid=(vb.shape[0],),
        in_specs=[pl.BlockSpec((tb, d), blk), pl.BlockSpec((1, d, ff), wsel), pl.BlockSpec((1, d, ff), wsel),
                  pl.BlockSpec((1, ff, d), wsel)],
        out_specs=pl.BlockSpec((tb, d), blk),
        scratch_shapes=[pltpu.VMEM((d, ff), BF16), pltpu.VMEM((d, ff), BF16), pltpu.VMEM((ff, d), BF16)],
    )
    return pl.pallas_call(
        _expert_body,
        grid_spec=grid_spec,
        out_shape=jax.ShapeDtypeStruct((n_rows, d), F32),
        compiler_params=pltpu.CompilerParams(dimension_semantics=("arbitrary",), vmem_limit_bytes=VMEM_LIMIT),
        name="experts",
    )(vb, ve, lo, hi, xs, wg, wu, wd)


def _final_body(*refs, seg_sizes):
    n_seg = len(seg_sizes)
    g_ref, o_ref = refs[4 * n_seg:]

    def run(x1_ref, y0_ref, y1_ref, rt_ref):
        rt = rt_ref[...]
        x = x1_ref[...] + (rt[:, 2:3] * y0_ref[...] + rt[:, 3:4] * y1_ref[...])
        o_ref[...] = x * lax.rsqrt(jnp.mean(x * x, axis=-1, keepdims=True) + RMS_EPS) * g_ref[...]

    _run_segment(pl.program_id(0), seg_sizes, run, [refs[4 * s:4 * s + 4] for s in range(n_seg)])


def _final_call(sources, firsts, seg_sizes, g, *, tm):
    d = sources[0][0].shape[1]
    n_tiles = sum(seg_sizes)
    in_specs, start = [], 0
    for size, first in zip(seg_sizes, firsts):
        src = lambda i, s=start, f=first, n=size: (f + jnp.clip(i - s, 0, n - 1), 0)
        in_specs += [pl.BlockSpec((tm, d), src)] * 3 + [pl.BlockSpec((tm, ROUTE_W), src)]
        start += size
    return pl.pallas_call(
        functools.partial(_final_body, seg_sizes=tuple(seg_sizes)),
        grid=(n_tiles,),
        in_specs=in_specs + [pl.BlockSpec((1, d), lambda i: (0, 0))],
        out_specs=pl.BlockSpec((tm, d), lambda i: (i, 0)),
        out_shape=jax.ShapeDtypeStruct((n_tiles * tm, d), F32),
        compiler_params=pltpu.CompilerParams(dimension_semantics=("arbitrary",), vmem_limit_bytes=VMEM_LIMIT),
        name="final",
    )(*[a for src in sources for a in src], g)


def _dispatch(flat_e, n, tb):
    n_assign = n * TOP_K_INNER
    idx_bits = (n_assign - 1).bit_length()
    assert N_EXPERTS << idx_bits < 2 ** 31 and n_assign % tb == 0
    ids = jnp.arange(n_assign, dtype=jnp.int32)
    sorted_key = jnp.sort((flat_e << idx_bits) | ids)
    sorted_id = sorted_key & ((1 << idx_bits) - 1)
    src_tok = sorted_id - n * (sorted_id // n)

    experts = jnp.arange(N_EXPERTS, dtype=jnp.int32)
    onehot = (flat_e[:, None] == experts[None, :]).astype(jnp.int32)
    csum = jnp.cumsum(onehot, axis=0)
    counts = csum[-1]
    ends = jnp.cumsum(counts)
    starts = ends - counts
    dest = (jnp.sum((csum - 1 + starts[None, :]) * onehot, axis=1)).astype(jnp.int32)

    n_blocks = n_assign // tb
    n_visits = n_blocks + N_EXPERTS - 1
    first_blk = starts // tb
    n_vis_e = jnp.where(counts > 0, (ends + tb - 1) // tb - first_blk, 0)
    v_end = jnp.cumsum(n_vis_e)
    v = jnp.arange(n_visits, dtype=jnp.int32)
    valid = v < v_end[-1]
    last_e = jnp.max(jnp.where(counts > 0, experts, 0))
    ve = jnp.where(valid, jnp.sum((v_end[None, :] <= v[:, None]).astype(jnp.int32), axis=1), last_e)
    pick = lambda a: jnp.sum(jnp.where(ve[:, None] == experts[None, :], a[None, :], 0), axis=1)
    vb = pick(first_blk) + v - pick(v_end - n_vis_e)
    lo = jnp.maximum(pick(starts), vb * tb) - vb * tb
    hi = jnp.minimum(pick(ends), (vb + 1) * tb) - vb * tb
    vb = jnp.where(valid, vb, n_blocks - 1)
    lo = jnp.where(valid, lo, 0)
    hi = jnp.where(valid, hi, 0)
    return dest, src_tok, (vb.astype(jnp.int32), ve.astype(jnp.int32), lo.astype(jnp.int32), hi.astype(jnp.int32))


def _q_perm():
    cols = []
    for g in range(B_GROUP):
        for kv in range(B_KV_HEADS):
            h = kv * B_GROUP + g
            cols.extend(range(h * HEAD_DIM, (h + 1) * HEAD_DIM))
    return jnp.array(cols, dtype=jnp.int32)


def _layer(x_prompt, x_sample, state_rwkv, state_shift, cache_win_k, cache_win_v, prm, norm_final_g, *,
           chunk, n_sub, moe_block):
    bp, tp, d = x_prompt.shape
    bs, ts, _ = x_sample.shape
    n_p, n_s = bp * tp, bs * ts
    ts_pad = 8
    tm = n_s
    assert tp % tm == 0 and tm % 8 == 0 and cache_win_k.shape[1] == WINDOW
    xp = x_prompt.reshape(n_p, d)
    xs = x_sample.reshape(n_s, d)

    qp = _q_perm()
    w_in = prm["w_in"]
    w_in = jnp.concatenate([w_in[:, :SHIFT_W], w_in[:, SHIFT_W:SHIFT_W + B_WIDTH][:, qp],
                            w_in[:, SHIFT_W + B_WIDTH:]], axis=1).astype(BF16)
    gn = prm["attn_norm_g"][qp][None]
    w_out = prm["w_out"]
    wa, wb = w_out[:A_WIDTH].astype(BF16), w_out[A_WIDTH:][qp].astype(BF16)
    pad_lanes = jnp.zeros((d, ROUTE_W - N_GROUPS - N_EXPERTS), F32)
    wr = jnp.concatenate([prm["w_route_group"], prm["w_route_expert"], pad_lanes], axis=1).astype(BF16)
    br = jnp.concatenate([prm["b_route_group"], prm["b_route_expert"], pad_lanes[0]])[None]

    zero_blk = jnp.zeros((LORA_W // 2, A_WIDTH), F32)
    wlora = jnp.concatenate([jnp.concatenate([prm["w_decay_up"], zero_blk], axis=1),
                             jnp.concatenate([zero_blk, prm["w_iclr_up"]], axis=1)], axis=0).astype(BF16)
    rp = dict(mu=prm["mu_shift"][None], w0=prm["w_decay0"][None], a0=prm["w_iclr0"][None], wlora=wlora,
              wgate=prm["w_gate_up"].astype(BF16), kk=prm["k_k"][None], ka=prm["k_a"][None],
              rk=prm["r_k"].reshape(1, A_WIDTH), lng=prm["lnx_g"][None], lnb=prm["lnx_b"][None])

    groups = [(0, bp)]
    tiles_per_seq = tp // tm
    nb = tp // WINDOW
    n_qb = _row_tile(nb, 4)
    kv4 = lambda a: a.reshape(a.shape[0], a.shape[1], B_KV_HEADS, HEAD_DIM)
    finals, s_p, sh_p, k_p, v_p = [], [], [], [], []
    for gi, (s0, s1) in enumerate(groups):
        has_sample = gi == len(groups) - 1
        n_seq = s1 - s0
        n_pt = n_seq * tiles_per_seq
        n_pg = n_pt * tm
        seg_sizes = [n_pt] + ([1] if has_sample else [])
        pa, q, k, v = _inproj_call([xp] + ([xs] if has_sample else []), [s0 * tiles_per_seq, 0], seg_sizes,
                                   prm["norm_mix_g"][None], w_in, tm=tm)

        oa_p, s_pg = _rwkv_call(pa, jnp.zeros((n_seq, 1, SHIFT_W), F32),
                                jnp.zeros((n_seq,) + state_rwkv.shape[1:], F32), rp, n_seq=n_seq,
                                n_chunks=tp // chunk, chunk=chunk, n_sub=n_sub, n_bat=1, t_real=chunk)
        ob_p, = _swa_call(prm["attn_sink"], q, k.reshape(-1, WINDOW, B_KV_WIDTH), v.reshape(-1, WINDOW, B_KV_WIDTH),
                          k, v, gn, n_seq=n_seq, n_blk=nb // n_qb, tq=WINDOW, n_bat=1, n_qb=n_qb,
                          first_has_prev=False, out_dtype=BF16,
                          prev_map=lambda b, i: (b * nb + jnp.maximum(i * n_qb - 1, 0), 0, 0))
        oa_list, ob_list = [oa_p], [ob_p]
        if has_sample:
            pad_rows = lambda a: jnp.pad(a[n_pg:].reshape(bs, ts, -1), ((0, 0), (0, ts_pad - ts), (0, 0))).reshape(
                bs * ts_pad, -1)
            oa_s, s_s = _rwkv_call(pad_rows(pa), state_shift[:, None, :], state_rwkv, rp, n_seq=bs,
                                   n_chunks=1, chunk=ts_pad, n_sub=1, n_bat=_row_tile(bs, 8), t_real=ts)
            ob_s, k_s, v_s = _swa_call(prm["attn_sink"], pad_rows(q).astype(F32),
                                       cache_win_k.reshape(bs, WINDOW, B_KV_WIDTH),
                                       cache_win_v.reshape(bs, WINDOW, B_KV_WIDTH), pad_rows(k), pad_rows(v), gn,
                                       n_seq=bs, n_blk=1, tq=ts_pad, n_bat=_row_tile(bs, 16), first_has_prev=True,
                                       prev_map=lambda b, i: (b, 0, 0), out_dtype=F32, cache_rows=ts)
            oa_list.append(oa_s.reshape(bs, ts_pad, A_WIDTH)[:, :ts].reshape(n_s, A_WIDTH))
            ob_list.append(ob_s.reshape(bs, ts_pad, B_WIDTH)[:, :ts].reshape(n_s, B_WIDTH))
            sh_s = pa[n_pg:].reshape(bs, ts, SHIFT_W)[:, -1]

        x1, h2, route, route_t = _outproj_call([xp] + ([xs] if has_sample else []), [s0 * tiles_per_seq, 0],
                                               oa_list, ob_list, seg_sizes, wa, wb, prm["norm_ffn_g"][None], wr, br,
                                               tm=tm)
        n = sum(seg_sizes) * tm
        route_t = route_t.reshape(n // tm, ROUTE_ROWS, tm)
        flat_e = jnp.concatenate([route_t[:, j, :].reshape(n) for j in range(TOP_K_INNER)]).astype(jnp.int32)
        dest, src_tok, visits = _dispatch(flat_e, n, moe_block)
        ybuf = _expert_call(visits, h2[src_tok], prm["w_exp_gate"], prm["w_exp_up"], prm["w_exp_down"], tb=moe_block)
        finals.append((x1, ybuf[dest[:n]], ybuf[dest[n:]], route))

        last_rows = lambda a, m: jnp.stack([a[(b + 1) * tp - m:(b + 1) * tp] for b in range(n_seq)], axis=0)
        s_p.append(s_pg)
        sh_p.append(last_rows(pa, 1)[:, 0])
        k_p.append(last_rows(k, WINDOW))
        v_p.append(last_rows(v, WINDOW))

    gf = norm_final_g[None]
    n_pts = [(s1 - s0) * tiles_per_seq for s0, s1 in groups]
    y_p = _final_call(finals, [0] * len(groups), n_pts, gf, tm=tm)
    y_s = _final_call(finals[-1:], n_pts[-1:], [1], gf, tm=tm)
    cat = lambda parts: jnp.concatenate(parts, axis=0)
    return (y_p.reshape(bp, tp, d), y_s.reshape(bs, ts, d), cat(s_p), cat(sh_p), kv4(cat(k_p)), kv4(cat(v_p)),
            s_s, sh_s, kv4(k_s), kv4(v_s))


def kernel(x_prompt, x_sample, state_rwkv, state_shift, cache_win_k, cache_win_v, norm_mix_g, w_in, mu_shift, w_decay0, w_decay_up, w_iclr0, w_iclr_up, w_gate_up, k_k, k_a, r_k, lnx_g, lnx_b, attn_sink, attn_norm_g, w_out, norm_ffn_g, w_route_group, b_route_group, w_route_expert, b_route_expert, w_exp_gate, w_exp_up, w_exp_down, norm_final_g):
    assert norm_mix_g.shape[0] == 1, "single-layer trunk"
    prm = dict(norm_mix_g=norm_mix_g[0], w_in=w_in[0], mu_shift=mu_shift[0], w_decay0=w_decay0[0],
               w_decay_up=w_decay_up[0], w_iclr0=w_iclr0[0], w_iclr_up=w_iclr_up[0], w_gate_up=w_gate_up[0],
               k_k=k_k[0], k_a=k_a[0], r_k=r_k[0], lnx_g=lnx_g[0], lnx_b=lnx_b[0], attn_sink=attn_sink[0],
               attn_norm_g=attn_norm_g[0], w_out=w_out[0], norm_ffn_g=norm_ffn_g[0],
               w_route_group=w_route_group[0], b_route_group=b_route_group[0],
               w_route_expert=w_route_expert[0], b_route_expert=b_route_expert[0],
               w_exp_gate=w_exp_gate[0], w_exp_up=w_exp_up[0], w_exp_down=w_exp_down[0])
    outs = _layer(x_prompt, x_sample, state_rwkv[0], state_shift[0], cache_win_k[0], cache_win_v[0], prm,
                  norm_final_g, chunk=64, n_sub=8, moe_block=512)
    y_p, y_s, s_p, sh_p, kp, vp, s_s, sh_s, ks, vs = outs
    return (y_p, y_s, s_p[None], sh_p[None], kp[None], vp[None], s_s[None], sh_s[None], ks[None], vs[None])
```

```python
import functools
import math

import jax
import jax.numpy as jnp
from jax import lax
from jax.experimental import pallas as pl
from jax.experimental.pallas import tpu as pltpu

F32 = jnp.float32
BF16 = jnp.bfloat16

HEAD_DIM = 64
A_HEADS = 8
A_WIDTH = A_HEADS * HEAD_DIM
B_HEADS = 8
B_KV_HEADS = 2
B_GROUP = B_HEADS // B_KV_HEADS
B_WIDTH = B_HEADS * HEAD_DIM
B_KV_WIDTH = B_KV_HEADS * HEAD_DIM
DECAY_LORA = 64
ICLR_LORA = 64
GATE_LORA = 128
LORA_W = DECAY_LORA + ICLR_LORA
SHIFT_W = 3 * A_WIDTH + LORA_W + GATE_LORA
IN_W = SHIFT_W + B_WIDTH + 2 * B_KV_WIDTH
WINDOW = 128
N_GROUPS = 4
EXPERTS_PER_GROUP = 4
N_EXPERTS = N_GROUPS * EXPERTS_PER_GROUP
TOP_K_INNER = 2
EXPERT_FF = 512
RMS_EPS = 1e-6
LNX_EPS = 64e-5
DECAY_OFFSET = 0.5

LANE = 128
PAIR = LANE // HEAD_DIM
N_PAIRS = A_HEADS // PAIR
ROUTE_W = LANE
ROUTE_ROWS = 8
MASK_NEG = -1e30
VMEM_LIMIT = 48 * 1024 * 1024

LOG2_E = math.log2(math.e)
DECAY_SCALE = math.exp(-DECAY_OFFSET)
HEAD_SHIFT = HEAD_DIM.bit_length() - 1
HI = lax.Precision.HIGHEST


def _row_tile(n, cap):
    t = cap
    while n % t:
        t //= 2
    return t


def _dot(a, b, precision=None):
    return jnp.dot(a, b, preferred_element_type=F32, precision=precision)


def _dot_nt(a, b, precision=None):
    return lax.dot_general(a, b, (((1,), (1,)), ((), ())), preferred_element_type=F32, precision=precision)


def _dot_tn(a, b, precision=None):
    return lax.dot_general(a, b, (((0,), (0,)), ((), ())), preferred_element_type=F32, precision=precision)


def _sigmoid(x):
    return 1.0 / (1.0 + jnp.exp(-x))


def _hi_lo(x, axis):
    hi = x.astype(BF16)
    lo = (x - hi.astype(F32)).astype(BF16)
    return jnp.concatenate([hi, lo], axis=axis)


def _run_segment(i, seg_sizes, run, seg_refs):
    start = 0
    for size, refs in zip(seg_sizes, seg_refs):
        pl.when((i >= start) & (i < start + size))(functools.partial(run, *refs))
        start += size


def _seg_specs(tm, width, seg_sizes, firsts):
    specs, start = [], 0
    for size, first in zip(seg_sizes, firsts):
        specs.append(pl.BlockSpec((tm, width),
                                  lambda i, s=start, f=first, n=size: (f + jnp.clip(i - s, 0, n - 1), 0)))
        start += size
    return specs


def _inproj_body(*refs, seg_sizes):
    n_seg = len(seg_sizes)
    g_ref, w_ref, pa_ref, q_ref, k_ref, v_ref = refs[n_seg:]

    def run(x_ref):
        x = x_ref[...]
        h = x * lax.rsqrt(jnp.mean(x * x, axis=-1, keepdims=True) + RMS_EPS) * g_ref[...]
        p = _dot(h.astype(BF16), w_ref[...])
        pa_ref[...] = p[:, :SHIFT_W]
        q_ref[...] = (p[:, SHIFT_W:SHIFT_W + B_WIDTH] * (HEAD_DIM ** -0.5)).astype(BF16)
        k_ref[...] = p[:, SHIFT_W + B_WIDTH:SHIFT_W + B_WIDTH + B_KV_WIDTH]
        v_ref[...] = p[:, SHIFT_W + B_WIDTH + B_KV_WIDTH:]

    _run_segment(pl.program_id(0), seg_sizes, run, [(r,) for r in refs[:n_seg]])


def _inproj_call(xs_list, firsts, seg_sizes, g, w_bf16, *, tm):
    d = xs_list[0].shape[1]
    n_tiles = sum(seg_sizes)
    n = n_tiles * tm
    row = lambda i: (i, 0)
    fixed = lambda i: (0, 0)
    return pl.pallas_call(
        functools.partial(_inproj_body, seg_sizes=tuple(seg_sizes)),
        grid=(n_tiles,),
        in_specs=_seg_specs(tm, d, seg_sizes, firsts) + [pl.BlockSpec((1, d), fixed), pl.BlockSpec((d, IN_W), fixed)],
        out_specs=[pl.BlockSpec((tm, SHIFT_W), row), pl.BlockSpec((tm, B_WIDTH), row),
                   pl.BlockSpec((tm, B_KV_WIDTH), row), pl.BlockSpec((tm, B_KV_WIDTH), row)],
        out_shape=[jax.ShapeDtypeStruct((n, SHIFT_W), F32), jax.ShapeDtypeStruct((n, B_WIDTH), BF16),
                   jax.ShapeDtypeStruct((n, B_KV_WIDTH), F32), jax.ShapeDtypeStruct((n, B_KV_WIDTH), F32)],
        compiler_params=pltpu.CompilerParams(dimension_semantics=("arbitrary",), vmem_limit_bytes=VMEM_LIMIT),
        name="inproj",
    )(*xs_list, g, w_bf16)


def _rwkv_body(pa_ref, shift0_ref, s0_ref, mu_ref, w0_ref, a0_ref, wlora_ref, wgate_ref, kk_ref, ka_ref,
               rk_ref, lng_ref, lnb_ref, oa_ref, sout_ref, s_scr, prev_scr, *, chunk, n_sub, n_bat, t_real):
    C = chunk
    seq_rows = n_sub * C
    n_seg = n_bat * n_sub
    rows = n_seg * C
    c = pl.program_id(1)

    @pl.when(c == 0)
    def _():
        prev_scr[...] = shift0_ref[...]
        for b in range(n_bat):
            for j in range(N_PAIRS):
                s_scr[b * N_PAIRS + j] = jnp.concatenate([s0_ref[b, PAIR * j + h] for h in range(PAIR)], axis=1)

    pa = pa_ref[...]
    row = lax.broadcasted_iota(jnp.int32, (rows, 1), 0)
    row_in_chunk = row & (C - 1)
    pa_prev = pltpu.roll(pa, 1, axis=0)
    for b in range(n_bat):
        pa_prev = jnp.where(row == b * seq_rows, prev_scr[b], pa_prev)
        last = (b + 1) * seq_rows - C + t_real - 1
        prev_scr[b] = pa[last:last + 1]
    xm = pa + mu_ref[...] * (pa_prev - pa)

    r = xm[:, :A_WIDTH]
    k = xm[:, A_WIDTH:2 * A_WIDTH]
    v = xm[:, 2 * A_WIDTH:3 * A_WIDTH]
    lora_in = xm[:, 3 * A_WIDTH:3 * A_WIDTH + LORA_W]
    gd = xm[:, 3 * A_WIDTH + LORA_W:]

    lane = lax.broadcasted_iota(jnp.int32, (1, LANE), 1)
    lo_half = lane < HEAD_DIM
    z = jnp.where(lo_half, jnp.tanh(lora_in), lora_in)
    lw = _dot(z.astype(BF16), wlora_ref[...])
    dec_pre = w0_ref[...] + lw[:, :A_WIDTH]
    a = _sigmoid(a0_ref[...] + lw[:, A_WIDTH:])
    logdec = (-LOG2_E * DECAY_SCALE) * _sigmoid(dec_pre)
    g = _dot(_sigmoid(gd).astype(BF16), wgate_ref[...])

    ri = lax.broadcasted_iota(jnp.int32, (LANE, LANE), 0)
    ci = lax.broadcasted_iota(jnp.int32, (LANE, LANE), 1)
    seg = jnp.where((ri >> HEAD_SHIFT) == (ci >> HEAD_SHIFT), 1.0, 0.0).astype(BF16)
    seg2 = jnp.concatenate([seg, seg], axis=0)

    def headsum(x):
        return jnp.concatenate([_dot(_hi_lo(x[:, j * LANE:(j + 1) * LANE], 1), seg2) for j in range(N_PAIRS)],
                               axis=1)

    kk = k * kk_ref[...]
    kk = kk * lax.rsqrt(jnp.maximum(headsum(kk * kk), 1e-24))
    k = k * (1.0 + (a - 1.0) * ka_ref[...])
    bonus = headsum(r * k * rk_ref[...]) * v

    if t_real < C:
        valid = row_in_chunk < t_real
        logdec = jnp.where(valid, logdec, 0.0)
        kk = jnp.where(valid, kk, 0.0)
        k = jnp.where(valid, k, 0.0)
        v = jnp.where(valid, v, 0.0)

    cum = logdec
    shift = 1
    while shift < C:
        cum = cum + jnp.where(row_in_chunk >= shift, pltpu.roll(cum, shift, axis=0), 0.0)
        shift *= 2
    ends = [cum[(s + 1) * C - 1:(s + 1) * C] for s in range(n_seg)]
    cum_end = jnp.concatenate([jnp.broadcast_to(e, (C, A_WIDTH)) for e in ends], axis=0) if n_seg > 1 else ends[0]
    w_incl = jnp.exp2(cum)
    w_prev = jnp.exp2(cum - logdec)
    w_inv = jnp.exp2(-cum)
    w_end = jnp.exp2(cum_end - cum)
    w_chunk = [jnp.exp2(e) for e in ends]
    kka = kk * a
    terms = dict(A=-kk * w_prev, R=r * w_incl, B=kka * w_inv, K=k * w_inv, V=v, Be=kka * w_end, Ke=k * w_end)

    C2 = PAIR * C
    t_idx = lax.broadcasted_iota(jnp.int32, (C, 1), 0)
    i_idx = lax.broadcasted_iota(jnp.int32, (1, C2), 1) & (C - 1)
    strict = i_idx < t_idx
    incl = i_idx <= t_idx
    ident = jnp.where(i_idx == t_idx, 1.0, 0.0).astype(F32)
    lo_time = lax.broadcasted_iota(jnp.int32, (1, C2), 1) < C
    n_levels = max(1, (C - 1).bit_length())
    bf = lambda x: x.astype(BF16)

    def bd(x, lo_mask=lo_half):
        return bf(jnp.concatenate([jnp.where(lo_mask, x, 0.0), jnp.where(lo_mask, 0.0, x)], axis=0))

    chains = [(s, j) for s in range(n_seg) for j in range(N_PAIRS)]
    tile = lambda name, s, j: terms[name][s * C:(s + 1) * C, j * LANE:(j + 1) * LANE]
    r_sbs = {ch: tile("R", *ch) for ch in chains}
    stk = {ch: {nm: bd(tile(nm, *ch)) for nm in ("A", "B", "K", "V", "Be", "Ke")} for ch in chains}
    m_ab, aak, arb, ark = {}, {}, {}, {}
    for ch in chains:
        t = stk[ch]
        a_sbs, rb = bf(tile("A", *ch)), bf(r_sbs[ch])
        if C2 % LANE == 0:
            m1 = _dot_nt(jnp.concatenate([a_sbs, rb], axis=0), jnp.concatenate([t["B"], t["K"]], axis=0))
            ab, ak, rbm, rk = m1[:C, :C2], m1[:C, C2:], m1[C:, :C2], m1[C:, C2:]
        else:
            ab, ak = _dot_nt(a_sbs, t["B"]), _dot_nt(a_sbs, t["K"])
            rbm, rk = _dot_nt(rb, t["B"]), _dot_nt(rb, t["K"])
        m_ab[ch] = jnp.where(strict, ab, 0.0)
        aak[ch] = jnp.where(strict, ak, 0.0)
        arb[ch] = bf(jnp.where(incl, rbm, 0.0))
        ark[ch] = jnp.where(incl, rk, 0.0)

    nn = dict(m_ab)
    tinv = {ch: ident + m_ab[ch] for ch in chains}
    for lvl in range(1, n_levels):
        if lvl == 1:
            for ch in chains:
                nn[ch] = _dot(bf(nn[ch]), bd(nn[ch], lo_time))
            continue
        for ch in chains:
            both = _dot(bf(jnp.concatenate([nn[ch], tinv[ch]], axis=0)), bd(nn[ch], lo_time))
            nn[ch] = both[:C]
            tinv[ch] = tinv[ch] + both[C:]
    if n_levels > 1:
        for ch in chains:
            tinv[ch] = tinv[ch] + _dot(bf(tinv[ch]), bd(nn[ch], lo_time))
    av = {ch: _dot(bf(jnp.concatenate([aak[ch], ark[ch]], axis=0)), stk[ch]["V"]) for ch in chains}
    x = {ch: _dot(bf(tinv[ch]), jnp.concatenate([stk[ch]["A"], bd(av[ch][:C])], axis=1)) for ch in chains}
    a_eff = {ch: bd(x[ch][:, :LANE]) for ch in chains}
    v_eff = {ch: bd(x[ch][:, LANE:]) for ch in chains}
    zed = {ch: _dot(arb[ch], jnp.concatenate([a_eff[ch], v_eff[ch]], axis=1)) for ch in chains}
    rhat = {ch: bf(r_sbs[ch] + zed[ch][:, :LANE]) for ch in chains}
    y0 = {ch: av[ch][C:] + zed[ch][:, LANE:] for ch in chains}
    p_mat = {ch: bf(_dot_tn(stk[ch]["Be"], a_eff[ch])) for ch in chains}
    q_bd = {ch: _dot_tn(jnp.concatenate([v_eff[ch], stk[ch]["V"]], axis=0),
                        jnp.concatenate([stk[ch]["Be"], stk[ch]["Ke"]], axis=0)) for ch in chains}

    y_rows = []
    for b in range(n_bat):
        state = [s_scr[b * N_PAIRS + j] for j in range(N_PAIRS)]
        for sub in range(n_sub):
            s = b * n_sub + sub
            ys = []
            for j in range(N_PAIRS):
                ch = (s, j)
                hi = bf(state[j])
                lo = bf(state[j] - hi.astype(F32))
                yy = _dot_nt(rhat[ch], jnp.concatenate([bd(hi), bd(lo)], axis=0))
                ys.append(yy[:, :LANE] + yy[:, LANE:] + y0[ch])
                sp = _dot_nt(jnp.concatenate([hi, lo], axis=0), p_mat[ch])
                q_sbs = q_bd[ch][:HEAD_DIM] + q_bd[ch][HEAD_DIM:]
                state[j] = state[j] * w_chunk[s][:, j * LANE:(j + 1) * LANE] + sp[:HEAD_DIM] + sp[HEAD_DIM:] + q_sbs
            y_rows.append(jnp.concatenate(ys, axis=1))
        for j in range(N_PAIRS):
            s_scr[b * N_PAIRS + j] = state[j]

    y = jnp.concatenate(y_rows, axis=0) if n_seg > 1 else y_rows[0]
    inv_n = 1.0 / HEAD_DIM
    mean = headsum(y) * inv_n
    d = y - mean
    var = headsum(d * d) * inv_n
    yn = d * lax.rsqrt(var + LNX_EPS) * lng_ref[...] + lnb_ref[...]
    oa_ref[...] = ((yn + bonus) * g).astype(oa_ref.dtype)

    @pl.when(c == pl.num_programs(1) - 1)
    def _():
        for b in range(n_bat):
            for j in range(N_PAIRS):
                for h in range(PAIR):
                    sout_ref[b, PAIR * j + h] = s_scr[b * N_PAIRS + j][:, h * HEAD_DIM:(h + 1) * HEAD_DIM]


def _rwkv_call(pa2d, shift0, s0_pairs, prm, *, n_seq, n_chunks, chunk, n_sub, n_bat, t_real):
    n_steps = n_chunks // n_sub
    assert n_seq % n_bat == 0 and (n_bat == 1 or n_steps == 1)
    row = lambda b, c: (b * n_steps + c, 0)
    seq3 = lambda b, c: (b, 0, 0)
    fixed = lambda b, c: (0, 0)
    vec = lambda w: pl.BlockSpec((1, w), fixed)
    body = functools.partial(_rwkv_body, chunk=chunk, n_sub=n_sub, n_bat=n_bat, t_real=t_real)
    n_rows = n_seq * n_chunks * chunk
    blk_rows = n_bat * n_sub * chunk
    state_blk = (n_bat, A_HEADS, HEAD_DIM, HEAD_DIM)
    seq4 = lambda b, c: (b, 0, 0, 0)
    return pl.pallas_call(
        body,
        grid=(n_seq // n_bat, n_steps),
        in_specs=[pl.BlockSpec((blk_rows, SHIFT_W), row),
                  pl.BlockSpec((n_bat, 1, SHIFT_W), seq3),
                  pl.BlockSpec(state_blk, seq4),
                  vec(SHIFT_W), vec(A_WIDTH), vec(A_WIDTH),
                  pl.BlockSpec((LORA_W, 2 * A_WIDTH), fixed), pl.BlockSpec((GATE_LORA, A_WIDTH), fixed),
                  vec(A_WIDTH), vec(A_WIDTH), vec(A_WIDTH), vec(A_WIDTH), vec(A_WIDTH)],
        out_specs=[pl.BlockSpec((blk_rows, A_WIDTH), row), pl.BlockSpec(state_blk, seq4)],
        out_shape=[jax.ShapeDtypeStruct((n_rows, A_WIDTH), BF16),
                   jax.ShapeDtypeStruct((n_seq, A_HEADS, HEAD_DIM, HEAD_DIM), F32)],
        scratch_shapes=[pltpu.VMEM((n_bat * N_PAIRS, HEAD_DIM, LANE), F32), pltpu.VMEM((n_bat, 1, SHIFT_W), F32)],
        compiler_params=pltpu.CompilerParams(dimension_semantics=("arbitrary", "arbitrary"),
                                             vmem_limit_bytes=VMEM_LIMIT),
        name="rwkv",
    )(pa2d, shift0, s0_pairs, prm["mu"], prm["w0"], prm["a0"], prm["wlora"], prm["wgate"], prm["kk"], prm["ka"],
      prm["rk"], prm["lng"], prm["lnb"])


def _swa_bias(tq):
    rows = B_GROUP * tq
    grp = jnp.arange(rows, dtype=jnp.int32)[:, None] // tq
    t = jnp.arange(rows, dtype=jnp.int32)[:, None] % tq
    dist_p = t + WINDOW - jnp.arange(WINDOW, dtype=jnp.int32)[None, :]
    dist_c = t - jnp.arange(tq, dtype=jnp.int32)[None, :]

    def bias(dist, kv):
        slope = sum(jnp.where(grp == g, 2.0 ** -(kv * B_GROUP + g + 1), 0.0) for g in range(B_GROUP))
        return jnp.where((dist >= 0) & (dist < WINDOW), -slope * dist.astype(F32), MASK_NEG)

    return (jnp.stack([bias(dist_p, kv) for kv in range(B_KV_HEADS)]),
            jnp.stack([bias(dist_c, kv) for kv in range(B_KV_HEADS)]))


def _swa_body(sink_ref, q_ref, kp_ref, vp_ref, kc_ref, vc_ref, bp_ref, bc_ref, gn_ref, o_ref, *cache_refs, tq, n_bat,
              n_qb, t_real, first_has_prev):
    nblk = pl.program_id(1)
    lane = lax.broadcasted_iota(jnp.int32, (1, LANE), 1)
    kv_masks = [(lane < HEAD_DIM) if kv == 0 else (lane >= HEAD_DIM) for kv in range(B_KV_HEADS)]
    blocks = []
    for s in range(n_bat * n_qb):
        rs = slice(s * tq, (s + 1) * tq)
        kc, vc = kc_ref[rs], vc_ref[rs]
        if n_qb > 1 and s > 0:
            ps = slice((s - 1) * tq, s * tq)
            kp, vp, has_prev = kc_ref[ps], vc_ref[ps], True
        else:
            kp, vp, has_prev = kp_ref[s], vp_ref[s], first_has_prev
        if cache_refs:
            row = lax.broadcasted_iota(jnp.int32, (WINDOW, 1), 0)
            for out_ref, old, new in zip(cache_refs, (kp, vp), (kc, vc)):
                new_tail = jnp.concatenate([pltpu.roll(new, tq - t_real, axis=0)] * (WINDOW // tq), axis=0)
                out_ref[s] = jnp.where(row >= WINDOW - t_real, new_tail, pltpu.roll(old, WINDOW - t_real, axis=0))
        pen = 0.0 if has_prev else jnp.where(nblk > 0, 0.0, MASK_NEG)
        blocks.append(dict(rs=rs, q=q_ref[rs], kp=kp.astype(BF16), kc=kc.astype(BF16), vp=vp, vc=vc, pen=pen))

    keys = [(b, kv) for b in range(len(blocks)) for kv in range(B_KV_HEADS)]
    s_p, s_c, vpm, vcm = {}, {}, {}, {}
    for b, kv in keys:
        blk, mk = blocks[b], kv_masks[kv]
        q_st = jnp.concatenate([jnp.where(mk, blk["q"][:, g * LANE:(g + 1) * LANE], 0.0) for g in range(B_GROUP)],
                               axis=0).astype(BF16)
        s_p[b, kv] = _dot_nt(q_st, blk["kp"])
        s_c[b, kv] = _dot_nt(q_st, blk["kc"])
        vpm[b, kv] = jnp.where(mk, blk["vp"], 0.0).astype(BF16)
        vcm[b, kv] = jnp.where(mk, blk["vc"], 0.0).astype(BF16)

    slabs = [(b, kv, g) for b, kv in keys for g in range(B_GROUP)]
    e_p, e_c, inv = {}, {}, {}
    for b, kv, g in slabs:
        rs = slice(g * tq, (g + 1) * tq)
        sink = sink_ref[kv * B_GROUP + g]
        sp = s_p[b, kv][rs] + (bp_ref[kv, rs, :] + blocks[b]["pen"])
        sc = s_c[b, kv][rs] + bc_ref[kv, rs, :]
        if tq == WINDOW:
            m = jnp.maximum(jnp.max(jnp.maximum(sp, sc), axis=-1, keepdims=True), sink)
        else:
            m = jnp.maximum(jnp.maximum(jnp.max(sp, axis=-1, keepdims=True), jnp.max(sc, axis=-1, keepdims=True)),
                            sink)
        ep, ec = jnp.exp(sp - m), jnp.exp(sc - m)
        if tq == WINDOW:
            e_sum = jnp.sum(ep + ec, axis=-1, keepdims=True)
        else:
            e_sum = jnp.sum(ep, axis=-1, keepdims=True) + jnp.sum(ec, axis=-1, keepdims=True)
        e_p[b, kv, g], e_c[b, kv, g] = ep.astype(BF16), ec.astype(BF16)
        inv[b, kv, g] = 1.0 / (e_sum + jnp.exp(sink - m))

    tiles = {}
    for b, kv, g in slabs:
        o = (_dot(e_p[b, kv, g], vpm[b, kv]) + _dot(e_c[b, kv, g], vcm[b, kv])) * inv[b, kv, g]
        tiles[b, g] = o if kv == 0 else tiles[b, g] + o

    for b, blk in enumerate(blocks):
        ssq = sum(jnp.sum(tiles[b, g] * tiles[b, g], axis=-1, keepdims=True) for g in range(B_GROUP))
        inv_rms = lax.rsqrt(ssq * (1.0 / B_WIDTH) + RMS_EPS)
        out = jnp.concatenate([tiles[b, g] for g in range(B_GROUP)], axis=1) * inv_rms * gn_ref[...]
        o_ref[blk["rs"]] = out.astype(o_ref.dtype)


def _swa_call(sink, q2d, kprev, vprev, kcur2d, vcur2d, gn, *, n_seq, n_blk, tq, n_bat, first_has_prev, prev_map,
              out_dtype, n_qb=1, cache_rows=0):
    assert n_seq % n_bat == 0 and (n_bat == 1 or n_blk * n_qb == 1) and (cache_rows == 0 or n_blk * n_qb == 1)
    row = lambda b, n: (b * n_blk + n, 0)
    body = functools.partial(_swa_body, tq=tq, n_bat=n_bat, n_qb=n_qb, t_real=cache_rows,
                             first_has_prev=first_has_prev)
    bias_p, bias_c = _swa_bias(tq)
    whole = lambda b, n: (0, 0, 0)
    blk_rows = n_bat * n_qb * tq
    out_specs = [pl.BlockSpec((blk_rows, B_WIDTH), row)]
    out_shape = [jax.ShapeDtypeStruct((n_seq * n_blk * n_qb * tq, B_WIDTH), out_dtype)]
    if cache_rows:
        out_specs += [pl.BlockSpec((n_bat, WINDOW, B_KV_WIDTH), prev_map)] * 2
        out_shape += [jax.ShapeDtypeStruct((n_seq, WINDOW, B_KV_WIDTH), F32)] * 2
    return pl.pallas_call(
        body,
        grid=(n_seq // n_bat, n_blk),
        in_specs=[pl.BlockSpec(memory_space=pltpu.SMEM),
                  pl.BlockSpec((blk_rows, B_WIDTH), row),
                  pl.BlockSpec((n_bat, WINDOW, B_KV_WIDTH), prev_map),
                  pl.BlockSpec((n_bat, WINDOW, B_KV_WIDTH), prev_map),
                  pl.BlockSpec((blk_rows, B_KV_WIDTH), row),
                  pl.BlockSpec((blk_rows, B_KV_WIDTH), row),
                  pl.BlockSpec(bias_p.shape, whole),
                  pl.BlockSpec(bias_c.shape, whole),
                  pl.BlockSpec((1, B_WIDTH), lambda b, n: (0, 0))],
        out_specs=out_specs,
        out_shape=out_shape,
        compiler_params=pltpu.CompilerParams(dimension_semantics=("arbitrary", "arbitrary"),
                                             vmem_limit_bytes=VMEM_LIMIT),
        name="swa",
    )(sink, q2d, kprev, vprev, kcur2d, vcur2d, bias_p, bias_c, gn)


def _route_rows(lg):
    lane = lax.broadcasted_iota(jnp.int32, (1, ROUTE_W), 1)
    lane_f = lane.astype(F32)
    no_lane = float(ROUTE_W)
    is_group = lane < N_GROUPS
    m_g = jnp.max(jnp.where(is_group, lg, MASK_NEG), axis=-1, keepdims=True)
    g_idx = jnp.min(jnp.where(is_group & (lg == m_g), lane_f, no_lane), axis=-1, keepdims=True)
    p_group = 1.0 / jnp.sum(jnp.where(is_group, jnp.exp(lg - m_g), 0.0), axis=-1, keepdims=True)
    e_lane = lane - N_GROUPS
    in_group = (e_lane >= 0) & (e_lane < N_EXPERTS) & ((e_lane >> 2).astype(F32) == g_idx)
    m_1 = jnp.max(jnp.where(in_group, lg, MASK_NEG), axis=-1, keepdims=True)
    i_1 = jnp.min(jnp.where(in_group & (lg == m_1), lane_f, no_lane), axis=-1, keepdims=True)
    rest = in_group & (lane_f != i_1)
    m_2 = jnp.max(jnp.where(rest, lg, MASK_NEG), axis=-1, keepdims=True)
    i_2 = jnp.min(jnp.where(rest & (lg == m_2), lane_f, no_lane), axis=-1, keepdims=True)
    ratio = jnp.exp(m_2 - m_1)
    w_1 = p_group / (1.0 + ratio)
    return jnp.where(lane == 0, i_1 - N_GROUPS,
                     jnp.where(lane == 1, i_2 - N_GROUPS,
                               jnp.where(lane == 2, w_1, jnp.where(lane == 3, w_1 * ratio, 0.0))))


def _outproj_body(*refs, seg_sizes):
    n_seg = len(seg_sizes)
    x_refs, oa_refs, ob_refs = refs[:n_seg], refs[n_seg:2 * n_seg], refs[2 * n_seg:3 * n_seg]
    wa_ref, wb_ref, g_ref, wr_ref, br_ref, x1_ref, h2_ref, rt_ref, rtt_ref, x1_scr = refs[3 * n_seg:]
    i = pl.program_id(0)

    @pl.when(i == 0)
    def _():
        x1_scr[1] = jnp.zeros(x1_scr.shape[1:], F32)

    def run(x_ref, oa_ref, ob_ref):
        x1 = x_ref[...] + _dot(oa_ref[...].astype(BF16), wa_ref[...]) + _dot(ob_ref[...].astype(BF16), wb_ref[...])
        x1_ref[...] = x1
        x1_scr[i % 2] = x1

        prev = x1_scr[(i + 1) % 2]
        h2 = prev * lax.rsqrt(jnp.mean(prev * prev, axis=-1, keepdims=True) + RMS_EPS) * g_ref[...]
        h2b = h2.astype(BF16)
        h2_ref[...] = h2b
        lg = _dot(h2b, wr_ref[...]) + br_ref[...]
        rt = jnp.concatenate([_route_rows(lg[s:s + LANE]) for s in range(0, lg.shape[0], LANE)], axis=0)
        rt_ref[...] = rt
        rtt_ref[...] = rt.T[:ROUTE_ROWS]

    start = 0
    for s, size in enumerate(seg_sizes):
        in_seg = (i >= start) if s == n_seg - 1 else ((i >= start) & (i < start + size))
        pl.when(in_seg)(functools.partial(run, x_refs[s], oa_refs[s], ob_refs[s]))
        start += size


def _outproj_call(x_list, x_firsts, oa_list, ob_list, seg_sizes, wa, wb, g, wr, br, *, tm):
    d = x_list[0].shape[1]
    n_tiles = sum(seg_sizes)
    n = n_tiles * tm
    cur = lambda i: (jnp.minimum(i, n_tiles - 1), 0)
    prev = lambda i: (jnp.maximum(i - 1, 0), 0)
    fixed = lambda i: (0, 0)
    zeros = [0] * len(seg_sizes)
    return pl.pallas_call(
        functools.partial(_outproj_body, seg_sizes=tuple(seg_sizes)),
        grid=(n_tiles + 1,),
        in_specs=(_seg_specs(tm, d, seg_sizes, x_firsts) + _seg_specs(tm, A_WIDTH, seg_sizes, zeros)
                  + _seg_specs(tm, B_WIDTH, seg_sizes, zeros)
                  + [pl.BlockSpec((A_WIDTH, d), fixed), pl.BlockSpec((B_WIDTH, d), fixed), pl.BlockSpec((1, d), fixed),
                     pl.BlockSpec((d, ROUTE_W), fixed), pl.BlockSpec((1, ROUTE_W), fixed)]),
        out_specs=[pl.BlockSpec((tm, d), cur), pl.BlockSpec((tm, d), prev), pl.BlockSpec((tm, ROUTE_W), prev),
                   pl.BlockSpec((ROUTE_ROWS, tm), prev)],
        out_shape=[jax.ShapeDtypeStruct((n, d), F32), jax.ShapeDtypeStruct((n, d), BF16),
                   jax.ShapeDtypeStruct((n, ROUTE_W), F32),
                   jax.ShapeDtypeStruct((n_tiles * ROUTE_ROWS, tm), F32)],
        scratch_shapes=[pltpu.VMEM((2, tm, d), F32)],
        compiler_params=pltpu.CompilerParams(dimension_semantics=("arbitrary",), vmem_limit_bytes=VMEM_LIMIT),
        name="outproj",
    )(*x_list, *oa_list, *ob_list, wa, wb, g, wr, br)


def _expert_body(vb_ref, ve_ref, lo_ref, hi_ref, x_ref, wg_ref, wu_ref, wd_ref, y_ref, wg_bf, wu_bf, wd_bf):
    v = pl.program_id(0)
    pv = jnp.maximum(v - 1, 0)
    lo, hi = lo_ref[v], hi_ref[v]
    first_visit = (v == 0) | (vb_ref[v] != vb_ref[pv])

    @pl.when((v == 0) | (ve_ref[v] != ve_ref[pv]))
    def _():
        wg_bf[...] = wg_ref[0].astype(BF16)
        wu_bf[...] = wu_ref[0].astype(BF16)
        wd_bf[...] = wd_ref[0].astype(BF16)

    @pl.when(hi > lo)
    def _():
        x = x_ref[...]
        gate = _dot(x, wg_bf[...])
        up = _dot(x, wu_bf[...])
        mid = (gate * _sigmoid(gate) * up).astype(BF16)
        y = _dot(mid, wd_bf[...])
        row = lax.broadcasted_iota(jnp.int32, (y.shape[0], 1), 0)
        mine = (row >= lo) & (row < hi)

        @pl.when(first_visit)
        def _():
            y_ref[...] = jnp.where(mine, y, 0.0)

        @pl.when(jnp.logical_not(first_visit))
        def _():
            y_ref[...] = jnp.where(mine, y, y_ref[...])


def _expert_call(visits, xs, wg, wu, wd, *, tb):
    n_rows, d = xs.shape
    ff = wg.shape[-1]
    vb, ve, lo, hi = visits
    blk = lambda v, vb, ve, lo, hi: (vb[v], 0)
    wsel = lambda v, vb, ve, lo, hi: (ve[v], 0, 0)
    grid_spec = pltpu.PrefetchScalarGridSpec(
        num_scalar_prefetch=4,
        grid=(vb.shape[0],),
        in_specs=[pl.BlockSpec((tb, d), blk), pl.BlockSpec((1, d, ff), wsel), pl.BlockSpec((1, d, ff), wsel),
                  pl.BlockSpec((1, ff, d), wsel)],
        out_specs=pl.BlockSpec((tb, d), blk),
        scratch_shapes=[pltpu.VMEM((d, ff), BF16), pltpu.VMEM((d, ff), BF16), pltpu.VMEM((ff, d), BF16)],
    )
    return pl.pallas_call(
        _expert_body,
        grid_spec=grid_spec,
        out_shape=jax.ShapeDtypeStruct((n_rows, d), F32),
        compiler_params=pltpu.CompilerParams(dimension_semantics=("arbitrary",), vmem_limit_bytes=VMEM_LIMIT),
        name="experts",
    )(vb, ve, lo, hi, xs, wg, wu, wd)


def _final_body(*refs, seg_sizes):
    n_seg = len(seg_sizes)
    g_ref, o_ref = refs[4 * n_seg:]

    def run(x1_ref, y0_ref, y1_ref, rt_ref):
        rt = rt_ref[...]
        x = x1_ref[...] + (rt[:, 2:3] * y0_ref[...] + rt[:, 3:4] * y1_ref[...])
        o_ref[...] = x * lax.rsqrt(jnp.mean(x * x, axis=-1, keepdims=True) + RMS_EPS) * g_ref[...]

    _run_segment(pl.program_id(0), seg_sizes, run, [refs[4 * s:4 * s + 4] for s in range(n_seg)])


def _final_call(sources, firsts, seg_sizes, g, *, tm):
    d = sources[0][0].shape[1]
    n_tiles = sum(seg_sizes)
    in_specs, start = [], 0
    for size, first in zip(seg_sizes, firsts):
        src = lambda i, s=start, f=first, n=size: (f + jnp.clip(i - s, 0, n - 1), 0)
        in_specs += [pl.BlockSpec((tm, d), src)] * 3 + [pl.BlockSpec((tm, ROUTE_W), src)]
        start += size
    return pl.pallas_call(
        functools.partial(_final_body, seg_sizes=tuple(seg_sizes)),
        grid=(n_tiles,),
        in_specs=in_specs + [pl.BlockSpec((1, d), lambda i: (0, 0))],
        out_specs=pl.BlockSpec((tm, d), lambda i: (i, 0)),
        out_shape=jax.ShapeDtypeStruct((n_tiles * tm, d), F32),
        compiler_params=pltpu.CompilerParams(dimension_semantics=("arbitrary",), vmem_limit_bytes=VMEM_LIMIT),
        name="final",
    )(*[a for src in sources for a in src], g)


def _dispatch(flat_e, n, tb):
    n_assign = n * TOP_K_INNER
    idx_bits = (n_assign - 1).bit_length()
    assert N_EXPERTS << idx_bits < 2 ** 31 and n_assign % tb == 0
    ids = jnp.arange(n_assign, dtype=jnp.int32)
    sorted_key = jnp.sort((flat_e << idx_bits) | ids)
    sorted_id = sorted_key & ((1 << idx_bits) - 1)
    src_tok = sorted_id - n * (sorted_id // n)

    experts = jnp.arange(N_EXPERTS, dtype=jnp.int32)
    onehot = (flat_e[:, None] == experts[None, :]).astype(jnp.int32)
    csum = jnp.cumsum(onehot, axis=0)
    counts = csum[-1]
    ends = jnp.cumsum(counts)
    starts = ends - counts
    dest = (jnp.sum((csum - 1 + starts[None, :]) * onehot, axis=1)).astype(jnp.int32)

    n_blocks = n_assign // tb
    n_visits = n_blocks + N_EXPERTS - 1
    first_blk = starts // tb
    n_vis_e = jnp.where(counts > 0, (ends + tb - 1) // tb - first_blk, 0)
    v_end = jnp.cumsum(n_vis_e)
    v = jnp.arange(n_visits, dtype=jnp.int32)
    valid = v < v_end[-1]
    last_e = jnp.max(jnp.where(counts > 0, experts, 0))
    ve = jnp.where(valid, jnp.sum((v_end[None, :] <= v[:, None]).astype(jnp.int32), axis=1), last_e)
    pick = lambda a: jnp.sum(jnp.where(ve[:, None] == experts[None, :], a[None, :], 0), axis=1)
    vb = pick(first_blk) + v - pick(v_end - n_vis_e)
    lo = jnp.maximum(pick(starts), vb * tb) - vb * tb
    hi = jnp.minimum(pick(ends), (vb + 1) * tb) - vb * tb
    vb = jnp.where(valid, vb, n_blocks - 1)
    lo = jnp.where(valid, lo, 0)
    hi = jnp.where(valid, hi, 0)
    return dest, src_tok, (vb.astype(jnp.int32), ve.astype(jnp.int32), lo.astype(jnp.int32), hi.astype(jnp.int32))


def _q_perm():
    cols = []
    for g in range(B_GROUP):
        for kv in range(B_KV_HEADS):
            h = kv * B_GROUP + g
            cols.extend(range(h * HEAD_DIM, (h + 1) * HEAD_DIM))
    return jnp.array(cols, dtype=jnp.int32)


def _layer(x_prompt, x_sample, state_rwkv, state_shift, cache_win_k, cache_win_v, prm, norm_final_g, *,
           chunk, n_sub, moe_block):
    bp, tp, d = x_prompt.shape
    bs, ts, _ = x_sample.shape
    n_p, n_s = bp * tp, bs * ts
    ts_pad = 8
    tm = n_s
    assert tp % tm == 0 and tm % 8 == 0 and cache_win_k.shape[1] == WINDOW
    xp = x_prompt.reshape(n_p, d)
    xs = x_sample.reshape(n_s, d)

    qp = _q_perm()
    w_in = prm["w_in"]
    w_in = jnp.concatenate([w_in[:, :SHIFT_W], w_in[:, SHIFT_W:SHIFT_W + B_WIDTH][:, qp],
                            w_in[:, SHIFT_W + B_WIDTH:]], axis=1).astype(BF16)
    gn = prm["attn_norm_g"][qp][None]
    w_out = prm["w_out"]
    wa, wb = w_out[:A_WIDTH].astype(BF16), w_out[A_WIDTH:][qp].astype(BF16)
    pad_lanes = jnp.zeros((d, ROUTE_W - N_GROUPS - N_EXPERTS), F32)
    wr = jnp.concatenate([prm["w_route_group"], prm["w_route_expert"], pad_lanes], axis=1).astype(BF16)
    br = jnp.concatenate([prm["b_route_group"], prm["b_route_expert"], pad_lanes[0]])[None]

    zero_blk = jnp.zeros((LORA_W // 2, A_WIDTH), F32)
    wlora = jnp.concatenate([jnp.concatenate([prm["w_decay_up"], zero_blk], axis=1),
                             jnp.concatenate([zero_blk, prm["w_iclr_up"]], axis=1)], axis=0).astype(BF16)
    rp = dict(mu=prm["mu_shift"][None], w0=prm["w_decay0"][None], a0=prm["w_iclr0"][None], wlora=wlora,
              wgate=prm["w_gate_up"].astype(BF16), kk=prm["k_k"][None], ka=prm["k_a"][None],
              rk=prm["r_k"].reshape(1, A_WIDTH), lng=prm["lnx_g"][None], lnb=prm["lnx_b"][None])

    groups = [(0, bp)]
    tiles_per_seq = tp // tm
    nb = tp // WINDOW
    n_qb = _row_tile(nb, 8)
    kv4 = lambda a: a.reshape(a.shape[0], a.shape[1], B_KV_HEADS, HEAD_DIM)
    finals, s_p, sh_p, k_p, v_p = [], [], [], [], []
    for gi, (s0, s1) in enumerate(groups):
        has_sample = gi == len(groups) - 1
        n_seq = s1 - s0
        n_pt = n_seq * tiles_per_seq
        n_pg = n_pt * tm
        seg_sizes = [n_pt] + ([1] if has_sample else [])
        pa, q, k, v = _inproj_call([xp] + ([xs] if has_sample else []), [s0 * tiles_per_seq, 0], seg_sizes,
                                   prm["norm_mix_g"][None], w_in, tm=tm)

        oa_p, s_pg = _rwkv_call(pa, jnp.zeros((n_seq, 1, SHIFT_W), F32),
                                jnp.zeros((n_seq,) + state_rwkv.shape[1:], F32), rp, n_seq=n_seq,
                                n_chunks=tp // chunk, chunk=chunk, n_sub=n_sub, n_bat=1, t_real=chunk)
        ob_p, = _swa_call(prm["attn_sink"], q, k.reshape(-1, WINDOW, B_KV_WIDTH), v.reshape(-1, WINDOW, B_KV_WIDTH),
                          k, v, gn, n_seq=n_seq, n_blk=nb // n_qb, tq=WINDOW, n_bat=1, n_qb=n_qb,
                          first_has_prev=False, out_dtype=BF16,
                          prev_map=lambda b, i: (b * nb + jnp.maximum(i * n_qb - 1, 0), 0, 0))
        oa_list, ob_list = [oa_p], [ob_p]
        if has_sample:
            pad_rows = lambda a: jnp.pad(a[n_pg:].reshape(bs, ts, -1), ((0, 0), (0, ts_pad - ts), (0, 0))).reshape(
                bs * ts_pad, -1)
            oa_s, s_s = _rwkv_call(pad_rows(pa), state_shift[:, None, :], state_rwkv, rp, n_seq=bs,
                                   n_chunks=1, chunk=ts_pad, n_sub=1, n_bat=_row_tile(bs, 16), t_real=ts)
            ob_s, k_s, v_s = _swa_call(prm["attn_sink"], pad_rows(q).astype(F32),
                                       cache_win_k.reshape(bs, WINDOW, B_KV_WIDTH),
                                       cache_win_v.reshape(bs, WINDOW, B_KV_WIDTH), pad_rows(k), pad_rows(v), gn,
                                       n_seq=bs, n_blk=1, tq=ts_pad, n_bat=_row_tile(bs, 16), first_has_prev=True,
                                       prev_map=lambda b, i: (b, 0, 0), out_dtype=F32, cache_rows=ts)
            oa_list.append(oa_s.reshape(bs, ts_pad, A_WIDTH)[:, :ts].reshape(n_s, A_WIDTH))
            ob_list.append(ob_s.reshape(bs, ts_pad, B_WIDTH)[:, :ts].reshape(n_s, B_WIDTH))
            sh_s = pa[n_pg:].reshape(bs, ts, SHIFT_W)[:, -1]

        x1, h2, route, route_t = _outproj_call([xp] + ([xs] if has_sample else []), [s0 * tiles_per_seq, 0],
                                               oa_list, ob_list, seg_sizes, wa, wb, prm["norm_ffn_g"][None], wr, br,
                                               tm=tm)
        n = sum(seg_sizes) * tm
        route_t = route_t.reshape(n // tm, ROUTE_ROWS, tm)
        flat_e = jnp.concatenate([route_t[:, j, :].reshape(n) for j in range(TOP_K_INNER)]).astype(jnp.int32)
        dest, src_tok, visits = _dispatch(flat_e, n, moe_block)
        ybuf = _expert_call(visits, h2[src_tok], prm["w_exp_gate"], prm["w_exp_up"], prm["w_exp_down"], tb=moe_block)
        finals.append((x1, ybuf[dest[:n]], ybuf[dest[n:]], route))

        last_rows = lambda a, m: jnp.stack([a[(b + 1) * tp - m:(b + 1) * tp] for b in range(n_seq)], axis=0)
        s_p.append(s_pg)
        sh_p.append(last_rows(pa, 1)[:, 0])
        k_p.append(last_rows(k, WINDOW))
        v_p.append(last_rows(v, WINDOW))

    gf = norm_final_g[None]
    n_pts = [(s1 - s0) * tiles_per_seq for s0, s1 in groups]
    y_p = _final_call(finals, [0] * len(groups), n_pts, gf, tm=tm)
    y_s = _final_call(finals[-1:], n_pts[-1:], [1], gf, tm=tm)
    cat = lambda parts: jnp.concatenate(parts, axis=0)
    return (y_p.reshape(bp, tp, d), y_s.reshape(bs, ts, d), cat(s_p), cat(sh_p), kv4(cat(k_p)), kv4(cat(v_p)),
            s_s, sh_s, kv4(k_s), kv4(v_s))


def kernel(x_prompt, x_sample, state_rwkv, state_shift, cache_win_k, cache_win_v, norm_mix_g, w_in, mu_shift, w_decay0, w_decay_up, w_iclr0, w_iclr_up, w_gate_up, k_k, k_a, r_k, lnx_g, lnx_b, attn_sink, attn_norm_g, w_out, norm_ffn_g, w_route_group, b_route_group, w_route_expert, b_route_expert, w_exp_gate, w_exp_up, w_exp_down, norm_final_g):
    assert norm_mix_g.shape[0] == 1, "single-layer trunk"
    prm = dict(norm_mix_g=norm_mix_g[0], w_in=w_in[0], mu_shift=mu_shift[0], w_decay0=w_decay0[0],
               w_decay_up=w_decay_up[0], w_iclr0=w_iclr0[0], w_iclr_up=w_iclr_up[0], w_gate_up=w_gate_up[0],
               k_k=k_k[0], k_a=k_a[0], r_k=r_k[0], lnx_g=lnx_g[0], lnx_b=lnx_b[0], attn_sink=attn_sink[0],
               attn_norm_g=attn_norm_g[0], w_out=w_out[0], norm_ffn_g=norm_ffn_g[0],
               w_route_group=w_route_group[0], b_route_group=b_route_group[0],
               w_route_expert=w_route_expert[0], b_route_expert=b_route_expert[0],
               w_exp_gate=w_exp_gate[0], w_exp_up=w_exp_up[0], w_exp_down=w_exp_down[0])
    outs = _layer(x_prompt, x_sample, state_rwkv[0], state_shift[0], cache_win_k[0], cache_win_v[0], prm,
                  norm_final_g, chunk=64, n_sub=8, moe_block=512)
    y_p, y_s, s_p, sh_p, kp, vp, s_s, sh_s, ks, vs = outs
    return (y_p, y_s, s_p[None], sh_p[None], kp[None], vp[None], s_s[None], sh_s[None], ks[None], vs[None])
```

```python
import functools
import math

import jax
import jax.numpy as jnp
from jax import lax
from jax.experimental import pallas as pl
from jax.experimental.pallas import tpu as pltpu

F32 = jnp.float32
BF16 = jnp.bfloat16

HEAD_DIM = 64
A_HEADS = 8
A_WIDTH = A_HEADS * HEAD_DIM
B_HEADS = 8
B_KV_HEADS = 2
B_GROUP = B_HEADS // B_KV_HEADS
B_WIDTH = B_HEADS * HEAD_DIM
B_KV_WIDTH = B_KV_HEADS * HEAD_DIM
DECAY_LORA = 64
ICLR_LORA = 64
GATE_LORA = 128
LORA_W = DECAY_LORA + ICLR_LORA
SHIFT_W = 3 * A_WIDTH + LORA_W + GATE_LORA
IN_W = SHIFT_W + B_WIDTH + 2 * B_KV_WIDTH
WINDOW = 128
N_GROUPS = 4
EXPERTS_PER_GROUP = 4
N_EXPERTS = N_GROUPS * EXPERTS_PER_GROUP
TOP_K_INNER = 2
EXPERT_FF = 512
RMS_EPS = 1e-6
LNX_EPS = 64e-5
DECAY_OFFSET = 0.5

LANE = 128
PAIR = LANE // HEAD_DIM
N_PAIRS = A_HEADS // PAIR
ROUTE_W = LANE
ROUTE_ROWS = 8
LOGIT_ROWS = 24
MASK_NEG = -1e30
VMEM_LIMIT = 48 * 1024 * 1024

LOG2_E = math.log2(math.e)
DECAY_SCALE = math.exp(-DECAY_OFFSET)
HEAD_SHIFT = HEAD_DIM.bit_length() - 1
HI = lax.Precision.HIGHEST


def _row_tile(n, cap):
    t = cap
    while n % t:
        t //= 2
    return t


def _dot(a, b, precision=None):
    return jnp.dot(a, b, preferred_element_type=F32, precision=precision)


def _dot_nt(a, b, precision=None):
    return lax.dot_general(a, b, (((1,), (1,)), ((), ())), preferred_element_type=F32, precision=precision)


def _dot_tn(a, b, precision=None):
    return lax.dot_general(a, b, (((0,), (0,)), ((), ())), preferred_element_type=F32, precision=precision)


def _sigmoid(x):
    return 1.0 / (1.0 + jnp.exp(-x))


def _hi_lo(x, axis):
    hi = x.astype(BF16)
    lo = (x - hi.astype(F32)).astype(BF16)
    return jnp.concatenate([hi, lo], axis=axis)


def _run_segment(i, seg_sizes, run, seg_refs):
    start = 0
    for size, refs in zip(seg_sizes, seg_refs):
        pl.when((i >= start) & (i < start + size))(functools.partial(run, *refs))
        start += size


def _seg_specs(tm, width, seg_sizes, firsts):
    specs, start = [], 0
    for size, first in zip(seg_sizes, firsts):
        specs.append(pl.BlockSpec((tm, width),
                                  lambda i, s=start, f=first, n=size: (f + jnp.clip(i - s, 0, n - 1), 0)))
        start += size
    return specs


def _inproj_body(*refs, seg_sizes):
    n_seg = len(seg_sizes)
    g_ref, w_ref, pa_ref, q_ref, k_ref, v_ref = refs[n_seg:]

    def run(x_ref):
        x = x_ref[...]
        h = x * lax.rsqrt(jnp.mean(x * x, axis=-1, keepdims=True) + RMS_EPS) * g_ref[...]
        p = _dot(h.astype(BF16), w_ref[...])
        pa_ref[...] = p[:, :SHIFT_W]
        q_ref[...] = (p[:, SHIFT_W:SHIFT_W + B_WIDTH] * (HEAD_DIM ** -0.5)).astype(BF16)
        k_ref[...] = p[:, SHIFT_W + B_WIDTH:SHIFT_W + B_WIDTH + B_KV_WIDTH]
        v_ref[...] = p[:, SHIFT_W + B_WIDTH + B_KV_WIDTH:]

    _run_segment(pl.program_id(0), seg_sizes, run, [(r,) for r in refs[:n_seg]])


def _inproj_call(xs_list, firsts, seg_sizes, g, w_bf16, *, tm):
    d = xs_list[0].shape[1]
    n_tiles = sum(seg_sizes)
    n = n_tiles * tm
    row = lambda i: (i, 0)
    fixed = lambda i: (0, 0)
    return pl.pallas_call(
        functools.partial(_inproj_body, seg_sizes=tuple(seg_sizes)),
        grid=(n_tiles,),
        in_specs=_seg_specs(tm, d, seg_sizes, firsts) + [pl.BlockSpec((1, d), fixed), pl.BlockSpec((d, IN_W), fixed)],
        out_specs=[pl.BlockSpec((tm, SHIFT_W), row), pl.BlockSpec((tm, B_WIDTH), row),
                   pl.BlockSpec((tm, B_KV_WIDTH), row), pl.BlockSpec((tm, B_KV_WIDTH), row)],
        out_shape=[jax.ShapeDtypeStruct((n, SHIFT_W), F32), jax.ShapeDtypeStruct((n, B_WIDTH), BF16),
                   jax.ShapeDtypeStruct((n, B_KV_WIDTH), F32), jax.ShapeDtypeStruct((n, B_KV_WIDTH), F32)],
        compiler_params=pltpu.CompilerParams(dimension_semantics=("arbitrary",), vmem_limit_bytes=VMEM_LIMIT),
        name="inproj",
    )(*xs_list, g, w_bf16)


def _rwkv_body(pa_ref, shift0_ref, s0_ref, mu_ref, w0_ref, a0_ref, wlora_ref, wgate_ref, kk_ref, ka_ref,
               rk_ref, lng_ref, lnb_ref, oa_ref, sout_ref, s_scr, prev_scr, *, chunk, n_sub, n_bat, t_real):
    C = chunk
    seq_rows = n_sub * C
    n_seg = n_bat * n_sub
    rows = n_seg * C
    c = pl.program_id(1)

    @pl.when(c == 0)
    def _():
        prev_scr[...] = shift0_ref[...]
        for b in range(n_bat):
            for j in range(N_PAIRS):
                s_scr[b * N_PAIRS + j] = jnp.concatenate([s0_ref[b, PAIR * j + h] for h in range(PAIR)], axis=1)

    pa = pa_ref[...]
    row = lax.broadcasted_iota(jnp.int32, (rows, 1), 0)
    row_in_chunk = row & (C - 1)
    pa_prev = pltpu.roll(pa, 1, axis=0)
    for b in range(n_bat):
        pa_prev = jnp.where(row == b * seq_rows, prev_scr[b], pa_prev)
        last = (b + 1) * seq_rows - C + t_real - 1
        prev_scr[b] = pa[last:last + 1]
    xm = pa + mu_ref[...] * (pa_prev - pa)

    r = xm[:, :A_WIDTH]
    k = xm[:, A_WIDTH:2 * A_WIDTH]
    v = xm[:, 2 * A_WIDTH:3 * A_WIDTH]
    lora_in = xm[:, 3 * A_WIDTH:3 * A_WIDTH + LORA_W]
    gd = xm[:, 3 * A_WIDTH + LORA_W:]

    lane = lax.broadcasted_iota(jnp.int32, (1, LANE), 1)
    lo_half = lane < HEAD_DIM
    z = jnp.where(lo_half, jnp.tanh(lora_in), lora_in)
    lw = _dot(z.astype(BF16), wlora_ref[...])
    dec_pre = w0_ref[...] + lw[:, :A_WIDTH]
    a = _sigmoid(a0_ref[...] + lw[:, A_WIDTH:])
    logdec = (-LOG2_E * DECAY_SCALE) * _sigmoid(dec_pre)
    g = _dot(_sigmoid(gd).astype(BF16), wgate_ref[...])

    ri = lax.broadcasted_iota(jnp.int32, (LANE, LANE), 0)
    ci = lax.broadcasted_iota(jnp.int32, (LANE, LANE), 1)
    seg = jnp.where((ri >> HEAD_SHIFT) == (ci >> HEAD_SHIFT), 1.0, 0.0).astype(BF16)
    seg2 = jnp.concatenate([seg, seg], axis=0)

    def headsum(x):
        return jnp.concatenate([_dot(_hi_lo(x[:, j * LANE:(j + 1) * LANE], 1), seg2) for j in range(N_PAIRS)],
                               axis=1)

    kk = k * kk_ref[...]
    kk = kk * lax.rsqrt(jnp.maximum(headsum(kk * kk), 1e-24))
    k = k * (1.0 + (a - 1.0) * ka_ref[...])
    bonus = headsum(r * k * rk_ref[...]) * v

    if t_real < C:
        valid = row_in_chunk < t_real
        logdec = jnp.where(valid, logdec, 0.0)
        kk = jnp.where(valid, kk, 0.0)
        k = jnp.where(valid, k, 0.0)
        v = jnp.where(valid, v, 0.0)

    cum = logdec
    shift = 1
    while shift < C:
        cum = cum + jnp.where(row_in_chunk >= shift, pltpu.roll(cum, shift, axis=0), 0.0)
        shift *= 2
    ends = [cum[(s + 1) * C - 1:(s + 1) * C] for s in range(n_seg)]
    cum_end = jnp.concatenate([jnp.broadcast_to(e, (C, A_WIDTH)) for e in ends], axis=0) if n_seg > 1 else ends[0]
    w_incl = jnp.exp2(cum)
    w_prev = jnp.exp2(cum - logdec)
    w_inv = jnp.exp2(-cum)
    w_end = jnp.exp2(cum_end - cum)
    w_chunk = [jnp.exp2(e) for e in ends]
    kka = kk * a
    terms = dict(A=-kk * w_prev, R=r * w_incl, B=kka * w_inv, K=k * w_inv, V=v, Be=kka * w_end, Ke=k * w_end)

    C2 = PAIR * C
    t_idx = lax.broadcasted_iota(jnp.int32, (C, 1), 0)
    i_idx = lax.broadcasted_iota(jnp.int32, (1, C2), 1) & (C - 1)
    strict = i_idx < t_idx
    incl = i_idx <= t_idx
    ident = jnp.where(i_idx == t_idx, 1.0, 0.0).astype(F32)
    lo_time = lax.broadcasted_iota(jnp.int32, (1, C2), 1) < C
    n_levels = max(1, (C - 1).bit_length())
    bf = lambda x: x.astype(BF16)

    def bd(x, lo_mask=lo_half):
        return bf(jnp.concatenate([jnp.where(lo_mask, x, 0.0), jnp.where(lo_mask, 0.0, x)], axis=0))

    chains = [(s, j) for s in range(n_seg) for j in range(N_PAIRS)]
    tile = lambda name, s, j: terms[name][s * C:(s + 1) * C, j * LANE:(j + 1) * LANE]
    r_sbs = {ch: tile("R", *ch) for ch in chains}
    stk = {ch: {nm: bd(tile(nm, *ch)) for nm in ("A", "B", "K", "V", "Be", "Ke")} for ch in chains}
    m_ab, aak, arb, ark = {}, {}, {}, {}
    for ch in chains:
        t = stk[ch]
        a_sbs, rb = bf(tile("A", *ch)), bf(r_sbs[ch])
        if C2 % LANE == 0:
            m1 = _dot_nt(jnp.concatenate([a_sbs, rb], axis=0), jnp.concatenate([t["B"], t["K"]], axis=0))
            ab, ak, rbm, rk = m1[:C, :C2], m1[:C, C2:], m1[C:, :C2], m1[C:, C2:]
        else:
            ab, ak = _dot_nt(a_sbs, t["B"]), _dot_nt(a_sbs, t["K"])
            rbm, rk = _dot_nt(rb, t["B"]), _dot_nt(rb, t["K"])
        m_ab[ch] = jnp.where(strict, ab, 0.0)
        aak[ch] = jnp.where(strict, ak, 0.0)
        arb[ch] = bf(jnp.where(incl, rbm, 0.0))
        ark[ch] = jnp.where(incl, rk, 0.0)

    nn = dict(m_ab)
    tinv = {ch: ident + m_ab[ch] for ch in chains}
    for lvl in range(1, n_levels):
        if lvl == 1:
            for ch in chains:
                nn[ch] = _dot(bf(nn[ch]), bd(nn[ch], lo_time))
            continue
        for ch in chains:
            both = _dot(bf(jnp.concatenate([nn[ch], tinv[ch]], axis=0)), bd(nn[ch], lo_time))
            nn[ch] = both[:C]
            tinv[ch] = tinv[ch] + both[C:]
    if n_levels > 1:
        for ch in chains:
            tinv[ch] = tinv[ch] + _dot(bf(tinv[ch]), bd(nn[ch], lo_time))
    av = {ch: _dot(bf(jnp.concatenate([aak[ch], ark[ch]], axis=0)), stk[ch]["V"]) for ch in chains}
    x = {ch: _dot(bf(tinv[ch]), jnp.concatenate([stk[ch]["A"], bd(av[ch][:C])], axis=1)) for ch in chains}
    a_eff = {ch: bd(x[ch][:, :LANE]) for ch in chains}
    v_eff = {ch: bd(x[ch][:, LANE:]) for ch in chains}
    zed = {ch: _dot(arb[ch], jnp.concatenate([a_eff[ch], v_eff[ch]], axis=1)) for ch in chains}
    rhat = {ch: bf(r_sbs[ch] + zed[ch][:, :LANE]) for ch in chains}
    y0 = {ch: av[ch][C:] + zed[ch][:, LANE:] for ch in chains}
    p_mat = {ch: bf(_dot_tn(stk[ch]["Be"], a_eff[ch])) for ch in chains}
    q_bd = {ch: _dot_tn(jnp.concatenate([v_eff[ch], stk[ch]["V"]], axis=0),
                        jnp.concatenate([stk[ch]["Be"], stk[ch]["Ke"]], axis=0)) for ch in chains}

    y_rows = []
    for b in range(n_bat):
        state = [s_scr[b * N_PAIRS + j] for j in range(N_PAIRS)]
        for sub in range(n_sub):
            s = b * n_sub + sub
            ys = []
            for j in range(N_PAIRS):
                ch = (s, j)
                hi = bf(state[j])
                lo = bf(state[j] - hi.astype(F32))
                yy = _dot_nt(rhat[ch], jnp.concatenate([bd(hi), bd(lo)], axis=0))
                ys.append(yy[:, :LANE] + yy[:, LANE:] + y0[ch])
                sp = _dot_nt(jnp.concatenate([hi, lo], axis=0), p_mat[ch])
                q_sbs = q_bd[ch][:HEAD_DIM] + q_bd[ch][HEAD_DIM:]
                state[j] = state[j] * w_chunk[s][:, j * LANE:(j + 1) * LANE] + sp[:HEAD_DIM] + sp[HEAD_DIM:] + q_sbs
            y_rows.append(jnp.concatenate(ys, axis=1))
        for j in range(N_PAIRS):
            s_scr[b * N_PAIRS + j] = state[j]

    y = jnp.concatenate(y_rows, axis=0) if n_seg > 1 else y_rows[0]
    inv_n = 1.0 / HEAD_DIM
    mean = headsum(y) * inv_n
    d = y - mean
    var = headsum(d * d) * inv_n
    yn = d * lax.rsqrt(var + LNX_EPS) * lng_ref[...] + lnb_ref[...]
    oa_ref[...] = ((yn + bonus) * g).astype(oa_ref.dtype)

    @pl.when(c == pl.num_programs(1) - 1)
    def _():
        for b in range(n_bat):
            for j in range(N_PAIRS):
                for h in range(PAIR):
                    sout_ref[b, PAIR * j + h] = s_scr[b * N_PAIRS + j][:, h * HEAD_DIM:(h + 1) * HEAD_DIM]


def _rwkv_call(pa2d, shift0, s0_pairs, prm, *, n_seq, n_chunks, chunk, n_sub, n_bat, t_real):
    n_steps = n_chunks // n_sub
    assert n_seq % n_bat == 0 and (n_bat == 1 or n_steps == 1)
    row = lambda b, c: (b * n_steps + c, 0)
    seq3 = lambda b, c: (b, 0, 0)
    fixed = lambda b, c: (0, 0)
    vec = lambda w: pl.BlockSpec((1, w), fixed)
    body = functools.partial(_rwkv_body, chunk=chunk, n_sub=n_sub, n_bat=n_bat, t_real=t_real)
    n_rows = n_seq * n_chunks * chunk
    blk_rows = n_bat * n_sub * chunk
    state_blk = (n_bat, A_HEADS, HEAD_DIM, HEAD_DIM)
    seq4 = lambda b, c: (b, 0, 0, 0)
    return pl.pallas_call(
        body,
        grid=(n_seq // n_bat, n_steps),
        in_specs=[pl.BlockSpec((blk_rows, SHIFT_W), row),
                  pl.BlockSpec((n_bat, 1, SHIFT_W), seq3),
                  pl.BlockSpec(state_blk, seq4),
                  vec(SHIFT_W), vec(A_WIDTH), vec(A_WIDTH),
                  pl.BlockSpec((LORA_W, 2 * A_WIDTH), fixed), pl.BlockSpec((GATE_LORA, A_WIDTH), fixed),
                  vec(A_WIDTH), vec(A_WIDTH), vec(A_WIDTH), vec(A_WIDTH), vec(A_WIDTH)],
        out_specs=[pl.BlockSpec((blk_rows, A_WIDTH), row), pl.BlockSpec(state_blk, seq4)],
        out_shape=[jax.ShapeDtypeStruct((n_rows, A_WIDTH), BF16),
                   jax.ShapeDtypeStruct((n_seq, A_HEADS, HEAD_DIM, HEAD_DIM), F32)],
        scratch_shapes=[pltpu.VMEM((n_bat * N_PAIRS, HEAD_DIM, LANE), F32), pltpu.VMEM((n_bat, 1, SHIFT_W), F32)],
        compiler_params=pltpu.CompilerParams(dimension_semantics=("arbitrary", "arbitrary"),
                                             vmem_limit_bytes=VMEM_LIMIT),
        name="rwkv",
    )(pa2d, shift0, s0_pairs, prm["mu"], prm["w0"], prm["a0"], prm["wlora"], prm["wgate"], prm["kk"], prm["ka"],
      prm["rk"], prm["lng"], prm["lnb"])


def _swa_bias(tq):
    rows = B_GROUP * tq
    grp = jnp.arange(rows, dtype=jnp.int32)[:, None] // tq
    t = jnp.arange(rows, dtype=jnp.int32)[:, None] % tq
    dist_p = t + WINDOW - jnp.arange(WINDOW, dtype=jnp.int32)[None, :]
    dist_c = t - jnp.arange(tq, dtype=jnp.int32)[None, :]

    def bias(dist, kv):
        slope = sum(jnp.where(grp == g, 2.0 ** -(kv * B_GROUP + g + 1), 0.0) for g in range(B_GROUP))
        return jnp.where((dist >= 0) & (dist < WINDOW), -slope * dist.astype(F32), MASK_NEG)

    return (jnp.stack([bias(dist_p, kv) for kv in range(B_KV_HEADS)]),
            jnp.stack([bias(dist_c, kv) for kv in range(B_KV_HEADS)]))


def _swa_body(sink_ref, q_ref, kp_ref, vp_ref, kc_ref, vc_ref, bp_ref, bc_ref, gn_ref, o_ref, *cache_refs, tq, n_bat,
              n_qb, t_real, first_has_prev):
    nblk = pl.program_id(1)
    lane = lax.broadcasted_iota(jnp.int32, (1, LANE), 1)
    kv_masks = [(lane < HEAD_DIM) if kv == 0 else (lane >= HEAD_DIM) for kv in range(B_KV_HEADS)]
    blocks = []
    for s in range(n_bat * n_qb):
        rs = slice(s * tq, (s + 1) * tq)
        kc, vc = kc_ref[rs], vc_ref[rs]
        if n_qb > 1 and s > 0:
            ps = slice((s - 1) * tq, s * tq)
            kp, vp, has_prev = kc_ref[ps], vc_ref[ps], True
        else:
            kp, vp, has_prev = kp_ref[s], vp_ref[s], first_has_prev
        if cache_refs:
            row = lax.broadcasted_iota(jnp.int32, (WINDOW, 1), 0)
            for out_ref, old, new in zip(cache_refs, (kp, vp), (kc, vc)):
                new_tail = jnp.concatenate([pltpu.roll(new, tq - t_real, axis=0)] * (WINDOW // tq), axis=0)
                out_ref[s] = jnp.where(row >= WINDOW - t_real, new_tail, pltpu.roll(old, WINDOW - t_real, axis=0))
        pen = 0.0 if has_prev else jnp.where(nblk > 0, 0.0, MASK_NEG)
        blocks.append(dict(rs=rs, q=q_ref[rs], kp=kp.astype(BF16), kc=kc.astype(BF16), vp=vp, vc=vc, pen=pen))

    keys = [(b, kv) for b in range(len(blocks)) for kv in range(B_KV_HEADS)]
    s_p, s_c, vpm, vcm = {}, {}, {}, {}
    for b, kv in keys:
        blk, mk = blocks[b], kv_masks[kv]
        q_st = jnp.concatenate([jnp.where(mk, blk["q"][:, g * LANE:(g + 1) * LANE], 0.0) for g in range(B_GROUP)],
                               axis=0).astype(BF16)
        s_p[b, kv] = _dot_nt(q_st, blk["kp"])
        s_c[b, kv] = _dot_nt(q_st, blk["kc"])
        vpm[b, kv] = jnp.where(mk, blk["vp"], 0.0).astype(BF16)
        vcm[b, kv] = jnp.where(mk, blk["vc"], 0.0).astype(BF16)

    slabs = [(b, kv, g) for b, kv in keys for g in range(B_GROUP)]
    e_p, e_c, inv = {}, {}, {}
    for b, kv, g in slabs:
        rs = slice(g * tq, (g + 1) * tq)
        sink = sink_ref[kv * B_GROUP + g]
        sp = s_p[b, kv][rs] + (bp_ref[kv, rs, :] + blocks[b]["pen"])
        sc = s_c[b, kv][rs] + bc_ref[kv, rs, :]
        if tq == WINDOW:
            m = jnp.maximum(jnp.max(jnp.maximum(sp, sc), axis=-1, keepdims=True), sink)
        else:
            m = jnp.maximum(jnp.maximum(jnp.max(sp, axis=-1, keepdims=True), jnp.max(sc, axis=-1, keepdims=True)),
                            sink)
        ep, ec = jnp.exp(sp - m), jnp.exp(sc - m)
        if tq == WINDOW:
            e_sum = jnp.sum(ep + ec, axis=-1, keepdims=True)
        else:
            e_sum = jnp.sum(ep, axis=-1, keepdims=True) + jnp.sum(ec, axis=-1, keepdims=True)
        e_p[b, kv, g], e_c[b, kv, g] = ep.astype(BF16), ec.astype(BF16)
        inv[b, kv, g] = 1.0 / (e_sum + jnp.exp(sink - m))

    tiles = {}
    for b, kv, g in slabs:
        o = (_dot(e_p[b, kv, g], vpm[b, kv]) + _dot(e_c[b, kv, g], vcm[b, kv])) * inv[b, kv, g]
        tiles[b, g] = o if kv == 0 else tiles[b, g] + o

    for b, blk in enumerate(blocks):
        ssq = sum(jnp.sum(tiles[b, g] * tiles[b, g], axis=-1, keepdims=True) for g in range(B_GROUP))
        inv_rms = lax.rsqrt(ssq * (1.0 / B_WIDTH) + RMS_EPS)
        out = jnp.concatenate([tiles[b, g] for g in range(B_GROUP)], axis=1) * inv_rms * gn_ref[...]
        o_ref[blk["rs"]] = out.astype(o_ref.dtype)


def _swa_call(sink, q2d, kprev, vprev, kcur2d, vcur2d, gn, *, n_seq, n_blk, tq, n_bat, first_has_prev, prev_map,
              out_dtype, n_qb=1, cache_rows=0):
    assert n_seq % n_bat == 0 and (n_bat == 1 or n_blk * n_qb == 1) and (cache_rows == 0 or n_blk * n_qb == 1)
    row = lambda b, n: (b * n_blk + n, 0)
    body = functools.partial(_swa_body, tq=tq, n_bat=n_bat, n_qb=n_qb, t_real=cache_rows,
                             first_has_prev=first_has_prev)
    bias_p, bias_c = _swa_bias(tq)
    whole = lambda b, n: (0, 0, 0)
    blk_rows = n_bat * n_qb * tq
    out_specs = [pl.BlockSpec((blk_rows, B_WIDTH), row)]
    out_shape = [jax.ShapeDtypeStruct((n_seq * n_blk * n_qb * tq, B_WIDTH), out_dtype)]
    if cache_rows:
        out_specs += [pl.BlockSpec((n_bat, WINDOW, B_KV_WIDTH), prev_map)] * 2
        out_shape += [jax.ShapeDtypeStruct((n_seq, WINDOW, B_KV_WIDTH), F32)] * 2
    return pl.pallas_call(
        body,
        grid=(n_seq // n_bat, n_blk),
        in_specs=[pl.BlockSpec(memory_space=pltpu.SMEM),
                  pl.BlockSpec((blk_rows, B_WIDTH), row),
                  pl.BlockSpec((n_bat, WINDOW, B_KV_WIDTH), prev_map),
                  pl.BlockSpec((n_bat, WINDOW, B_KV_WIDTH), prev_map),
                  pl.BlockSpec((blk_rows, B_KV_WIDTH), row),
                  pl.BlockSpec((blk_rows, B_KV_WIDTH), row),
                  pl.BlockSpec(bias_p.shape, whole),
                  pl.BlockSpec(bias_c.shape, whole),
                  pl.BlockSpec((1, B_WIDTH), lambda b, n: (0, 0))],
        out_specs=out_specs,
        out_shape=out_shape,
        compiler_params=pltpu.CompilerParams(dimension_semantics=("arbitrary", "arbitrary"),
                                             vmem_limit_bytes=VMEM_LIMIT),
        name="swa",
    )(sink, q2d, kprev, vprev, kcur2d, vcur2d, bias_p, bias_c, gn)


def _route_cols(lg):
    row = lax.broadcasted_iota(jnp.int32, (LOGIT_ROWS, 1), 0)
    row_f = row.astype(F32)
    no_row = float(LOGIT_ROWS)
    is_group = row < N_GROUPS
    m_g = jnp.max(jnp.where(is_group, lg, MASK_NEG), axis=0, keepdims=True)
    g_idx = jnp.min(jnp.where(is_group & (lg == m_g), row_f, no_row), axis=0, keepdims=True)
    p_group = 1.0 / jnp.sum(jnp.where(is_group, jnp.exp(lg - m_g), 0.0), axis=0, keepdims=True)
    e_row = row - N_GROUPS
    in_group = (e_row >= 0) & (e_row < N_EXPERTS) & ((e_row >> 2).astype(F32) == g_idx)
    m_1 = jnp.max(jnp.where(in_group, lg, MASK_NEG), axis=0, keepdims=True)
    i_1 = jnp.min(jnp.where(in_group & (lg == m_1), row_f, no_row), axis=0, keepdims=True)
    rest = in_group & (row_f != i_1)
    m_2 = jnp.max(jnp.where(rest, lg, MASK_NEG), axis=0, keepdims=True)
    i_2 = jnp.min(jnp.where(rest & (lg == m_2), row_f, no_row), axis=0, keepdims=True)
    ratio = jnp.exp(m_2 - m_1)
    w_1 = p_group / (1.0 + ratio)
    out_row = lax.broadcasted_iota(jnp.int32, (ROUTE_ROWS, 1), 0)
    return jnp.where(out_row == 0, i_1 - N_GROUPS,
                     jnp.where(out_row == 1, i_2 - N_GROUPS,
                               jnp.where(out_row == 2, w_1, jnp.where(out_row == 3, w_1 * ratio, 0.0))))


def _outproj_body(*refs, seg_sizes):
    n_seg = len(seg_sizes)
    x_refs, oa_refs, ob_refs = refs[:n_seg], refs[n_seg:2 * n_seg], refs[2 * n_seg:3 * n_seg]
    wa_ref, wb_ref, g_ref, wr_ref, br_ref, x1_ref, h2_ref, rt_ref = refs[3 * n_seg:]

    def run(x_ref, oa_ref, ob_ref):
        x1 = x_ref[...] + _dot(oa_ref[...].astype(BF16), wa_ref[...]) + _dot(ob_ref[...].astype(BF16), wb_ref[...])
        x1_ref[...] = x1
        h2 = x1 * lax.rsqrt(jnp.mean(x1 * x1, axis=-1, keepdims=True) + RMS_EPS) * g_ref[...]
        h2b = h2.astype(BF16)
        h2_ref[...] = h2b
        lg = _dot(h2b, wr_ref[...]) + br_ref[...]
        rt_ref[...] = _route_cols(lg.T[:LOGIT_ROWS])

    _run_segment(pl.program_id(0), seg_sizes, run, list(zip(x_refs, oa_refs, ob_refs)))


def _outproj_call(x_list, x_firsts, oa_list, ob_list, seg_sizes, wa, wb, g, wr, br, *, tm):
    d = x_list[0].shape[1]
    n_tiles = sum(seg_sizes)
    n = n_tiles * tm
    row = lambda i: (i, 0)
    fixed = lambda i: (0, 0)
    zeros = [0] * len(seg_sizes)
    return pl.pallas_call(
        functools.partial(_outproj_body, seg_sizes=tuple(seg_sizes)),
        grid=(n_tiles,),
        in_specs=(_seg_specs(tm, d, seg_sizes, x_firsts) + _seg_specs(tm, A_WIDTH, seg_sizes, zeros)
                  + _seg_specs(tm, B_WIDTH, seg_sizes, zeros)
                  + [pl.BlockSpec((A_WIDTH, d), fixed), pl.BlockSpec((B_WIDTH, d), fixed), pl.BlockSpec((1, d), fixed),
                     pl.BlockSpec((d, ROUTE_W), fixed), pl.BlockSpec((1, ROUTE_W), fixed)]),
        out_specs=[pl.BlockSpec((tm, d), row), pl.BlockSpec((tm, d), row), pl.BlockSpec((ROUTE_ROWS, tm), row)],
        out_shape=[jax.ShapeDtypeStruct((n, d), F32), jax.ShapeDtypeStruct((n, d), BF16),
                   jax.ShapeDtypeStruct((n_tiles * ROUTE_ROWS, tm), F32)],
        compiler_params=pltpu.CompilerParams(dimension_semantics=("arbitrary",), vmem_limit_bytes=VMEM_LIMIT),
        name="outproj",
    )(*x_list, *oa_list, *ob_list, wa, wb, g, wr, br)


def _expert_body(vb_ref, ve_ref, lo_ref, hi_ref, x_ref, wg_ref, wu_ref, wd_ref, y_ref, wg_bf, wu_bf, wd_bf):
    v = pl.program_id(0)
    pv = jnp.maximum(v - 1, 0)
    lo, hi = lo_ref[v], hi_ref[v]
    first_visit = (v == 0) | (vb_ref[v] != vb_ref[pv])

    @pl.when((v == 0) | (ve_ref[v] != ve_ref[pv]))
    def _():
        wg_bf[...] = wg_ref[0].astype(BF16)
        wu_bf[...] = wu_ref[0].astype(BF16)
        wd_bf[...] = wd_ref[0].astype(BF16)

    @pl.when(hi > lo)
    def _():
        x = x_ref[...]
        gate = _dot(x, wg_bf[...])
        up = _dot(x, wu_bf[...])
        mid = (gate * _sigmoid(gate) * up).astype(BF16)
        y = _dot(mid, wd_bf[...])
        row = lax.broadcasted_iota(jnp.int32, (y.shape[0], 1), 0)
        mine = (row >= lo) & (row < hi)

        @pl.when(first_visit)
        def _():
            y_ref[...] = jnp.where(mine, y, 0.0)

        @pl.when(jnp.logical_not(first_visit))
        def _():
            y_ref[...] = jnp.where(mine, y, y_ref[...])


def _expert_call(visits, xs, wg, wu, wd, *, tb):
    n_rows, d = xs.shape
    ff = wg.shape[-1]
    vb, ve, lo, hi = visits
    blk = lambda v, vb, ve, lo, hi: (vb[v], 0)
    wsel = lambda v, vb, ve, lo, hi: (ve[v], 0, 0)
    grid_spec = pltpu.PrefetchScalarGridSpec(
        num_scalar_prefetch=4,
        grid=(vb.shape[0],),
        in_specs=[pl.BlockSpec((tb, d), blk), pl.BlockSpec((1, d, ff), wsel), pl.BlockSpec((1, d, ff), wsel),
                  pl.BlockSpec((1, ff, d), wsel)],
        out_specs=pl.BlockSpec((tb, d), blk),
        scratch_shapes=[pltpu.VMEM((d, ff), BF16), pltpu.VMEM((d, ff), BF16), pltpu.VMEM((ff, d), BF16)],
    )
    return pl.pallas_call(
        _expert_body,
        grid_spec=grid_spec,
        out_shape=jax.ShapeDtypeStruct((n_rows, d), F32),
        compiler_params=pltpu.CompilerParams(dimension_semantics=("arbitrary",), vmem_limit_bytes=VMEM_LIMIT),
        name="experts",
    )(vb, ve, lo, hi, xs, wg, wu, wd)


def _final_body(*refs, seg_sizes):
    n_seg = len(seg_sizes)
    g_ref, o_ref = refs[4 * n_seg:]

    def run(x1_ref, y0_ref, y1_ref, rt_ref):
        tm = rt_ref.shape[1]
        rt = jnp.concatenate([rt_ref[...], jnp.zeros((LANE - ROUTE_ROWS, tm), F32)], axis=0).T
        x = x1_ref[...] + (rt[:, 2:3] * y0_ref[...] + rt[:, 3:4] * y1_ref[...])
        o_ref[...] = x * lax.rsqrt(jnp.mean(x * x, axis=-1, keepdims=True) + RMS_EPS) * g_ref[...]

    _run_segment(pl.program_id(0), seg_sizes, run, [refs[4 * s:4 * s + 4] for s in range(n_seg)])


def _final_call(sources, firsts, seg_sizes, g, *, tm):
    d = sources[0][0].shape[1]
    n_tiles = sum(seg_sizes)
    in_specs, start = [], 0
    for size, first in zip(seg_sizes, firsts):
        src = lambda i, s=start, f=first, n=size: (f + jnp.clip(i - s, 0, n - 1), 0)
        in_specs += [pl.BlockSpec((tm, d), src)] * 3 + [pl.BlockSpec((ROUTE_ROWS, tm), src)]
        start += size
    return pl.pallas_call(
        functools.partial(_final_body, seg_sizes=tuple(seg_sizes)),
        grid=(n_tiles,),
        in_specs=in_specs + [pl.BlockSpec((1, d), lambda i: (0, 0))],
        out_specs=pl.BlockSpec((tm, d), lambda i: (i, 0)),
        out_shape=jax.ShapeDtypeStruct((n_tiles * tm, d), F32),
        compiler_params=pltpu.CompilerParams(dimension_semantics=("arbitrary",), vmem_limit_bytes=VMEM_LIMIT),
        name="final",
    )(*[a for src in sources for a in src], g)


def _dispatch(flat_e, n, tb):
    n_assign = n * TOP_K_INNER
    idx_bits = (n_assign - 1).bit_length()
    assert N_EXPERTS << idx_bits < 2 ** 31 and n_assign % tb == 0
    ids = jnp.arange(n_assign, dtype=jnp.int32)
    sorted_key = jnp.sort((flat_e << idx_bits) | ids)
    sorted_id = sorted_key & ((1 << idx_bits) - 1)
    src_tok = sorted_id - n * (sorted_id // n)

    experts = jnp.arange(N_EXPERTS, dtype=jnp.int32)
    onehot = (flat_e[:, None] == experts[None, :]).astype(jnp.int32)
    csum = jnp.cumsum(onehot, axis=0)
    counts = csum[-1]
    ends = jnp.cumsum(counts)
    starts = ends - counts
    dest = (jnp.sum((csum - 1 + starts[None, :]) * onehot, axis=1)).astype(jnp.int32)

    n_blocks = n_assign // tb
    n_visits = n_blocks + N_EXPERTS - 1
    first_blk = starts // tb
    n_vis_e = jnp.where(counts > 0, (ends + tb - 1) // tb - first_blk, 0)
    v_end = jnp.cumsum(n_vis_e)
    v = jnp.arange(n_visits, dtype=jnp.int32)
    valid = v < v_end[-1]
    last_e = jnp.max(jnp.where(counts > 0, experts, 0))
    ve = jnp.where(valid, jnp.sum((v_end[None, :] <= v[:, None]).astype(jnp.int32), axis=1), last_e)
    pick = lambda a: jnp.sum(jnp.where(ve[:, None] == experts[None, :], a[None, :], 0), axis=1)
    vb = pick(first_blk) + v - pick(v_end - n_vis_e)
    lo = jnp.maximum(pick(starts), vb * tb) - vb * tb
    hi = jnp.minimum(pick(ends), (vb + 1) * tb) - vb * tb
    vb = jnp.where(valid, vb, n_blocks - 1)
    lo = jnp.where(valid, lo, 0)
    hi = jnp.where(valid, hi, 0)
    return dest, src_tok, (vb.astype(jnp.int32), ve.astype(jnp.int32), lo.astype(jnp.int32), hi.astype(jnp.int32))


def _q_perm():
    cols = []
    for g in range(B_GROUP):
        for kv in range(B_KV_HEADS):
            h = kv * B_GROUP + g
            cols.extend(range(h * HEAD_DIM, (h + 1) * HEAD_DIM))
    return jnp.array(cols, dtype=jnp.int32)


def _layer(x_prompt, x_sample, state_rwkv, state_shift, cache_win_k, cache_win_v, prm, norm_final_g, *,
           chunk, n_sub, moe_block):
    bp, tp, d = x_prompt.shape
    bs, ts, _ = x_sample.shape
    n_p, n_s = bp * tp, bs * ts
    ts_pad = 8
    tm = n_s
    assert tp % tm == 0 and tm % 8 == 0 and cache_win_k.shape[1] == WINDOW
    xp = x_prompt.reshape(n_p, d)
    xs = x_sample.reshape(n_s, d)

    qp = _q_perm()
    w_in = prm["w_in"]
    w_in = jnp.concatenate([w_in[:, :SHIFT_W], w_in[:, SHIFT_W:SHIFT_W + B_WIDTH][:, qp],
                            w_in[:, SHIFT_W + B_WIDTH:]], axis=1).astype(BF16)
    gn = prm["attn_norm_g"][qp][None]
    w_out = prm["w_out"]
    wa, wb = w_out[:A_WIDTH].astype(BF16), w_out[A_WIDTH:][qp].astype(BF16)
    pad_lanes = jnp.zeros((d, ROUTE_W - N_GROUPS - N_EXPERTS), F32)
    wr = jnp.concatenate([prm["w_route_group"], prm["w_route_expert"], pad_lanes], axis=1).astype(BF16)
    br = jnp.concatenate([prm["b_route_group"], prm["b_route_expert"], pad_lanes[0]])[None]

    zero_blk = jnp.zeros((LORA_W // 2, A_WIDTH), F32)
    wlora = jnp.concatenate([jnp.concatenate([prm["w_decay_up"], zero_blk], axis=1),
                             jnp.concatenate([zero_blk, prm["w_iclr_up"]], axis=1)], axis=0).astype(BF16)
    rp = dict(mu=prm["mu_shift"][None], w0=prm["w_decay0"][None], a0=prm["w_iclr0"][None], wlora=wlora,
              wgate=prm["w_gate_up"].astype(BF16), kk=prm["k_k"][None], ka=prm["k_a"][None],
              rk=prm["r_k"].reshape(1, A_WIDTH), lng=prm["lnx_g"][None], lnb=prm["lnx_b"][None])

    groups = [(0, bp)]
    tiles_per_seq = tp // tm
    nb = tp // WINDOW
    n_qb = _row_tile(nb, 4)
    kv4 = lambda a: a.reshape(a.shape[0], a.shape[1], B_KV_HEADS, HEAD_DIM)
    finals, s_p, sh_p, k_p, v_p = [], [], [], [], []
    for gi, (s0, s1) in enumerate(groups):
        has_sample = gi == len(groups) - 1
        n_seq = s1 - s0
        n_pt = n_seq * tiles_per_seq
        n_pg = n_pt * tm
        seg_sizes = [n_pt] + ([1] if has_sample else [])
        pa, q, k, v = _inproj_call([xp] + ([xs] if has_sample else []), [s0 * tiles_per_seq, 0], seg_sizes,
                                   prm["norm_mix_g"][None], w_in, tm=tm)

        oa_p, s_pg = _rwkv_call(pa, jnp.zeros((n_seq, 1, SHIFT_W), F32),
                                jnp.zeros((n_seq,) + state_rwkv.shape[1:], F32), rp, n_seq=n_seq,
                                n_chunks=tp // chunk, chunk=chunk, n_sub=n_sub, n_bat=1, t_real=chunk)
        ob_p, = _swa_call(prm["attn_sink"], q, k.reshape(-1, WINDOW, B_KV_WIDTH), v.reshape(-1, WINDOW, B_KV_WIDTH),
                          k, v, gn, n_seq=n_seq, n_blk=nb // n_qb, tq=WINDOW, n_bat=1, n_qb=n_qb,
                          first_has_prev=False, out_dtype=BF16,
                          prev_map=lambda b, i: (b * nb + jnp.maximum(i * n_qb - 1, 0), 0, 0))
        oa_list, ob_list = [oa_p], [ob_p]
        if has_sample:
            pad_rows = lambda a: jnp.pad(a[n_pg:].reshape(bs, ts, -1), ((0, 0), (0, ts_pad - ts), (0, 0))).reshape(
                bs * ts_pad, -1)
            oa_s, s_s = _rwkv_call(pad_rows(pa), state_shift[:, None, :], state_rwkv, rp, n_seq=bs,
                                   n_chunks=1, chunk=ts_pad, n_sub=1, n_bat=_row_tile(bs, 16), t_real=ts)
            ob_s, k_s, v_s = _swa_call(prm["attn_sink"], pad_rows(q).astype(F32),
                                       cache_win_k.reshape(bs, WINDOW, B_KV_WIDTH),
                                       cache_win_v.reshape(bs, WINDOW, B_KV_WIDTH), pad_rows(k), pad_rows(v), gn,
                                       n_seq=bs, n_blk=1, tq=ts_pad, n_bat=_row_tile(bs, 16), first_has_prev=True,
                                       prev_map=lambda b, i: (b, 0, 0), out_dtype=F32, cache_rows=ts)
            oa_list.append(oa_s.reshape(bs, ts_pad, A_WIDTH)[:, :ts].reshape(n_s, A_WIDTH))
            ob_list.append(ob_s.reshape(bs, ts_pad, B_WIDTH)[:, :ts].reshape(n_s, B_WIDTH))
            sh_s = pa[n_pg:].reshape(bs, ts, SHIFT_W)[:, -1]

        x1, h2, route = _outproj_call([xp] + ([xs] if has_sample else []), [s0 * tiles_per_seq, 0],
                                               oa_list, ob_list, seg_sizes, wa, wb, prm["norm_ffn_g"][None], wr, br,
                                               tm=tm)
        n = sum(seg_sizes) * tm
        route_t = route.reshape(n // tm, ROUTE_ROWS, tm)
        flat_e = jnp.concatenate([route_t[:, j, :].reshape(n) for j in range(TOP_K_INNER)]).astype(jnp.int32)
        dest, src_tok, visits = _dispatch(flat_e, n, moe_block)
        ybuf = _expert_call(visits, h2[src_tok], prm["w_exp_gate"], prm["w_exp_up"], prm["w_exp_down"], tb=moe_block)
        finals.append((x1, ybuf[dest[:n]], ybuf[dest[n:]], route))

        last_rows = lambda a, m: jnp.stack([a[(b + 1) * tp - m:(b + 1) * tp] for b in range(n_seq)], axis=0)
        s_p.append(s_pg)
        sh_p.append(last_rows(pa, 1)[:, 0])
        k_p.append(last_rows(k, WINDOW))
        v_p.append(last_rows(v, WINDOW))

    gf = norm_final_g[None]
    n_pts = [(s1 - s0) * tiles_per_seq for s0, s1 in groups]
    y_p = _final_call(finals, [0] * len(groups), n_pts, gf, tm=tm)
    y_s = _final_call(finals[-1:], n_pts[-1:], [1], gf, tm=tm)
    cat = lambda parts: jnp.concatenate(parts, axis=0)
    return (y_p.reshape(bp, tp, d), y_s.reshape(bs, ts, d), cat(s_p), cat(sh_p), kv4(cat(k_p)), kv4(cat(v_p)),
            s_s, sh_s, kv4(k_s), kv4(v_s))


def kernel(x_prompt, x_sample, state_rwkv, state_shift, cache_win_k, cache_win_v, norm_mix_g, w_in, mu_shift, w_decay0, w_decay_up, w_iclr0, w_iclr_up, w_gate_up, k_k, k_a, r_k, lnx_g, lnx_b, attn_sink, attn_norm_g, w_out, norm_ffn_g, w_route_group, b_route_group, w_route_expert, b_route_expert, w_exp_gate, w_exp_up, w_exp_down, norm_final_g):
    assert norm_mix_g.shape[0] == 1, "single-layer trunk"
    prm = dict(norm_mix_g=norm_mix_g[0], w_in=w_in[0], mu_shift=mu_shift[0], w_decay0=w_decay0[0],
               w_decay_up=w_decay_up[0], w_iclr0=w_iclr0[0], w_iclr_up=w_iclr_up[0], w_gate_up=w_gate_up[0],
               k_k=k_k[0], k_a=k_a[0], r_k=r_k[0], lnx_g=lnx_g[0], lnx_b=lnx_b[0], attn_sink=attn_sink[0],
               attn_norm_g=attn_norm_g[0], w_out=w_out[0], norm_ffn_g=norm_ffn_g[0],
               w_route_group=w_route_group[0], b_route_group=b_route_group[0],
               w_route_expert=w_route_expert[0], b_route_expert=b_route_expert[0],
               w_exp_gate=w_exp_gate[0], w_exp_up=w_exp_up[0], w_exp_down=w_exp_down[0])
    outs = _layer(x_prompt, x_sample, state_rwkv[0], state_shift[0], cache_win_k[0], cache_win_v[0], prm,
                  norm_final_g, chunk=64, n_sub=8, moe_block=512)
    y_p, y_s, s_p, sh_p, kp, vp, s_s, sh_s, ks, vs = outs
    return (y_p, y_s, s_p[None], sh_p[None], kp[None], vp[None], s_s[None], sh_s[None], ks[None], vs[None])
```

```python
import functools
import math

import jax
import jax.numpy as jnp
from jax import lax
from jax.experimental import pallas as pl
from jax.experimental.pallas import tpu as pltpu

F32 = jnp.float32
BF16 = jnp.bfloat16

HEAD_DIM = 64
A_HEADS = 8
A_WIDTH = A_HEADS * HEAD_DIM
B_HEADS = 8
B_KV_HEADS = 2
B_GROUP = B_HEADS // B_KV_HEADS
B_WIDTH = B_HEADS * HEAD_DIM
B_KV_WIDTH = B_KV_HEADS * HEAD_DIM
DECAY_LORA = 64
ICLR_LORA = 64
GATE_LORA = 128
LORA_W = DECAY_LORA + ICLR_LORA
SHIFT_W = 3 * A_WIDTH + LORA_W + GATE_LORA
IN_W = SHIFT_W + B_WIDTH + 2 * B_KV_WIDTH
WINDOW = 128
N_GROUPS = 4
EXPERTS_PER_GROUP = 4
N_EXPERTS = N_GROUPS * EXPERTS_PER_GROUP
TOP_K_INNER = 2
EXPERT_FF = 512
RMS_EPS = 1e-6
LNX_EPS = 64e-5
DECAY_OFFSET = 0.5

LANE = 128
PAIR = LANE // HEAD_DIM
N_PAIRS = A_HEADS // PAIR
ROUTE_ROWS = 8
LOGIT_ROWS = 32
MASK_NEG = -1e30
VMEM_LIMIT = 48 * 1024 * 1024

LOG2_E = math.log2(math.e)
DECAY_SCALE = math.exp(-DECAY_OFFSET)
HEAD_SHIFT = HEAD_DIM.bit_length() - 1
HI = lax.Precision.HIGHEST


def _row_tile(n, cap):
    t = cap
    while n % t:
        t //= 2
    return t


def _dot(a, b, precision=None):
    return jnp.dot(a, b, preferred_element_type=F32, precision=precision)


def _dot_nt(a, b, precision=None):
    return lax.dot_general(a, b, (((1,), (1,)), ((), ())), preferred_element_type=F32, precision=precision)


def _dot_tn(a, b, precision=None):
    return lax.dot_general(a, b, (((0,), (0,)), ((), ())), preferred_element_type=F32, precision=precision)


def _sigmoid(x):
    return 1.0 / (1.0 + jnp.exp(-x))


def _hi_lo(x, axis):
    hi = x.astype(BF16)
    lo = (x - hi.astype(F32)).astype(BF16)
    return jnp.concatenate([hi, lo], axis=axis)


def _run_segment(i, seg_sizes, run, seg_refs):
    start = 0
    for size, refs in zip(seg_sizes, seg_refs):
        pl.when((i >= start) & (i < start + size))(functools.partial(run, *refs))
        start += size


def _seg_specs(tm, width, seg_sizes, firsts):
    specs, start = [], 0
    for size, first in zip(seg_sizes, firsts):
        specs.append(pl.BlockSpec((tm, width),
                                  lambda i, s=start, f=first, n=size: (f + jnp.clip(i - s, 0, n - 1), 0)))
        start += size
    return specs


def _inproj_body(*refs, seg_sizes):
    n_seg = len(seg_sizes)
    g_ref, w_ref, pa_ref, q_ref, k_ref, v_ref = refs[n_seg:]

    def run(x_ref):
        x = x_ref[...]
        h = x * lax.rsqrt(jnp.mean(x * x, axis=-1, keepdims=True) + RMS_EPS) * g_ref[...]
        p = _dot(h.astype(BF16), w_ref[...])
        pa_ref[...] = p[:, :SHIFT_W]
        q_ref[...] = (p[:, SHIFT_W:SHIFT_W + B_WIDTH] * (HEAD_DIM ** -0.5)).astype(BF16)
        k_ref[...] = p[:, SHIFT_W + B_WIDTH:SHIFT_W + B_WIDTH + B_KV_WIDTH]
        v_ref[...] = p[:, SHIFT_W + B_WIDTH + B_KV_WIDTH:]

    _run_segment(pl.program_id(0), seg_sizes, run, [(r,) for r in refs[:n_seg]])


def _inproj_call(xs_list, firsts, seg_sizes, g, w_bf16, *, tm):
    d = xs_list[0].shape[1]
    n_tiles = sum(seg_sizes)
    n = n_tiles * tm
    row = lambda i: (i, 0)
    fixed = lambda i: (0, 0)
    return pl.pallas_call(
        functools.partial(_inproj_body, seg_sizes=tuple(seg_sizes)),
        grid=(n_tiles,),
        in_specs=_seg_specs(tm, d, seg_sizes, firsts) + [pl.BlockSpec((1, d), fixed), pl.BlockSpec((d, IN_W), fixed)],
        out_specs=[pl.BlockSpec((tm, SHIFT_W), row), pl.BlockSpec((tm, B_WIDTH), row),
                   pl.BlockSpec((tm, B_KV_WIDTH), row), pl.BlockSpec((tm, B_KV_WIDTH), row)],
        out_shape=[jax.ShapeDtypeStruct((n, SHIFT_W), F32), jax.ShapeDtypeStruct((n, B_WIDTH), BF16),
                   jax.ShapeDtypeStruct((n, B_KV_WIDTH), F32), jax.ShapeDtypeStruct((n, B_KV_WIDTH), F32)],
        compiler_params=pltpu.CompilerParams(dimension_semantics=("arbitrary",), vmem_limit_bytes=VMEM_LIMIT),
        name="inproj",
    )(*xs_list, g, w_bf16)


def _rwkv_body(pa_ref, shift0_ref, s0_ref, mu_ref, w0_ref, a0_ref, wlora_ref, wgate_ref, kk_ref, ka_ref,
               rk_ref, lng_ref, lnb_ref, oa_ref, sout_ref, s_scr, prev_scr, *, chunk, n_sub, n_bat, t_real):
    C = chunk
    seq_rows = n_sub * C
    n_seg = n_bat * n_sub
    rows = n_seg * C
    c = pl.program_id(1)

    @pl.when(c == 0)
    def _():
        prev_scr[...] = shift0_ref[...]
        for b in range(n_bat):
            for j in range(N_PAIRS):
                s_scr[b * N_PAIRS + j] = jnp.concatenate([s0_ref[b, PAIR * j + h] for h in range(PAIR)], axis=1)

    pa = pa_ref[...]
    row = lax.broadcasted_iota(jnp.int32, (rows, 1), 0)
    row_in_chunk = row & (C - 1)
    pa_prev = pltpu.roll(pa, 1, axis=0)
    for b in range(n_bat):
        pa_prev = jnp.where(row == b * seq_rows, prev_scr[b], pa_prev)
        last = (b + 1) * seq_rows - C + t_real - 1
        prev_scr[b] = pa[last:last + 1]
    xm = pa + mu_ref[...] * (pa_prev - pa)

    r = xm[:, :A_WIDTH]
    k = xm[:, A_WIDTH:2 * A_WIDTH]
    v = xm[:, 2 * A_WIDTH:3 * A_WIDTH]
    lora_in = xm[:, 3 * A_WIDTH:3 * A_WIDTH + LORA_W]
    gd = xm[:, 3 * A_WIDTH + LORA_W:]

    lane = lax.broadcasted_iota(jnp.int32, (1, LANE), 1)
    lo_half = lane < HEAD_DIM
    z = jnp.where(lo_half, jnp.tanh(lora_in), lora_in)
    lw = _dot(z.astype(BF16), wlora_ref[...])
    dec_pre = w0_ref[...] + lw[:, :A_WIDTH]
    a = _sigmoid(a0_ref[...] + lw[:, A_WIDTH:])
    logdec = (-LOG2_E * DECAY_SCALE) * _sigmoid(dec_pre)
    g = _dot(_sigmoid(gd).astype(BF16), wgate_ref[...])

    ri = lax.broadcasted_iota(jnp.int32, (LANE, LANE), 0)
    ci = lax.broadcasted_iota(jnp.int32, (LANE, LANE), 1)
    same_head = (ri >> HEAD_SHIFT) == (ci >> HEAD_SHIFT)
    seg = jnp.where(same_head, 1.0, 0.0).astype(BF16)
    seg2 = jnp.concatenate([seg, seg], axis=0)

    def headsum(x):
        return jnp.concatenate([_dot(_hi_lo(x[:, j * LANE:(j + 1) * LANE], 1), seg2) for j in range(N_PAIRS)],
                               axis=1)

    kk = k * kk_ref[...]
    kk = kk * lax.rsqrt(jnp.maximum(headsum(kk * kk), 1e-24))
    k = k * (1.0 + (a - 1.0) * ka_ref[...])
    bonus = headsum(r * k * rk_ref[...]) * v

    if t_real < C:
        valid = row_in_chunk < t_real
        logdec = jnp.where(valid, logdec, 0.0)
        kk = jnp.where(valid, kk, 0.0)
        k = jnp.where(valid, k, 0.0)
        v = jnp.where(valid, v, 0.0)

    cum = logdec
    shift = 1
    while shift < C:
        cum = cum + jnp.where(row_in_chunk >= shift, pltpu.roll(cum, shift, axis=0), 0.0)
        shift *= 2
    ends = [cum[(s + 1) * C - 1:(s + 1) * C] for s in range(n_seg)]
    cum_end = jnp.concatenate([jnp.broadcast_to(e, (C, A_WIDTH)) for e in ends], axis=0) if n_seg > 1 else ends[0]
    w_incl = jnp.exp2(cum)
    w_prev = jnp.exp2(cum - logdec)
    w_inv = jnp.exp2(-cum)
    w_end = jnp.exp2(cum_end - cum)
    w_chunk = [jnp.exp2(e) for e in ends]
    kka = kk * a
    terms = dict(A=-kk * w_prev, R=r * w_incl, B=kka * w_inv, K=k * w_inv, V=v, Be=kka * w_end, Ke=k * w_end)

    C2 = PAIR * C
    t_idx = lax.broadcasted_iota(jnp.int32, (C, 1), 0)
    i_idx = lax.broadcasted_iota(jnp.int32, (1, C2), 1) & (C - 1)
    strict = i_idx < t_idx
    incl = i_idx <= t_idx
    ident = jnp.where(i_idx == t_idx, 1.0, 0.0).astype(F32)
    lo_time = lax.broadcasted_iota(jnp.int32, (1, C2), 1) < C
    n_levels = max(1, (C - 1).bit_length())
    bf = lambda x: x.astype(BF16)

    def bd(x, lo_mask=lo_half):
        return bf(jnp.concatenate([jnp.where(lo_mask, x, 0.0), jnp.where(lo_mask, 0.0, x)], axis=0))

    chains = [(s, j) for s in range(n_seg) for j in range(N_PAIRS)]
    tile = lambda name, s, j: terms[name][s * C:(s + 1) * C, j * LANE:(j + 1) * LANE]
    r_sbs = {ch: tile("R", *ch) for ch in chains}
    stk = {ch: {nm: bd(tile(nm, *ch)) for nm in ("A", "B", "K", "V")} for ch in chains}
    m_ab, aak, arb, ark = {}, {}, {}, {}
    for ch in chains:
        t = stk[ch]
        a_sbs, rb = bf(tile("A", *ch)), bf(r_sbs[ch])
        if C2 % LANE == 0:
            m1 = _dot_nt(jnp.concatenate([a_sbs, rb], axis=0), jnp.concatenate([t["B"], t["K"]], axis=0))
            ab, ak, rbm, rk = m1[:C, :C2], m1[:C, C2:], m1[C:, :C2], m1[C:, C2:]
        else:
            ab, ak = _dot_nt(a_sbs, t["B"]), _dot_nt(a_sbs, t["K"])
            rbm, rk = _dot_nt(rb, t["B"]), _dot_nt(rb, t["K"])
        m_ab[ch] = jnp.where(strict, ab, 0.0)
        aak[ch] = jnp.where(strict, ak, 0.0)
        arb[ch] = bf(jnp.where(incl, rbm, 0.0))
        ark[ch] = jnp.where(incl, rk, 0.0)

    nn = dict(m_ab)
    tinv = {ch: ident + m_ab[ch] for ch in chains}
    for lvl in range(1, n_levels):
        if lvl == 1:
            for ch in chains:
                nn[ch] = _dot(bf(nn[ch]), bd(nn[ch], lo_time))
            continue
        for ch in chains:
            both = _dot(bf(jnp.concatenate([nn[ch], tinv[ch]], axis=0)), bd(nn[ch], lo_time))
            nn[ch] = both[:C]
            tinv[ch] = tinv[ch] + both[C:]
    if n_levels > 1:
        for ch in chains:
            tinv[ch] = tinv[ch] + _dot(bf(tinv[ch]), bd(nn[ch], lo_time))
    av = {ch: _dot(bf(jnp.concatenate([aak[ch], ark[ch]], axis=0)), stk[ch]["V"]) for ch in chains}
    x = {ch: _dot(bf(tinv[ch]), jnp.concatenate([stk[ch]["A"], bd(av[ch][:C])], axis=1)) for ch in chains}
    a_eff = {ch: bd(x[ch][:, :LANE]) for ch in chains}
    v_eff = {ch: bd(x[ch][:, LANE:]) for ch in chains}
    zed = {ch: _dot(arb[ch], jnp.concatenate([a_eff[ch], v_eff[ch]], axis=1)) for ch in chains}
    rhat = {ch: bf(r_sbs[ch] + zed[ch][:, :LANE]) for ch in chains}
    y0 = {ch: av[ch][C:] + zed[ch][:, LANE:] for ch in chains}
    p_mat = {ch: bf(jnp.where(same_head, _dot_tn(bf(tile("Be", *ch)), bf(x[ch][:, :LANE])), 0.0))
             for ch in chains}
    q_full = {ch: _dot_tn(bf(jnp.concatenate([x[ch][:, LANE:], tile("V", *ch)], axis=0)),
                          bf(jnp.concatenate([tile("Be", *ch), tile("Ke", *ch)], axis=0))) for ch in chains}

    y_rows = []
    for b in range(n_bat):
        state = [s_scr[b * N_PAIRS + j] for j in range(N_PAIRS)]
        for sub in range(n_sub):
            s = b * n_sub + sub
            ys = []
            for j in range(N_PAIRS):
                ch = (s, j)
                hi = bf(state[j])
                lo = bf(state[j] - hi.astype(F32))
                yy = _dot_nt(rhat[ch], jnp.concatenate([bd(hi), bd(lo)], axis=0))
                ys.append(yy[:, :LANE] + yy[:, LANE:] + y0[ch])
                sp = _dot_nt(jnp.concatenate([hi, lo], axis=0), p_mat[ch])
                q_sbs = jnp.where(lo_half, q_full[ch][:HEAD_DIM], q_full[ch][HEAD_DIM:])
                state[j] = state[j] * w_chunk[s][:, j * LANE:(j + 1) * LANE] + sp[:HEAD_DIM] + sp[HEAD_DIM:] + q_sbs
            y_rows.append(jnp.concatenate(ys, axis=1))
        for j in range(N_PAIRS):
            s_scr[b * N_PAIRS + j] = state[j]

    y = jnp.concatenate(y_rows, axis=0) if n_seg > 1 else y_rows[0]
    inv_n = 1.0 / HEAD_DIM
    mean = headsum(y) * inv_n
    d = y - mean
    var = headsum(d * d) * inv_n
    yn = d * lax.rsqrt(var + LNX_EPS) * lng_ref[...] + lnb_ref[...]
    oa_ref[...] = ((yn + bonus) * g).astype(oa_ref.dtype)

    @pl.when(c == pl.num_programs(1) - 1)
    def _():
        for b in range(n_bat):
            for j in range(N_PAIRS):
                for h in range(PAIR):
                    sout_ref[b, PAIR * j + h] = s_scr[b * N_PAIRS + j][:, h * HEAD_DIM:(h + 1) * HEAD_DIM]


def _rwkv_call(pa2d, shift0, s0_pairs, prm, *, n_seq, n_chunks, chunk, n_sub, n_bat, t_real):
    n_steps = n_chunks // n_sub
    assert n_seq % n_bat == 0 and (n_bat == 1 or n_steps == 1)
    row = lambda b, c: (b * n_steps + c, 0)
    seq3 = lambda b, c: (b, 0, 0)
    fixed = lambda b, c: (0, 0)
    vec = lambda w: pl.BlockSpec((1, w), fixed)
    body = functools.partial(_rwkv_body, chunk=chunk, n_sub=n_sub, n_bat=n_bat, t_real=t_real)
    n_rows = n_seq * n_chunks * chunk
    blk_rows = n_bat * n_sub * chunk
    state_blk = (n_bat, A_HEADS, HEAD_DIM, HEAD_DIM)
    seq4 = lambda b, c: (b, 0, 0, 0)
    return pl.pallas_call(
        body,
        grid=(n_seq // n_bat, n_steps),
        in_specs=[pl.BlockSpec((blk_rows, SHIFT_W), row),
                  pl.BlockSpec((n_bat, 1, SHIFT_W), seq3),
                  pl.BlockSpec(state_blk, seq4),
                  vec(SHIFT_W), vec(A_WIDTH), vec(A_WIDTH),
                  pl.BlockSpec((LORA_W, 2 * A_WIDTH), fixed), pl.BlockSpec((GATE_LORA, A_WIDTH), fixed),
                  vec(A_WIDTH), vec(A_WIDTH), vec(A_WIDTH), vec(A_WIDTH), vec(A_WIDTH)],
        out_specs=[pl.BlockSpec((blk_rows, A_WIDTH), row), pl.BlockSpec(state_blk, seq4)],
        out_shape=[jax.ShapeDtypeStruct((n_rows, A_WIDTH), BF16),
                   jax.ShapeDtypeStruct((n_seq, A_HEADS, HEAD_DIM, HEAD_DIM), F32)],
        scratch_shapes=[pltpu.VMEM((n_bat * N_PAIRS, HEAD_DIM, LANE), F32), pltpu.VMEM((n_bat, 1, SHIFT_W), F32)],
        compiler_params=pltpu.CompilerParams(dimension_semantics=("arbitrary", "arbitrary"),
                                             vmem_limit_bytes=VMEM_LIMIT),
        name="rwkv",
    )(pa2d, shift0, s0_pairs, prm["mu"], prm["w0"], prm["a0"], prm["wlora"], prm["wgate"], prm["kk"], prm["ka"],
      prm["rk"], prm["lng"], prm["lnb"])


def _swa_bias(tq):
    rows = B_GROUP * tq
    grp = jnp.arange(rows, dtype=jnp.int32)[:, None] // tq
    t = jnp.arange(rows, dtype=jnp.int32)[:, None] % tq
    dist_p = t + WINDOW - jnp.arange(WINDOW, dtype=jnp.int32)[None, :]
    dist_c = t - jnp.arange(tq, dtype=jnp.int32)[None, :]

    def bias(dist, kv):
        slope = sum(jnp.where(grp == g, 2.0 ** -(kv * B_GROUP + g + 1), 0.0) for g in range(B_GROUP))
        return jnp.where((dist >= 0) & (dist < WINDOW), -slope * dist.astype(F32), MASK_NEG)

    return (jnp.stack([bias(dist_p, kv) for kv in range(B_KV_HEADS)]),
            jnp.stack([bias(dist_c, kv) for kv in range(B_KV_HEADS)]))


def _swa_body(sink_ref, q_ref, kp_ref, vp_ref, kc_ref, vc_ref, bp_ref, bc_ref, gn_ref, o_ref, *cache_refs, tq, n_bat,
              n_qb, t_real, first_has_prev):
    nblk = pl.program_id(1)
    lane = lax.broadcasted_iota(jnp.int32, (1, LANE), 1)
    kv_masks = [(lane < HEAD_DIM) if kv == 0 else (lane >= HEAD_DIM) for kv in range(B_KV_HEADS)]
    blocks = []
    for s in range(n_bat * n_qb):
        rs = slice(s * tq, (s + 1) * tq)
        kc, vc = kc_ref[rs], vc_ref[rs]
        if n_qb > 1 and s > 0:
            ps = slice((s - 1) * tq, s * tq)
            kp, vp, has_prev = kc_ref[ps], vc_ref[ps], True
        else:
            kp, vp, has_prev = kp_ref[s], vp_ref[s], first_has_prev
        if cache_refs:
            row = lax.broadcasted_iota(jnp.int32, (WINDOW, 1), 0)
            for out_ref, old, new in zip(cache_refs, (kp, vp), (kc, vc)):
                new_tail = jnp.concatenate([pltpu.roll(new, tq - t_real, axis=0)] * (WINDOW // tq), axis=0)
                out_ref[s] = jnp.where(row >= WINDOW - t_real, new_tail, pltpu.roll(old, WINDOW - t_real, axis=0))
        pen = 0.0 if has_prev else jnp.where(nblk > 0, 0.0, MASK_NEG)
        blocks.append(dict(rs=rs, q=q_ref[rs], kp=kp.astype(BF16), kc=kc.astype(BF16), vp=vp, vc=vc, pen=pen))

    keys = [(b, kv) for b in range(len(blocks)) for kv in range(B_KV_HEADS)]
    s_p, s_c, vpm, vcm = {}, {}, {}, {}
    for b, kv in keys:
        blk, mk = blocks[b], kv_masks[kv]
        q_st = jnp.concatenate([jnp.where(mk, blk["q"][:, g * LANE:(g + 1) * LANE], 0.0) for g in range(B_GROUP)],
                               axis=0).astype(BF16)
        s_p[b, kv] = _dot_nt(q_st, blk["kp"])
        s_c[b, kv] = _dot_nt(q_st, blk["kc"])
        vpm[b, kv] = jnp.where(mk, blk["vp"], 0.0).astype(BF16)
        vcm[b, kv] = jnp.where(mk, blk["vc"], 0.0).astype(BF16)

    slabs = [(b, kv, g) for b, kv in keys for g in range(B_GROUP)]
    e_p, e_c, inv = {}, {}, {}
    for b, kv, g in slabs:
        rs = slice(g * tq, (g + 1) * tq)
        sink = sink_ref[kv * B_GROUP + g]
        sp = s_p[b, kv][rs] + (bp_ref[kv, rs, :] + blocks[b]["pen"])
        sc = s_c[b, kv][rs] + bc_ref[kv, rs, :]
        if tq == WINDOW:
            m = jnp.maximum(jnp.max(jnp.maximum(sp, sc), axis=-1, keepdims=True), sink)
        else:
            m = jnp.maximum(jnp.maximum(jnp.max(sp, axis=-1, keepdims=True), jnp.max(sc, axis=-1, keepdims=True)),
                            sink)
        ep, ec = jnp.exp(sp - m), jnp.exp(sc - m)
        if tq == WINDOW:
            e_sum = jnp.sum(ep + ec, axis=-1, keepdims=True)
        else:
            e_sum = jnp.sum(ep, axis=-1, keepdims=True) + jnp.sum(ec, axis=-1, keepdims=True)
        e_p[b, kv, g], e_c[b, kv, g] = ep, ec
        inv[b, kv, g] = 1.0 / (e_sum + jnp.exp(sink - m))

    tiles = {}
    for b, kv in keys:
        stack = lambda e: jnp.concatenate([e[b, kv, g] for g in range(B_GROUP)], axis=0).astype(BF16)
        o = _dot(stack(e_p), vpm[b, kv]) + _dot(stack(e_c), vcm[b, kv])
        for g in range(B_GROUP):
            part = o[g * tq:(g + 1) * tq] * inv[b, kv, g]
            tiles[b, g] = part if kv == 0 else tiles[b, g] + part

    for b, blk in enumerate(blocks):
        ssq = sum(jnp.sum(tiles[b, g] * tiles[b, g], axis=-1, keepdims=True) for g in range(B_GROUP))
        inv_rms = lax.rsqrt(ssq * (1.0 / B_WIDTH) + RMS_EPS)
        out = jnp.concatenate([tiles[b, g] for g in range(B_GROUP)], axis=1) * inv_rms * gn_ref[...]
        o_ref[blk["rs"]] = out.astype(o_ref.dtype)


def _swa_call(sink, q2d, kprev, vprev, kcur2d, vcur2d, gn, *, n_seq, n_blk, tq, n_bat, first_has_prev, prev_map,
              out_dtype, n_qb=1, cache_rows=0):
    assert n_seq % n_bat == 0 and (n_bat == 1 or n_blk * n_qb == 1) and (cache_rows == 0 or n_blk * n_qb == 1)
    row = lambda b, n: (b * n_blk + n, 0)
    body = functools.partial(_swa_body, tq=tq, n_bat=n_bat, n_qb=n_qb, t_real=cache_rows,
                             first_has_prev=first_has_prev)
    bias_p, bias_c = _swa_bias(tq)
    whole = lambda b, n: (0, 0, 0)
    blk_rows = n_bat * n_qb * tq
    out_specs = [pl.BlockSpec((blk_rows, B_WIDTH), row)]
    out_shape = [jax.ShapeDtypeStruct((n_seq * n_blk * n_qb * tq, B_WIDTH), out_dtype)]
    if cache_rows:
        out_specs += [pl.BlockSpec((n_bat, WINDOW, B_KV_WIDTH), prev_map)] * 2
        out_shape += [jax.ShapeDtypeStruct((n_seq, WINDOW, B_KV_WIDTH), F32)] * 2
    return pl.pallas_call(
        body,
        grid=(n_seq // n_bat, n_blk),
        in_specs=[pl.BlockSpec(memory_space=pltpu.SMEM),
                  pl.BlockSpec((blk_rows, B_WIDTH), row),
                  pl.BlockSpec((n_bat, WINDOW, B_KV_WIDTH), prev_map),
                  pl.BlockSpec((n_bat, WINDOW, B_KV_WIDTH), prev_map),
                  pl.BlockSpec((blk_rows, B_KV_WIDTH), row),
                  pl.BlockSpec((blk_rows, B_KV_WIDTH), row),
                  pl.BlockSpec(bias_p.shape, whole),
                  pl.BlockSpec(bias_c.shape, whole),
                  pl.BlockSpec((1, B_WIDTH), lambda b, n: (0, 0))],
        out_specs=out_specs,
        out_shape=out_shape,
        compiler_params=pltpu.CompilerParams(dimension_semantics=("arbitrary", "arbitrary"),
                                             vmem_limit_bytes=VMEM_LIMIT),
        name="swa",
    )(sink, q2d, kprev, vprev, kcur2d, vcur2d, bias_p, bias_c, gn)


def _route_cols(lg):
    row = lax.broadcasted_iota(jnp.int32, (LOGIT_ROWS, 1), 0)
    row_f = row.astype(F32)
    no_row = float(LOGIT_ROWS)
    is_group = row < N_GROUPS
    m_g = jnp.max(jnp.where(is_group, lg, MASK_NEG), axis=0, keepdims=True)
    g_idx = jnp.min(jnp.where(is_group & (lg == m_g), row_f, no_row), axis=0, keepdims=True)
    p_group = 1.0 / jnp.sum(jnp.where(is_group, jnp.exp(lg - m_g), 0.0), axis=0, keepdims=True)
    e_row = row - N_GROUPS
    in_group = (e_row >= 0) & (e_row < N_EXPERTS) & ((e_row >> 2).astype(F32) == g_idx)
    m_1 = jnp.max(jnp.where(in_group, lg, MASK_NEG), axis=0, keepdims=True)
    i_1 = jnp.min(jnp.where(in_group & (lg == m_1), row_f, no_row), axis=0, keepdims=True)
    rest = in_group & (row_f != i_1)
    m_2 = jnp.max(jnp.where(rest, lg, MASK_NEG), axis=0, keepdims=True)
    i_2 = jnp.min(jnp.where(rest & (lg == m_2), row_f, no_row), axis=0, keepdims=True)
    ratio = jnp.exp(m_2 - m_1)
    w_1 = p_group / (1.0 + ratio)
    out_row = lax.broadcasted_iota(jnp.int32, (ROUTE_ROWS, 1), 0)
    return jnp.where(out_row == 0, i_1 - N_GROUPS,
                     jnp.where(out_row == 1, i_2 - N_GROUPS,
                               jnp.where(out_row == 2, w_1, jnp.where(out_row == 3, w_1 * ratio, 0.0))))


def _outproj_body(*refs, seg_sizes):
    n_seg = len(seg_sizes)
    x_refs, oa_refs, ob_refs = refs[:n_seg], refs[n_seg:2 * n_seg], refs[2 * n_seg:3 * n_seg]
    wa_ref, wb_ref, g_ref, wr_ref, br_ref, x1_ref, h2_ref, rt_ref = refs[3 * n_seg:]

    def run(x_ref, oa_ref, ob_ref):
        x1 = x_ref[...] + _dot(oa_ref[...].astype(BF16), wa_ref[...]) + _dot(ob_ref[...].astype(BF16), wb_ref[...])
        x1_ref[...] = x1
        h2 = x1 * lax.rsqrt(jnp.mean(x1 * x1, axis=-1, keepdims=True) + RMS_EPS) * g_ref[...]
        h2b = h2.astype(BF16)
        h2_ref[...] = h2b
        rt_ref[...] = _route_cols(_dot_nt(wr_ref[...], h2b) + br_ref[...])

    _run_segment(pl.program_id(0), seg_sizes, run, list(zip(x_refs, oa_refs, ob_refs)))


def _outproj_call(x_list, x_firsts, oa_list, ob_list, seg_sizes, wa, wb, g, wr, br, *, tm):
    d = x_list[0].shape[1]
    n_tiles = sum(seg_sizes)
    n = n_tiles * tm
    row = lambda i: (i, 0)
    fixed = lambda i: (0, 0)
    zeros = [0] * len(seg_sizes)
    return pl.pallas_call(
        functools.partial(_outproj_body, seg_sizes=tuple(seg_sizes)),
        grid=(n_tiles,),
        in_specs=(_seg_specs(tm, d, seg_sizes, x_firsts) + _seg_specs(tm, A_WIDTH, seg_sizes, zeros)
                  + _seg_specs(tm, B_WIDTH, seg_sizes, zeros)
                  + [pl.BlockSpec((A_WIDTH, d), fixed), pl.BlockSpec((B_WIDTH, d), fixed), pl.BlockSpec((1, d), fixed),
                     pl.BlockSpec((LOGIT_ROWS, d), fixed), pl.BlockSpec((LOGIT_ROWS, 1), fixed)]),
        out_specs=[pl.BlockSpec((tm, d), row), pl.BlockSpec((tm, d), row), pl.BlockSpec((ROUTE_ROWS, tm), row)],
        out_shape=[jax.ShapeDtypeStruct((n, d), F32), jax.ShapeDtypeStruct((n, d), BF16),
                   jax.ShapeDtypeStruct((n_tiles * ROUTE_ROWS, tm), F32)],
        compiler_params=pltpu.CompilerParams(dimension_semantics=("arbitrary",), vmem_limit_bytes=VMEM_LIMIT),
        name="outproj",
    )(*x_list, *oa_list, *ob_list, wa, wb, g, wr, br)


def _expert_body(vb_ref, ve_ref, lo_ref, hi_ref, x_ref, wg_ref, wu_ref, wd_ref, y_ref, wg_bf, wu_bf, wd_bf):
    v = pl.program_id(0)
    pv = jnp.maximum(v - 1, 0)
    lo, hi = lo_ref[v], hi_ref[v]
    first_visit = (v == 0) | (vb_ref[v] != vb_ref[pv])

    @pl.when((v == 0) | (ve_ref[v] != ve_ref[pv]))
    def _():
        wg_bf[...] = wg_ref[0].astype(BF16)
        wu_bf[...] = wu_ref[0].astype(BF16)
        wd_bf[...] = wd_ref[0].astype(BF16)

    @pl.when(hi > lo)
    def _():
        x = x_ref[...]
        gate = _dot(x, wg_bf[...])
        up = _dot(x, wu_bf[...])
        mid = (gate * _sigmoid(gate) * up).astype(BF16)
        y = _dot(mid, wd_bf[...])
        row = lax.broadcasted_iota(jnp.int32, (y.shape[0], 1), 0)
        mine = (row >= lo) & (row < hi)

        @pl.when(first_visit)
        def _():
            y_ref[...] = jnp.where(mine, y, 0.0)

        @pl.when(jnp.logical_not(first_visit))
        def _():
            y_ref[...] = jnp.where(mine, y, y_ref[...])


def _expert_call(visits, xs, wg, wu, wd, *, tb):
    n_rows, d = xs.shape
    ff = wg.shape[-1]
    vb, ve, lo, hi = visits
    blk = lambda v, vb, ve, lo, hi: (vb[v], 0)
    wsel = lambda v, vb, ve, lo, hi: (ve[v], 0, 0)
    grid_spec = pltpu.PrefetchScalarGridSpec(
        num_scalar_prefetch=4,
        grid=(vb.shape[0],),
        in_specs=[pl.BlockSpec((tb, d), blk), pl.BlockSpec((1, d, ff), wsel), pl.BlockSpec((1, d, ff), wsel),
                  pl.BlockSpec((1, ff, d), wsel)],
        out_specs=pl.BlockSpec((tb, d), blk),
        scratch_shapes=[pltpu.VMEM((d, ff), BF16), pltpu.VMEM((d, ff), BF16), pltpu.VMEM((ff, d), BF16)],
    )
    return pl.pallas_call(
        _expert_body,
        grid_spec=grid_spec,
        out_shape=jax.ShapeDtypeStruct((n_rows, d), F32),
        compiler_params=pltpu.CompilerParams(dimension_semantics=("arbitrary",), vmem_limit_bytes=VMEM_LIMIT),
        name="experts",
    )(vb, ve, lo, hi, xs, wg, wu, wd)


def _final_body(*refs, seg_sizes):
    n_seg = len(seg_sizes)
    g_ref, o_ref = refs[4 * n_seg:]

    def run(x1_ref, y0_ref, y1_ref, rt_ref):
        tm = rt_ref.shape[1]
        rt = jnp.concatenate([rt_ref[...], jnp.zeros((LANE - ROUTE_ROWS, tm), F32)], axis=0).T
        x = x1_ref[...] + (rt[:, 2:3] * y0_ref[...] + rt[:, 3:4] * y1_ref[...])
        o_ref[...] = x * lax.rsqrt(jnp.mean(x * x, axis=-1, keepdims=True) + RMS_EPS) * g_ref[...]

    _run_segment(pl.program_id(0), seg_sizes, run, [refs[4 * s:4 * s + 4] for s in range(n_seg)])


def _final_call(sources, firsts, seg_sizes, g, *, tm):
    d = sources[0][0].shape[1]
    n_tiles = sum(seg_sizes)
    in_specs, start = [], 0
    for size, first in zip(seg_sizes, firsts):
        src = lambda i, s=start, f=first, n=size: (f + jnp.clip(i - s, 0, n - 1), 0)
        in_specs += [pl.BlockSpec((tm, d), src)] * 3 + [pl.BlockSpec((ROUTE_ROWS, tm), src)]
        start += size
    return pl.pallas_call(
        functools.partial(_final_body, seg_sizes=tuple(seg_sizes)),
        grid=(n_tiles,),
        in_specs=in_specs + [pl.BlockSpec((1, d), lambda i: (0, 0))],
        out_specs=pl.BlockSpec((tm, d), lambda i: (i, 0)),
        out_shape=jax.ShapeDtypeStruct((n_tiles * tm, d), F32),
        compiler_params=pltpu.CompilerParams(dimension_semantics=("arbitrary",), vmem_limit_bytes=VMEM_LIMIT),
        name="final",
    )(*[a for src in sources for a in src], g)


def _dispatch(flat_e, n, tb):
    n_assign = n * TOP_K_INNER
    idx_bits = (n_assign - 1).bit_length()
    assert N_EXPERTS << idx_bits < 2 ** 31 and n_assign % tb == 0
    ids = jnp.arange(n_assign, dtype=jnp.int32)
    sorted_key = jnp.sort((flat_e << idx_bits) | ids)
    sorted_id = sorted_key & ((1 << idx_bits) - 1)
    src_tok = sorted_id - n * (sorted_id // n)

    experts = jnp.arange(N_EXPERTS, dtype=jnp.int32)
    onehot = (flat_e[:, None] == experts[None, :]).astype(jnp.int32)
    csum = jnp.cumsum(onehot, axis=0)
    counts = csum[-1]
    ends = jnp.cumsum(counts)
    starts = ends - counts
    dest = (jnp.sum((csum - 1 + starts[None, :]) * onehot, axis=1)).astype(jnp.int32)

    n_blocks = n_assign // tb
    n_visits = n_blocks + N_EXPERTS - 1
    first_blk = starts // tb
    n_vis_e = jnp.where(counts > 0, (ends + tb - 1) // tb - first_blk, 0)
    v_end = jnp.cumsum(n_vis_e)
    v = jnp.arange(n_visits, dtype=jnp.int32)
    valid = v < v_end[-1]
    last_e = jnp.max(jnp.where(counts > 0, experts, 0))
    ve = jnp.where(valid, jnp.sum((v_end[None, :] <= v[:, None]).astype(jnp.int32), axis=1), last_e)
    pick = lambda a: jnp.sum(jnp.where(ve[:, None] == experts[None, :], a[None, :], 0), axis=1)
    vb = pick(first_blk) + v - pick(v_end - n_vis_e)
    lo = jnp.maximum(pick(starts), vb * tb) - vb * tb
    hi = jnp.minimum(pick(ends), (vb + 1) * tb) - vb * tb
    vb = jnp.where(valid, vb, n_blocks - 1)
    lo = jnp.where(valid, lo, 0)
    hi = jnp.where(valid, hi, 0)
    return dest, src_tok, (vb.astype(jnp.int32), ve.astype(jnp.int32), lo.astype(jnp.int32), hi.astype(jnp.int32))


def _q_perm():
    cols = []
    for g in range(B_GROUP):
        for kv in range(B_KV_HEADS):
            h = kv * B_GROUP + g
            cols.extend(range(h * HEAD_DIM, (h + 1) * HEAD_DIM))
    return jnp.array(cols, dtype=jnp.int32)


def _layer(x_prompt, x_sample, state_rwkv, state_shift, cache_win_k, cache_win_v, prm, norm_final_g, *,
           chunk, n_sub, moe_block):
    bp, tp, d = x_prompt.shape
    bs, ts, _ = x_sample.shape
    n_p, n_s = bp * tp, bs * ts
    ts_pad = 8
    tm = n_s
    assert tp % tm == 0 and tm % 8 == 0 and cache_win_k.shape[1] == WINDOW
    xp = x_prompt.reshape(n_p, d)
    xs = x_sample.reshape(n_s, d)

    qp = _q_perm()
    w_in = prm["w_in"]
    w_in = jnp.concatenate([w_in[:, :SHIFT_W], w_in[:, SHIFT_W:SHIFT_W + B_WIDTH][:, qp],
                            w_in[:, SHIFT_W + B_WIDTH:]], axis=1).astype(BF16)
    gn = prm["attn_norm_g"][qp][None]
    w_out = prm["w_out"]
    wa, wb = w_out[:A_WIDTH].astype(BF16), w_out[A_WIDTH:][qp].astype(BF16)
    pad_rows_r = jnp.zeros((LOGIT_ROWS - N_GROUPS - N_EXPERTS, d), F32)
    wr = jnp.concatenate([prm["w_route_group"].T, prm["w_route_expert"].T, pad_rows_r], axis=0).astype(BF16)
    br = jnp.concatenate([prm["b_route_group"], prm["b_route_expert"], pad_rows_r[:, 0]])[:, None]

    zero_blk = jnp.zeros((LORA_W // 2, A_WIDTH), F32)
    wlora = jnp.concatenate([jnp.concatenate([prm["w_decay_up"], zero_blk], axis=1),
                             jnp.concatenate([zero_blk, prm["w_iclr_up"]], axis=1)], axis=0).astype(BF16)
    rp = dict(mu=prm["mu_shift"][None], w0=prm["w_decay0"][None], a0=prm["w_iclr0"][None], wlora=wlora,
              wgate=prm["w_gate_up"].astype(BF16), kk=prm["k_k"][None], ka=prm["k_a"][None],
              rk=prm["r_k"].reshape(1, A_WIDTH), lng=prm["lnx_g"][None], lnb=prm["lnx_b"][None])

    groups = [(0, bp)]
    tiles_per_seq = tp // tm
    nb = tp // WINDOW
    n_qb = _row_tile(nb, 4)
    kv4 = lambda a: a.reshape(a.shape[0], a.shape[1], B_KV_HEADS, HEAD_DIM)
    finals, s_p, sh_p, k_p, v_p = [], [], [], [], []
    for gi, (s0, s1) in enumerate(groups):
        has_sample = gi == len(groups) - 1
        n_seq = s1 - s0
        n_pt = n_seq * tiles_per_seq
        n_pg = n_pt * tm
        seg_sizes = [n_pt] + ([1] if has_sample else [])
        pa, q, k, v = _inproj_call([xp] + ([xs] if has_sample else []), [s0 * tiles_per_seq, 0], seg_sizes,
                                   prm["norm_mix_g"][None], w_in, tm=tm)

        oa_p, s_pg = _rwkv_call(pa, jnp.zeros((n_seq, 1, SHIFT_W), F32),
                                jnp.zeros((n_seq,) + state_rwkv.shape[1:], F32), rp, n_seq=n_seq,
                                n_chunks=tp // chunk, chunk=chunk, n_sub=n_sub, n_bat=1, t_real=chunk)
        ob_p, = _swa_call(prm["attn_sink"], q, k.reshape(-1, WINDOW, B_KV_WIDTH), v.reshape(-1, WINDOW, B_KV_WIDTH),
                          k, v, gn, n_seq=n_seq, n_blk=nb // n_qb, tq=WINDOW, n_bat=1, n_qb=n_qb,
                          first_has_prev=False, out_dtype=BF16,
                          prev_map=lambda b, i: (b * nb + jnp.maximum(i * n_qb - 1, 0), 0, 0))
        oa_list, ob_list = [oa_p], [ob_p]
        if has_sample:
            pad_rows = lambda a: jnp.pad(a[n_pg:].reshape(bs, ts, -1), ((0, 0), (0, ts_pad - ts), (0, 0))).reshape(
                bs * ts_pad, -1)
            oa_s, s_s = _rwkv_call(pad_rows(pa), state_shift[:, None, :], state_rwkv, rp, n_seq=bs,
                                   n_chunks=1, chunk=ts_pad, n_sub=1, n_bat=_row_tile(bs, 16), t_real=ts)
            ob_s, k_s, v_s = _swa_call(prm["attn_sink"], pad_rows(q).astype(F32),
                                       cache_win_k.reshape(bs, WINDOW, B_KV_WIDTH),
                                       cache_win_v.reshape(bs, WINDOW, B_KV_WIDTH), pad_rows(k), pad_rows(v), gn,
                                       n_seq=bs, n_blk=1, tq=ts_pad, n_bat=_row_tile(bs, 16), first_has_prev=True,
                                       prev_map=lambda b, i: (b, 0, 0), out_dtype=F32, cache_rows=ts)
            oa_list.append(oa_s.reshape(bs, ts_pad, A_WIDTH)[:, :ts].reshape(n_s, A_WIDTH))
            ob_list.append(ob_s.reshape(bs, ts_pad, B_WIDTH)[:, :ts].reshape(n_s, B_WIDTH))
            sh_s = pa[n_pg:].reshape(bs, ts, SHIFT_W)[:, -1]

        x1, h2, route = _outproj_call([xp] + ([xs] if has_sample else []), [s0 * tiles_per_seq, 0],
                                               oa_list, ob_list, seg_sizes, wa, wb, prm["norm_ffn_g"][None], wr, br,
                                               tm=tm)
        n = sum(seg_sizes) * tm
        route_t = route.reshape(n // tm, ROUTE_ROWS, tm)
        flat_e = jnp.concatenate([route_t[:, j, :].reshape(n) for j in range(TOP_K_INNER)]).astype(jnp.int32)
        dest, src_tok, visits = _dispatch(flat_e, n, moe_block)
        ybuf = _expert_call(visits, h2[src_tok], prm["w_exp_gate"], prm["w_exp_up"], prm["w_exp_down"], tb=moe_block)
        finals.append((x1, ybuf[dest[:n]], ybuf[dest[n:]], route))

        last_rows = lambda a, m: jnp.stack([a[(b + 1) * tp - m:(b + 1) * tp] for b in range(n_seq)], axis=0)
        s_p.append(s_pg)
        sh_p.append(last_rows(pa, 1)[:, 0])
        k_p.append(last_rows(k, WINDOW))
        v_p.append(last_rows(v, WINDOW))

    gf = norm_final_g[None]
    n_pts = [(s1 - s0) * tiles_per_seq for s0, s1 in groups]
    y_p = _final_call(finals, [0] * len(groups), n_pts, gf, tm=tm)
    y_s = _final_call(finals[-1:], n_pts[-1:], [1], gf, tm=tm)
    cat = lambda parts: jnp.concatenate(parts, axis=0)
    return (y_p.reshape(bp, tp, d), y_s.reshape(bs, ts, d), cat(s_p), cat(sh_p), kv4(cat(k_p)), kv4(cat(v_p)),
            s_s, sh_s, kv4(k_s), kv4(v_s))


def kernel(x_prompt, x_sample, state_rwkv, state_shift, cache_win_k, cache_win_v, norm_mix_g, w_in, mu_shift, w_decay0, w_decay_up, w_iclr0, w_iclr_up, w_gate_up, k_k, k_a, r_k, lnx_g, lnx_b, attn_sink, attn_norm_g, w_out, norm_ffn_g, w_route_group, b_route_group, w_route_expert, b_route_expert, w_exp_gate, w_exp_up, w_exp_down, norm_final_g):
    assert norm_mix_g.shape[0] == 1, "single-layer trunk"
    prm = dict(norm_mix_g=norm_mix_g[0], w_in=w_in[0], mu_shift=mu_shift[0], w_decay0=w_decay0[0],
               w_decay_up=w_decay_up[0], w_iclr0=w_iclr0[0], w_iclr_up=w_iclr_up[0], w_gate_up=w_gate_up[0],
               k_k=k_k[0], k_a=k_a[0], r_k=r_k[0], lnx_g=lnx_g[0], lnx_b=lnx_b[0], attn_sink=attn_sink[0],
               attn_norm_g=attn_norm_g[0], w_out=w_out[0], norm_ffn_g=norm_ffn_g[0],
               w_route_group=w_route_group[0], b_route_group=b_route_group[0],
               w_route_expert=w_route_expert[0], b_route_expert=b_route_expert[0],
               w_exp_gate=w_exp_gate[0], w_exp_up=w_exp_up[0], w_exp_down=w_exp_down[0])
    outs = _layer(x_prompt, x_sample, state_rwkv[0], state_shift[0], cache_win_k[0], cache_win_v[0], prm,
                  norm_final_g, chunk=64, n_sub=8, moe_block=512)
    y_p, y_s, s_p, sh_p, kp, vp, s_s, sh_s, ks, vs = outs
    return (y_p, y_s, s_p[None], sh_p[None], kp[None], vp[None], s_s[None], sh_s[None], ks[None], vs[None])
```

```python
import functools
import math

import jax
import jax.numpy as jnp
from jax import lax
from jax.experimental import pallas as pl
from jax.experimental.pallas import tpu as pltpu

F32 = jnp.float32
BF16 = jnp.bfloat16

HEAD_DIM = 64
A_HEADS = 8
A_WIDTH = A_HEADS * HEAD_DIM
B_HEADS = 8
B_KV_HEADS = 2
B_GROUP = B_HEADS // B_KV_HEADS
B_WIDTH = B_HEADS * HEAD_DIM
B_KV_WIDTH = B_KV_HEADS * HEAD_DIM
DECAY_LORA = 64
ICLR_LORA = 64
GATE_LORA = 128
LORA_W = DECAY_LORA + ICLR_LORA
SHIFT_W = 3 * A_WIDTH + LORA_W + GATE_LORA
IN_W = SHIFT_W + B_WIDTH + 2 * B_KV_WIDTH
WINDOW = 128
N_GROUPS = 4
EXPERTS_PER_GROUP = 4
N_EXPERTS = N_GROUPS * EXPERTS_PER_GROUP
TOP_K_INNER = 2
EXPERT_FF = 512
RMS_EPS = 1e-6
LNX_EPS = 64e-5
DECAY_OFFSET = 0.5

LANE = 128
PAIR = LANE // HEAD_DIM
N_PAIRS = A_HEADS // PAIR
ROUTE_ROWS = 8
LOGIT_ROWS = 32
MASK_NEG = -1e30
VMEM_LIMIT = 48 * 1024 * 1024

LOG2_E = math.log2(math.e)
DECAY_SCALE = math.exp(-DECAY_OFFSET)
HEAD_SHIFT = HEAD_DIM.bit_length() - 1
HI = lax.Precision.HIGHEST


def _row_tile(n, cap):
    t = cap
    while n % t:
        t //= 2
    return t


def _dot(a, b, precision=None):
    return jnp.dot(a, b, preferred_element_type=F32, precision=precision)


def _dot_nt(a, b, precision=None):
    return lax.dot_general(a, b, (((1,), (1,)), ((), ())), preferred_element_type=F32, precision=precision)


def _dot_tn(a, b, precision=None):
    return lax.dot_general(a, b, (((0,), (0,)), ((), ())), preferred_element_type=F32, precision=precision)


def _sigmoid(x):
    return 1.0 / (1.0 + jnp.exp(-x))


def _hi_lo(x, axis):
    hi = x.astype(BF16)
    lo = (x - hi.astype(F32)).astype(BF16)
    return jnp.concatenate([hi, lo], axis=axis)


def _run_segment(i, seg_sizes, run, seg_refs):
    start = 0
    for size, refs in zip(seg_sizes, seg_refs):
        pl.when((i >= start) & (i < start + size))(functools.partial(run, *refs))
        start += size


def _seg_specs(tm, width, seg_sizes, firsts):
    specs, start = [], 0
    for size, first in zip(seg_sizes, firsts):
        specs.append(pl.BlockSpec((tm, width),
                                  lambda i, s=start, f=first, n=size: (f + jnp.clip(i - s, 0, n - 1), 0)))
        start += size
    return specs


def _inproj_body(*refs, seg_sizes):
    n_seg = len(seg_sizes)
    g_ref, w_ref, pa_ref, q_ref, k_ref, v_ref = refs[n_seg:]

    def run(x_ref):
        x = x_ref[...]
        h = x * lax.rsqrt(jnp.mean(x * x, axis=-1, keepdims=True) + RMS_EPS) * g_ref[...]
        p = _dot(h.astype(BF16), w_ref[...])
        pa_ref[...] = p[:, :SHIFT_W]
        q_ref[...] = (p[:, SHIFT_W:SHIFT_W + B_WIDTH] * (HEAD_DIM ** -0.5)).astype(BF16)
        k_ref[...] = p[:, SHIFT_W + B_WIDTH:SHIFT_W + B_WIDTH + B_KV_WIDTH]
        v_ref[...] = p[:, SHIFT_W + B_WIDTH + B_KV_WIDTH:]

    _run_segment(pl.program_id(0), seg_sizes, run, [(r,) for r in refs[:n_seg]])


def _inproj_call(xs_list, firsts, seg_sizes, g, w_bf16, *, tm):
    d = xs_list[0].shape[1]
    n_tiles = sum(seg_sizes)
    n = n_tiles * tm
    row = lambda i: (i, 0)
    fixed = lambda i: (0, 0)
    return pl.pallas_call(
        functools.partial(_inproj_body, seg_sizes=tuple(seg_sizes)),
        grid=(n_tiles,),
        in_specs=_seg_specs(tm, d, seg_sizes, firsts) + [pl.BlockSpec((1, d), fixed), pl.BlockSpec((d, IN_W), fixed)],
        out_specs=[pl.BlockSpec((tm, SHIFT_W), row), pl.BlockSpec((tm, B_WIDTH), row),
                   pl.BlockSpec((tm, B_KV_WIDTH), row), pl.BlockSpec((tm, B_KV_WIDTH), row)],
        out_shape=[jax.ShapeDtypeStruct((n, SHIFT_W), F32), jax.ShapeDtypeStruct((n, B_WIDTH), BF16),
                   jax.ShapeDtypeStruct((n, B_KV_WIDTH), F32), jax.ShapeDtypeStruct((n, B_KV_WIDTH), F32)],
        compiler_params=pltpu.CompilerParams(dimension_semantics=("arbitrary",), vmem_limit_bytes=VMEM_LIMIT),
        name="inproj",
    )(*xs_list, g, w_bf16)


def _rwkv_body(pa_ref, shift0_ref, s0_ref, mu_ref, w0_ref, a0_ref, wlora_ref, wgate_ref, kk_ref, ka_ref,
               rk_ref, lng_ref, lnb_ref, oa_ref, sout_ref, s_scr, prev_scr, *, chunk, n_sub, n_bat, t_real):
    C = chunk
    seq_rows = n_sub * C
    n_seg = n_bat * n_sub
    rows = n_seg * C
    c = pl.program_id(1)

    @pl.when(c == 0)
    def _():
        prev_scr[...] = shift0_ref[...]
        for b in range(n_bat):
            for j in range(N_PAIRS):
                s_scr[b * N_PAIRS + j] = jnp.concatenate([s0_ref[b, PAIR * j + h] for h in range(PAIR)], axis=1)

    pa = pa_ref[...]
    row = lax.broadcasted_iota(jnp.int32, (rows, 1), 0)
    row_in_chunk = row & (C - 1)
    pa_prev = pltpu.roll(pa, 1, axis=0)
    for b in range(n_bat):
        pa_prev = jnp.where(row == b * seq_rows, prev_scr[b], pa_prev)
        last = (b + 1) * seq_rows - C + t_real - 1
        prev_scr[b] = pa[last:last + 1]
    xm = pa + mu_ref[...] * (pa_prev - pa)

    r = xm[:, :A_WIDTH]
    k = xm[:, A_WIDTH:2 * A_WIDTH]
    v = xm[:, 2 * A_WIDTH:3 * A_WIDTH]
    lora_in = xm[:, 3 * A_WIDTH:3 * A_WIDTH + LORA_W]
    gd = xm[:, 3 * A_WIDTH + LORA_W:]

    lane = lax.broadcasted_iota(jnp.int32, (1, LANE), 1)
    lo_half = lane < HEAD_DIM
    z = jnp.where(lo_half, jnp.tanh(lora_in), lora_in)
    lw = _dot(z.astype(BF16), wlora_ref[...])
    dec_pre = w0_ref[...] + lw[:, :A_WIDTH]
    a = _sigmoid(a0_ref[...] + lw[:, A_WIDTH:])
    logdec = (-LOG2_E * DECAY_SCALE) * _sigmoid(dec_pre)
    g = _dot(_sigmoid(gd).astype(BF16), wgate_ref[...])

    ri = lax.broadcasted_iota(jnp.int32, (LANE, LANE), 0)
    ci = lax.broadcasted_iota(jnp.int32, (LANE, LANE), 1)
    same_head = (ri >> HEAD_SHIFT) == (ci >> HEAD_SHIFT)
    seg = jnp.where(same_head, 1.0, 0.0).astype(BF16)
    seg2 = jnp.concatenate([seg, seg], axis=0)

    def headsum(x):
        return jnp.concatenate([_dot(_hi_lo(x[:, j * LANE:(j + 1) * LANE], 1), seg2) for j in range(N_PAIRS)],
                               axis=1)

    kk = k * kk_ref[...]
    kk = kk * lax.rsqrt(jnp.maximum(headsum(kk * kk), 1e-24))
    k = k * (1.0 + (a - 1.0) * ka_ref[...])
    bonus = headsum(r * k * rk_ref[...]) * v

    if t_real < C:
        valid = row_in_chunk < t_real
        logdec = jnp.where(valid, logdec, 0.0)
        kk = jnp.where(valid, kk, 0.0)
        k = jnp.where(valid, k, 0.0)
        v = jnp.where(valid, v, 0.0)

    cum = logdec
    shift = 1
    while shift < C:
        cum = cum + jnp.where(row_in_chunk >= shift, pltpu.roll(cum, shift, axis=0), 0.0)
        shift *= 2
    ends = [cum[(s + 1) * C - 1:(s + 1) * C] for s in range(n_seg)]
    cum_end = jnp.concatenate([jnp.broadcast_to(e, (C, A_WIDTH)) for e in ends], axis=0) if n_seg > 1 else ends[0]
    w_incl = jnp.exp2(cum)
    w_prev = jnp.exp2(cum - logdec)
    w_inv = jnp.exp2(-cum)
    w_end = jnp.exp2(cum_end - cum)
    w_chunk = [jnp.exp2(e) for e in ends]
    kka = kk * a
    terms = dict(A=-kk * w_prev, R=r * w_incl, B=kka * w_inv, K=k * w_inv, V=v, Be=kka * w_end, Ke=k * w_end)

    C2 = PAIR * C
    t_idx = lax.broadcasted_iota(jnp.int32, (C, 1), 0)
    i_idx = lax.broadcasted_iota(jnp.int32, (1, C2), 1) & (C - 1)
    strict = i_idx < t_idx
    incl = i_idx <= t_idx
    ident = jnp.where(i_idx == t_idx, 1.0, 0.0).astype(F32)
    lo_time = lax.broadcasted_iota(jnp.int32, (1, C2), 1) < C
    n_levels = max(1, (C - 1).bit_length())
    bf = lambda x: x.astype(BF16)

    def bd(x, lo_mask=lo_half):
        return bf(jnp.concatenate([jnp.where(lo_mask, x, 0.0), jnp.where(lo_mask, 0.0, x)], axis=0))

    chains = [(s, j) for s in range(n_seg) for j in range(N_PAIRS)]
    tile = lambda name, s, j: terms[name][s * C:(s + 1) * C, j * LANE:(j + 1) * LANE]
    r_sbs = {ch: tile("R", *ch) for ch in chains}
    stk = {ch: {nm: bd(tile(nm, *ch)) for nm in ("A", "B", "K", "V")} for ch in chains}
    m_ab, aak, arb, ark = {}, {}, {}, {}
    for ch in chains:
        t = stk[ch]
        a_sbs, rb = bf(tile("A", *ch)), bf(r_sbs[ch])
        if C2 % LANE == 0:
            m1 = _dot_nt(jnp.concatenate([a_sbs, rb], axis=0), jnp.concatenate([t["B"], t["K"]], axis=0))
            ab, ak, rbm, rk = m1[:C, :C2], m1[:C, C2:], m1[C:, :C2], m1[C:, C2:]
        else:
            ab, ak = _dot_nt(a_sbs, t["B"]), _dot_nt(a_sbs, t["K"])
            rbm, rk = _dot_nt(rb, t["B"]), _dot_nt(rb, t["K"])
        m_ab[ch] = jnp.where(strict, ab, 0.0)
        aak[ch] = jnp.where(strict, ak, 0.0)
        arb[ch] = bf(jnp.where(incl, rbm, 0.0))
        ark[ch] = jnp.where(incl, rk, 0.0)

    nn = dict(m_ab)
    tinv = {ch: ident + m_ab[ch] for ch in chains}
    for lvl in range(1, n_levels):
        if lvl == 1:
            for ch in chains:
                nn[ch] = _dot(bf(nn[ch]), bd(nn[ch], lo_time))
            continue
        for ch in chains:
            both = _dot(bf(jnp.concatenate([nn[ch], tinv[ch]], axis=0)), bd(nn[ch], lo_time))
            nn[ch] = both[:C]
            tinv[ch] = tinv[ch] + both[C:]
    if n_levels > 1:
        for ch in chains:
            tinv[ch] = tinv[ch] + _dot(bf(tinv[ch]), bd(nn[ch], lo_time))
    av = {ch: _dot(bf(jnp.concatenate([aak[ch], ark[ch]], axis=0)), stk[ch]["V"]) for ch in chains}
    x = {ch: _dot(bf(tinv[ch]), jnp.concatenate([stk[ch]["A"], bd(av[ch][:C])], axis=1)) for ch in chains}
    a_eff = {ch: bd(x[ch][:, :LANE]) for ch in chains}
    v_eff = {ch: bd(x[ch][:, LANE:]) for ch in chains}
    zed = {ch: _dot(arb[ch], jnp.concatenate([a_eff[ch], v_eff[ch]], axis=1)) for ch in chains}
    rhat = {ch: bf(r_sbs[ch] + zed[ch][:, :LANE]) for ch in chains}
    y0 = {ch: av[ch][C:] + zed[ch][:, LANE:] for ch in chains}
    p_mat = {ch: bf(jnp.where(same_head, _dot_tn(bf(tile("Be", *ch)), bf(x[ch][:, :LANE])), 0.0))
             for ch in chains}
    q_full = {ch: _dot_tn(bf(jnp.concatenate([x[ch][:, LANE:], tile("V", *ch)], axis=0)),
                          bf(jnp.concatenate([tile("Be", *ch), tile("Ke", *ch)], axis=0))) for ch in chains}

    y_rows = []
    for b in range(n_bat):
        state = [s_scr[b * N_PAIRS + j] for j in range(N_PAIRS)]
        for sub in range(n_sub):
            s = b * n_sub + sub
            ys = []
            for j in range(N_PAIRS):
                ch = (s, j)
                hi = bf(state[j])
                lo = bf(state[j] - hi.astype(F32))
                yy = _dot_nt(rhat[ch], jnp.concatenate([bd(hi), bd(lo)], axis=0))
                ys.append(yy[:, :LANE] + yy[:, LANE:] + y0[ch])
                sp = _dot_nt(jnp.concatenate([hi, lo], axis=0), p_mat[ch])
                q_sbs = jnp.where(lo_half, q_full[ch][:HEAD_DIM], q_full[ch][HEAD_DIM:])
                state[j] = state[j] * w_chunk[s][:, j * LANE:(j + 1) * LANE] + sp[:HEAD_DIM] + sp[HEAD_DIM:] + q_sbs
            y_rows.append(jnp.concatenate(ys, axis=1))
        for j in range(N_PAIRS):
            s_scr[b * N_PAIRS + j] = state[j]

    y = jnp.concatenate(y_rows, axis=0) if n_seg > 1 else y_rows[0]
    inv_n = 1.0 / HEAD_DIM
    mean = headsum(y) * inv_n
    d = y - mean
    var = headsum(d * d) * inv_n
    yn = d * lax.rsqrt(var + LNX_EPS) * lng_ref[...] + lnb_ref[...]
    oa_ref[...] = ((yn + bonus) * g).astype(oa_ref.dtype)

    @pl.when(c == pl.num_programs(1) - 1)
    def _():
        for b in range(n_bat):
            for j in range(N_PAIRS):
                for h in range(PAIR):
                    sout_ref[b, PAIR * j + h] = s_scr[b * N_PAIRS + j][:, h * HEAD_DIM:(h + 1) * HEAD_DIM]


def _rwkv_call(pa2d, shift0, s0_pairs, prm, *, n_seq, n_chunks, chunk, n_sub, n_bat, t_real):
    n_steps = n_chunks // n_sub
    assert n_seq % n_bat == 0 and (n_bat == 1 or n_steps == 1)
    row = lambda b, c: (b * n_steps + c, 0)
    seq3 = lambda b, c: (b, 0, 0)
    fixed = lambda b, c: (0, 0)
    vec = lambda w: pl.BlockSpec((1, w), fixed)
    body = functools.partial(_rwkv_body, chunk=chunk, n_sub=n_sub, n_bat=n_bat, t_real=t_real)
    n_rows = n_seq * n_chunks * chunk
    blk_rows = n_bat * n_sub * chunk
    state_blk = (n_bat, A_HEADS, HEAD_DIM, HEAD_DIM)
    seq4 = lambda b, c: (b, 0, 0, 0)
    return pl.pallas_call(
        body,
        grid=(n_seq // n_bat, n_steps),
        in_specs=[pl.BlockSpec((blk_rows, SHIFT_W), row),
                  pl.BlockSpec((n_bat, 1, SHIFT_W), seq3),
                  pl.BlockSpec(state_blk, seq4),
                  vec(SHIFT_W), vec(A_WIDTH), vec(A_WIDTH),
                  pl.BlockSpec((LORA_W, 2 * A_WIDTH), fixed), pl.BlockSpec((GATE_LORA, A_WIDTH), fixed),
                  vec(A_WIDTH), vec(A_WIDTH), vec(A_WIDTH), vec(A_WIDTH), vec(A_WIDTH)],
        out_specs=[pl.BlockSpec((blk_rows, A_WIDTH), row), pl.BlockSpec(state_blk, seq4)],
        out_shape=[jax.ShapeDtypeStruct((n_rows, A_WIDTH), BF16),
                   jax.ShapeDtypeStruct((n_seq, A_HEADS, HEAD_DIM, HEAD_DIM), F32)],
        scratch_shapes=[pltpu.VMEM((n_bat * N_PAIRS, HEAD_DIM, LANE), F32), pltpu.VMEM((n_bat, 1, SHIFT_W), F32)],
        compiler_params=pltpu.CompilerParams(dimension_semantics=("arbitrary", "arbitrary"),
                                             vmem_limit_bytes=VMEM_LIMIT),
        name="rwkv",
    )(pa2d, shift0, s0_pairs, prm["mu"], prm["w0"], prm["a0"], prm["wlora"], prm["wgate"], prm["kk"], prm["ka"],
      prm["rk"], prm["lng"], prm["lnb"])


def _swa_bias(tq):
    rows = B_GROUP * tq
    grp = jnp.arange(rows, dtype=jnp.int32)[:, None] // tq
    t = jnp.arange(rows, dtype=jnp.int32)[:, None] % tq
    dist_p = t + WINDOW - jnp.arange(WINDOW, dtype=jnp.int32)[None, :]
    dist_c = t - jnp.arange(tq, dtype=jnp.int32)[None, :]

    def bias(dist, kv):
        slope = sum(jnp.where(grp == g, 2.0 ** -(kv * B_GROUP + g + 1), 0.0) for g in range(B_GROUP))
        return jnp.where((dist >= 0) & (dist < WINDOW), -slope * dist.astype(F32), MASK_NEG)

    return (jnp.stack([bias(dist_p, kv) for kv in range(B_KV_HEADS)]),
            jnp.stack([bias(dist_c, kv) for kv in range(B_KV_HEADS)]))


def _swa_body(sink_ref, q_ref, kp_ref, vp_ref, kc_ref, vc_ref, bp_ref, bc_ref, gn_ref, o_ref, *cache_refs, tq, n_bat,
              n_qb, t_real, first_has_prev):
    nblk = pl.program_id(1)
    lane = lax.broadcasted_iota(jnp.int32, (1, LANE), 1)
    kv_masks = [(lane < HEAD_DIM) if kv == 0 else (lane >= HEAD_DIM) for kv in range(B_KV_HEADS)]
    blocks = []
    for s in range(n_bat * n_qb):
        rs = slice(s * tq, (s + 1) * tq)
        kc, vc = kc_ref[rs], vc_ref[rs]
        if n_qb > 1 and s > 0:
            ps = slice((s - 1) * tq, s * tq)
            kp, vp, has_prev = kc_ref[ps], vc_ref[ps], True
        else:
            kp, vp, has_prev = kp_ref[s], vp_ref[s], first_has_prev
        if cache_refs:
            row = lax.broadcasted_iota(jnp.int32, (WINDOW, 1), 0)
            for out_ref, old, new in zip(cache_refs, (kp, vp), (kc, vc)):
                new_tail = jnp.concatenate([pltpu.roll(new, tq - t_real, axis=0)] * (WINDOW // tq), axis=0)
                out_ref[s] = jnp.where(row >= WINDOW - t_real, new_tail, pltpu.roll(old, WINDOW - t_real, axis=0))
        pen = 0.0 if has_prev else jnp.where(nblk > 0, 0.0, MASK_NEG)
        blocks.append(dict(rs=rs, q=q_ref[rs], kp=kp.astype(BF16), kc=kc.astype(BF16), vp=vp, vc=vc, pen=pen))

    keys = [(b, kv) for b in range(len(blocks)) for kv in range(B_KV_HEADS)]
    s_p, s_c, vpm, vcm = {}, {}, {}, {}
    for b, kv in keys:
        blk, mk = blocks[b], kv_masks[kv]
        q_st = jnp.concatenate([jnp.where(mk, blk["q"][:, g * LANE:(g + 1) * LANE], 0.0) for g in range(B_GROUP)],
                               axis=0).astype(BF16)
        s_p[b, kv] = _dot_nt(q_st, blk["kp"])
        s_c[b, kv] = _dot_nt(q_st, blk["kc"])
        vpm[b, kv] = jnp.where(mk, blk["vp"], 0.0).astype(BF16)
        vcm[b, kv] = jnp.where(mk, blk["vc"], 0.0).astype(BF16)

    slabs = [(b, kv, g) for b, kv in keys for g in range(B_GROUP)]
    sink = {(kv, g): sink_ref[kv * B_GROUP + g] for kv in range(B_KV_HEADS) for g in range(B_GROUP)}
    sp, sc, m = {}, {}, {}
    for b, kv, g in slabs:
        rs = slice(g * tq, (g + 1) * tq)
        sp[b, kv, g] = s_p[b, kv][rs] + (bp_ref[kv, rs, :] + blocks[b]["pen"])
        sc[b, kv, g] = s_c[b, kv][rs] + bc_ref[kv, rs, :]
        if tq == WINDOW:
            row_max = jnp.max(jnp.maximum(sp[b, kv, g], sc[b, kv, g]), axis=-1, keepdims=True)
        else:
            row_max = jnp.maximum(jnp.max(sp[b, kv, g], axis=-1, keepdims=True),
                                  jnp.max(sc[b, kv, g], axis=-1, keepdims=True))
        m[b, kv, g] = jnp.maximum(row_max, sink[kv, g])
    e_p, e_c, e_sum = {}, {}, {}
    for key in slabs:
        ep, ec = jnp.exp(sp[key] - m[key]), jnp.exp(sc[key] - m[key])
        if tq == WINDOW:
            e_sum[key] = jnp.sum(ep + ec, axis=-1, keepdims=True)
        else:
            e_sum[key] = jnp.sum(ep, axis=-1, keepdims=True) + jnp.sum(ec, axis=-1, keepdims=True)
        e_p[key], e_c[key] = ep.astype(BF16), ec.astype(BF16)

    tiles = {}
    for b, kv, g in slabs:
        inv = 1.0 / (e_sum[b, kv, g] + jnp.exp(sink[kv, g] - m[b, kv, g]))
        o = (_dot(e_p[b, kv, g], vpm[b, kv]) + _dot(e_c[b, kv, g], vcm[b, kv])) * inv
        tiles[b, g] = o if kv == 0 else tiles[b, g] + o

    for b, blk in enumerate(blocks):
        ssq = sum(jnp.sum(tiles[b, g] * tiles[b, g], axis=-1, keepdims=True) for g in range(B_GROUP))
        inv_rms = lax.rsqrt(ssq * (1.0 / B_WIDTH) + RMS_EPS)
        out = jnp.concatenate([tiles[b, g] for g in range(B_GROUP)], axis=1) * inv_rms * gn_ref[...]
        o_ref[blk["rs"]] = out.astype(o_ref.dtype)


def _swa_call(sink, q2d, kprev, vprev, kcur2d, vcur2d, gn, *, n_seq, n_blk, tq, n_bat, first_has_prev, prev_map,
              out_dtype, n_qb=1, cache_rows=0):
    assert n_seq % n_bat == 0 and (n_bat == 1 or n_blk * n_qb == 1) and (cache_rows == 0 or n_blk * n_qb == 1)
    row = lambda b, n: (b * n_blk + n, 0)
    body = functools.partial(_swa_body, tq=tq, n_bat=n_bat, n_qb=n_qb, t_real=cache_rows,
                             first_has_prev=first_has_prev)
    bias_p, bias_c = _swa_bias(tq)
    whole = lambda b, n: (0, 0, 0)
    blk_rows = n_bat * n_qb * tq
    out_specs = [pl.BlockSpec((blk_rows, B_WIDTH), row)]
    out_shape = [jax.ShapeDtypeStruct((n_seq * n_blk * n_qb * tq, B_WIDTH), out_dtype)]
    if cache_rows:
        out_specs += [pl.BlockSpec((n_bat, WINDOW, B_KV_WIDTH), prev_map)] * 2
        out_shape += [jax.ShapeDtypeStruct((n_seq, WINDOW, B_KV_WIDTH), F32)] * 2
    return pl.pallas_call(
        body,
        grid=(n_seq // n_bat, n_blk),
        in_specs=[pl.BlockSpec(memory_space=pltpu.SMEM),
                  pl.BlockSpec((blk_rows, B_WIDTH), row),
                  pl.BlockSpec((n_bat, WINDOW, B_KV_WIDTH), prev_map),
                  pl.BlockSpec((n_bat, WINDOW, B_KV_WIDTH), prev_map),
                  pl.BlockSpec((blk_rows, B_KV_WIDTH), row),
                  pl.BlockSpec((blk_rows, B_KV_WIDTH), row),
                  pl.BlockSpec(bias_p.shape, whole),
                  pl.BlockSpec(bias_c.shape, whole),
                  pl.BlockSpec((1, B_WIDTH), lambda b, n: (0, 0))],
        out_specs=out_specs,
        out_shape=out_shape,
        compiler_params=pltpu.CompilerParams(dimension_semantics=("arbitrary", "arbitrary"),
                                             vmem_limit_bytes=VMEM_LIMIT),
        name="swa",
    )(sink, q2d, kprev, vprev, kcur2d, vcur2d, bias_p, bias_c, gn)


def _route_cols(lg):
    row = lax.broadcasted_iota(jnp.int32, (LOGIT_ROWS, 1), 0)
    row_f = row.astype(F32)
    no_row = float(LOGIT_ROWS)
    is_group = row < N_GROUPS
    m_g = jnp.max(jnp.where(is_group, lg, MASK_NEG), axis=0, keepdims=True)
    g_idx = jnp.min(jnp.where(is_group & (lg == m_g), row_f, no_row), axis=0, keepdims=True)
    p_group = 1.0 / jnp.sum(jnp.where(is_group, jnp.exp(lg - m_g), 0.0), axis=0, keepdims=True)
    e_row = row - N_GROUPS
    in_group = (e_row >= 0) & (e_row < N_EXPERTS) & ((e_row >> 2).astype(F32) == g_idx)
    m_1 = jnp.max(jnp.where(in_group, lg, MASK_NEG), axis=0, keepdims=True)
    i_1 = jnp.min(jnp.where(in_group & (lg == m_1), row_f, no_row), axis=0, keepdims=True)
    rest = in_group & (row_f != i_1)
    m_2 = jnp.max(jnp.where(rest, lg, MASK_NEG), axis=0, keepdims=True)
    i_2 = jnp.min(jnp.where(rest & (lg == m_2), row_f, no_row), axis=0, keepdims=True)
    ratio = jnp.exp(m_2 - m_1)
    w_1 = p_group / (1.0 + ratio)
    out_row = lax.broadcasted_iota(jnp.int32, (ROUTE_ROWS, 1), 0)
    return jnp.where(out_row == 0, i_1 - N_GROUPS,
                     jnp.where(out_row == 1, i_2 - N_GROUPS,
                               jnp.where(out_row == 2, w_1, jnp.where(out_row == 3, w_1 * ratio, 0.0))))


def _outproj_body(*refs, seg_sizes):
    n_seg = len(seg_sizes)
    x_refs, oa_refs, ob_refs = refs[:n_seg], refs[n_seg:2 * n_seg], refs[2 * n_seg:3 * n_seg]
    wa_ref, wb_ref, g_ref, wr_ref, br_ref, x1_ref, h2_ref, rt_ref = refs[3 * n_seg:]

    def run(x_ref, oa_ref, ob_ref):
        x1 = x_ref[...] + _dot(oa_ref[...].astype(BF16), wa_ref[...]) + _dot(ob_ref[...].astype(BF16), wb_ref[...])
        x1_ref[...] = x1
        h2 = x1 * lax.rsqrt(jnp.mean(x1 * x1, axis=-1, keepdims=True) + RMS_EPS) * g_ref[...]
        h2b = h2.astype(BF16)
        h2_ref[...] = h2b
        rt_ref[...] = _route_cols(_dot_nt(wr_ref[...], h2b) + br_ref[...])

    _run_segment(pl.program_id(0), seg_sizes, run, list(zip(x_refs, oa_refs, ob_refs)))


def _outproj_call(x_list, x_firsts, oa_list, ob_list, seg_sizes, wa, wb, g, wr, br, *, tm):
    d = x_list[0].shape[1]
    n_tiles = sum(seg_sizes)
    n = n_tiles * tm
    row = lambda i: (i, 0)
    fixed = lambda i: (0, 0)
    zeros = [0] * len(seg_sizes)
    return pl.pallas_call(
        functools.partial(_outproj_body, seg_sizes=tuple(seg_sizes)),
        grid=(n_tiles,),
        in_specs=(_seg_specs(tm, d, seg_sizes, x_firsts) + _seg_specs(tm, A_WIDTH, seg_sizes, zeros)
                  + _seg_specs(tm, B_WIDTH, seg_sizes, zeros)
                  + [pl.BlockSpec((A_WIDTH, d), fixed), pl.BlockSpec((B_WIDTH, d), fixed), pl.BlockSpec((1, d), fixed),
                     pl.BlockSpec((LOGIT_ROWS, d), fixed), pl.BlockSpec((LOGIT_ROWS, 1), fixed)]),
        out_specs=[pl.BlockSpec((tm, d), row), pl.BlockSpec((tm, d), row), pl.BlockSpec((ROUTE_ROWS, tm), row)],
        out_shape=[jax.ShapeDtypeStruct((n, d), F32), jax.ShapeDtypeStruct((n, d), BF16),
                   jax.ShapeDtypeStruct((n_tiles * ROUTE_ROWS, tm), F32)],
        compiler_params=pltpu.CompilerParams(dimension_semantics=("arbitrary",), vmem_limit_bytes=VMEM_LIMIT),
        name="outproj",
    )(*x_list, *oa_list, *ob_list, wa, wb, g, wr, br)


def _expert_body(vb_ref, ve_ref, lo_ref, hi_ref, x_ref, wg_ref, wu_ref, wd_ref, y_ref, wg_bf, wu_bf, wd_bf):
    v = pl.program_id(0)
    pv = jnp.maximum(v - 1, 0)
    lo, hi = lo_ref[v], hi_ref[v]
    first_visit = (v == 0) | (vb_ref[v] != vb_ref[pv])

    @pl.when((v == 0) | (ve_ref[v] != ve_ref[pv]))
    def _():
        wg_bf[...] = wg_ref[0].astype(BF16)
        wu_bf[...] = wu_ref[0].astype(BF16)
        wd_bf[...] = wd_ref[0].astype(BF16)

    @pl.when(hi > lo)
    def _():
        x = x_ref[...]
        gate = _dot(x, wg_bf[...])
        up = _dot(x, wu_bf[...])
        mid = (gate * _sigmoid(gate) * up).astype(BF16)
        y = _dot(mid, wd_bf[...])
        row = lax.broadcasted_iota(jnp.int32, (y.shape[0], 1), 0)
        mine = (row >= lo) & (row < hi)

        @pl.when(first_visit)
        def _():
            y_ref[...] = jnp.where(mine, y, 0.0)

        @pl.when(jnp.logical_not(first_visit))
        def _():
            y_ref[...] = jnp.where(mine, y, y_ref[...])


def _expert_call(visits, xs, wg, wu, wd, *, tb):
    n_rows, d = xs.shape
    ff = wg.shape[-1]
    vb, ve, lo, hi = visits
    blk = lambda v, vb, ve, lo, hi: (vb[v], 0)
    wsel = lambda v, vb, ve, lo, hi: (ve[v], 0, 0)
    grid_spec = pltpu.PrefetchScalarGridSpec(
        num_scalar_prefetch=4,
        grid=(vb.shape[0],),
        in_specs=[pl.BlockSpec((tb, d), blk), pl.BlockSpec((1, d, ff), wsel), pl.BlockSpec((1, d, ff), wsel),
                  pl.BlockSpec((1, ff, d), wsel)],
        out_specs=pl.BlockSpec((tb, d), blk),
        scratch_shapes=[pltpu.VMEM((d, ff), BF16), pltpu.VMEM((d, ff), BF16), pltpu.VMEM((ff, d), BF16)],
    )
    return pl.pallas_call(
        _expert_body,
        grid_spec=grid_spec,
        out_shape=jax.ShapeDtypeStruct((n_rows, d), F32),
        compiler_params=pltpu.CompilerParams(dimension_semantics=("arbitrary",), vmem_limit_bytes=VMEM_LIMIT),
        name="experts",
    )(vb, ve, lo, hi, xs, wg, wu, wd)


def _final_body(*refs, seg_sizes):
    n_seg = len(seg_sizes)
    g_ref, o_ref = refs[4 * n_seg:]

    def run(x1_ref, y0_ref, y1_ref, rt_ref):
        tm = rt_ref.shape[1]
        rt = jnp.concatenate([rt_ref[...], jnp.zeros((LANE - ROUTE_ROWS, tm), F32)], axis=0).T
        x = x1_ref[...] + (rt[:, 2:3] * y0_ref[...] + rt[:, 3:4] * y1_ref[...])
        o_ref[...] = x * lax.rsqrt(jnp.mean(x * x, axis=-1, keepdims=True) + RMS_EPS) * g_ref[...]

    _run_segment(pl.program_id(0), seg_sizes, run, [refs[4 * s:4 * s + 4] for s in range(n_seg)])


def _final_call(sources, firsts, seg_sizes, g, *, tm):
    d = sources[0][0].shape[1]
    n_tiles = sum(seg_sizes)
    in_specs, start = [], 0
    for size, first in zip(seg_sizes, firsts):
        src = lambda i, s=start, f=first, n=size: (f + jnp.clip(i - s, 0, n - 1), 0)
        in_specs += [pl.BlockSpec((tm, d), src)] * 3 + [pl.BlockSpec((ROUTE_ROWS, tm), src)]
        start += size
    return pl.pallas_call(
        functools.partial(_final_body, seg_sizes=tuple(seg_sizes)),
        grid=(n_tiles,),
        in_specs=in_specs + [pl.BlockSpec((1, d), lambda i: (0, 0))],
        out_specs=pl.BlockSpec((tm, d), lambda i: (i, 0)),
        out_shape=jax.ShapeDtypeStruct((n_tiles * tm, d), F32),
        compiler_params=pltpu.CompilerParams(dimension_semantics=("arbitrary",), vmem_limit_bytes=VMEM_LIMIT),
        name="final",
    )(*[a for src in sources for a in src], g)


def _dispatch(flat_e, n, tb):
    n_assign = n * TOP_K_INNER
    idx_bits = (n_assign - 1).bit_length()
    assert N_EXPERTS << idx_bits < 2 ** 31 and n_assign % tb == 0
    ids = jnp.arange(n_assign, dtype=jnp.int32)
    sorted_key = jnp.sort((flat_e << idx_bits) | ids)
    sorted_id = sorted_key & ((1 << idx_bits) - 1)
    src_tok = sorted_id - n * (sorted_id // n)

    experts = jnp.arange(N_EXPERTS, dtype=jnp.int32)
    onehot = (flat_e[:, None] == experts[None, :]).astype(jnp.int32)
    csum = jnp.cumsum(onehot, axis=0)
    counts = csum[-1]
    ends = jnp.cumsum(counts)
    starts = ends - counts
    dest = (jnp.sum((csum - 1 + starts[None, :]) * onehot, axis=1)).astype(jnp.int32)

    n_blocks = n_assign // tb
    n_visits = n_blocks + N_EXPERTS - 1
    first_blk = starts // tb
    n_vis_e = jnp.where(counts > 0, (ends + tb - 1) // tb - first_blk, 0)
    v_end = jnp.cumsum(n_vis_e)
    v = jnp.arange(n_visits, dtype=jnp.int32)
    valid = v < v_end[-1]
    last_e = jnp.max(jnp.where(counts > 0, experts, 0))
    ve = jnp.where(valid, jnp.sum((v_end[None, :] <= v[:, None]).astype(jnp.int32), axis=1), last_e)
    pick = lambda a: jnp.sum(jnp.where(ve[:, None] == experts[None, :], a[None, :], 0), axis=1)
    vb = pick(first_blk) + v - pick(v_end - n_vis_e)
    lo = jnp.maximum(pick(starts), vb * tb) - vb * tb
    hi = jnp.minimum(pick(ends), (vb + 1) * tb) - vb * tb
    vb = jnp.where(valid, vb, n_blocks - 1)
    lo = jnp.where(valid, lo, 0)
    hi = jnp.where(valid, hi, 0)
    return dest, src_tok, (vb.astype(jnp.int32), ve.astype(jnp.int32), lo.astype(jnp.int32), hi.astype(jnp.int32))


def _q_perm():
    cols = []
    for g in range(B_GROUP):
        for kv in range(B_KV_HEADS):
            h = kv * B_GROUP + g
            cols.extend(range(h * HEAD_DIM, (h + 1) * HEAD_DIM))
    return jnp.array(cols, dtype=jnp.int32)


def _layer(x_prompt, x_sample, state_rwkv, state_shift, cache_win_k, cache_win_v, prm, norm_final_g, *,
           chunk, n_sub, moe_block):
    bp, tp, d = x_prompt.shape
    bs, ts, _ = x_sample.shape
    n_p, n_s = bp * tp, bs * ts
    ts_pad = 8
    tm = n_s
    assert tp % tm == 0 and tm % 8 == 0 and cache_win_k.shape[1] == WINDOW
    xp = x_prompt.reshape(n_p, d)
    xs = x_sample.reshape(n_s, d)

    qp = _q_perm()
    w_in = prm["w_in"]
    w_in = jnp.concatenate([w_in[:, :SHIFT_W], w_in[:, SHIFT_W:SHIFT_W + B_WIDTH][:, qp],
                            w_in[:, SHIFT_W + B_WIDTH:]], axis=1).astype(BF16)
    gn = prm["attn_norm_g"][qp][None]
    w_out = prm["w_out"]
    wa, wb = w_out[:A_WIDTH].astype(BF16), w_out[A_WIDTH:][qp].astype(BF16)
    pad_rows_r = jnp.zeros((LOGIT_ROWS - N_GROUPS - N_EXPERTS, d), F32)
    wr = jnp.concatenate([prm["w_route_group"].T, prm["w_route_expert"].T, pad_rows_r], axis=0).astype(BF16)
    br = jnp.concatenate([prm["b_route_group"], prm["b_route_expert"], pad_rows_r[:, 0]])[:, None]

    zero_blk = jnp.zeros((LORA_W // 2, A_WIDTH), F32)
    wlora = jnp.concatenate([jnp.concatenate([prm["w_decay_up"], zero_blk], axis=1),
                             jnp.concatenate([zero_blk, prm["w_iclr_up"]], axis=1)], axis=0).astype(BF16)
    rp = dict(mu=prm["mu_shift"][None], w0=prm["w_decay0"][None], a0=prm["w_iclr0"][None], wlora=wlora,
              wgate=prm["w_gate_up"].astype(BF16), kk=prm["k_k"][None], ka=prm["k_a"][None],
              rk=prm["r_k"].reshape(1, A_WIDTH), lng=prm["lnx_g"][None], lnb=prm["lnx_b"][None])

    groups = [(0, bp)]
    tiles_per_seq = tp // tm
    nb = tp // WINDOW
    n_qb = _row_tile(nb, 4)
    kv4 = lambda a: a.reshape(a.shape[0], a.shape[1], B_KV_HEADS, HEAD_DIM)
    finals, s_p, sh_p, k_p, v_p = [], [], [], [], []
    for gi, (s0, s1) in enumerate(groups):
        has_sample = gi == len(groups) - 1
        n_seq = s1 - s0
        n_pt = n_seq * tiles_per_seq
        n_pg = n_pt * tm
        seg_sizes = [n_pt] + ([1] if has_sample else [])
        pa, q, k, v = _inproj_call([xp] + ([xs] if has_sample else []), [s0 * tiles_per_seq, 0], seg_sizes,
                                   prm["norm_mix_g"][None], w_in, tm=tm)

        oa_p, s_pg = _rwkv_call(pa, jnp.zeros((n_seq, 1, SHIFT_W), F32),
                                jnp.zeros((n_seq,) + state_rwkv.shape[1:], F32), rp, n_seq=n_seq,
                                n_chunks=tp // chunk, chunk=chunk, n_sub=n_sub, n_bat=1, t_real=chunk)
        ob_p, = _swa_call(prm["attn_sink"], q, k.reshape(-1, WINDOW, B_KV_WIDTH), v.reshape(-1, WINDOW, B_KV_WIDTH),
                          k, v, gn, n_seq=n_seq, n_blk=nb // n_qb, tq=WINDOW, n_bat=1, n_qb=n_qb,
                          first_has_prev=False, out_dtype=BF16,
                          prev_map=lambda b, i: (b * nb + jnp.maximum(i * n_qb - 1, 0), 0, 0))
        oa_list, ob_list = [oa_p], [ob_p]
        if has_sample:
            pad_rows = lambda a: jnp.pad(a[n_pg:].reshape(bs, ts, -1), ((0, 0), (0, ts_pad - ts), (0, 0))).reshape(
                bs * ts_pad, -1)
            oa_s, s_s = _rwkv_call(pad_rows(pa), state_shift[:, None, :], state_rwkv, rp, n_seq=bs,
                                   n_chunks=1, chunk=ts_pad, n_sub=1, n_bat=_row_tile(bs, 16), t_real=ts)
            ob_s, k_s, v_s = _swa_call(prm["attn_sink"], pad_rows(q).astype(F32),
                                       cache_win_k.reshape(bs, WINDOW, B_KV_WIDTH),
                                       cache_win_v.reshape(bs, WINDOW, B_KV_WIDTH), pad_rows(k), pad_rows(v), gn,
                                       n_seq=bs, n_blk=1, tq=ts_pad, n_bat=_row_tile(bs, 16), first_has_prev=True,
                                       prev_map=lambda b, i: (b, 0, 0), out_dtype=F32, cache_rows=ts)
            oa_list.append(oa_s.reshape(bs, ts_pad, A_WIDTH)[:, :ts].reshape(n_s, A_WIDTH))
            ob_list.append(ob_s.reshape(bs, ts_pad, B_WIDTH)[:, :ts].reshape(n_s, B_WIDTH))
            sh_s = pa[n_pg:].reshape(bs, ts, SHIFT_W)[:, -1]

        x1, h2, route = _outproj_call([xp] + ([xs] if has_sample else []), [s0 * tiles_per_seq, 0],
                                               oa_list, ob_list, seg_sizes, wa, wb, prm["norm_ffn_g"][None], wr, br,
                                               tm=tm)
        n = sum(seg_sizes) * tm
        route_t = route.reshape(n // tm, ROUTE_ROWS, tm)
        flat_e = jnp.concatenate([route_t[:, j, :].reshape(n) for j in range(TOP_K_INNER)]).astype(jnp.int32)
        dest, src_tok, visits = _dispatch(flat_e, n, moe_block)
        ybuf = _expert_call(visits, h2[src_tok], prm["w_exp_gate"], prm["w_exp_up"], prm["w_exp_down"], tb=moe_block)
        finals.append((x1, ybuf[dest[:n]], ybuf[dest[n:]], route))

        last_rows = lambda a, m: jnp.stack([a[(b + 1) * tp - m:(b + 1) * tp] for b in range(n_seq)], axis=0)
        s_p.append(s_pg)
        sh_p.append(last_rows(pa, 1)[:, 0])
        k_p.append(last_rows(k, WINDOW))
        v_p.append(last_rows(v, WINDOW))

    gf = norm_final_g[None]
    n_pts = [(s1 - s0) * tiles_per_seq for s0, s1 in groups]
    y_p = _final_call(finals, [0] * len(groups), n_pts, gf, tm=tm)
    y_s = _final_call(finals[-1:], n_pts[-1:], [1], gf, tm=tm)
    cat = lambda parts: jnp.concatenate(parts, axis=0)
    return (y_p.reshape(bp, tp, d), y_s.reshape(bs, ts, d), cat(s_p), cat(sh_p), kv4(cat(k_p)), kv4(cat(v_p)),
            s_s, sh_s, kv4(k_s), kv4(v_s))


def kernel(x_prompt, x_sample, state_rwkv, state_shift, cache_win_k, cache_win_v, norm_mix_g, w_in, mu_shift, w_decay0, w_decay_up, w_iclr0, w_iclr_up, w_gate_up, k_k, k_a, r_k, lnx_g, lnx_b, attn_sink, attn_norm_g, w_out, norm_ffn_g, w_route_group, b_route_group, w_route_expert, b_route_expert, w_exp_gate, w_exp_up, w_exp_down, norm_final_g):
    assert norm_mix_g.shape[0] == 1, "single-layer trunk"
    prm = dict(norm_mix_g=norm_mix_g[0], w_in=w_in[0], mu_shift=mu_shift[0], w_decay0=w_decay0[0],
               w_decay_up=w_decay_up[0], w_iclr0=w_iclr0[0], w_iclr_up=w_iclr_up[0], w_gate_up=w_gate_up[0],
               k_k=k_k[0], k_a=k_a[0], r_k=r_k[0], lnx_g=lnx_g[0], lnx_b=lnx_b[0], attn_sink=attn_sink[0],
               attn_norm_g=attn_norm_g[0], w_out=w_out[0], norm_ffn_g=norm_ffn_g[0],
               w_route_group=w_route_group[0], b_route_group=b_route_group[0],
               w_route_expert=w_route_expert[0], b_route_expert=b_route_expert[0],
               w_exp_gate=w_exp_gate[0], w_exp_up=w_exp_up[0], w_exp_down=w_exp_down[0])
    outs = _layer(x_prompt, x_sample, state_rwkv[0], state_shift[0], cache_win_k[0], cache_win_v[0], prm,
                  norm_final_g, chunk=64, n_sub=8, moe_block=512)
    y_p, y_s, s_p, sh_p, kp, vp, s_s, sh_s, ks, vs = outs
    return (y_p, y_s, s_p[None], sh_p[None], kp[None], vp[None], s_s[None], sh_s[None], ks[None], vs[None])
```

```python
import functools
import math

import jax
import jax.numpy as jnp
from jax import lax
from jax.experimental import pallas as pl
from jax.experimental.pallas import tpu as pltpu

F32 = jnp.float32
BF16 = jnp.bfloat16

HEAD_DIM = 64
A_HEADS = 8
A_WIDTH = A_HEADS * HEAD_DIM
B_HEADS = 8
B_KV_HEADS = 2
B_GROUP = B_HEADS // B_KV_HEADS
B_WIDTH = B_HEADS * HEAD_DIM
B_KV_WIDTH = B_KV_HEADS * HEAD_DIM
DECAY_LORA = 64
ICLR_LORA = 64
GATE_LORA = 128
LORA_W = DECAY_LORA + ICLR_LORA
SHIFT_W = 3 * A_WIDTH + LORA_W + GATE_LORA
IN_W = SHIFT_W + B_WIDTH + 2 * B_KV_WIDTH
WINDOW = 128
N_GROUPS = 4
EXPERTS_PER_GROUP = 4
N_EXPERTS = N_GROUPS * EXPERTS_PER_GROUP
TOP_K_INNER = 2
EXPERT_FF = 512
RMS_EPS = 1e-6
LNX_EPS = 64e-5
DECAY_OFFSET = 0.5

LANE = 128
PAIR = LANE // HEAD_DIM
N_PAIRS = A_HEADS // PAIR
ROUTE_ROWS = 8
LOGIT_ROWS = 32
MASK_NEG = -1e30
VMEM_LIMIT = 48 * 1024 * 1024
SUBLANE = 8

RWKV_CHUNK = 64
RWKV_CHUNKS_PER_STEP = 8
SWA_BLOCKS_PER_STEP = 8
SAMPLE_SEQS_PER_STEP = 16
MOE_BLOCK = 512

LOG2_E = math.log2(math.e)
DECAY_SCALE = math.exp(-DECAY_OFFSET)
HEAD_SHIFT = HEAD_DIM.bit_length() - 1
HI = lax.Precision.HIGHEST


def _row_tile(n, cap):
    t = cap
    while n % t:
        t //= 2
    return t


def _dot(a, b, precision=None):
    return jnp.dot(a, b, preferred_element_type=F32, precision=precision)


def _dot_nt(a, b, precision=None):
    return lax.dot_general(a, b, (((1,), (1,)), ((), ())), preferred_element_type=F32, precision=precision)


def _dot_tn(a, b, precision=None):
    return lax.dot_general(a, b, (((0,), (0,)), ((), ())), preferred_element_type=F32, precision=precision)


def _sigmoid(x):
    return 1.0 / (1.0 + jnp.exp(-x))


def _hi_lo(x, axis):
    hi = x.astype(BF16)
    lo = (x - hi.astype(F32)).astype(BF16)
    return jnp.concatenate([hi, lo], axis=axis)


def _run_segment(i, seg_sizes, run, seg_refs):
    start = 0
    for size, refs in zip(seg_sizes, seg_refs):
        pl.when((i >= start) & (i < start + size))(functools.partial(run, *refs))
        start += size


def _seg_specs(tm, width, seg_sizes, firsts):
    specs, start = [], 0
    for size, first in zip(seg_sizes, firsts):
        specs.append(pl.BlockSpec((tm, width),
                                  lambda i, s=start, f=first, n=size: (f + jnp.clip(i - s, 0, n - 1), 0)))
        start += size
    return specs


def _inproj_body(*refs, seg_sizes):
    n_seg = len(seg_sizes)
    g_ref, w_ref, pa_ref, q_ref, k_ref, v_ref = refs[n_seg:]

    def run(x_ref):
        x = x_ref[...]
        h = x * lax.rsqrt(jnp.mean(x * x, axis=-1, keepdims=True) + RMS_EPS) * g_ref[...]
        p = _dot(h.astype(BF16), w_ref[...])
        pa_ref[...] = p[:, :SHIFT_W]
        q_ref[...] = (p[:, SHIFT_W:SHIFT_W + B_WIDTH] * (HEAD_DIM ** -0.5)).astype(BF16)
        k_ref[...] = p[:, SHIFT_W + B_WIDTH:SHIFT_W + B_WIDTH + B_KV_WIDTH]
        v_ref[...] = p[:, SHIFT_W + B_WIDTH + B_KV_WIDTH:]

    _run_segment(pl.program_id(0), seg_sizes, run, [(r,) for r in refs[:n_seg]])


def _inproj_call(xs_list, firsts, seg_sizes, g, w_bf16, *, tm):
    d = xs_list[0].shape[1]
    n_tiles = sum(seg_sizes)
    n = n_tiles * tm
    row = lambda i: (i, 0)
    fixed = lambda i: (0, 0)
    return pl.pallas_call(
        functools.partial(_inproj_body, seg_sizes=tuple(seg_sizes)),
        grid=(n_tiles,),
        in_specs=_seg_specs(tm, d, seg_sizes, firsts) + [pl.BlockSpec((1, d), fixed), pl.BlockSpec((d, IN_W), fixed)],
        out_specs=[pl.BlockSpec((tm, SHIFT_W), row), pl.BlockSpec((tm, B_WIDTH), row),
                   pl.BlockSpec((tm, B_KV_WIDTH), row), pl.BlockSpec((tm, B_KV_WIDTH), row)],
        out_shape=[jax.ShapeDtypeStruct((n, SHIFT_W), F32), jax.ShapeDtypeStruct((n, B_WIDTH), BF16),
                   jax.ShapeDtypeStruct((n, B_KV_WIDTH), F32), jax.ShapeDtypeStruct((n, B_KV_WIDTH), F32)],
        compiler_params=pltpu.CompilerParams(dimension_semantics=("arbitrary",), vmem_limit_bytes=VMEM_LIMIT),
        name="inproj",
    )(*xs_list, g, w_bf16)


def _rwkv_body(pa_ref, shift0_ref, s0_ref, mu_ref, w0_ref, a0_ref, wlora_ref, wgate_ref, kk_ref, ka_ref,
               rk_ref, lng_ref, lnb_ref, oa_ref, sout_ref, s_scr, prev_scr, *, chunk, n_sub, n_bat, t_real):
    C = chunk
    seq_rows = n_sub * C
    n_seg = n_bat * n_sub
    rows = n_seg * C
    c = pl.program_id(1)

    @pl.when(c == 0)
    def _():
        prev_scr[...] = shift0_ref[...]
        for b in range(n_bat):
            for j in range(N_PAIRS):
                s_scr[b * N_PAIRS + j] = jnp.concatenate([s0_ref[b, PAIR * j + h] for h in range(PAIR)], axis=1)

    pa = pa_ref[...]
    row = lax.broadcasted_iota(jnp.int32, (rows, 1), 0)
    row_in_chunk = row & (C - 1)
    pa_prev = pltpu.roll(pa, 1, axis=0)
    for b in range(n_bat):
        pa_prev = jnp.where(row == b * seq_rows, prev_scr[b], pa_prev)
        last = (b + 1) * seq_rows - C + t_real - 1
        prev_scr[b] = pa[last:last + 1]
    xm = pa + mu_ref[...] * (pa_prev - pa)

    r = xm[:, :A_WIDTH]
    k = xm[:, A_WIDTH:2 * A_WIDTH]
    v = xm[:, 2 * A_WIDTH:3 * A_WIDTH]
    lora_in = xm[:, 3 * A_WIDTH:3 * A_WIDTH + LORA_W]
    gd = xm[:, 3 * A_WIDTH + LORA_W:]

    lane = lax.broadcasted_iota(jnp.int32, (1, LANE), 1)
    lo_half = lane < HEAD_DIM
    z = jnp.where(lo_half, jnp.tanh(lora_in), lora_in)
    lw = _dot(z.astype(BF16), wlora_ref[...])
    dec_pre = w0_ref[...] + lw[:, :A_WIDTH]
    a = _sigmoid(a0_ref[...] + lw[:, A_WIDTH:])
    logdec = (-LOG2_E * DECAY_SCALE) * _sigmoid(dec_pre)
    g = _dot(_sigmoid(gd).astype(BF16), wgate_ref[...])

    ri = lax.broadcasted_iota(jnp.int32, (LANE, LANE), 0)
    ci = lax.broadcasted_iota(jnp.int32, (LANE, LANE), 1)
    same_head = (ri >> HEAD_SHIFT) == (ci >> HEAD_SHIFT)
    seg = jnp.where(same_head, 1.0, 0.0).astype(BF16)
    seg2 = jnp.concatenate([seg, seg], axis=0)

    def headsum(x):
        return jnp.concatenate([_dot(_hi_lo(x[:, j * LANE:(j + 1) * LANE], 1), seg2) for j in range(N_PAIRS)],
                               axis=1)

    kk = k * kk_ref[...]
    kk = kk * lax.rsqrt(jnp.maximum(headsum(kk * kk), 1e-24))
    k = k * (1.0 + (a - 1.0) * ka_ref[...])
    bonus = headsum(r * k * rk_ref[...]) * v

    if t_real < C:
        valid = row_in_chunk < t_real
        logdec = jnp.where(valid, logdec, 0.0)
        kk = jnp.where(valid, kk, 0.0)
        k = jnp.where(valid, k, 0.0)
        v = jnp.where(valid, v, 0.0)

    cum = logdec
    shift = 1
    while shift < C:
        cum = cum + jnp.where(row_in_chunk >= shift, pltpu.roll(cum, shift, axis=0), 0.0)
        shift *= 2
    ends = [cum[(s + 1) * C - 1:(s + 1) * C] for s in range(n_seg)]
    cum_end = jnp.concatenate([jnp.broadcast_to(e, (C, A_WIDTH)) for e in ends], axis=0) if n_seg > 1 else ends[0]
    w_incl = jnp.exp2(cum)
    w_prev = jnp.exp2(cum - logdec)
    w_inv = jnp.exp2(-cum)
    w_end = jnp.exp2(cum_end - cum)
    w_chunk = [jnp.exp2(e) for e in ends]
    kka = kk * a
    terms = dict(A=-kk * w_prev, R=r * w_incl, B=kka * w_inv, K=k * w_inv, V=v, Be=kka * w_end, Ke=k * w_end)

    C2 = PAIR * C
    t_idx = lax.broadcasted_iota(jnp.int32, (C, 1), 0)
    i_idx = lax.broadcasted_iota(jnp.int32, (1, C2), 1) & (C - 1)
    strict = i_idx < t_idx
    incl = i_idx <= t_idx
    ident = jnp.where(i_idx == t_idx, 1.0, 0.0).astype(F32)
    lo_time = lax.broadcasted_iota(jnp.int32, (1, C2), 1) < C
    n_levels = max(1, (C - 1).bit_length())
    bf = lambda x: x.astype(BF16)

    def bd(x, lo_mask=lo_half):
        return bf(jnp.concatenate([jnp.where(lo_mask, x, 0.0), jnp.where(lo_mask, 0.0, x)], axis=0))

    chains = [(s, j) for s in range(n_seg) for j in range(N_PAIRS)]
    tile = lambda name, s, j: terms[name][s * C:(s + 1) * C, j * LANE:(j + 1) * LANE]
    r_sbs = {ch: tile("R", *ch) for ch in chains}
    stk = {ch: {nm: bd(tile(nm, *ch)) for nm in ("A", "B", "K", "V")} for ch in chains}
    m_ab, aak, arb, ark = {}, {}, {}, {}
    for ch in chains:
        t = stk[ch]
        a_sbs, rb = bf(tile("A", *ch)), bf(r_sbs[ch])
        if C2 % LANE == 0:
            m1 = _dot_nt(jnp.concatenate([a_sbs, rb], axis=0), jnp.concatenate([t["B"], t["K"]], axis=0))
            ab, ak, rbm, rk = m1[:C, :C2], m1[:C, C2:], m1[C:, :C2], m1[C:, C2:]
        else:
            ab, ak = _dot_nt(a_sbs, t["B"]), _dot_nt(a_sbs, t["K"])
            rbm, rk = _dot_nt(rb, t["B"]), _dot_nt(rb, t["K"])
        m_ab[ch] = jnp.where(strict, ab, 0.0)
        aak[ch] = jnp.where(strict, ak, 0.0)
        arb[ch] = bf(jnp.where(incl, rbm, 0.0))
        ark[ch] = jnp.where(incl, rk, 0.0)

    nn = dict(m_ab)
    tinv = {ch: ident + m_ab[ch] for ch in chains}
    for lvl in range(1, n_levels):
        if lvl == 1:
            for ch in chains:
                nn[ch] = _dot(bf(nn[ch]), bd(nn[ch], lo_time))
            continue
        for ch in chains:
            both = _dot(bf(jnp.concatenate([nn[ch], tinv[ch]], axis=0)), bd(nn[ch], lo_time))
            nn[ch] = both[:C]
            tinv[ch] = tinv[ch] + both[C:]
    if n_levels > 1:
        for ch in chains:
            tinv[ch] = tinv[ch] + _dot(bf(tinv[ch]), bd(nn[ch], lo_time))
    av = {ch: _dot(bf(jnp.concatenate([aak[ch], ark[ch]], axis=0)), stk[ch]["V"]) for ch in chains}
    x = {ch: _dot(bf(tinv[ch]), jnp.concatenate([stk[ch]["A"], bd(av[ch][:C])], axis=1)) for ch in chains}
    a_eff = {ch: bd(x[ch][:, :LANE]) for ch in chains}
    v_eff = {ch: bd(x[ch][:, LANE:]) for ch in chains}
    zed = {ch: _dot(arb[ch], jnp.concatenate([a_eff[ch], v_eff[ch]], axis=1)) for ch in chains}
    rhat = {ch: bf(r_sbs[ch] + zed[ch][:, :LANE]) for ch in chains}
    y0 = {ch: av[ch][C:] + zed[ch][:, LANE:] for ch in chains}
    p_mat = {ch: bf(jnp.where(same_head, _dot_tn(bf(tile("Be", *ch)), bf(x[ch][:, :LANE])), 0.0))
             for ch in chains}
    q_full = {ch: _dot_tn(bf(jnp.concatenate([x[ch][:, LANE:], tile("V", *ch)], axis=0)),
                          bf(jnp.concatenate([tile("Be", *ch), tile("Ke", *ch)], axis=0))) for ch in chains}

    y_rows = []
    for b in range(n_bat):
        state = [s_scr[b * N_PAIRS + j] for j in range(N_PAIRS)]
        for sub in range(n_sub):
            s = b * n_sub + sub
            ys = []
            for j in range(N_PAIRS):
                ch = (s, j)
                hi = bf(state[j])
                lo = bf(state[j] - hi.astype(F32))
                yy = _dot_nt(rhat[ch], jnp.concatenate([bd(hi), bd(lo)], axis=0))
                ys.append(yy[:, :LANE] + yy[:, LANE:] + y0[ch])
                sp = _dot_nt(jnp.concatenate([hi, lo], axis=0), p_mat[ch])
                q_sbs = jnp.where(lo_half, q_full[ch][:HEAD_DIM], q_full[ch][HEAD_DIM:])
                state[j] = state[j] * w_chunk[s][:, j * LANE:(j + 1) * LANE] + sp[:HEAD_DIM] + sp[HEAD_DIM:] + q_sbs
            y_rows.append(jnp.concatenate(ys, axis=1))
        for j in range(N_PAIRS):
            s_scr[b * N_PAIRS + j] = state[j]

    y = jnp.concatenate(y_rows, axis=0) if n_seg > 1 else y_rows[0]
    inv_n = 1.0 / HEAD_DIM
    mean = headsum(y) * inv_n
    d = y - mean
    var = headsum(d * d) * inv_n
    yn = d * lax.rsqrt(var + LNX_EPS) * lng_ref[...] + lnb_ref[...]
    oa_ref[...] = ((yn + bonus) * g).astype(oa_ref.dtype)

    @pl.when(c == pl.num_programs(1) - 1)
    def _():
        for b in range(n_bat):
            for j in range(N_PAIRS):
                for h in range(PAIR):
                    sout_ref[b, PAIR * j + h] = s_scr[b * N_PAIRS + j][:, h * HEAD_DIM:(h + 1) * HEAD_DIM]


def _rwkv_call(pa2d, shift0, s0_pairs, prm, *, n_seq, n_chunks, chunk, n_sub, n_bat, t_real):
    n_steps = n_chunks // n_sub
    assert n_seq % n_bat == 0 and (n_bat == 1 or n_steps == 1)
    row = lambda b, c: (b * n_steps + c, 0)
    seq3 = lambda b, c: (b, 0, 0)
    fixed = lambda b, c: (0, 0)
    vec = lambda w: pl.BlockSpec((1, w), fixed)
    body = functools.partial(_rwkv_body, chunk=chunk, n_sub=n_sub, n_bat=n_bat, t_real=t_real)
    n_rows = n_seq * n_chunks * chunk
    blk_rows = n_bat * n_sub * chunk
    state_blk = (n_bat, A_HEADS, HEAD_DIM, HEAD_DIM)
    seq4 = lambda b, c: (b, 0, 0, 0)
    return pl.pallas_call(
        body,
        grid=(n_seq // n_bat, n_steps),
        in_specs=[pl.BlockSpec((blk_rows, SHIFT_W), row),
                  pl.BlockSpec((n_bat, 1, SHIFT_W), seq3),
                  pl.BlockSpec(state_blk, seq4),
                  vec(SHIFT_W), vec(A_WIDTH), vec(A_WIDTH),
                  pl.BlockSpec((LORA_W, 2 * A_WIDTH), fixed), pl.BlockSpec((GATE_LORA, A_WIDTH), fixed),
                  vec(A_WIDTH), vec(A_WIDTH), vec(A_WIDTH), vec(A_WIDTH), vec(A_WIDTH)],
        out_specs=[pl.BlockSpec((blk_rows, A_WIDTH), row), pl.BlockSpec(state_blk, seq4)],
        out_shape=[jax.ShapeDtypeStruct((n_rows, A_WIDTH), BF16),
                   jax.ShapeDtypeStruct((n_seq, A_HEADS, HEAD_DIM, HEAD_DIM), F32)],
        scratch_shapes=[pltpu.VMEM((n_bat * N_PAIRS, HEAD_DIM, LANE), F32), pltpu.VMEM((n_bat, 1, SHIFT_W), F32)],
        compiler_params=pltpu.CompilerParams(dimension_semantics=("arbitrary", "arbitrary"),
                                             vmem_limit_bytes=VMEM_LIMIT),
        name="rwkv",
    )(pa2d, shift0, s0_pairs, prm["mu"], prm["w0"], prm["a0"], prm["wlora"], prm["wgate"], prm["kk"], prm["ka"],
      prm["rk"], prm["lng"], prm["lnb"])


def _swa_bias(tq):
    rows = B_GROUP * tq
    grp = jnp.arange(rows, dtype=jnp.int32)[:, None] // tq
    t = jnp.arange(rows, dtype=jnp.int32)[:, None] % tq
    dist_p = t + WINDOW - jnp.arange(WINDOW, dtype=jnp.int32)[None, :]
    dist_c = t - jnp.arange(tq, dtype=jnp.int32)[None, :]

    def bias(dist, kv):
        slope = sum(jnp.where(grp == g, 2.0 ** -(kv * B_GROUP + g + 1), 0.0) for g in range(B_GROUP))
        return jnp.where((dist >= 0) & (dist < WINDOW), -slope * dist.astype(F32), MASK_NEG)

    return (jnp.stack([bias(dist_p, kv) for kv in range(B_KV_HEADS)]),
            jnp.stack([bias(dist_c, kv) for kv in range(B_KV_HEADS)]))


def _swa_body(sink_ref, q_ref, kp_ref, vp_ref, kc_ref, vc_ref, bp_ref, bc_ref, gn_ref, o_ref, *cache_refs, tq, n_bat,
              n_qb, t_real, first_has_prev):
    nblk = pl.program_id(1)
    lane = lax.broadcasted_iota(jnp.int32, (1, LANE), 1)
    kv_masks = [(lane < HEAD_DIM) if kv == 0 else (lane >= HEAD_DIM) for kv in range(B_KV_HEADS)]
    blocks = []
    for s in range(n_bat * n_qb):
        rs = slice(s * tq, (s + 1) * tq)
        kc, vc = kc_ref[rs], vc_ref[rs]
        if n_qb > 1 and s > 0:
            ps = slice((s - 1) * tq, s * tq)
            kp, vp, has_prev = kc_ref[ps], vc_ref[ps], True
        else:
            kp, vp, has_prev = kp_ref[s], vp_ref[s], first_has_prev
        if cache_refs:
            row = lax.broadcasted_iota(jnp.int32, (WINDOW, 1), 0)
            for out_ref, old, new in zip(cache_refs, (kp, vp), (kc, vc)):
                new_tail = jnp.concatenate([pltpu.roll(new, tq - t_real, axis=0)] * (WINDOW // tq), axis=0)
                out_ref[s] = jnp.where(row >= WINDOW - t_real, new_tail, pltpu.roll(old, WINDOW - t_real, axis=0))
        pen = 0.0 if has_prev else jnp.where(nblk > 0, 0.0, MASK_NEG)
        blocks.append(dict(rs=rs, q=q_ref[rs], kp=kp.astype(BF16), kc=kc.astype(BF16), vp=vp, vc=vc, pen=pen))

    keys = [(b, kv) for b in range(len(blocks)) for kv in range(B_KV_HEADS)]
    s_p, s_c, vpm, vcm = {}, {}, {}, {}
    for b, kv in keys:
        blk, mk = blocks[b], kv_masks[kv]
        q_st = jnp.concatenate([jnp.where(mk, blk["q"][:, g * LANE:(g + 1) * LANE], 0.0) for g in range(B_GROUP)],
                               axis=0).astype(BF16)
        s_p[b, kv] = _dot_nt(q_st, blk["kp"])
        s_c[b, kv] = _dot_nt(q_st, blk["kc"])
        vpm[b, kv] = jnp.where(mk, blk["vp"], 0.0).astype(BF16)
        vcm[b, kv] = jnp.where(mk, blk["vc"], 0.0).astype(BF16)

    slabs = [(b, kv, g) for b, kv in keys for g in range(B_GROUP)]
    sink = {(kv, g): sink_ref[kv * B_GROUP + g] for kv in range(B_KV_HEADS) for g in range(B_GROUP)}
    sp, sc, m = {}, {}, {}
    for b, kv, g in slabs:
        rs = slice(g * tq, (g + 1) * tq)
        sp[b, kv, g] = s_p[b, kv][rs] + (bp_ref[kv, rs, :] + blocks[b]["pen"])
        sc[b, kv, g] = s_c[b, kv][rs] + bc_ref[kv, rs, :]
        if tq == WINDOW:
            row_max = jnp.max(jnp.maximum(sp[b, kv, g], sc[b, kv, g]), axis=-1, keepdims=True)
        else:
            row_max = jnp.maximum(jnp.max(sp[b, kv, g], axis=-1, keepdims=True),
                                  jnp.max(sc[b, kv, g], axis=-1, keepdims=True))
        m[b, kv, g] = jnp.maximum(row_max, sink[kv, g])
    e_p, e_c, e_sum = {}, {}, {}
    for key in slabs:
        ep, ec = jnp.exp(sp[key] - m[key]), jnp.exp(sc[key] - m[key])
        if tq == WINDOW:
            e_sum[key] = jnp.sum(ep + ec, axis=-1, keepdims=True)
        else:
            e_sum[key] = jnp.sum(ep, axis=-1, keepdims=True) + jnp.sum(ec, axis=-1, keepdims=True)
        e_p[key], e_c[key] = ep.astype(BF16), ec.astype(BF16)

    tiles = {}
    for b, kv, g in slabs:
        inv = 1.0 / (e_sum[b, kv, g] + jnp.exp(sink[kv, g] - m[b, kv, g]))
        o = (_dot(e_p[b, kv, g], vpm[b, kv]) + _dot(e_c[b, kv, g], vcm[b, kv])) * inv
        tiles[b, g] = o if kv == 0 else tiles[b, g] + o

    for b, blk in enumerate(blocks):
        ssq = sum(jnp.sum(tiles[b, g] * tiles[b, g], axis=-1, keepdims=True) for g in range(B_GROUP))
        inv_rms = lax.rsqrt(ssq * (1.0 / B_WIDTH) + RMS_EPS)
        out = jnp.concatenate([tiles[b, g] for g in range(B_GROUP)], axis=1) * inv_rms * gn_ref[...]
        o_ref[blk["rs"]] = out.astype(o_ref.dtype)


def _swa_call(sink, q2d, kprev, vprev, kcur2d, vcur2d, gn, *, n_seq, n_blk, tq, n_bat, first_has_prev, prev_map,
              out_dtype, n_qb=1, cache_rows=0):
    assert n_seq % n_bat == 0 and (n_bat == 1 or n_blk * n_qb == 1) and (cache_rows == 0 or n_blk * n_qb == 1)
    row = lambda b, n: (b * n_blk + n, 0)
    body = functools.partial(_swa_body, tq=tq, n_bat=n_bat, n_qb=n_qb, t_real=cache_rows,
                             first_has_prev=first_has_prev)
    bias_p, bias_c = _swa_bias(tq)
    whole = lambda b, n: (0, 0, 0)
    blk_rows = n_bat * n_qb * tq
    out_specs = [pl.BlockSpec((blk_rows, B_WIDTH), row)]
    out_shape = [jax.ShapeDtypeStruct((n_seq * n_blk * n_qb * tq, B_WIDTH), out_dtype)]
    if cache_rows:
        out_specs += [pl.BlockSpec((n_bat, WINDOW, B_KV_WIDTH), prev_map)] * 2
        out_shape += [jax.ShapeDtypeStruct((n_seq, WINDOW, B_KV_WIDTH), F32)] * 2
    return pl.pallas_call(
        body,
        grid=(n_seq // n_bat, n_blk),
        in_specs=[pl.BlockSpec(memory_space=pltpu.SMEM),
                  pl.BlockSpec((blk_rows, B_WIDTH), row),
                  pl.BlockSpec((n_bat, WINDOW, B_KV_WIDTH), prev_map),
                  pl.BlockSpec((n_bat, WINDOW, B_KV_WIDTH), prev_map),
                  pl.BlockSpec((blk_rows, B_KV_WIDTH), row),
                  pl.BlockSpec((blk_rows, B_KV_WIDTH), row),
                  pl.BlockSpec(bias_p.shape, whole),
                  pl.BlockSpec(bias_c.shape, whole),
                  pl.BlockSpec((1, B_WIDTH), lambda b, n: (0, 0))],
        out_specs=out_specs,
        out_shape=out_shape,
        compiler_params=pltpu.CompilerParams(dimension_semantics=("arbitrary", "arbitrary"),
                                             vmem_limit_bytes=VMEM_LIMIT),
        name="swa",
    )(sink, q2d, kprev, vprev, kcur2d, vcur2d, bias_p, bias_c, gn)


def _route_cols(lg):
    row = lax.broadcasted_iota(jnp.int32, (LOGIT_ROWS, 1), 0)
    row_f = row.astype(F32)
    no_row = float(LOGIT_ROWS)
    is_group = row < N_GROUPS
    m_g = jnp.max(jnp.where(is_group, lg, MASK_NEG), axis=0, keepdims=True)
    g_idx = jnp.min(jnp.where(is_group & (lg == m_g), row_f, no_row), axis=0, keepdims=True)
    p_group = 1.0 / jnp.sum(jnp.where(is_group, jnp.exp(lg - m_g), 0.0), axis=0, keepdims=True)
    e_row = row - N_GROUPS
    in_group = (e_row >= 0) & (e_row < N_EXPERTS) & ((e_row >> 2).astype(F32) == g_idx)
    m_1 = jnp.max(jnp.where(in_group, lg, MASK_NEG), axis=0, keepdims=True)
    i_1 = jnp.min(jnp.where(in_group & (lg == m_1), row_f, no_row), axis=0, keepdims=True)
    rest = in_group & (row_f != i_1)
    m_2 = jnp.max(jnp.where(rest, lg, MASK_NEG), axis=0, keepdims=True)
    i_2 = jnp.min(jnp.where(rest & (lg == m_2), row_f, no_row), axis=0, keepdims=True)
    ratio = jnp.exp(m_2 - m_1)
    w_1 = p_group / (1.0 + ratio)
    out_row = lax.broadcasted_iota(jnp.int32, (ROUTE_ROWS, 1), 0)
    return jnp.where(out_row == 0, i_1 - N_GROUPS,
                     jnp.where(out_row == 1, i_2 - N_GROUPS,
                               jnp.where(out_row == 2, w_1, jnp.where(out_row == 3, w_1 * ratio, 0.0))))


def _outproj_body(*refs, seg_sizes):
    n_seg = len(seg_sizes)
    x_refs, oa_refs, ob_refs = refs[:n_seg], refs[n_seg:2 * n_seg], refs[2 * n_seg:3 * n_seg]
    wa_ref, wb_ref, g_ref, wr_ref, br_ref, x1_ref, h2_ref, rt_ref = refs[3 * n_seg:]

    def run(x_ref, oa_ref, ob_ref):
        x1 = x_ref[...] + _dot(oa_ref[...].astype(BF16), wa_ref[...]) + _dot(ob_ref[...].astype(BF16), wb_ref[...])
        x1_ref[...] = x1
        h2 = x1 * lax.rsqrt(jnp.mean(x1 * x1, axis=-1, keepdims=True) + RMS_EPS) * g_ref[...]
        h2b = h2.astype(BF16)
        h2_ref[...] = h2b
        rt_ref[...] = _route_cols(_dot_nt(wr_ref[...], h2b) + br_ref[...])

    _run_segment(pl.program_id(0), seg_sizes, run, list(zip(x_refs, oa_refs, ob_refs)))


def _outproj_call(x_list, x_firsts, oa_list, ob_list, seg_sizes, wa, wb, g, wr, br, *, tm):
    d = x_list[0].shape[1]
    n_tiles = sum(seg_sizes)
    n = n_tiles * tm
    row = lambda i: (i, 0)
    fixed = lambda i: (0, 0)
    zeros = [0] * len(seg_sizes)
    return pl.pallas_call(
        functools.partial(_outproj_body, seg_sizes=tuple(seg_sizes)),
        grid=(n_tiles,),
        in_specs=(_seg_specs(tm, d, seg_sizes, x_firsts) + _seg_specs(tm, A_WIDTH, seg_sizes, zeros)
                  + _seg_specs(tm, B_WIDTH, seg_sizes, zeros)
                  + [pl.BlockSpec((A_WIDTH, d), fixed), pl.BlockSpec((B_WIDTH, d), fixed), pl.BlockSpec((1, d), fixed),
                     pl.BlockSpec((LOGIT_ROWS, d), fixed), pl.BlockSpec((LOGIT_ROWS, 1), fixed)]),
        out_specs=[pl.BlockSpec((tm, d), row), pl.BlockSpec((tm, d), row), pl.BlockSpec((ROUTE_ROWS, tm), row)],
        out_shape=[jax.ShapeDtypeStruct((n, d), F32), jax.ShapeDtypeStruct((n, d), BF16),
                   jax.ShapeDtypeStruct((n_tiles * ROUTE_ROWS, tm), F32)],
        compiler_params=pltpu.CompilerParams(dimension_semantics=("arbitrary",), vmem_limit_bytes=VMEM_LIMIT),
        name="outproj",
    )(*x_list, *oa_list, *ob_list, wa, wb, g, wr, br)


def _expert_body(vb_ref, ve_ref, lo_ref, hi_ref, x_ref, wg_ref, wu_ref, wd_ref, y_ref, wg_bf, wu_bf, wd_bf):
    v = pl.program_id(0)
    pv = jnp.maximum(v - 1, 0)
    lo, hi = lo_ref[v], hi_ref[v]
    first_visit = (v == 0) | (vb_ref[v] != vb_ref[pv])

    @pl.when((v == 0) | (ve_ref[v] != ve_ref[pv]))
    def _():
        wg_bf[...] = wg_ref[0].astype(BF16)
        wu_bf[...] = wu_ref[0].astype(BF16)
        wd_bf[...] = wd_ref[0].astype(BF16)

    @pl.when(hi > lo)
    def _():
        x = x_ref[...]
        gate = _dot(x, wg_bf[...])
        up = _dot(x, wu_bf[...])
        mid = (gate * _sigmoid(gate) * up).astype(BF16)
        y = _dot(mid, wd_bf[...])
        row = lax.broadcasted_iota(jnp.int32, (y.shape[0], 1), 0)
        mine = (row >= lo) & (row < hi)

        @pl.when(first_visit)
        def _():
            y_ref[...] = jnp.where(mine, y, 0.0)

        @pl.when(jnp.logical_not(first_visit))
        def _():
            y_ref[...] = jnp.where(mine, y, y_ref[...])


def _expert_call(visits, xs, wg, wu, wd, *, tb):
    n_rows, d = xs.shape
    ff = wg.shape[-1]
    vb, ve, lo, hi = visits
    blk = lambda v, vb, ve, lo, hi: (vb[v], 0)
    wsel = lambda v, vb, ve, lo, hi: (ve[v], 0, 0)
    grid_spec = pltpu.PrefetchScalarGridSpec(
        num_scalar_prefetch=4,
        grid=(vb.shape[0],),
        in_specs=[pl.BlockSpec((tb, d), blk), pl.BlockSpec((1, d, ff), wsel), pl.BlockSpec((1, d, ff), wsel),
                  pl.BlockSpec((1, ff, d), wsel)],
        out_specs=pl.BlockSpec((tb, d), blk),
        scratch_shapes=[pltpu.VMEM((d, ff), BF16), pltpu.VMEM((d, ff), BF16), pltpu.VMEM((ff, d), BF16)],
    )
    return pl.pallas_call(
        _expert_body,
        grid_spec=grid_spec,
        out_shape=jax.ShapeDtypeStruct((n_rows, d), F32),
        compiler_params=pltpu.CompilerParams(dimension_semantics=("arbitrary",), vmem_limit_bytes=VMEM_LIMIT),
        name="experts",
    )(vb, ve, lo, hi, xs, wg, wu, wd)


def _final_body(*refs, seg_sizes):
    n_seg = len(seg_sizes)
    g_ref, o_ref = refs[4 * n_seg:]

    def run(x1_ref, y0_ref, y1_ref, rt_ref):
        tm = rt_ref.shape[1]
        rt = jnp.concatenate([rt_ref[...], jnp.zeros((LANE - ROUTE_ROWS, tm), F32)], axis=0).T
        x = x1_ref[...] + (rt[:, 2:3] * y0_ref[...] + rt[:, 3:4] * y1_ref[...])
        o_ref[...] = x * lax.rsqrt(jnp.mean(x * x, axis=-1, keepdims=True) + RMS_EPS) * g_ref[...]

    _run_segment(pl.program_id(0), seg_sizes, run, [refs[4 * s:4 * s + 4] for s in range(n_seg)])


def _final_call(sources, firsts, seg_sizes, g, *, tm):
    d = sources[0][0].shape[1]
    n_tiles = sum(seg_sizes)
    in_specs, start = [], 0
    for size, first in zip(seg_sizes, firsts):
        src = lambda i, s=start, f=first, n=size: (f + jnp.clip(i - s, 0, n - 1), 0)
        in_specs += [pl.BlockSpec((tm, d), src)] * 3 + [pl.BlockSpec((ROUTE_ROWS, tm), src)]
        start += size
    return pl.pallas_call(
        functools.partial(_final_body, seg_sizes=tuple(seg_sizes)),
        grid=(n_tiles,),
        in_specs=in_specs + [pl.BlockSpec((1, d), lambda i: (0, 0))],
        out_specs=pl.BlockSpec((tm, d), lambda i: (i, 0)),
        out_shape=jax.ShapeDtypeStruct((n_tiles * tm, d), F32),
        compiler_params=pltpu.CompilerParams(dimension_semantics=("arbitrary",), vmem_limit_bytes=VMEM_LIMIT),
        name="final",
    )(*[a for src in sources for a in src], g)


def _dispatch(flat_e, n, tb):
    n_assign = n * TOP_K_INNER
    idx_bits = (n_assign - 1).bit_length()
    assert N_EXPERTS << idx_bits < 2 ** 31 and n_assign % tb == 0
    ids = jnp.arange(n_assign, dtype=jnp.int32)
    sorted_key = jnp.sort((flat_e << idx_bits) | ids)
    sorted_id = sorted_key & ((1 << idx_bits) - 1)
    src_tok = sorted_id - n * (sorted_id // n)

    experts = jnp.arange(N_EXPERTS, dtype=jnp.int32)
    onehot = (flat_e[:, None] == experts[None, :]).astype(jnp.int32)
    csum = jnp.cumsum(onehot, axis=0)
    counts = csum[-1]
    ends = jnp.cumsum(counts)
    starts = ends - counts
    dest = (jnp.sum((csum - 1 + starts[None, :]) * onehot, axis=1)).astype(jnp.int32)

    n_blocks = n_assign // tb
    n_visits = n_blocks + N_EXPERTS - 1
    first_blk = starts // tb
    n_vis_e = jnp.where(counts > 0, (ends + tb - 1) // tb - first_blk, 0)
    v_end = jnp.cumsum(n_vis_e)
    v = jnp.arange(n_visits, dtype=jnp.int32)
    valid = v < v_end[-1]
    last_e = jnp.max(jnp.where(counts > 0, experts, 0))
    ve = jnp.where(valid, jnp.sum((v_end[None, :] <= v[:, None]).astype(jnp.int32), axis=1), last_e)
    pick = lambda a: jnp.sum(jnp.where(ve[:, None] == experts[None, :], a[None, :], 0), axis=1)
    vb = pick(first_blk) + v - pick(v_end - n_vis_e)
    lo = jnp.maximum(pick(starts), vb * tb) - vb * tb
    hi = jnp.minimum(pick(ends), (vb + 1) * tb) - vb * tb
    vb = jnp.where(valid, vb, n_blocks - 1)
    lo = jnp.where(valid, lo, 0)
    hi = jnp.where(valid, hi, 0)
    return dest, src_tok, (vb.astype(jnp.int32), ve.astype(jnp.int32), lo.astype(jnp.int32), hi.astype(jnp.int32))


def _q_perm():
    cols = []
    for g in range(B_GROUP):
        for kv in range(B_KV_HEADS):
            h = kv * B_GROUP + g
            cols.extend(range(h * HEAD_DIM, (h + 1) * HEAD_DIM))
    return jnp.array(cols, dtype=jnp.int32)


def _layer(x_prompt, x_sample, state_rwkv, state_shift, cache_win_k, cache_win_v, prm, norm_final_g, *,
           chunk, n_sub, moe_block):
    bp, tp, d = x_prompt.shape
    bs, ts, _ = x_sample.shape
    n_p, n_s = bp * tp, bs * ts
    ts_pad = SUBLANE
    tm = n_s
    assert tp % tm == 0 and tm % 8 == 0 and cache_win_k.shape[1] == WINDOW
    xp = x_prompt.reshape(n_p, d)
    xs = x_sample.reshape(n_s, d)

    qp = _q_perm()
    w_in = prm["w_in"]
    w_in = jnp.concatenate([w_in[:, :SHIFT_W], w_in[:, SHIFT_W:SHIFT_W + B_WIDTH][:, qp],
                            w_in[:, SHIFT_W + B_WIDTH:]], axis=1).astype(BF16)
    gn = prm["attn_norm_g"][qp][None]
    w_out = prm["w_out"]
    wa, wb = w_out[:A_WIDTH].astype(BF16), w_out[A_WIDTH:][qp].astype(BF16)
    pad_rows_r = jnp.zeros((LOGIT_ROWS - N_GROUPS - N_EXPERTS, d), F32)
    wr = jnp.concatenate([prm["w_route_group"].T, prm["w_route_expert"].T, pad_rows_r], axis=0).astype(BF16)
    br = jnp.concatenate([prm["b_route_group"], prm["b_route_expert"], pad_rows_r[:, 0]])[:, None]

    zero_blk = jnp.zeros((LORA_W // 2, A_WIDTH), F32)
    wlora = jnp.concatenate([jnp.concatenate([prm["w_decay_up"], zero_blk], axis=1),
                             jnp.concatenate([zero_blk, prm["w_iclr_up"]], axis=1)], axis=0).astype(BF16)
    rp = dict(mu=prm["mu_shift"][None], w0=prm["w_decay0"][None], a0=prm["w_iclr0"][None], wlora=wlora,
              wgate=prm["w_gate_up"].astype(BF16), kk=prm["k_k"][None], ka=prm["k_a"][None],
              rk=prm["r_k"].reshape(1, A_WIDTH), lng=prm["lnx_g"][None], lnb=prm["lnx_b"][None])

    groups = [(0, bp)]
    tiles_per_seq = tp // tm
    nb = tp // WINDOW
    n_qb = _row_tile(nb, SWA_BLOCKS_PER_STEP)
    kv4 = lambda a: a.reshape(a.shape[0], a.shape[1], B_KV_HEADS, HEAD_DIM)
    finals, s_p, sh_p, k_p, v_p = [], [], [], [], []
    for gi, (s0, s1) in enumerate(groups):
        has_sample = gi == len(groups) - 1
        n_seq = s1 - s0
        n_pt = n_seq * tiles_per_seq
        n_pg = n_pt * tm
        seg_sizes = [n_pt] + ([1] if has_sample else [])
        pa, q, k, v = _inproj_call([xp] + ([xs] if has_sample else []), [s0 * tiles_per_seq, 0], seg_sizes,
                                   prm["norm_mix_g"][None], w_in, tm=tm)

        oa_p, s_pg = _rwkv_call(pa, jnp.zeros((n_seq, 1, SHIFT_W), F32),
                                jnp.zeros((n_seq,) + state_rwkv.shape[1:], F32), rp, n_seq=n_seq,
                                n_chunks=tp // chunk, chunk=chunk, n_sub=n_sub, n_bat=1, t_real=chunk)
        ob_p, = _swa_call(prm["attn_sink"], q, k.reshape(-1, WINDOW, B_KV_WIDTH), v.reshape(-1, WINDOW, B_KV_WIDTH),
                          k, v, gn, n_seq=n_seq, n_blk=nb // n_qb, tq=WINDOW, n_bat=1, n_qb=n_qb,
                          first_has_prev=False, out_dtype=BF16,
                          prev_map=lambda b, i: (b * nb + jnp.maximum(i * n_qb - 1, 0), 0, 0))
        oa_list, ob_list = [oa_p], [ob_p]
        if has_sample:
            pad_rows = lambda a: jnp.pad(a[n_pg:].reshape(bs, ts, -1), ((0, 0), (0, ts_pad - ts), (0, 0))).reshape(
                bs * ts_pad, -1)
            oa_s, s_s = _rwkv_call(pad_rows(pa), state_shift[:, None, :], state_rwkv, rp, n_seq=bs,
                                   n_chunks=1, chunk=ts_pad, n_sub=1, n_bat=_row_tile(bs, SAMPLE_SEQS_PER_STEP), t_real=ts)
            ob_s, k_s, v_s = _swa_call(prm["attn_sink"], pad_rows(q).astype(F32),
                                       cache_win_k.reshape(bs, WINDOW, B_KV_WIDTH),
                                       cache_win_v.reshape(bs, WINDOW, B_KV_WIDTH), pad_rows(k), pad_rows(v), gn,
                                       n_seq=bs, n_blk=1, tq=ts_pad, n_bat=_row_tile(bs, SAMPLE_SEQS_PER_STEP), first_has_prev=True,
                                       prev_map=lambda b, i: (b, 0, 0), out_dtype=F32, cache_rows=ts)
            oa_list.append(oa_s.reshape(bs, ts_pad, A_WIDTH)[:, :ts].reshape(n_s, A_WIDTH))
            ob_list.append(ob_s.reshape(bs, ts_pad, B_WIDTH)[:, :ts].reshape(n_s, B_WIDTH))
            sh_s = pa[n_pg:].reshape(bs, ts, SHIFT_W)[:, -1]

        x1, h2, route = _outproj_call([xp] + ([xs] if has_sample else []), [s0 * tiles_per_seq, 0],
                                               oa_list, ob_list, seg_sizes, wa, wb, prm["norm_ffn_g"][None], wr, br,
                                               tm=tm)
        n = sum(seg_sizes) * tm
        route_t = route.reshape(n // tm, ROUTE_ROWS, tm)
        flat_e = jnp.concatenate([route_t[:, j, :].reshape(n) for j in range(TOP_K_INNER)]).astype(jnp.int32)
        dest, src_tok, visits = _dispatch(flat_e, n, moe_block)
        ybuf = _expert_call(visits, h2[src_tok], prm["w_exp_gate"], prm["w_exp_up"], prm["w_exp_down"], tb=moe_block)
        finals.append((x1, ybuf[dest[:n]], ybuf[dest[n:]], route))

        last_rows = lambda a, m: jnp.stack([a[(b + 1) * tp - m:(b + 1) * tp] for b in range(n_seq)], axis=0)
        s_p.append(s_pg)
        sh_p.append(last_rows(pa, 1)[:, 0])
        k_p.append(last_rows(k, WINDOW))
        v_p.append(last_rows(v, WINDOW))

    gf = norm_final_g[None]
    n_pts = [(s1 - s0) * tiles_per_seq for s0, s1 in groups]
    y_p = _final_call(finals, [0] * len(groups), n_pts, gf, tm=tm)
    y_s = _final_call(finals[-1:], n_pts[-1:], [1], gf, tm=tm)
    cat = lambda parts: jnp.concatenate(parts, axis=0)
    return (y_p.reshape(bp, tp, d), y_s.reshape(bs, ts, d), cat(s_p), cat(sh_p), kv4(cat(k_p)), kv4(cat(v_p)),
            s_s, sh_s, kv4(k_s), kv4(v_s))


def kernel(x_prompt, x_sample, state_rwkv, state_shift, cache_win_k, cache_win_v, norm_mix_g, w_in, mu_shift, w_decay0, w_decay_up, w_iclr0, w_iclr_up, w_gate_up, k_k, k_a, r_k, lnx_g, lnx_b, attn_sink, attn_norm_g, w_out, norm_ffn_g, w_route_group, b_route_group, w_route_expert, b_route_expert, w_exp_gate, w_exp_up, w_exp_down, norm_final_g):
    assert norm_mix_g.shape[0] == 1, "single-layer trunk"
    prm = dict(norm_mix_g=norm_mix_g[0], w_in=w_in[0], mu_shift=mu_shift[0], w_decay0=w_decay0[0],
               w_decay_up=w_decay_up[0], w_iclr0=w_iclr0[0], w_iclr_up=w_iclr_up[0], w_gate_up=w_gate_up[0],
               k_k=k_k[0], k_a=k_a[0], r_k=r_k[0], lnx_g=lnx_g[0], lnx_b=lnx_b[0], attn_sink=attn_sink[0],
               attn_norm_g=attn_norm_g[0], w_out=w_out[0], norm_ffn_g=norm_ffn_g[0],
               w_route_group=w_route_group[0], b_route_group=b_route_group[0],
               w_route_expert=w_route_expert[0], b_route_expert=b_route_expert[0],
               w_exp_gate=w_exp_gate[0], w_exp_up=w_exp_up[0], w_exp_down=w_exp_down[0])
    outs = _layer(x_prompt, x_sample, state_rwkv[0], state_shift[0], cache_win_k[0], cache_win_v[0], prm,
                  norm_final_g, chunk=RWKV_CHUNK, n_sub=RWKV_CHUNKS_PER_STEP, moe_block=MOE_BLOCK)
    y_p, y_s, s_p, sh_p, kp, vp, s_s, sh_s, ks, vs = outs
    return (y_p, y_s, s_p[None], sh_p[None], kp[None], vp[None], s_s[None], sh_s[None], ks[None], vs[None])
```

```python
import functools
import math

import jax
import jax.numpy as jnp
from jax import lax
from jax.experimental import pallas as pl
from jax.experimental.pallas import tpu as pltpu

F32 = jnp.float32
BF16 = jnp.bfloat16

HEAD_DIM = 64
A_HEADS = 8
A_WIDTH = A_HEADS * HEAD_DIM
B_HEADS = 8
B_KV_HEADS = 2
B_GROUP = B_HEADS // B_KV_HEADS
B_WIDTH = B_HEADS * HEAD_DIM
B_KV_WIDTH = B_KV_HEADS * HEAD_DIM
DECAY_LORA = 64
ICLR_LORA = 64
GATE_LORA = 128
LORA_W = DECAY_LORA + ICLR_LORA
SHIFT_W = 3 * A_WIDTH + LORA_W + GATE_LORA
IN_W = SHIFT_W + B_WIDTH + 2 * B_KV_WIDTH
WINDOW = 128
N_GROUPS = 4
EXPERTS_PER_GROUP = 4
N_EXPERTS = N_GROUPS * EXPERTS_PER_GROUP
TOP_K_INNER = 2
EXPERT_FF = 512
RMS_EPS = 1e-6
LNX_EPS = 64e-5
DECAY_OFFSET = 0.5

LANE = 128
PAIR = LANE // HEAD_DIM
N_PAIRS = A_HEADS // PAIR
ROUTE_ROWS = 8
LOGIT_ROWS = 32
MASK_NEG = -1e30
VMEM_LIMIT = 48 * 1024 * 1024
SUBLANE = 8

RWKV_CHUNK = 64
RWKV_CHUNKS_PER_STEP = 8
SWA_BLOCKS_PER_STEP = 16
SAMPLE_SEQS_PER_STEP = 16
MOE_BLOCK = 512

LOG2_E = math.log2(math.e)
DECAY_SCALE = math.exp(-DECAY_OFFSET)
HEAD_SHIFT = HEAD_DIM.bit_length() - 1
HI = lax.Precision.HIGHEST


def _row_tile(n, cap):
    t = cap
    while n % t:
        t //= 2
    return t


def _dot(a, b, precision=None):
    return jnp.dot(a, b, preferred_element_type=F32, precision=precision)


def _dot_nt(a, b, precision=None):
    return lax.dot_general(a, b, (((1,), (1,)), ((), ())), preferred_element_type=F32, precision=precision)


def _dot_tn(a, b, precision=None):
    return lax.dot_general(a, b, (((0,), (0,)), ((), ())), preferred_element_type=F32, precision=precision)


def _sigmoid(x):
    return 1.0 / (1.0 + jnp.exp(-x))


def _hi_lo(x, axis):
    hi = x.astype(BF16)
    lo = (x - hi.astype(F32)).astype(BF16)
    return jnp.concatenate([hi, lo], axis=axis)


def _run_segment(i, seg_sizes, run, seg_refs):
    start = 0
    for size, refs in zip(seg_sizes, seg_refs):
        pl.when((i >= start) & (i < start + size))(functools.partial(run, *refs))
        start += size


def _seg_specs(tm, width, seg_sizes, firsts):
    specs, start = [], 0
    for size, first in zip(seg_sizes, firsts):
        specs.append(pl.BlockSpec((tm, width),
                                  lambda i, s=start, f=first, n=size: (f + jnp.clip(i - s, 0, n - 1), 0)))
        start += size
    return specs


def _inproj_body(*refs, seg_sizes):
    n_seg = len(seg_sizes)
    g_ref, w_ref, pa_ref, q_ref, k_ref, v_ref = refs[n_seg:]

    def run(x_ref):
        x = x_ref[...]
        h = x * lax.rsqrt(jnp.mean(x * x, axis=-1, keepdims=True) + RMS_EPS) * g_ref[...]
        p = _dot(h.astype(BF16), w_ref[...])
        pa_ref[...] = p[:, :SHIFT_W]
        q_ref[...] = (p[:, SHIFT_W:SHIFT_W + B_WIDTH] * (HEAD_DIM ** -0.5)).astype(BF16)
        k_ref[...] = p[:, SHIFT_W + B_WIDTH:SHIFT_W + B_WIDTH + B_KV_WIDTH]
        v_ref[...] = p[:, SHIFT_W + B_WIDTH + B_KV_WIDTH:]

    _run_segment(pl.program_id(0), seg_sizes, run, [(r,) for r in refs[:n_seg]])


def _inproj_call(xs_list, firsts, seg_sizes, g, w_bf16, *, tm):
    d = xs_list[0].shape[1]
    n_tiles = sum(seg_sizes)
    n = n_tiles * tm
    row = lambda i: (i, 0)
    fixed = lambda i: (0, 0)
    return pl.pallas_call(
        functools.partial(_inproj_body, seg_sizes=tuple(seg_sizes)),
        grid=(n_tiles,),
        in_specs=_seg_specs(tm, d, seg_sizes, firsts) + [pl.BlockSpec((1, d), fixed), pl.BlockSpec((d, IN_W), fixed)],
        out_specs=[pl.BlockSpec((tm, SHIFT_W), row), pl.BlockSpec((tm, B_WIDTH), row),
                   pl.BlockSpec((tm, B_KV_WIDTH), row), pl.BlockSpec((tm, B_KV_WIDTH), row)],
        out_shape=[jax.ShapeDtypeStruct((n, SHIFT_W), F32), jax.ShapeDtypeStruct((n, B_WIDTH), BF16),
                   jax.ShapeDtypeStruct((n, B_KV_WIDTH), F32), jax.ShapeDtypeStruct((n, B_KV_WIDTH), F32)],
        compiler_params=pltpu.CompilerParams(dimension_semantics=("arbitrary",), vmem_limit_bytes=VMEM_LIMIT),
        name="inproj",
    )(*xs_list, g, w_bf16)


def _rwkv_body(pa_ref, shift0_ref, s0_ref, mu_ref, w0_ref, a0_ref, wlora_ref, wgate_ref, kk_ref, ka_ref,
               rk_ref, lng_ref, lnb_ref, oa_ref, sout_ref, s_scr, prev_scr, *, chunk, n_sub, n_bat, t_real):
    C = chunk
    seq_rows = n_sub * C
    n_seg = n_bat * n_sub
    rows = n_seg * C
    c = pl.program_id(1)

    @pl.when(c == 0)
    def _():
        prev_scr[...] = shift0_ref[...]
        for b in range(n_bat):
            for j in range(N_PAIRS):
                s_scr[b * N_PAIRS + j] = jnp.concatenate([s0_ref[b, PAIR * j + h] for h in range(PAIR)], axis=1)

    pa = pa_ref[...]
    row = lax.broadcasted_iota(jnp.int32, (rows, 1), 0)
    row_in_chunk = row & (C - 1)
    pa_prev = pltpu.roll(pa, 1, axis=0)
    for b in range(n_bat):
        pa_prev = jnp.where(row == b * seq_rows, prev_scr[b], pa_prev)
        last = (b + 1) * seq_rows - C + t_real - 1
        prev_scr[b] = pa[last:last + 1]
    xm = pa + mu_ref[...] * (pa_prev - pa)

    r = xm[:, :A_WIDTH]
    k = xm[:, A_WIDTH:2 * A_WIDTH]
    v = xm[:, 2 * A_WIDTH:3 * A_WIDTH]
    lora_in = xm[:, 3 * A_WIDTH:3 * A_WIDTH + LORA_W]
    gd = xm[:, 3 * A_WIDTH + LORA_W:]

    lane = lax.broadcasted_iota(jnp.int32, (1, LANE), 1)
    lo_half = lane < HEAD_DIM
    z = jnp.where(lo_half, jnp.tanh(lora_in), lora_in)
    lw = _dot(z.astype(BF16), wlora_ref[...])
    dec_pre = w0_ref[...] + lw[:, :A_WIDTH]
    a = _sigmoid(a0_ref[...] + lw[:, A_WIDTH:])
    logdec = (-LOG2_E * DECAY_SCALE) * _sigmoid(dec_pre)
    g = _dot(_sigmoid(gd).astype(BF16), wgate_ref[...])

    ri = lax.broadcasted_iota(jnp.int32, (LANE, LANE), 0)
    ci = lax.broadcasted_iota(jnp.int32, (LANE, LANE), 1)
    same_head = (ri >> HEAD_SHIFT) == (ci >> HEAD_SHIFT)
    seg = jnp.where(same_head, 1.0, 0.0).astype(BF16)
    seg2 = jnp.concatenate([seg, seg], axis=0)

    def headsum(x):
        return jnp.concatenate([_dot(_hi_lo(x[:, j * LANE:(j + 1) * LANE], 1), seg2) for j in range(N_PAIRS)],
                               axis=1)

    kk = k * kk_ref[...]
    kk = kk * lax.rsqrt(jnp.maximum(headsum(kk * kk), 1e-24))
    k = k * (1.0 + (a - 1.0) * ka_ref[...])
    bonus = headsum(r * k * rk_ref[...]) * v

    if t_real < C:
        valid = row_in_chunk < t_real
        logdec = jnp.where(valid, logdec, 0.0)
        kk = jnp.where(valid, kk, 0.0)
        k = jnp.where(valid, k, 0.0)
        v = jnp.where(valid, v, 0.0)

    cum = logdec
    shift = 1
    while shift < C:
        cum = cum + jnp.where(row_in_chunk >= shift, pltpu.roll(cum, shift, axis=0), 0.0)
        shift *= 2
    ends = [cum[(s + 1) * C - 1:(s + 1) * C] for s in range(n_seg)]
    cum_end = jnp.concatenate([jnp.broadcast_to(e, (C, A_WIDTH)) for e in ends], axis=0) if n_seg > 1 else ends[0]
    w_incl = jnp.exp2(cum)
    w_prev = jnp.exp2(cum - logdec)
    w_inv = jnp.exp2(-cum)
    w_end = jnp.exp2(cum_end - cum)
    w_chunk = [jnp.exp2(e) for e in ends]
    kka = kk * a
    terms = dict(A=-kk * w_prev, R=r * w_incl, B=kka * w_inv, K=k * w_inv, V=v, Be=kka * w_end, Ke=k * w_end)

    C2 = PAIR * C
    t_idx = lax.broadcasted_iota(jnp.int32, (C, 1), 0)
    i_idx = lax.broadcasted_iota(jnp.int32, (1, C2), 1) & (C - 1)
    strict = i_idx < t_idx
    incl = i_idx <= t_idx
    ident = jnp.where(i_idx == t_idx, 1.0, 0.0).astype(F32)
    lo_time = lax.broadcasted_iota(jnp.int32, (1, C2), 1) < C
    n_levels = max(1, (C - 1).bit_length())
    bf = lambda x: x.astype(BF16)

    def bd(x, lo_mask=lo_half):
        return bf(jnp.concatenate([jnp.where(lo_mask, x, 0.0), jnp.where(lo_mask, 0.0, x)], axis=0))

    chains = [(s, j) for s in range(n_seg) for j in range(N_PAIRS)]
    tile = lambda name, s, j: terms[name][s * C:(s + 1) * C, j * LANE:(j + 1) * LANE]
    r_sbs = {ch: tile("R", *ch) for ch in chains}
    stk = {ch: {nm: bd(tile(nm, *ch)) for nm in ("A", "B", "K", "V")} for ch in chains}
    m_ab, aak, arb, ark = {}, {}, {}, {}
    for ch in chains:
        t = stk[ch]
        a_sbs, rb = bf(tile("A", *ch)), bf(r_sbs[ch])
        if C2 % LANE == 0:
            m1 = _dot_nt(jnp.concatenate([a_sbs, rb], axis=0), jnp.concatenate([t["B"], t["K"]], axis=0))
            ab, ak, rbm, rk = m1[:C, :C2], m1[:C, C2:], m1[C:, :C2], m1[C:, C2:]
        else:
            ab, ak = _dot_nt(a_sbs, t["B"]), _dot_nt(a_sbs, t["K"])
            rbm, rk = _dot_nt(rb, t["B"]), _dot_nt(rb, t["K"])
        m_ab[ch] = jnp.where(strict, ab, 0.0)
        aak[ch] = jnp.where(strict, ak, 0.0)
        arb[ch] = bf(jnp.where(incl, rbm, 0.0))
        ark[ch] = jnp.where(incl, rk, 0.0)

    nn = dict(m_ab)
    tinv = {ch: ident + m_ab[ch] for ch in chains}
    for lvl in range(1, n_levels):
        if lvl == 1:
            for ch in chains:
                nn[ch] = _dot(bf(nn[ch]), bd(nn[ch], lo_time))
            continue
        for ch in chains:
            both = _dot(bf(jnp.concatenate([nn[ch], tinv[ch]], axis=0)), bd(nn[ch], lo_time))
            nn[ch] = both[:C]
            tinv[ch] = tinv[ch] + both[C:]
    if n_levels > 1:
        for ch in chains:
            tinv[ch] = tinv[ch] + _dot(bf(tinv[ch]), bd(nn[ch], lo_time))
    av = {ch: _dot(bf(jnp.concatenate([aak[ch], ark[ch]], axis=0)), stk[ch]["V"]) for ch in chains}
    x = {ch: _dot(bf(tinv[ch]), jnp.concatenate([stk[ch]["A"], bd(av[ch][:C])], axis=1)) for ch in chains}
    a_eff = {ch: bd(x[ch][:, :LANE]) for ch in chains}
    v_eff = {ch: bd(x[ch][:, LANE:]) for ch in chains}
    zed = {ch: _dot(arb[ch], jnp.concatenate([a_eff[ch], v_eff[ch]], axis=1)) for ch in chains}
    rhat = {ch: bf(r_sbs[ch] + zed[ch][:, :LANE]) for ch in chains}
    y0 = {ch: av[ch][C:] + zed[ch][:, LANE:] for ch in chains}
    p_mat = {ch: bf(jnp.where(same_head, _dot_tn(bf(tile("Be", *ch)), bf(x[ch][:, :LANE])), 0.0))
             for ch in chains}
    q_full = {ch: _dot_tn(bf(jnp.concatenate([x[ch][:, LANE:], tile("V", *ch)], axis=0)),
                          bf(jnp.concatenate([tile("Be", *ch), tile("Ke", *ch)], axis=0))) for ch in chains}

    y_rows = []
    for b in range(n_bat):
        state = [s_scr[b * N_PAIRS + j] for j in range(N_PAIRS)]
        for sub in range(n_sub):
            s = b * n_sub + sub
            ys = []
            for j in range(N_PAIRS):
                ch = (s, j)
                hi = bf(state[j])
                lo = bf(state[j] - hi.astype(F32))
                yy = _dot_nt(rhat[ch], jnp.concatenate([bd(hi), bd(lo)], axis=0))
                ys.append(yy[:, :LANE] + yy[:, LANE:] + y0[ch])
                sp = _dot_nt(jnp.concatenate([hi, lo], axis=0), p_mat[ch])
                q_sbs = jnp.where(lo_half, q_full[ch][:HEAD_DIM], q_full[ch][HEAD_DIM:])
                state[j] = state[j] * w_chunk[s][:, j * LANE:(j + 1) * LANE] + sp[:HEAD_DIM] + sp[HEAD_DIM:] + q_sbs
            y_rows.append(jnp.concatenate(ys, axis=1))
        for j in range(N_PAIRS):
            s_scr[b * N_PAIRS + j] = state[j]

    y = jnp.concatenate(y_rows, axis=0) if n_seg > 1 else y_rows[0]
    inv_n = 1.0 / HEAD_DIM
    mean = headsum(y) * inv_n
    d = y - mean
    var = headsum(d * d) * inv_n
    yn = d * lax.rsqrt(var + LNX_EPS) * lng_ref[...] + lnb_ref[...]
    oa_ref[...] = ((yn + bonus) * g).astype(oa_ref.dtype)

    @pl.when(c == pl.num_programs(1) - 1)
    def _():
        for b in range(n_bat):
            for j in range(N_PAIRS):
                for h in range(PAIR):
                    sout_ref[b, PAIR * j + h] = s_scr[b * N_PAIRS + j][:, h * HEAD_DIM:(h + 1) * HEAD_DIM]


def _rwkv_call(pa2d, shift0, s0_pairs, prm, *, n_seq, n_chunks, chunk, n_sub, n_bat, t_real):
    n_steps = n_chunks // n_sub
    assert n_seq % n_bat == 0 and (n_bat == 1 or n_steps == 1)
    row = lambda b, c: (b * n_steps + c, 0)
    seq3 = lambda b, c: (b, 0, 0)
    fixed = lambda b, c: (0, 0)
    vec = lambda w: pl.BlockSpec((1, w), fixed)
    body = functools.partial(_rwkv_body, chunk=chunk, n_sub=n_sub, n_bat=n_bat, t_real=t_real)
    n_rows = n_seq * n_chunks * chunk
    blk_rows = n_bat * n_sub * chunk
    state_blk = (n_bat, A_HEADS, HEAD_DIM, HEAD_DIM)
    seq4 = lambda b, c: (b, 0, 0, 0)
    return pl.pallas_call(
        body,
        grid=(n_seq // n_bat, n_steps),
        in_specs=[pl.BlockSpec((blk_rows, SHIFT_W), row),
                  pl.BlockSpec((n_bat, 1, SHIFT_W), seq3),
                  pl.BlockSpec(state_blk, seq4),
                  vec(SHIFT_W), vec(A_WIDTH), vec(A_WIDTH),
                  pl.BlockSpec((LORA_W, 2 * A_WIDTH), fixed), pl.BlockSpec((GATE_LORA, A_WIDTH), fixed),
                  vec(A_WIDTH), vec(A_WIDTH), vec(A_WIDTH), vec(A_WIDTH), vec(A_WIDTH)],
        out_specs=[pl.BlockSpec((blk_rows, A_WIDTH), row), pl.BlockSpec(state_blk, seq4)],
        out_shape=[jax.ShapeDtypeStruct((n_rows, A_WIDTH), BF16),
                   jax.ShapeDtypeStruct((n_seq, A_HEADS, HEAD_DIM, HEAD_DIM), F32)],
        scratch_shapes=[pltpu.VMEM((n_bat * N_PAIRS, HEAD_DIM, LANE), F32), pltpu.VMEM((n_bat, 1, SHIFT_W), F32)],
        compiler_params=pltpu.CompilerParams(dimension_semantics=("arbitrary", "arbitrary"),
                                             vmem_limit_bytes=VMEM_LIMIT),
        name="rwkv",
    )(pa2d, shift0, s0_pairs, prm["mu"], prm["w0"], prm["a0"], prm["wlora"], prm["wgate"], prm["kk"], prm["ka"],
      prm["rk"], prm["lng"], prm["lnb"])


def _swa_bias(tq):
    rows = B_GROUP * tq
    grp = jnp.arange(rows, dtype=jnp.int32)[:, None] // tq
    t = jnp.arange(rows, dtype=jnp.int32)[:, None] % tq
    dist_p = t + WINDOW - jnp.arange(WINDOW, dtype=jnp.int32)[None, :]
    dist_c = t - jnp.arange(tq, dtype=jnp.int32)[None, :]

    def bias(dist, kv):
        slope = sum(jnp.where(grp == g, 2.0 ** -(kv * B_GROUP + g + 1), 0.0) for g in range(B_GROUP))
        return jnp.where((dist >= 0) & (dist < WINDOW), -slope * dist.astype(F32), MASK_NEG)

    return (jnp.stack([bias(dist_p, kv) for kv in range(B_KV_HEADS)]),
            jnp.stack([bias(dist_c, kv) for kv in range(B_KV_HEADS)]))


def _swa_body(sink_ref, q_ref, kp_ref, vp_ref, kc_ref, vc_ref, bp_ref, bc_ref, gn_ref, o_ref, *cache_refs, tq, n_bat,
              n_qb, t_real, first_has_prev):
    nblk = pl.program_id(1)
    lane = lax.broadcasted_iota(jnp.int32, (1, LANE), 1)
    kv_masks = [(lane < HEAD_DIM) if kv == 0 else (lane >= HEAD_DIM) for kv in range(B_KV_HEADS)]
    blocks = []
    for s in range(n_bat * n_qb):
        rs = slice(s * tq, (s + 1) * tq)
        kc, vc = kc_ref[rs], vc_ref[rs]
        if n_qb > 1 and s > 0:
            ps = slice((s - 1) * tq, s * tq)
            kp, vp, has_prev = kc_ref[ps], vc_ref[ps], True
        else:
            kp, vp, has_prev = kp_ref[s], vp_ref[s], first_has_prev
        if cache_refs:
            row = lax.broadcasted_iota(jnp.int32, (WINDOW, 1), 0)
            for out_ref, old, new in zip(cache_refs, (kp, vp), (kc, vc)):
                new_tail = jnp.concatenate([pltpu.roll(new, tq - t_real, axis=0)] * (WINDOW // tq), axis=0)
                out_ref[s] = jnp.where(row >= WINDOW - t_real, new_tail, pltpu.roll(old, WINDOW - t_real, axis=0))
        pen = 0.0 if has_prev else jnp.where(nblk > 0, 0.0, MASK_NEG)
        blocks.append(dict(rs=rs, q=q_ref[rs], kp=kp.astype(BF16), kc=kc.astype(BF16), vp=vp, vc=vc, pen=pen))

    keys = [(b, kv) for b in range(len(blocks)) for kv in range(B_KV_HEADS)]
    s_p, s_c, vpm, vcm = {}, {}, {}, {}
    for b, kv in keys:
        blk, mk = blocks[b], kv_masks[kv]
        q_st = jnp.concatenate([jnp.where(mk, blk["q"][:, g * LANE:(g + 1) * LANE], 0.0) for g in range(B_GROUP)],
                               axis=0).astype(BF16)
        s_p[b, kv] = _dot_nt(q_st, blk["kp"])
        s_c[b, kv] = _dot_nt(q_st, blk["kc"])
        vpm[b, kv] = jnp.where(mk, blk["vp"], 0.0).astype(BF16)
        vcm[b, kv] = jnp.where(mk, blk["vc"], 0.0).astype(BF16)

    slabs = [(b, kv, g) for b, kv in keys for g in range(B_GROUP)]
    sink = {(kv, g): sink_ref[kv * B_GROUP + g] for kv in range(B_KV_HEADS) for g in range(B_GROUP)}
    sp, sc, m = {}, {}, {}
    for b, kv, g in slabs:
        rs = slice(g * tq, (g + 1) * tq)
        sp[b, kv, g] = s_p[b, kv][rs] + (bp_ref[kv, rs, :] + blocks[b]["pen"])
        sc[b, kv, g] = s_c[b, kv][rs] + bc_ref[kv, rs, :]
        if tq == WINDOW:
            row_max = jnp.max(jnp.maximum(sp[b, kv, g], sc[b, kv, g]), axis=-1, keepdims=True)
        else:
            row_max = jnp.maximum(jnp.max(sp[b, kv, g], axis=-1, keepdims=True),
                                  jnp.max(sc[b, kv, g], axis=-1, keepdims=True))
        m[b, kv, g] = jnp.maximum(row_max, sink[kv, g])
    e_p, e_c, e_sum = {}, {}, {}
    for key in slabs:
        ep, ec = jnp.exp(sp[key] - m[key]), jnp.exp(sc[key] - m[key])
        if tq == WINDOW:
            e_sum[key] = jnp.sum(ep + ec, axis=-1, keepdims=True)
        else:
            e_sum[key] = jnp.sum(ep, axis=-1, keepdims=True) + jnp.sum(ec, axis=-1, keepdims=True)
        e_p[key], e_c[key] = ep.astype(BF16), ec.astype(BF16)

    tiles = {}
    for b, kv, g in slabs:
        inv = 1.0 / (e_sum[b, kv, g] + jnp.exp(sink[kv, g] - m[b, kv, g]))
        o = (_dot(e_p[b, kv, g], vpm[b, kv]) + _dot(e_c[b, kv, g], vcm[b, kv])) * inv
        tiles[b, g] = o if kv == 0 else tiles[b, g] + o

    for b, blk in enumerate(blocks):
        ssq = sum(jnp.sum(tiles[b, g] * tiles[b, g], axis=-1, keepdims=True) for g in range(B_GROUP))
        inv_rms = lax.rsqrt(ssq * (1.0 / B_WIDTH) + RMS_EPS)
        out = jnp.concatenate([tiles[b, g] for g in range(B_GROUP)], axis=1) * inv_rms * gn_ref[...]
        o_ref[blk["rs"]] = out.astype(o_ref.dtype)


def _swa_call(sink, q2d, kprev, vprev, kcur2d, vcur2d, gn, *, n_seq, n_blk, tq, n_bat, first_has_prev, prev_map,
              out_dtype, n_qb=1, cache_rows=0):
    assert n_seq % n_bat == 0 and (n_bat == 1 or n_blk * n_qb == 1) and (cache_rows == 0 or n_blk * n_qb == 1)
    row = lambda b, n: (b * n_blk + n, 0)
    body = functools.partial(_swa_body, tq=tq, n_bat=n_bat, n_qb=n_qb, t_real=cache_rows,
                             first_has_prev=first_has_prev)
    bias_p, bias_c = _swa_bias(tq)
    whole = lambda b, n: (0, 0, 0)
    blk_rows = n_bat * n_qb * tq
    out_specs = [pl.BlockSpec((blk_rows, B_WIDTH), row)]
    out_shape = [jax.ShapeDtypeStruct((n_seq * n_blk * n_qb * tq, B_WIDTH), out_dtype)]
    if cache_rows:
        out_specs += [pl.BlockSpec((n_bat, WINDOW, B_KV_WIDTH), prev_map)] * 2
        out_shape += [jax.ShapeDtypeStruct((n_seq, WINDOW, B_KV_WIDTH), F32)] * 2
    return pl.pallas_call(
        body,
        grid=(n_seq // n_bat, n_blk),
        in_specs=[pl.BlockSpec(memory_space=pltpu.SMEM),
                  pl.BlockSpec((blk_rows, B_WIDTH), row),
                  pl.BlockSpec((n_bat, WINDOW, B_KV_WIDTH), prev_map),
                  pl.BlockSpec((n_bat, WINDOW, B_KV_WIDTH), prev_map),
                  pl.BlockSpec((blk_rows, B_KV_WIDTH), row),
                  pl.BlockSpec((blk_rows, B_KV_WIDTH), row),
                  pl.BlockSpec(bias_p.shape, whole),
                  pl.BlockSpec(bias_c.shape, whole),
                  pl.BlockSpec((1, B_WIDTH), lambda b, n: (0, 0))],
        out_specs=out_specs,
        out_shape=out_shape,
        compiler_params=pltpu.CompilerParams(dimension_semantics=("arbitrary", "arbitrary"),
                                             vmem_limit_bytes=VMEM_LIMIT),
        name="swa",
    )(sink, q2d, kprev, vprev, kcur2d, vcur2d, bias_p, bias_c, gn)


def _route_cols(lg):
    row = lax.broadcasted_iota(jnp.int32, (LOGIT_ROWS, 1), 0)
    row_f = row.astype(F32)
    no_row = float(LOGIT_ROWS)
    is_group = row < N_GROUPS
    m_g = jnp.max(jnp.where(is_group, lg, MASK_NEG), axis=0, keepdims=True)
    g_idx = jnp.min(jnp.where(is_group & (lg == m_g), row_f, no_row), axis=0, keepdims=True)
    p_group = 1.0 / jnp.sum(jnp.where(is_group, jnp.exp(lg - m_g), 0.0), axis=0, keepdims=True)
    e_row = row - N_GROUPS
    in_group = (e_row >= 0) & (e_row < N_EXPERTS) & ((e_row >> 2).astype(F32) == g_idx)
    m_1 = jnp.max(jnp.where(in_group, lg, MASK_NEG), axis=0, keepdims=True)
    i_1 = jnp.min(jnp.where(in_group & (lg == m_1), row_f, no_row), axis=0, keepdims=True)
    rest = in_group & (row_f != i_1)
    m_2 = jnp.max(jnp.where(rest, lg, MASK_NEG), axis=0, keepdims=True)
    i_2 = jnp.min(jnp.where(rest & (lg == m_2), row_f, no_row), axis=0, keepdims=True)
    ratio = jnp.exp(m_2 - m_1)
    w_1 = p_group / (1.0 + ratio)
    out_row = lax.broadcasted_iota(jnp.int32, (ROUTE_ROWS, 1), 0)
    return jnp.where(out_row == 0, i_1 - N_GROUPS,
                     jnp.where(out_row == 1, i_2 - N_GROUPS,
                               jnp.where(out_row == 2, w_1, jnp.where(out_row == 3, w_1 * ratio, 0.0))))


def _outproj_body(*refs, seg_sizes):
    n_seg = len(seg_sizes)
    x_refs, oa_refs, ob_refs = refs[:n_seg], refs[n_seg:2 * n_seg], refs[2 * n_seg:3 * n_seg]
    wa_ref, wb_ref, g_ref, wr_ref, br_ref, x1_ref, h2_ref, rt_ref = refs[3 * n_seg:]

    def run(x_ref, oa_ref, ob_ref):
        x1 = x_ref[...] + _dot(oa_ref[...].astype(BF16), wa_ref[...]) + _dot(ob_ref[...].astype(BF16), wb_ref[...])
        x1_ref[...] = x1
        h2 = x1 * lax.rsqrt(jnp.mean(x1 * x1, axis=-1, keepdims=True) + RMS_EPS) * g_ref[...]
        h2b = h2.astype(BF16)
        h2_ref[...] = h2b
        rt_ref[...] = _route_cols(_dot_nt(wr_ref[...], h2b) + br_ref[...])

    _run_segment(pl.program_id(0), seg_sizes, run, list(zip(x_refs, oa_refs, ob_refs)))


def _outproj_call(x_list, x_firsts, oa_list, ob_list, seg_sizes, wa, wb, g, wr, br, *, tm):
    d = x_list[0].shape[1]
    n_tiles = sum(seg_sizes)
    n = n_tiles * tm
    row = lambda i: (i, 0)
    fixed = lambda i: (0, 0)
    zeros = [0] * len(seg_sizes)
    return pl.pallas_call(
        functools.partial(_outproj_body, seg_sizes=tuple(seg_sizes)),
        grid=(n_tiles,),
        in_specs=(_seg_specs(tm, d, seg_sizes, x_firsts) + _seg_specs(tm, A_WIDTH, seg_sizes, zeros)
                  + _seg_specs(tm, B_WIDTH, seg_sizes, zeros)
                  + [pl.BlockSpec((A_WIDTH, d), fixed), pl.BlockSpec((B_WIDTH, d), fixed), pl.BlockSpec((1, d), fixed),
                     pl.BlockSpec((LOGIT_ROWS, d), fixed), pl.BlockSpec((LOGIT_ROWS, 1), fixed)]),
        out_specs=[pl.BlockSpec((tm, d), row), pl.BlockSpec((tm, d), row), pl.BlockSpec((ROUTE_ROWS, tm), row)],
        out_shape=[jax.ShapeDtypeStruct((n, d), F32), jax.ShapeDtypeStruct((n, d), BF16),
                   jax.ShapeDtypeStruct((n_tiles * ROUTE_ROWS, tm), F32)],
        compiler_params=pltpu.CompilerParams(dimension_semantics=("arbitrary",), vmem_limit_bytes=VMEM_LIMIT),
        name="outproj",
    )(*x_list, *oa_list, *ob_list, wa, wb, g, wr, br)


def _expert_body(vb_ref, ve_ref, lo_ref, hi_ref, x_ref, wg_ref, wu_ref, wd_ref, y_ref, wg_bf, wu_bf, wd_bf):
    v = pl.program_id(0)
    pv = jnp.maximum(v - 1, 0)
    lo, hi = lo_ref[v], hi_ref[v]
    first_visit = (v == 0) | (vb_ref[v] != vb_ref[pv])

    @pl.when((v == 0) | (ve_ref[v] != ve_ref[pv]))
    def _():
        wg_bf[...] = wg_ref[0].astype(BF16)
        wu_bf[...] = wu_ref[0].astype(BF16)
        wd_bf[...] = wd_ref[0].astype(BF16)

    @pl.when(hi > lo)
    def _():
        x = x_ref[...]
        gate = _dot(x, wg_bf[...])
        up = _dot(x, wu_bf[...])
        mid = (gate * _sigmoid(gate) * up).astype(BF16)
        y = _dot(mid, wd_bf[...])
        row = lax.broadcasted_iota(jnp.int32, (y.shape[0], 1), 0)
        mine = (row >= lo) & (row < hi)
        whole = (lo == 0) & (hi == y.shape[0])

        @pl.when(whole)
        def _():
            y_ref[...] = y

        @pl.when(jnp.logical_not(whole) & first_visit)
        def _():
            y_ref[...] = jnp.where(mine, y, 0.0)

        @pl.when(jnp.logical_not(whole) & jnp.logical_not(first_visit))
        def _():
            y_ref[...] = jnp.where(mine, y, y_ref[...])


def _expert_call(visits, xs, wg, wu, wd, *, tb):
    n_rows, d = xs.shape
    ff = wg.shape[-1]
    vb, ve, lo, hi = visits
    blk = lambda v, vb, ve, lo, hi: (vb[v], 0)
    wsel = lambda v, vb, ve, lo, hi: (ve[v], 0, 0)
    grid_spec = pltpu.PrefetchScalarGridSpec(
        num_scalar_prefetch=4,
        grid=(vb.shape[0],),
        in_specs=[pl.BlockSpec((tb, d), blk), pl.BlockSpec((1, d, ff), wsel), pl.BlockSpec((1, d, ff), wsel),
                  pl.BlockSpec((1, ff, d), wsel)],
        out_specs=pl.BlockSpec((tb, d), blk),
        scratch_shapes=[pltpu.VMEM((d, ff), BF16), pltpu.VMEM((d, ff), BF16), pltpu.VMEM((ff, d), BF16)],
    )
    return pl.pallas_call(
        _expert_body,
        grid_spec=grid_spec,
        out_shape=jax.ShapeDtypeStruct((n_rows, d), F32),
        compiler_params=pltpu.CompilerParams(dimension_semantics=("arbitrary",), vmem_limit_bytes=VMEM_LIMIT),
        name="experts",
    )(vb, ve, lo, hi, xs, wg, wu, wd)


def _final_body(*refs, seg_sizes):
    n_seg = len(seg_sizes)
    g_ref, o_ref = refs[4 * n_seg:]

    def run(x1_ref, y0_ref, y1_ref, rt_ref):
        tm = rt_ref.shape[1]
        rt = jnp.concatenate([rt_ref[...], jnp.zeros((LANE - ROUTE_ROWS, tm), F32)], axis=0).T
        x = x1_ref[...] + (rt[:, 2:3] * y0_ref[...] + rt[:, 3:4] * y1_ref[...])
        o_ref[...] = x * lax.rsqrt(jnp.mean(x * x, axis=-1, keepdims=True) + RMS_EPS) * g_ref[...]

    _run_segment(pl.program_id(0), seg_sizes, run, [refs[4 * s:4 * s + 4] for s in range(n_seg)])


def _final_call(sources, firsts, seg_sizes, g, *, tm):
    d = sources[0][0].shape[1]
    n_tiles = sum(seg_sizes)
    in_specs, start = [], 0
    for size, first in zip(seg_sizes, firsts):
        src = lambda i, s=start, f=first, n=size: (f + jnp.clip(i - s, 0, n - 1), 0)
        in_specs += [pl.BlockSpec((tm, d), src)] * 3 + [pl.BlockSpec((ROUTE_ROWS, tm), src)]
        start += size
    return pl.pallas_call(
        functools.partial(_final_body, seg_sizes=tuple(seg_sizes)),
        grid=(n_tiles,),
        in_specs=in_specs + [pl.BlockSpec((1, d), lambda i: (0, 0))],
        out_specs=pl.BlockSpec((tm, d), lambda i: (i, 0)),
        out_shape=jax.ShapeDtypeStruct((n_tiles * tm, d), F32),
        compiler_params=pltpu.CompilerParams(dimension_semantics=("arbitrary",), vmem_limit_bytes=VMEM_LIMIT),
        name="final",
    )(*[a for src in sources for a in src], g)


def _dispatch(flat_e, n, tb):
    n_assign = n * TOP_K_INNER
    idx_bits = (n_assign - 1).bit_length()
    assert N_EXPERTS << idx_bits < 2 ** 31 and n_assign % tb == 0
    ids = jnp.arange(n_assign, dtype=jnp.int32)
    sorted_key = jnp.sort((flat_e << idx_bits) | ids)
    sorted_id = sorted_key & ((1 << idx_bits) - 1)
    src_tok = sorted_id - n * (sorted_id // n)

    experts = jnp.arange(N_EXPERTS, dtype=jnp.int32)
    onehot = (flat_e[:, None] == experts[None, :]).astype(jnp.int32)
    csum = jnp.cumsum(onehot, axis=0)
    counts = csum[-1]
    ends = jnp.cumsum(counts)
    starts = ends - counts
    dest = (jnp.sum((csum - 1 + starts[None, :]) * onehot, axis=1)).astype(jnp.int32)

    n_blocks = n_assign // tb
    n_visits = n_blocks + N_EXPERTS - 1
    first_blk = starts // tb
    n_vis_e = jnp.where(counts > 0, (ends + tb - 1) // tb - first_blk, 0)
    v_end = jnp.cumsum(n_vis_e)
    v = jnp.arange(n_visits, dtype=jnp.int32)
    valid = v < v_end[-1]
    last_e = jnp.max(jnp.where(counts > 0, experts, 0))
    ve = jnp.where(valid, jnp.sum((v_end[None, :] <= v[:, None]).astype(jnp.int32), axis=1), last_e)
    pick = lambda a: jnp.sum(jnp.where(ve[:, None] == experts[None, :], a[None, :], 0), axis=1)
    vb = pick(first_blk) + v - pick(v_end - n_vis_e)
    lo = jnp.maximum(pick(starts), vb * tb) - vb * tb
    hi = jnp.minimum(pick(ends), (vb + 1) * tb) - vb * tb
    vb = jnp.where(valid, vb, n_blocks - 1)
    lo = jnp.where(valid, lo, 0)
    hi = jnp.where(valid, hi, 0)
    return dest, src_tok, (vb.astype(jnp.int32), ve.astype(jnp.int32), lo.astype(jnp.int32), hi.astype(jnp.int32))


def _q_perm():
    cols = []
    for g in range(B_GROUP):
        for kv in range(B_KV_HEADS):
            h = kv * B_GROUP + g
            cols.extend(range(h * HEAD_DIM, (h + 1) * HEAD_DIM))
    return jnp.array(cols, dtype=jnp.int32)


def _layer(x_prompt, x_sample, state_rwkv, state_shift, cache_win_k, cache_win_v, prm, norm_final_g, *,
           chunk, n_sub, moe_block):
    bp, tp, d = x_prompt.shape
    bs, ts, _ = x_sample.shape
    n_p, n_s = bp * tp, bs * ts
    ts_pad = SUBLANE
    tm = n_s
    assert tp % tm == 0 and tm % 8 == 0 and cache_win_k.shape[1] == WINDOW
    xp = x_prompt.reshape(n_p, d)
    xs = x_sample.reshape(n_s, d)

    qp = _q_perm()
    w_in = prm["w_in"]
    w_in = jnp.concatenate([w_in[:, :SHIFT_W], w_in[:, SHIFT_W:SHIFT_W + B_WIDTH][:, qp],
                            w_in[:, SHIFT_W + B_WIDTH:]], axis=1).astype(BF16)
    gn = prm["attn_norm_g"][qp][None]
    w_out = prm["w_out"]
    wa, wb = w_out[:A_WIDTH].astype(BF16), w_out[A_WIDTH:][qp].astype(BF16)
    pad_rows_r = jnp.zeros((LOGIT_ROWS - N_GROUPS - N_EXPERTS, d), F32)
    wr = jnp.concatenate([prm["w_route_group"].T, prm["w_route_expert"].T, pad_rows_r], axis=0).astype(BF16)
    br = jnp.concatenate([prm["b_route_group"], prm["b_route_expert"], pad_rows_r[:, 0]])[:, None]

    zero_blk = jnp.zeros((LORA_W // 2, A_WIDTH), F32)
    wlora = jnp.concatenate([jnp.concatenate([prm["w_decay_up"], zero_blk], axis=1),
                             jnp.concatenate([zero_blk, prm["w_iclr_up"]], axis=1)], axis=0).astype(BF16)
    rp = dict(mu=prm["mu_shift"][None], w0=prm["w_decay0"][None], a0=prm["w_iclr0"][None], wlora=wlora,
              wgate=prm["w_gate_up"].astype(BF16), kk=prm["k_k"][None], ka=prm["k_a"][None],
              rk=prm["r_k"].reshape(1, A_WIDTH), lng=prm["lnx_g"][None], lnb=prm["lnx_b"][None])

    groups = [(0, bp)]
    tiles_per_seq = tp // tm
    nb = tp // WINDOW
    n_qb = _row_tile(nb, SWA_BLOCKS_PER_STEP)
    kv4 = lambda a: a.reshape(a.shape[0], a.shape[1], B_KV_HEADS, HEAD_DIM)
    finals, s_p, sh_p, k_p, v_p = [], [], [], [], []
    for gi, (s0, s1) in enumerate(groups):
        has_sample = gi == len(groups) - 1
        n_seq = s1 - s0
        n_pt = n_seq * tiles_per_seq
        n_pg = n_pt * tm
        seg_sizes = [n_pt] + ([1] if has_sample else [])
        pa, q, k, v = _inproj_call([xp] + ([xs] if has_sample else []), [s0 * tiles_per_seq, 0], seg_sizes,
                                   prm["norm_mix_g"][None], w_in, tm=tm)

        oa_p, s_pg = _rwkv_call(pa, jnp.zeros((n_seq, 1, SHIFT_W), F32),
                                jnp.zeros((n_seq,) + state_rwkv.shape[1:], F32), rp, n_seq=n_seq,
                                n_chunks=tp // chunk, chunk=chunk, n_sub=n_sub, n_bat=1, t_real=chunk)
        ob_p, = _swa_call(prm["attn_sink"], q, k.reshape(-1, WINDOW, B_KV_WIDTH), v.reshape(-1, WINDOW, B_KV_WIDTH),
                          k, v, gn, n_seq=n_seq, n_blk=nb // n_qb, tq=WINDOW, n_bat=1, n_qb=n_qb,
                          first_has_prev=False, out_dtype=BF16,
                          prev_map=lambda b, i: (b * nb + jnp.maximum(i * n_qb - 1, 0), 0, 0))
        oa_list, ob_list = [oa_p], [ob_p]
        if has_sample:
            pad_rows = lambda a: jnp.pad(a[n_pg:].reshape(bs, ts, -1), ((0, 0), (0, ts_pad - ts), (0, 0))).reshape(
                bs * ts_pad, -1)
            oa_s, s_s = _rwkv_call(pad_rows(pa), state_shift[:, None, :], state_rwkv, rp, n_seq=bs,
                                   n_chunks=1, chunk=ts_pad, n_sub=1, n_bat=_row_tile(bs, SAMPLE_SEQS_PER_STEP), t_real=ts)
            ob_s, k_s, v_s = _swa_call(prm["attn_sink"], pad_rows(q).astype(F32),
                                       cache_win_k.reshape(bs, WINDOW, B_KV_WIDTH),
                                       cache_win_v.reshape(bs, WINDOW, B_KV_WIDTH), pad_rows(k), pad_rows(v), gn,
                                       n_seq=bs, n_blk=1, tq=ts_pad, n_bat=_row_tile(bs, SAMPLE_SEQS_PER_STEP), first_has_prev=True,
                                       prev_map=lambda b, i: (b, 0, 0), out_dtype=F32, cache_rows=ts)
            oa_list.append(oa_s.reshape(bs, ts_pad, A_WIDTH)[:, :ts].reshape(n_s, A_WIDTH))
            ob_list.append(ob_s.reshape(bs, ts_pad, B_WIDTH)[:, :ts].reshape(n_s, B_WIDTH))
            sh_s = pa[n_pg:].reshape(bs, ts, SHIFT_W)[:, -1]

        x1, h2, route = _outproj_call([xp] + ([xs] if has_sample else []), [s0 * tiles_per_seq, 0],
                                               oa_list, ob_list, seg_sizes, wa, wb, prm["norm_ffn_g"][None], wr, br,
                                               tm=tm)
        n = sum(seg_sizes) * tm
        route_t = route.reshape(n // tm, ROUTE_ROWS, tm)
        flat_e = jnp.concatenate([route_t[:, j, :].reshape(n) for j in range(TOP_K_INNER)]).astype(jnp.int32)
        dest, src_tok, visits = _dispatch(flat_e, n, moe_block)
        ybuf = _expert_call(visits, h2[src_tok], prm["w_exp_gate"], prm["w_exp_up"], prm["w_exp_down"], tb=moe_block)
        finals.append((x1, ybuf[dest[:n]], ybuf[dest[n:]], route))

        last_rows = lambda a, m: jnp.stack([a[(b + 1) * tp - m:(b + 1) * tp] for b in range(n_seq)], axis=0)
        s_p.append(s_pg)
        sh_p.append(last_rows(pa, 1)[:, 0])
        k_p.append(last_rows(k, WINDOW))
        v_p.append(last_rows(v, WINDOW))

    gf = norm_final_g[None]
    n_pts = [(s1 - s0) * tiles_per_seq for s0, s1 in groups]
    y_p = _final_call(finals, [0] * len(groups), n_pts, gf, tm=tm)
    y_s = _final_call(finals[-1:], n_pts[-1:], [1], gf, tm=tm)
    cat = lambda parts: jnp.concatenate(parts, axis=0)
    return (y_p.reshape(bp, tp, d), y_s.reshape(bs, ts, d), cat(s_p), cat(sh_p), kv4(cat(k_p)), kv4(cat(v_p)),
            s_s, sh_s, kv4(k_s), kv4(v_s))


def kernel(x_prompt, x_sample, state_rwkv, state_shift, cache_win_k, cache_win_v, norm_mix_g, w_in, mu_shift, w_decay0, w_decay_up, w_iclr0, w_iclr_up, w_gate_up, k_k, k_a, r_k, lnx_g, lnx_b, attn_sink, attn_norm_g, w_out, norm_ffn_g, w_route_group, b_route_group, w_route_expert, b_route_expert, w_exp_gate, w_exp_up, w_exp_down, norm_final_g):
    assert norm_mix_g.shape[0] == 1, "single-layer trunk"
    prm = dict(norm_mix_g=norm_mix_g[0], w_in=w_in[0], mu_shift=mu_shift[0], w_decay0=w_decay0[0],
               w_decay_up=w_decay_up[0], w_iclr0=w_iclr0[0], w_iclr_up=w_iclr_up[0], w_gate_up=w_gate_up[0],
               k_k=k_k[0], k_a=k_a[0], r_k=r_k[0], lnx_g=lnx_g[0], lnx_b=lnx_b[0], attn_sink=attn_sink[0],
               attn_norm_g=attn_norm_g[0], w_out=w_out[0], norm_ffn_g=norm_ffn_g[0],
               w_route_group=w_route_group[0], b_route_group=b_route_group[0],
               w_route_expert=w_route_expert[0], b_route_expert=b_route_expert[0],
               w_exp_gate=w_exp_gate[0], w_exp_up=w_exp_up[0], w_exp_down=w_exp_down[0])
    outs = _layer(x_prompt, x_sample, state_rwkv[0], state_shift[0], cache_win_k[0], cache_win_v[0], prm,
                  norm_final_g, chunk=RWKV_CHUNK, n_sub=RWKV_CHUNKS_PER_STEP, moe_block=MOE_BLOCK)
    y_p, y_s, s_p, sh_p, kp, vp, s_s, sh_s, ks, vs = outs
    return (y_p, y_s, s_p[None], sh_p[None], kp[None], vp[None], s_s[None], sh_s[None], ks[None], vs[None])
```

```python
import functools
import math

import jax
import jax.numpy as jnp
from jax import lax
from jax.experimental import pallas as pl
from jax.experimental.pallas import tpu as pltpu

F32 = jnp.float32
BF16 = jnp.bfloat16

HEAD_DIM = 64
A_HEADS = 8
A_WIDTH = A_HEADS * HEAD_DIM
B_HEADS = 8
B_KV_HEADS = 2
B_GROUP = B_HEADS // B_KV_HEADS
B_WIDTH = B_HEADS * HEAD_DIM
B_KV_WIDTH = B_KV_HEADS * HEAD_DIM
DECAY_LORA = 64
ICLR_LORA = 64
GATE_LORA = 128
LORA_W = DECAY_LORA + ICLR_LORA
SHIFT_W = 3 * A_WIDTH + LORA_W + GATE_LORA
IN_W = SHIFT_W + B_WIDTH + 2 * B_KV_WIDTH
WINDOW = 128
N_GROUPS = 4
EXPERTS_PER_GROUP = 4
N_EXPERTS = N_GROUPS * EXPERTS_PER_GROUP
TOP_K_INNER = 2
EXPERT_FF = 512
RMS_EPS = 1e-6
LNX_EPS = 64e-5
DECAY_OFFSET = 0.5

LANE = 128
PAIR = LANE // HEAD_DIM
N_PAIRS = A_HEADS // PAIR
ROUTE_ROWS = 8
LOGIT_ROWS = 32
MASK_NEG = -1e30
VMEM_LIMIT = 48 * 1024 * 1024
SUBLANE = 8

RWKV_CHUNK = 64
RWKV_CHUNKS_PER_STEP = 8
SWA_BLOCKS_PER_STEP = 8
SAMPLE_SEQS_PER_STEP = 16
MOE_BLOCK = 1024

LOG2_E = math.log2(math.e)
DECAY_SCALE = math.exp(-DECAY_OFFSET)
HEAD_SHIFT = HEAD_DIM.bit_length() - 1
HI = lax.Precision.HIGHEST


def _row_tile(n, cap):
    t = cap
    while n % t:
        t //= 2
    return t


def _dot(a, b, precision=None):
    return jnp.dot(a, b, preferred_element_type=F32, precision=precision)


def _dot_nt(a, b, precision=None):
    return lax.dot_general(a, b, (((1,), (1,)), ((), ())), preferred_element_type=F32, precision=precision)


def _dot_tn(a, b, precision=None):
    return lax.dot_general(a, b, (((0,), (0,)), ((), ())), preferred_element_type=F32, precision=precision)


def _sigmoid(x):
    return 1.0 / (1.0 + jnp.exp(-x))


def _hi_lo(x, axis):
    hi = x.astype(BF16)
    lo = (x - hi.astype(F32)).astype(BF16)
    return jnp.concatenate([hi, lo], axis=axis)


def _run_segment(i, seg_sizes, run, seg_refs):
    start = 0
    for size, refs in zip(seg_sizes, seg_refs):
        pl.when((i >= start) & (i < start + size))(functools.partial(run, *refs))
        start += size


def _seg_specs(tm, width, seg_sizes, firsts):
    specs, start = [], 0
    for size, first in zip(seg_sizes, firsts):
        specs.append(pl.BlockSpec((tm, width),
                                  lambda i, s=start, f=first, n=size: (f + jnp.clip(i - s, 0, n - 1), 0)))
        start += size
    return specs


def _inproj_body(*refs, seg_sizes):
    n_seg = len(seg_sizes)
    g_ref, w_ref, pa_ref, q_ref, k_ref, v_ref = refs[n_seg:]

    def run(x_ref):
        x = x_ref[...]
        h = x * lax.rsqrt(jnp.mean(x * x, axis=-1, keepdims=True) + RMS_EPS) * g_ref[...]
        p = _dot(h.astype(BF16), w_ref[...])
        pa_ref[...] = p[:, :SHIFT_W]
        q_ref[...] = (p[:, SHIFT_W:SHIFT_W + B_WIDTH] * (HEAD_DIM ** -0.5)).astype(BF16)
        k_ref[...] = p[:, SHIFT_W + B_WIDTH:SHIFT_W + B_WIDTH + B_KV_WIDTH]
        v_ref[...] = p[:, SHIFT_W + B_WIDTH + B_KV_WIDTH:]

    _run_segment(pl.program_id(0), seg_sizes, run, [(r,) for r in refs[:n_seg]])


def _inproj_call(xs_list, firsts, seg_sizes, g, w_bf16, *, tm):
    d = xs_list[0].shape[1]
    n_tiles = sum(seg_sizes)
    n = n_tiles * tm
    row = lambda i: (i, 0)
    fixed = lambda i: (0, 0)
    return pl.pallas_call(
        functools.partial(_inproj_body, seg_sizes=tuple(seg_sizes)),
        grid=(n_tiles,),
        in_specs=_seg_specs(tm, d, seg_sizes, firsts) + [pl.BlockSpec((1, d), fixed), pl.BlockSpec((d, IN_W), fixed)],
        out_specs=[pl.BlockSpec((tm, SHIFT_W), row), pl.BlockSpec((tm, B_WIDTH), row),
                   pl.BlockSpec((tm, B_KV_WIDTH), row), pl.BlockSpec((tm, B_KV_WIDTH), row)],
        out_shape=[jax.ShapeDtypeStruct((n, SHIFT_W), F32), jax.ShapeDtypeStruct((n, B_WIDTH), BF16),
                   jax.ShapeDtypeStruct((n, B_KV_WIDTH), F32), jax.ShapeDtypeStruct((n, B_KV_WIDTH), F32)],
        compiler_params=pltpu.CompilerParams(dimension_semantics=("arbitrary",), vmem_limit_bytes=VMEM_LIMIT),
        name="inproj",
    )(*xs_list, g, w_bf16)


def _rwkv_body(pa_ref, shift0_ref, s0_ref, mu_ref, w0_ref, a0_ref, wlora_ref, wgate_ref, kk_ref, ka_ref,
               rk_ref, lng_ref, lnb_ref, oa_ref, sout_ref, s_scr, prev_scr, *, chunk, n_sub, n_bat, t_real):
    C = chunk
    seq_rows = n_sub * C
    n_seg = n_bat * n_sub
    rows = n_seg * C
    c = pl.program_id(1)

    @pl.when(c == 0)
    def _():
        prev_scr[...] = shift0_ref[...]
        for b in range(n_bat):
            for j in range(N_PAIRS):
                s_scr[b * N_PAIRS + j] = jnp.concatenate([s0_ref[b, PAIR * j + h] for h in range(PAIR)], axis=1)

    pa = pa_ref[...]
    row = lax.broadcasted_iota(jnp.int32, (rows, 1), 0)
    row_in_chunk = row & (C - 1)
    pa_prev = pltpu.roll(pa, 1, axis=0)
    for b in range(n_bat):
        pa_prev = jnp.where(row == b * seq_rows, prev_scr[b], pa_prev)
        last = (b + 1) * seq_rows - C + t_real - 1
        prev_scr[b] = pa[last:last + 1]
    xm = pa + mu_ref[...] * (pa_prev - pa)

    r = xm[:, :A_WIDTH]
    k = xm[:, A_WIDTH:2 * A_WIDTH]
    v = xm[:, 2 * A_WIDTH:3 * A_WIDTH]
    lora_in = xm[:, 3 * A_WIDTH:3 * A_WIDTH + LORA_W]
    gd = xm[:, 3 * A_WIDTH + LORA_W:]

    lane = lax.broadcasted_iota(jnp.int32, (1, LANE), 1)
    lo_half = lane < HEAD_DIM
    z = jnp.where(lo_half, jnp.tanh(lora_in), lora_in)
    lw = _dot(z.astype(BF16), wlora_ref[...])
    dec_pre = w0_ref[...] + lw[:, :A_WIDTH]
    a = _sigmoid(a0_ref[...] + lw[:, A_WIDTH:])
    logdec = (-LOG2_E * DECAY_SCALE) * _sigmoid(dec_pre)
    g = _dot(_sigmoid(gd).astype(BF16), wgate_ref[...])

    ri = lax.broadcasted_iota(jnp.int32, (LANE, LANE), 0)
    ci = lax.broadcasted_iota(jnp.int32, (LANE, LANE), 1)
    same_head = (ri >> HEAD_SHIFT) == (ci >> HEAD_SHIFT)
    seg = jnp.where(same_head, 1.0, 0.0).astype(BF16)
    seg2 = jnp.concatenate([seg, seg], axis=0)

    def headsum(x):
        return jnp.concatenate([_dot(_hi_lo(x[:, j * LANE:(j + 1) * LANE], 1), seg2) for j in range(N_PAIRS)],
                               axis=1)

    kk = k * kk_ref[...]
    kk = kk * lax.rsqrt(jnp.maximum(headsum(kk * kk), 1e-24))
    k = k * (1.0 + (a - 1.0) * ka_ref[...])
    bonus = headsum(r * k * rk_ref[...]) * v

    if t_real < C:
        valid = row_in_chunk < t_real
        logdec = jnp.where(valid, logdec, 0.0)
        kk = jnp.where(valid, kk, 0.0)
        k = jnp.where(valid, k, 0.0)
        v = jnp.where(valid, v, 0.0)

    cum = logdec
    shift = 1
    while shift < C:
        cum = cum + jnp.where(row_in_chunk >= shift, pltpu.roll(cum, shift, axis=0), 0.0)
        shift *= 2
    ends = [cum[(s + 1) * C - 1:(s + 1) * C] for s in range(n_seg)]
    cum_end = jnp.concatenate([jnp.broadcast_to(e, (C, A_WIDTH)) for e in ends], axis=0) if n_seg > 1 else ends[0]
    w_incl = jnp.exp2(cum)
    w_prev = jnp.exp2(cum - logdec)
    w_inv = jnp.exp2(-cum)
    w_end = jnp.exp2(cum_end - cum)
    w_chunk = [jnp.exp2(e) for e in ends]
    kka = kk * a
    terms = dict(A=-kk * w_prev, R=r * w_incl, B=kka * w_inv, K=k * w_inv, V=v, Be=kka * w_end, Ke=k * w_end)

    C2 = PAIR * C
    t_idx = lax.broadcasted_iota(jnp.int32, (C, 1), 0)
    i_idx = lax.broadcasted_iota(jnp.int32, (1, C2), 1) & (C - 1)
    strict = i_idx < t_idx
    incl = i_idx <= t_idx
    ident = jnp.where(i_idx == t_idx, 1.0, 0.0).astype(F32)
    lo_time = lax.broadcasted_iota(jnp.int32, (1, C2), 1) < C
    n_levels = max(1, (C - 1).bit_length())
    bf = lambda x: x.astype(BF16)

    def bd(x, lo_mask=lo_half):
        return bf(jnp.concatenate([jnp.where(lo_mask, x, 0.0), jnp.where(lo_mask, 0.0, x)], axis=0))

    chains = [(s, j) for s in range(n_seg) for j in range(N_PAIRS)]
    tile = lambda name, s, j: terms[name][s * C:(s + 1) * C, j * LANE:(j + 1) * LANE]
    r_sbs = {ch: tile("R", *ch) for ch in chains}
    stk = {ch: {nm: bd(tile(nm, *ch)) for nm in ("A", "B", "K", "V")} for ch in chains}
    m_ab, aak, arb, ark = {}, {}, {}, {}
    for ch in chains:
        t = stk[ch]
        a_sbs, rb = bf(tile("A", *ch)), bf(r_sbs[ch])
        if C2 % LANE == 0:
            m1 = _dot_nt(jnp.concatenate([a_sbs, rb], axis=0), jnp.concatenate([t["B"], t["K"]], axis=0))
            ab, ak, rbm, rk = m1[:C, :C2], m1[:C, C2:], m1[C:, :C2], m1[C:, C2:]
        else:
            ab, ak = _dot_nt(a_sbs, t["B"]), _dot_nt(a_sbs, t["K"])
            rbm, rk = _dot_nt(rb, t["B"]), _dot_nt(rb, t["K"])
        m_ab[ch] = jnp.where(strict, ab, 0.0)
        aak[ch] = jnp.where(strict, ak, 0.0)
        arb[ch] = bf(jnp.where(incl, rbm, 0.0))
        ark[ch] = jnp.where(incl, rk, 0.0)

    nn = dict(m_ab)
    tinv = {ch: ident + m_ab[ch] for ch in chains}
    for lvl in range(1, n_levels):
        if lvl == 1:
            for ch in chains:
                nn[ch] = _dot(bf(nn[ch]), bd(nn[ch], lo_time))
            continue
        for ch in chains:
            both = _dot(bf(jnp.concatenate([nn[ch], tinv[ch]], axis=0)), bd(nn[ch], lo_time))
            nn[ch] = both[:C]
            tinv[ch] = tinv[ch] + both[C:]
    if n_levels > 1:
        for ch in chains:
            tinv[ch] = tinv[ch] + _dot(bf(tinv[ch]), bd(nn[ch], lo_time))
    av = {ch: _dot(bf(jnp.concatenate([aak[ch], ark[ch]], axis=0)), stk[ch]["V"]) for ch in chains}
    x = {ch: _dot(bf(tinv[ch]), jnp.concatenate([stk[ch]["A"], bd(av[ch][:C])], axis=1)) for ch in chains}
    a_eff = {ch: bd(x[ch][:, :LANE]) for ch in chains}
    v_eff = {ch: bd(x[ch][:, LANE:]) for ch in chains}
    zed = {ch: _dot(arb[ch], jnp.concatenate([a_eff[ch], v_eff[ch]], axis=1)) for ch in chains}
    rhat = {ch: bf(r_sbs[ch] + zed[ch][:, :LANE]) for ch in chains}
    y0 = {ch: av[ch][C:] + zed[ch][:, LANE:] for ch in chains}
    p_mat = {ch: bf(jnp.where(same_head, _dot_tn(bf(tile("Be", *ch)), bf(x[ch][:, :LANE])), 0.0))
             for ch in chains}
    q_full = {ch: _dot_tn(bf(jnp.concatenate([x[ch][:, LANE:], tile("V", *ch)], axis=0)),
                          bf(jnp.concatenate([tile("Be", *ch), tile("Ke", *ch)], axis=0))) for ch in chains}

    y_rows = []
    for b in range(n_bat):
        state = [s_scr[b * N_PAIRS + j] for j in range(N_PAIRS)]
        for sub in range(n_sub):
            s = b * n_sub + sub
            ys = []
            for j in range(N_PAIRS):
                ch = (s, j)
                hi = bf(state[j])
                lo = bf(state[j] - hi.astype(F32))
                yy = _dot_nt(rhat[ch], jnp.concatenate([bd(hi), bd(lo)], axis=0))
                ys.append(yy[:, :LANE] + yy[:, LANE:] + y0[ch])
                sp = _dot_nt(jnp.concatenate([hi, lo], axis=0), p_mat[ch])
                q_sbs = jnp.where(lo_half, q_full[ch][:HEAD_DIM], q_full[ch][HEAD_DIM:])
                state[j] = state[j] * w_chunk[s][:, j * LANE:(j + 1) * LANE] + sp[:HEAD_DIM] + sp[HEAD_DIM:] + q_sbs
            y_rows.append(jnp.concatenate(ys, axis=1))
        for j in range(N_PAIRS):
            s_scr[b * N_PAIRS + j] = state[j]

    y = jnp.concatenate(y_rows, axis=0) if n_seg > 1 else y_rows[0]
    inv_n = 1.0 / HEAD_DIM
    mean = headsum(y) * inv_n
    d = y - mean
    var = headsum(d * d) * inv_n
    yn = d * lax.rsqrt(var + LNX_EPS) * lng_ref[...] + lnb_ref[...]
    oa_ref[...] = ((yn + bonus) * g).astype(oa_ref.dtype)

    @pl.when(c == pl.num_programs(1) - 1)
    def _():
        for b in range(n_bat):
            for j in range(N_PAIRS):
                for h in range(PAIR):
                    sout_ref[b, PAIR * j + h] = s_scr[b * N_PAIRS + j][:, h * HEAD_DIM:(h + 1) * HEAD_DIM]


def _rwkv_call(pa2d, shift0, s0_pairs, prm, *, n_seq, n_chunks, chunk, n_sub, n_bat, t_real):
    n_steps = n_chunks // n_sub
    assert n_seq % n_bat == 0 and (n_bat == 1 or n_steps == 1)
    row = lambda b, c: (b * n_steps + c, 0)
    seq3 = lambda b, c: (b, 0, 0)
    fixed = lambda b, c: (0, 0)
    vec = lambda w: pl.BlockSpec((1, w), fixed)
    body = functools.partial(_rwkv_body, chunk=chunk, n_sub=n_sub, n_bat=n_bat, t_real=t_real)
    n_rows = n_seq * n_chunks * chunk
    blk_rows = n_bat * n_sub * chunk
    state_blk = (n_bat, A_HEADS, HEAD_DIM, HEAD_DIM)
    seq4 = lambda b, c: (b, 0, 0, 0)
    return pl.pallas_call(
        body,
        grid=(n_seq // n_bat, n_steps),
        in_specs=[pl.BlockSpec((blk_rows, SHIFT_W), row),
                  pl.BlockSpec((n_bat, 1, SHIFT_W), seq3),
                  pl.BlockSpec(state_blk, seq4),
                  vec(SHIFT_W), vec(A_WIDTH), vec(A_WIDTH),
                  pl.BlockSpec((LORA_W, 2 * A_WIDTH), fixed), pl.BlockSpec((GATE_LORA, A_WIDTH), fixed),
                  vec(A_WIDTH), vec(A_WIDTH), vec(A_WIDTH), vec(A_WIDTH), vec(A_WIDTH)],
        out_specs=[pl.BlockSpec((blk_rows, A_WIDTH), row), pl.BlockSpec(state_blk, seq4)],
        out_shape=[jax.ShapeDtypeStruct((n_rows, A_WIDTH), BF16),
                   jax.ShapeDtypeStruct((n_seq, A_HEADS, HEAD_DIM, HEAD_DIM), F32)],
        scratch_shapes=[pltpu.VMEM((n_bat * N_PAIRS, HEAD_DIM, LANE), F32), pltpu.VMEM((n_bat, 1, SHIFT_W), F32)],
        compiler_params=pltpu.CompilerParams(dimension_semantics=("arbitrary", "arbitrary"),
                                             vmem_limit_bytes=VMEM_LIMIT),
        name="rwkv",
    )(pa2d, shift0, s0_pairs, prm["mu"], prm["w0"], prm["a0"], prm["wlora"], prm["wgate"], prm["kk"], prm["ka"],
      prm["rk"], prm["lng"], prm["lnb"])


def _swa_bias(tq):
    rows = B_GROUP * tq
    grp = jnp.arange(rows, dtype=jnp.int32)[:, None] // tq
    t = jnp.arange(rows, dtype=jnp.int32)[:, None] % tq
    dist_p = t + WINDOW - jnp.arange(WINDOW, dtype=jnp.int32)[None, :]
    dist_c = t - jnp.arange(tq, dtype=jnp.int32)[None, :]

    def bias(dist, kv):
        slope = sum(jnp.where(grp == g, 2.0 ** -(kv * B_GROUP + g + 1), 0.0) for g in range(B_GROUP))
        return jnp.where((dist >= 0) & (dist < WINDOW), -slope * dist.astype(F32), MASK_NEG)

    return (jnp.stack([bias(dist_p, kv) for kv in range(B_KV_HEADS)]),
            jnp.stack([bias(dist_c, kv) for kv in range(B_KV_HEADS)]))


def _swa_body(sink_ref, q_ref, kp_ref, vp_ref, kc_ref, vc_ref, bp_ref, bc_ref, gn_ref, o_ref, *cache_refs, tq, n_bat,
              n_qb, t_real, first_has_prev):
    nblk = pl.program_id(1)
    lane = lax.broadcasted_iota(jnp.int32, (1, LANE), 1)
    kv_masks = [(lane < HEAD_DIM) if kv == 0 else (lane >= HEAD_DIM) for kv in range(B_KV_HEADS)]
    blocks = []
    for s in range(n_bat * n_qb):
        rs = slice(s * tq, (s + 1) * tq)
        kc, vc = kc_ref[rs], vc_ref[rs]
        if n_qb > 1 and s > 0:
            ps = slice((s - 1) * tq, s * tq)
            kp, vp, has_prev = kc_ref[ps], vc_ref[ps], True
        else:
            kp, vp, has_prev = kp_ref[s], vp_ref[s], first_has_prev
        if cache_refs:
            row = lax.broadcasted_iota(jnp.int32, (WINDOW, 1), 0)
            for out_ref, old, new in zip(cache_refs, (kp, vp), (kc, vc)):
                new_tail = jnp.concatenate([pltpu.roll(new, tq - t_real, axis=0)] * (WINDOW // tq), axis=0)
                out_ref[s] = jnp.where(row >= WINDOW - t_real, new_tail, pltpu.roll(old, WINDOW - t_real, axis=0))
        pen = 0.0 if has_prev else jnp.where(nblk > 0, 0.0, MASK_NEG)
        blocks.append(dict(rs=rs, q=q_ref[rs], kp=kp.astype(BF16), kc=kc.astype(BF16), vp=vp, vc=vc, pen=pen))

    keys = [(b, kv) for b in range(len(blocks)) for kv in range(B_KV_HEADS)]
    s_p, s_c, vpm, vcm = {}, {}, {}, {}
    for b, kv in keys:
        blk, mk = blocks[b], kv_masks[kv]
        q_st = jnp.concatenate([jnp.where(mk, blk["q"][:, g * LANE:(g + 1) * LANE], 0.0) for g in range(B_GROUP)],
                               axis=0).astype(BF16)
        s_p[b, kv] = _dot_nt(q_st, blk["kp"])
        s_c[b, kv] = _dot_nt(q_st, blk["kc"])
        vpm[b, kv] = jnp.where(mk, blk["vp"], 0.0).astype(BF16)
        vcm[b, kv] = jnp.where(mk, blk["vc"], 0.0).astype(BF16)

    slabs = [(b, kv, g) for b, kv in keys for g in range(B_GROUP)]
    sink = {(kv, g): sink_ref[kv * B_GROUP + g] for kv in range(B_KV_HEADS) for g in range(B_GROUP)}
    sp, sc, m = {}, {}, {}
    for b, kv, g in slabs:
        rs = slice(g * tq, (g + 1) * tq)
        sp[b, kv, g] = s_p[b, kv][rs] + (bp_ref[kv, rs, :] + blocks[b]["pen"])
        sc[b, kv, g] = s_c[b, kv][rs] + bc_ref[kv, rs, :]
        if tq == WINDOW:
            row_max = jnp.max(jnp.maximum(sp[b, kv, g], sc[b, kv, g]), axis=-1, keepdims=True)
        else:
            row_max = jnp.maximum(jnp.max(sp[b, kv, g], axis=-1, keepdims=True),
                                  jnp.max(sc[b, kv, g], axis=-1, keepdims=True))
        m[b, kv, g] = jnp.maximum(row_max, sink[kv, g])
    e_p, e_c, e_sum = {}, {}, {}
    for key in slabs:
        ep, ec = jnp.exp(sp[key] - m[key]), jnp.exp(sc[key] - m[key])
        if tq == WINDOW:
            e_sum[key] = jnp.sum(ep + ec, axis=-1, keepdims=True)
        else:
            e_sum[key] = jnp.sum(ep, axis=-1, keepdims=True) + jnp.sum(ec, axis=-1, keepdims=True)
        e_p[key], e_c[key] = ep.astype(BF16), ec.astype(BF16)

    tiles = {}
    for b, kv, g in slabs:
        inv = 1.0 / (e_sum[b, kv, g] + jnp.exp(sink[kv, g] - m[b, kv, g]))
        o = (_dot(e_p[b, kv, g], vpm[b, kv]) + _dot(e_c[b, kv, g], vcm[b, kv])) * inv
        tiles[b, g] = o if kv == 0 else tiles[b, g] + o

    for b, blk in enumerate(blocks):
        ssq = sum(jnp.sum(tiles[b, g] * tiles[b, g], axis=-1, keepdims=True) for g in range(B_GROUP))
        inv_rms = lax.rsqrt(ssq * (1.0 / B_WIDTH) + RMS_EPS)
        out = jnp.concatenate([tiles[b, g] for g in range(B_GROUP)], axis=1) * inv_rms * gn_ref[...]
        o_ref[blk["rs"]] = out.astype(o_ref.dtype)


def _swa_call(sink, q2d, kprev, vprev, kcur2d, vcur2d, gn, *, n_seq, n_blk, tq, n_bat, first_has_prev, prev_map,
              out_dtype, n_qb=1, cache_rows=0):
    assert n_seq % n_bat == 0 and (n_bat == 1 or n_blk * n_qb == 1) and (cache_rows == 0 or n_blk * n_qb == 1)
    row = lambda b, n: (b * n_blk + n, 0)
    body = functools.partial(_swa_body, tq=tq, n_bat=n_bat, n_qb=n_qb, t_real=cache_rows,
                             first_has_prev=first_has_prev)
    bias_p, bias_c = _swa_bias(tq)
    whole = lambda b, n: (0, 0, 0)
    blk_rows = n_bat * n_qb * tq
    out_specs = [pl.BlockSpec((blk_rows, B_WIDTH), row)]
    out_shape = [jax.ShapeDtypeStruct((n_seq * n_blk * n_qb * tq, B_WIDTH), out_dtype)]
    if cache_rows:
        out_specs += [pl.BlockSpec((n_bat, WINDOW, B_KV_WIDTH), prev_map)] * 2
        out_shape += [jax.ShapeDtypeStruct((n_seq, WINDOW, B_KV_WIDTH), F32)] * 2
    return pl.pallas_call(
        body,
        grid=(n_seq // n_bat, n_blk),
        in_specs=[pl.BlockSpec(memory_space=pltpu.SMEM),
                  pl.BlockSpec((blk_rows, B_WIDTH), row),
                  pl.BlockSpec((n_bat, WINDOW, B_KV_WIDTH), prev_map),
                  pl.BlockSpec((n_bat, WINDOW, B_KV_WIDTH), prev_map),
                  pl.BlockSpec((blk_rows, B_KV_WIDTH), row),
                  pl.BlockSpec((blk_rows, B_KV_WIDTH), row),
                  pl.BlockSpec(bias_p.shape, whole),
                  pl.BlockSpec(bias_c.shape, whole),
                  pl.BlockSpec((1, B_WIDTH), lambda b, n: (0, 0))],
        out_specs=out_specs,
        out_shape=out_shape,
        compiler_params=pltpu.CompilerParams(dimension_semantics=("arbitrary", "arbitrary"),
                                             vmem_limit_bytes=VMEM_LIMIT),
        name="swa",
    )(sink, q2d, kprev, vprev, kcur2d, vcur2d, bias_p, bias_c, gn)


def _route_cols(lg):
    row = lax.broadcasted_iota(jnp.int32, (LOGIT_ROWS, 1), 0)
    row_f = row.astype(F32)
    no_row = float(LOGIT_ROWS)
    is_group = row < N_GROUPS
    m_g = jnp.max(jnp.where(is_group, lg, MASK_NEG), axis=0, keepdims=True)
    g_idx = jnp.min(jnp.where(is_group & (lg == m_g), row_f, no_row), axis=0, keepdims=True)
    p_group = 1.0 / jnp.sum(jnp.where(is_group, jnp.exp(lg - m_g), 0.0), axis=0, keepdims=True)
    e_row = row - N_GROUPS
    in_group = (e_row >= 0) & (e_row < N_EXPERTS) & ((e_row >> 2).astype(F32) == g_idx)
    m_1 = jnp.max(jnp.where(in_group, lg, MASK_NEG), axis=0, keepdims=True)
    i_1 = jnp.min(jnp.where(in_group & (lg == m_1), row_f, no_row), axis=0, keepdims=True)
    rest = in_group & (row_f != i_1)
    m_2 = jnp.max(jnp.where(rest, lg, MASK_NEG), axis=0, keepdims=True)
    i_2 = jnp.min(jnp.where(rest & (lg == m_2), row_f, no_row), axis=0, keepdims=True)
    ratio = jnp.exp(m_2 - m_1)
    w_1 = p_group / (1.0 + ratio)
    out_row = lax.broadcasted_iota(jnp.int32, (ROUTE_ROWS, 1), 0)
    return jnp.where(out_row == 0, i_1 - N_GROUPS,
                     jnp.where(out_row == 1, i_2 - N_GROUPS,
                               jnp.where(out_row == 2, w_1, jnp.where(out_row == 3, w_1 * ratio, 0.0))))


def _outproj_body(*refs, seg_sizes):
    n_seg = len(seg_sizes)
    x_refs, oa_refs, ob_refs = refs[:n_seg], refs[n_seg:2 * n_seg], refs[2 * n_seg:3 * n_seg]
    wa_ref, wb_ref, g_ref, wr_ref, br_ref, x1_ref, h2_ref, rt_ref = refs[3 * n_seg:]

    def run(x_ref, oa_ref, ob_ref):
        x1 = x_ref[...] + _dot(oa_ref[...].astype(BF16), wa_ref[...]) + _dot(ob_ref[...].astype(BF16), wb_ref[...])
        x1_ref[...] = x1
        h2 = x1 * lax.rsqrt(jnp.mean(x1 * x1, axis=-1, keepdims=True) + RMS_EPS) * g_ref[...]
        h2b = h2.astype(BF16)
        h2_ref[...] = h2b
        rt_ref[...] = _route_cols(_dot_nt(wr_ref[...], h2b) + br_ref[...])

    _run_segment(pl.program_id(0), seg_sizes, run, list(zip(x_refs, oa_refs, ob_refs)))


def _outproj_call(x_list, x_firsts, oa_list, ob_list, seg_sizes, wa, wb, g, wr, br, *, tm):
    d = x_list[0].shape[1]
    n_tiles = sum(seg_sizes)
    n = n_tiles * tm
    row = lambda i: (i, 0)
    fixed = lambda i: (0, 0)
    zeros = [0] * len(seg_sizes)
    return pl.pallas_call(
        functools.partial(_outproj_body, seg_sizes=tuple(seg_sizes)),
        grid=(n_tiles,),
        in_specs=(_seg_specs(tm, d, seg_sizes, x_firsts) + _seg_specs(tm, A_WIDTH, seg_sizes, zeros)
                  + _seg_specs(tm, B_WIDTH, seg_sizes, zeros)
                  + [pl.BlockSpec((A_WIDTH, d), fixed), pl.BlockSpec((B_WIDTH, d), fixed), pl.BlockSpec((1, d), fixed),
                     pl.BlockSpec((LOGIT_ROWS, d), fixed), pl.BlockSpec((LOGIT_ROWS, 1), fixed)]),
        out_specs=[pl.BlockSpec((tm, d), row), pl.BlockSpec((tm, d), row), pl.BlockSpec((ROUTE_ROWS, tm), row)],
        out_shape=[jax.ShapeDtypeStruct((n, d), F32), jax.ShapeDtypeStruct((n, d), BF16),
                   jax.ShapeDtypeStruct((n_tiles * ROUTE_ROWS, tm), F32)],
        compiler_params=pltpu.CompilerParams(dimension_semantics=("arbitrary",), vmem_limit_bytes=VMEM_LIMIT),
        name="outproj",
    )(*x_list, *oa_list, *ob_list, wa, wb, g, wr, br)


def _expert_body(vb_ref, ve_ref, lo_ref, hi_ref, x_ref, wg_ref, wu_ref, wd_ref, y_ref, wg_bf, wu_bf, wd_bf):
    v = pl.program_id(0)
    pv = jnp.maximum(v - 1, 0)
    lo, hi = lo_ref[v], hi_ref[v]
    first_visit = (v == 0) | (vb_ref[v] != vb_ref[pv])

    @pl.when((v == 0) | (ve_ref[v] != ve_ref[pv]))
    def _():
        wg_bf[...] = wg_ref[0].astype(BF16)
        wu_bf[...] = wu_ref[0].astype(BF16)
        wd_bf[...] = wd_ref[0].astype(BF16)

    @pl.when(hi > lo)
    def _():
        x = x_ref[...]
        gate = _dot(x, wg_bf[...])
        up = _dot(x, wu_bf[...])
        mid = (gate * _sigmoid(gate) * up).astype(BF16)
        y = _dot(mid, wd_bf[...])
        row = lax.broadcasted_iota(jnp.int32, (y.shape[0], 1), 0)
        mine = (row >= lo) & (row < hi)

        @pl.when(first_visit)
        def _():
            y_ref[...] = jnp.where(mine, y, 0.0)

        @pl.when(jnp.logical_not(first_visit))
        def _():
            y_ref[...] = jnp.where(mine, y, y_ref[...])


def _expert_call(visits, xs, wg, wu, wd, *, tb):
    n_rows, d = xs.shape
    ff = wg.shape[-1]
    vb, ve, lo, hi = visits
    blk = lambda v, vb, ve, lo, hi: (vb[v], 0)
    wsel = lambda v, vb, ve, lo, hi: (ve[v], 0, 0)
    grid_spec = pltpu.PrefetchScalarGridSpec(
        num_scalar_prefetch=4,
        grid=(vb.shape[0],),
        in_specs=[pl.BlockSpec((tb, d), blk), pl.BlockSpec((1, d, ff), wsel), pl.BlockSpec((1, d, ff), wsel),
                  pl.BlockSpec((1, ff, d), wsel)],
        out_specs=pl.BlockSpec((tb, d), blk),
        scratch_shapes=[pltpu.VMEM((d, ff), BF16), pltpu.VMEM((d, ff), BF16), pltpu.VMEM((ff, d), BF16)],
    )
    return pl.pallas_call(
        _expert_body,
        grid_spec=grid_spec,
        out_shape=jax.ShapeDtypeStruct((n_rows, d), F32),
        compiler_params=pltpu.CompilerParams(dimension_semantics=("arbitrary",), vmem_limit_bytes=VMEM_LIMIT),
        name="experts",
    )(vb, ve, lo, hi, xs, wg, wu, wd)


def _final_body(*refs, seg_sizes):
    n_seg = len(seg_sizes)
    g_ref, o_ref = refs[4 * n_seg:]

    def run(x1_ref, y0_ref, y1_ref, rt_ref):
        tm = rt_ref.shape[1]
        rt = jnp.concatenate([rt_ref[...], jnp.zeros((LANE - ROUTE_ROWS, tm), F32)], axis=0).T
        x = x1_ref[...] + (rt[:, 2:3] * y0_ref[...] + rt[:, 3:4] * y1_ref[...])
        o_ref[...] = x * lax.rsqrt(jnp.mean(x * x, axis=-1, keepdims=True) + RMS_EPS) * g_ref[...]

    _run_segment(pl.program_id(0), seg_sizes, run, [refs[4 * s:4 * s + 4] for s in range(n_seg)])


def _final_call(sources, firsts, seg_sizes, g, *, tm):
    d = sources[0][0].shape[1]
    n_tiles = sum(seg_sizes)
    in_specs, start = [], 0
    for size, first in zip(seg_sizes, firsts):
        src = lambda i, s=start, f=first, n=size: (f + jnp.clip(i - s, 0, n - 1), 0)
        in_specs += [pl.BlockSpec((tm, d), src)] * 3 + [pl.BlockSpec((ROUTE_ROWS, tm), src)]
        start += size
    return pl.pallas_call(
        functools.partial(_final_body, seg_sizes=tuple(seg_sizes)),
        grid=(n_tiles,),
        in_specs=in_specs + [pl.BlockSpec((1, d), lambda i: (0, 0))],
        out_specs=pl.BlockSpec((tm, d), lambda i: (i, 0)),
        out_shape=jax.ShapeDtypeStruct((n_tiles * tm, d), F32),
        compiler_params=pltpu.CompilerParams(dimension_semantics=("arbitrary",), vmem_limit_bytes=VMEM_LIMIT),
        name="final",
    )(*[a for src in sources for a in src], g)


def _dispatch(flat_e, n, tb):
    n_assign = n * TOP_K_INNER
    idx_bits = (n_assign - 1).bit_length()
    assert N_EXPERTS << idx_bits < 2 ** 31 and n_assign % tb == 0
    ids = jnp.arange(n_assign, dtype=jnp.int32)
    sorted_key = jnp.sort((flat_e << idx_bits) | ids)
    sorted_id = sorted_key & ((1 << idx_bits) - 1)
    src_tok = sorted_id - n * (sorted_id // n)

    experts = jnp.arange(N_EXPERTS, dtype=jnp.int32)
    onehot = (flat_e[:, None] == experts[None, :]).astype(jnp.int32)
    csum = jnp.cumsum(onehot, axis=0)
    counts = csum[-1]
    ends = jnp.cumsum(counts)
    starts = ends - counts
    dest = (jnp.sum((csum - 1 + starts[None, :]) * onehot, axis=1)).astype(jnp.int32)

    n_blocks = n_assign // tb
    n_visits = n_blocks + N_EXPERTS - 1
    first_blk = starts // tb
    n_vis_e = jnp.where(counts > 0, (ends + tb - 1) // tb - first_blk, 0)
    v_end = jnp.cumsum(n_vis_e)
    v = jnp.arange(n_visits, dtype=jnp.int32)
    valid = v < v_end[-1]
    last_e = jnp.max(jnp.where(counts > 0, experts, 0))
    ve = jnp.where(valid, jnp.sum((v_end[None, :] <= v[:, None]).astype(jnp.int32), axis=1), last_e)
    pick = lambda a: jnp.sum(jnp.where(ve[:, None] == experts[None, :], a[None, :], 0), axis=1)
    vb = pick(first_blk) + v - pick(v_end - n_vis_e)
    lo = jnp.maximum(pick(starts), vb * tb) - vb * tb
    hi = jnp.minimum(pick(ends), (vb + 1) * tb) - vb * tb
    vb = jnp.where(valid, vb, n_blocks - 1)
    lo = jnp.where(valid, lo, 0)
    hi = jnp.where(valid, hi, 0)
    return dest, src_tok, (vb.astype(jnp.int32), ve.astype(jnp.int32), lo.astype(jnp.int32), hi.astype(jnp.int32))


def _q_perm():
    cols = []
    for g in range(B_GROUP):
        for kv in range(B_KV_HEADS):
            h = kv * B_GROUP + g
            cols.extend(range(h * HEAD_DIM, (h + 1) * HEAD_DIM))
    return jnp.array(cols, dtype=jnp.int32)


def _layer(x_prompt, x_sample, state_rwkv, state_shift, cache_win_k, cache_win_v, prm, norm_final_g, *,
           chunk, n_sub, moe_block):
    bp, tp, d = x_prompt.shape
    bs, ts, _ = x_sample.shape
    n_p, n_s = bp * tp, bs * ts
    ts_pad = SUBLANE
    tm = n_s
    assert tp % tm == 0 and tm % 8 == 0 and cache_win_k.shape[1] == WINDOW
    xp = x_prompt.reshape(n_p, d)
    xs = x_sample.reshape(n_s, d)

    qp = _q_perm()
    w_in = prm["w_in"]
    w_in = jnp.concatenate([w_in[:, :SHIFT_W], w_in[:, SHIFT_W:SHIFT_W + B_WIDTH][:, qp],
                            w_in[:, SHIFT_W + B_WIDTH:]], axis=1).astype(BF16)
    gn = prm["attn_norm_g"][qp][None]
    w_out = prm["w_out"]
    wa, wb = w_out[:A_WIDTH].astype(BF16), w_out[A_WIDTH:][qp].astype(BF16)
    pad_rows_r = jnp.zeros((LOGIT_ROWS - N_GROUPS - N_EXPERTS, d), F32)
    wr = jnp.concatenate([prm["w_route_group"].T, prm["w_route_expert"].T, pad_rows_r], axis=0).astype(BF16)
    br = jnp.concatenate([prm["b_route_group"], prm["b_route_expert"], pad_rows_r[:, 0]])[:, None]

    zero_blk = jnp.zeros((LORA_W // 2, A_WIDTH), F32)
    wlora = jnp.concatenate([jnp.concatenate([prm["w_decay_up"], zero_blk], axis=1),
                             jnp.concatenate([zero_blk, prm["w_iclr_up"]], axis=1)], axis=0).astype(BF16)
    rp = dict(mu=prm["mu_shift"][None], w0=prm["w_decay0"][None], a0=prm["w_iclr0"][None], wlora=wlora,
              wgate=prm["w_gate_up"].astype(BF16), kk=prm["k_k"][None], ka=prm["k_a"][None],
              rk=prm["r_k"].reshape(1, A_WIDTH), lng=prm["lnx_g"][None], lnb=prm["lnx_b"][None])

    groups = [(0, bp)]
    tiles_per_seq = tp // tm
    nb = tp // WINDOW
    n_qb = _row_tile(nb, SWA_BLOCKS_PER_STEP)
    kv4 = lambda a: a.reshape(a.shape[0], a.shape[1], B_KV_HEADS, HEAD_DIM)
    finals, s_p, sh_p, k_p, v_p = [], [], [], [], []
    for gi, (s0, s1) in enumerate(groups):
        has_sample = gi == len(groups) - 1
        n_seq = s1 - s0
        n_pt = n_seq * tiles_per_seq
        n_pg = n_pt * tm
        seg_sizes = [n_pt] + ([1] if has_sample else [])
        pa, q, k, v = _inproj_call([xp] + ([xs] if has_sample else []), [s0 * tiles_per_seq, 0], seg_sizes,
                                   prm["norm_mix_g"][None], w_in, tm=tm)

        oa_p, s_pg = _rwkv_call(pa, jnp.zeros((n_seq, 1, SHIFT_W), F32),
                                jnp.zeros((n_seq,) + state_rwkv.shape[1:], F32), rp, n_seq=n_seq,
                                n_chunks=tp // chunk, chunk=chunk, n_sub=n_sub, n_bat=1, t_real=chunk)
        ob_p, = _swa_call(prm["attn_sink"], q, k.reshape(-1, WINDOW, B_KV_WIDTH), v.reshape(-1, WINDOW, B_KV_WIDTH),
                          k, v, gn, n_seq=n_seq, n_blk=nb // n_qb, tq=WINDOW, n_bat=1, n_qb=n_qb,
                          first_has_prev=False, out_dtype=BF16,
                          prev_map=lambda b, i: (b * nb + jnp.maximum(i * n_qb - 1, 0), 0, 0))
        oa_list, ob_list = [oa_p], [ob_p]
        if has_sample:
            pad_rows = lambda a: jnp.pad(a[n_pg:].reshape(bs, ts, -1), ((0, 0), (0, ts_pad - ts), (0, 0))).reshape(
                bs * ts_pad, -1)
            oa_s, s_s = _rwkv_call(pad_rows(pa), state_shift[:, None, :], state_rwkv, rp, n_seq=bs,
                                   n_chunks=1, chunk=ts_pad, n_sub=1, n_bat=_row_tile(bs, SAMPLE_SEQS_PER_STEP), t_real=ts)
            ob_s, k_s, v_s = _swa_call(prm["attn_sink"], pad_rows(q).astype(F32),
                                       cache_win_k.reshape(bs, WINDOW, B_KV_WIDTH),
                                       cache_win_v.reshape(bs, WINDOW, B_KV_WIDTH), pad_rows(k), pad_rows(v), gn,
                                       n_seq=bs, n_blk=1, tq=ts_pad, n_bat=_row_tile(bs, SAMPLE_SEQS_PER_STEP), first_has_prev=True,
                                       prev_map=lambda b, i: (b, 0, 0), out_dtype=F32, cache_rows=ts)
            oa_list.append(oa_s.reshape(bs, ts_pad, A_WIDTH)[:, :ts].reshape(n_s, A_WIDTH))
            ob_list.append(ob_s.reshape(bs, ts_pad, B_WIDTH)[:, :ts].reshape(n_s, B_WIDTH))
            sh_s = pa[n_pg:].reshape(bs, ts, SHIFT_W)[:, -1]

        x1, h2, route = _outproj_call([xp] + ([xs] if has_sample else []), [s0 * tiles_per_seq, 0],
                                               oa_list, ob_list, seg_sizes, wa, wb, prm["norm_ffn_g"][None], wr, br,
                                               tm=tm)
        n = sum(seg_sizes) * tm
        route_t = route.reshape(n // tm, ROUTE_ROWS, tm)
        flat_e = jnp.concatenate([route_t[:, j, :].reshape(n) for j in range(TOP_K_INNER)]).astype(jnp.int32)
        dest, src_tok, visits = _dispatch(flat_e, n, moe_block)
        ybuf = _expert_call(visits, h2[src_tok], prm["w_exp_gate"], prm["w_exp_up"], prm["w_exp_down"], tb=moe_block)
        finals.append((x1, ybuf[dest[:n]], ybuf[dest[n:]], route))

        last_rows = lambda a, m: jnp.stack([a[(b + 1) * tp - m:(b + 1) * tp] for b in range(n_seq)], axis=0)
        s_p.append(s_pg)
        sh_p.append(last_rows(pa, 1)[:, 0])
        k_p.append(last_rows(k, WINDOW))
        v_p.append(last_rows(v, WINDOW))

    gf = norm_final_g[None]
    n_pts = [(s1 - s0) * tiles_per_seq for s0, s1 in groups]
    y_p = _final_call(finals, [0] * len(groups), n_pts, gf, tm=tm)
    y_s = _final_call(finals[-1:], n_pts[-1:], [1], gf, tm=tm)
    cat = lambda parts: jnp.concatenate(parts, axis=0)
    return (y_p.reshape(bp, tp, d), y_s.reshape(bs, ts, d), cat(s_p), cat(sh_p), kv4(cat(k_p)), kv4(cat(v_p)),
            s_s, sh_s, kv4(k_s), kv4(v_s))


def kernel(x_prompt, x_sample, state_rwkv, state_shift, cache_win_k, cache_win_v, norm_mix_g, w_in, mu_shift, w_decay0, w_decay_up, w_iclr0, w_iclr_up, w_gate_up, k_k, k_a, r_k, lnx_g, lnx_b, attn_sink, attn_norm_g, w_out, norm_ffn_g, w_route_group, b_route_group, w_route_expert, b_route_expert, w_exp_gate, w_exp_up, w_exp_down, norm_final_g):
    assert norm_mix_g.shape[0] == 1, "single-layer trunk"
    prm = dict(norm_mix_g=norm_mix_g[0], w_in=w_in[0], mu_shift=mu_shift[0], w_decay0=w_decay0[0],
               w_decay_up=w_decay_up[0], w_iclr0=w_iclr0[0], w_iclr_up=w_iclr_up[0], w_gate_up=w_gate_up[0],
               k_k=k_k[0], k_a=k_a[0], r_k=r_k[0], lnx_g=lnx_g[0], lnx_b=lnx_b[0], attn_sink=attn_sink[0],
               attn_norm_g=attn_norm_g[0], w_out=w_out[0], norm_ffn_g=norm_ffn_g[0],
               w_route_group=w_route_group[0], b_route_group=b_route_group[0],
               w_route_expert=w_route_expert[0], b_route_expert=b_route_expert[0],
               w_exp_gate=w_exp_gate[0], w_exp_up=w_exp_up[0], w_exp_down=w_exp_down[0])
    outs = _layer(x_prompt, x_sample, state_rwkv[0], state_shift[0], cache_win_k[0], cache_win_v[0], prm,
                  norm_final_g, chunk=RWKV_CHUNK, n_sub=RWKV_CHUNKS_PER_STEP, moe_block=MOE_BLOCK)
    y_p, y_s, s_p, sh_p, kp, vp, s_s, sh_s, ks, vs = outs
    return (y_p, y_s, s_p[None], sh_p[None], kp[None], vp[None], s_s[None], sh_s[None], ks[None], vs[None])
```

```python
import functools
import math

import jax
import jax.numpy as jnp
from jax import lax
from jax.experimental import pallas as pl
from jax.experimental.pallas import tpu as pltpu

F32 = jnp.float32
BF16 = jnp.bfloat16

HEAD_DIM = 64
A_HEADS = 8
A_WIDTH = A_HEADS * HEAD_DIM
B_HEADS = 8
B_KV_HEADS = 2
B_GROUP = B_HEADS // B_KV_HEADS
B_WIDTH = B_HEADS * HEAD_DIM
B_KV_WIDTH = B_KV_HEADS * HEAD_DIM
DECAY_LORA = 64
ICLR_LORA = 64
GATE_LORA = 128
LORA_W = DECAY_LORA + ICLR_LORA
SHIFT_W = 3 * A_WIDTH + LORA_W + GATE_LORA
IN_W = SHIFT_W + B_WIDTH + 2 * B_KV_WIDTH
WINDOW = 128
N_GROUPS = 4
EXPERTS_PER_GROUP = 4
N_EXPERTS = N_GROUPS * EXPERTS_PER_GROUP
TOP_K_INNER = 2
EXPERT_FF = 512
RMS_EPS = 1e-6
LNX_EPS = 64e-5
DECAY_OFFSET = 0.5

LANE = 128
PAIR = LANE // HEAD_DIM
N_PAIRS = A_HEADS // PAIR
ROUTE_ROWS = 8
LOGIT_ROWS = 32
MASK_NEG = -1e30
VMEM_LIMIT = 48 * 1024 * 1024
SUBLANE = 8

RWKV_CHUNK = 64
RWKV_CHUNKS_PER_STEP = 8
SWA_BLOCKS_PER_STEP = 8
SAMPLE_SEQS_PER_STEP = 16
MOE_BLOCK = 512

LOG2_E = math.log2(math.e)
DECAY_SCALE = math.exp(-DECAY_OFFSET)
HEAD_SHIFT = HEAD_DIM.bit_length() - 1
HI = lax.Precision.HIGHEST


def _row_tile(n, cap):
    t = cap
    while n % t:
        t //= 2
    return t


def _dot(a, b, precision=None):
    return jnp.dot(a, b, preferred_element_type=F32, precision=precision)


def _dot_nt(a, b, precision=None):
    return lax.dot_general(a, b, (((1,), (1,)), ((), ())), preferred_element_type=F32, precision=precision)


def _dot_tn(a, b, precision=None):
    return lax.dot_general(a, b, (((0,), (0,)), ((), ())), preferred_element_type=F32, precision=precision)


def _sigmoid(x):
    return 1.0 / (1.0 + jnp.exp(-x))


def _hi_lo(x, axis):
    hi = x.astype(BF16)
    lo = (x - hi.astype(F32)).astype(BF16)
    return jnp.concatenate([hi, lo], axis=axis)


def _run_segment(i, seg_sizes, run, seg_refs):
    start = 0
    for size, refs in zip(seg_sizes, seg_refs):
        pl.when((i >= start) & (i < start + size))(functools.partial(run, *refs))
        start += size


def _seg_specs(tm, width, seg_sizes, firsts):
    specs, start = [], 0
    for size, first in zip(seg_sizes, firsts):
        specs.append(pl.BlockSpec((tm, width),
                                  lambda i, s=start, f=first, n=size: (f + jnp.clip(i - s, 0, n - 1), 0)))
        start += size
    return specs


def _inproj_body(*refs, seg_sizes):
    n_seg = len(seg_sizes)
    g_ref, w_ref, pa_ref, q_ref, k_ref, v_ref = refs[n_seg:]

    def run(x_ref):
        x = x_ref[...]
        h = x * lax.rsqrt(jnp.mean(x * x, axis=-1, keepdims=True) + RMS_EPS) * g_ref[...]
        p = _dot(h.astype(BF16), w_ref[...])
        pa_ref[...] = p[:, :SHIFT_W]
        q_ref[...] = (p[:, SHIFT_W:SHIFT_W + B_WIDTH] * (HEAD_DIM ** -0.5)).astype(BF16)
        k_ref[...] = p[:, SHIFT_W + B_WIDTH:SHIFT_W + B_WIDTH + B_KV_WIDTH]
        v_ref[...] = p[:, SHIFT_W + B_WIDTH + B_KV_WIDTH:]

    _run_segment(pl.program_id(0), seg_sizes, run, [(r,) for r in refs[:n_seg]])


def _inproj_call(xs_list, firsts, seg_sizes, g, w_bf16, *, tm):
    d = xs_list[0].shape[1]
    n_tiles = sum(seg_sizes)
    n = n_tiles * tm
    row = lambda i: (i, 0)
    fixed = lambda i: (0, 0)
    return pl.pallas_call(
        functools.partial(_inproj_body, seg_sizes=tuple(seg_sizes)),
        grid=(n_tiles,),
        in_specs=_seg_specs(tm, d, seg_sizes, firsts) + [pl.BlockSpec((1, d), fixed), pl.BlockSpec((d, IN_W), fixed)],
        out_specs=[pl.BlockSpec((tm, SHIFT_W), row), pl.BlockSpec((tm, B_WIDTH), row),
                   pl.BlockSpec((tm, B_KV_WIDTH), row), pl.BlockSpec((tm, B_KV_WIDTH), row)],
        out_shape=[jax.ShapeDtypeStruct((n, SHIFT_W), F32), jax.ShapeDtypeStruct((n, B_WIDTH), BF16),
                   jax.ShapeDtypeStruct((n, B_KV_WIDTH), F32), jax.ShapeDtypeStruct((n, B_KV_WIDTH), F32)],
        compiler_params=pltpu.CompilerParams(dimension_semantics=("arbitrary",), vmem_limit_bytes=VMEM_LIMIT),
        name="inproj",
    )(*xs_list, g, w_bf16)


def _rwkv_body(pa_ref, shift0_ref, s0_ref, mu_ref, w0_ref, a0_ref, wlora_ref, wgate_ref, kk_ref, ka_ref,
               rk_ref, lng_ref, lnb_ref, oa_ref, sout_ref, s_scr, prev_scr, *, chunk, n_sub, n_bat, t_real):
    C = chunk
    seq_rows = n_sub * C
    n_seg = n_bat * n_sub
    rows = n_seg * C
    c = pl.program_id(1)

    @pl.when(c == 0)
    def _():
        prev_scr[...] = shift0_ref[...]
        for b in range(n_bat):
            for j in range(N_PAIRS):
                s_scr[b * N_PAIRS + j] = jnp.concatenate([s0_ref[b, PAIR * j + h] for h in range(PAIR)], axis=1)

    pa = pa_ref[...]
    row = lax.broadcasted_iota(jnp.int32, (rows, 1), 0)
    row_in_chunk = row & (C - 1)
    pa_prev = pltpu.roll(pa, 1, axis=0)
    for b in range(n_bat):
        pa_prev = jnp.where(row == b * seq_rows, prev_scr[b], pa_prev)
        last = (b + 1) * seq_rows - C + t_real - 1
        prev_scr[b] = pa[last:last + 1]
    xm = pa + mu_ref[...] * (pa_prev - pa)

    r = xm[:, :A_WIDTH]
    k = xm[:, A_WIDTH:2 * A_WIDTH]
    v = xm[:, 2 * A_WIDTH:3 * A_WIDTH]
    lora_in = xm[:, 3 * A_WIDTH:3 * A_WIDTH + LORA_W]
    gd = xm[:, 3 * A_WIDTH + LORA_W:]

    lane = lax.broadcasted_iota(jnp.int32, (1, LANE), 1)
    lo_half = lane < HEAD_DIM
    z = jnp.where(lo_half, jnp.tanh(lora_in), lora_in)
    lw = _dot(z.astype(BF16), wlora_ref[...])
    dec_pre = w0_ref[...] + lw[:, :A_WIDTH]
    a = _sigmoid(a0_ref[...] + lw[:, A_WIDTH:])
    logdec = (-LOG2_E * DECAY_SCALE) * _sigmoid(dec_pre)
    g = _dot(_sigmoid(gd).astype(BF16), wgate_ref[...])

    ri = lax.broadcasted_iota(jnp.int32, (LANE, LANE), 0)
    ci = lax.broadcasted_iota(jnp.int32, (LANE, LANE), 1)
    same_head = (ri >> HEAD_SHIFT) == (ci >> HEAD_SHIFT)
    seg = jnp.where(same_head, 1.0, 0.0).astype(BF16)
    seg2 = jnp.concatenate([seg, seg], axis=0)

    def headsum(x):
        return jnp.concatenate([_dot(_hi_lo(x[:, j * LANE:(j + 1) * LANE], 1), seg2) for j in range(N_PAIRS)],
                               axis=1)

    kk = k * kk_ref[...]
    kk = kk * lax.rsqrt(jnp.maximum(headsum(kk * kk), 1e-24))
    k = k * (1.0 + (a - 1.0) * ka_ref[...])
    bonus = headsum(r * k * rk_ref[...]) * v

    if t_real < C:
        valid = row_in_chunk < t_real
        logdec = jnp.where(valid, logdec, 0.0)
        kk = jnp.where(valid, kk, 0.0)
        k = jnp.where(valid, k, 0.0)
        v = jnp.where(valid, v, 0.0)

    cum = logdec
    shift = 1
    while shift < C:
        cum = cum + jnp.where(row_in_chunk >= shift, pltpu.roll(cum, shift, axis=0), 0.0)
        shift *= 2
    ends = [cum[(s + 1) * C - 1:(s + 1) * C] for s in range(n_seg)]
    cum_end = jnp.concatenate([jnp.broadcast_to(e, (C, A_WIDTH)) for e in ends], axis=0) if n_seg > 1 else ends[0]
    w_incl = jnp.exp2(cum)
    w_prev = jnp.exp2(cum - logdec)
    w_inv = jnp.exp2(-cum)
    w_end = jnp.exp2(cum_end - cum)
    w_chunk = [jnp.exp2(e) for e in ends]
    kka = kk * a
    terms = dict(A=-kk * w_prev, R=r * w_incl, B=kka * w_inv, K=k * w_inv, V=v, Be=kka * w_end, Ke=k * w_end)

    C2 = PAIR * C
    t_idx = lax.broadcasted_iota(jnp.int32, (C, 1), 0)
    i_idx = lax.broadcasted_iota(jnp.int32, (1, C2), 1) & (C - 1)
    strict = i_idx < t_idx
    incl = i_idx <= t_idx
    ident = jnp.where(i_idx == t_idx, 1.0, 0.0).astype(F32)
    lo_time = lax.broadcasted_iota(jnp.int32, (1, C2), 1) < C
    n_levels = max(1, (C - 1).bit_length())
    bf = lambda x: x.astype(BF16)

    def bd(x, lo_mask=lo_half):
        return bf(jnp.concatenate([jnp.where(lo_mask, x, 0.0), jnp.where(lo_mask, 0.0, x)], axis=0))

    chains = [(s, j) for s in range(n_seg) for j in range(N_PAIRS)]
    tile = lambda name, s, j: terms[name][s * C:(s + 1) * C, j * LANE:(j + 1) * LANE]
    r_sbs = {ch: tile("R", *ch) for ch in chains}
    stk = {ch: {nm: bd(tile(nm, *ch)) for nm in ("A", "B", "K", "V")} for ch in chains}
    m_ab, aak, arb, ark = {}, {}, {}, {}
    for ch in chains:
        t = stk[ch]
        a_sbs, rb = bf(tile("A", *ch)), bf(r_sbs[ch])
        if C2 % LANE == 0:
            m1 = _dot_nt(jnp.concatenate([a_sbs, rb], axis=0), jnp.concatenate([t["B"], t["K"]], axis=0))
            ab, ak, rbm, rk = m1[:C, :C2], m1[:C, C2:], m1[C:, :C2], m1[C:, C2:]
        else:
            ab, ak = _dot_nt(a_sbs, t["B"]), _dot_nt(a_sbs, t["K"])
            rbm, rk = _dot_nt(rb, t["B"]), _dot_nt(rb, t["K"])
        m_ab[ch] = jnp.where(strict, ab, 0.0)
        aak[ch] = jnp.where(strict, ak, 0.0)
        arb[ch] = bf(jnp.where(incl, rbm, 0.0))
        ark[ch] = jnp.where(incl, rk, 0.0)

    nn = dict(m_ab)
    tinv = {ch: ident + m_ab[ch] for ch in chains}
    for lvl in range(1, n_levels):
        if lvl == 1:
            for ch in chains:
                nn[ch] = _dot(bf(nn[ch]), bd(nn[ch], lo_time))
            continue
        for ch in chains:
            both = _dot(bf(jnp.concatenate([nn[ch], tinv[ch]], axis=0)), bd(nn[ch], lo_time))
            nn[ch] = both[:C]
            tinv[ch] = tinv[ch] + both[C:]
    if n_levels > 1:
        for ch in chains:
            tinv[ch] = tinv[ch] + _dot(bf(tinv[ch]), bd(nn[ch], lo_time))
    av = {ch: _dot(bf(jnp.concatenate([aak[ch], ark[ch]], axis=0)), stk[ch]["V"]) for ch in chains}
    x = {ch: _dot(bf(tinv[ch]), jnp.concatenate([stk[ch]["A"], bd(av[ch][:C])], axis=1)) for ch in chains}
    a_eff = {ch: bd(x[ch][:, :LANE]) for ch in chains}
    v_eff = {ch: bd(x[ch][:, LANE:]) for ch in chains}
    zed = {ch: _dot(arb[ch], jnp.concatenate([a_eff[ch], v_eff[ch]], axis=1)) for ch in chains}
    rhat = {ch: bf(r_sbs[ch] + zed[ch][:, :LANE]) for ch in chains}
    y0 = {ch: av[ch][C:] + zed[ch][:, LANE:] for ch in chains}
    p_mat = {ch: bf(jnp.where(same_head, _dot_tn(bf(tile("Be", *ch)), bf(x[ch][:, :LANE])), 0.0))
             for ch in chains}
    q_full = {ch: _dot_tn(bf(jnp.concatenate([x[ch][:, LANE:], tile("V", *ch)], axis=0)),
                          bf(jnp.concatenate([tile("Be", *ch), tile("Ke", *ch)], axis=0))) for ch in chains}

    y_rows = []
    for b in range(n_bat):
        state = [s_scr[b * N_PAIRS + j] for j in range(N_PAIRS)]
        for sub in range(n_sub):
            s = b * n_sub + sub
            ys = []
            for j in range(N_PAIRS):
                ch = (s, j)
                hi = bf(state[j])
                lo = bf(state[j] - hi.astype(F32))
                yy = _dot_nt(rhat[ch], jnp.concatenate([bd(hi), bd(lo)], axis=0))
                ys.append(yy[:, :LANE] + yy[:, LANE:] + y0[ch])
                sp = _dot_nt(jnp.concatenate([hi, lo], axis=0), p_mat[ch])
                q_sbs = jnp.where(lo_half, q_full[ch][:HEAD_DIM], q_full[ch][HEAD_DIM:])
                state[j] = state[j] * w_chunk[s][:, j * LANE:(j + 1) * LANE] + sp[:HEAD_DIM] + sp[HEAD_DIM:] + q_sbs
            y_rows.append(jnp.concatenate(ys, axis=1))
        for j in range(N_PAIRS):
            s_scr[b * N_PAIRS + j] = state[j]

    y = jnp.concatenate(y_rows, axis=0) if n_seg > 1 else y_rows[0]
    inv_n = 1.0 / HEAD_DIM
    mean = headsum(y) * inv_n
    d = y - mean
    var = headsum(d * d) * inv_n
    yn = d * lax.rsqrt(var + LNX_EPS) * lng_ref[...] + lnb_ref[...]
    oa_ref[...] = ((yn + bonus) * g).astype(oa_ref.dtype)

    @pl.when(c == pl.num_programs(1) - 1)
    def _():
        for b in range(n_bat):
            for j in range(N_PAIRS):
                for h in range(PAIR):
                    sout_ref[b, PAIR * j + h] = s_scr[b * N_PAIRS + j][:, h * HEAD_DIM:(h + 1) * HEAD_DIM]


def _rwkv_call(pa2d, shift0, s0_pairs, prm, *, n_seq, n_chunks, chunk, n_sub, n_bat, t_real):
    n_steps = n_chunks // n_sub
    assert n_seq % n_bat == 0 and (n_bat == 1 or n_steps == 1)
    row = lambda b, c: (b * n_steps + c, 0)
    seq3 = lambda b, c: (b, 0, 0)
    fixed = lambda b, c: (0, 0)
    vec = lambda w: pl.BlockSpec((1, w), fixed)
    body = functools.partial(_rwkv_body, chunk=chunk, n_sub=n_sub, n_bat=n_bat, t_real=t_real)
    n_rows = n_seq * n_chunks * chunk
    blk_rows = n_bat * n_sub * chunk
    state_blk = (n_bat, A_HEADS, HEAD_DIM, HEAD_DIM)
    seq4 = lambda b, c: (b, 0, 0, 0)
    return pl.pallas_call(
        body,
        grid=(n_seq // n_bat, n_steps),
        in_specs=[pl.BlockSpec((blk_rows, SHIFT_W), row),
                  pl.BlockSpec((n_bat, 1, SHIFT_W), seq3),
                  pl.BlockSpec(state_blk, seq4),
                  vec(SHIFT_W), vec(A_WIDTH), vec(A_WIDTH),
                  pl.BlockSpec((LORA_W, 2 * A_WIDTH), fixed), pl.BlockSpec((GATE_LORA, A_WIDTH), fixed),
                  vec(A_WIDTH), vec(A_WIDTH), vec(A_WIDTH), vec(A_WIDTH), vec(A_WIDTH)],
        out_specs=[pl.BlockSpec((blk_rows, A_WIDTH), row), pl.BlockSpec(state_blk, seq4)],
        out_shape=[jax.ShapeDtypeStruct((n_rows, A_WIDTH), BF16),
                   jax.ShapeDtypeStruct((n_seq, A_HEADS, HEAD_DIM, HEAD_DIM), F32)],
        scratch_shapes=[pltpu.VMEM((n_bat * N_PAIRS, HEAD_DIM, LANE), F32), pltpu.VMEM((n_bat, 1, SHIFT_W), F32)],
        compiler_params=pltpu.CompilerParams(dimension_semantics=("arbitrary", "arbitrary"),
                                             vmem_limit_bytes=VMEM_LIMIT),
        name="rwkv",
    )(pa2d, shift0, s0_pairs, prm["mu"], prm["w0"], prm["a0"], prm["wlora"], prm["wgate"], prm["kk"], prm["ka"],
      prm["rk"], prm["lng"], prm["lnb"])


def _swa_bias(tq):
    rows = B_GROUP * tq
    grp = jnp.arange(rows, dtype=jnp.int32)[:, None] // tq
    t = jnp.arange(rows, dtype=jnp.int32)[:, None] % tq
    dist_p = t + WINDOW - jnp.arange(WINDOW, dtype=jnp.int32)[None, :]
    dist_c = t - jnp.arange(tq, dtype=jnp.int32)[None, :]

    def bias(dist, kv):
        slope = sum(jnp.where(grp == g, 2.0 ** -(kv * B_GROUP + g + 1), 0.0) for g in range(B_GROUP))
        return jnp.where((dist >= 0) & (dist < WINDOW), -slope * dist.astype(F32), MASK_NEG)

    return (jnp.stack([bias(dist_p, kv) for kv in range(B_KV_HEADS)]),
            jnp.stack([bias(dist_c, kv) for kv in range(B_KV_HEADS)]))


def _swa_body(sink_ref, q_ref, kp_ref, vp_ref, kc_ref, vc_ref, bp_ref, bc_ref, gn_ref, o_ref, *cache_refs, tq, n_bat,
              n_qb, t_real, first_has_prev):
    nblk = pl.program_id(1)
    lane = lax.broadcasted_iota(jnp.int32, (1, LANE), 1)
    kv_masks = [(lane < HEAD_DIM) if kv == 0 else (lane >= HEAD_DIM) for kv in range(B_KV_HEADS)]
    blocks = []
    for s in range(n_bat * n_qb):
        rs = slice(s * tq, (s + 1) * tq)
        kc, vc = kc_ref[rs], vc_ref[rs]
        if n_qb > 1 and s > 0:
            ps = slice((s - 1) * tq, s * tq)
            kp, vp, has_prev = kc_ref[ps], vc_ref[ps], True
        else:
            kp, vp, has_prev = kp_ref[s], vp_ref[s], first_has_prev
        if cache_refs:
            row = lax.broadcasted_iota(jnp.int32, (WINDOW, 1), 0)
            for out_ref, old, new in zip(cache_refs, (kp, vp), (kc, vc)):
                new_tail = jnp.concatenate([pltpu.roll(new, tq - t_real, axis=0)] * (WINDOW // tq), axis=0)
                out_ref[s] = jnp.where(row >= WINDOW - t_real, new_tail, pltpu.roll(old, WINDOW - t_real, axis=0))
        pen = 0.0 if has_prev else jnp.where(nblk > 0, 0.0, MASK_NEG)
        blocks.append(dict(rs=rs, q=q_ref[rs], kp=kp.astype(BF16), kc=kc.astype(BF16), vp=vp, vc=vc, pen=pen))

    keys = [(b, kv) for b in range(len(blocks)) for kv in range(B_KV_HEADS)]
    s_p, s_c, vpm, vcm = {}, {}, {}, {}
    for b, kv in keys:
        blk, mk = blocks[b], kv_masks[kv]
        q_st = jnp.concatenate([jnp.where(mk, blk["q"][:, g * LANE:(g + 1) * LANE], 0.0) for g in range(B_GROUP)],
                               axis=0).astype(BF16)
        s_p[b, kv] = _dot_nt(q_st, blk["kp"])
        s_c[b, kv] = _dot_nt(q_st, blk["kc"])
        vpm[b, kv] = jnp.where(mk, blk["vp"], 0.0).astype(BF16)
        vcm[b, kv] = jnp.where(mk, blk["vc"], 0.0).astype(BF16)

    slabs = [(b, kv, g) for b, kv in keys for g in range(B_GROUP)]
    sink = {(kv, g): sink_ref[kv * B_GROUP + g] for kv in range(B_KV_HEADS) for g in range(B_GROUP)}
    sp, sc, m = {}, {}, {}
    for b, kv, g in slabs:
        rs = slice(g * tq, (g + 1) * tq)
        sp[b, kv, g] = s_p[b, kv][rs] + (bp_ref[kv, rs, :] + blocks[b]["pen"])
        sc[b, kv, g] = s_c[b, kv][rs] + bc_ref[kv, rs, :]
        if tq == WINDOW:
            row_max = jnp.max(jnp.maximum(sp[b, kv, g], sc[b, kv, g]), axis=-1, keepdims=True)
        else:
            row_max = jnp.maximum(jnp.max(sp[b, kv, g], axis=-1, keepdims=True),
                                  jnp.max(sc[b, kv, g], axis=-1, keepdims=True))
        m[b, kv, g] = jnp.maximum(row_max, sink[kv, g])
    e_p, e_c, e_sum = {}, {}, {}
    for key in slabs:
        ep, ec = jnp.exp(sp[key] - m[key]), jnp.exp(sc[key] - m[key])
        if tq == WINDOW:
            e_sum[key] = jnp.sum(ep + ec, axis=-1, keepdims=True)
        else:
            e_sum[key] = jnp.sum(ep, axis=-1, keepdims=True) + jnp.sum(ec, axis=-1, keepdims=True)
        e_p[key], e_c[key] = ep.astype(BF16), ec.astype(BF16)

    tiles = {}
    for b, kv, g in slabs:
        inv = 1.0 / (e_sum[b, kv, g] + jnp.exp(sink[kv, g] - m[b, kv, g]))
        o = (_dot(e_p[b, kv, g], vpm[b, kv]) + _dot(e_c[b, kv, g], vcm[b, kv])) * inv
        tiles[b, g] = o if kv == 0 else tiles[b, g] + o

    for b, blk in enumerate(blocks):
        ssq = sum(jnp.sum(tiles[b, g] * tiles[b, g], axis=-1, keepdims=True) for g in range(B_GROUP))
        inv_rms = lax.rsqrt(ssq * (1.0 / B_WIDTH) + RMS_EPS)
        out = jnp.concatenate([tiles[b, g] for g in range(B_GROUP)], axis=1) * inv_rms * gn_ref[...]
        o_ref[blk["rs"]] = out.astype(o_ref.dtype)


def _swa_call(sink, q2d, kprev, vprev, kcur2d, vcur2d, gn, *, n_seq, n_blk, tq, n_bat, first_has_prev, prev_map,
              out_dtype, n_qb=1, cache_rows=0):
    assert n_seq % n_bat == 0 and (n_bat == 1 or n_blk * n_qb == 1) and (cache_rows == 0 or n_blk * n_qb == 1)
    row = lambda b, n: (b * n_blk + n, 0)
    body = functools.partial(_swa_body, tq=tq, n_bat=n_bat, n_qb=n_qb, t_real=cache_rows,
                             first_has_prev=first_has_prev)
    bias_p, bias_c = _swa_bias(tq)
    whole = lambda b, n: (0, 0, 0)
    blk_rows = n_bat * n_qb * tq
    out_specs = [pl.BlockSpec((blk_rows, B_WIDTH), row)]
    out_shape = [jax.ShapeDtypeStruct((n_seq * n_blk * n_qb * tq, B_WIDTH), out_dtype)]
    if cache_rows:
        out_specs += [pl.BlockSpec((n_bat, WINDOW, B_KV_WIDTH), prev_map)] * 2
        out_shape += [jax.ShapeDtypeStruct((n_seq, WINDOW, B_KV_WIDTH), F32)] * 2
    return pl.pallas_call(
        body,
        grid=(n_seq // n_bat, n_blk),
        in_specs=[pl.BlockSpec(memory_space=pltpu.SMEM),
                  pl.BlockSpec((blk_rows, B_WIDTH), row),
                  pl.BlockSpec((n_bat, WINDOW, B_KV_WIDTH), prev_map),
                  pl.BlockSpec((n_bat, WINDOW, B_KV_WIDTH), prev_map),
                  pl.BlockSpec((blk_rows, B_KV_WIDTH), row),
                  pl.BlockSpec((blk_rows, B_KV_WIDTH), row),
                  pl.BlockSpec(bias_p.shape, whole),
                  pl.BlockSpec(bias_c.shape, whole),
                  pl.BlockSpec((1, B_WIDTH), lambda b, n: (0, 0))],
        out_specs=out_specs,
        out_shape=out_shape,
        compiler_params=pltpu.CompilerParams(dimension_semantics=("arbitrary", "arbitrary"),
                                             vmem_limit_bytes=VMEM_LIMIT),
        name="swa",
    )(sink, q2d, kprev, vprev, kcur2d, vcur2d, bias_p, bias_c, gn)


def _route_cols(lg):
    row = lax.broadcasted_iota(jnp.int32, (LOGIT_ROWS, 1), 0)
    row_f = row.astype(F32)
    no_row = float(LOGIT_ROWS)
    is_group = row < N_GROUPS
    m_g = jnp.max(jnp.where(is_group, lg, MASK_NEG), axis=0, keepdims=True)
    g_idx = jnp.min(jnp.where(is_group & (lg == m_g), row_f, no_row), axis=0, keepdims=True)
    p_group = 1.0 / jnp.sum(jnp.where(is_group, jnp.exp(lg - m_g), 0.0), axis=0, keepdims=True)
    e_row = row - N_GROUPS
    in_group = (e_row >= 0) & (e_row < N_EXPERTS) & ((e_row >> 2).astype(F32) == g_idx)
    m_1 = jnp.max(jnp.where(in_group, lg, MASK_NEG), axis=0, keepdims=True)
    i_1 = jnp.min(jnp.where(in_group & (lg == m_1), row_f, no_row), axis=0, keepdims=True)
    rest = in_group & (row_f != i_1)
    m_2 = jnp.max(jnp.where(rest, lg, MASK_NEG), axis=0, keepdims=True)
    i_2 = jnp.min(jnp.where(rest & (lg == m_2), row_f, no_row), axis=0, keepdims=True)
    ratio = jnp.exp(m_2 - m_1)
    w_1 = p_group / (1.0 + ratio)
    out_row = lax.broadcasted_iota(jnp.int32, (ROUTE_ROWS, 1), 0)
    return jnp.where(out_row == 0, i_1 - N_GROUPS,
                     jnp.where(out_row == 1, i_2 - N_GROUPS,
                               jnp.where(out_row == 2, w_1, jnp.where(out_row == 3, w_1 * ratio, 0.0))))


def _outproj_body(*refs, seg_sizes):
    n_seg = len(seg_sizes)
    x_refs, oa_refs, ob_refs = refs[:n_seg], refs[n_seg:2 * n_seg], refs[2 * n_seg:3 * n_seg]
    wa_ref, wb_ref, g_ref, wr_ref, br_ref, x1_ref, h2_ref, rt_ref = refs[3 * n_seg:]

    def run(x_ref, oa_ref, ob_ref):
        x1 = x_ref[...] + _dot(oa_ref[...].astype(BF16), wa_ref[...]) + _dot(ob_ref[...].astype(BF16), wb_ref[...])
        x1_ref[...] = x1
        h2 = x1 * lax.rsqrt(jnp.mean(x1 * x1, axis=-1, keepdims=True) + RMS_EPS) * g_ref[...]
        h2b = h2.astype(BF16)
        h2_ref[...] = h2b
        rt_ref[...] = _route_cols(_dot_nt(wr_ref[...], h2b) + br_ref[...])

    _run_segment(pl.program_id(0), seg_sizes, run, list(zip(x_refs, oa_refs, ob_refs)))


def _outproj_call(x_list, x_firsts, oa_list, ob_list, seg_sizes, wa, wb, g, wr, br, *, tm):
    d = x_list[0].shape[1]
    n_tiles = sum(seg_sizes)
    n = n_tiles * tm
    row = lambda i: (i, 0)
    fixed = lambda i: (0, 0)
    zeros = [0] * len(seg_sizes)
    return pl.pallas_call(
        functools.partial(_outproj_body, seg_sizes=tuple(seg_sizes)),
        grid=(n_tiles,),
        in_specs=(_seg_specs(tm, d, seg_sizes, x_firsts) + _seg_specs(tm, A_WIDTH, seg_sizes, zeros)
                  + _seg_specs(tm, B_WIDTH, seg_sizes, zeros)
                  + [pl.BlockSpec((A_WIDTH, d), fixed), pl.BlockSpec((B_WIDTH, d), fixed), pl.BlockSpec((1, d), fixed),
                     pl.BlockSpec((LOGIT_ROWS, d), fixed), pl.BlockSpec((LOGIT_ROWS, 1), fixed)]),
        out_specs=[pl.BlockSpec((tm, d), row), pl.BlockSpec((tm, d), row), pl.BlockSpec((ROUTE_ROWS, tm), row)],
        out_shape=[jax.ShapeDtypeStruct((n, d), F32), jax.ShapeDtypeStruct((n, d), BF16),
                   jax.ShapeDtypeStruct((n_tiles * ROUTE_ROWS, tm), F32)],
        compiler_params=pltpu.CompilerParams(dimension_semantics=("arbitrary",), vmem_limit_bytes=VMEM_LIMIT),
        name="outproj",
    )(*x_list, *oa_list, *ob_list, wa, wb, g, wr, br)


def _expert_body(vb_ref, ve_ref, lo_ref, hi_ref, x_ref, wg_ref, wu_ref, wd_ref, y_ref, wg_bf, wu_bf, wd_bf):
    v = pl.program_id(0)
    pv = jnp.maximum(v - 1, 0)
    lo, hi = lo_ref[v], hi_ref[v]
    first_visit = (v == 0) | (vb_ref[v] != vb_ref[pv])

    @pl.when((v == 0) | (ve_ref[v] != ve_ref[pv]))
    def _():
        wg_bf[...] = wg_ref[0].astype(BF16)
        wu_bf[...] = wu_ref[0].astype(BF16)
        wd_bf[...] = wd_ref[0].astype(BF16)

    @pl.when(hi > lo)
    def _():
        x = x_ref[...]
        gate = _dot(x, wg_bf[...])
        up = _dot(x, wu_bf[...])
        mid = (gate * _sigmoid(gate) * up).astype(BF16)
        y = _dot(mid, wd_bf[...])
        row = lax.broadcasted_iota(jnp.int32, (y.shape[0], 1), 0)
        mine = (row >= lo) & (row < hi)

        @pl.when(first_visit)
        def _():
            y_ref[...] = jnp.where(mine, y, 0.0)

        @pl.when(jnp.logical_not(first_visit))
        def _():
            y_ref[...] = jnp.where(mine, y, y_ref[...])


def _expert_call(visits, xs, wg, wu, wd, *, tb):
    n_rows, d = xs.shape
    ff = wg.shape[-1]
    vb, ve, lo, hi = visits
    blk = lambda v, vb, ve, lo, hi: (vb[v], 0)
    wsel = lambda v, vb, ve, lo, hi: (ve[v], 0, 0)
    grid_spec = pltpu.PrefetchScalarGridSpec(
        num_scalar_prefetch=4,
        grid=(vb.shape[0],),
        in_specs=[pl.BlockSpec((tb, d), blk), pl.BlockSpec((1, d, ff), wsel), pl.BlockSpec((1, d, ff), wsel),
                  pl.BlockSpec((1, ff, d), wsel)],
        out_specs=pl.BlockSpec((tb, d), blk),
        scratch_shapes=[pltpu.VMEM((d, ff), BF16), pltpu.VMEM((d, ff), BF16), pltpu.VMEM((ff, d), BF16)],
    )
    return pl.pallas_call(
        _expert_body,
        grid_spec=grid_spec,
        out_shape=jax.ShapeDtypeStruct((n_rows, d), F32),
        compiler_params=pltpu.CompilerParams(dimension_semantics=("arbitrary",), vmem_limit_bytes=VMEM_LIMIT),
        name="experts",
    )(vb, ve, lo, hi, xs, wg, wu, wd)


def _final_body(*refs, seg_sizes):
    n_seg = len(seg_sizes)
    g_ref, o_ref = refs[4 * n_seg:]

    def run(x1_ref, y0_ref, y1_ref, rt_ref):
        tm = rt_ref.shape[1]
        rt = jnp.concatenate([rt_ref[...], jnp.zeros((LANE - ROUTE_ROWS, tm), F32)], axis=0).T
        x = x1_ref[...] + (rt[:, 2:3] * y0_ref[...] + rt[:, 3:4] * y1_ref[...])
        o_ref[...] = x * lax.rsqrt(jnp.mean(x * x, axis=-1, keepdims=True) + RMS_EPS) * g_ref[...]

    _run_segment(pl.program_id(0), seg_sizes, run, [refs[4 * s:4 * s + 4] for s in range(n_seg)])


def _final_call(sources, firsts, seg_sizes, g, *, tm):
    d = sources[0][0].shape[1]
    n_tiles = sum(seg_sizes)
    in_specs, start = [], 0
    for size, first in zip(seg_sizes, firsts):
        src = lambda i, s=start, f=first, n=size: (f + jnp.clip(i - s, 0, n - 1), 0)
        in_specs += [pl.BlockSpec((tm, d), src)] * 3 + [pl.BlockSpec((ROUTE_ROWS, tm), src)]
        start += size
    return pl.pallas_call(
        functools.partial(_final_body, seg_sizes=tuple(seg_sizes)),
        grid=(n_tiles,),
        in_specs=in_specs + [pl.BlockSpec((1, d), lambda i: (0, 0))],
        out_specs=pl.BlockSpec((tm, d), lambda i: (i, 0)),
        out_shape=jax.ShapeDtypeStruct((n_tiles * tm, d), F32),
        compiler_params=pltpu.CompilerParams(dimension_semantics=("arbitrary",), vmem_limit_bytes=VMEM_LIMIT),
        name="final",
    )(*[a for src in sources for a in src], g)


def _dispatch(flat_e, n, tb):
    n_assign = n * TOP_K_INNER
    idx_bits = (n_assign - 1).bit_length()
    assert N_EXPERTS << idx_bits < 2 ** 31 and n_assign % tb == 0
    ids = jnp.arange(n_assign, dtype=jnp.int32)
    sorted_key = jnp.sort((flat_e << idx_bits) | ids)
    sorted_id = sorted_key & ((1 << idx_bits) - 1)
    src_tok = sorted_id - n * (sorted_id // n)

    experts = jnp.arange(N_EXPERTS, dtype=jnp.int32)
    onehot = (flat_e[:, None] == experts[None, :]).astype(jnp.int32)
    csum = jnp.cumsum(onehot, axis=0)
    counts = csum[-1]
    ends = jnp.cumsum(counts)
    starts = ends - counts
    dest = (jnp.sum((csum - 1 + starts[None, :]) * onehot, axis=1)).astype(jnp.int32)

    n_blocks = n_assign // tb
    n_visits = n_blocks + N_EXPERTS - 1
    first_blk = starts // tb
    n_vis_e = jnp.where(counts > 0, (ends + tb - 1) // tb - first_blk, 0)
    v_end = jnp.cumsum(n_vis_e)
    v = jnp.arange(n_visits, dtype=jnp.int32)
    valid = v < v_end[-1]
    last_e = jnp.max(jnp.where(counts > 0, experts, 0))
    ve = jnp.where(valid, jnp.sum((v_end[None, :] <= v[:, None]).astype(jnp.int32), axis=1), last_e)
    pick = lambda a: jnp.sum(jnp.where(ve[:, None] == experts[None, :], a[None, :], 0), axis=1)
    vb = pick(first_blk) + v - pick(v_end - n_vis_e)
    lo = jnp.maximum(pick(starts), vb * tb) - vb * tb
    hi = jnp.minimum(pick(ends), (vb + 1) * tb) - vb * tb
    vb = jnp.where(valid, vb, n_blocks - 1)
    lo = jnp.where(valid, lo, 0)
    hi = jnp.where(valid, hi, 0)
    return dest, src_tok, (vb.astype(jnp.int32), ve.astype(jnp.int32), lo.astype(jnp.int32), hi.astype(jnp.int32))


def _q_perm():
    cols = []
    for g in range(B_GROUP):
        for kv in range(B_KV_HEADS):
            h = kv * B_GROUP + g
            cols.extend(range(h * HEAD_DIM, (h + 1) * HEAD_DIM))
    return jnp.array(cols, dtype=jnp.int32)


def _layer(x_prompt, x_sample, state_rwkv, state_shift, cache_win_k, cache_win_v, prm, norm_final_g, *,
           chunk, n_sub, moe_block):
    bp, tp, d = x_prompt.shape
    bs, ts, _ = x_sample.shape
    n_p, n_s = bp * tp, bs * ts
    ts_pad = SUBLANE
    tm = n_s
    assert tp % tm == 0 and tm % 8 == 0 and cache_win_k.shape[1] == WINDOW
    xp = x_prompt.reshape(n_p, d)
    xs = x_sample.reshape(n_s, d)

    qp = _q_perm()
    w_in = prm["w_in"]
    w_in = jnp.concatenate([w_in[:, :SHIFT_W], w_in[:, SHIFT_W:SHIFT_W + B_WIDTH][:, qp],
                            w_in[:, SHIFT_W + B_WIDTH:]], axis=1).astype(BF16)
    gn = prm["attn_norm_g"][qp][None]
    w_out = prm["w_out"]
    wa, wb = w_out[:A_WIDTH].astype(BF16), w_out[A_WIDTH:][qp].astype(BF16)
    pad_rows_r = jnp.zeros((LOGIT_ROWS - N_GROUPS - N_EXPERTS, d), F32)
    wr = jnp.concatenate([prm["w_route_group"].T, prm["w_route_expert"].T, pad_rows_r], axis=0).astype(BF16)
    br = jnp.concatenate([prm["b_route_group"], prm["b_route_expert"], pad_rows_r[:, 0]])[:, None]

    zero_blk = jnp.zeros((LORA_W // 2, A_WIDTH), F32)
    wlora = jnp.concatenate([jnp.concatenate([prm["w_decay_up"], zero_blk], axis=1),
                             jnp.concatenate([zero_blk, prm["w_iclr_up"]], axis=1)], axis=0).astype(BF16)
    rp = dict(mu=prm["mu_shift"][None], w0=prm["w_decay0"][None], a0=prm["w_iclr0"][None], wlora=wlora,
              wgate=prm["w_gate_up"].astype(BF16), kk=prm["k_k"][None], ka=prm["k_a"][None],
              rk=prm["r_k"].reshape(1, A_WIDTH), lng=prm["lnx_g"][None], lnb=prm["lnx_b"][None])

    groups = [(0, bp)]
    tiles_per_seq = tp // tm
    nb = tp // WINDOW
    n_qb = _row_tile(nb, SWA_BLOCKS_PER_STEP)
    kv4 = lambda a: a.reshape(a.shape[0], a.shape[1], B_KV_HEADS, HEAD_DIM)
    finals, s_p, sh_p, k_p, v_p = [], [], [], [], []
    for gi, (s0, s1) in enumerate(groups):
        has_sample = gi == len(groups) - 1
        n_seq = s1 - s0
        n_pt = n_seq * tiles_per_seq
        n_pg = n_pt * tm
        seg_sizes = [n_pt] + ([1] if has_sample else [])
        pa, q, k, v = _inproj_call([xp] + ([xs] if has_sample else []), [s0 * tiles_per_seq, 0], seg_sizes,
                                   prm["norm_mix_g"][None], w_in, tm=tm)

        oa_p, s_pg = _rwkv_call(pa, jnp.zeros((n_seq, 1, SHIFT_W), F32),
                                jnp.zeros((n_seq,) + state_rwkv.shape[1:], F32), rp, n_seq=n_seq,
                                n_chunks=tp // chunk, chunk=chunk, n_sub=n_sub, n_bat=1, t_real=chunk)
        ob_p, = _swa_call(prm["attn_sink"], q, k.reshape(-1, WINDOW, B_KV_WIDTH), v.reshape(-1, WINDOW, B_KV_WIDTH),
                          k, v, gn, n_seq=n_seq, n_blk=nb // n_qb, tq=WINDOW, n_bat=1, n_qb=n_qb,
                          first_has_prev=False, out_dtype=BF16,
                          prev_map=lambda b, i: (b * nb + jnp.maximum(i * n_qb - 1, 0), 0, 0))
        oa_list, ob_list = [oa_p], [ob_p]
        if has_sample:
            pad_rows = lambda a: jnp.pad(a[n_pg:].reshape(bs, ts, -1), ((0, 0), (0, ts_pad - ts), (0, 0))).reshape(
                bs * ts_pad, -1)
            oa_s, s_s = _rwkv_call(pad_rows(pa), state_shift[:, None, :], state_rwkv, rp, n_seq=bs,
                                   n_chunks=1, chunk=ts_pad, n_sub=1, n_bat=_row_tile(bs, SAMPLE_SEQS_PER_STEP), t_real=ts)
            ob_s, k_s, v_s = _swa_call(prm["attn_sink"], pad_rows(q).astype(F32),
                                       cache_win_k.reshape(bs, WINDOW, B_KV_WIDTH),
                                       cache_win_v.reshape(bs, WINDOW, B_KV_WIDTH), pad_rows(k), pad_rows(v), gn,
                                       n_seq=bs, n_blk=1, tq=ts_pad, n_bat=_row_tile(bs, SAMPLE_SEQS_PER_STEP), first_has_prev=True,
                                       prev_map=lambda b, i: (b, 0, 0), out_dtype=F32, cache_rows=ts)
            oa_list.append(oa_s.reshape(bs, ts_pad, A_WIDTH)[:, :ts].reshape(n_s, A_WIDTH))
            ob_list.append(ob_s.reshape(bs, ts_pad, B_WIDTH)[:, :ts].reshape(n_s, B_WIDTH))
            sh_s = pa[n_pg:].reshape(bs, ts, SHIFT_W)[:, -1]

        x1, h2, route = _outproj_call([xp] + ([xs] if has_sample else []), [s0 * tiles_per_seq, 0],
                                               oa_list, ob_list, seg_sizes, wa, wb, prm["norm_ffn_g"][None], wr, br,
                                               tm=tm)
        n = sum(seg_sizes) * tm
        route_t = route.reshape(n // tm, ROUTE_ROWS, tm)
        flat_e = jnp.concatenate([route_t[:, j, :].reshape(n) for j in range(TOP_K_INNER)]).astype(jnp.int32)
        dest, src_tok, visits = _dispatch(flat_e, n, moe_block)
        ybuf = _expert_call(visits, h2[src_tok], prm["w_exp_gate"], prm["w_exp_up"], prm["w_exp_down"], tb=moe_block)
        finals.append((x1, ybuf[dest[:n]], ybuf[dest[n:]], route))

        last_rows = lambda a, m: jnp.stack([a[(b + 1) * tp - m:(b + 1) * tp] for b in range(n_seq)], axis=0)
        s_p.append(s_pg)
        sh_p.append(last_rows(pa, 1)[:, 0])
        k_p.append(last_rows(k, WINDOW))
        v_p.append(last_rows(v, WINDOW))

    gf = norm_final_g[None]
    n_pts = [(s1 - s0) * tiles_per_seq for s0, s1 in groups]
    y_p = _final_call(finals, [0] * len(groups), n_pts, gf, tm=tm)
    y_s = _final_call(finals[-1:], n_pts[-1:], [1], gf, tm=tm)
    cat = lambda parts: jnp.concatenate(parts, axis=0)
    return (y_p.reshape(bp, tp, d), y_s.reshape(bs, ts, d), cat(s_p), cat(sh_p), kv4(cat(k_p)), kv4(cat(v_p)),
            s_s, sh_s, kv4(k_s), kv4(v_s))


def kernel(x_prompt, x_sample, state_rwkv, state_shift, cache_win_k, cache_win_v, norm_mix_g, w_in, mu_shift, w_decay0, w_decay_up, w_iclr0, w_iclr_up, w_gate_up, k_k, k_a, r_k, lnx_g, lnx_b, attn_sink, attn_norm_g, w_out, norm_ffn_g, w_route_group, b_route_group, w_route_expert, b_route_expert, w_exp_gate, w_exp_up, w_exp_down, norm_final_g):
    assert norm_mix_g.shape[0] == 1, "single-layer trunk"
    prm = dict(norm_mix_g=norm_mix_g[0], w_in=w_in[0], mu_shift=mu_shift[0], w_decay0=w_decay0[0],
               w_decay_up=w_decay_up[0], w_iclr0=w_iclr0[0], w_iclr_up=w_iclr_up[0], w_gate_up=w_gate_up[0],
               k_k=k_k[0], k_a=k_a[0], r_k=r_k[0], lnx_g=lnx_g[0], lnx_b=lnx_b[0], attn_sink=attn_sink[0],
               attn_norm_g=attn_norm_g[0], w_out=w_out[0], norm_ffn_g=norm_ffn_g[0],
               w_route_group=w_route_group[0], b_route_group=b_route_group[0],
               w_route_expert=w_route_expert[0], b_route_expert=b_route_expert[0],
               w_exp_gate=w_exp_gate[0], w_exp_up=w_exp_up[0], w_exp_down=w_exp_down[0])
    outs = _layer(x_prompt, x_sample, state_rwkv[0], state_shift[0], cache_win_k[0], cache_win_v[0], prm,
                  norm_final_g, chunk=RWKV_CHUNK, n_sub=RWKV_CHUNKS_PER_STEP, moe_block=MOE_BLOCK)
    y_p, y_s, s_p, sh_p, kp, vp, s_s, sh_s, ks, vs = outs
    return (y_p, y_s, s_p[None], sh_p[None], kp[None], vp[None], s_s[None], sh_s[None], ks[None], vs[None])
```

```python
import functools
import math

import jax
import jax.numpy as jnp
from jax import lax
from jax.experimental import pallas as pl
from jax.experimental.pallas import tpu as pltpu

F32 = jnp.float32
BF16 = jnp.bfloat16

HEAD_DIM = 64
A_HEADS = 8
A_WIDTH = A_HEADS * HEAD_DIM
B_HEADS = 8
B_KV_HEADS = 2
B_GROUP = B_HEADS // B_KV_HEADS
B_WIDTH = B_HEADS * HEAD_DIM
B_KV_WIDTH = B_KV_HEADS * HEAD_DIM
DECAY_LORA = 64
ICLR_LORA = 64
GATE_LORA = 128
LORA_W = DECAY_LORA + ICLR_LORA
SHIFT_W = 3 * A_WIDTH + LORA_W + GATE_LORA
IN_W = SHIFT_W + B_WIDTH + 2 * B_KV_WIDTH
WINDOW = 128
N_GROUPS = 4
EXPERTS_PER_GROUP = 4
N_EXPERTS = N_GROUPS * EXPERTS_PER_GROUP
TOP_K_INNER = 2
EXPERT_FF = 512
RMS_EPS = 1e-6
LNX_EPS = 64e-5
DECAY_OFFSET = 0.5

LANE = 128
PAIR = LANE // HEAD_DIM
N_PAIRS = A_HEADS // PAIR
ROUTE_ROWS = 8
LOGIT_ROWS = 32
MASK_NEG = -1e30
VMEM_LIMIT = 48 * 1024 * 1024
SUBLANE = 8

RWKV_CHUNK = 64
RWKV_CHUNKS_PER_STEP = 8
SWA_BLOCKS_PER_STEP = 8
SAMPLE_SEQS_PER_STEP = 16
MOE_BLOCK = 512

LOG2_E = math.log2(math.e)
DECAY_SCALE = math.exp(-DECAY_OFFSET)
HEAD_SHIFT = HEAD_DIM.bit_length() - 1
HI = lax.Precision.HIGHEST


def _row_tile(n, cap):
    t = cap
    while n % t:
        t //= 2
    return t


def _dot(a, b, precision=None):
    return jnp.dot(a, b, preferred_element_type=F32, precision=precision)


def _dot_nt(a, b, precision=None):
    return lax.dot_general(a, b, (((1,), (1,)), ((), ())), preferred_element_type=F32, precision=precision)


def _dot_tn(a, b, precision=None):
    return lax.dot_general(a, b, (((0,), (0,)), ((), ())), preferred_element_type=F32, precision=precision)


def _sigmoid(x):
    return 1.0 / (1.0 + jnp.exp(-x))


def _hi_lo(x, axis):
    hi = x.astype(BF16)
    lo = (x - hi.astype(F32)).astype(BF16)
    return jnp.concatenate([hi, lo], axis=axis)


def _run_segment(i, seg_sizes, run, seg_refs):
    start = 0
    for size, refs in zip(seg_sizes, seg_refs):
        pl.when((i >= start) & (i < start + size))(functools.partial(run, *refs))
        start += size


def _seg_specs(tm, width, seg_sizes, firsts):
    specs, start = [], 0
    for size, first in zip(seg_sizes, firsts):
        specs.append(pl.BlockSpec((tm, width),
                                  lambda i, s=start, f=first, n=size: (f + jnp.clip(i - s, 0, n - 1), 0)))
        start += size
    return specs


def _inproj_body(*refs, seg_sizes):
    n_seg = len(seg_sizes)
    g_ref, w_ref, pa_ref, q_ref, k_ref, v_ref = refs[n_seg:]

    def run(x_ref):
        x = x_ref[...]
        h = x * lax.rsqrt(jnp.mean(x * x, axis=-1, keepdims=True) + RMS_EPS) * g_ref[...]
        p = _dot(h.astype(BF16), w_ref[...])
        pa_ref[...] = p[:, :SHIFT_W]
        q_ref[...] = (p[:, SHIFT_W:SHIFT_W + B_WIDTH] * (HEAD_DIM ** -0.5)).astype(BF16)
        k_ref[...] = p[:, SHIFT_W + B_WIDTH:SHIFT_W + B_WIDTH + B_KV_WIDTH]
        v_ref[...] = p[:, SHIFT_W + B_WIDTH + B_KV_WIDTH:]

    _run_segment(pl.program_id(0), seg_sizes, run, [(r,) for r in refs[:n_seg]])


def _inproj_call(xs_list, firsts, seg_sizes, g, w_bf16, *, tm):
    d = xs_list[0].shape[1]
    n_tiles = sum(seg_sizes)
    n = n_tiles * tm
    row = lambda i: (i, 0)
    fixed = lambda i: (0, 0)
    return pl.pallas_call(
        functools.partial(_inproj_body, seg_sizes=tuple(seg_sizes)),
        grid=(n_tiles,),
        in_specs=_seg_specs(tm, d, seg_sizes, firsts) + [pl.BlockSpec((1, d), fixed), pl.BlockSpec((d, IN_W), fixed)],
        out_specs=[pl.BlockSpec((tm, SHIFT_W), row), pl.BlockSpec((tm, B_WIDTH), row),
                   pl.BlockSpec((tm, B_KV_WIDTH), row), pl.BlockSpec((tm, B_KV_WIDTH), row)],
        out_shape=[jax.ShapeDtypeStruct((n, SHIFT_W), F32), jax.ShapeDtypeStruct((n, B_WIDTH), BF16),
                   jax.ShapeDtypeStruct((n, B_KV_WIDTH), F32), jax.ShapeDtypeStruct((n, B_KV_WIDTH), F32)],
        compiler_params=pltpu.CompilerParams(dimension_semantics=("arbitrary",), vmem_limit_bytes=VMEM_LIMIT),
        name="inproj",
    )(*xs_list, g, w_bf16)


def _rwkv_body(pa_ref, shift0_ref, s0_ref, mu_ref, w0_ref, a0_ref, wlora_ref, wgate_ref, kk_ref, ka_ref,
               rk_ref, lng_ref, lnb_ref, oa_ref, sout_ref, s_scr, prev_scr, *, chunk, n_sub, n_bat, t_real):
    C = chunk
    seq_rows = n_sub * C
    n_seg = n_bat * n_sub
    rows = n_seg * C
    c = pl.program_id(1)

    @pl.when(c == 0)
    def _():
        prev_scr[...] = shift0_ref[...]
        for b in range(n_bat):
            for j in range(N_PAIRS):
                s_scr[b * N_PAIRS + j] = jnp.concatenate([s0_ref[b, PAIR * j + h] for h in range(PAIR)], axis=1)

    pa = pa_ref[...]
    row = lax.broadcasted_iota(jnp.int32, (rows, 1), 0)
    row_in_chunk = row & (C - 1)
    pa_prev = pltpu.roll(pa, 1, axis=0)
    for b in range(n_bat):
        pa_prev = jnp.where(row == b * seq_rows, prev_scr[b], pa_prev)
        last = (b + 1) * seq_rows - C + t_real - 1
        prev_scr[b] = pa[last:last + 1]
    xm = pa + mu_ref[...] * (pa_prev - pa)

    r = xm[:, :A_WIDTH]
    k = xm[:, A_WIDTH:2 * A_WIDTH]
    v = xm[:, 2 * A_WIDTH:3 * A_WIDTH]
    lora_in = xm[:, 3 * A_WIDTH:3 * A_WIDTH + LORA_W]
    gd = xm[:, 3 * A_WIDTH + LORA_W:]

    lane = lax.broadcasted_iota(jnp.int32, (1, LANE), 1)
    lo_half = lane < HEAD_DIM
    z = jnp.where(lo_half, jnp.tanh(lora_in), lora_in)
    lw = _dot(z.astype(BF16), wlora_ref[...])
    dec_pre = w0_ref[...] + lw[:, :A_WIDTH]
    a = _sigmoid(a0_ref[...] + lw[:, A_WIDTH:])
    logdec = (-LOG2_E * DECAY_SCALE) * _sigmoid(dec_pre)
    g = _dot(_sigmoid(gd).astype(BF16), wgate_ref[...])

    ri = lax.broadcasted_iota(jnp.int32, (LANE, LANE), 0)
    ci = lax.broadcasted_iota(jnp.int32, (LANE, LANE), 1)
    same_head = (ri >> HEAD_SHIFT) == (ci >> HEAD_SHIFT)
    seg = jnp.where(same_head, 1.0, 0.0).astype(BF16)
    seg2 = jnp.concatenate([seg, seg], axis=0)

    def headsum(x):
        return jnp.concatenate([_dot(_hi_lo(x[:, j * LANE:(j + 1) * LANE], 1), seg2) for j in range(N_PAIRS)],
                               axis=1)

    kk = k * kk_ref[...]
    kk = kk * lax.rsqrt(jnp.maximum(headsum(kk * kk), 1e-24))
    k = k * (1.0 + (a - 1.0) * ka_ref[...])
    bonus = headsum(r * k * rk_ref[...]) * v

    if t_real < C:
        valid = row_in_chunk < t_real
        logdec = jnp.where(valid, logdec, 0.0)
        kk = jnp.where(valid, kk, 0.0)
        k = jnp.where(valid, k, 0.0)
        v = jnp.where(valid, v, 0.0)

    cum = logdec
    shift = 1
    while shift < C:
        cum = cum + jnp.where(row_in_chunk >= shift, pltpu.roll(cum, shift, axis=0), 0.0)
        shift *= 2
    ends = [cum[(s + 1) * C - 1:(s + 1) * C] for s in range(n_seg)]
    cum_end = jnp.concatenate([jnp.broadcast_to(e, (C, A_WIDTH)) for e in ends], axis=0) if n_seg > 1 else ends[0]
    w_incl = jnp.exp2(cum)
    w_prev = jnp.exp2(cum - logdec)
    w_inv = jnp.exp2(-cum)
    w_end = jnp.exp2(cum_end - cum)
    w_chunk = [jnp.exp2(e) for e in ends]
    kka = kk * a
    terms = dict(A=-kk * w_prev, R=r * w_incl, B=kka * w_inv, K=k * w_inv, V=v, Be=kka * w_end, Ke=k * w_end)

    C2 = PAIR * C
    t_idx = lax.broadcasted_iota(jnp.int32, (C, 1), 0)
    i_idx = lax.broadcasted_iota(jnp.int32, (1, C2), 1) & (C - 1)
    strict = i_idx < t_idx
    incl = i_idx <= t_idx
    ident = jnp.where(i_idx == t_idx, 1.0, 0.0).astype(F32)
    lo_time = lax.broadcasted_iota(jnp.int32, (1, C2), 1) < C
    n_levels = max(1, (C - 1).bit_length())
    bf = lambda x: x.astype(BF16)

    def bd(x, lo_mask=lo_half):
        return bf(jnp.concatenate([jnp.where(lo_mask, x, 0.0), jnp.where(lo_mask, 0.0, x)], axis=0))

    chains = [(s, j) for s in range(n_seg) for j in range(N_PAIRS)]
    tile = lambda name, s, j: terms[name][s * C:(s + 1) * C, j * LANE:(j + 1) * LANE]
    r_sbs = {ch: tile("R", *ch) for ch in chains}
    stk = {ch: {nm: bd(tile(nm, *ch)) for nm in ("A", "B", "K", "V")} for ch in chains}
    m_ab, aak, arb, ark = {}, {}, {}, {}
    for ch in chains:
        t = stk[ch]
        a_sbs, rb = bf(tile("A", *ch)), bf(r_sbs[ch])
        if C2 % LANE == 0:
            m1 = _dot_nt(jnp.concatenate([a_sbs, rb], axis=0), jnp.concatenate([t["B"], t["K"]], axis=0))
            ab, ak, rbm, rk = m1[:C, :C2], m1[:C, C2:], m1[C:, :C2], m1[C:, C2:]
        else:
            ab, ak = _dot_nt(a_sbs, t["B"]), _dot_nt(a_sbs, t["K"])
            rbm, rk = _dot_nt(rb, t["B"]), _dot_nt(rb, t["K"])
        m_ab[ch] = jnp.where(strict, ab, 0.0)
        aak[ch] = jnp.where(strict, ak, 0.0)
        arb[ch] = bf(jnp.where(incl, rbm, 0.0))
        ark[ch] = jnp.where(incl, rk, 0.0)

    nn = dict(m_ab)
    tinv = {ch: ident + m_ab[ch] for ch in chains}
    for lvl in range(1, n_levels):
        if lvl == 1:
            for ch in chains:
                nn[ch] = _dot(bf(nn[ch]), bd(nn[ch], lo_time))
            continue
        for ch in chains:
            both = _dot(bf(jnp.concatenate([nn[ch], tinv[ch]], axis=0)), bd(nn[ch], lo_time))
            nn[ch] = both[:C]
            tinv[ch] = tinv[ch] + both[C:]
    if n_levels > 1:
        for ch in chains:
            tinv[ch] = tinv[ch] + _dot(bf(tinv[ch]), bd(nn[ch], lo_time))
    av = {ch: _dot(bf(jnp.concatenate([aak[ch], ark[ch]], axis=0)), stk[ch]["V"]) for ch in chains}
    x = {ch: _dot(bf(tinv[ch]), jnp.concatenate([stk[ch]["A"], bd(av[ch][:C])], axis=1)) for ch in chains}
    a_eff = {ch: bd(x[ch][:, :LANE]) for ch in chains}
    v_eff = {ch: bd(x[ch][:, LANE:]) for ch in chains}
    zed = {ch: _dot(arb[ch], jnp.concatenate([a_eff[ch], v_eff[ch]], axis=1)) for ch in chains}
    rhat = {ch: bf(r_sbs[ch] + zed[ch][:, :LANE]) for ch in chains}
    y0 = {ch: av[ch][C:] + zed[ch][:, LANE:] for ch in chains}
    p_mat = {ch: bf(jnp.where(same_head, _dot_tn(bf(tile("Be", *ch)), bf(x[ch][:, :LANE])), 0.0))
             for ch in chains}
    q_full = {ch: _dot_tn(bf(jnp.concatenate([x[ch][:, LANE:], tile("V", *ch)], axis=0)),
                          bf(jnp.concatenate([tile("Be", *ch), tile("Ke", *ch)], axis=0))) for ch in chains}

    y_rows = []
    for b in range(n_bat):
        state = [s_scr[b * N_PAIRS + j] for j in range(N_PAIRS)]
        for sub in range(n_sub):
            s = b * n_sub + sub
            ys = []
            for j in range(N_PAIRS):
                ch = (s, j)
                hi = bf(state[j])
                lo = bf(state[j] - hi.astype(F32))
                yy = _dot_nt(rhat[ch], jnp.concatenate([bd(hi), bd(lo)], axis=0))
                ys.append(yy[:, :LANE] + yy[:, LANE:] + y0[ch])
                sp = _dot_nt(jnp.concatenate([hi, lo], axis=0), p_mat[ch])
                q_sbs = jnp.where(lo_half, q_full[ch][:HEAD_DIM], q_full[ch][HEAD_DIM:])
                state[j] = state[j] * w_chunk[s][:, j * LANE:(j + 1) * LANE] + sp[:HEAD_DIM] + sp[HEAD_DIM:] + q_sbs
            y_rows.append(jnp.concatenate(ys, axis=1))
        for j in range(N_PAIRS):
            s_scr[b * N_PAIRS + j] = state[j]

    y = jnp.concatenate(y_rows, axis=0) if n_seg > 1 else y_rows[0]
    inv_n = 1.0 / HEAD_DIM
    mean = headsum(y) * inv_n
    d = y - mean
    var = headsum(d * d) * inv_n
    yn = d * lax.rsqrt(var + LNX_EPS) * lng_ref[...] + lnb_ref[...]
    oa_ref[...] = ((yn + bonus) * g).astype(oa_ref.dtype)

    @pl.when(c == pl.num_programs(1) - 1)
    def _():
        for b in range(n_bat):
            for j in range(N_PAIRS):
                for h in range(PAIR):
                    sout_ref[b, PAIR * j + h] = s_scr[b * N_PAIRS + j][:, h * HEAD_DIM:(h + 1) * HEAD_DIM]


def _rwkv_call(pa2d, shift0, s0_pairs, prm, *, n_seq, n_chunks, chunk, n_sub, n_bat, t_real):
    n_steps = n_chunks // n_sub
    assert n_seq % n_bat == 0 and (n_bat == 1 or n_steps == 1)
    row = lambda b, c: (b * n_steps + c, 0)
    seq3 = lambda b, c: (b, 0, 0)
    fixed = lambda b, c: (0, 0)
    vec = lambda w: pl.BlockSpec((1, w), fixed)
    body = functools.partial(_rwkv_body, chunk=chunk, n_sub=n_sub, n_bat=n_bat, t_real=t_real)
    n_rows = n_seq * n_chunks * chunk
    blk_rows = n_bat * n_sub * chunk
    state_blk = (n_bat, A_HEADS, HEAD_DIM, HEAD_DIM)
    seq4 = lambda b, c: (b, 0, 0, 0)
    return pl.pallas_call(
        body,
        grid=(n_seq // n_bat, n_steps),
        in_specs=[pl.BlockSpec((blk_rows, SHIFT_W), row),
                  pl.BlockSpec((n_bat, 1, SHIFT_W), seq3),
                  pl.BlockSpec(state_blk, seq4),
                  vec(SHIFT_W), vec(A_WIDTH), vec(A_WIDTH),
                  pl.BlockSpec((LORA_W, 2 * A_WIDTH), fixed), pl.BlockSpec((GATE_LORA, A_WIDTH), fixed),
                  vec(A_WIDTH), vec(A_WIDTH), vec(A_WIDTH), vec(A_WIDTH), vec(A_WIDTH)],
        out_specs=[pl.BlockSpec((blk_rows, A_WIDTH), row), pl.BlockSpec(state_blk, seq4)],
        out_shape=[jax.ShapeDtypeStruct((n_rows, A_WIDTH), BF16),
                   jax.ShapeDtypeStruct((n_seq, A_HEADS, HEAD_DIM, HEAD_DIM), F32)],
        scratch_shapes=[pltpu.VMEM((n_bat * N_PAIRS, HEAD_DIM, LANE), F32), pltpu.VMEM((n_bat, 1, SHIFT_W), F32)],
        compiler_params=pltpu.CompilerParams(dimension_semantics=("arbitrary", "arbitrary"),
                                             vmem_limit_bytes=VMEM_LIMIT),
        name="rwkv",
    )(pa2d, shift0, s0_pairs, prm["mu"], prm["w0"], prm["a0"], prm["wlora"], prm["wgate"], prm["kk"], prm["ka"],
      prm["rk"], prm["lng"], prm["lnb"])


def _swa_bias(tq):
    rows = B_GROUP * tq
    grp = jnp.arange(rows, dtype=jnp.int32)[:, None] // tq
    t = jnp.arange(rows, dtype=jnp.int32)[:, None] % tq
    dist_p = t + WINDOW - jnp.arange(WINDOW, dtype=jnp.int32)[None, :]
    dist_c = t - jnp.arange(tq, dtype=jnp.int32)[None, :]

    def bias(dist, kv):
        slope = sum(jnp.where(grp == g, 2.0 ** -(kv * B_GROUP + g + 1), 0.0) for g in range(B_GROUP))
        return jnp.where((dist >= 0) & (dist < WINDOW), -slope * dist.astype(F32), MASK_NEG)

    return (jnp.stack([bias(dist_p, kv) for kv in range(B_KV_HEADS)]),
            jnp.stack([bias(dist_c, kv) for kv in range(B_KV_HEADS)]))


def _swa_body(sink_ref, q_ref, kp_ref, vp_ref, kc_ref, vc_ref, bp_ref, bc_ref, gn_ref, o_ref, *cache_refs, tq, n_bat,
              n_qb, t_real, first_has_prev):
    nblk = pl.program_id(1)
    lane = lax.broadcasted_iota(jnp.int32, (1, LANE), 1)
    kv_masks = [(lane < HEAD_DIM) if kv == 0 else (lane >= HEAD_DIM) for kv in range(B_KV_HEADS)]
    blocks = []
    for s in range(n_bat * n_qb):
        rs = slice(s * tq, (s + 1) * tq)
        kc, vc = kc_ref[rs], vc_ref[rs]
        if n_qb > 1 and s > 0:
            ps = slice((s - 1) * tq, s * tq)
            kp, vp, has_prev = kc_ref[ps], vc_ref[ps], True
        else:
            kp, vp, has_prev = kp_ref[s], vp_ref[s], first_has_prev
        if cache_refs:
            row = lax.broadcasted_iota(jnp.int32, (WINDOW, 1), 0)
            for out_ref, old, new in zip(cache_refs, (kp, vp), (kc, vc)):
                new_tail = jnp.concatenate([pltpu.roll(new, tq - t_real, axis=0)] * (WINDOW // tq), axis=0)
                out_ref[s] = jnp.where(row >= WINDOW - t_real, new_tail, pltpu.roll(old, WINDOW - t_real, axis=0))
        pen = 0.0 if has_prev else jnp.where(nblk > 0, 0.0, MASK_NEG)
        blocks.append(dict(rs=rs, q=q_ref[rs], kp=kp.astype(BF16), kc=kc.astype(BF16), vp=vp, vc=vc, pen=pen))

    keys = [(b, kv) for b in range(len(blocks)) for kv in range(B_KV_HEADS)]
    s_p, s_c, vpm, vcm = {}, {}, {}, {}
    for b, kv in keys:
        blk, mk = blocks[b], kv_masks[kv]
        q_st = jnp.concatenate([jnp.where(mk, blk["q"][:, g * LANE:(g + 1) * LANE], 0.0) for g in range(B_GROUP)],
                               axis=0).astype(BF16)
        s_p[b, kv] = _dot_nt(q_st, blk["kp"])
        s_c[b, kv] = _dot_nt(q_st, blk["kc"])
        vpm[b, kv] = jnp.where(mk, blk["vp"], 0.0).astype(BF16)
        vcm[b, kv] = jnp.where(mk, blk["vc"], 0.0).astype(BF16)

    slabs = [(b, kv, g) for b, kv in keys for g in range(B_GROUP)]
    sink = {(kv, g): sink_ref[kv * B_GROUP + g] for kv in range(B_KV_HEADS) for g in range(B_GROUP)}
    sp, sc, m = {}, {}, {}
    for b, kv, g in slabs:
        rs = slice(g * tq, (g + 1) * tq)
        sp[b, kv, g] = s_p[b, kv][rs] + (bp_ref[kv, rs, :] + blocks[b]["pen"])
        sc[b, kv, g] = s_c[b, kv][rs] + bc_ref[kv, rs, :]
        if tq == WINDOW:
            row_max = jnp.max(jnp.maximum(sp[b, kv, g], sc[b, kv, g]), axis=-1, keepdims=True)
        else:
            row_max = jnp.maximum(jnp.max(sp[b, kv, g], axis=-1, keepdims=True),
                                  jnp.max(sc[b, kv, g], axis=-1, keepdims=True))
        m[b, kv, g] = jnp.maximum(row_max, sink[kv, g])
    e_p, e_c, e_sum = {}, {}, {}
    for key in slabs:
        ep, ec = jnp.exp(sp[key] - m[key]), jnp.exp(sc[key] - m[key])
        if tq == WINDOW:
            e_sum[key] = jnp.sum(ep + ec, axis=-1, keepdims=True)
        else:
            e_sum[key] = jnp.sum(ep, axis=-1, keepdims=True) + jnp.sum(ec, axis=-1, keepdims=True)
        e_p[key], e_c[key] = ep.astype(BF16), ec.astype(BF16)

    tiles = {}
    for b, kv, g in slabs:
        inv = 1.0 / (e_sum[b, kv, g] + jnp.exp(sink[kv, g] - m[b, kv, g]))
        o = (_dot(e_p[b, kv, g], vpm[b, kv]) + _dot(e_c[b, kv, g], vcm[b, kv])) * inv
        tiles[b, g] = o if kv == 0 else tiles[b, g] + o

    for b, blk in enumerate(blocks):
        ssq = sum(jnp.sum(tiles[b, g] * tiles[b, g], axis=-1, keepdims=True) for g in range(B_GROUP))
        inv_rms = lax.rsqrt(ssq * (1.0 / B_WIDTH) + RMS_EPS)
        out = jnp.concatenate([tiles[b, g] for g in range(B_GROUP)], axis=1) * inv_rms * gn_ref[...]
        o_ref[blk["rs"]] = out.astype(o_ref.dtype)


def _swa_call(sink, q2d, kprev, vprev, kcur2d, vcur2d, gn, *, n_seq, n_blk, tq, n_bat, first_has_prev, prev_map,
              out_dtype, n_qb=1, cache_rows=0):
    assert n_seq % n_bat == 0 and (n_bat == 1 or n_blk * n_qb == 1) and (cache_rows == 0 or n_blk * n_qb == 1)
    row = lambda b, n: (b * n_blk + n, 0)
    body = functools.partial(_swa_body, tq=tq, n_bat=n_bat, n_qb=n_qb, t_real=cache_rows,
                             first_has_prev=first_has_prev)
    bias_p, bias_c = _swa_bias(tq)
    whole = lambda b, n: (0, 0, 0)
    blk_rows = n_bat * n_qb * tq
    out_specs = [pl.BlockSpec((blk_rows, B_WIDTH), row)]
    out_shape = [jax.ShapeDtypeStruct((n_seq * n_blk * n_qb * tq, B_WIDTH), out_dtype)]
    if cache_rows:
        out_specs += [pl.BlockSpec((n_bat, WINDOW, B_KV_WIDTH), prev_map)] * 2
        out_shape += [jax.ShapeDtypeStruct((n_seq, WINDOW, B_KV_WIDTH), F32)] * 2
    return pl.pallas_call(
        body,
        grid=(n_seq // n_bat, n_blk),
        in_specs=[pl.BlockSpec(memory_space=pltpu.SMEM),
                  pl.BlockSpec((blk_rows, B_WIDTH), row),
                  pl.BlockSpec((n_bat, WINDOW, B_KV_WIDTH), prev_map),
                  pl.BlockSpec((n_bat, WINDOW, B_KV_WIDTH), prev_map),
                  pl.BlockSpec((blk_rows, B_KV_WIDTH), row),
                  pl.BlockSpec((blk_rows, B_KV_WIDTH), row),
                  pl.BlockSpec(bias_p.shape, whole),
                  pl.BlockSpec(bias_c.shape, whole),
                  pl.BlockSpec((1, B_WIDTH), lambda b, n: (0, 0))],
        out_specs=out_specs,
        out_shape=out_shape,
        compiler_params=pltpu.CompilerParams(dimension_semantics=("arbitrary", "arbitrary"),
                                             vmem_limit_bytes=VMEM_LIMIT),
        name="swa",
    )(sink, q2d, kprev, vprev, kcur2d, vcur2d, bias_p, bias_c, gn)


def _route_cols(lg):
    row = lax.broadcasted_iota(jnp.int32, (LOGIT_ROWS, 1), 0)
    row_f = row.astype(F32)
    no_row = float(LOGIT_ROWS)
    is_group = row < N_GROUPS
    m_g = jnp.max(jnp.where(is_group, lg, MASK_NEG), axis=0, keepdims=True)
    g_idx = jnp.min(jnp.where(is_group & (lg == m_g), row_f, no_row), axis=0, keepdims=True)
    p_group = 1.0 / jnp.sum(jnp.where(is_group, jnp.exp(lg - m_g), 0.0), axis=0, keepdims=True)
    e_row = row - N_GROUPS
    in_group = (e_row >= 0) & (e_row < N_EXPERTS) & ((e_row >> 2).astype(F32) == g_idx)
    m_1 = jnp.max(jnp.where(in_group, lg, MASK_NEG), axis=0, keepdims=True)
    i_1 = jnp.min(jnp.where(in_group & (lg == m_1), row_f, no_row), axis=0, keepdims=True)
    rest = in_group & (row_f != i_1)
    m_2 = jnp.max(jnp.where(rest, lg, MASK_NEG), axis=0, keepdims=True)
    i_2 = jnp.min(jnp.where(rest & (lg == m_2), row_f, no_row), axis=0, keepdims=True)
    ratio = jnp.exp(m_2 - m_1)
    w_1 = p_group / (1.0 + ratio)
    out_row = lax.broadcasted_iota(jnp.int32, (ROUTE_ROWS, 1), 0)
    return jnp.where(out_row == 0, i_1 - N_GROUPS,
                     jnp.where(out_row == 1, i_2 - N_GROUPS,
                               jnp.where(out_row == 2, w_1, jnp.where(out_row == 3, w_1 * ratio, 0.0))))


def _outproj_body(*refs, seg_sizes):
    n_seg = len(seg_sizes)
    x_refs, oa_refs, ob_refs = refs[:n_seg], refs[n_seg:2 * n_seg], refs[2 * n_seg:3 * n_seg]
    wa_ref, wb_ref, g_ref, wr_ref, br_ref, x1_ref, h2_ref, rt_ref = refs[3 * n_seg:]

    def run(x_ref, oa_ref, ob_ref):
        x1 = x_ref[...] + _dot(oa_ref[...].astype(BF16), wa_ref[...]) + _dot(ob_ref[...].astype(BF16), wb_ref[...])
        x1_ref[...] = x1
        h2 = x1 * lax.rsqrt(jnp.mean(x1 * x1, axis=-1, keepdims=True) + RMS_EPS) * g_ref[...]
        h2b = h2.astype(BF16)
        h2_ref[...] = h2b
        rt_ref[...] = _route_cols(_dot_nt(wr_ref[...], h2b) + br_ref[...])

    _run_segment(pl.program_id(0), seg_sizes, run, list(zip(x_refs, oa_refs, ob_refs)))


def _outproj_call(x_list, x_firsts, oa_list, ob_list, seg_sizes, wa, wb, g, wr, br, *, tm):
    d = x_list[0].shape[1]
    n_tiles = sum(seg_sizes)
    n = n_tiles * tm
    row = lambda i: (i, 0)
    fixed = lambda i: (0, 0)
    zeros = [0] * len(seg_sizes)
    return pl.pallas_call(
        functools.partial(_outproj_body, seg_sizes=tuple(seg_sizes)),
        grid=(n_tiles,),
        in_specs=(_seg_specs(tm, d, seg_sizes, x_firsts) + _seg_specs(tm, A_WIDTH, seg_sizes, zeros)
                  + _seg_specs(tm, B_WIDTH, seg_sizes, zeros)
                  + [pl.BlockSpec((A_WIDTH, d), fixed), pl.BlockSpec((B_WIDTH, d), fixed), pl.BlockSpec((1, d), fixed),
                     pl.BlockSpec((LOGIT_ROWS, d), fixed), pl.BlockSpec((LOGIT_ROWS, 1), fixed)]),
        out_specs=[pl.BlockSpec((tm, d), row), pl.BlockSpec((tm, d), row), pl.BlockSpec((ROUTE_ROWS, tm), row)],
        out_shape=[jax.ShapeDtypeStruct((n, d), F32), jax.ShapeDtypeStruct((n, d), BF16),
                   jax.ShapeDtypeStruct((n_tiles * ROUTE_ROWS, tm), F32)],
        compiler_params=pltpu.CompilerParams(dimension_semantics=("arbitrary",), vmem_limit_bytes=VMEM_LIMIT),
        name="outproj",
    )(*x_list, *oa_list, *ob_list, wa, wb, g, wr, br)


def _expert_body(vb_ref, ve_ref, lo_ref, hi_ref, x_ref, wg_ref, wu_ref, wd_ref, y_ref, wg_bf, wu_bf, wd_bf):
    v = pl.program_id(0)
    pv = jnp.maximum(v - 1, 0)
    lo, hi = lo_ref[v], hi_ref[v]
    first_visit = (v == 0) | (vb_ref[v] != vb_ref[pv])

    @pl.when((v == 0) | (ve_ref[v] != ve_ref[pv]))
    def _():
        wg_bf[...] = wg_ref[0].astype(BF16)
        wu_bf[...] = wu_ref[0].astype(BF16)
        wd_bf[...] = wd_ref[0].astype(BF16)

    @pl.when(hi > lo)
    def _():
        x = x_ref[...]
        gate = _dot(x, wg_bf[...])
        up = _dot(x, wu_bf[...])
        mid = (gate * _sigmoid(gate) * up).astype(BF16)
        y = _dot(mid, wd_bf[...])
        row = lax.broadcasted_iota(jnp.int32, (y.shape[0], 1), 0)
        mine = (row >= lo) & (row < hi)

        @pl.when(first_visit)
        def _():
            y_ref[...] = jnp.where(mine, y, 0.0)

        @pl.when(jnp.logical_not(first_visit))
        def _():
            y_ref[...] = jnp.where(mine, y, y_ref[...])


def _expert_call(visits, xs, wg, wu, wd, *, tb):
    n_rows, d = xs.shape
    ff = wg.shape[-1]
    vb, ve, lo, hi = visits
    blk = lambda v, vb, ve, lo, hi: (vb[v], 0)
    wsel = lambda v, vb, ve, lo, hi: (ve[v], 0, 0)
    grid_spec = pltpu.PrefetchScalarGridSpec(
        num_scalar_prefetch=4,
        grid=(vb.shape[0],),
        in_specs=[pl.BlockSpec((tb, d), blk), pl.BlockSpec((1, d, ff), wsel), pl.BlockSpec((1, d, ff), wsel),
                  pl.BlockSpec((1, ff, d), wsel)],
        out_specs=pl.BlockSpec((tb, d), blk),
        scratch_shapes=[pltpu.VMEM((d, ff), BF16), pltpu.VMEM((d, ff), BF16), pltpu.VMEM((ff, d), BF16)],
    )
    return pl.pallas_call(
        _expert_body,
        grid_spec=grid_spec,
        out_shape=jax.ShapeDtypeStruct((n_rows, d), F32),
        compiler_params=pltpu.CompilerParams(dimension_semantics=("arbitrary",), vmem_limit_bytes=VMEM_LIMIT),
        name="experts",
    )(vb, ve, lo, hi, xs, wg, wu, wd)


def _token_weight(rt_ref, slot):
    tm = rt_ref.shape[1]
    rt = jnp.concatenate([rt_ref[...], jnp.zeros((LANE - ROUTE_ROWS, tm), F32)], axis=0).T
    return rt[:, TOP_K_INNER + slot:TOP_K_INNER + slot + 1]


def _combine_body(x1_ref, y0_ref, rt_ref, o_ref):
    o_ref[...] = x1_ref[...] + _token_weight(rt_ref, 0) * y0_ref[...]


def _combine_call(x1, y0, route, *, tm):
    n, d = x1.shape
    row = lambda i: (i, 0)
    return pl.pallas_call(
        _combine_body,
        grid=(n // tm,),
        in_specs=[pl.BlockSpec((tm, d), row), pl.BlockSpec((tm, d), row), pl.BlockSpec((ROUTE_ROWS, tm), row)],
        out_specs=pl.BlockSpec((tm, d), row),
        out_shape=jax.ShapeDtypeStruct((n, d), F32),
        compiler_params=pltpu.CompilerParams(dimension_semantics=("arbitrary",), vmem_limit_bytes=VMEM_LIMIT),
        name="combine",
    )(x1, y0, route)


def _final_body(*refs, seg_sizes):
    n_seg = len(seg_sizes)
    g_ref, o_ref = refs[3 * n_seg:]

    def run(t_ref, y1_ref, rt_ref):
        x = t_ref[...] + _token_weight(rt_ref, 1) * y1_ref[...]
        o_ref[...] = x * lax.rsqrt(jnp.mean(x * x, axis=-1, keepdims=True) + RMS_EPS) * g_ref[...]

    _run_segment(pl.program_id(0), seg_sizes, run, [refs[3 * s:3 * s + 3] for s in range(n_seg)])


def _final_call(sources, firsts, seg_sizes, g, *, tm):
    d = sources[0][0].shape[1]
    n_tiles = sum(seg_sizes)
    in_specs, start = [], 0
    for size, first in zip(seg_sizes, firsts):
        src = lambda i, s=start, f=first, n=size: (f + jnp.clip(i - s, 0, n - 1), 0)
        in_specs += [pl.BlockSpec((tm, d), src)] * 2 + [pl.BlockSpec((ROUTE_ROWS, tm), src)]
        start += size
    return pl.pallas_call(
        functools.partial(_final_body, seg_sizes=tuple(seg_sizes)),
        grid=(n_tiles,),
        in_specs=in_specs + [pl.BlockSpec((1, d), lambda i: (0, 0))],
        out_specs=pl.BlockSpec((tm, d), lambda i: (i, 0)),
        out_shape=jax.ShapeDtypeStruct((n_tiles * tm, d), F32),
        compiler_params=pltpu.CompilerParams(dimension_semantics=("arbitrary",), vmem_limit_bytes=VMEM_LIMIT),
        name="final",
    )(*[a for src in sources for a in src], g)


def _dispatch(flat_e, n, tb):
    n_assign = n * TOP_K_INNER
    idx_bits = (n_assign - 1).bit_length()
    assert N_EXPERTS << idx_bits < 2 ** 31 and n_assign % tb == 0
    ids = jnp.arange(n_assign, dtype=jnp.int32)
    sorted_key = jnp.sort((flat_e << idx_bits) | ids)
    sorted_id = sorted_key & ((1 << idx_bits) - 1)
    src_tok = sorted_id - n * (sorted_id // n)

    experts = jnp.arange(N_EXPERTS, dtype=jnp.int32)
    onehot = (flat_e[:, None] == experts[None, :]).astype(jnp.int32)
    csum = jnp.cumsum(onehot, axis=0)
    counts = csum[-1]
    ends = jnp.cumsum(counts)
    starts = ends - counts
    dest = (jnp.sum((csum - 1 + starts[None, :]) * onehot, axis=1)).astype(jnp.int32)

    n_blocks = n_assign // tb
    n_visits = n_blocks + N_EXPERTS - 1
    first_blk = starts // tb
    n_vis_e = jnp.where(counts > 0, (ends + tb - 1) // tb - first_blk, 0)
    v_end = jnp.cumsum(n_vis_e)
    v = jnp.arange(n_visits, dtype=jnp.int32)
    valid = v < v_end[-1]
    last_e = jnp.max(jnp.where(counts > 0, experts, 0))
    ve = jnp.where(valid, jnp.sum((v_end[None, :] <= v[:, None]).astype(jnp.int32), axis=1), last_e)
    pick = lambda a: jnp.sum(jnp.where(ve[:, None] == experts[None, :], a[None, :], 0), axis=1)
    vb = pick(first_blk) + v - pick(v_end - n_vis_e)
    lo = jnp.maximum(pick(starts), vb * tb) - vb * tb
    hi = jnp.minimum(pick(ends), (vb + 1) * tb) - vb * tb
    vb = jnp.where(valid, vb, n_blocks - 1)
    lo = jnp.where(valid, lo, 0)
    hi = jnp.where(valid, hi, 0)
    return dest, src_tok, (vb.astype(jnp.int32), ve.astype(jnp.int32), lo.astype(jnp.int32), hi.astype(jnp.int32))


def _q_perm():
    cols = []
    for g in range(B_GROUP):
        for kv in range(B_KV_HEADS):
            h = kv * B_GROUP + g
            cols.extend(range(h * HEAD_DIM, (h + 1) * HEAD_DIM))
    return jnp.array(cols, dtype=jnp.int32)


def _layer(x_prompt, x_sample, state_rwkv, state_shift, cache_win_k, cache_win_v, prm, norm_final_g, *,
           chunk, n_sub, moe_block):
    bp, tp, d = x_prompt.shape
    bs, ts, _ = x_sample.shape
    n_p, n_s = bp * tp, bs * ts
    ts_pad = SUBLANE
    tm = n_s
    assert tp % tm == 0 and tm % 8 == 0 and cache_win_k.shape[1] == WINDOW
    xp = x_prompt.reshape(n_p, d)
    xs = x_sample.reshape(n_s, d)

    qp = _q_perm()
    w_in = prm["w_in"]
    w_in = jnp.concatenate([w_in[:, :SHIFT_W], w_in[:, SHIFT_W:SHIFT_W + B_WIDTH][:, qp],
                            w_in[:, SHIFT_W + B_WIDTH:]], axis=1).astype(BF16)
    gn = prm["attn_norm_g"][qp][None]
    w_out = prm["w_out"]
    wa, wb = w_out[:A_WIDTH].astype(BF16), w_out[A_WIDTH:][qp].astype(BF16)
    pad_rows_r = jnp.zeros((LOGIT_ROWS - N_GROUPS - N_EXPERTS, d), F32)
    wr = jnp.concatenate([prm["w_route_group"].T, prm["w_route_expert"].T, pad_rows_r], axis=0).astype(BF16)
    br = jnp.concatenate([prm["b_route_group"], prm["b_route_expert"], pad_rows_r[:, 0]])[:, None]

    zero_blk = jnp.zeros((LORA_W // 2, A_WIDTH), F32)
    wlora = jnp.concatenate([jnp.concatenate([prm["w_decay_up"], zero_blk], axis=1),
                             jnp.concatenate([zero_blk, prm["w_iclr_up"]], axis=1)], axis=0).astype(BF16)
    rp = dict(mu=prm["mu_shift"][None], w0=prm["w_decay0"][None], a0=prm["w_iclr0"][None], wlora=wlora,
              wgate=prm["w_gate_up"].astype(BF16), kk=prm["k_k"][None], ka=prm["k_a"][None],
              rk=prm["r_k"].reshape(1, A_WIDTH), lng=prm["lnx_g"][None], lnb=prm["lnx_b"][None])

    groups = [(0, bp)]
    tiles_per_seq = tp // tm
    nb = tp // WINDOW
    n_qb = _row_tile(nb, SWA_BLOCKS_PER_STEP)
    kv4 = lambda a: a.reshape(a.shape[0], a.shape[1], B_KV_HEADS, HEAD_DIM)
    finals, s_p, sh_p, k_p, v_p = [], [], [], [], []
    for gi, (s0, s1) in enumerate(groups):
        has_sample = gi == len(groups) - 1
        n_seq = s1 - s0
        n_pt = n_seq * tiles_per_seq
        n_pg = n_pt * tm
        seg_sizes = [n_pt] + ([1] if has_sample else [])
        pa, q, k, v = _inproj_call([xp] + ([xs] if has_sample else []), [s0 * tiles_per_seq, 0], seg_sizes,
                                   prm["norm_mix_g"][None], w_in, tm=tm)

        oa_p, s_pg = _rwkv_call(pa, jnp.zeros((n_seq, 1, SHIFT_W), F32),
                                jnp.zeros((n_seq,) + state_rwkv.shape[1:], F32), rp, n_seq=n_seq,
                                n_chunks=tp // chunk, chunk=chunk, n_sub=n_sub, n_bat=1, t_real=chunk)
        ob_p, = _swa_call(prm["attn_sink"], q, k.reshape(-1, WINDOW, B_KV_WIDTH), v.reshape(-1, WINDOW, B_KV_WIDTH),
                          k, v, gn, n_seq=n_seq, n_blk=nb // n_qb, tq=WINDOW, n_bat=1, n_qb=n_qb,
                          first_has_prev=False, out_dtype=BF16,
                          prev_map=lambda b, i: (b * nb + jnp.maximum(i * n_qb - 1, 0), 0, 0))
        oa_list, ob_list = [oa_p], [ob_p]
        if has_sample:
            pad_rows = lambda a: jnp.pad(a[n_pg:].reshape(bs, ts, -1), ((0, 0), (0, ts_pad - ts), (0, 0))).reshape(
                bs * ts_pad, -1)
            oa_s, s_s = _rwkv_call(pad_rows(pa), state_shift[:, None, :], state_rwkv, rp, n_seq=bs,
                                   n_chunks=1, chunk=ts_pad, n_sub=1, n_bat=_row_tile(bs, SAMPLE_SEQS_PER_STEP), t_real=ts)
            ob_s, k_s, v_s = _swa_call(prm["attn_sink"], pad_rows(q).astype(F32),
                                       cache_win_k.reshape(bs, WINDOW, B_KV_WIDTH),
                                       cache_win_v.reshape(bs, WINDOW, B_KV_WIDTH), pad_rows(k), pad_rows(v), gn,
                                       n_seq=bs, n_blk=1, tq=ts_pad, n_bat=_row_tile(bs, SAMPLE_SEQS_PER_STEP), first_has_prev=True,
                                       prev_map=lambda b, i: (b, 0, 0), out_dtype=F32, cache_rows=ts)
            oa_list.append(oa_s.reshape(bs, ts_pad, A_WIDTH)[:, :ts].reshape(n_s, A_WIDTH))
            ob_list.append(ob_s.reshape(bs, ts_pad, B_WIDTH)[:, :ts].reshape(n_s, B_WIDTH))
            sh_s = pa[n_pg:].reshape(bs, ts, SHIFT_W)[:, -1]

        x1, h2, route = _outproj_call([xp] + ([xs] if has_sample else []), [s0 * tiles_per_seq, 0],
                                               oa_list, ob_list, seg_sizes, wa, wb, prm["norm_ffn_g"][None], wr, br,
                                               tm=tm)
        n = sum(seg_sizes) * tm
        route_t = route.reshape(n // tm, ROUTE_ROWS, tm)
        flat_e = jnp.concatenate([route_t[:, j, :].reshape(n) for j in range(TOP_K_INNER)]).astype(jnp.int32)
        dest, src_tok, visits = _dispatch(flat_e, n, moe_block)
        ybuf = _expert_call(visits, h2[src_tok], prm["w_exp_gate"], prm["w_exp_up"], prm["w_exp_down"], tb=moe_block)
        finals.append((_combine_call(x1, ybuf[dest[:n]], route, tm=tm), ybuf[dest[n:]], route))

        last_rows = lambda a, m: jnp.stack([a[(b + 1) * tp - m:(b + 1) * tp] for b in range(n_seq)], axis=0)
        s_p.append(s_pg)
        sh_p.append(last_rows(pa, 1)[:, 0])
        k_p.append(last_rows(k, WINDOW))
        v_p.append(last_rows(v, WINDOW))

    gf = norm_final_g[None]
    n_pts = [(s1 - s0) * tiles_per_seq for s0, s1 in groups]
    y_p = _final_call(finals, [0] * len(groups), n_pts, gf, tm=tm)
    y_s = _final_call(finals[-1:], n_pts[-1:], [1], gf, tm=tm)
    cat = lambda parts: jnp.concatenate(parts, axis=0)
    return (y_p.reshape(bp, tp, d), y_s.reshape(bs, ts, d), cat(s_p), cat(sh_p), kv4(cat(k_p)), kv4(cat(v_p)),
            s_s, sh_s, kv4(k_s), kv4(v_s))


def kernel(x_prompt, x_sample, state_rwkv, state_shift, cache_win_k, cache_win_v, norm_mix_g, w_in, mu_shift, w_decay0, w_decay_up, w_iclr0, w_iclr_up, w_gate_up, k_k, k_a, r_k, lnx_g, lnx_b, attn_sink, attn_norm_g, w_out, norm_ffn_g, w_route_group, b_route_group, w_route_expert, b_route_expert, w_exp_gate, w_exp_up, w_exp_down, norm_final_g):
    assert norm_mix_g.shape[0] == 1, "single-layer trunk"
    prm = dict(norm_mix_g=norm_mix_g[0], w_in=w_in[0], mu_shift=mu_shift[0], w_decay0=w_decay0[0],
               w_decay_up=w_decay_up[0], w_iclr0=w_iclr0[0], w_iclr_up=w_iclr_up[0], w_gate_up=w_gate_up[0],
               k_k=k_k[0], k_a=k_a[0], r_k=r_k[0], lnx_g=lnx_g[0], lnx_b=lnx_b[0], attn_sink=attn_sink[0],
               attn_norm_g=attn_norm_g[0], w_out=w_out[0], norm_ffn_g=norm_ffn_g[0],
               w_route_group=w_route_group[0], b_route_group=b_route_group[0],
               w_route_expert=w_route_expert[0], b_route_expert=b_route_expert[0],
               w_exp_gate=w_exp_gate[0], w_exp_up=w_exp_up[0], w_exp_down=w_exp_down[0])
    outs = _layer(x_prompt, x_sample, state_rwkv[0], state_shift[0], cache_win_k[0], cache_win_v[0], prm,
                  norm_final_g, chunk=RWKV_CHUNK, n_sub=RWKV_CHUNKS_PER_STEP, moe_block=MOE_BLOCK)
    y_p, y_s, s_p, sh_p, kp, vp, s_s, sh_s, ks, vs = outs
    return (y_p, y_s, s_p[None], sh_p[None], kp[None], vp[None], s_s[None], sh_s[None], ks[None], vs[None])
```
